```python
import math
import jax
import jax.numpy as jnp
from jax import lax
import numpy as np


D_MODEL = 2048
BATCH = 8
SEQ = 8192
DEPTH = 1

ATTN_HEADS = 16
ATTN_HEAD_DIM = D_MODEL // ATTN_HEADS
ATTN_WIDTH = ATTN_HEADS * ATTN_HEAD_DIM
DILATED_PATTERNS = ((128, 1), (512, 4), (2048, 16))
ATTN_BLOCK = 128

SSM_EXPAND = 2
SSM_INNER = SSM_EXPAND * D_MODEL
SSM_HEAD_DIM = 64
SSM_HEADS = SSM_INNER // SSM_HEAD_DIM
SSM_GROUPS = 8
SSM_STATE = 128
SSM_CONV = 4
SSM_CHUNK = 128
SSM_CONV_DIM = SSM_INNER + 2 * SSM_GROUPS * SSM_STATE

RMS_EPS = 1e-6
IN_SIZES = (ATTN_WIDTH, ATTN_WIDTH, ATTN_WIDTH, ATTN_WIDTH,
            SSM_INNER, SSM_CONV_DIM, SSM_HEADS, D_MODEL, D_MODEL)
N_IN = 4 * ATTN_WIDTH + SSM_INNER + SSM_CONV_DIM + SSM_HEADS + 2 * D_MODEL

kernel_name = 'hybrid_dilated_attn_ssd_block'


def rms_norm(x, w):
    xf = x.astype(jnp.float32)
    y = xf * lax.rsqrt(jnp.mean(xf * xf, axis=-1, keepdims=True) + RMS_EPS)
    return (y * w.astype(jnp.float32)).astype(x.dtype)


def alibi_slopes(n_heads):
    return jnp.asarray([2.0 ** (-8.0 * (h + 1) / n_heads) for h in range(n_heads)], jnp.float32)


def dilated_window_attention(q, k, v, window, dilation, slopes):
    b, s, h, e = q.shape
    sub_len = s // dilation
    span = window // dilation
    blk = ATTN_BLOCK
    nb = -(-sub_len // blk)
    padded = nb * blk

    def to_sub(t):
        t = t.reshape(b, sub_len, dilation, h, e).transpose(0, 2, 3, 1, 4)
        t = jnp.pad(t, ((0, 0), (0, 0), (0, 0), (0, padded - sub_len), (0, 0)))
        return t.reshape(b, dilation, h, nb, blk, e)

    def with_prev(t):
        prev = jnp.pad(t[:, :, :, :-1], ((0, 0), (0, 0), (0, 0), (1, 0), (0, 0), (0, 0)))
        return jnp.concatenate([prev, t], axis=4)

    qb = to_sub(q)
    kc = with_prev(to_sub(k))
    vc = with_prev(to_sub(v))
    scores = jnp.einsum('brhiqe,brhike->brhiqk', qb, kc).astype(jnp.float32) * (e ** -0.5)

    qi = jnp.arange(blk)[:, None]
    ki = jnp.arange(2 * blk)[None, :]
    dist = qi - ki + blk
    key_idx = jnp.arange(nb)[:, None, None] * blk - blk + ki
    valid = (dist >= 0) & (dist <= span) & (key_idx >= 0)
    alibi = -slopes[:, None, None, None] * (dist * dilation).astype(jnp.float32)
    scores = jnp.where(valid, scores + alibi, -jnp.inf)
    lse = jax.nn.logsumexp(scores, axis=-1)
    p = jnp.exp(scores - lse[..., None])
    o = jnp.einsum('brhiqk,brhike->brhiqe', p.astype(v.dtype), vc)

    o = o.reshape(b, dilation, h, padded, e)[:, :, :, :sub_len]
    o = o.transpose(0, 3, 1, 2, 4).reshape(b, s, h, e)
    lse = lse.reshape(b, dilation, h, padded)[:, :, :, :sub_len]
    lse = lse.transpose(0, 3, 1, 2).reshape(b, s, h)
    return o, lse


def causal_depthwise_conv(x, w, bias):
    c = x.shape[-1]
    y = lax.conv_general_dilated(x, w[:, None, :].astype(x.dtype), window_strides=(1,),
                                 padding=[(SSM_CONV - 1, 0)],
                                 dimension_numbers=('NWC', 'WIO', 'NWC'),
                                 feature_group_count=c)
    return y + bias


def ssd_chunked_scan(xs, dt, a, bm, cm):
    b, s, g, j, p = xs.shape
    n = bm.shape[-1]
    L = SSM_CHUNK
    nc = s // L
    xdt = xs.astype(jnp.float32) * dt[..., None]
    da = dt * a

    def chunks(t):
        return jnp.moveaxis(t.reshape(b, nc, L, *t.shape[2:]), 1, 0)

    causal = jnp.tril(jnp.ones((L, L), dtype=bool))

    def step(state, inp):
        xc, dac, bc, cc = inp
        acum = jnp.cumsum(dac, axis=1)
        acum_t = jnp.moveaxis(acum, 1, -1)
        seg = acum_t[..., :, None] - acum_t[..., None, :]
        decay = jnp.exp(jnp.where(causal, seg, -jnp.inf))
        cb = jnp.einsum('blgn,bsgn->bgls', cc, bc)
        y_diag = jnp.einsum('bgls,bgjls,bsgjp->blgjp', cb, decay, xc)
        y_off = jnp.einsum('blgn,bgjpn,blgj->blgjp', cc, state, jnp.exp(acum))
        last = acum[:, -1]
        w_s = jnp.exp(last[:, None] - acum)
        new_state = state * jnp.exp(last)[..., None, None] + \
            jnp.einsum('blgj,blgjp,blgn->bgjpn', w_s, xc, bc)
        return new_state, y_diag + y_off

    state0 = jnp.zeros((b, g, j, p, n), jnp.float32)
    _, ys = lax.scan(step, state0, (chunks(xdt), chunks(da),
                                     chunks(bm.astype(jnp.float32)), chunks(cm.astype(jnp.float32))))
    return jnp.moveaxis(ys, 0, 1).reshape(b, s, g, j, p)


def hybrid_layer(x, norm_w, w_in, conv_w, conv_b, dt_bias, a_log, d_skip, ssm_norm_w,
                 w_attn_branch, w_ssm_branch, w_out):
    b, s, _ = x.shape
    hpg = SSM_HEADS // SSM_GROUPS
    hn = rms_norm(x, norm_w)
    proj = hn @ w_in
    split_points = []
    acc = 0
    for size in IN_SIZES[:-1]:
        acc += size
        split_points.append(acc)
    q, k, v, z_a, z_s, xbc, dt_raw, g_a, g_s = jnp.split(proj, split_points, axis=-1)

    q = q.reshape(b, s, ATTN_HEADS, ATTN_HEAD_DIM)
    k = k.reshape(b, s, ATTN_HEADS, ATTN_HEAD_DIM)
    v = v.reshape(b, s, ATTN_HEADS, ATTN_HEAD_DIM)
    slopes = alibi_slopes(ATTN_HEADS)
    outs = []
    lses = []
    for window, dilation in DILATED_PATTERNS:
        o, l = dilated_window_attention(q, k, v, window, dilation, slopes)
        outs.append(o)
        lses.append(l)
    wts = jax.nn.softmax(jnp.stack(lses), axis=0)
    o_a = jnp.einsum('pbsh,pbshe->bshe', wts.astype(q.dtype), jnp.stack(outs))
    o_a = o_a.reshape(b, s, ATTN_WIDTH) * jax.nn.silu(z_a)

    xbc = jax.nn.silu(causal_depthwise_conv(xbc, conv_w, conv_b))
    xs, bm, cm = jnp.split(xbc, [SSM_INNER, SSM_INNER + SSM_GROUPS * SSM_STATE], axis=-1)
    xs = xs.reshape(b, s, SSM_GROUPS, hpg, SSM_HEAD_DIM)
    bm = bm.reshape(b, s, SSM_GROUPS, SSM_STATE)
    cm = cm.reshape(b, s, SSM_GROUPS, SSM_STATE)
    dt = jax.nn.softplus(dt_raw.astype(jnp.float32) + dt_bias.astype(jnp.float32))
    dt = dt.reshape(b, s, SSM_GROUPS, hpg)
    a = -jnp.exp(a_log.astype(jnp.float32)).reshape(SSM_GROUPS, hpg)
    y = ssd_chunked_scan(xs, dt, a, bm, cm)
    y = y + d_skip.astype(jnp.float32).reshape(SSM_GROUPS, hpg)[..., None] * xs.astype(jnp.float32)
    y = y.reshape(b, s, SSM_INNER).astype(x.dtype)
    y = rms_norm(y * jax.nn.silu(z_s), ssm_norm_w)

    merged = jax.nn.sigmoid(g_a) * (o_a @ w_attn_branch) + jax.nn.sigmoid(g_s) * (y @ w_ssm_branch)
    return x + merged @ w_out


def _fwd_setup_inputs(seed: int = 0) -> dict:
    key = jax.random.key(seed)
    ks = jax.random.split(key, 16)
    f32 = jnp.float32

    def dense(k, fan_in, fan_out):
        return jax.random.normal(k, (DEPTH, fan_in, fan_out), f32) * fan_in ** -0.5

    def gain(k, n):
        return 1.0 + 0.02 * jax.random.normal(k, (DEPTH, n), f32)

    x = jax.random.normal(ks[0], (BATCH, SEQ, D_MODEL), f32)
    norm_w = gain(ks[1], D_MODEL)
    w_in = dense(ks[2], D_MODEL, N_IN)
    conv_w = jax.random.normal(ks[3], (DEPTH, SSM_CONV, SSM_CONV_DIM), f32) * SSM_CONV ** -0.5
    conv_b = 0.01 * jax.random.normal(ks[4], (DEPTH, SSM_CONV_DIM), f32)
    u = jax.random.uniform(ks[5], (DEPTH, SSM_HEADS), f32)
    dt0 = jnp.exp(u * (math.log(0.1) - math.log(0.001)) + math.log(0.001))
    dt_bias = dt0 + jnp.log(-jnp.expm1(-dt0))
    a_log = jnp.log(jax.random.uniform(ks[6], (DEPTH, SSM_HEADS), f32, 1.0, 16.0))
    d_skip = gain(ks[7], SSM_HEADS)
    ssm_norm_w = gain(ks[8], SSM_INNER)
    w_attn_branch = dense(ks[9], ATTN_WIDTH, D_MODEL)
    w_ssm_branch = dense(ks[10], SSM_INNER, D_MODEL)
    w_out = dense(ks[11], D_MODEL, D_MODEL)
    final_norm_w = 1.0 + 0.02 * jax.random.normal(ks[12], (D_MODEL,), f32)
    return {'x': x, 'norm_w': norm_w, 'w_in': w_in, 'conv_w': conv_w, 'conv_b': conv_b,
            'dt_bias': dt_bias, 'a_log': a_log, 'd_skip': d_skip, 'ssm_norm_w': ssm_norm_w,
            'w_attn_branch': w_attn_branch, 'w_ssm_branch': w_ssm_branch, 'w_out': w_out,
            'final_norm_w': final_norm_w}


def _fwd_reference(x, norm_w, w_in, conv_w, conv_b, dt_bias, a_log, d_skip, ssm_norm_w,
              w_attn_branch, w_ssm_branch, w_out, final_norm_w):
    for layer in range(DEPTH):
        x = hybrid_layer(x, norm_w[layer], w_in[layer], conv_w[layer], conv_b[layer],
                         dt_bias[layer], a_log[layer], d_skip[layer], ssm_norm_w[layer],
                         w_attn_branch[layer], w_ssm_branch[layer], w_out[layer])
    return rms_norm(x, final_norm_w)


import jax as _jax
import jax.numpy as _jnp

TWIN_FORMAT = 'train_step'
FWD_PARAMS = ['x', 'norm_w', 'w_in', 'conv_w', 'conv_b', 'dt_bias', 'a_log', 'd_skip', 'ssm_norm_w', 'w_attn_branch', 'w_ssm_branch', 'w_out', 'final_norm_w']
TWIN_WEIGHTS = ['norm_w', 'w_in', 'conv_w', 'conv_b', 'dt_bias', 'a_log', 'd_skip', 'ssm_norm_w', 'w_attn_branch', 'w_ssm_branch', 'w_out', 'final_norm_w']
TWIN_DIFF_INPUT = 'x'
TWIN_INPUTS = ['x', 'norm_w', 'w_in', 'conv_w', 'conv_b', 'dt_bias', 'a_log', 'd_skip', 'ssm_norm_w', 'w_attn_branch', 'w_ssm_branch', 'w_out', 'final_norm_w', 'loss_target', 'm_norm_w', 'm_w_in', 'm_conv_w', 'm_conv_b', 'm_dt_bias', 'm_a_log', 'm_d_skip', 'm_ssm_norm_w', 'm_w_attn_branch', 'm_w_ssm_branch', 'm_w_out', 'm_final_norm_w', 'v_norm_w', 'v_w_in', 'v_conv_w', 'v_conv_b', 'v_dt_bias', 'v_a_log', 'v_d_skip', 'v_ssm_norm_w', 'v_w_attn_branch', 'v_w_ssm_branch', 'v_w_out', 'v_final_norm_w']
TWIN_OUTPUTS = ['loss', 'grad_x', 'grad_norm_w', 'grad_w_in', 'grad_conv_w', 'grad_conv_b', 'grad_dt_bias', 'grad_a_log', 'grad_d_skip', 'grad_ssm_norm_w', 'grad_w_attn_branch', 'grad_w_ssm_branch', 'grad_w_out', 'grad_final_norm_w', 'delta_norm_w', 'delta_w_in', 'delta_conv_w', 'delta_conv_b', 'delta_dt_bias', 'delta_a_log', 'delta_d_skip', 'delta_ssm_norm_w', 'delta_w_attn_branch', 'delta_w_ssm_branch', 'delta_w_out', 'delta_final_norm_w', 'new_m_norm_w', 'new_m_w_in', 'new_m_conv_w', 'new_m_conv_b', 'new_m_dt_bias', 'new_m_a_log', 'new_m_d_skip', 'new_m_ssm_norm_w', 'new_m_w_attn_branch', 'new_m_w_ssm_branch', 'new_m_w_out', 'new_m_final_norm_w', 'new_v_norm_w', 'new_v_w_in', 'new_v_conv_w', 'new_v_conv_b', 'new_v_dt_bias', 'new_v_a_log', 'new_v_d_skip', 'new_v_ssm_norm_w', 'new_v_w_attn_branch', 'new_v_w_ssm_branch', 'new_v_w_out', 'new_v_final_norm_w']
TWIN_LEAF_KINDS = {'loss': 'loss', 'grad_x': 'grad_x', 'grad_norm_w': 'grad_w', 'grad_w_in': 'grad_w', 'grad_conv_w': 'grad_w', 'grad_conv_b': 'grad_w', 'grad_dt_bias': 'grad_w', 'grad_a_log': 'grad_w', 'grad_d_skip': 'grad_w', 'grad_ssm_norm_w': 'grad_w', 'grad_w_attn_branch': 'grad_w', 'grad_w_ssm_branch': 'grad_w', 'grad_w_out': 'grad_w', 'grad_final_norm_w': 'grad_w', 'delta_norm_w': 'delta_w', 'delta_w_in': 'delta_w', 'delta_conv_w': 'delta_w', 'delta_conv_b': 'delta_w', 'delta_dt_bias': 'delta_w', 'delta_a_log': 'delta_w', 'delta_d_skip': 'delta_w', 'delta_ssm_norm_w': 'delta_w', 'delta_w_attn_branch': 'delta_w', 'delta_w_ssm_branch': 'delta_w', 'delta_w_out': 'delta_w', 'delta_final_norm_w': 'delta_w', 'new_m_norm_w': 'new_m', 'new_m_w_in': 'new_m', 'new_m_conv_w': 'new_m', 'new_m_conv_b': 'new_m', 'new_m_dt_bias': 'new_m', 'new_m_a_log': 'new_m', 'new_m_d_skip': 'new_m', 'new_m_ssm_norm_w': 'new_m', 'new_m_w_attn_branch': 'new_m', 'new_m_w_ssm_branch': 'new_m', 'new_m_w_out': 'new_m', 'new_m_final_norm_w': 'new_m', 'new_v_norm_w': 'new_v', 'new_v_w_in': 'new_v', 'new_v_conv_w': 'new_v', 'new_v_conv_b': 'new_v', 'new_v_dt_bias': 'new_v', 'new_v_a_log': 'new_v', 'new_v_d_skip': 'new_v', 'new_v_ssm_norm_w': 'new_v', 'new_v_w_attn_branch': 'new_v', 'new_v_w_ssm_branch': 'new_v', 'new_v_w_out': 'new_v', 'new_v_final_norm_w': 'new_v'}


def _forward(args):
    return _fwd_reference(*[args[k] for k in FWD_PARAMS])


def _output_shape():
    def fwd():
        inp = _fwd_setup_inputs(0)
        return _fwd_reference(*[inp[k] for k in FWD_PARAMS])
    out = _jax.eval_shape(fwd)
    return out.shape, out.dtype

N_MICROBATCH = 1
ADAM_LR = 0.001
ADAM_B1 = 0.9
ADAM_B2 = 0.999
ADAM_EPS = 1e-08
ADAM_WD = 0.01
ADAM_STEP = 10
PER_EXAMPLE_BATCH_AXIS = {'x': 0, 'loss_target': 0}
SHARED_INPUTS = []
_WEIGHT_DTYPES = {'norm_w': _jnp.float32, 'w_in': _jnp.float32, 'conv_w': _jnp.float32, 'conv_b': _jnp.float32, 'dt_bias': _jnp.float32, 'a_log': _jnp.float32, 'd_skip': _jnp.float32, 'ssm_norm_w': _jnp.float32, 'w_attn_branch': _jnp.float32, 'w_ssm_branch': _jnp.float32, 'w_out': _jnp.float32, 'final_norm_w': _jnp.float32}
MOMENT_SCALE = {'norm_w': 9.861103e-02, 'w_in': 2.941098e-02, 'conv_w': 3.766198e-02, 'conv_b': 5.025355e-02, 'dt_bias': 1.188416e-01, 'a_log': 1.524286e-01, 'd_skip': 2.210832e-01, 'ssm_norm_w': 4.274261e-02, 'w_attn_branch': 1.446352e-02, 'w_ssm_branch': 5.970021e-02, 'w_out': 6.119059e-02, 'final_norm_w': 3.200352e+01}


def _to_microbatches(a, axis):
    t = _jnp.moveaxis(a, axis, 0)
    t = t.reshape((N_MICROBATCH, t.shape[0] // N_MICROBATCH) + t.shape[1:])
    return _jnp.moveaxis(t, 1, axis + 1)


def setup_inputs(seed: int = 0) -> dict:
    inp = _fwd_setup_inputs(seed)
    key = _jax.random.fold_in(_jax.random.key(seed), 7919)
    shape, _ = _output_shape()
    out = dict(inp)
    out["loss_target"] = _jax.random.normal(_jax.random.fold_in(key, 0), shape, _jnp.float32)
    for i, name in enumerate(TWIN_WEIGHTS):
        w = inp[name].astype(_jnp.float32)
        if MOMENT_SCALE is None:
            s = _jnp.sqrt(_jnp.mean(_jnp.square(w)) + 1e-30)
        else:
            s = MOMENT_SCALE[name]
        km, kv = _jax.random.split(_jax.random.fold_in(key, i + 1))
        out[name] = w
        out["m_" + name] = s * _jax.random.normal(km, w.shape, _jnp.float32)
        out["v_" + name] = (s * s) * _jax.random.uniform(kv, w.shape, _jnp.float32, 0.5, 1.5)
    if N_MICROBATCH > 1:
        for name, axis in PER_EXAMPLE_BATCH_AXIS.items():
            out[name] = _to_microbatches(out[name], axis)
    return {'x': out['x'], 'norm_w': out['norm_w'], 'w_in': out['w_in'], 'conv_w': out['conv_w'], 'conv_b': out['conv_b'], 'dt_bias': out['dt_bias'], 'a_log': out['a_log'], 'd_skip': out['d_skip'], 'ssm_norm_w': out['ssm_norm_w'], 'w_attn_branch': out['w_attn_branch'], 'w_ssm_branch': out['w_ssm_branch'], 'w_out': out['w_out'], 'final_norm_w': out['final_norm_w'], 'loss_target': out['loss_target'], 'm_norm_w': out['m_norm_w'], 'm_w_in': out['m_w_in'], 'm_conv_w': out['m_conv_w'], 'm_conv_b': out['m_conv_b'], 'm_dt_bias': out['m_dt_bias'], 'm_a_log': out['m_a_log'], 'm_d_skip': out['m_d_skip'], 'm_ssm_norm_w': out['m_ssm_norm_w'], 'm_w_attn_branch': out['m_w_attn_branch'], 'm_w_ssm_branch': out['m_w_ssm_branch'], 'm_w_out': out['m_w_out'], 'm_final_norm_w': out['m_final_norm_w'], 'v_norm_w': out['v_norm_w'], 'v_w_in': out['v_w_in'], 'v_conv_w': out['v_conv_w'], 'v_conv_b': out['v_conv_b'], 'v_dt_bias': out['v_dt_bias'], 'v_a_log': out['v_a_log'], 'v_d_skip': out['v_d_skip'], 'v_ssm_norm_w': out['v_ssm_norm_w'], 'v_w_attn_branch': out['v_w_attn_branch'], 'v_w_ssm_branch': out['v_w_ssm_branch'], 'v_w_out': out['v_w_out'], 'v_final_norm_w': out['v_final_norm_w']}


def _loss(weights, diff, rest, loss_target):
    with _jax.named_scope("forward"):
        args = {**rest, TWIN_DIFF_INPUT: diff, **{k: w.astype(_WEIGHT_DTYPES[k]) for k, w in weights.items()}}
        y = _forward(args)
    with _jax.named_scope("loss_head"):
        err = _jnp.square(y.astype(_jnp.float32) - loss_target)
        return 0.5 * _jnp.sum(_jnp.mean(err, axis=-1)) if err.ndim else 0.5 * err


def _adamw(w, g, m, v):
    m = ADAM_B1 * m + (1.0 - ADAM_B1) * g
    v = ADAM_B2 * v + (1.0 - ADAM_B2) * _jnp.square(g)
    m_hat = m / (1.0 - ADAM_B1 ** ADAM_STEP)
    v_hat = v / (1.0 - ADAM_B2 ** ADAM_STEP)
    delta = -ADAM_LR * (m_hat / (_jnp.sqrt(v_hat) + ADAM_EPS) + ADAM_WD * w)
    return delta, m, v


def reference(x, norm_w, w_in, conv_w, conv_b, dt_bias, a_log, d_skip, ssm_norm_w, w_attn_branch, w_ssm_branch, w_out, final_norm_w, loss_target, m_norm_w, m_w_in, m_conv_w, m_conv_b, m_dt_bias, m_a_log, m_d_skip, m_ssm_norm_w, m_w_attn_branch, m_w_ssm_branch, m_w_out, m_final_norm_w, v_norm_w, v_w_in, v_conv_w, v_conv_b, v_dt_bias, v_a_log, v_d_skip, v_ssm_norm_w, v_w_attn_branch, v_w_ssm_branch, v_w_out, v_final_norm_w):
    given = dict(x=x, norm_w=norm_w, w_in=w_in, conv_w=conv_w, conv_b=conv_b, dt_bias=dt_bias, a_log=a_log, d_skip=d_skip, ssm_norm_w=ssm_norm_w, w_attn_branch=w_attn_branch, w_ssm_branch=w_ssm_branch, w_out=w_out, final_norm_w=final_norm_w, loss_target=loss_target, m_norm_w=m_norm_w, m_w_in=m_w_in, m_conv_w=m_conv_w, m_conv_b=m_conv_b, m_dt_bias=m_dt_bias, m_a_log=m_a_log, m_d_skip=m_d_skip, m_ssm_norm_w=m_ssm_norm_w, m_w_attn_branch=m_w_attn_branch, m_w_ssm_branch=m_w_ssm_branch, m_w_out=m_w_out, m_final_norm_w=m_final_norm_w, v_norm_w=v_norm_w, v_w_in=v_w_in, v_conv_w=v_conv_w, v_conv_b=v_conv_b, v_dt_bias=v_dt_bias, v_a_log=v_a_log, v_d_skip=v_d_skip, v_ssm_norm_w=v_ssm_norm_w, v_w_attn_branch=v_w_attn_branch, v_w_ssm_branch=v_w_ssm_branch, v_w_out=v_w_out, v_final_norm_w=v_final_norm_w)
    weights = {n: given[n] for n in TWIN_WEIGHTS}
    shared = {n: given[n] for n in SHARED_INPUTS}
    per_example = {n: given[n] for n in ['x']}
    grad_fn = _jax.value_and_grad(_loss, argnums=(0, 1))

    def one_microbatch(ex, loss_target):
        ex = dict(ex)
        diff = ex.pop(TWIN_DIFF_INPUT)
        return grad_fn(weights, diff, {**shared, **ex}, loss_target)

    if N_MICROBATCH == 1:
        loss, (grad_w, grad_x) = one_microbatch(per_example, given["loss_target"])
    else:
        def body(carry, xs):
            loss_sum, grad_sum = carry
            l_k, (gw_k, gx_k) = one_microbatch(xs[0], xs[1])
            with _jax.named_scope("update"):
                return (loss_sum + l_k, _jax.tree.map(_jnp.add, grad_sum, gw_k)), gx_k

        init = (_jnp.zeros((), _jnp.float32), _jax.tree.map(_jnp.zeros_like, weights))
        (loss, grad_w), grad_x = _jax.lax.scan(body, init, (per_example, given["loss_target"]))
    with _jax.named_scope("update"):
        delta_w, new_m, new_v = {}, {}, {}
        for n in TWIN_WEIGHTS:
            delta_w[n], new_m[n], new_v[n] = _adamw(weights[n], grad_w[n], given["m_" + n], given["v_" + n])
    return (loss, grad_x, *[grad_w[n] for n in TWIN_WEIGHTS], *[delta_w[n] for n in TWIN_WEIGHTS],
            *[new_m[n] for n in TWIN_WEIGHTS], *[new_v[n] for n in TWIN_WEIGHTS])
```

```python
import functools
import math
from typing import NamedTuple

import jax
import jax.numpy as jnp
from jax import lax
from jax.experimental import pallas as pl
from jax.experimental.pallas import tpu as pltpu

F32 = jnp.float32
BF16 = jnp.bfloat16
RMS_EPS = 1e-6
NEG = -1e30
N_DEV = 8
LANES = 128
ATTN_BLOCK = 128
ADAM_LR, ADAM_B1, ADAM_B2, ADAM_EPS, ADAM_WD, ADAM_STEP = 0.001, 0.9, 0.999, 1e-08, 0.01, 10
VMEM_LIMIT = 56 * 1024 * 1024


class Cfg(NamedTuple):
    D: int = 2048
    S: int = 8192
    AH: int = 16
    E: int = 128
    HB: int = 4
    patterns: tuple = ((128, 1), (512, 4), (2048, 16))
    SI: int = 4096
    P: int = 64
    SG: int = 8
    SN: int = 128
    KC: int = 4
    L: int = 128

    @property
    def AW(self): return self.AH * self.E
    @property
    def SH(self): return self.SI // self.P
    @property
    def HPG(self): return self.SH // self.SG
    @property
    def GN(self): return self.SG * self.SN
    @property
    def CD(self): return self.SI + 2 * self.GN
    @property
    def k0(self): return self.AW
    @property
    def v0(self): return 2 * self.AW
    @property
    def za0(self): return 3 * self.AW
    @property
    def zs0(self): return 4 * self.AW
    @property
    def xbc0(self): return 4 * self.AW + self.SI
    @property
    def ga0(self): return self.xbc0 + self.CD
    @property
    def gs0(self): return self.ga0 + self.D
    @property
    def NP(self): return self.gs0 + self.D
    @property
    def N_IN(self): return self.NP + self.SH


CFG = Cfg()


def _cp(sem=None, vmem=VMEM_LIMIT):
    return pltpu.CompilerParams(dimension_semantics=sem, vmem_limit_bytes=vmem)


def _sigmoid(z):
    return 1.0 / (1.0 + jnp.exp(-z))


def _dot(a, b, dims):
    return lax.dot_general(a, b, (dims, ((), ())), preferred_element_type=F32)


NN = ((1,), (0,))
NT = ((1,), (1,))
TN = ((0,), (0,))


def _blk(off, width):
    assert off % width == 0, (off, width)
    return off // width


def matmul(a, b, mode, tm, tn, tk, out_dtype, name):
    if mode == 'nn':
        (M, K), (_, N) = a.shape, b.shape
    elif mode == 'nt':
        (M, K), (N, _) = a.shape, b.shape
    else:
        (K, M), (_, N) = a.shape, b.shape
    tm, tn, tk = min(tm, M), min(tn, N), min(tk, K)
    assert M % tm == 0 and N % tn == 0 and K % tk == 0, (M, N, K, tm, tn, tk)
    nk = K // tk
    dims = {'nn': NN, 'nt': NT, 'tn': TN}[mode]

    def body(a_ref, b_ref, o_ref, *acc):
        part = _dot(a_ref[...].astype(BF16), b_ref[...].astype(BF16), dims)
        if nk == 1:
            o_ref[...] = part.astype(out_dtype)
        else:
            acc_ref, = acc
            k = pl.program_id(2)

            @pl.when(k == 0)
            def _():
                acc_ref[...] = part

            @pl.when(k > 0)
            def _():
                acc_ref[...] += part

            @pl.when(k == nk - 1)
            def _():
                o_ref[...] = acc_ref[...].astype(out_dtype)

    if mode == 'tn':
        a_spec = pl.BlockSpec((tk, tm), lambda n, m, k: (k, m))
    else:
        a_spec = pl.BlockSpec((tm, tk), lambda n, m, k: (m, k))
    if mode == 'nt':
        b_spec = pl.BlockSpec((tn, tk), lambda n, m, k: (n, k))
    else:
        b_spec = pl.BlockSpec((tk, tn), lambda n, m, k: (k, n))
    return pl.pallas_call(
        body, name=name, grid=(N // tn, M // tm, nk),
        in_specs=[a_spec, b_spec],
        out_specs=pl.BlockSpec((tm, tn), lambda n, m, k: (m, n)),
        out_shape=jax.ShapeDtypeStruct((M, N), out_dtype),
        scratch_shapes=[] if nk == 1 else [pltpu.VMEM((tm, tn), F32)],
        compiler_params=_cp(("parallel", "parallel", "arbitrary")),
    )(a, b)


def rmsnorm_fwd(x, w, name, tm=256):
    S, D = x.shape

    def body(x_ref, w_ref, o_ref):
        xv = x_ref[...]
        r = lax.rsqrt(jnp.mean(xv * xv, axis=-1, keepdims=True) + RMS_EPS)
        o_ref[...] = ((xv * r) * w_ref[...]).astype(BF16)

    return pl.pallas_call(
        body, name=name, grid=(S // tm,),
        in_specs=[pl.BlockSpec((tm, D), lambda i: (i, 0)), pl.BlockSpec((1, D), lambda i: (0, 0))],
        out_specs=pl.BlockSpec((tm, D), lambda i: (i, 0)),
        out_shape=jax.ShapeDtypeStruct((S, D), BF16),
        compiler_params=_cp(("parallel",)),
    )(x, w)


def rmsnorm_bwd(dh_a, dh_b, x, w, dout, name, tm=128):
    S, D = x.shape

    def body(da_ref, db_ref, x_ref, w_ref, do_ref, gx_ref, gw_ref):
        xv = x_ref[...]
        dh = da_ref[...] + db_ref[...]
        r = lax.rsqrt(jnp.mean(xv * xv, axis=-1, keepdims=True) + RMS_EPS)
        g = dh * w_ref[...]
        dx = r * g - xv * (r * r * r) * jnp.mean(g * xv, axis=-1, keepdims=True)
        gx_ref[...] = do_ref[...] + dx
        gw = jnp.sum(dh * (xv * r), axis=0, keepdims=True)

        @pl.when(pl.program_id(0) == 0)
        def _():
            gw_ref[...] = gw

        @pl.when(pl.program_id(0) > 0)
        def _():
            gw_ref[...] += gw

    row = pl.BlockSpec((tm, D), lambda i: (i, 0))
    vec = pl.BlockSpec((1, D), lambda i: (0, 0))
    return pl.pallas_call(
        body, name=name, grid=(S // tm,),
        in_specs=[row, row, row, vec, row],
        out_specs=[row, vec],
        out_shape=[jax.ShapeDtypeStruct((S, D), F32), jax.ShapeDtypeStruct((1, D), F32)],
        compiler_params=_cp(("arbitrary",)),
    )(dh_a, dh_b, x, w, dout)


def final_fwd_bwd(x, res, fw, tgt, name, tm=128):
    S, D = x.shape

    def body(x_ref, r_ref, w_ref, t_ref, do_ref, loss_ref, gw_ref):
        out = x_ref[...] + r_ref[...]
        w = w_ref[...]
        r = lax.rsqrt(jnp.mean(out * out, axis=-1, keepdims=True) + RMS_EPS)
        yn = out * r
        err = yn * w - t_ref[...]
        lrow = 0.5 * jnp.mean(err * err, axis=-1, keepdims=True)
        lsum = jnp.zeros((1, LANES), F32) + jnp.sum(lrow, axis=0, keepdims=True)
        dfin = err * (1.0 / D)
        g = dfin * w
        do_ref[...] = r * g - out * (r * r * r) * jnp.mean(g * out, axis=-1, keepdims=True)
        gw = jnp.sum(dfin * yn, axis=0, keepdims=True)

        @pl.when(pl.program_id(0) == 0)
        def _():
            gw_ref[...] = gw
            loss_ref[...] = lsum

        @pl.when(pl.program_id(0) > 0)
        def _():
            gw_ref[...] += gw
            loss_ref[...] += lsum

    row = pl.BlockSpec((tm, D), lambda i: (i, 0))
    vec = pl.BlockSpec((1, D), lambda i: (0, 0))
    return pl.pallas_call(
        body, name=name, grid=(S // tm,),
        in_specs=[row, row, vec, row],
        out_specs=[row, pl.BlockSpec((1, LANES), lambda i: (0, 0)), vec],
        out_shape=[jax.ShapeDtypeStruct((S, D), F32), jax.ShapeDtypeStruct((1, LANES), F32),
                   jax.ShapeDtypeStruct((1, D), F32)],
        compiler_params=_cp(("arbitrary",)),
    )(x, res, fw, tgt)


def merge_fwd(a_out, s_out, proj, cfg, name, tm=256):
    S, D = a_out.shape

    def body(a_ref, s_ref, ga_ref, gs_ref, o_ref):
        o_ref[...] = (_sigmoid(ga_ref[...]) * a_ref[...] + _sigmoid(gs_ref[...]) * s_ref[...]).astype(BF16)

    row = pl.BlockSpec((tm, D), lambda i: (i, 0))
    return pl.pallas_call(
        body, name=name, grid=(S // tm,),
        in_specs=[row, row, pl.BlockSpec((tm, D), lambda i: (i, _blk(cfg.ga0, D))),
                  pl.BlockSpec((tm, D), lambda i: (i, _blk(cfg.gs0, D)))],
        out_specs=row, out_shape=jax.ShapeDtypeStruct((S, D), BF16),
        compiler_params=_cp(("parallel",)),
    )(a_out, s_out, proj, proj)


def merge_bwd(dm, a_out, s_out, proj, cfg, name, tm=128):
    S, D = dm.shape

    def body(dm_ref, a_ref, s_ref, ga_ref, gs_ref, da_ref, ds_ref, dga_ref, dgs_ref):
        dmv = dm_ref[...]
        sa = _sigmoid(ga_ref[...])
        ss = _sigmoid(gs_ref[...])
        da_ref[...] = (dmv * sa).astype(BF16)
        ds_ref[...] = (dmv * ss).astype(BF16)
        dga_ref[...] = (dmv * a_ref[...] * (sa * (1.0 - sa))).astype(BF16)
        dgs_ref[...] = (dmv * s_ref[...] * (ss * (1.0 - ss))).astype(BF16)

    row = pl.BlockSpec((tm, D), lambda i: (i, 0))
    sh = jax.ShapeDtypeStruct((S, D), BF16)
    return pl.pallas_call(
        body, name=name, grid=(S // tm,),
        in_specs=[row, row, row, pl.BlockSpec((tm, D), lambda i: (i, _blk(cfg.ga0, D))),
                  pl.BlockSpec((tm, D), lambda i: (i, _blk(cfg.gs0, D)))],
        out_specs=[row] * 4, out_shape=[sh] * 4,
        compiler_params=_cp(("parallel",)),
    )(dm, a_out, s_out, proj, proj)


def add3_cast(a, b, c, name, tm=256):
    S, W = a.shape

    def body(a_ref, b_ref, c_ref, o_ref):
        o_ref[...] = (a_ref[...] + b_ref[...] + c_ref[...]).astype(BF16)

    row = pl.BlockSpec((tm, W), lambda i: (i, 0))
    return pl.pallas_call(
        body, name=name, grid=(S // tm,), in_specs=[row] * 3, out_specs=row,
        out_shape=jax.ShapeDtypeStruct((S, W), BF16), compiler_params=_cp(("parallel",)),
    )(a, b, c)


def _alibi_table(cfg):
    slopes = jnp.asarray([2.0 ** (-8.0 * (h + 1) / cfg.AH) for h in range(cfg.AH)], F32)
    t = jnp.broadcast_to(slopes.reshape(cfg.AH // cfg.HB, cfg.HB, 1), (cfg.AH // cfg.HB, cfg.HB, LANES))
    return jnp.pad(t, ((0, 0), (0, 8 - cfg.HB), (0, 0)))


def _to_streams(nat, cfg, d):
    S = nat.shape[0]
    nb = S // d // ATTN_BLOCK
    nhb = cfg.AH // cfg.HB
    t = nat.reshape(nb, ATTN_BLOCK, d, nhb, cfg.HB).transpose(2, 3, 0, 1, 4)
    return t.reshape(d * nhb, nb, ATTN_BLOCK, cfg.HB)


def _from_streams(st, cfg, d):
    nhb = cfg.AH // cfg.HB
    nb = st.shape[1]
    t = st.reshape(d, nhb, nb, ATTN_BLOCK, cfg.HB).transpose(2, 3, 0, 1, 4)
    return t.reshape(nb * ATTN_BLOCK * d, cfg.AH)


def _attn_masks(i, d):
    qi = lax.broadcasted_iota(jnp.int32, (ATTN_BLOCK, ATTN_BLOCK), 0)
    ki = lax.broadcasted_iota(jnp.int32, (ATTN_BLOCK, ATTN_BLOCK), 1)
    valid_p = jnp.logical_and(ki >= qi, i > 0)
    valid_c = ki <= qi
    dist_p = ((ATTN_BLOCK + qi - ki) * d).astype(F32)
    dist_c = ((qi - ki) * d).astype(F32)
    return valid_p, valid_c, dist_p, dist_c


def attn_fwd(proj, slopes, cfg, window, d, name):
    assert window // d == ATTN_BLOCK
    S, NP, HB, E = cfg.S, cfg.NP, cfg.HB, cfg.E
    Sd = S // d
    assert Sd % ATTN_BLOCK == 0
    nb = Sd // ATTN_BLOCK
    W = HB * E
    nhb = cfg.AH // HB
    ns = d * nhb
    cpr = _blk(NP, W)
    pv = proj.reshape(Sd, d * NP)
    scale = E ** -0.5

    def col(off):
        return lambda s, i: (s // nhb) * cpr + _blk(off, W) + s % nhb

    def spec(off, prev):
        c = col(off)
        if prev:
            return pl.BlockSpec((ATTN_BLOCK, W), lambda s, i: (jnp.maximum(i - 1, 0), c(s, i)))
        return pl.BlockSpec((ATTN_BLOCK, W), lambda s, i: (i, c(s, i)))

    def body(q_ref, kp_ref, kc_ref, vp_ref, vc_ref, sl_ref, o_ref, lse_ref):
        i = pl.program_id(1)
        valid_p, valid_c, dist_p, dist_c = _attn_masks(i, d)
        for g in range(HB):
            sl = slice(g * E, (g + 1) * E)
            q = q_ref[:, sl].astype(BF16)
            slope = sl_ref[0, g:g + 1, :]
            sp = _dot(q, kp_ref[:, sl].astype(BF16), NT) * scale + (-slope) * dist_p
            sc = _dot(q, kc_ref[:, sl].astype(BF16), NT) * scale + (-slope) * dist_c
            sp = jnp.where(valid_p, sp, NEG)
            sc = jnp.where(valid_c, sc, NEG)
            m = jnp.maximum(jnp.max(sp, axis=1, keepdims=True), jnp.max(sc, axis=1, keepdims=True))
            pp = jnp.exp(sp - m)
            pc = jnp.exp(sc - m)
            l = jnp.sum(pp, axis=1, keepdims=True) + jnp.sum(pc, axis=1, keepdims=True)
            inv = 1.0 / l
            o = _dot((pp * inv).astype(BF16), vp_ref[:, sl].astype(BF16), NN) + \
                _dot((pc * inv).astype(BF16), vc_ref[:, sl].astype(BF16), NN)
            o_ref[:, sl] = o
            lse_ref[0, 0, :, g:g + 1] = m + jnp.log(l)

    o, lse = pl.pallas_call(
        body, name=name, grid=(ns, nb),
        in_specs=[spec(0, False), spec(cfg.k0, True), spec(cfg.k0, False), spec(cfg.v0, True), spec(cfg.v0, False),
                  pl.BlockSpec((1, 8, LANES), lambda s, i: (s % nhb, 0, 0))],
        out_specs=[pl.BlockSpec((ATTN_BLOCK, W), lambda s, i: (i, s)),
                   pl.BlockSpec((1, 1, ATTN_BLOCK, HB), lambda s, i: (s, i, 0, 0))],
        out_shape=[jax.ShapeDtypeStruct((Sd, d * cfg.AW), F32),
                   jax.ShapeDtypeStruct((ns, nb, ATTN_BLOCK, HB), F32)],
        compiler_params=_cp(("parallel", "parallel")),
    )(pv, pv, pv, pv, pv, slopes)
    return o.reshape(S, cfg.AW), _from_streams(lse, cfg, d)


def attn_combine(o_list, lse, proj, cfg, name, tm=256):
    S, AW, AH, E = cfg.S, cfg.AW, cfg.AH, cfg.E
    npat = len(o_list)

    def body(*refs):
        o_refs = refs[:npat]
        lse_ref, z_ref, oa_ref, om_ref, lt_ref = refs[npat:]
        ls = [lse_ref[p] for p in range(npat)]
        m = functools.reduce(jnp.maximum, ls)
        ssum = sum(jnp.exp(l_ - m) for l_ in ls)
        lt = m + jnp.log(ssum)
        lt_ref[...] = lt
        ws = [jnp.exp(l_ - lt) for l_ in ls]
        for h in range(AH):
            sl = slice(h * E, (h + 1) * E)
            acc = sum(ws[p][:, h:h + 1] * o_refs[p][:, sl] for p in range(npat))
            om_ref[:, sl] = acc
            z = z_ref[:, sl]
            oa_ref[:, sl] = (acc * (z * _sigmoid(z))).astype(BF16)

    row = pl.BlockSpec((tm, AW), lambda i: (i, 0))
    return pl.pallas_call(
        body, name=name, grid=(S // tm,),
        in_specs=[row] * npat + [pl.BlockSpec((npat, tm, AH), lambda i: (0, i, 0)),
                                 pl.BlockSpec((tm, AW), lambda i: (i, _blk(cfg.za0, AW)))],
        out_specs=[row, row, pl.BlockSpec((tm, AH), lambda i: (i, 0))],
        out_shape=[jax.ShapeDtypeStruct((S, AW), BF16), jax.ShapeDtypeStruct((S, AW), F32),
                   jax.ShapeDtypeStruct((S, AH), F32)],
        compiler_params=_cp(("parallel",)),
    )(*o_list, lse, proj)


def attn_out_bwd(do_a, o_mix, proj, cfg, name, tm=256):
    S, AW, AH, E = cfg.S, cfg.AW, cfg.AH, cfg.E

    def body(do_ref, om_ref, z_ref, dm_ref, dz_ref, dl_ref):
        z = z_ref[...]
        s = _sigmoid(z)
        doa = do_ref[...]
        om = om_ref[...]
        dmix = doa * (z * s)
        dm_ref[...] = dmix.astype(BF16)
        dz_ref[...] = (doa * om * (s * (1.0 + z * (1.0 - s)))).astype(BF16)
        prod = dmix * om
        for h in range(AH):
            dl_ref[:, h:h + 1] = jnp.sum(prod[:, h * E:(h + 1) * E], axis=1, keepdims=True)

    row = pl.BlockSpec((tm, AW), lambda i: (i, 0))
    return pl.pallas_call(
        body, name=name, grid=(S // tm,),
        in_specs=[row, row, pl.BlockSpec((tm, AW), lambda i: (i, _blk(cfg.za0, AW)))],
        out_specs=[row, row, pl.BlockSpec((tm, AH), lambda i: (i, 0))],
        out_shape=[jax.ShapeDtypeStruct((S, AW), BF16), jax.ShapeDtypeStruct((S, AW), BF16),
                   jax.ShapeDtypeStruct((S, AH), F32)],
        compiler_params=_cp(("parallel",)),
    )(do_a, o_mix, proj)


def attn_bwd(proj, do_mix, ltot, dl, slopes, cfg, window, d, name):
    assert window // d == ATTN_BLOCK
    S, NP, HB, E = cfg.S, cfg.NP, cfg.HB, cfg.E
    Sd = S // d
    nb = Sd // ATTN_BLOCK
    W = HB * E
    nhb = cfg.AH // HB
    ns = d * nhb
    cpr = _blk(NP, W)
    pv = proj.reshape(Sd, d * NP)
    dov = do_mix.reshape(Sd, d * cfg.AW)
    lt_s = _to_streams(ltot, cfg, d)
    dl_s = _to_streams(dl, cfg, d)
    scale = E ** -0.5
    last = nb - 1

    def pspec(off, prev):
        def c(s):
            return (s // nhb) * cpr + _blk(off, W) + s % nhb
        if prev:
            return pl.BlockSpec((ATTN_BLOCK, W), lambda s, i: (jnp.maximum(i - 1, 0), c(s)))
        return pl.BlockSpec((ATTN_BLOCK, W), lambda s, i: (jnp.minimum(i, last), c(s)))

    cur = pl.BlockSpec((ATTN_BLOCK, W), lambda s, i: (jnp.minimum(i, last), s))
    prev = pl.BlockSpec((ATTN_BLOCK, W), lambda s, i: (jnp.maximum(i - 1, 0), s))
    tab = pl.BlockSpec((1, 1, ATTN_BLOCK, HB), lambda s, i: (s, jnp.minimum(i, last), 0, 0))

    def body(q_ref, kp_ref, kc_ref, vp_ref, vc_ref, do_ref, lt_ref, dl_ref, sl_ref,
             dq_ref, dk_ref, dv_ref, dkc, dvc):
        i = pl.program_id(1)

        @pl.when(i == 0)
        def _():
            dkc[...] = jnp.zeros_like(dkc)
            dvc[...] = jnp.zeros_like(dvc)

        @pl.when(i < nb)
        def _():
            valid_p, valid_c, dist_p, dist_c = _attn_masks(i, d)
            for g in range(HB):
                sl = slice(g * E, (g + 1) * E)
                q = q_ref[:, sl].astype(BF16)
                kp = kp_ref[:, sl].astype(BF16)
                kc = kc_ref[:, sl].astype(BF16)
                do = do_ref[:, sl]
                slope = sl_ref[0, g:g + 1, :]
                lt = lt_ref[0, 0, :, g:g + 1]
                dlt = dl_ref[0, 0, :, g:g + 1]
                sp = _dot(q, kp, NT) * scale + (-slope) * dist_p
                sc = _dot(q, kc, NT) * scale + (-slope) * dist_c
                pp = jnp.where(valid_p, jnp.exp(jnp.where(valid_p, sp, NEG) - lt), 0.0)
                pc = jnp.where(valid_c, jnp.exp(jnp.where(valid_c, sc, NEG) - lt), 0.0)
                dsp = (pp * (_dot(do, vp_ref[:, sl].astype(BF16), NT) - dlt) * scale).astype(BF16)
                dsc = (pc * (_dot(do, vc_ref[:, sl].astype(BF16), NT) - dlt) * scale).astype(BF16)
                dq_ref[:, sl] = _dot(dsp, kp, NN) + _dot(dsc, kc, NN)
                dk_ref[:, sl] = dkc[:, sl] + _dot(dsp, q, TN)
                dv_ref[:, sl] = dvc[:, sl] + _dot(pp.astype(BF16), do, TN)
                dkc[:, sl] = _dot(dsc, q, TN)
                dvc[:, sl] = _dot(pc.astype(BF16), do, TN)

        @pl.when(i == nb)
        def _():
            dk_ref[...] = dkc[...]
            dv_ref[...] = dvc[...]

    sh = jax.ShapeDtypeStruct((Sd, d * cfg.AW), F32)
    dq, dk, dv = pl.pallas_call(
        body, name=name, grid=(ns, nb + 1),
        in_specs=[pspec(0, False), pspec(cfg.k0, True), pspec(cfg.k0, False), pspec(cfg.v0, True),
                  pspec(cfg.v0, False), cur, tab, tab,
                  pl.BlockSpec((1, 8, LANES), lambda s, i: (s % nhb, 0, 0))],
        out_specs=[cur, prev, prev], out_shape=[sh, sh, sh],
        scratch_shapes=[pltpu.VMEM((ATTN_BLOCK, W), F32), pltpu.VMEM((ATTN_BLOCK, W), F32)],
        compiler_params=_cp(("parallel", "arbitrary")),
    )(pv, pv, pv, pv, pv, dov, lt_s, dl_s, slopes)
    return dq.reshape(S, cfg.AW), dk.reshape(S, cfg.AW), dv.reshape(S, cfg.AW)


HALO = 8


def _conv_pre(ext_ref, x_ref, h_ref, w_ref, b_ref, tm, kc):
    first = pl.program_id(1) == 0
    ext_ref[0:HALO, :] = jnp.where(first, 0.0, h_ref[...])
    ext_ref[HALO:, :] = x_ref[...]
    pre = b_ref[...] + jnp.zeros_like(x_ref[...])
    for k in range(kc):
        pre = pre + w_ref[k:k + 1, :] * ext_ref[pl.ds(HALO - (kc - 1) + k, tm), :]
    return pre


def conv_fwd(proj, w, b, cfg, name, tm=512, tc=512):
    S, CD, KC = cfg.S, cfg.CD, cfg.KC
    tc = min(tc, CD)
    c0 = _blk(cfg.xbc0, tc)
    hb = tm // HALO

    def body(x_ref, h_ref, w_ref, b_ref, o_ref, ext_ref):
        pre = _conv_pre(ext_ref, x_ref, h_ref, w_ref, b_ref, tm, KC)
        o_ref[...] = pre * _sigmoid(pre)

    return pl.pallas_call(
        body, name=name, grid=(CD // tc, S // tm),
        in_specs=[pl.BlockSpec((tm, tc), lambda c, i: (i, c0 + c)),
                  pl.BlockSpec((HALO, tc), lambda c, i: (jnp.maximum(i * hb - 1, 0), c0 + c)),
                  pl.BlockSpec((KC, tc), lambda c, i: (0, c)),
                  pl.BlockSpec((1, tc), lambda c, i: (0, c))],
        out_specs=pl.BlockSpec((tm, tc), lambda c, i: (i, c)),
        out_shape=jax.ShapeDtypeStruct((S, CD), F32),
        scratch_shapes=[pltpu.VMEM((tm + HALO, tc), F32)],
        compiler_params=_cp(("parallel", "arbitrary")),
    )(proj, proj, w, b)


def conv_bwd_a(proj, dxc, w, b, cfg, name, tm=512, tc=512):
    S, CD, KC = cfg.S, cfg.CD, cfg.KC
    tc = min(tc, CD)
    c0 = _blk(cfg.xbc0, tc)
    hb = tm // HALO

    def body(x_ref, h_ref, d_ref, w_ref, b_ref, dp_ref, gw_ref, gb_ref, ext_ref):
        pre = _conv_pre(ext_ref, x_ref, h_ref, w_ref, b_ref, tm, KC)
        s = _sigmoid(pre)
        dpre = d_ref[...] * (s * (1.0 + pre * (1.0 - s)))
        dp_ref[...] = dpre
        gb = jnp.sum(dpre, axis=0, keepdims=True)
        gws = [jnp.sum(dpre * ext_ref[pl.ds(HALO - (KC - 1) + k, tm), :], axis=0, keepdims=True) for k in range(KC)]
        gw = jnp.concatenate(gws + [jnp.zeros((8 - KC, tc), F32)], axis=0)

        @pl.when(pl.program_id(1) == 0)
        def _():
            gw_ref[...] = gw
            gb_ref[...] = gb

        @pl.when(pl.program_id(1) > 0)
        def _():
            gw_ref[...] += gw
            gb_ref[...] += gb

    return pl.pallas_call(
        body, name=name, grid=(CD // tc, S // tm),
        in_specs=[pl.BlockSpec((tm, tc), lambda c, i: (i, c0 + c)),
                  pl.BlockSpec((HALO, tc), lambda c, i: (jnp.maximum(i * hb - 1, 0), c0 + c)),
                  pl.BlockSpec((tm, tc), lambda c, i: (i, c)),
                  pl.BlockSpec((KC, tc), lambda c, i: (0, c)),
                  pl.BlockSpec((1, tc), lambda c, i: (0, c))],
        out_specs=[pl.BlockSpec((tm, tc), lambda c, i: (i, c)),
                   pl.BlockSpec((8, tc), lambda c, i: (0, c)),
                   pl.BlockSpec((1, tc), lambda c, i: (0, c))],
        out_shape=[jax.ShapeDtypeStruct((S, CD), F32), jax.ShapeDtypeStruct((8, CD), F32),
                   jax.ShapeDtypeStruct((1, CD), F32)],
        scratch_shapes=[pltpu.VMEM((tm + HALO, tc), F32)],
        compiler_params=_cp(("parallel", "arbitrary")),
    )(proj, proj, dxc, w, b)


def conv_bwd_b(dpre, w, cfg, name, tm=512, tc=512):
    S, CD, KC = cfg.S, cfg.CD, cfg.KC
    tc = min(tc, CD)
    hb = tm // HALO
    nrb = S // tm
    last_h = S // HALO - 1

    def body(d_ref, h_ref, w_ref, o_ref, ext_ref):
        is_last = pl.program_id(1) == nrb - 1
        ext_ref[0:tm, :] = d_ref[...]
        ext_ref[tm:, :] = jnp.where(is_last, 0.0, h_ref[...])
        acc = w_ref[KC - 1:KC, :] * d_ref[...]
        for j in range(1, KC):
            acc = acc + w_ref[KC - 1 - j:KC - j, :] * ext_ref[pl.ds(j, tm), :]
        o_ref[...] = acc.astype(BF16)

    return pl.pallas_call(
        body, name=name, grid=(CD // tc, nrb),
        in_specs=[pl.BlockSpec((tm, tc), lambda c, i: (i, c)),
                  pl.BlockSpec((HALO, tc), lambda c, i: (jnp.minimum((i + 1) * hb, last_h), c)),
                  pl.BlockSpec((KC, tc), lambda c, i: (0, c))],
        out_specs=pl.BlockSpec((tm, tc), lambda c, i: (i, c)),
        out_shape=jax.ShapeDtypeStruct((S, CD), BF16),
        scratch_shapes=[pltpu.VMEM((tm + HALO, tc), F32)],
        compiler_params=_cp(("parallel", "arbitrary")),
    )(dpre, dpre, w)


def _pad_lanes(v, width=LANES):
    return jnp.pad(v, ((0, 0), (0, width - v.shape[1])))


def ssd_prep(dt_raw, dt_bias, a_log, cfg, name):
    S, L = cfg.S, cfg.L

    def body(x_ref, b_ref, al_ref, dt_ref, ac_ref):
        x = x_ref[...] + b_ref[...]
        dt = jnp.maximum(x, 0.0) + jnp.log(1.0 + jnp.exp(-jnp.abs(x)))
        da = dt * (-jnp.exp(al_ref[...]))
        li = lax.broadcasted_iota(jnp.int32, (L, L), 0)
        si = lax.broadcasted_iota(jnp.int32, (L, L), 1)
        tri = jnp.where(li >= si, 1.0, 0.0).astype(F32)
        dt_ref[...] = dt
        ac_ref[...] = lax.dot_general(tri, da, ((NN), ((), ())), precision=lax.Precision.HIGHEST,
                                      preferred_element_type=F32)

    row = pl.BlockSpec((L, LANES), lambda i: (i, 0))
    vec = pl.BlockSpec((1, LANES), lambda i: (0, 0))
    sh = jax.ShapeDtypeStruct((S, LANES), F32)
    return pl.pallas_call(
        body, name=name, grid=(S // L,), in_specs=[row, vec, vec], out_specs=[row, row], out_shape=[sh, sh],
        compiler_params=_cp(("parallel",)),
    )(dt_raw, dt_bias, a_log)


def _by_group(v, cfg):
    return v[:, :cfg.SH].reshape(cfg.S, cfg.SG, cfg.HPG).transpose(1, 0, 2)


def _from_group(v, cfg):
    return _pad_lanes(v.transpose(1, 0, 2).reshape(cfg.S, cfg.SH))


def ssd_scan_fwd(xc, dtg, acg, act, cfg, name):
    S, L, P, SN, HPG, SG, SI = cfg.S, cfg.L, cfg.P, cfg.SN, cfg.HPG, cfg.SG, cfg.SI
    nc = S // L
    GW = HPG * P
    bcol, ccol = _blk(SI, SN), _blk(SI + cfg.GN, SN)

    def body(xs_ref, b_ref, c_ref, dt_ref, ac_ref, at_ref, y_ref, st_ref, st):
        @pl.when(pl.program_id(1) == 0)
        def _():
            st[...] = jnp.zeros_like(st)

        st_ref[0] = st[...]
        B = b_ref[...].astype(BF16)
        C = c_ref[...].astype(BF16)
        G = _dot(C, B, NT)
        li = lax.broadcasted_iota(jnp.int32, (L, L), 0)
        si = lax.broadcasted_iota(jnp.int32, (L, L), 1)
        causal = li >= si
        for j in range(HPG):
            sl = slice(j * P, (j + 1) * P)
            a_col = ac_ref[0, :, j:j + 1]
            a_row = at_ref[j:j + 1, :]
            dm = jnp.where(causal, jnp.exp(a_col - a_row), 0.0)
            xdt = xs_ref[:, sl] * dt_ref[0, :, j:j + 1]
            s0 = st[j]
            yd = _dot((G * dm).astype(BF16), xdt.astype(BF16), NN)
            yo = jnp.exp(a_col) * _dot(C, s0.astype(BF16), NT)
            y_ref[:, sl] = yd + yo
            a_last = a_row[:, L - 1:L]
            ws = jnp.exp(a_last - a_col)
            st[j] = s0 * jnp.exp(a_last) + _dot((xdt * ws).astype(BF16), B, TN)

    y, states = pl.pallas_call(
        body, name=name, grid=(SG, nc),
        in_specs=[pl.BlockSpec((L, GW), lambda g, c: (c, g)),
                  pl.BlockSpec((L, SN), lambda g, c: (c, bcol + g)),
                  pl.BlockSpec((L, SN), lambda g, c: (c, ccol + g)),
                  pl.BlockSpec((1, L, HPG), lambda g, c: (g, c, 0)),
                  pl.BlockSpec((1, L, HPG), lambda g, c: (g, c, 0)),
                  pl.BlockSpec((HPG, L), lambda g, c: (g, c))],
        out_specs=[pl.BlockSpec((L, GW), lambda g, c: (c, g)),
                   pl.BlockSpec((1, HPG, P, SN), lambda g, c: (c, g, 0, 0))],
        out_shape=[jax.ShapeDtypeStruct((S, SI), F32), jax.ShapeDtypeStruct((nc, cfg.SH, P, SN), F32)],
        scratch_shapes=[pltpu.VMEM((HPG, P, SN), F32)],
        compiler_params=_cp(("parallel", "arbitrary")),
    )(xc, xc, xc, dtg, acg, act)
    return y, states


def ssd_scan_bwd(xc, dtg, acg, act, states, dy, dtab, cfg, name):
    S, L, P, SN, HPG, SG, SI = cfg.S, cfg.L, cfg.P, cfg.SN, cfg.HPG, cfg.SG, cfg.SI
    nc = S // L
    GW = HPG * P
    bcol, ccol = _blk(SI, SN), _blk(SI + cfg.GN, SN)

    def rc(c):
        return nc - 1 - c

    def body(xs_ref, b_ref, c_ref, dt_ref, ac_ref, at_ref, st_ref, dy_ref, dk_ref,
             dxs_ref, db_ref, dc_ref, dxsum_ref, dda_ref, dst):
        @pl.when(pl.program_id(1) == 0)
        def _():
            dst[...] = jnp.zeros_like(dst)

        Bf = b_ref[...]
        B = Bf.astype(BF16)
        C = c_ref[...].astype(BF16)
        G = _dot(C, B, NT)
        li = lax.broadcasted_iota(jnp.int32, (L, L), 0)
        si = lax.broadcasted_iota(jnp.int32, (L, L), 1)
        causal = li >= si
        eye = li == si
        lane = lax.broadcasted_iota(jnp.int32, (1, L), 1)
        dgsum = jnp.zeros((L, L), F32)
        dc_acc = jnp.zeros((L, SN), F32)
        db_acc = jnp.zeros((L, SN), F32)
        for j in range(HPG):
            sl = slice(j * P, (j + 1) * P)
            a_col = ac_ref[0, :, j:j + 1]
            a_row = at_ref[j:j + 1, :]
            dtc = dt_ref[0, :, j:j + 1]
            xs = xs_ref[:, sl]
            xdt = xs * dtc
            xdtb = xdt.astype(BF16)
            dY = dy_ref[:, sl]
            dYb = dY.astype(BF16)
            s0 = st_ref[0, j]
            s0b = s0.astype(BF16)
            ds1 = dst[j]
            ds1b = ds1.astype(BF16)
            dm = jnp.where(causal, jnp.exp(a_col - a_row), 0.0)
            M = G * dm
            dMm = _dot(dYb, xdtb, NT) * dm
            dgsum = dgsum + dMm
            E = dMm * G
            a_last = a_row[:, L - 1:L]
            ea_last = jnp.exp(a_last)
            ws = jnp.exp(a_last - a_col)
            ea = jnp.exp(a_col)
            R = _dot(C, s0b, NT)
            dR = (ea * dY).astype(BF16)
            xds = _dot(xdtb, ds1b, NN)
            dW = jnp.sum(xds * Bf, axis=1, keepdims=True)
            dX = _dot(M.astype(BF16), dYb, TN) + ws * _dot(B, ds1b, NT)
            dc_acc = dc_acc + _dot(dR, s0b, NN)
            db_acc = db_acc + ws * xds
            tcol = jnp.sum(E, axis=1, keepdims=True) + jnp.sum(dY * (ea * R), axis=1, keepdims=True) - dW * ws
            trow = jnp.sum(jnp.where(eye, tcol, 0.0), axis=0, keepdims=True) - jnp.sum(E, axis=0, keepdims=True)
            sc = ea_last * jnp.sum(jnp.sum(ds1 * s0, axis=1, keepdims=True), axis=0, keepdims=True) + \
                jnp.sum(dW * ws, axis=0, keepdims=True)
            da_row = trow + jnp.where(lane == L - 1, sc, 0.0)
            dda_ref[0, :, j:j + 1] = jnp.sum(jnp.where(si >= li, da_row, 0.0), axis=1, keepdims=True)
            dxsum_ref[0, :, j:j + 1] = jnp.sum(dX * xs, axis=1, keepdims=True)
            dxs_ref[:, sl] = dX * dtc + dk_ref[0, j:j + 1, 0:1] * dY
            dst[j] = ea_last * ds1 + _dot(dR, C, TN)
        dgb = dgsum.astype(BF16)
        dc_ref[...] = dc_acc + _dot(dgb, B, NN)
        db_ref[...] = db_acc + _dot(dgb, C, TN)

    colspec = pl.BlockSpec((1, L, HPG), lambda g, c: (g, rc(c), 0))
    return pl.pallas_call(
        body, name=name, grid=(SG, nc),
        in_specs=[pl.BlockSpec((L, GW), lambda g, c: (rc(c), g)),
                  pl.BlockSpec((L, SN), lambda g, c: (rc(c), bcol + g)),
                  pl.BlockSpec((L, SN), lambda g, c: (rc(c), ccol + g)),
                  colspec, colspec,
                  pl.BlockSpec((HPG, L), lambda g, c: (g, rc(c))),
                  pl.BlockSpec((1, HPG, P, SN), lambda g, c: (rc(c), g, 0, 0)),
                  pl.BlockSpec((L, GW), lambda g, c: (rc(c), g)),
                  pl.BlockSpec((1, HPG, LANES), lambda g, c: (g, 0, 0))],
        out_specs=[pl.BlockSpec((L, GW), lambda g, c: (rc(c), g)),
                   pl.BlockSpec((L, SN), lambda g, c: (rc(c), g)),
                   pl.BlockSpec((L, SN), lambda g, c: (rc(c), g)),
                   colspec, colspec],
        out_shape=[jax.ShapeDtypeStruct((S, SI), F32), jax.ShapeDtypeStruct((S, cfg.GN), F32),
                   jax.ShapeDtypeStruct((S, cfg.GN), F32),
                   jax.ShapeDtypeStruct((SG, S, HPG), F32), jax.ShapeDtypeStruct((SG, S, HPG), F32)],
        scratch_shapes=[pltpu.VMEM((HPG, P, SN), F32)],
        compiler_params=_cp(("parallel", "arbitrary")),
    )(xc, xc, xc, dtg, acg, act, states, dy, dtab)


def dt_bwd(dxsum, dda, dt_raw, dt, dt_bias, a_log, cfg, name, tm=512):
    S = cfg.S
    tm = min(tm, S)

    def body(dx_ref, dd_ref, x_ref, dt_ref, b_ref, al_ref, o_ref, gb_ref, ga_ref):
        a = -jnp.exp(al_ref[...])
        ddt = dx_ref[...] + dd_ref[...] * a
        draw = ddt * _sigmoid(x_ref[...] + b_ref[...])
        o_ref[...] = draw.astype(BF16)
        gb = jnp.sum(draw, axis=0, keepdims=True)
        ga = jnp.sum(dd_ref[...] * dt_ref[...], axis=0, keepdims=True) * a

        @pl.when(pl.program_id(0) == 0)
        def _():
            gb_ref[...] = gb
            ga_ref[...] = ga

        @pl.when(pl.program_id(0) > 0)
        def _():
            gb_ref[...] += gb
            ga_ref[...] += ga

    row = pl.BlockSpec((tm, LANES), lambda i: (i, 0))
    vec = pl.BlockSpec((1, LANES), lambda i: (0, 0))
    return pl.pallas_call(
        body, name=name, grid=(S // tm,), in_specs=[row, row, row, row, vec, vec],
        out_specs=[row, vec, vec],
        out_shape=[jax.ShapeDtypeStruct((S, LANES), BF16), jax.ShapeDtypeStruct((1, LANES), F32),
                   jax.ShapeDtypeStruct((1, LANES), F32)],
        compiler_params=_cp(("arbitrary",)),
    )(dxsum, dda, dt_raw, dt, dt_bias, a_log)


def gated_norm_fwd(y, xc, proj, dvec, nw, cfg, name, tm=128):
    S, SI = cfg.S, cfg.SI

    def body(y_ref, xs_ref, z_ref, d_ref, w_ref, o_ref):
        z = z_ref[...]
        yg = (y_ref[...] + d_ref[...] * xs_ref[...]) * (z * _sigmoid(z))
        r = lax.rsqrt(jnp.mean(yg * yg, axis=-1, keepdims=True) + RMS_EPS)
        o_ref[...] = ((yg * r) * w_ref[...]).astype(BF16)

    row = pl.BlockSpec((tm, SI), lambda i: (i, 0))
    vec = pl.BlockSpec((1, SI), lambda i: (0, 0))
    return pl.pallas_call(
        body, name=name, grid=(S // tm,),
        in_specs=[row, row, pl.BlockSpec((tm, SI), lambda i: (i, _blk(cfg.zs0, SI))), vec, vec],
        out_specs=row, out_shape=jax.ShapeDtypeStruct((S, SI), BF16),
        compiler_params=_cp(("parallel",)),
    )(y, xc, proj, dvec, nw)


def gated_norm_bwd(dyn, y, xc, proj, dvec, nw, cfg, name, tm=128):
    S, SI = cfg.S, cfg.SI

    def body(dn_ref, y_ref, xs_ref, z_ref, d_ref, w_ref, dy_ref, dz_ref, gw_ref, gd_ref):
        z = z_ref[...]
        s = _sigmoid(z)
        sz = z * s
        xs = xs_ref[...]
        yf = y_ref[...] + d_ref[...] * xs
        yg = yf * sz
        r = lax.rsqrt(jnp.mean(yg * yg, axis=-1, keepdims=True) + RMS_EPS)
        dn = dn_ref[...]
        g = dn * w_ref[...]
        dyg = r * g - yg * (r * r * r) * jnp.mean(g * yg, axis=-1, keepdims=True)
        dy = dyg * sz
        dy_ref[...] = dy
        dz_ref[...] = (dyg * yf * (s * (1.0 + z * (1.0 - s)))).astype(BF16)
        gw = jnp.sum(dn * (yg * r), axis=0, keepdims=True)
        gd = jnp.sum(dy * xs, axis=0, keepdims=True)

        @pl.when(pl.program_id(0) == 0)
        def _():
            gw_ref[...] = gw
            gd_ref[...] = gd

        @pl.when(pl.program_id(0) > 0)
        def _():
            gw_ref[...] += gw
            gd_ref[...] += gd

    row = pl.BlockSpec((tm, SI), lambda i: (i, 0))
    vec = pl.BlockSpec((1, SI), lambda i: (0, 0))
    return pl.pallas_call(
        body, name=name, grid=(S // tm,),
        in_specs=[row, row, row, pl.BlockSpec((tm, SI), lambda i: (i, _blk(cfg.zs0, SI))), vec, vec],
        out_specs=[row, row, vec, vec],
        out_shape=[jax.ShapeDtypeStruct((S, SI), F32), jax.ShapeDtypeStruct((S, SI), BF16),
                   jax.ShapeDtypeStruct((1, SI), F32), jax.ShapeDtypeStruct((1, SI), F32)],
        compiler_params=_cp(("arbitrary",)),
    )(dyn, y, xc, proj, dvec, nw)


def local_step(cfg, x, tgt, norm_w, conv_w, conv_b, dt_bias, a_log, d_skip, ssm_norm_w, final_norm_w,
               w_main, w_dt, w_attn, w_ssm, w_out):
    S, D = cfg.S, cfg.D
    slopes = _alibi_table(cfg)
    dt_bias_p = _pad_lanes(dt_bias)
    a_log_p = _pad_lanes(a_log)
    dvec = jnp.repeat(d_skip, cfg.P, axis=1)
    dtab = jnp.broadcast_to(d_skip.reshape(cfg.SG, cfg.HPG, 1), (cfg.SG, cfg.HPG, LANES))

    hn = rmsnorm_fwd(x, norm_w, "rmsnorm_fwd")
    proj = matmul(hn, w_main, 'nn', 512, 1024, 2048, F32, "in_proj")
    dt_raw = matmul(hn, w_dt, 'nn', 512, 128, 2048, F32, "in_proj_dt")
    o_list, lse_list = [], []
    for window, d in cfg.patterns:
        o, lse = attn_fwd(proj, slopes, cfg, window, d, "attn_fwd_d%d" % d)
        o_list.append(o)
        lse_list.append(lse)
    o_a, o_mix, ltot = attn_combine(o_list, jnp.stack(lse_list), proj, cfg, "attn_combine")
    xc = conv_fwd(proj, conv_w, conv_b, cfg, "conv_fwd")
    dt, acum = ssd_prep(dt_raw, dt_bias_p, a_log_p, cfg, "ssd_prep")
    dtg, acg = _by_group(dt, cfg), _by_group(acum, cfg)
    act = acum[:, :cfg.SH].T
    y, states = ssd_scan_fwd(xc, dtg, acg, act, cfg, "ssd_scan_fwd")
    y_n = gated_norm_fwd(y, xc, proj, dvec, ssm_norm_w, cfg, "gated_norm_fwd")
    a_out = matmul(o_a, w_attn, 'nn', 512, 1024, 2048, F32, "attn_branch")
    s_out = matmul(y_n, w_ssm, 'nn', 512, 1024, 2048, F32, "ssm_branch")
    merged = merge_fwd(a_out, s_out, proj, cfg, "merge_fwd")
    res = matmul(merged, w_out, 'nn', 512, 1024, 2048, F32, "out_proj")
    dout, loss_p, g_final_w = final_fwd_bwd(x, res, final_norm_w.reshape(1, D), tgt, "final_fwd_bwd")

    g_w_out = matmul(merged, dout, 'tn', 1024, 1024, 512, F32, "g_w_out")
    dmerged = matmul(dout, w_out, 'nt', 512, 1024, 2048, F32, "d_merged")
    da_out, ds_out, dga, dgs = merge_bwd(dmerged, a_out, s_out, proj, cfg, "merge_bwd")
    g_w_attn = matmul(o_a, da_out, 'tn', 1024, 1024, 512, F32, "g_w_attn")
    g_w_ssm = matmul(y_n, ds_out, 'tn', 1024, 1024, 512, F32, "g_w_ssm")
    do_a = matmul(da_out, w_attn, 'nt', 512, 1024, 2048, F32, "d_o_a")
    dyn = matmul(ds_out, w_ssm, 'nt', 512, 1024, 2048, F32, "d_y_n")
    dy, dz_s, g_ssm_norm, g_dvec = gated_norm_bwd(dyn, y, xc, proj, dvec, ssm_norm_w, cfg, "gated_norm_bwd")
    dxs, dB, dC, dxsum_g, dda_g = ssd_scan_bwd(xc, dtg, acg, act, states, dy, dtab, cfg, "ssd_scan_bwd")
    ddt_raw, g_dt_bias, g_a_log = dt_bwd(_from_group(dxsum_g, cfg), _from_group(dda_g, cfg), dt_raw, dt,
                                         dt_bias_p, a_log_p, cfg, "dt_bwd")
    dxc = jnp.concatenate([dxs, dB, dC], axis=1)
    dpre, g_conv_w, g_conv_b = conv_bwd_a(proj, dxc, conv_w, conv_b, cfg, "conv_bwd_a")
    dxbc = conv_bwd_b(dpre, conv_w, cfg, "conv_bwd_b")
    do_mix, dz_a, dl = attn_out_bwd(do_a, o_mix, proj, cfg, "attn_out_bwd")
    parts = [attn_bwd(proj, do_mix, ltot, dl, slopes, cfg, window, d, "attn_bwd_d%d" % d)
             for window, d in cfg.patterns]
    dq = add3_cast(parts[0][0], parts[1][0], parts[2][0], "sum_dq")
    dk = add3_cast(parts[0][1], parts[1][1], parts[2][1], "sum_dk")
    dv = add3_cast(parts[0][2], parts[1][2], parts[2][2], "sum_dv")
    dproj = jnp.concatenate([dq, dk, dv, dz_a, dz_s, dxbc, dga, dgs], axis=1)
    g_w_main = matmul(hn, dproj, 'tn', 1024, 1024, 512, F32, "g_w_main")
    g_w_dt = matmul(hn, ddt_raw, 'tn', 1024, 128, 512, F32, "g_w_dt")
    dhn_a = matmul(dproj, w_main, 'nt', 512, 1024, 2048, F32, "d_hn")
    dhn_b = matmul(ddt_raw, w_dt, 'nt', 512, 1024, 128, F32, "d_hn_dt")
    grad_x, g_norm_w = rmsnorm_bwd(dhn_a, dhn_b, x, norm_w, dout, "rmsnorm_bwd")

    g_d_skip = jnp.sum(g_dvec.reshape(cfg.SH, cfg.P), axis=1).reshape(1, cfg.SH)
    small = dict(norm_w=g_norm_w, conv_b=g_conv_b, dt_bias=g_dt_bias[:, :cfg.SH], a_log=g_a_log[:, :cfg.SH],
                 d_skip=g_d_skip, ssm_norm_w=g_ssm_norm, final_norm_w=g_final_w, conv_w=g_conv_w[:cfg.KC])
    big = dict(w_main=g_w_main, w_dt=g_w_dt, w_attn=g_w_attn, w_ssm=g_w_ssm, w_out=g_w_out)
    return loss_p, grad_x, small, big


def _mesh_pos():
    return lax.axis_index("x"), lax.axis_index("y"), lax.axis_index("c")


def _flat(pos):
    return 4 * pos[0] + 2 * pos[1] + pos[2]


def exchange(arrs, gathers, name):
    n = len(arrs)
    out_shapes = [jax.ShapeDtypeStruct(((N_DEV,) + a.shape) if g else a.shape, a.dtype) for a, g in zip(arrs, gathers)]

    def body(*refs):
        ins, outs = refs[:n], refs[n:2 * n]
        send_sems, recv_sems, loc_sems = refs[2 * n:]
        pos = _mesh_pos()
        me = _flat(pos)
        copies = []
        for a in range(n):
            mine = ins[a] if gathers[a] else ins[a].at[me]
            loc = pltpu.make_async_copy(mine, outs[a].at[me], loc_sems.at[a])
            loc.start()
            copies.append(loc)
            for k in range(1, N_DEV):
                flip = ((k >> 2) & 1, (k >> 1) & 1, k & 1)
                peer = tuple(1 - p if f else p for p, f in zip(pos, flip))
                pk = _flat(peer)
                src = ins[a] if gathers[a] else ins[a].at[pk]
                cp = pltpu.make_async_remote_copy(
                    src_ref=src, dst_ref=outs[a].at[me],
                    send_sem=send_sems.at[a * (N_DEV - 1) + k - 1], recv_sem=recv_sems.at[a * (N_DEV - 1) + k - 1],
                    device_id=peer, device_id_type=pl.DeviceIdType.MESH)
                cp.start()
                arrive = pltpu.make_async_remote_copy(
                    src_ref=src, dst_ref=outs[a].at[pk],
                    send_sem=send_sems.at[a * (N_DEV - 1) + k - 1], recv_sem=recv_sems.at[a * (N_DEV - 1) + k - 1],
                    device_id=peer, device_id_type=pl.DeviceIdType.MESH)
                copies.append(arrive)
        for cp in copies:
            cp.wait()

    hbm = pl.BlockSpec(memory_space=pltpu.HBM)
    return pl.pallas_call(
        body, name=name, in_specs=[hbm] * n, out_specs=[hbm] * n, out_shape=out_shapes,
        scratch_shapes=[pltpu.SemaphoreType.DMA((n * (N_DEV - 1),)), pltpu.SemaphoreType.DMA((n * (N_DEV - 1),)),
                        pltpu.SemaphoreType.DMA((n,))],
    )(*arrs)


def adamw(g_src, w, m, v, summed, name, tr=64):
    R, C = w.shape
    tr = min(tr, R)
    assert R % tr == 0

    def body(g_ref, w_ref, m_ref, v_ref, g_out, d_out, m_out, v_out):
        if summed:
            g = g_ref[0].astype(F32)
            for j in range(1, N_DEV):
                g = g + g_ref[j].astype(F32)
        else:
            g = g_ref[...]
        mn = ADAM_B1 * m_ref[...] + (1.0 - ADAM_B1) * g
        vn = ADAM_B2 * v_ref[...] + (1.0 - ADAM_B2) * (g * g)
        m_hat = mn / (1.0 - ADAM_B1 ** ADAM_STEP)
        v_hat = vn / (1.0 - ADAM_B2 ** ADAM_STEP)
        g_out[...] = g
        d_out[...] = -ADAM_LR * (m_hat / (jnp.sqrt(v_hat) + ADAM_EPS) + ADAM_WD * w_ref[...])
        m_out[...] = mn
        v_out[...] = vn

    row = pl.BlockSpec((tr, C), lambda i: (i, 0))
    gspec = pl.BlockSpec((N_DEV, tr, C), lambda i: (0, i, 0)) if summed else row
    sh = jax.ShapeDtypeStruct((R, C), F32)
    return pl.pallas_call(
        body, name=name, grid=(R // tr,), in_specs=[gspec, row, row, row], out_specs=[row] * 4, out_shape=[sh] * 4,
        compiler_params=_cp(("parallel",)),
    )(g_src, w, m, v)


SMALL = ('norm_w', 'conv_b', 'dt_bias', 'a_log', 'd_skip', 'ssm_norm_w', 'final_norm_w')


def _rows(n):
    return -(-n // (8 * LANES)) * 8


def _pack(vals):
    parts = []
    for a in vals:
        f = a.reshape(-1)
        parts.append(jnp.pad(f, (0, _rows(f.size) * LANES - f.size)).reshape(-1, LANES))
    return jnp.concatenate(parts, axis=0)


def _unpack(packed, shapes):
    out, r = [], 0
    for s in shapes:
        n = math.prod(s)
        out.append(packed[r:r + _rows(n)].reshape(-1)[:n].reshape(s))
        r += _rows(n)
    return out


def kernel(x, norm_w, w_in, conv_w, conv_b, dt_bias, a_log, d_skip, ssm_norm_w, w_attn_branch, w_ssm_branch, w_out, final_norm_w, loss_target, m_norm_w, m_w_in, m_conv_w, m_conv_b, m_dt_bias, m_a_log, m_d_skip, m_ssm_norm_w, m_w_attn_branch, m_w_ssm_branch, m_w_out, m_final_norm_w, v_norm_w, v_w_in, v_conv_w, v_conv_b, v_dt_bias, v_a_log, v_d_skip, v_ssm_norm_w, v_w_attn_branch, v_w_ssm_branch, v_w_out, v_final_norm_w):
    cfg = CFG
    D, SH = cfg.D, cfg.SH
    me = _flat(_mesh_pos())
    dt0 = 4 * cfg.AW + cfg.SI + cfg.CD
    ws = w_in.shape[-1]

    g_in, g_attn, g_ssm, g_out, g_cw = exchange(
        [w_in[0].astype(BF16), w_attn_branch[0].astype(BF16), w_ssm_branch[0].astype(BF16), w_out[0].astype(BF16),
         conv_w[0]], [True] * 5, "gather_weights")
    w_full = g_in.transpose(1, 0, 2).reshape(D, N_DEV * ws)
    w_main = jnp.concatenate([w_full[:, :dt0], w_full[:, dt0 + SH:]], axis=1)
    w_dt = _pad_lanes(w_full[:, dt0:dt0 + SH])
    w_attn = g_attn.reshape(cfg.AW, D)
    w_ssm = g_ssm.reshape(cfg.SI, D)
    w_o = g_out.reshape(D, D)
    conv_full = g_cw.transpose(1, 0, 2).reshape(cfg.KC, cfg.CD)

    loss_p, grad_x, small, big = local_step(
        cfg, x[0], loss_target[0], norm_w, conv_full, conv_b, dt_bias, a_log, d_skip,
        ssm_norm_w, final_norm_w, w_main, w_dt, w_attn, w_ssm, w_o)

    g_w_in = jnp.concatenate([big['w_main'][:, :dt0], big['w_dt'][:, :SH], big['w_main'][:, dt0:]], axis=1)
    sends = [g_w_in.reshape(D, N_DEV, ws).transpose(1, 0, 2),
             big['w_attn'].reshape(N_DEV, cfg.AW // N_DEV, D),
             big['w_ssm'].reshape(N_DEV, cfg.SI // N_DEV, D),
             big['w_out'].reshape(N_DEV, D // N_DEV, D)]
    r_in, r_attn, r_ssm, r_out = exchange([s.astype(BF16) for s in sends], [False] * 4, "scatter_grads")
    upd = {}
    upd['w_in'] = adamw(r_in, w_in[0], m_w_in[0], v_w_in[0], True, "adamw_w_in")
    upd['w_attn_branch'] = adamw(r_attn, w_attn_branch[0], m_w_attn_branch[0], v_w_attn_branch[0], True, "adamw_w_attn")
    upd['w_ssm_branch'] = adamw(r_ssm, w_ssm_branch[0], m_w_ssm_branch[0], v_w_ssm_branch[0], True, "adamw_w_ssm")
    upd['w_out'] = adamw(r_out, w_out[0], m_w_out[0], v_w_out[0], True, "adamw_w_out")

    extra = [jnp.zeros((cfg.KC, cfg.CD), F32), jnp.zeros((1, 1), F32)]
    shapes = [small[n].shape for n in SMALL] + [e.shape for e in extra]
    part = _pack([small[n] for n in SMALL] + [small['conv_w'], loss_p[:, :1]])
    gathered, = exchange([part], [True], "gather_small")
    given = dict(norm_w=(norm_w, m_norm_w, v_norm_w), conv_b=(conv_b, m_conv_b, v_conv_b),
                 dt_bias=(dt_bias, m_dt_bias, v_dt_bias), a_log=(a_log, m_a_log, v_a_log),
                 d_skip=(d_skip, m_d_skip, v_d_skip), ssm_norm_w=(ssm_norm_w, m_ssm_norm_w, v_ssm_norm_w),
                 final_norm_w=(final_norm_w, m_final_norm_w, v_final_norm_w))
    packed = [_pack([given[n][t] for n in SMALL] + extra) for t in range(3)]
    outs = adamw(gathered, *packed, True, "adamw_small", tr=part.shape[0])
    unpacked = [_unpack(o, shapes) for o in outs]
    for i, n in enumerate(SMALL):
        upd[n] = [u[i].reshape(given[n][0].shape) for u in unpacked]
    loss = unpacked[0][-1].reshape(())
    cw = conv_w.shape[-1]
    g_cw_mine = lax.dynamic_slice_in_dim(unpacked[0][-2], me * cw, cw, axis=1)
    upd['conv_w'] = adamw(g_cw_mine.reshape(-1, LANES), conv_w.reshape(-1, LANES), m_conv_w.reshape(-1, LANES),
                          v_conv_w.reshape(-1, LANES), False, "adamw_conv_w")

    order = ['norm_w', 'w_in', 'conv_w', 'conv_b', 'dt_bias', 'a_log', 'd_skip', 'ssm_norm_w', 'w_attn_branch',
             'w_ssm_branch', 'w_out', 'final_norm_w']
    like = dict(norm_w=norm_w, w_in=w_in, conv_w=conv_w, conv_b=conv_b, dt_bias=dt_bias, a_log=a_log, d_skip=d_skip,
                ssm_norm_w=ssm_norm_w, w_attn_branch=w_attn_branch, w_ssm_branch=w_ssm_branch, w_out=w_out,
                final_norm_w=final_norm_w)
    result = [loss, grad_x[None]]
    for t in range(4):
        result += [upd[n][t].reshape(like[n].shape) for n in order]
    return tuple(result)
```

```python
import functools
import math
from typing import NamedTuple

import jax
import jax.numpy as jnp
from jax import lax
from jax.experimental import pallas as pl
from jax.experimental.pallas import tpu as pltpu

F32 = jnp.float32
BF16 = jnp.bfloat16
RMS_EPS = 1e-6
NEG = -1e30
N_DEV = 8
LANES = 128
ATTN_BLOCK = 128
ADAM_LR, ADAM_B1, ADAM_B2, ADAM_EPS, ADAM_WD, ADAM_STEP = 0.001, 0.9, 0.999, 1e-08, 0.01, 10
VMEM_LIMIT = 56 * 1024 * 1024


class Cfg(NamedTuple):
    D: int = 2048
    S: int = 8192
    AH: int = 16
    E: int = 128
    HB: int = 4
    patterns: tuple = ((128, 1), (512, 4), (2048, 16))
    SI: int = 4096
    P: int = 64
    SG: int = 8
    SN: int = 128
    KC: int = 4
    L: int = 128

    @property
    def AW(self): return self.AH * self.E
    @property
    def SH(self): return self.SI // self.P
    @property
    def HPG(self): return self.SH // self.SG
    @property
    def GN(self): return self.SG * self.SN
    @property
    def CD(self): return self.SI + 2 * self.GN
    @property
    def k0(self): return self.AW
    @property
    def v0(self): return 2 * self.AW
    @property
    def za0(self): return 3 * self.AW
    @property
    def zs0(self): return 4 * self.AW
    @property
    def xbc0(self): return 4 * self.AW + self.SI
    @property
    def ga0(self): return self.xbc0 + self.CD
    @property
    def gs0(self): return self.ga0 + self.D
    @property
    def NP(self): return self.gs0 + self.D
    @property
    def N_IN(self): return self.NP + self.SH


CFG = Cfg()


def _cp(sem=None, vmem=VMEM_LIMIT):
    return pltpu.CompilerParams(dimension_semantics=sem, vmem_limit_bytes=vmem)


def _sigmoid(z):
    return 1.0 / (1.0 + jnp.exp(-z))


def _dot(a, b, dims):
    return lax.dot_general(a, b, (dims, ((), ())), preferred_element_type=F32)


NN = ((1,), (0,))
NT = ((1,), (1,))
TN = ((0,), (0,))


def _blk(off, width):
    assert off % width == 0, (off, width)
    return off // width


def matmul(a, b, mode, tm, tn, tk, out_dtype, name):
    if mode == 'nn':
        (M, K), (_, N) = a.shape, b.shape
    elif mode == 'nt':
        (M, K), (N, _) = a.shape, b.shape
    else:
        (K, M), (_, N) = a.shape, b.shape
    tm, tn, tk = min(tm, M), min(tn, N), min(tk, K)
    assert M % tm == 0 and N % tn == 0 and K % tk == 0, (M, N, K, tm, tn, tk)
    nk = K // tk
    dims = {'nn': NN, 'nt': NT, 'tn': TN}[mode]

    def body(a_ref, b_ref, o_ref, *acc):
        part = _dot(a_ref[...].astype(BF16), b_ref[...].astype(BF16), dims)
        if nk == 1:
            o_ref[...] = part.astype(out_dtype)
        else:
            acc_ref, = acc
            k = pl.program_id(2)

            @pl.when(k == 0)
            def _():
                acc_ref[...] = part

            @pl.when(k > 0)
            def _():
                acc_ref[...] += part

            @pl.when(k == nk - 1)
            def _():
                o_ref[...] = acc_ref[...].astype(out_dtype)

    if mode == 'tn':
        a_spec = pl.BlockSpec((tk, tm), lambda n, m, k: (k, m))
    else:
        a_spec = pl.BlockSpec((tm, tk), lambda n, m, k: (m, k))
    if mode == 'nt':
        b_spec = pl.BlockSpec((tn, tk), lambda n, m, k: (n, k))
    else:
        b_spec = pl.BlockSpec((tk, tn), lambda n, m, k: (k, n))
    return pl.pallas_call(
        body, name=name, grid=(N // tn, M // tm, nk),
        in_specs=[a_spec, b_spec],
        out_specs=pl.BlockSpec((tm, tn), lambda n, m, k: (m, n)),
        out_shape=jax.ShapeDtypeStruct((M, N), out_dtype),
        scratch_shapes=[] if nk == 1 else [pltpu.VMEM((tm, tn), F32)],
        compiler_params=_cp(("parallel", "parallel", "arbitrary")),
    )(a, b)


def rmsnorm_fwd(x, w, name, tm=256):
    S, D = x.shape

    def body(x_ref, w_ref, o_ref):
        xv = x_ref[...]
        r = lax.rsqrt(jnp.mean(xv * xv, axis=-1, keepdims=True) + RMS_EPS)
        o_ref[...] = ((xv * r) * w_ref[...]).astype(BF16)

    return pl.pallas_call(
        body, name=name, grid=(S // tm,),
        in_specs=[pl.BlockSpec((tm, D), lambda i: (i, 0)), pl.BlockSpec((1, D), lambda i: (0, 0))],
        out_specs=pl.BlockSpec((tm, D), lambda i: (i, 0)),
        out_shape=jax.ShapeDtypeStruct((S, D), BF16),
        compiler_params=_cp(("parallel",)),
    )(x, w)


def rmsnorm_bwd(dh_a, dh_b, x, w, dout, name, tm=128):
    S, D = x.shape

    def body(da_ref, db_ref, x_ref, w_ref, do_ref, gx_ref, gw_ref):
        xv = x_ref[...]
        dh = da_ref[...] + db_ref[...]
        r = lax.rsqrt(jnp.mean(xv * xv, axis=-1, keepdims=True) + RMS_EPS)
        g = dh * w_ref[...]
        dx = r * g - xv * (r * r * r) * jnp.mean(g * xv, axis=-1, keepdims=True)
        gx_ref[...] = do_ref[...] + dx
        gw = jnp.sum(dh * (xv * r), axis=0, keepdims=True)

        @pl.when(pl.program_id(0) == 0)
        def _():
            gw_ref[...] = gw

        @pl.when(pl.program_id(0) > 0)
        def _():
            gw_ref[...] += gw

    row = pl.BlockSpec((tm, D), lambda i: (i, 0))
    vec = pl.BlockSpec((1, D), lambda i: (0, 0))
    return pl.pallas_call(
        body, name=name, grid=(S // tm,),
        in_specs=[row, row, row, vec, row],
        out_specs=[row, vec],
        out_shape=[jax.ShapeDtypeStruct((S, D), F32), jax.ShapeDtypeStruct((1, D), F32)],
        compiler_params=_cp(("arbitrary",)),
    )(dh_a, dh_b, x, w, dout)


def final_fwd_bwd(x, res, fw, tgt, name, tm=128):
    S, D = x.shape

    def body(x_ref, r_ref, w_ref, t_ref, do_ref, loss_ref, gw_ref):
        out = x_ref[...] + r_ref[...]
        w = w_ref[...]
        r = lax.rsqrt(jnp.mean(out * out, axis=-1, keepdims=True) + RMS_EPS)
        yn = out * r
        err = yn * w - t_ref[...]
        lrow = 0.5 * jnp.mean(err * err, axis=-1, keepdims=True)
        lsum = jnp.zeros((1, LANES), F32) + jnp.sum(lrow, axis=0, keepdims=True)
        dfin = err * (1.0 / D)
        g = dfin * w
        do_ref[...] = r * g - out * (r * r * r) * jnp.mean(g * out, axis=-1, keepdims=True)
        gw = jnp.sum(dfin * yn, axis=0, keepdims=True)

        @pl.when(pl.program_id(0) == 0)
        def _():
            gw_ref[...] = gw
            loss_ref[...] = lsum

        @pl.when(pl.program_id(0) > 0)
        def _():
            gw_ref[...] += gw
            loss_ref[...] += lsum

    row = pl.BlockSpec((tm, D), lambda i: (i, 0))
    vec = pl.BlockSpec((1, D), lambda i: (0, 0))
    return pl.pallas_call(
        body, name=name, grid=(S // tm,),
        in_specs=[row, row, vec, row],
        out_specs=[row, pl.BlockSpec((1, LANES), lambda i: (0, 0)), vec],
        out_shape=[jax.ShapeDtypeStruct((S, D), F32), jax.ShapeDtypeStruct((1, LANES), F32),
                   jax.ShapeDtypeStruct((1, D), F32)],
        compiler_params=_cp(("arbitrary",)),
    )(x, res, fw, tgt)


def merge_fwd(a_out, s_out, proj, cfg, name, tm=256):
    S, D = a_out.shape

    def body(a_ref, s_ref, ga_ref, gs_ref, o_ref):
        o_ref[...] = (_sigmoid(ga_ref[...]) * a_ref[...] + _sigmoid(gs_ref[...]) * s_ref[...]).astype(BF16)

    row = pl.BlockSpec((tm, D), lambda i: (i, 0))
    return pl.pallas_call(
        body, name=name, grid=(S // tm,),
        in_specs=[row, row, pl.BlockSpec((tm, D), lambda i: (i, _blk(cfg.ga0, D))),
                  pl.BlockSpec((tm, D), lambda i: (i, _blk(cfg.gs0, D)))],
        out_specs=row, out_shape=jax.ShapeDtypeStruct((S, D), BF16),
        compiler_params=_cp(("parallel",)),
    )(a_out, s_out, proj, proj)


def merge_bwd(dm, a_out, s_out, proj, cfg, name, tm=128):
    S, D = dm.shape

    def body(dm_ref, a_ref, s_ref, ga_ref, gs_ref, da_ref, ds_ref, dga_ref, dgs_ref):
        dmv = dm_ref[...]
        sa = _sigmoid(ga_ref[...])
        ss = _sigmoid(gs_ref[...])
        da_ref[...] = (dmv * sa).astype(BF16)
        ds_ref[...] = (dmv * ss).astype(BF16)
        dga_ref[...] = (dmv * a_ref[...] * (sa * (1.0 - sa))).astype(BF16)
        dgs_ref[...] = (dmv * s_ref[...] * (ss * (1.0 - ss))).astype(BF16)

    row = pl.BlockSpec((tm, D), lambda i: (i, 0))
    sh = jax.ShapeDtypeStruct((S, D), BF16)
    return pl.pallas_call(
        body, name=name, grid=(S // tm,),
        in_specs=[row, row, row, pl.BlockSpec((tm, D), lambda i: (i, _blk(cfg.ga0, D))),
                  pl.BlockSpec((tm, D), lambda i: (i, _blk(cfg.gs0, D)))],
        out_specs=[row] * 4, out_shape=[sh] * 4,
        compiler_params=_cp(("parallel",)),
    )(dm, a_out, s_out, proj, proj)


def _attn_rows(base, d):
    return pl.ds(base, ATTN_BLOCK) if d == 1 else pl.ds(base, ATTN_BLOCK, stride=d)


def _attn_units(cfg):
    dmax = max(d for _, d in cfg.patterns)
    units = []
    for p, (window, d) in enumerate(cfg.patterns):
        assert window // d == ATTN_BLOCK and dmax % d == 0
        nsub = dmax // d
        for b in range(nsub):
            for r in range(d):
                base = b * ATTN_BLOCK * d + r
                if b > 0:
                    units.append((p, d, base, (b - 1) * ATTN_BLOCK * d + r, False))
                else:
                    units.append((p, d, base, (nsub - 1) * ATTN_BLOCK * d + r, True))
    return units, ATTN_BLOCK * dmax


def _unit_scores(q, kp, kc, slope, d, prev_ok, scale):
    qi = lax.broadcasted_iota(jnp.int32, (ATTN_BLOCK, ATTN_BLOCK), 0)
    ki = lax.broadcasted_iota(jnp.int32, (ATTN_BLOCK, ATTN_BLOCK), 1)
    valid_p = ki >= qi if prev_ok is None else jnp.logical_and(ki >= qi, prev_ok)
    valid_c = ki <= qi
    sp = _dot(q, kp, NT) * scale + (-slope) * ((ATTN_BLOCK + qi - ki) * d).astype(F32)
    sc = _dot(q, kc, NT) * scale + (-slope) * ((qi - ki) * d).astype(F32)
    return jnp.where(valid_p, sp, NEG), jnp.where(valid_c, sc, NEG), valid_p, valid_c


def _slope_table(cfg):
    slopes = jnp.asarray([2.0 ** (-8.0 * (h + 1) / cfg.AH) for h in range(cfg.AH)], F32)
    return jnp.broadcast_to(slopes.reshape(cfg.AH, 1, 1), (cfg.AH, 8, LANES))


def attn_fused_fwd(proj, slopes, cfg, name):
    S, E, AH = cfg.S, cfg.E, cfg.AH
    units, SB = _attn_units(cfg)
    assert S % SB == 0
    npat = len(cfg.patterns)
    scale = E ** -0.5

    def spec(off, prev):
        c0 = _blk(off, E)
        if prev:
            return pl.BlockSpec((SB, E), lambda h, i: (jnp.maximum(i - 1, 0), c0 + h))
        return pl.BlockSpec((SB, E), lambda h, i: (i, c0 + h))

    def body(q_ref, kp_ref, kc_ref, vp_ref, vc_ref, z_ref, sl_ref, oa_ref, om_ref, lt_ref, *scr):
        o_s, l_s = scr[:npat], scr[npat:]
        i = pl.program_id(1)
        slope = sl_ref[0, 0:1, :]
        for p, d, base, pbase, from_prev in units:
            rows, prows = _attn_rows(base, d), _attn_rows(pbase, d)
            q = q_ref[rows, :].astype(BF16)
            kp = (kp_ref if from_prev else kc_ref)[prows, :].astype(BF16)
            vp = (vp_ref if from_prev else vc_ref)[prows, :].astype(BF16)
            sp, sc, _, _ = _unit_scores(q, kp, kc_ref[rows, :].astype(BF16), slope, d, (i > 0) if from_prev else None,
                                        scale)
            m = jnp.maximum(jnp.max(sp, axis=1, keepdims=True), jnp.max(sc, axis=1, keepdims=True))
            pp = jnp.exp(sp - m)
            pc = jnp.exp(sc - m)
            l = jnp.sum(pp, axis=1, keepdims=True) + jnp.sum(pc, axis=1, keepdims=True)
            inv = 1.0 / l
            o_s[p][rows, :] = _dot((pp * inv).astype(BF16), vp, NN) + \
                _dot((pc * inv).astype(BF16), vc_ref[rows, :].astype(BF16), NN)
            l_s[p][rows, :] = m + jnp.log(l)
        ls = [l_s[p][...] for p in range(npat)]
        m = functools.reduce(jnp.maximum, ls)
        lt = m + jnp.log(sum(jnp.exp(l_ - m) for l_ in ls))
        lt_ref[...] = lt
        mix = sum(jnp.exp(ls[p] - lt) * o_s[p][...] for p in range(npat))
        om_ref[...] = mix
        z = z_ref[...]
        oa_ref[...] = (mix * (z * _sigmoid(z))).astype(BF16)

    out = pl.BlockSpec((SB, E), lambda h, i: (i, h))
    return pl.pallas_call(
        body, name=name, grid=(AH, S // SB),
        in_specs=[spec(0, False), spec(cfg.k0, True), spec(cfg.k0, False), spec(cfg.v0, True), spec(cfg.v0, False),
                  spec(cfg.za0, False), pl.BlockSpec((1, 8, LANES), lambda h, i: (h, 0, 0))],
        out_specs=[out, out, pl.BlockSpec((SB, 1), lambda h, i: (h * (S // SB) + i, 0))],
        out_shape=[jax.ShapeDtypeStruct((S, cfg.AW), BF16), jax.ShapeDtypeStruct((S, cfg.AW), F32),
                   jax.ShapeDtypeStruct((AH * S, 1), F32)],
        scratch_shapes=[pltpu.VMEM((SB, E), F32)] * npat + [pltpu.VMEM((SB, 1), F32)] * npat,
        compiler_params=_cp(("parallel", "arbitrary")),
    )(proj, proj, proj, proj, proj, proj, slopes)


def attn_fused_bwd(proj, do_a, o_mix, ltot, slopes, cfg, name):
    S, E, AH = cfg.S, cfg.E, cfg.AH
    units, SB = _attn_units(cfg)
    nsb = S // SB
    last = nsb - 1
    scale = E ** -0.5

    def spec(off, prev):
        c0 = _blk(off, E)
        if prev:
            return pl.BlockSpec((SB, E), lambda h, i: (jnp.maximum(i - 1, 0), c0 + h))
        return pl.BlockSpec((SB, E), lambda h, i: (jnp.minimum(i, last), c0 + h))

    cur = pl.BlockSpec((SB, E), lambda h, i: (jnp.minimum(i, last), h))
    prev = pl.BlockSpec((SB, E), lambda h, i: (jnp.maximum(i - 1, 0), h))

    def body(q_ref, kp_ref, kc_ref, vp_ref, vc_ref, z_ref, doa_ref, om_ref, lt_ref, sl_ref,
             dq_ref, dk_ref, dv_ref, dz_ref, dmix_s, dl_s, dq_s, dkp_s, dvp_s, dkc_s, dvc_s):
        i = pl.program_id(1)

        @pl.when(i == 0)
        def _():
            dkc_s[...] = jnp.zeros_like(dkc_s)
            dvc_s[...] = jnp.zeros_like(dvc_s)

        @pl.when(i < nsb)
        def _():
            z = z_ref[...]
            s = _sigmoid(z)
            doa = doa_ref[...]
            om = om_ref[...]
            dmix = doa * (z * s)
            dmix_s[...] = dmix
            dz_ref[...] = (doa * om * (s * (1.0 + z * (1.0 - s)))).astype(BF16)
            dl_s[...] = jnp.sum(dmix * om, axis=1, keepdims=True)
            dkp_s[...] = dkc_s[...]
            dvp_s[...] = dvc_s[...]
            dkc_s[...] = jnp.zeros_like(dkc_s)
            dvc_s[...] = jnp.zeros_like(dvc_s)
            dq_s[...] = jnp.zeros_like(dq_s)
            slope = sl_ref[0, 0:1, :]
            for p, d, base, pbase, from_prev in units:
                rows, prows = _attn_rows(base, d), _attn_rows(pbase, d)
                q = q_ref[rows, :].astype(BF16)
                kc = kc_ref[rows, :].astype(BF16)
                kp = (kp_ref if from_prev else kc_ref)[prows, :].astype(BF16)
                vp = (vp_ref if from_prev else vc_ref)[prows, :].astype(BF16)
                do = dmix_s[rows, :].astype(BF16)
                lt = lt_ref[rows, :]
                dlt = dl_s[rows, :]
                sp, sc, valid_p, valid_c = _unit_scores(q, kp, kc, slope, d, (i > 0) if from_prev else None, scale)
                pp = jnp.where(valid_p, jnp.exp(sp - lt), 0.0)
                pc = jnp.where(valid_c, jnp.exp(sc - lt), 0.0)
                dsp = (pp * (_dot(do, vp, NT) - dlt) * scale).astype(BF16)
                dsc = (pc * (_dot(do, vc_ref[rows, :].astype(BF16), NT) - dlt) * scale).astype(BF16)
                dq_s[rows, :] += _dot(dsp, kp, NN) + _dot(dsc, kc, NN)
                dkc_s[rows, :] += _dot(dsc, q, TN)
                dvc_s[rows, :] += _dot(pc.astype(BF16), do, TN)
                dk_t, dv_t = (dkp_s, dvp_s) if from_prev else (dkc_s, dvc_s)
                dk_t[prows, :] += _dot(dsp, q, TN)
                dv_t[prows, :] += _dot(pp.astype(BF16), do, TN)
            dq_ref[...] = dq_s[...].astype(BF16)
            dk_ref[...] = dkp_s[...].astype(BF16)
            dv_ref[...] = dvp_s[...].astype(BF16)

        @pl.when(i == nsb)
        def _():
            dk_ref[...] = dkc_s[...].astype(BF16)
            dv_ref[...] = dvc_s[...].astype(BF16)

    sh = jax.ShapeDtypeStruct((S, cfg.AW), BF16)
    acc = pltpu.VMEM((SB, E), F32)
    return pl.pallas_call(
        body, name=name, grid=(AH, nsb + 1),
        in_specs=[spec(0, False), spec(cfg.k0, True), spec(cfg.k0, False), spec(cfg.v0, True), spec(cfg.v0, False),
                  spec(cfg.za0, False), cur, cur,
                  pl.BlockSpec((SB, 1), lambda h, i: (h * nsb + jnp.minimum(i, last), 0)),
                  pl.BlockSpec((1, 8, LANES), lambda h, i: (h, 0, 0))],
        out_specs=[cur, prev, prev, cur], out_shape=[sh] * 4,
        scratch_shapes=[acc, pltpu.VMEM((SB, 1), F32), acc, acc, acc, acc, acc],
        compiler_params=_cp(("parallel", "arbitrary")),
    )(proj, proj, proj, proj, proj, proj, do_a, o_mix, ltot, slopes)


HALO = 8


def _conv_pre(ext_ref, x_ref, h_ref, w_ref, b_ref, tm, kc):
    first = pl.program_id(1) == 0
    ext_ref[0:HALO, :] = jnp.where(first, 0.0, h_ref[...])
    ext_ref[HALO:, :] = x_ref[...]
    pre = b_ref[...] + jnp.zeros_like(x_ref[...])
    for k in range(kc):
        pre = pre + w_ref[k:k + 1, :] * ext_ref[pl.ds(HALO - (kc - 1) + k, tm), :]
    return pre


def conv_fwd(proj, w, b, cfg, name, tm=512, tc=512):
    S, CD, KC = cfg.S, cfg.CD, cfg.KC
    tc = min(tc, CD)
    c0 = _blk(cfg.xbc0, tc)
    hb = tm // HALO

    def body(x_ref, h_ref, w_ref, b_ref, o_ref, ext_ref):
        pre = _conv_pre(ext_ref, x_ref, h_ref, w_ref, b_ref, tm, KC)
        o_ref[...] = pre * _sigmoid(pre)

    return pl.pallas_call(
        body, name=name, grid=(CD // tc, S // tm),
        in_specs=[pl.BlockSpec((tm, tc), lambda c, i: (i, c0 + c)),
                  pl.BlockSpec((HALO, tc), lambda c, i: (jnp.maximum(i * hb - 1, 0), c0 + c)),
                  pl.BlockSpec((KC, tc), lambda c, i: (0, c)),
                  pl.BlockSpec((1, tc), lambda c, i: (0, c))],
        out_specs=pl.BlockSpec((tm, tc), lambda c, i: (i, c)),
        out_shape=jax.ShapeDtypeStruct((S, CD), F32),
        scratch_shapes=[pltpu.VMEM((tm + HALO, tc), F32)],
        compiler_params=_cp(("parallel", "arbitrary")),
    )(proj, proj, w, b)


def conv_bwd_a(proj, dxc, w, b, cfg, name, tm=512, tc=512):
    S, CD, KC = cfg.S, cfg.CD, cfg.KC
    tc = min(tc, CD)
    c0 = _blk(cfg.xbc0, tc)
    hb = tm // HALO

    def body(x_ref, h_ref, d_ref, w_ref, b_ref, dp_ref, gw_ref, gb_ref, ext_ref):
        pre = _conv_pre(ext_ref, x_ref, h_ref, w_ref, b_ref, tm, KC)
        s = _sigmoid(pre)
        dpre = d_ref[...] * (s * (1.0 + pre * (1.0 - s)))
        dp_ref[...] = dpre
        gb = jnp.sum(dpre, axis=0, keepdims=True)
        gws = [jnp.sum(dpre * ext_ref[pl.ds(HALO - (KC - 1) + k, tm), :], axis=0, keepdims=True) for k in range(KC)]
        gw = jnp.concatenate(gws + [jnp.zeros((8 - KC, tc), F32)], axis=0)

        @pl.when(pl.program_id(1) == 0)
        def _():
            gw_ref[...] = gw
            gb_ref[...] = gb

        @pl.when(pl.program_id(1) > 0)
        def _():
            gw_ref[...] += gw
            gb_ref[...] += gb

    return pl.pallas_call(
        body, name=name, grid=(CD // tc, S // tm),
        in_specs=[pl.BlockSpec((tm, tc), lambda c, i: (i, c0 + c)),
                  pl.BlockSpec((HALO, tc), lambda c, i: (jnp.maximum(i * hb - 1, 0), c0 + c)),
                  pl.BlockSpec((tm, tc), lambda c, i: (i, c)),
                  pl.BlockSpec((KC, tc), lambda c, i: (0, c)),
                  pl.BlockSpec((1, tc), lambda c, i: (0, c))],
        out_specs=[pl.BlockSpec((tm, tc), lambda c, i: (i, c)),
                   pl.BlockSpec((8, tc), lambda c, i: (0, c)),
                   pl.BlockSpec((1, tc), lambda c, i: (0, c))],
        out_shape=[jax.ShapeDtypeStruct((S, CD), F32), jax.ShapeDtypeStruct((8, CD), F32),
                   jax.ShapeDtypeStruct((1, CD), F32)],
        scratch_shapes=[pltpu.VMEM((tm + HALO, tc), F32)],
        compiler_params=_cp(("parallel", "arbitrary")),
    )(proj, proj, dxc, w, b)


def conv_bwd_b(dpre, w, cfg, name, tm=512, tc=512):
    S, CD, KC = cfg.S, cfg.CD, cfg.KC
    tc = min(tc, CD)
    hb = tm // HALO
    nrb = S // tm
    last_h = S // HALO - 1

    def body(d_ref, h_ref, w_ref, o_ref, ext_ref):
        is_last = pl.program_id(1) == nrb - 1
        ext_ref[0:tm, :] = d_ref[...]
        ext_ref[tm:, :] = jnp.where(is_last, 0.0, h_ref[...])
        acc = w_ref[KC - 1:KC, :] * d_ref[...]
        for j in range(1, KC):
            acc = acc + w_ref[KC - 1 - j:KC - j, :] * ext_ref[pl.ds(j, tm), :]
        o_ref[...] = acc.astype(BF16)

    return pl.pallas_call(
        body, name=name, grid=(CD // tc, nrb),
        in_specs=[pl.BlockSpec((tm, tc), lambda c, i: (i, c)),
                  pl.BlockSpec((HALO, tc), lambda c, i: (jnp.minimum((i + 1) * hb, last_h), c)),
                  pl.BlockSpec((KC, tc), lambda c, i: (0, c))],
        out_specs=pl.BlockSpec((tm, tc), lambda c, i: (i, c)),
        out_shape=jax.ShapeDtypeStruct((S, CD), BF16),
        scratch_shapes=[pltpu.VMEM((tm + HALO, tc), F32)],
        compiler_params=_cp(("parallel", "arbitrary")),
    )(dpre, dpre, w)


def _pad_lanes(v, width=LANES):
    return jnp.pad(v, ((0, 0), (0, width - v.shape[1])))


def ssd_prep(dt_raw, dt_bias, a_log, cfg, name):
    S, L = cfg.S, cfg.L

    def body(x_ref, b_ref, al_ref, dt_ref, ac_ref):
        x = x_ref[...] + b_ref[...]
        dt = jnp.maximum(x, 0.0) + jnp.log(1.0 + jnp.exp(-jnp.abs(x)))
        da = dt * (-jnp.exp(al_ref[...]))
        li = lax.broadcasted_iota(jnp.int32, (L, L), 0)
        si = lax.broadcasted_iota(jnp.int32, (L, L), 1)
        tri = jnp.where(li >= si, 1.0, 0.0).astype(F32)
        dt_ref[...] = dt
        ac_ref[...] = lax.dot_general(tri, da, ((NN), ((), ())), precision=lax.Precision.HIGHEST,
                                      preferred_element_type=F32)

    row = pl.BlockSpec((L, LANES), lambda i: (i, 0))
    vec = pl.BlockSpec((1, LANES), lambda i: (0, 0))
    sh = jax.ShapeDtypeStruct((S, LANES), F32)
    return pl.pallas_call(
        body, name=name, grid=(S // L,), in_specs=[row, vec, vec], out_specs=[row, row], out_shape=[sh, sh],
        compiler_params=_cp(("parallel",)),
    )(dt_raw, dt_bias, a_log)


def _spread(v, n):
    return jnp.broadcast_to(v[:, :, None], v.shape + (n,)).reshape(v.shape[0], v.shape[1] * n)


def _from_group(v, cfg):
    return _pad_lanes(v.transpose(1, 0, 2).reshape(cfg.S, cfg.SH))


def ssd_scan_fwd(xc, dtb, acb, act, cfg, name):
    S, L, P, SN, HPG, SG, SI = cfg.S, cfg.L, cfg.P, cfg.SN, cfg.HPG, cfg.SG, cfg.SI
    nc = S // L
    GW = HPG * P
    bcol, ccol = _blk(SI, SN), _blk(SI + cfg.GN, SN)

    def body(xs_ref, b_ref, c_ref, dt_ref, ac_ref, at_ref, y_ref, st_ref, st):
        @pl.when(pl.program_id(1) == 0)
        def _():
            st[...] = jnp.zeros_like(st)

        st_ref[0] = st[...]
        B = b_ref[...].astype(BF16)
        C = c_ref[...].astype(BF16)
        G = _dot(C, B, NT)
        causal = lax.broadcasted_iota(jnp.int32, (L, L), 0) >= lax.broadcasted_iota(jnp.int32, (L, L), 1)
        for j in range(HPG):
            sl = slice(j * P, (j + 1) * P)
            a_col = ac_ref[:, j * LANES:(j + 1) * LANES]
            a_p = ac_ref[:, j * LANES:j * LANES + P]
            dm = jnp.where(causal, jnp.exp(a_col - at_ref[j:j + 1, :]), 0.0)
            xdt = xs_ref[:, sl] * dt_ref[:, sl]
            s0 = st[j]
            yd = _dot((G * dm).astype(BF16), xdt.astype(BF16), NN)
            yo = jnp.exp(a_p) * _dot(C, s0.astype(BF16), NT)
            y_ref[:, sl] = yd + yo
            ws = jnp.exp(a_p[L - 1:L, :] - a_p)
            st[j] = s0 * jnp.exp(a_col[L - 1:L, :]) + _dot((xdt * ws).astype(BF16), B, TN)

    y, states = pl.pallas_call(
        body, name=name, grid=(SG, nc),
        in_specs=[pl.BlockSpec((L, GW), lambda g, c: (c, g)),
                  pl.BlockSpec((L, SN), lambda g, c: (c, bcol + g)),
                  pl.BlockSpec((L, SN), lambda g, c: (c, ccol + g)),
                  pl.BlockSpec((L, GW), lambda g, c: (c, g)),
                  pl.BlockSpec((L, HPG * LANES), lambda g, c: (c, g)),
                  pl.BlockSpec((HPG, L), lambda g, c: (g, c))],
        out_specs=[pl.BlockSpec((L, GW), lambda g, c: (c, g)),
                   pl.BlockSpec((1, HPG, P, SN), lambda g, c: (c, g, 0, 0))],
        out_shape=[jax.ShapeDtypeStruct((S, SI), F32), jax.ShapeDtypeStruct((nc, cfg.SH, P, SN), F32)],
        scratch_shapes=[pltpu.VMEM((HPG, P, SN), F32)],
        compiler_params=_cp(("parallel", "arbitrary")),
    )(xc, xc, xc, dtb, acb, act)
    return y, states


def ssd_scan_bwd(xc, dtb, acb, act, states, y, dy, dvec, cfg, name):
    S, L, P, SN, HPG, SG, SI = cfg.S, cfg.L, cfg.P, cfg.SN, cfg.HPG, cfg.SG, cfg.SI
    nc = S // L
    GW = HPG * P
    bcol, ccol = _blk(SI, SN), _blk(SI + cfg.GN, SN)

    def rc(c):
        return nc - 1 - c

    def body(xs_ref, b_ref, c_ref, dt_ref, ac_ref, at_ref, st_ref, y_ref, dy_ref, dk_ref,
             dxs_ref, db_ref, dc_ref, dac_ref, dxsum_ref, dst):
        @pl.when(pl.program_id(1) == 0)
        def _():
            dst[...] = jnp.zeros_like(dst)

        B = b_ref[...].astype(BF16)
        C = c_ref[...].astype(BF16)
        G = _dot(C, B, NT)
        causal = lax.broadcasted_iota(jnp.int32, (L, L), 0) >= lax.broadcasted_iota(jnp.int32, (L, L), 1)
        is_last = lax.broadcasted_iota(jnp.int32, (L, 1), 0) == L - 1
        dgsum = jnp.zeros((L, L), F32)
        dc_acc = jnp.zeros((L, SN), F32)
        db_acc = jnp.zeros((L, SN), F32)
        for j in range(HPG):
            sl = slice(j * P, (j + 1) * P)
            a_col = ac_ref[:, j * LANES:(j + 1) * LANES]
            a_p = ac_ref[:, j * LANES:j * LANES + P]
            dm = jnp.where(causal, jnp.exp(a_col - at_ref[j:j + 1, :]), 0.0)
            xs = xs_ref[:, sl]
            dtp = dt_ref[:, sl]
            xdt = xs * dtp
            xdtb = xdt.astype(BF16)
            dY = dy_ref[:, sl]
            dYb = dY.astype(BF16)
            s0 = st_ref[0, j]
            s0b = s0.astype(BF16)
            ds1 = dst[j]
            ds1b = ds1.astype(BF16)
            ea_last = jnp.exp(a_col[L - 1:L, :])
            ws = jnp.exp(a_p[L - 1:L, :] - a_p)
            dR = (jnp.exp(a_p) * dY).astype(BF16)
            dgsum = dgsum + _dot(dYb, xdtb, NT) * dm
            Mb = (G * dm).astype(BF16)
            dX1 = _dot(Mb, dYb, TN)
            dX2 = ws * _dot(B, ds1b, NT)
            dX = dX1 + dX2
            pair = (dYb.astype(F32) - dY) * _dot(Mb, xdtb, NN) - xdtb.astype(F32) * dX1
            dc_acc = dc_acc + _dot(dR, s0b, NN)
            db_acc = db_acc + _dot((xdt * ws).astype(BF16), ds1b, NN)
            sc = jnp.sum(jnp.sum(ea_last * (ds1 * s0), axis=1, keepdims=True), axis=0, keepdims=True) + \
                jnp.sum(jnp.sum(xdt * dX2, axis=1, keepdims=True), axis=0, keepdims=True)
            dac_ref[0, :, j:j + 1] = jnp.sum(dY * y_ref[:, sl] + pair - xdt * dX2, axis=1, keepdims=True) + \
                jnp.where(is_last, sc, 0.0)
            dxsum_ref[0, :, j:j + 1] = jnp.sum(dX * xs, axis=1, keepdims=True)
            dxs_ref[:, sl] = dX * dtp + dk_ref[:, sl] * dY
            dst[j] = ea_last * ds1 + _dot(dR, C, TN)
        dgb = dgsum.astype(BF16)
        dc_ref[...] = dc_acc + _dot(dgb, B, NN)
        db_ref[...] = db_acc + _dot(dgb, C, TN)

    wide = pl.BlockSpec((L, GW), lambda g, c: (rc(c), g))
    colspec = pl.BlockSpec((1, L, HPG), lambda g, c: (g, rc(c), 0))
    return pl.pallas_call(
        body, name=name, grid=(SG, nc),
        in_specs=[wide,
                  pl.BlockSpec((L, SN), lambda g, c: (rc(c), bcol + g)),
                  pl.BlockSpec((L, SN), lambda g, c: (rc(c), ccol + g)),
                  wide,
                  pl.BlockSpec((L, HPG * LANES), lambda g, c: (rc(c), g)),
                  pl.BlockSpec((HPG, L), lambda g, c: (g, rc(c))),
                  pl.BlockSpec((1, HPG, P, SN), lambda g, c: (rc(c), g, 0, 0)),
                  wide, wide,
                  pl.BlockSpec((1, GW), lambda g, c: (0, g))],
        out_specs=[wide,
                   pl.BlockSpec((L, SN), lambda g, c: (rc(c), g)),
                   pl.BlockSpec((L, SN), lambda g, c: (rc(c), g)),
                   colspec, colspec],
        out_shape=[jax.ShapeDtypeStruct((S, SI), F32), jax.ShapeDtypeStruct((S, cfg.GN), F32),
                   jax.ShapeDtypeStruct((S, cfg.GN), F32),
                   jax.ShapeDtypeStruct((SG, S, HPG), F32), jax.ShapeDtypeStruct((SG, S, HPG), F32)],
        scratch_shapes=[pltpu.VMEM((HPG, P, SN), F32)],
        compiler_params=_cp(("parallel", "arbitrary")),
    )(xc, xc, xc, dtb, acb, act, states, y, dy, dvec)


def dt_bwd(dac, dxsum, dt_raw, dt, dt_bias, a_log, cfg, name):
    S, L = cfg.S, cfg.L

    def body(da_ref, dx_ref, x_ref, dt_ref, b_ref, al_ref, o_ref, gb_ref, ga_ref):
        a = -jnp.exp(al_ref[...])
        dtv = dt_ref[...]
        dxs = dx_ref[...]
        upper = jnp.where(lax.broadcasted_iota(jnp.int32, (L, L), 1) >= lax.broadcasted_iota(jnp.int32, (L, L), 0),
                          1.0, 0.0).astype(F32)
        dda = lax.dot_general(upper, da_ref[...], (NN, ((), ())), precision=lax.Precision.HIGHEST,
                              preferred_element_type=F32)
        draw = (dxs + dda * a) * _sigmoid(x_ref[...] + b_ref[...])
        o_ref[...] = draw.astype(BF16)
        gb = jnp.sum(draw, axis=0, keepdims=True)
        ga = jnp.sum(dda * dtv, axis=0, keepdims=True) * a

        @pl.when(pl.program_id(0) == 0)
        def _():
            gb_ref[...] = gb
            ga_ref[...] = ga

        @pl.when(pl.program_id(0) > 0)
        def _():
            gb_ref[...] += gb
            ga_ref[...] += ga

    row = pl.BlockSpec((L, LANES), lambda i: (i, 0))
    vec = pl.BlockSpec((1, LANES), lambda i: (0, 0))
    return pl.pallas_call(
        body, name=name, grid=(S // L,), in_specs=[row, row, row, row, vec, vec],
        out_specs=[row, vec, vec],
        out_shape=[jax.ShapeDtypeStruct((S, LANES), BF16), jax.ShapeDtypeStruct((1, LANES), F32),
                   jax.ShapeDtypeStruct((1, LANES), F32)],
        compiler_params=_cp(("arbitrary",)),
    )(dac, dxsum, dt_raw, dt, dt_bias, a_log)


def gated_norm_fwd(y, xc, proj, dvec, nw, cfg, name, tm=128):
    S, SI = cfg.S, cfg.SI

    def body(y_ref, xs_ref, z_ref, d_ref, w_ref, o_ref):
        z = z_ref[...]
        yg = (y_ref[...] + d_ref[...] * xs_ref[...]) * (z * _sigmoid(z))
        r = lax.rsqrt(jnp.mean(yg * yg, axis=-1, keepdims=True) + RMS_EPS)
        o_ref[...] = ((yg * r) * w_ref[...]).astype(BF16)

    row = pl.BlockSpec((tm, SI), lambda i: (i, 0))
    vec = pl.BlockSpec((1, SI), lambda i: (0, 0))
    return pl.pallas_call(
        body, name=name, grid=(S // tm,),
        in_specs=[row, row, pl.BlockSpec((tm, SI), lambda i: (i, _blk(cfg.zs0, SI))), vec, vec],
        out_specs=row, out_shape=jax.ShapeDtypeStruct((S, SI), BF16),
        compiler_params=_cp(("parallel",)),
    )(y, xc, proj, dvec, nw)


def gated_norm_bwd(dyn, y, xc, proj, dvec, nw, cfg, name, tm=128):
    S, SI = cfg.S, cfg.SI

    def body(dn_ref, y_ref, xs_ref, z_ref, d_ref, w_ref, dy_ref, dz_ref, gw_ref, gd_ref):
        z = z_ref[...]
        s = _sigmoid(z)
        sz = z * s
        xs = xs_ref[...]
        yf = y_ref[...] + d_ref[...] * xs
        yg = yf * sz
        r = lax.rsqrt(jnp.mean(yg * yg, axis=-1, keepdims=True) + RMS_EPS)
        dn = dn_ref[...]
        g = dn * w_ref[...]
        dyg = r * g - yg * (r * r * r) * jnp.mean(g * yg, axis=-1, keepdims=True)
        dy = dyg * sz
        dy_ref[...] = dy
        dz_ref[...] = (dyg * yf * (s * (1.0 + z * (1.0 - s)))).astype(BF16)
        gw = jnp.sum(dn * (yg * r), axis=0, keepdims=True)
        gd = jnp.sum(dy * xs, axis=0, keepdims=True)

        @pl.when(pl.program_id(0) == 0)
        def _():
            gw_ref[...] = gw
            gd_ref[...] = gd

        @pl.when(pl.program_id(0) > 0)
        def _():
            gw_ref[...] += gw
            gd_ref[...] += gd

    row = pl.BlockSpec((tm, SI), lambda i: (i, 0))
    vec = pl.BlockSpec((1, SI), lambda i: (0, 0))
    return pl.pallas_call(
        body, name=name, grid=(S // tm,),
        in_specs=[row, row, row, pl.BlockSpec((tm, SI), lambda i: (i, _blk(cfg.zs0, SI))), vec, vec],
        out_specs=[row, row, vec, vec],
        out_shape=[jax.ShapeDtypeStruct((S, SI), F32), jax.ShapeDtypeStruct((S, SI), BF16),
                   jax.ShapeDtypeStruct((1, SI), F32), jax.ShapeDtypeStruct((1, SI), F32)],
        compiler_params=_cp(("arbitrary",)),
    )(dyn, y, xc, proj, dvec, nw)


def local_step(cfg, x, tgt, norm_w, conv_w, conv_b, dt_bias, a_log, d_skip, ssm_norm_w, final_norm_w,
               w_main, w_dt, w_attn, w_ssm, w_out):
    S, D = cfg.S, cfg.D
    slopes = _slope_table(cfg)
    dt_bias_p = _pad_lanes(dt_bias)
    a_log_p = _pad_lanes(a_log)
    dvec = _spread(d_skip, cfg.P)

    hn = rmsnorm_fwd(x, norm_w, "rmsnorm_fwd")
    proj = matmul(hn, w_main, 'nn', 512, 1024, 2048, F32, "in_proj")
    dt_raw = matmul(hn, w_dt, 'nn', 512, 128, 2048, F32, "in_proj_dt")
    o_a, o_mix, ltot = attn_fused_fwd(proj, slopes, cfg, "attn_fwd")
    xc = conv_fwd(proj, conv_w, conv_b, cfg, "conv_fwd")
    dt, acum = ssd_prep(dt_raw, dt_bias_p, a_log_p, cfg, "ssd_prep")
    dtb = _spread(dt[:, :cfg.SH], cfg.P)
    acb = _spread(acum[:, :cfg.SH], LANES)
    act = acum[:, :cfg.SH].T
    y, states = ssd_scan_fwd(xc, dtb, acb, act, cfg, "ssd_scan_fwd")
    y_n = gated_norm_fwd(y, xc, proj, dvec, ssm_norm_w, cfg, "gated_norm_fwd")
    a_out = matmul(o_a, w_attn, 'nn', 512, 1024, 2048, F32, "attn_branch")
    s_out = matmul(y_n, w_ssm, 'nn', 512, 1024, 2048, F32, "ssm_branch")
    merged = merge_fwd(a_out, s_out, proj, cfg, "merge_fwd")
    res = matmul(merged, w_out, 'nn', 512, 1024, 2048, F32, "out_proj")
    dout, loss_p, g_final_w = final_fwd_bwd(x, res, final_norm_w.reshape(1, D), tgt, "final_fwd_bwd")

    g_w_out = matmul(merged, dout, 'tn', 1024, 1024, 512, F32, "g_w_out")
    dmerged = matmul(dout, w_out, 'nt', 512, 1024, 2048, F32, "d_merged")
    da_out, ds_out, dga, dgs = merge_bwd(dmerged, a_out, s_out, proj, cfg, "merge_bwd")
    g_w_attn = matmul(o_a, da_out, 'tn', 1024, 1024, 512, F32, "g_w_attn")
    g_w_ssm = matmul(y_n, ds_out, 'tn', 1024, 1024, 512, F32, "g_w_ssm")
    do_a = matmul(da_out, w_attn, 'nt', 512, 1024, 2048, F32, "d_o_a")
    dyn = matmul(ds_out, w_ssm, 'nt', 512, 1024, 2048, F32, "d_y_n")
    dy, dz_s, g_ssm_norm, g_dvec = gated_norm_bwd(dyn, y, xc, proj, dvec, ssm_norm_w, cfg, "gated_norm_bwd")
    dxs, dB, dC, dac_g, dxsum_g = ssd_scan_bwd(xc, dtb, acb, act, states, y, dy, dvec, cfg, "ssd_scan_bwd")
    ddt_raw, g_dt_bias, g_a_log = dt_bwd(_from_group(dac_g, cfg), _from_group(dxsum_g, cfg), dt_raw, dt,
                                         dt_bias_p, a_log_p, cfg, "dt_bwd")
    dxc = jnp.concatenate([dxs, dB, dC], axis=1)
    dpre, g_conv_w, g_conv_b = conv_bwd_a(proj, dxc, conv_w, conv_b, cfg, "conv_bwd_a")
    dxbc = conv_bwd_b(dpre, conv_w, cfg, "conv_bwd_b")
    dq, dk, dv, dz_a = attn_fused_bwd(proj, do_a, o_mix, ltot, slopes, cfg, "attn_bwd")
    dproj = jnp.concatenate([dq, dk, dv, dz_a, dz_s, dxbc, dga, dgs], axis=1)
    g_w_main = matmul(hn, dproj, 'tn', 1024, 1024, 512, F32, "g_w_main")
    g_w_dt = matmul(hn, ddt_raw, 'tn', 1024, 128, 512, F32, "g_w_dt")
    dhn_a = matmul(dproj, w_main, 'nt', 512, 1024, 2048, F32, "d_hn")
    dhn_b = matmul(ddt_raw, w_dt, 'nt', 512, 1024, 128, F32, "d_hn_dt")
    grad_x, g_norm_w = rmsnorm_bwd(dhn_a, dhn_b, x, norm_w, dout, "rmsnorm_bwd")

    g_d_skip = jnp.sum(g_dvec.reshape(cfg.SH, cfg.P), axis=1).reshape(1, cfg.SH)
    small = dict(norm_w=g_norm_w, conv_b=g_conv_b, dt_bias=g_dt_bias[:, :cfg.SH], a_log=g_a_log[:, :cfg.SH],
                 d_skip=g_d_skip, ssm_norm_w=g_ssm_norm, final_norm_w=g_final_w, conv_w=g_conv_w[:cfg.KC])
    big = dict(w_main=g_w_main, w_dt=g_w_dt, w_attn=g_w_attn, w_ssm=g_w_ssm, w_out=g_w_out)
    return loss_p, grad_x, small, big


def _mesh_pos():
    return lax.axis_index("x"), lax.axis_index("y"), lax.axis_index("c")


def _flat(pos):
    return 4 * pos[0] + 2 * pos[1] + pos[2]


def exchange(arrs, gathers, name):
    n = len(arrs)
    out_shapes = [jax.ShapeDtypeStruct(((N_DEV,) + a.shape) if g else a.shape, a.dtype) for a, g in zip(arrs, gathers)]

    def body(*refs):
        ins, outs = refs[:n], refs[n:2 * n]
        send_sems, recv_sems, loc_sems = refs[2 * n:]
        pos = _mesh_pos()
        me = _flat(pos)
        copies = []
        for a in range(n):
            mine = ins[a] if gathers[a] else ins[a].at[me]
            loc = pltpu.make_async_copy(mine, outs[a].at[me], loc_sems.at[a])
            loc.start()
            copies.append(loc)
            for k in range(1, N_DEV):
                flip = ((k >> 2) & 1, (k >> 1) & 1, k & 1)
                peer = tuple(1 - p if f else p for p, f in zip(pos, flip))
                pk = _flat(peer)
                src = ins[a] if gathers[a] else ins[a].at[pk]
                cp = pltpu.make_async_remote_copy(
                    src_ref=src, dst_ref=outs[a].at[me],
                    send_sem=send_sems.at[a * (N_DEV - 1) + k - 1], recv_sem=recv_sems.at[a * (N_DEV - 1) + k - 1],
                    device_id=peer, device_id_type=pl.DeviceIdType.MESH)
                cp.start()
                arrive = pltpu.make_async_remote_copy(
                    src_ref=src, dst_ref=outs[a].at[pk],
                    send_sem=send_sems.at[a * (N_DEV - 1) + k - 1], recv_sem=recv_sems.at[a * (N_DEV - 1) + k - 1],
                    device_id=peer, device_id_type=pl.DeviceIdType.MESH)
                copies.append(arrive)
        for cp in copies:
            cp.wait()

    hbm = pl.BlockSpec(memory_space=pltpu.HBM)
    return pl.pallas_call(
        body, name=name, in_specs=[hbm] * n, out_specs=[hbm] * n, out_shape=out_shapes,
        scratch_shapes=[pltpu.SemaphoreType.DMA((n * (N_DEV - 1),)), pltpu.SemaphoreType.DMA((n * (N_DEV - 1),)),
                        pltpu.SemaphoreType.DMA((n,))],
    )(*arrs)


def adamw(g_src, w, m, v, summed, name, tr=64):
    R, C = w.shape
    tr = min(tr, R)
    assert R % tr == 0

    def body(g_ref, w_ref, m_ref, v_ref, g_out, d_out, m_out, v_out):
        if summed:
            g = g_ref[0].astype(F32)
            for j in range(1, N_DEV):
                g = g + g_ref[j].astype(F32)
        else:
            g = g_ref[...]
        mn = ADAM_B1 * m_ref[...] + (1.0 - ADAM_B1) * g
        vn = ADAM_B2 * v_ref[...] + (1.0 - ADAM_B2) * (g * g)
        m_hat = mn / (1.0 - ADAM_B1 ** ADAM_STEP)
        v_hat = vn / (1.0 - ADAM_B2 ** ADAM_STEP)
        g_out[...] = g
        d_out[...] = -ADAM_LR * (m_hat / (jnp.sqrt(v_hat) + ADAM_EPS) + ADAM_WD * w_ref[...])
        m_out[...] = mn
        v_out[...] = vn

    row = pl.BlockSpec((tr, C), lambda i: (i, 0))
    gspec = pl.BlockSpec((N_DEV, tr, C), lambda i: (0, i, 0)) if summed else row
    sh = jax.ShapeDtypeStruct((R, C), F32)
    return pl.pallas_call(
        body, name=name, grid=(R // tr,), in_specs=[gspec, row, row, row], out_specs=[row] * 4, out_shape=[sh] * 4,
        compiler_params=_cp(("parallel",)),
    )(g_src, w, m, v)


SMALL = ('norm_w', 'conv_b', 'dt_bias', 'a_log', 'd_skip', 'ssm_norm_w', 'final_norm_w')


def _rows(n):
    return -(-n // (8 * LANES)) * 8


def _pack(vals):
    parts = []
    for a in vals:
        f = a.reshape(-1)
        parts.append(jnp.pad(f, (0, _rows(f.size) * LANES - f.size)).reshape(-1, LANES))
    return jnp.concatenate(parts, axis=0)


def _unpack(packed, shapes):
    out, r = [], 0
    for s in shapes:
        n = math.prod(s)
        out.append(packed[r:r + _rows(n)].reshape(-1)[:n].reshape(s))
        r += _rows(n)
    return out


def kernel(x, norm_w, w_in, conv_w, conv_b, dt_bias, a_log, d_skip, ssm_norm_w, w_attn_branch, w_ssm_branch, w_out, final_norm_w, loss_target, m_norm_w, m_w_in, m_conv_w, m_conv_b, m_dt_bias, m_a_log, m_d_skip, m_ssm_norm_w, m_w_attn_branch, m_w_ssm_branch, m_w_out, m_final_norm_w, v_norm_w, v_w_in, v_conv_w, v_conv_b, v_dt_bias, v_a_log, v_d_skip, v_ssm_norm_w, v_w_attn_branch, v_w_ssm_branch, v_w_out, v_final_norm_w):
    cfg = CFG
    D, SH = cfg.D, cfg.SH
    me = _flat(_mesh_pos())
    dt0 = 4 * cfg.AW + cfg.SI + cfg.CD
    ws = w_in.shape[-1]

    g_in, g_attn, g_ssm, g_out, g_cw = exchange(
        [w_in[0].astype(BF16), w_attn_branch[0].astype(BF16), w_ssm_branch[0].astype(BF16), w_out[0].astype(BF16),
         conv_w[0]], [True] * 5, "gather_weights")
    w_full = g_in.transpose(1, 0, 2).reshape(D, N_DEV * ws)
    w_main = jnp.concatenate([w_full[:, :dt0], w_full[:, dt0 + SH:]], axis=1)
    w_dt = _pad_lanes(w_full[:, dt0:dt0 + SH])
    w_attn = g_attn.reshape(cfg.AW, D)
    w_ssm = g_ssm.reshape(cfg.SI, D)
    w_o = g_out.reshape(D, D)
    conv_full = g_cw.transpose(1, 0, 2).reshape(cfg.KC, cfg.CD)

    loss_p, grad_x, small, big = local_step(
        cfg, x[0], loss_target[0], norm_w, conv_full, conv_b, dt_bias, a_log, d_skip,
        ssm_norm_w, final_norm_w, w_main, w_dt, w_attn, w_ssm, w_o)

    g_w_in = jnp.concatenate([big['w_main'][:, :dt0], big['w_dt'][:, :SH], big['w_main'][:, dt0:]], axis=1)
    sends = [g_w_in.reshape(D, N_DEV, ws).transpose(1, 0, 2),
             big['w_attn'].reshape(N_DEV, cfg.AW // N_DEV, D),
             big['w_ssm'].reshape(N_DEV, cfg.SI // N_DEV, D),
             big['w_out'].reshape(N_DEV, D // N_DEV, D)]
    r_in, r_attn, r_ssm, r_out = exchange([s.astype(BF16) for s in sends], [False] * 4, "scatter_grads")
    upd = {}
    upd['w_in'] = adamw(r_in, w_in[0], m_w_in[0], v_w_in[0], True, "adamw_w_in")
    upd['w_attn_branch'] = adamw(r_attn, w_attn_branch[0], m_w_attn_branch[0], v_w_attn_branch[0], True, "adamw_w_attn")
    upd['w_ssm_branch'] = adamw(r_ssm, w_ssm_branch[0], m_w_ssm_branch[0], v_w_ssm_branch[0], True, "adamw_w_ssm")
    upd['w_out'] = adamw(r_out, w_out[0], m_w_out[0], v_w_out[0], True, "adamw_w_out")

    extra = [jnp.zeros((cfg.KC, cfg.CD), F32), jnp.zeros((1, 1), F32)]
    shapes = [small[n].shape for n in SMALL] + [e.shape for e in extra]
    part = _pack([small[n] for n in SMALL] + [small['conv_w'], loss_p[:, :1]])
    gathered, = exchange([part], [True], "gather_small")
    given = dict(norm_w=(norm_w, m_norm_w, v_norm_w), conv_b=(conv_b, m_conv_b, v_conv_b),
                 dt_bias=(dt_bias, m_dt_bias, v_dt_bias), a_log=(a_log, m_a_log, v_a_log),
                 d_skip=(d_skip, m_d_skip, v_d_skip), ssm_norm_w=(ssm_norm_w, m_ssm_norm_w, v_ssm_norm_w),
                 final_norm_w=(final_norm_w, m_final_norm_w, v_final_norm_w))
    packed = [_pack([given[n][t] for n in SMALL] + extra) for t in range(3)]
    outs = adamw(gathered, *packed, True, "adamw_small", tr=part.shape[0])
    unpacked = [_unpack(o, shapes) for o in outs]
    for i, n in enumerate(SMALL):
        upd[n] = [u[i].reshape(given[n][0].shape) for u in unpacked]
    loss = unpacked[0][-1].reshape(())
    cw = conv_w.shape[-1]
    g_cw_mine = lax.dynamic_slice_in_dim(unpacked[0][-2], me * cw, cw, axis=1)
    upd['conv_w'] = adamw(g_cw_mine.reshape(-1, LANES), conv_w.reshape(-1, LANES), m_conv_w.reshape(-1, LANES),
                          v_conv_w.reshape(-1, LANES), False, "adamw_conv_w")

    order = ['norm_w', 'w_in', 'conv_w', 'conv_b', 'dt_bias', 'a_log', 'd_skip', 'ssm_norm_w', 'w_attn_branch',
             'w_ssm_branch', 'w_out', 'final_norm_w']
    like = dict(norm_w=norm_w, w_in=w_in, conv_w=conv_w, conv_b=conv_b, dt_bias=dt_bias, a_log=a_log, d_skip=d_skip,
                ssm_norm_w=ssm_norm_w, w_attn_branch=w_attn_branch, w_ssm_branch=w_ssm_branch, w_out=w_out,
                final_norm_w=final_norm_w)
    result = [loss, grad_x[None]]
    for t in range(4):
        result += [upd[n][t].reshape(like[n].shape) for n in order]
    return tuple(result)
```

```python
import functools
import math
from typing import NamedTuple

import jax
import jax.numpy as jnp
from jax import lax
from jax.experimental import pallas as pl
from jax.experimental.pallas import tpu as pltpu

F32 = jnp.float32
BF16 = jnp.bfloat16
RMS_EPS = 1e-6
NEG = -1e30
N_DEV = 8
LANES = 128
ATTN_BLOCK = 128
ADAM_LR, ADAM_B1, ADAM_B2, ADAM_EPS, ADAM_WD, ADAM_STEP = 0.001, 0.9, 0.999, 1e-08, 0.01, 10
VMEM_LIMIT = 56 * 1024 * 1024


class Cfg(NamedTuple):
    D: int = 2048
    S: int = 8192
    AH: int = 16
    E: int = 128
    HB: int = 4
    patterns: tuple = ((128, 1), (512, 4), (2048, 16))
    SI: int = 4096
    P: int = 64
    SG: int = 8
    SN: int = 128
    KC: int = 4
    L: int = 128

    @property
    def AW(self): return self.AH * self.E
    @property
    def SH(self): return self.SI // self.P
    @property
    def HPG(self): return self.SH // self.SG
    @property
    def GN(self): return self.SG * self.SN
    @property
    def CD(self): return self.SI + 2 * self.GN
    @property
    def k0(self): return self.AW
    @property
    def v0(self): return 2 * self.AW
    @property
    def za0(self): return 3 * self.AW
    @property
    def zs0(self): return 4 * self.AW
    @property
    def xbc0(self): return 4 * self.AW + self.SI
    @property
    def ga0(self): return self.xbc0 + self.CD
    @property
    def gs0(self): return self.ga0 + self.D
    @property
    def NP(self): return self.gs0 + self.D
    @property
    def N_IN(self): return self.NP + self.SH


CFG = Cfg()


def _cp(sem=None, vmem=VMEM_LIMIT):
    return pltpu.CompilerParams(dimension_semantics=sem, vmem_limit_bytes=vmem)


def _sigmoid(z):
    return 1.0 / (1.0 + jnp.exp(-z))


def _dot(a, b, dims):
    return lax.dot_general(a, b, (dims, ((), ())), preferred_element_type=F32)


NN = ((1,), (0,))
NT = ((1,), (1,))
TN = ((0,), (0,))


def _blk(off, width):
    assert off % width == 0, (off, width)
    return off // width


def matmul(a, b, mode, tm, tn, tk, out_dtype, name, side=None):
    if mode == 'nn':
        (M, K), (_, N) = a.shape, b.shape
    elif mode == 'nt':
        (M, K), (N, _) = a.shape, b.shape
    else:
        (K, M), (_, N) = a.shape, b.shape
    tm, tn, tk = min(tm, M), min(tn, N), min(tk, K)
    assert M % tm == 0 and N % tn == 0 and K % tk == 0, (M, N, K, tm, tn, tk)
    nk = K // tk
    dims = {'nn': NN, 'nt': NT, 'tn': TN}[mode]

    def body(a_ref, b_ref, o_ref, *acc):
        part = _dot(a_ref[...].astype(BF16), b_ref[...].astype(BF16), dims)
        if nk == 1:
            o_ref[...] = part.astype(out_dtype)
        else:
            acc_ref, = acc
            k = pl.program_id(2)

            @pl.when(k == 0)
            def _():
                acc_ref[...] = part

            @pl.when(k > 0)
            def _():
                acc_ref[...] += part

            @pl.when(k == nk - 1)
            def _():
                o_ref[...] = acc_ref[...].astype(out_dtype)

    if mode == 'tn':
        a_spec = pl.BlockSpec((tk, tm), lambda n, m, k: (k, m))
    else:
        a_spec = pl.BlockSpec((tm, tk), lambda n, m, k: (m, k))
    if mode == 'nt':
        b_spec = pl.BlockSpec((tn, tk), lambda n, m, k: (n, k))
    else:
        b_spec = pl.BlockSpec((tk, tn), lambda n, m, k: (k, n))
    grid = (N // tn, M // tm, nk)
    o_spec = pl.BlockSpec((tm, tn), lambda n, m, k: (m, n))
    o_shape = jax.ShapeDtypeStruct((M, N), out_dtype)
    acc = [] if nk == 1 else [pltpu.VMEM((tm, tn), F32)]
    if side is None:
        return pl.pallas_call(
            body, name=name, grid=grid, in_specs=[a_spec, b_spec], out_specs=o_spec, out_shape=o_shape,
            scratch_shapes=acc, compiler_params=_cp(("parallel", "parallel", "arbitrary")),
        )(a, b)
    arrs, gathers = side
    whole = pl.BlockSpec(memory_space=pl.ANY)
    res = pl.pallas_call(
        with_exchange(body, 2, 1, gathers, grid), name=name, grid=grid,
        in_specs=[a_spec, b_spec] + [whole] * len(arrs), out_specs=[o_spec] + [whole] * len(arrs),
        out_shape=[o_shape] + _exchange_shapes(arrs, gathers),
        scratch_shapes=acc + _exchange_sems(len(arrs)),
        compiler_params=_cp(("arbitrary", "arbitrary", "arbitrary")),
    )(a, b, *arrs)
    return res[0], res[1:]


def rmsnorm_fwd(x, w, name, tm=256):
    S, D = x.shape

    def body(x_ref, w_ref, o_ref):
        xv = x_ref[...]
        r = lax.rsqrt(jnp.mean(xv * xv, axis=-1, keepdims=True) + RMS_EPS)
        o_ref[...] = ((xv * r) * w_ref[...]).astype(BF16)

    return pl.pallas_call(
        body, name=name, grid=(S // tm,),
        in_specs=[pl.BlockSpec((tm, D), lambda i: (i, 0)), pl.BlockSpec((1, D), lambda i: (0, 0))],
        out_specs=pl.BlockSpec((tm, D), lambda i: (i, 0)),
        out_shape=jax.ShapeDtypeStruct((S, D), BF16),
        compiler_params=_cp(("parallel",)),
    )(x, w)


def rmsnorm_bwd(dh_a, dh_b, x, w, dout, name, tm=128):
    S, D = x.shape

    def body(da_ref, db_ref, x_ref, w_ref, do_ref, gx_ref, gw_ref):
        xv = x_ref[...]
        dh = da_ref[...] + db_ref[...]
        r = lax.rsqrt(jnp.mean(xv * xv, axis=-1, keepdims=True) + RMS_EPS)
        g = dh * w_ref[...]
        dx = r * g - xv * (r * r * r) * jnp.mean(g * xv, axis=-1, keepdims=True)
        gx_ref[...] = do_ref[...] + dx
        gw = jnp.sum(dh * (xv * r), axis=0, keepdims=True)

        @pl.when(pl.program_id(0) == 0)
        def _():
            gw_ref[...] = gw

        @pl.when(pl.program_id(0) > 0)
        def _():
            gw_ref[...] += gw

    row = pl.BlockSpec((tm, D), lambda i: (i, 0))
    vec = pl.BlockSpec((1, D), lambda i: (0, 0))
    return pl.pallas_call(
        body, name=name, grid=(S // tm,),
        in_specs=[row, row, row, vec, row],
        out_specs=[row, vec],
        out_shape=[jax.ShapeDtypeStruct((S, D), F32), jax.ShapeDtypeStruct((1, D), F32)],
        compiler_params=_cp(("arbitrary",)),
    )(dh_a, dh_b, x, w, dout)


def final_fwd_bwd(x, res, fw, tgt, name, tm=128):
    S, D = x.shape

    def body(x_ref, r_ref, w_ref, t_ref, do_ref, loss_ref, gw_ref):
        out = x_ref[...] + r_ref[...]
        w = w_ref[...]
        r = lax.rsqrt(jnp.mean(out * out, axis=-1, keepdims=True) + RMS_EPS)
        yn = out * r
        err = yn * w - t_ref[...]
        lrow = 0.5 * jnp.mean(err * err, axis=-1, keepdims=True)
        lsum = jnp.zeros((1, LANES), F32) + jnp.sum(lrow, axis=0, keepdims=True)
        dfin = err * (1.0 / D)
        g = dfin * w
        do_ref[...] = r * g - out * (r * r * r) * jnp.mean(g * out, axis=-1, keepdims=True)
        gw = jnp.sum(dfin * yn, axis=0, keepdims=True)

        @pl.when(pl.program_id(0) == 0)
        def _():
            gw_ref[...] = gw
            loss_ref[...] = lsum

        @pl.when(pl.program_id(0) > 0)
        def _():
            gw_ref[...] += gw
            loss_ref[...] += lsum

    row = pl.BlockSpec((tm, D), lambda i: (i, 0))
    vec = pl.BlockSpec((1, D), lambda i: (0, 0))
    return pl.pallas_call(
        body, name=name, grid=(S // tm,),
        in_specs=[row, row, vec, row],
        out_specs=[row, pl.BlockSpec((1, LANES), lambda i: (0, 0)), vec],
        out_shape=[jax.ShapeDtypeStruct((S, D), F32), jax.ShapeDtypeStruct((1, LANES), F32),
                   jax.ShapeDtypeStruct((1, D), F32)],
        compiler_params=_cp(("arbitrary",)),
    )(x, res, fw, tgt)


def merge_fwd(a_out, s_out, proj, cfg, name, tm=256):
    S, D = a_out.shape

    def body(a_ref, s_ref, ga_ref, gs_ref, o_ref):
        o_ref[...] = (_sigmoid(ga_ref[...]) * a_ref[...] + _sigmoid(gs_ref[...]) * s_ref[...]).astype(BF16)

    row = pl.BlockSpec((tm, D), lambda i: (i, 0))
    return pl.pallas_call(
        body, name=name, grid=(S // tm,),
        in_specs=[row, row, pl.BlockSpec((tm, D), lambda i: (i, _blk(cfg.ga0, D))),
                  pl.BlockSpec((tm, D), lambda i: (i, _blk(cfg.gs0, D)))],
        out_specs=row, out_shape=jax.ShapeDtypeStruct((S, D), BF16),
        compiler_params=_cp(("parallel",)),
    )(a_out, s_out, proj, proj)


def merge_bwd(dm, a_out, s_out, proj, cfg, name, tm=128):
    S, D = dm.shape

    def body(dm_ref, a_ref, s_ref, ga_ref, gs_ref, da_ref, ds_ref, dga_ref, dgs_ref):
        dmv = dm_ref[...]
        sa = _sigmoid(ga_ref[...])
        ss = _sigmoid(gs_ref[...])
        da_ref[...] = (dmv * sa).astype(BF16)
        ds_ref[...] = (dmv * ss).astype(BF16)
        dga_ref[...] = (dmv * a_ref[...] * (sa * (1.0 - sa))).astype(BF16)
        dgs_ref[...] = (dmv * s_ref[...] * (ss * (1.0 - ss))).astype(BF16)

    row = pl.BlockSpec((tm, D), lambda i: (i, 0))
    sh = jax.ShapeDtypeStruct((S, D), BF16)
    return pl.pallas_call(
        body, name=name, grid=(S // tm,),
        in_specs=[row, row, row, pl.BlockSpec((tm, D), lambda i: (i, _blk(cfg.ga0, D))),
                  pl.BlockSpec((tm, D), lambda i: (i, _blk(cfg.gs0, D)))],
        out_specs=[row] * 4, out_shape=[sh] * 4,
        compiler_params=_cp(("parallel",)),
    )(dm, a_out, s_out, proj, proj)


def _attn_rows(base, d):
    return pl.ds(base, ATTN_BLOCK) if d == 1 else pl.ds(base, ATTN_BLOCK, stride=d)


def _attn_units(cfg):
    dmax = max(d for _, d in cfg.patterns)
    units = []
    for p, (window, d) in enumerate(cfg.patterns):
        assert window // d == ATTN_BLOCK and dmax % d == 0
        nsub = dmax // d
        for b in range(nsub):
            for r in range(d):
                base = b * ATTN_BLOCK * d + r
                if b > 0:
                    units.append((p, d, base, (b - 1) * ATTN_BLOCK * d + r, False))
                else:
                    units.append((p, d, base, (nsub - 1) * ATTN_BLOCK * d + r, True))
    return units, ATTN_BLOCK * dmax


def _unit_scores(q, kp, kc, slope, d, prev_ok, scale):
    qi = lax.broadcasted_iota(jnp.int32, (ATTN_BLOCK, ATTN_BLOCK), 0)
    ki = lax.broadcasted_iota(jnp.int32, (ATTN_BLOCK, ATTN_BLOCK), 1)
    valid_p = ki >= qi if prev_ok is None else jnp.logical_and(ki >= qi, prev_ok)
    valid_c = ki <= qi
    sp = _dot(q, kp, NT) * scale + (-slope) * ((ATTN_BLOCK + qi - ki) * d).astype(F32)
    sc = _dot(q, kc, NT) * scale + (-slope) * ((qi - ki) * d).astype(F32)
    return jnp.where(valid_p, sp, NEG), jnp.where(valid_c, sc, NEG), valid_p, valid_c


def _slope_table(cfg):
    slopes = jnp.asarray([2.0 ** (-8.0 * (h + 1) / cfg.AH) for h in range(cfg.AH)], F32)
    return jnp.broadcast_to(slopes.reshape(cfg.AH, 1, 1), (cfg.AH, 8, LANES))


def attn_fused_fwd(proj, slopes, cfg, name):
    S, E, AH = cfg.S, cfg.E, cfg.AH
    units, SB = _attn_units(cfg)
    assert S % SB == 0
    npat = len(cfg.patterns)
    scale = E ** -0.5

    def spec(off, prev):
        c0 = _blk(off, E)
        if prev:
            return pl.BlockSpec((SB, E), lambda h, i: (jnp.maximum(i - 1, 0), c0 + h))
        return pl.BlockSpec((SB, E), lambda h, i: (i, c0 + h))

    def body(q_ref, kp_ref, kc_ref, vp_ref, vc_ref, z_ref, sl_ref, oa_ref, om_ref, lt_ref, *scr):
        o_s, l_s = scr[:npat], scr[npat:]
        i = pl.program_id(1)
        slope = sl_ref[0, 0:1, :]
        for p, d, base, pbase, from_prev in units:
            rows, prows = _attn_rows(base, d), _attn_rows(pbase, d)
            q = q_ref[rows, :].astype(BF16)
            kp = (kp_ref if from_prev else kc_ref)[prows, :].astype(BF16)
            vp = (vp_ref if from_prev else vc_ref)[prows, :].astype(BF16)
            sp, sc, _, _ = _unit_scores(q, kp, kc_ref[rows, :].astype(BF16), slope, d, (i > 0) if from_prev else None,
                                        scale)
            m = jnp.maximum(jnp.max(sp, axis=1, keepdims=True), jnp.max(sc, axis=1, keepdims=True))
            pp = jnp.exp(sp - m)
            pc = jnp.exp(sc - m)
            l = jnp.sum(pp, axis=1, keepdims=True) + jnp.sum(pc, axis=1, keepdims=True)
            inv = 1.0 / l
            o_s[p][rows, :] = _dot((pp * inv).astype(BF16), vp, NN) + \
                _dot((pc * inv).astype(BF16), vc_ref[rows, :].astype(BF16), NN)
            l_s[p][rows, :] = m + jnp.log(l)
        ls = [l_s[p][...] for p in range(npat)]
        m = functools.reduce(jnp.maximum, ls)
        lt = m + jnp.log(sum(jnp.exp(l_ - m) for l_ in ls))
        lt_ref[...] = lt
        mix = sum(jnp.exp(ls[p] - lt) * o_s[p][...] for p in range(npat))
        om_ref[...] = mix
        z = z_ref[...]
        oa_ref[...] = (mix * (z * _sigmoid(z))).astype(BF16)

    out = pl.BlockSpec((SB, E), lambda h, i: (i, h))
    return pl.pallas_call(
        body, name=name, grid=(AH, S // SB),
        in_specs=[spec(0, False), spec(cfg.k0, True), spec(cfg.k0, False), spec(cfg.v0, True), spec(cfg.v0, False),
                  spec(cfg.za0, False), pl.BlockSpec((1, 8, LANES), lambda h, i: (h, 0, 0))],
        out_specs=[out, out, pl.BlockSpec((SB, 1), lambda h, i: (h * (S // SB) + i, 0))],
        out_shape=[jax.ShapeDtypeStruct((S, cfg.AW), BF16), jax.ShapeDtypeStruct((S, cfg.AW), F32),
                   jax.ShapeDtypeStruct((AH * S, 1), F32)],
        scratch_shapes=[pltpu.VMEM((SB, E), F32)] * npat + [pltpu.VMEM((SB, 1), F32)] * npat,
        compiler_params=_cp(("parallel", "arbitrary")),
    )(proj, proj, proj, proj, proj, proj, slopes)


def attn_fused_bwd(proj, do_a, o_mix, ltot, slopes, cfg, name):
    S, E, AH = cfg.S, cfg.E, cfg.AH
    units, SB = _attn_units(cfg)
    nsb = S // SB
    last = nsb - 1
    scale = E ** -0.5

    def spec(off, prev):
        c0 = _blk(off, E)
        if prev:
            return pl.BlockSpec((SB, E), lambda h, i: (jnp.maximum(i - 1, 0), c0 + h))
        return pl.BlockSpec((SB, E), lambda h, i: (jnp.minimum(i, last), c0 + h))

    cur = pl.BlockSpec((SB, E), lambda h, i: (jnp.minimum(i, last), h))
    prev = pl.BlockSpec((SB, E), lambda h, i: (jnp.maximum(i - 1, 0), h))

    def body(q_ref, kp_ref, kc_ref, vp_ref, vc_ref, z_ref, doa_ref, om_ref, lt_ref, sl_ref,
             dq_ref, dk_ref, dv_ref, dz_ref, dmix_s, dl_s, dq_s, dkp_s, dvp_s, dkc_s, dvc_s):
        i = pl.program_id(1)

        @pl.when(i == 0)
        def _():
            dkc_s[...] = jnp.zeros_like(dkc_s)
            dvc_s[...] = jnp.zeros_like(dvc_s)

        @pl.when(i < nsb)
        def _():
            z = z_ref[...]
            s = _sigmoid(z)
            doa = doa_ref[...]
            om = om_ref[...]
            dmix = doa * (z * s)
            dmix_s[...] = dmix
            dz_ref[...] = (doa * om * (s * (1.0 + z * (1.0 - s)))).astype(BF16)
            dl_s[...] = jnp.sum(dmix * om, axis=1, keepdims=True)
            dkp_s[...] = dkc_s[...]
            dvp_s[...] = dvc_s[...]
            dkc_s[...] = jnp.zeros_like(dkc_s)
            dvc_s[...] = jnp.zeros_like(dvc_s)
            dq_s[...] = jnp.zeros_like(dq_s)
            slope = sl_ref[0, 0:1, :]
            for p, d, base, pbase, from_prev in units:
                rows, prows = _attn_rows(base, d), _attn_rows(pbase, d)
                q = q_ref[rows, :].astype(BF16)
                kc = kc_ref[rows, :].astype(BF16)
                kp = (kp_ref if from_prev else kc_ref)[prows, :].astype(BF16)
                vp = (vp_ref if from_prev else vc_ref)[prows, :].astype(BF16)
                do = dmix_s[rows, :].astype(BF16)
                lt = lt_ref[rows, :]
                dlt = dl_s[rows, :]
                sp, sc, valid_p, valid_c = _unit_scores(q, kp, kc, slope, d, (i > 0) if from_prev else None, scale)
                pp = jnp.where(valid_p, jnp.exp(sp - lt), 0.0)
                pc = jnp.where(valid_c, jnp.exp(sc - lt), 0.0)
                dsp = (pp * (_dot(do, vp, NT) - dlt) * scale).astype(BF16)
                dsc = (pc * (_dot(do, vc_ref[rows, :].astype(BF16), NT) - dlt) * scale).astype(BF16)
                dq_s[rows, :] += _dot(dsp, kp, NN) + _dot(dsc, kc, NN)
                dkc_s[rows, :] += _dot(dsc, q, TN)
                dvc_s[rows, :] += _dot(pc.astype(BF16), do, TN)
                dk_t, dv_t = (dkp_s, dvp_s) if from_prev else (dkc_s, dvc_s)
                dk_t[prows, :] += _dot(dsp, q, TN)
                dv_t[prows, :] += _dot(pp.astype(BF16), do, TN)
            dq_ref[...] = dq_s[...].astype(BF16)
            dk_ref[...] = dkp_s[...].astype(BF16)
            dv_ref[...] = dvp_s[...].astype(BF16)

        @pl.when(i == nsb)
        def _():
            dk_ref[...] = dkc_s[...].astype(BF16)
            dv_ref[...] = dvc_s[...].astype(BF16)

    sh = jax.ShapeDtypeStruct((S, cfg.AW), BF16)
    acc = pltpu.VMEM((SB, E), F32)
    return pl.pallas_call(
        body, name=name, grid=(AH, nsb + 1),
        in_specs=[spec(0, False), spec(cfg.k0, True), spec(cfg.k0, False), spec(cfg.v0, True), spec(cfg.v0, False),
                  spec(cfg.za0, False), cur, cur,
                  pl.BlockSpec((SB, 1), lambda h, i: (h * nsb + jnp.minimum(i, last), 0)),
                  pl.BlockSpec((1, 8, LANES), lambda h, i: (h, 0, 0))],
        out_specs=[cur, prev, prev, cur], out_shape=[sh] * 4,
        scratch_shapes=[acc, pltpu.VMEM((SB, 1), F32), acc, acc, acc, acc, acc],
        compiler_params=_cp(("parallel", "arbitrary")),
    )(proj, proj, proj, proj, proj, proj, do_a, o_mix, ltot, slopes)


HALO = 8


def _conv_pre(ext_ref, x_ref, h_ref, w_ref, b_ref, tm, kc):
    first = pl.program_id(1) == 0
    ext_ref[0:HALO, :] = jnp.where(first, 0.0, h_ref[...])
    ext_ref[HALO:, :] = x_ref[...]
    pre = b_ref[...] + jnp.zeros_like(x_ref[...])
    for k in range(kc):
        pre = pre + w_ref[k:k + 1, :] * ext_ref[pl.ds(HALO - (kc - 1) + k, tm), :]
    return pre


def conv_fwd(proj, w, b, cfg, name, tm=512, tc=512):
    S, CD, KC = cfg.S, cfg.CD, cfg.KC
    tc = min(tc, CD)
    c0 = _blk(cfg.xbc0, tc)
    hb = tm // HALO

    def body(x_ref, h_ref, w_ref, b_ref, o_ref, ext_ref):
        pre = _conv_pre(ext_ref, x_ref, h_ref, w_ref, b_ref, tm, KC)
        o_ref[...] = pre * _sigmoid(pre)

    return pl.pallas_call(
        body, name=name, grid=(CD // tc, S // tm),
        in_specs=[pl.BlockSpec((tm, tc), lambda c, i: (i, c0 + c)),
                  pl.BlockSpec((HALO, tc), lambda c, i: (jnp.maximum(i * hb - 1, 0), c0 + c)),
                  pl.BlockSpec((KC, tc), lambda c, i: (0, c)),
                  pl.BlockSpec((1, tc), lambda c, i: (0, c))],
        out_specs=pl.BlockSpec((tm, tc), lambda c, i: (i, c)),
        out_shape=jax.ShapeDtypeStruct((S, CD), F32),
        scratch_shapes=[pltpu.VMEM((tm + HALO, tc), F32)],
        compiler_params=_cp(("parallel", "arbitrary")),
    )(proj, proj, w, b)


def conv_bwd_a(proj, dxc, w, b, cfg, name, tm=512, tc=512):
    S, CD, KC = cfg.S, cfg.CD, cfg.KC
    tc = min(tc, CD)
    c0 = _blk(cfg.xbc0, tc)
    hb = tm // HALO

    def body(x_ref, h_ref, d_ref, w_ref, b_ref, dp_ref, gw_ref, gb_ref, ext_ref):
        pre = _conv_pre(ext_ref, x_ref, h_ref, w_ref, b_ref, tm, KC)
        s = _sigmoid(pre)
        dpre = d_ref[...] * (s * (1.0 + pre * (1.0 - s)))
        dp_ref[...] = dpre
        gb = jnp.sum(dpre, axis=0, keepdims=True)
        gws = [jnp.sum(dpre * ext_ref[pl.ds(HALO - (KC - 1) + k, tm), :], axis=0, keepdims=True) for k in range(KC)]
        gw = jnp.concatenate(gws + [jnp.zeros((8 - KC, tc), F32)], axis=0)

        @pl.when(pl.program_id(1) == 0)
        def _():
            gw_ref[...] = gw
            gb_ref[...] = gb

        @pl.when(pl.program_id(1) > 0)
        def _():
            gw_ref[...] += gw
            gb_ref[...] += gb

    return pl.pallas_call(
        body, name=name, grid=(CD // tc, S // tm),
        in_specs=[pl.BlockSpec((tm, tc), lambda c, i: (i, c0 + c)),
                  pl.BlockSpec((HALO, tc), lambda c, i: (jnp.maximum(i * hb - 1, 0), c0 + c)),
                  pl.BlockSpec((tm, tc), lambda c, i: (i, c)),
                  pl.BlockSpec((KC, tc), lambda c, i: (0, c)),
                  pl.BlockSpec((1, tc), lambda c, i: (0, c))],
        out_specs=[pl.BlockSpec((tm, tc), lambda c, i: (i, c)),
                   pl.BlockSpec((8, tc), lambda c, i: (0, c)),
                   pl.BlockSpec((1, tc), lambda c, i: (0, c))],
        out_shape=[jax.ShapeDtypeStruct((S, CD), F32), jax.ShapeDtypeStruct((8, CD), F32),
                   jax.ShapeDtypeStruct((1, CD), F32)],
        scratch_shapes=[pltpu.VMEM((tm + HALO, tc), F32)],
        compiler_params=_cp(("parallel", "arbitrary")),
    )(proj, proj, dxc, w, b)


def conv_bwd_b(dpre, w, cfg, name, tm=512, tc=512):
    S, CD, KC = cfg.S, cfg.CD, cfg.KC
    tc = min(tc, CD)
    hb = tm // HALO
    nrb = S // tm
    last_h = S // HALO - 1

    def body(d_ref, h_ref, w_ref, o_ref, ext_ref):
        is_last = pl.program_id(1) == nrb - 1
        ext_ref[0:tm, :] = d_ref[...]
        ext_ref[tm:, :] = jnp.where(is_last, 0.0, h_ref[...])
        acc = w_ref[KC - 1:KC, :] * d_ref[...]
        for j in range(1, KC):
            acc = acc + w_ref[KC - 1 - j:KC - j, :] * ext_ref[pl.ds(j, tm), :]
        o_ref[...] = acc.astype(BF16)

    return pl.pallas_call(
        body, name=name, grid=(CD // tc, nrb),
        in_specs=[pl.BlockSpec((tm, tc), lambda c, i: (i, c)),
                  pl.BlockSpec((HALO, tc), lambda c, i: (jnp.minimum((i + 1) * hb, last_h), c)),
                  pl.BlockSpec((KC, tc), lambda c, i: (0, c))],
        out_specs=pl.BlockSpec((tm, tc), lambda c, i: (i, c)),
        out_shape=jax.ShapeDtypeStruct((S, CD), BF16),
        scratch_shapes=[pltpu.VMEM((tm + HALO, tc), F32)],
        compiler_params=_cp(("parallel", "arbitrary")),
    )(dpre, dpre, w)


def _pad_lanes(v, width=LANES):
    return jnp.pad(v, ((0, 0), (0, width - v.shape[1])))


def ssd_prep(dt_raw, dt_bias, a_log, cfg, name):
    S, L = cfg.S, cfg.L

    def body(x_ref, b_ref, al_ref, dt_ref, ac_ref):
        x = x_ref[...] + b_ref[...]
        dt = jnp.maximum(x, 0.0) + jnp.log(1.0 + jnp.exp(-jnp.abs(x)))
        da = dt * (-jnp.exp(al_ref[...]))
        li = lax.broadcasted_iota(jnp.int32, (L, L), 0)
        si = lax.broadcasted_iota(jnp.int32, (L, L), 1)
        tri = jnp.where(li >= si, 1.0, 0.0).astype(F32)
        dt_ref[...] = dt
        ac_ref[...] = lax.dot_general(tri, da, ((NN), ((), ())), precision=lax.Precision.HIGHEST,
                                      preferred_element_type=F32)

    row = pl.BlockSpec((L, LANES), lambda i: (i, 0))
    vec = pl.BlockSpec((1, LANES), lambda i: (0, 0))
    sh = jax.ShapeDtypeStruct((S, LANES), F32)
    return pl.pallas_call(
        body, name=name, grid=(S // L,), in_specs=[row, vec, vec], out_specs=[row, row], out_shape=[sh, sh],
        compiler_params=_cp(("parallel",)),
    )(dt_raw, dt_bias, a_log)


def _spread(v, n):
    return jnp.broadcast_to(v[:, :, None], v.shape + (n,)).reshape(v.shape[0], v.shape[1] * n)


def _from_group(v, cfg):
    return _pad_lanes(v.transpose(1, 0, 2).reshape(cfg.S, cfg.SH))


def ssd_scan_fwd(xc, dtb, acb, act, cfg, name):
    S, L, P, SN, HPG, SG, SI = cfg.S, cfg.L, cfg.P, cfg.SN, cfg.HPG, cfg.SG, cfg.SI
    nc = S // L
    GW = HPG * P
    bcol, ccol = _blk(SI, SN), _blk(SI + cfg.GN, SN)

    def body(xs_ref, b_ref, c_ref, dt_ref, ac_ref, at_ref, y_ref, st_ref, st):
        @pl.when(pl.program_id(1) == 0)
        def _():
            st[...] = jnp.zeros_like(st)

        st_ref[0] = st[...]
        B = b_ref[...].astype(BF16)
        C = c_ref[...].astype(BF16)
        G = _dot(C, B, NT)
        causal = lax.broadcasted_iota(jnp.int32, (L, L), 0) >= lax.broadcasted_iota(jnp.int32, (L, L), 1)
        for j in range(HPG):
            sl = slice(j * P, (j + 1) * P)
            a_col = ac_ref[:, j * LANES:(j + 1) * LANES]
            a_p = ac_ref[:, j * LANES:j * LANES + P]
            dm = jnp.where(causal, jnp.exp(a_col - at_ref[j:j + 1, :]), 0.0)
            xdt = xs_ref[:, sl] * dt_ref[:, sl]
            s0 = st[j]
            yd = _dot((G * dm).astype(BF16), xdt.astype(BF16), NN)
            yo = jnp.exp(a_p) * _dot(C, s0.astype(BF16), NT)
            y_ref[:, sl] = yd + yo
            ws = jnp.exp(a_p[L - 1:L, :] - a_p)
            st[j] = s0 * jnp.exp(a_col[L - 1:L, :]) + _dot((xdt * ws).astype(BF16), B, TN)

    y, states = pl.pallas_call(
        body, name=name, grid=(SG, nc),
        in_specs=[pl.BlockSpec((L, GW), lambda g, c: (c, g)),
                  pl.BlockSpec((L, SN), lambda g, c: (c, bcol + g)),
                  pl.BlockSpec((L, SN), lambda g, c: (c, ccol + g)),
                  pl.BlockSpec((L, GW), lambda g, c: (c, g)),
                  pl.BlockSpec((L, HPG * LANES), lambda g, c: (c, g)),
                  pl.BlockSpec((HPG, L), lambda g, c: (g, c))],
        out_specs=[pl.BlockSpec((L, GW), lambda g, c: (c, g)),
                   pl.BlockSpec((1, HPG, P, SN), lambda g, c: (c, g, 0, 0))],
        out_shape=[jax.ShapeDtypeStruct((S, SI), F32), jax.ShapeDtypeStruct((nc, cfg.SH, P, SN), F32)],
        scratch_shapes=[pltpu.VMEM((HPG, P, SN), F32)],
        compiler_params=_cp(("parallel", "arbitrary")),
    )(xc, xc, xc, dtb, acb, act)
    return y, states


def ssd_scan_bwd(xc, dtb, acb, act, states, y, dy, dvec, cfg, name, side):
    S, L, P, SN, HPG, SG, SI = cfg.S, cfg.L, cfg.P, cfg.SN, cfg.HPG, cfg.SG, cfg.SI
    nc = S // L
    GW = HPG * P
    bcol, ccol = _blk(SI, SN), _blk(SI + cfg.GN, SN)

    def rc(c):
        return nc - 1 - c

    def body(xs_ref, b_ref, c_ref, dt_ref, ac_ref, at_ref, st_ref, y_ref, dy_ref, dk_ref,
             dxs_ref, db_ref, dc_ref, dac_ref, dxsum_ref, dst):
        @pl.when(pl.program_id(1) == 0)
        def _():
            dst[...] = jnp.zeros_like(dst)

        B = b_ref[...].astype(BF16)
        C = c_ref[...].astype(BF16)
        G = _dot(C, B, NT)
        causal = lax.broadcasted_iota(jnp.int32, (L, L), 0) >= lax.broadcasted_iota(jnp.int32, (L, L), 1)
        is_last = lax.broadcasted_iota(jnp.int32, (L, 1), 0) == L - 1
        dgsum = jnp.zeros((L, L), F32)
        dc_acc = jnp.zeros((L, SN), F32)
        db_acc = jnp.zeros((L, SN), F32)
        for j in range(HPG):
            sl = slice(j * P, (j + 1) * P)
            a_col = ac_ref[:, j * LANES:(j + 1) * LANES]
            a_p = ac_ref[:, j * LANES:j * LANES + P]
            dm = jnp.where(causal, jnp.exp(a_col - at_ref[j:j + 1, :]), 0.0)
            xs = xs_ref[:, sl]
            dtp = dt_ref[:, sl]
            xdt = xs * dtp
            xdtb = xdt.astype(BF16)
            dY = dy_ref[:, sl]
            dYb = dY.astype(BF16)
            s0 = st_ref[0, j]
            s0b = s0.astype(BF16)
            ds1 = dst[j]
            ds1b = ds1.astype(BF16)
            ea_last = jnp.exp(a_col[L - 1:L, :])
            ws = jnp.exp(a_p[L - 1:L, :] - a_p)
            dR = (jnp.exp(a_p) * dY).astype(BF16)
            dgsum = dgsum + _dot(dYb, xdtb, NT) * dm
            Mb = (G * dm).astype(BF16)
            dX1 = _dot(Mb, dYb, TN)
            dX2 = ws * _dot(B, ds1b, NT)
            dX = dX1 + dX2
            pair = (dYb.astype(F32) - dY) * _dot(Mb, xdtb, NN) - xdtb.astype(F32) * dX1
            dc_acc = dc_acc + _dot(dR, s0b, NN)
            db_acc = db_acc + _dot((xdt * ws).astype(BF16), ds1b, NN)
            sc = jnp.sum(jnp.sum(ea_last * (ds1 * s0), axis=1, keepdims=True), axis=0, keepdims=True) + \
                jnp.sum(jnp.sum(xdt * dX2, axis=1, keepdims=True), axis=0, keepdims=True)
            dac_ref[0, :, j:j + 1] = jnp.sum(dY * y_ref[:, sl] + pair - xdt * dX2, axis=1, keepdims=True) + \
                jnp.where(is_last, sc, 0.0)
            dxsum_ref[0, :, j:j + 1] = jnp.sum(dX * xs, axis=1, keepdims=True)
            dxs_ref[:, sl] = dX * dtp + dk_ref[:, sl] * dY
            dst[j] = ea_last * ds1 + _dot(dR, C, TN)
        dgb = dgsum.astype(BF16)
        dc_ref[...] = dc_acc + _dot(dgb, B, NN)
        db_ref[...] = db_acc + _dot(dgb, C, TN)

    wide = pl.BlockSpec((L, GW), lambda g, c: (rc(c), g))
    colspec = pl.BlockSpec((1, L, HPG), lambda g, c: (g, rc(c), 0))
    whole = pl.BlockSpec(memory_space=pl.ANY)
    arrs, gathers = side
    grid = (SG, nc)
    res = pl.pallas_call(
        with_exchange(body, 10, 5, gathers, grid), name=name, grid=grid,
        in_specs=[wide,
                  pl.BlockSpec((L, SN), lambda g, c: (rc(c), bcol + g)),
                  pl.BlockSpec((L, SN), lambda g, c: (rc(c), ccol + g)),
                  wide,
                  pl.BlockSpec((L, HPG * LANES), lambda g, c: (rc(c), g)),
                  pl.BlockSpec((HPG, L), lambda g, c: (g, rc(c))),
                  pl.BlockSpec((1, HPG, P, SN), lambda g, c: (rc(c), g, 0, 0)),
                  wide, wide,
                  pl.BlockSpec((1, GW), lambda g, c: (0, g))] + [whole] * len(arrs),
        out_specs=[wide,
                   pl.BlockSpec((L, SN), lambda g, c: (rc(c), g)),
                   pl.BlockSpec((L, SN), lambda g, c: (rc(c), g)),
                   colspec, colspec] + [whole] * len(arrs),
        out_shape=[jax.ShapeDtypeStruct((S, SI), F32), jax.ShapeDtypeStruct((S, cfg.GN), F32),
                   jax.ShapeDtypeStruct((S, cfg.GN), F32),
                   jax.ShapeDtypeStruct((SG, S, HPG), F32), jax.ShapeDtypeStruct((SG, S, HPG), F32)] +
        _exchange_shapes(arrs, gathers),
        scratch_shapes=[pltpu.VMEM((HPG, P, SN), F32)] + _exchange_sems(len(arrs)),
        compiler_params=_cp(("arbitrary", "arbitrary")),
    )(xc, xc, xc, dtb, acb, act, states, y, dy, dvec, *arrs)
    return res[:5], res[5:]


def dt_bwd(dac, dxsum, dt_raw, dt, dt_bias, a_log, cfg, name):
    S, L = cfg.S, cfg.L

    def body(da_ref, dx_ref, x_ref, dt_ref, b_ref, al_ref, o_ref, gb_ref, ga_ref):
        a = -jnp.exp(al_ref[...])
        dtv = dt_ref[...]
        dxs = dx_ref[...]
        upper = jnp.where(lax.broadcasted_iota(jnp.int32, (L, L), 1) >= lax.broadcasted_iota(jnp.int32, (L, L), 0),
                          1.0, 0.0).astype(F32)
        dda = lax.dot_general(upper, da_ref[...], (NN, ((), ())), precision=lax.Precision.HIGHEST,
                              preferred_element_type=F32)
        draw = (dxs + dda * a) * _sigmoid(x_ref[...] + b_ref[...])
        o_ref[...] = draw.astype(BF16)
        gb = jnp.sum(draw, axis=0, keepdims=True)
        ga = jnp.sum(dda * dtv, axis=0, keepdims=True) * a

        @pl.when(pl.program_id(0) == 0)
        def _():
            gb_ref[...] = gb
            ga_ref[...] = ga

        @pl.when(pl.program_id(0) > 0)
        def _():
            gb_ref[...] += gb
            ga_ref[...] += ga

    row = pl.BlockSpec((L, LANES), lambda i: (i, 0))
    vec = pl.BlockSpec((1, LANES), lambda i: (0, 0))
    return pl.pallas_call(
        body, name=name, grid=(S // L,), in_specs=[row, row, row, row, vec, vec],
        out_specs=[row, vec, vec],
        out_shape=[jax.ShapeDtypeStruct((S, LANES), BF16), jax.ShapeDtypeStruct((1, LANES), F32),
                   jax.ShapeDtypeStruct((1, LANES), F32)],
        compiler_params=_cp(("arbitrary",)),
    )(dac, dxsum, dt_raw, dt, dt_bias, a_log)


def gated_norm_fwd(y, xc, proj, dvec, nw, cfg, name, tm=128):
    S, SI = cfg.S, cfg.SI

    def body(y_ref, xs_ref, z_ref, d_ref, w_ref, o_ref):
        z = z_ref[...]
        yg = (y_ref[...] + d_ref[...] * xs_ref[...]) * (z * _sigmoid(z))
        r = lax.rsqrt(jnp.mean(yg * yg, axis=-1, keepdims=True) + RMS_EPS)
        o_ref[...] = ((yg * r) * w_ref[...]).astype(BF16)

    row = pl.BlockSpec((tm, SI), lambda i: (i, 0))
    vec = pl.BlockSpec((1, SI), lambda i: (0, 0))
    return pl.pallas_call(
        body, name=name, grid=(S // tm,),
        in_specs=[row, row, pl.BlockSpec((tm, SI), lambda i: (i, _blk(cfg.zs0, SI))), vec, vec],
        out_specs=row, out_shape=jax.ShapeDtypeStruct((S, SI), BF16),
        compiler_params=_cp(("parallel",)),
    )(y, xc, proj, dvec, nw)


def gated_norm_bwd(dyn, y, xc, proj, dvec, nw, cfg, name, tm=128):
    S, SI = cfg.S, cfg.SI

    def body(dn_ref, y_ref, xs_ref, z_ref, d_ref, w_ref, dy_ref, dz_ref, gw_ref, gd_ref):
        z = z_ref[...]
        s = _sigmoid(z)
        sz = z * s
        xs = xs_ref[...]
        yf = y_ref[...] + d_ref[...] * xs
        yg = yf * sz
        r = lax.rsqrt(jnp.mean(yg * yg, axis=-1, keepdims=True) + RMS_EPS)
        dn = dn_ref[...]
        g = dn * w_ref[...]
        dyg = r * g - yg * (r * r * r) * jnp.mean(g * yg, axis=-1, keepdims=True)
        dy = dyg * sz
        dy_ref[...] = dy
        dz_ref[...] = (dyg * yf * (s * (1.0 + z * (1.0 - s)))).astype(BF16)
        gw = jnp.sum(dn * (yg * r), axis=0, keepdims=True)
        gd = jnp.sum(dy * xs, axis=0, keepdims=True)

        @pl.when(pl.program_id(0) == 0)
        def _():
            gw_ref[...] = gw
            gd_ref[...] = gd

        @pl.when(pl.program_id(0) > 0)
        def _():
            gw_ref[...] += gw
            gd_ref[...] += gd

    row = pl.BlockSpec((tm, SI), lambda i: (i, 0))
    vec = pl.BlockSpec((1, SI), lambda i: (0, 0))
    return pl.pallas_call(
        body, name=name, grid=(S // tm,),
        in_specs=[row, row, row, pl.BlockSpec((tm, SI), lambda i: (i, _blk(cfg.zs0, SI))), vec, vec],
        out_specs=[row, row, vec, vec],
        out_shape=[jax.ShapeDtypeStruct((S, SI), F32), jax.ShapeDtypeStruct((S, SI), BF16),
                   jax.ShapeDtypeStruct((1, SI), F32), jax.ShapeDtypeStruct((1, SI), F32)],
        compiler_params=_cp(("arbitrary",)),
    )(dyn, y, xc, proj, dvec, nw)


def local_step(cfg, x, tgt, norm_w, conv_w, conv_b, dt_bias, a_log, d_skip, ssm_norm_w, final_norm_w,
               w_main, w_dt, shards, dt0):
    S, D = cfg.S, cfg.D
    slopes = _slope_table(cfg)
    dt_bias_p = _pad_lanes(dt_bias)
    a_log_p = _pad_lanes(a_log)
    dvec = _spread(d_skip, cfg.P)

    hn = rmsnorm_fwd(x, norm_w, "rmsnorm_fwd")
    proj, gathered = matmul(hn, w_main, 'nn', 512, 1024, 2048, F32, "in_proj", side=(shards, [True] * 3))
    w_attn, w_ssm, w_out = gathered[0].reshape(cfg.AW, D), gathered[1].reshape(cfg.SI, D), gathered[2].reshape(D, D)
    dt_raw = matmul(hn, w_dt, 'nn', 512, 128, 2048, F32, "in_proj_dt")
    o_a, o_mix, ltot = attn_fused_fwd(proj, slopes, cfg, "attn_fwd")
    xc = conv_fwd(proj, conv_w, conv_b, cfg, "conv_fwd")
    dt, acum = ssd_prep(dt_raw, dt_bias_p, a_log_p, cfg, "ssd_prep")
    dtb = _spread(dt[:, :cfg.SH], cfg.P)
    acb = _spread(acum[:, :cfg.SH], LANES)
    act = acum[:, :cfg.SH].T
    y, states = ssd_scan_fwd(xc, dtb, acb, act, cfg, "ssd_scan_fwd")
    y_n = gated_norm_fwd(y, xc, proj, dvec, ssm_norm_w, cfg, "gated_norm_fwd")
    a_out = matmul(o_a, w_attn, 'nn', 512, 1024, 2048, F32, "attn_branch")
    s_out = matmul(y_n, w_ssm, 'nn', 512, 1024, 2048, F32, "ssm_branch")
    merged = merge_fwd(a_out, s_out, proj, cfg, "merge_fwd")
    res = matmul(merged, w_out, 'nn', 512, 1024, 2048, F32, "out_proj")
    dout, loss_p, g_final_w = final_fwd_bwd(x, res, final_norm_w.reshape(1, D), tgt, "final_fwd_bwd")

    g_w_out = matmul(merged, dout, 'tn', 1024, 1024, 512, F32, "g_w_out")
    dmerged = matmul(dout, w_out, 'nt', 512, 1024, 2048, F32, "d_merged")
    da_out, ds_out, dga, dgs = merge_bwd(dmerged, a_out, s_out, proj, cfg, "merge_bwd")
    g_w_attn = matmul(o_a, da_out, 'tn', 1024, 1024, 512, F32, "g_w_attn")
    g_w_ssm = matmul(y_n, ds_out, 'tn', 1024, 1024, 512, F32, "g_w_ssm")
    do_a = matmul(da_out, w_attn, 'nt', 512, 1024, 2048, F32, "d_o_a")
    dyn = matmul(ds_out, w_ssm, 'nt', 512, 1024, 2048, F32, "d_y_n")
    dy, dz_s, g_ssm_norm, g_dvec = gated_norm_bwd(dyn, y, xc, proj, dvec, ssm_norm_w, cfg, "gated_norm_bwd")
    sends = [g.reshape((N_DEV, g.shape[0] // N_DEV, D)).astype(BF16) for g in (g_w_attn, g_w_ssm, g_w_out)]
    (dxs, dB, dC, dac_g, dxsum_g), (r_attn, r_ssm, r_out) = ssd_scan_bwd(
        xc, dtb, acb, act, states, y, dy, dvec, cfg, "ssd_scan_bwd", side=(sends, [False] * 3))
    ddt_raw, g_dt_bias, g_a_log = dt_bwd(_from_group(dac_g, cfg), _from_group(dxsum_g, cfg), dt_raw, dt,
                                         dt_bias_p, a_log_p, cfg, "dt_bwd")
    dxc = jnp.concatenate([dxs, dB, dC], axis=1)
    dpre, g_conv_w, g_conv_b = conv_bwd_a(proj, dxc, conv_w, conv_b, cfg, "conv_bwd_a")
    dxbc = conv_bwd_b(dpre, conv_w, cfg, "conv_bwd_b")
    dq, dk, dv, dz_a = attn_fused_bwd(proj, do_a, o_mix, ltot, slopes, cfg, "attn_bwd")
    dproj = jnp.concatenate([dq, dk, dv, dz_a, dz_s, dxbc, dga, dgs], axis=1)
    g_w_main = matmul(hn, dproj, 'tn', 1024, 1024, 512, F32, "g_w_main")
    g_w_dt = matmul(hn, ddt_raw, 'tn', 1024, 128, 512, F32, "g_w_dt")
    g_w_in = jnp.concatenate([g_w_main[:, :dt0], g_w_dt[:, :cfg.SH], g_w_main[:, dt0:]], axis=1)
    send_in = g_w_in.reshape(D, N_DEV, cfg.N_IN // N_DEV).transpose(1, 0, 2).astype(BF16)
    dhn_a, (r_in,) = matmul(dproj, w_main, 'nt', 512, 1024, 2048, F32, "d_hn", side=([send_in], [False]))
    dhn_b = matmul(ddt_raw, w_dt, 'nt', 512, 1024, 128, F32, "d_hn_dt")
    grad_x, g_norm_w = rmsnorm_bwd(dhn_a, dhn_b, x, norm_w, dout, "rmsnorm_bwd")

    g_d_skip = jnp.sum(g_dvec.reshape(cfg.SH, cfg.P), axis=1).reshape(1, cfg.SH)
    small = dict(norm_w=g_norm_w, conv_b=g_conv_b, dt_bias=g_dt_bias[:, :cfg.SH], a_log=g_a_log[:, :cfg.SH],
                 d_skip=g_d_skip, ssm_norm_w=g_ssm_norm, final_norm_w=g_final_w, conv_w=g_conv_w[:cfg.KC])
    return loss_p, grad_x, small, dict(w_in=r_in, w_attn=r_attn, w_ssm=r_ssm, w_out=r_out)


def _mesh_pos():
    return lax.axis_index("x"), lax.axis_index("y"), lax.axis_index("c")


def _flat(pos):
    return 4 * pos[0] + 2 * pos[1] + pos[2]


def _exchange_shapes(arrs, gathers):
    return [jax.ShapeDtypeStruct(((N_DEV,) + a.shape) if g else a.shape, a.dtype) for a, g in zip(arrs, gathers)]


def _exchange_sems(n):
    return [pltpu.SemaphoreType.DMA((n * (N_DEV - 1),)), pltpu.SemaphoreType.DMA((n * (N_DEV - 1),)),
            pltpu.SemaphoreType.DMA((n,))]


def _exchange_copies(ins, outs, gathers, send_sems, recv_sems, loc_sems):
    pos = _mesh_pos()
    me = _flat(pos)
    starts, waits = [], []
    for a in range(len(ins)):
        mine = ins[a] if gathers[a] else ins[a].at[me]
        loc = pltpu.make_async_copy(mine, outs[a].at[me], loc_sems.at[a])
        starts.append(loc)
        waits.append(loc)
        for k in range(1, N_DEV):
            flip = ((k >> 2) & 1, (k >> 1) & 1, k & 1)
            peer = tuple(1 - p if f else p for p, f in zip(pos, flip))
            pk = _flat(peer)
            src = ins[a] if gathers[a] else ins[a].at[pk]
            sems = dict(send_sem=send_sems.at[a * (N_DEV - 1) + k - 1], recv_sem=recv_sems.at[a * (N_DEV - 1) + k - 1],
                        device_id=peer, device_id_type=pl.DeviceIdType.MESH)
            starts.append(pltpu.make_async_remote_copy(src_ref=src, dst_ref=outs[a].at[me], **sems))
            waits.append(pltpu.make_async_remote_copy(src_ref=src, dst_ref=outs[a].at[pk], **sems))
    return starts, waits


def exchange(arrs, gathers, name):
    n = len(arrs)

    def body(*refs):
        starts, waits = _exchange_copies(refs[:n], refs[n:2 * n], gathers, *refs[2 * n:])
        for cp in starts:
            cp.start()
        for cp in waits:
            cp.wait()

    hbm = pl.BlockSpec(memory_space=pltpu.HBM)
    return pl.pallas_call(
        body, name=name, in_specs=[hbm] * n, out_specs=[hbm] * n, out_shape=_exchange_shapes(arrs, gathers),
        scratch_shapes=_exchange_sems(n),
    )(*arrs)


def with_exchange(body, n_in, n_out, gathers, grid):
    n = len(gathers)

    def wrapped(*refs):
        ins, sends = refs[:n_in], refs[n_in:n_in + n]
        outs, recvs = refs[n_in + n:n_in + n + n_out], refs[n_in + 2 * n + n_out - n:n_in + 2 * n + n_out]
        scratch, sems = refs[n_in + 2 * n + n_out:-3], refs[-3:]
        ids = [pl.program_id(d) for d in range(len(grid))]
        first = functools.reduce(jnp.logical_and, [i == 0 for i in ids])
        last = functools.reduce(jnp.logical_and, [i == g - 1 for i, g in zip(ids, grid)])

        @pl.when(first)
        def _():
            for cp in _exchange_copies(sends, recvs, gathers, *sems)[0]:
                cp.start()

        body(*ins, *outs, *scratch)

        @pl.when(last)
        def _():
            for cp in _exchange_copies(sends, recvs, gathers, *sems)[1]:
                cp.wait()

    return wrapped


def adamw(g_src, w, m, v, summed, name, tr=64):
    R, C = w.shape
    tr = min(tr, R)
    assert R % tr == 0

    def body(g_ref, w_ref, m_ref, v_ref, g_out, d_out, m_out, v_out):
        if summed:
            g = g_ref[0].astype(F32)
            for j in range(1, N_DEV):
                g = g + g_ref[j].astype(F32)
        else:
            g = g_ref[...]
        mn = ADAM_B1 * m_ref[...] + (1.0 - ADAM_B1) * g
        vn = ADAM_B2 * v_ref[...] + (1.0 - ADAM_B2) * (g * g)
        m_hat = mn / (1.0 - ADAM_B1 ** ADAM_STEP)
        v_hat = vn / (1.0 - ADAM_B2 ** ADAM_STEP)
        g_out[...] = g
        d_out[...] = -ADAM_LR * (m_hat / (jnp.sqrt(v_hat) + ADAM_EPS) + ADAM_WD * w_ref[...])
        m_out[...] = mn
        v_out[...] = vn

    row = pl.BlockSpec((tr, C), lambda i: (i, 0))
    gspec = pl.BlockSpec((N_DEV, tr, C), lambda i: (0, i, 0)) if summed else row
    sh = jax.ShapeDtypeStruct((R, C), F32)
    return pl.pallas_call(
        body, name=name, grid=(R // tr,), in_specs=[gspec, row, row, row], out_specs=[row] * 4, out_shape=[sh] * 4,
        compiler_params=_cp(("parallel",)),
    )(g_src, w, m, v)


SMALL = ('norm_w', 'conv_b', 'dt_bias', 'a_log', 'd_skip', 'ssm_norm_w', 'final_norm_w')


def _rows(n):
    return -(-n // (8 * LANES)) * 8


def _pack(vals):
    parts = []
    for a in vals:
        f = a.reshape(-1)
        parts.append(jnp.pad(f, (0, _rows(f.size) * LANES - f.size)).reshape(-1, LANES))
    return jnp.concatenate(parts, axis=0)


def _unpack(packed, shapes):
    out, r = [], 0
    for s in shapes:
        n = math.prod(s)
        out.append(packed[r:r + _rows(n)].reshape(-1)[:n].reshape(s))
        r += _rows(n)
    return out


def kernel(x, norm_w, w_in, conv_w, conv_b, dt_bias, a_log, d_skip, ssm_norm_w, w_attn_branch, w_ssm_branch, w_out, final_norm_w, loss_target, m_norm_w, m_w_in, m_conv_w, m_conv_b, m_dt_bias, m_a_log, m_d_skip, m_ssm_norm_w, m_w_attn_branch, m_w_ssm_branch, m_w_out, m_final_norm_w, v_norm_w, v_w_in, v_conv_w, v_conv_b, v_dt_bias, v_a_log, v_d_skip, v_ssm_norm_w, v_w_attn_branch, v_w_ssm_branch, v_w_out, v_final_norm_w):
    cfg = CFG
    D, SH = cfg.D, cfg.SH
    me = _flat(_mesh_pos())
    dt0 = 4 * cfg.AW + cfg.SI + cfg.CD
    ws = w_in.shape[-1]

    g_in, g_cw = exchange([w_in[0].astype(BF16), conv_w[0]], [True] * 2, "gather_w_in")
    w_full = g_in.transpose(1, 0, 2).reshape(D, N_DEV * ws)
    w_main = jnp.concatenate([w_full[:, :dt0], w_full[:, dt0 + SH:]], axis=1)
    w_dt = _pad_lanes(w_full[:, dt0:dt0 + SH])
    conv_full = g_cw.transpose(1, 0, 2).reshape(cfg.KC, cfg.CD)
    shards = [w_attn_branch[0].astype(BF16), w_ssm_branch[0].astype(BF16), w_out[0].astype(BF16)]

    loss_p, grad_x, small, recv = local_step(
        cfg, x[0], loss_target[0], norm_w, conv_full, conv_b, dt_bias, a_log, d_skip,
        ssm_norm_w, final_norm_w, w_main, w_dt, shards, dt0)

    upd = {}
    upd['w_in'] = adamw(recv['w_in'], w_in[0], m_w_in[0], v_w_in[0], True, "adamw_w_in")
    upd['w_attn_branch'] = adamw(recv['w_attn'], w_attn_branch[0], m_w_attn_branch[0], v_w_attn_branch[0], True,
                                 "adamw_w_attn")
    upd['w_ssm_branch'] = adamw(recv['w_ssm'], w_ssm_branch[0], m_w_ssm_branch[0], v_w_ssm_branch[0], True,
                                "adamw_w_ssm")
    upd['w_out'] = adamw(recv['w_out'], w_out[0], m_w_out[0], v_w_out[0], True, "adamw_w_out")

    extra = [jnp.zeros((cfg.KC, cfg.CD), F32), jnp.zeros((1, 1), F32)]
    shapes = [small[n].shape for n in SMALL] + [e.shape for e in extra]
    part = _pack([small[n] for n in SMALL] + [small['conv_w'], loss_p[:, :1]])
    gathered, = exchange([part], [True], "gather_small")
    given = dict(norm_w=(norm_w, m_norm_w, v_norm_w), conv_b=(conv_b, m_conv_b, v_conv_b),
                 dt_bias=(dt_bias, m_dt_bias, v_dt_bias), a_log=(a_log, m_a_log, v_a_log),
                 d_skip=(d_skip, m_d_skip, v_d_skip), ssm_norm_w=(ssm_norm_w, m_ssm_norm_w, v_ssm_norm_w),
                 final_norm_w=(final_norm_w, m_final_norm_w, v_final_norm_w))
    packed = [_pack([given[n][t] for n in SMALL] + extra) for t in range(3)]
    outs = adamw(gathered, *packed, True, "adamw_small", tr=part.shape[0])
    unpacked = [_unpack(o, shapes) for o in outs]
    for i, n in enumerate(SMALL):
        upd[n] = [u[i].reshape(given[n][0].shape) for u in unpacked]
    loss = unpacked[0][-1].reshape(())
    cw = conv_w.shape[-1]
    g_cw_mine = lax.dynamic_slice_in_dim(unpacked[0][-2], me * cw, cw, axis=1)
    upd['conv_w'] = adamw(g_cw_mine.reshape(-1, LANES), conv_w.reshape(-1, LANES), m_conv_w.reshape(-1, LANES),
                          v_conv_w.reshape(-1, LANES), False, "adamw_conv_w")

    order = ['norm_w', 'w_in', 'conv_w', 'conv_b', 'dt_bias', 'a_log', 'd_skip', 'ssm_norm_w', 'w_attn_branch',
             'w_ssm_branch', 'w_out', 'final_norm_w']
    like = dict(norm_w=norm_w, w_in=w_in, conv_w=conv_w, conv_b=conv_b, dt_bias=dt_bias, a_log=a_log, d_skip=d_skip,
                ssm_norm_w=ssm_norm_w, w_attn_branch=w_attn_branch, w_ssm_branch=w_ssm_branch, w_out=w_out,
                final_norm_w=final_norm_w)
    result = [loss, grad_x[None]]
    for t in range(4):
        result += [upd[n][t].reshape(like[n].shape) for n in order]
    return tuple(result)
```

```python
import functools
import math
from typing import NamedTuple

import jax
import jax.numpy as jnp
from jax import lax
from jax.experimental import pallas as pl
from jax.experimental.pallas import tpu as pltpu

F32 = jnp.float32
BF16 = jnp.bfloat16
RMS_EPS = 1e-6
NEG = -1e30
N_DEV = 8
LANES = 128
ATTN_BLOCK = 128
ADAM_LR, ADAM_B1, ADAM_B2, ADAM_EPS, ADAM_WD, ADAM_STEP = 0.001, 0.9, 0.999, 1e-08, 0.01, 10
VMEM_LIMIT = 56 * 1024 * 1024


class Cfg(NamedTuple):
    D: int = 2048
    S: int = 8192
    AH: int = 16
    E: int = 128
    HB: int = 4
    patterns: tuple = ((128, 1), (512, 4), (2048, 16))
    SI: int = 4096
    P: int = 64
    SG: int = 8
    SN: int = 128
    KC: int = 4
    L: int = 128

    @property
    def AW(self): return self.AH * self.E
    @property
    def SH(self): return self.SI // self.P
    @property
    def HPG(self): return self.SH // self.SG
    @property
    def GN(self): return self.SG * self.SN
    @property
    def CD(self): return self.SI + 2 * self.GN
    @property
    def k0(self): return self.AW
    @property
    def v0(self): return 2 * self.AW
    @property
    def za0(self): return 3 * self.AW
    @property
    def zs0(self): return 4 * self.AW
    @property
    def xbc0(self): return 4 * self.AW + self.SI
    @property
    def ga0(self): return self.xbc0 + self.CD
    @property
    def gs0(self): return self.ga0 + self.D
    @property
    def NP(self): return self.gs0 + self.D
    @property
    def N_IN(self): return self.NP + self.SH


CFG = Cfg()


def _cp(sem=None, vmem=VMEM_LIMIT):
    return pltpu.CompilerParams(dimension_semantics=sem, vmem_limit_bytes=vmem)


def _sigmoid(z):
    return 1.0 / (1.0 + jnp.exp(-z))


def _dot(a, b, dims):
    return lax.dot_general(a, b, (dims, ((), ())), preferred_element_type=F32)


NN = ((1,), (0,))
NT = ((1,), (1,))
TN = ((0,), (0,))


def _blk(off, width):
    assert off % width == 0, (off, width)
    return off // width


def matmul(a, b, mode, tm, tn, tk, out_dtype, name, side=None):
    if mode == 'nn':
        (M, K), (_, N) = a.shape, b.shape
    elif mode == 'nt':
        (M, K), (N, _) = a.shape, b.shape
    else:
        (K, M), (_, N) = a.shape, b.shape
    tm, tn, tk = min(tm, M), min(tn, N), min(tk, K)
    assert M % tm == 0 and N % tn == 0 and K % tk == 0, (M, N, K, tm, tn, tk)
    nk = K // tk
    dims = {'nn': NN, 'nt': NT, 'tn': TN}[mode]

    def body(a_ref, b_ref, o_ref, *acc):
        part = _dot(a_ref[...].astype(BF16), b_ref[...].astype(BF16), dims)
        if nk == 1:
            o_ref[...] = part.astype(out_dtype)
        else:
            acc_ref, = acc
            k = pl.program_id(2)

            @pl.when(k == 0)
            def _():
                acc_ref[...] = part

            @pl.when(k > 0)
            def _():
                acc_ref[...] += part

            @pl.when(k == nk - 1)
            def _():
                o_ref[...] = acc_ref[...].astype(out_dtype)

    if mode == 'tn':
        a_spec = pl.BlockSpec((tk, tm), lambda n, m, k: (k, m))
    else:
        a_spec = pl.BlockSpec((tm, tk), lambda n, m, k: (m, k))
    if mode == 'nt':
        b_spec = pl.BlockSpec((tn, tk), lambda n, m, k: (n, k))
    else:
        b_spec = pl.BlockSpec((tk, tn), lambda n, m, k: (k, n))
    grid = (N // tn, M // tm, nk)
    o_spec = pl.BlockSpec((tm, tn), lambda n, m, k: (m, n))
    o_shape = jax.ShapeDtypeStruct((M, N), out_dtype)
    acc = [] if nk == 1 else [pltpu.VMEM((tm, tn), F32)]
    if side is None:
        return pl.pallas_call(
            body, name=name, grid=grid, in_specs=[a_spec, b_spec], out_specs=o_spec, out_shape=o_shape,
            scratch_shapes=acc, compiler_params=_cp(("parallel", "parallel", "arbitrary")),
        )(a, b)
    arrs, gathers = side
    whole = pl.BlockSpec(memory_space=pl.ANY)
    res = pl.pallas_call(
        with_exchange(body, 2, 1, gathers, grid), name=name, grid=grid,
        in_specs=[a_spec, b_spec] + [whole] * len(arrs), out_specs=[o_spec] + [whole] * len(arrs),
        out_shape=[o_shape] + _exchange_shapes(arrs, gathers),
        scratch_shapes=acc + _exchange_sems(len(arrs)),
        compiler_params=_cp(("arbitrary", "arbitrary", "arbitrary")),
    )(a, b, *arrs)
    return res[0], res[1:]


def rmsnorm_fwd(x, w, name, tm=256):
    S, D = x.shape

    def body(x_ref, w_ref, o_ref):
        xv = x_ref[...]
        r = lax.rsqrt(jnp.mean(xv * xv, axis=-1, keepdims=True) + RMS_EPS)
        o_ref[...] = ((xv * r) * w_ref[...]).astype(BF16)

    return pl.pallas_call(
        body, name=name, grid=(S // tm,),
        in_specs=[pl.BlockSpec((tm, D), lambda i: (i, 0)), pl.BlockSpec((1, D), lambda i: (0, 0))],
        out_specs=pl.BlockSpec((tm, D), lambda i: (i, 0)),
        out_shape=jax.ShapeDtypeStruct((S, D), BF16),
        compiler_params=_cp(("parallel",)),
    )(x, w)


def rmsnorm_bwd(dh_a, dh_b, x, w, dout, name, tm=128):
    S, D = x.shape

    def body(da_ref, db_ref, x_ref, w_ref, do_ref, gx_ref, gw_ref):
        xv = x_ref[...]
        dh = da_ref[...] + db_ref[...]
        r = lax.rsqrt(jnp.mean(xv * xv, axis=-1, keepdims=True) + RMS_EPS)
        g = dh * w_ref[...]
        dx = r * g - xv * (r * r * r) * jnp.mean(g * xv, axis=-1, keepdims=True)
        gx_ref[...] = do_ref[...] + dx
        gw = jnp.sum(dh * (xv * r), axis=0, keepdims=True)

        @pl.when(pl.program_id(0) == 0)
        def _():
            gw_ref[...] = gw

        @pl.when(pl.program_id(0) > 0)
        def _():
            gw_ref[...] += gw

    row = pl.BlockSpec((tm, D), lambda i: (i, 0))
    vec = pl.BlockSpec((1, D), lambda i: (0, 0))
    return pl.pallas_call(
        body, name=name, grid=(S // tm,),
        in_specs=[row, row, row, vec, row],
        out_specs=[row, vec],
        out_shape=[jax.ShapeDtypeStruct((S, D), F32), jax.ShapeDtypeStruct((1, D), F32)],
        compiler_params=_cp(("arbitrary",)),
    )(dh_a, dh_b, x, w, dout)


def final_fwd_bwd(x, res, fw, tgt, name, tm=128):
    S, D = x.shape

    def body(x_ref, r_ref, w_ref, t_ref, do_ref, loss_ref, gw_ref):
        out = x_ref[...] + r_ref[...]
        w = w_ref[...]
        r = lax.rsqrt(jnp.mean(out * out, axis=-1, keepdims=True) + RMS_EPS)
        yn = out * r
        err = yn * w - t_ref[...]
        lrow = 0.5 * jnp.mean(err * err, axis=-1, keepdims=True)
        lsum = jnp.zeros((1, LANES), F32) + jnp.sum(lrow, axis=0, keepdims=True)
        dfin = err * (1.0 / D)
        g = dfin * w
        do_ref[...] = r * g - out * (r * r * r) * jnp.mean(g * out, axis=-1, keepdims=True)
        gw = jnp.sum(dfin * yn, axis=0, keepdims=True)

        @pl.when(pl.program_id(0) == 0)
        def _():
            gw_ref[...] = gw
            loss_ref[...] = lsum

        @pl.when(pl.program_id(0) > 0)
        def _():
            gw_ref[...] += gw
            loss_ref[...] += lsum

    row = pl.BlockSpec((tm, D), lambda i: (i, 0))
    vec = pl.BlockSpec((1, D), lambda i: (0, 0))
    return pl.pallas_call(
        body, name=name, grid=(S // tm,),
        in_specs=[row, row, vec, row],
        out_specs=[row, pl.BlockSpec((1, LANES), lambda i: (0, 0)), vec],
        out_shape=[jax.ShapeDtypeStruct((S, D), F32), jax.ShapeDtypeStruct((1, LANES), F32),
                   jax.ShapeDtypeStruct((1, D), F32)],
        compiler_params=_cp(("arbitrary",)),
    )(x, res, fw, tgt)


def merge_fwd(a_out, s_out, proj, cfg, name, tm=256):
    S, D = a_out.shape

    def body(a_ref, s_ref, ga_ref, gs_ref, o_ref):
        o_ref[...] = (_sigmoid(ga_ref[...]) * a_ref[...] + _sigmoid(gs_ref[...]) * s_ref[...]).astype(BF16)

    row = pl.BlockSpec((tm, D), lambda i: (i, 0))
    return pl.pallas_call(
        body, name=name, grid=(S // tm,),
        in_specs=[row, row, pl.BlockSpec((tm, D), lambda i: (i, _blk(cfg.ga0, D))),
                  pl.BlockSpec((tm, D), lambda i: (i, _blk(cfg.gs0, D)))],
        out_specs=row, out_shape=jax.ShapeDtypeStruct((S, D), BF16),
        compiler_params=_cp(("parallel",)),
    )(a_out, s_out, proj, proj)


def merge_bwd(dm, a_out, s_out, proj, cfg, name, tm=128):
    S, D = dm.shape

    def body(dm_ref, a_ref, s_ref, ga_ref, gs_ref, da_ref, ds_ref, dga_ref, dgs_ref):
        dmv = dm_ref[...]
        sa = _sigmoid(ga_ref[...])
        ss = _sigmoid(gs_ref[...])
        da_ref[...] = (dmv * sa).astype(BF16)
        ds_ref[...] = (dmv * ss).astype(BF16)
        dga_ref[...] = (dmv * a_ref[...] * (sa * (1.0 - sa))).astype(BF16)
        dgs_ref[...] = (dmv * s_ref[...] * (ss * (1.0 - ss))).astype(BF16)

    row = pl.BlockSpec((tm, D), lambda i: (i, 0))
    sh = jax.ShapeDtypeStruct((S, D), BF16)
    return pl.pallas_call(
        body, name=name, grid=(S // tm,),
        in_specs=[row, row, row, pl.BlockSpec((tm, D), lambda i: (i, _blk(cfg.ga0, D))),
                  pl.BlockSpec((tm, D), lambda i: (i, _blk(cfg.gs0, D)))],
        out_specs=[row] * 4, out_shape=[sh] * 4,
        compiler_params=_cp(("parallel",)),
    )(dm, a_out, s_out, proj, proj)


def _attn_rows(base, d):
    return pl.ds(base, ATTN_BLOCK) if d == 1 else pl.ds(base, ATTN_BLOCK, stride=d)


def _attn_units(cfg):
    dmax = max(d for _, d in cfg.patterns)
    units = []
    for p, (window, d) in enumerate(cfg.patterns):
        assert window // d == ATTN_BLOCK and dmax % d == 0
        nsub = dmax // d
        for b in range(nsub):
            for r in range(d):
                base = b * ATTN_BLOCK * d + r
                if b > 0:
                    units.append((p, d, base, (b - 1) * ATTN_BLOCK * d + r, False))
                else:
                    units.append((p, d, base, (nsub - 1) * ATTN_BLOCK * d + r, True))
    return units, ATTN_BLOCK * dmax


def _unit_scores(q, kp, kc, slope, d, prev_ok, scale):
    qi = lax.broadcasted_iota(jnp.int32, (ATTN_BLOCK, ATTN_BLOCK), 0)
    ki = lax.broadcasted_iota(jnp.int32, (ATTN_BLOCK, ATTN_BLOCK), 1)
    valid_p = ki >= qi if prev_ok is None else jnp.logical_and(ki >= qi, prev_ok)
    valid_c = ki <= qi
    sp = _dot(q, kp, NT) * scale + (-slope) * ((ATTN_BLOCK + qi - ki) * d).astype(F32)
    sc = _dot(q, kc, NT) * scale + (-slope) * ((qi - ki) * d).astype(F32)
    return jnp.where(valid_p, sp, NEG), jnp.where(valid_c, sc, NEG), valid_p, valid_c


def _slope_table(cfg):
    slopes = jnp.asarray([2.0 ** (-8.0 * (h + 1) / cfg.AH) for h in range(cfg.AH)], F32)
    return jnp.broadcast_to(slopes.reshape(cfg.AH, 1, 1), (cfg.AH, 8, LANES))


def attn_fused_fwd(proj, slopes, cfg, name):
    S, E, AH = cfg.S, cfg.E, cfg.AH
    units, SB = _attn_units(cfg)
    assert S % SB == 0
    npat = len(cfg.patterns)
    scale = E ** -0.5

    def spec(off, prev):
        c0 = _blk(off, E)
        if prev:
            return pl.BlockSpec((SB, E), lambda h, i: (jnp.maximum(i - 1, 0), c0 + h))
        return pl.BlockSpec((SB, E), lambda h, i: (i, c0 + h))

    def body(q_ref, kp_ref, kc_ref, vp_ref, vc_ref, z_ref, sl_ref, oa_ref, om_ref, lt_ref, *scr):
        o_s, l_s = scr[:npat], scr[npat:]
        i = pl.program_id(1)
        slope = sl_ref[0, 0:1, :]
        for p, d, base, pbase, from_prev in units:
            rows, prows = _attn_rows(base, d), _attn_rows(pbase, d)
            q = q_ref[rows, :].astype(BF16)
            kp = (kp_ref if from_prev else kc_ref)[prows, :].astype(BF16)
            vp = (vp_ref if from_prev else vc_ref)[prows, :].astype(BF16)
            sp, sc, _, _ = _unit_scores(q, kp, kc_ref[rows, :].astype(BF16), slope, d, (i > 0) if from_prev else None,
                                        scale)
            m = jnp.maximum(jnp.max(sp, axis=1, keepdims=True), jnp.max(sc, axis=1, keepdims=True))
            pp = jnp.exp(sp - m)
            pc = jnp.exp(sc - m)
            l = jnp.sum(pp, axis=1, keepdims=True) + jnp.sum(pc, axis=1, keepdims=True)
            inv = 1.0 / l
            o_s[p][rows, :] = _dot((pp * inv).astype(BF16), vp, NN) + \
                _dot((pc * inv).astype(BF16), vc_ref[rows, :].astype(BF16), NN)
            l_s[p][rows, :] = m + jnp.log(l)
        ls = [l_s[p][...] for p in range(npat)]
        m = functools.reduce(jnp.maximum, ls)
        lt = m + jnp.log(sum(jnp.exp(l_ - m) for l_ in ls))
        lt_ref[...] = lt
        mix = sum(jnp.exp(ls[p] - lt) * o_s[p][...] for p in range(npat))
        om_ref[...] = mix
        z = z_ref[...]
        oa_ref[...] = (mix * (z * _sigmoid(z))).astype(BF16)

    out = pl.BlockSpec((SB, E), lambda h, i: (i, h))
    return pl.pallas_call(
        body, name=name, grid=(AH, S // SB),
        in_specs=[spec(0, False), spec(cfg.k0, True), spec(cfg.k0, False), spec(cfg.v0, True), spec(cfg.v0, False),
                  spec(cfg.za0, False), pl.BlockSpec((1, 8, LANES), lambda h, i: (h, 0, 0))],
        out_specs=[out, out, pl.BlockSpec((SB, 1), lambda h, i: (h * (S // SB) + i, 0))],
        out_shape=[jax.ShapeDtypeStruct((S, cfg.AW), BF16), jax.ShapeDtypeStruct((S, cfg.AW), F32),
                   jax.ShapeDtypeStruct((AH * S, 1), F32)],
        scratch_shapes=[pltpu.VMEM((SB, E), F32)] * npat + [pltpu.VMEM((SB, 1), F32)] * npat,
        compiler_params=_cp(("parallel", "arbitrary")),
    )(proj, proj, proj, proj, proj, proj, slopes)


def attn_fused_bwd(proj, do_a, o_mix, ltot, slopes, cfg, name):
    S, E, AH = cfg.S, cfg.E, cfg.AH
    units, SB = _attn_units(cfg)
    nsb = S // SB
    last = nsb - 1
    scale = E ** -0.5

    def spec(off, prev):
        c0 = _blk(off, E)
        if prev:
            return pl.BlockSpec((SB, E), lambda h, i: (jnp.maximum(i - 1, 0), c0 + h))
        return pl.BlockSpec((SB, E), lambda h, i: (jnp.minimum(i, last), c0 + h))

    cur = pl.BlockSpec((SB, E), lambda h, i: (jnp.minimum(i, last), h))
    prev = pl.BlockSpec((SB, E), lambda h, i: (jnp.maximum(i - 1, 0), h))

    def body(q_ref, kp_ref, kc_ref, vp_ref, vc_ref, z_ref, doa_ref, om_ref, lt_ref, sl_ref,
             dq_ref, dk_ref, dv_ref, dz_ref, dmix_s, dl_s, dq_s, dkp_s, dvp_s, dkc_s, dvc_s):
        i = pl.program_id(1)

        @pl.when(i == 0)
        def _():
            dkc_s[...] = jnp.zeros_like(dkc_s)
            dvc_s[...] = jnp.zeros_like(dvc_s)

        @pl.when(i < nsb)
        def _():
            z = z_ref[...]
            s = _sigmoid(z)
            doa = doa_ref[...]
            om = om_ref[...]
            dmix = doa * (z * s)
            dmix_s[...] = dmix
            dz_ref[...] = (doa * om * (s * (1.0 + z * (1.0 - s)))).astype(BF16)
            dl_s[...] = jnp.sum(dmix * om, axis=1, keepdims=True)
            dkp_s[...] = dkc_s[...]
            dvp_s[...] = dvc_s[...]
            dkc_s[...] = jnp.zeros_like(dkc_s)
            dvc_s[...] = jnp.zeros_like(dvc_s)
            dq_s[...] = jnp.zeros_like(dq_s)
            slope = sl_ref[0, 0:1, :]
            for p, d, base, pbase, from_prev in units:
                rows, prows = _attn_rows(base, d), _attn_rows(pbase, d)
                q = q_ref[rows, :].astype(BF16)
                kc = kc_ref[rows, :].astype(BF16)
                kp = (kp_ref if from_prev else kc_ref)[prows, :].astype(BF16)
                vp = (vp_ref if from_prev else vc_ref)[prows, :].astype(BF16)
                do = dmix_s[rows, :].astype(BF16)
                lt = lt_ref[rows, :]
                dlt = dl_s[rows, :]
                sp, sc, valid_p, valid_c = _unit_scores(q, kp, kc, slope, d, (i > 0) if from_prev else None, scale)
                pp = jnp.where(valid_p, jnp.exp(sp - lt), 0.0)
                pc = jnp.where(valid_c, jnp.exp(sc - lt), 0.0)
                dsp = (pp * (_dot(do, vp, NT) - dlt) * scale).astype(BF16)
                dsc = (pc * (_dot(do, vc_ref[rows, :].astype(BF16), NT) - dlt) * scale).astype(BF16)
                dq_s[rows, :] += _dot(dsp, kp, NN) + _dot(dsc, kc, NN)
                dkc_s[rows, :] += _dot(dsc, q, TN)
                dvc_s[rows, :] += _dot(pc.astype(BF16), do, TN)
                dk_t, dv_t = (dkp_s, dvp_s) if from_prev else (dkc_s, dvc_s)
                dk_t[prows, :] += _dot(dsp, q, TN)
                dv_t[prows, :] += _dot(pp.astype(BF16), do, TN)
            dq_ref[...] = dq_s[...].astype(BF16)
            dk_ref[...] = dkp_s[...].astype(BF16)
            dv_ref[...] = dvp_s[...].astype(BF16)

        @pl.when(i == nsb)
        def _():
            dk_ref[...] = dkc_s[...].astype(BF16)
            dv_ref[...] = dvc_s[...].astype(BF16)

    sh = jax.ShapeDtypeStruct((S, cfg.AW), BF16)
    acc = pltpu.VMEM((SB, E), F32)
    return pl.pallas_call(
        body, name=name, grid=(AH, nsb + 1),
        in_specs=[spec(0, False), spec(cfg.k0, True), spec(cfg.k0, False), spec(cfg.v0, True), spec(cfg.v0, False),
                  spec(cfg.za0, False), cur, cur,
                  pl.BlockSpec((SB, 1), lambda h, i: (h * nsb + jnp.minimum(i, last), 0)),
                  pl.BlockSpec((1, 8, LANES), lambda h, i: (h, 0, 0))],
        out_specs=[cur, prev, prev, cur], out_shape=[sh] * 4,
        scratch_shapes=[acc, pltpu.VMEM((SB, 1), F32), acc, acc, acc, acc, acc],
        compiler_params=_cp(("parallel", "arbitrary")),
    )(proj, proj, proj, proj, proj, proj, do_a, o_mix, ltot, slopes)


HALO = 8


def _conv_pre(ext_ref, x_ref, h_ref, w_ref, b_ref, tm, kc):
    first = pl.program_id(1) == 0
    ext_ref[0:HALO, :] = jnp.where(first, 0.0, h_ref[...])
    ext_ref[HALO:, :] = x_ref[...]
    pre = b_ref[...] + jnp.zeros_like(x_ref[...])
    for k in range(kc):
        pre = pre + w_ref[k:k + 1, :] * ext_ref[pl.ds(HALO - (kc - 1) + k, tm), :]
    return pre


def conv_fwd(proj, w, b, cfg, name, tm=512, tc=512):
    S, CD, KC = cfg.S, cfg.CD, cfg.KC
    tc = min(tc, CD)
    c0 = _blk(cfg.xbc0, tc)
    hb = tm // HALO

    def body(x_ref, h_ref, w_ref, b_ref, o_ref, ext_ref):
        pre = _conv_pre(ext_ref, x_ref, h_ref, w_ref, b_ref, tm, KC)
        o_ref[...] = pre * _sigmoid(pre)

    return pl.pallas_call(
        body, name=name, grid=(CD // tc, S // tm),
        in_specs=[pl.BlockSpec((tm, tc), lambda c, i: (i, c0 + c)),
                  pl.BlockSpec((HALO, tc), lambda c, i: (jnp.maximum(i * hb - 1, 0), c0 + c)),
                  pl.BlockSpec((KC, tc), lambda c, i: (0, c)),
                  pl.BlockSpec((1, tc), lambda c, i: (0, c))],
        out_specs=pl.BlockSpec((tm, tc), lambda c, i: (i, c)),
        out_shape=jax.ShapeDtypeStruct((S, CD), F32),
        scratch_shapes=[pltpu.VMEM((tm + HALO, tc), F32)],
        compiler_params=_cp(("parallel", "arbitrary")),
    )(proj, proj, w, b)


def conv_bwd_a(proj, dxc, w, b, cfg, name, c_off, tm=512, tc=512):
    S, KC = cfg.S, cfg.KC
    CD = dxc.shape[1]
    tc = min(tc, CD)
    c0 = _blk(cfg.xbc0 + c_off, tc)
    w0 = _blk(c_off, tc)
    hb = tm // HALO

    def body(x_ref, h_ref, d_ref, w_ref, b_ref, dp_ref, gw_ref, gb_ref, ext_ref):
        pre = _conv_pre(ext_ref, x_ref, h_ref, w_ref, b_ref, tm, KC)
        s = _sigmoid(pre)
        dpre = d_ref[...] * (s * (1.0 + pre * (1.0 - s)))
        dp_ref[...] = dpre
        gb = jnp.sum(dpre, axis=0, keepdims=True)
        gws = [jnp.sum(dpre * ext_ref[pl.ds(HALO - (KC - 1) + k, tm), :], axis=0, keepdims=True) for k in range(KC)]
        gw = jnp.concatenate(gws + [jnp.zeros((8 - KC, tc), F32)], axis=0)

        @pl.when(pl.program_id(1) == 0)
        def _():
            gw_ref[...] = gw
            gb_ref[...] = gb

        @pl.when(pl.program_id(1) > 0)
        def _():
            gw_ref[...] += gw
            gb_ref[...] += gb

    return pl.pallas_call(
        body, name=name, grid=(CD // tc, S // tm),
        in_specs=[pl.BlockSpec((tm, tc), lambda c, i: (i, c0 + c)),
                  pl.BlockSpec((HALO, tc), lambda c, i: (jnp.maximum(i * hb - 1, 0), c0 + c)),
                  pl.BlockSpec((tm, tc), lambda c, i: (i, c)),
                  pl.BlockSpec((KC, tc), lambda c, i: (0, w0 + c)),
                  pl.BlockSpec((1, tc), lambda c, i: (0, w0 + c))],
        out_specs=[pl.BlockSpec((tm, tc), lambda c, i: (i, c)),
                   pl.BlockSpec((8, tc), lambda c, i: (0, c)),
                   pl.BlockSpec((1, tc), lambda c, i: (0, c))],
        out_shape=[jax.ShapeDtypeStruct((S, CD), F32), jax.ShapeDtypeStruct((8, CD), F32),
                   jax.ShapeDtypeStruct((1, CD), F32)],
        scratch_shapes=[pltpu.VMEM((tm + HALO, tc), F32)],
        compiler_params=_cp(("parallel", "arbitrary")),
    )(proj, proj, dxc, w, b)


def conv_bwd_b(dpre, w, cfg, name, c_off, tm=512, tc=512):
    S, KC = cfg.S, cfg.KC
    CD = dpre.shape[1]
    tc = min(tc, CD)
    w0 = _blk(c_off, tc)
    hb = tm // HALO
    nrb = S // tm
    last_h = S // HALO - 1

    def body(d_ref, h_ref, w_ref, o_ref, ext_ref):
        is_last = pl.program_id(1) == nrb - 1
        ext_ref[0:tm, :] = d_ref[...]
        ext_ref[tm:, :] = jnp.where(is_last, 0.0, h_ref[...])
        acc = w_ref[KC - 1:KC, :] * d_ref[...]
        for j in range(1, KC):
            acc = acc + w_ref[KC - 1 - j:KC - j, :] * ext_ref[pl.ds(j, tm), :]
        o_ref[...] = acc.astype(BF16)

    return pl.pallas_call(
        body, name=name, grid=(CD // tc, nrb),
        in_specs=[pl.BlockSpec((tm, tc), lambda c, i: (i, c)),
                  pl.BlockSpec((HALO, tc), lambda c, i: (jnp.minimum((i + 1) * hb, last_h), c)),
                  pl.BlockSpec((KC, tc), lambda c, i: (0, w0 + c))],
        out_specs=pl.BlockSpec((tm, tc), lambda c, i: (i, c)),
        out_shape=jax.ShapeDtypeStruct((S, CD), BF16),
        scratch_shapes=[pltpu.VMEM((tm + HALO, tc), F32)],
        compiler_params=_cp(("parallel", "arbitrary")),
    )(dpre, dpre, w)


def _pad_lanes(v, width=LANES):
    return jnp.pad(v, ((0, 0), (0, width - v.shape[1])))


def ssd_prep(dt_raw, dt_bias, a_log, cfg, name):
    S, L = cfg.S, cfg.L

    def body(x_ref, b_ref, al_ref, dt_ref, ac_ref):
        x = x_ref[...] + b_ref[...]
        dt = jnp.maximum(x, 0.0) + jnp.log(1.0 + jnp.exp(-jnp.abs(x)))
        da = dt * (-jnp.exp(al_ref[...]))
        li = lax.broadcasted_iota(jnp.int32, (L, L), 0)
        si = lax.broadcasted_iota(jnp.int32, (L, L), 1)
        tri = jnp.where(li >= si, 1.0, 0.0).astype(F32)
        dt_ref[...] = dt
        ac_ref[...] = lax.dot_general(tri, da, ((NN), ((), ())), precision=lax.Precision.HIGHEST,
                                      preferred_element_type=F32)

    row = pl.BlockSpec((L, LANES), lambda i: (i, 0))
    vec = pl.BlockSpec((1, LANES), lambda i: (0, 0))
    sh = jax.ShapeDtypeStruct((S, LANES), F32)
    return pl.pallas_call(
        body, name=name, grid=(S // L,), in_specs=[row, vec, vec], out_specs=[row, row], out_shape=[sh, sh],
        compiler_params=_cp(("parallel",)),
    )(dt_raw, dt_bias, a_log)


def _spread(v, n):
    return jnp.broadcast_to(v[:, :, None], v.shape + (n,)).reshape(v.shape[0], v.shape[1] * n)


def _from_group(v, cfg):
    return _pad_lanes(v.transpose(1, 0, 2).reshape(cfg.S, cfg.SH))


def _head_selectors(cfg):
    def sel(width):
        head = jnp.arange(LANES)[None, :, None]
        slot = jnp.arange(cfg.SG)[:, None, None] * cfg.HPG + (jnp.arange(cfg.HPG * width) // width)[None, None, :]
        return (head == slot).astype(BF16)
    return sel(cfg.P), sel(LANES)


def _spread_heads(v, sel):
    hi = v.astype(BF16)
    r1 = v - hi.astype(F32)
    mid = r1.astype(BF16)
    lo = (r1 - mid.astype(F32)).astype(BF16)
    return _dot(hi, sel, NN) + _dot(mid, sel, NN) + _dot(lo, sel, NN)


def ssd_scan_fwd(xc, dt, acum, act, sel_p, sel_l, cfg, name):
    S, L, P, SN, HPG, SG, SI = cfg.S, cfg.L, cfg.P, cfg.SN, cfg.HPG, cfg.SG, cfg.SI
    nc = S // L
    GW = HPG * P
    bcol, ccol = _blk(SI, SN), _blk(SI + cfg.GN, SN)

    def body(xs_ref, b_ref, c_ref, dtn_ref, acn_ref, at_ref, sp_ref, sl_ref, y_ref, st_ref, st):
        @pl.when(pl.program_id(1) == 0)
        def _():
            st[...] = jnp.zeros_like(st)

        dts = _spread_heads(dtn_ref[...], sp_ref[0])
        acs = _spread_heads(acn_ref[...], sl_ref[0])
        st_ref[0] = st[...]
        B = b_ref[...].astype(BF16)
        C = c_ref[...].astype(BF16)
        G = _dot(C, B, NT)
        causal = lax.broadcasted_iota(jnp.int32, (L, L), 0) >= lax.broadcasted_iota(jnp.int32, (L, L), 1)
        for j in range(HPG):
            sl = slice(j * P, (j + 1) * P)
            a_col = acs[:, j * LANES:(j + 1) * LANES]
            a_p = acs[:, j * LANES:j * LANES + P]
            dm = jnp.where(causal, jnp.exp(a_col - at_ref[j:j + 1, :]), 0.0)
            xdt = xs_ref[:, sl] * dts[:, sl]
            s0 = st[j]
            yd = _dot((G * dm).astype(BF16), xdt.astype(BF16), NN)
            yo = jnp.exp(a_p) * _dot(C, s0.astype(BF16), NT)
            y_ref[:, sl] = yd + yo
            ws = jnp.exp(a_p[L - 1:L, :] - a_p)
            st[j] = s0 * jnp.exp(a_col[L - 1:L, :]) + _dot((xdt * ws).astype(BF16), B, TN)

    y, states = pl.pallas_call(
        body, name=name, grid=(SG, nc),
        in_specs=[pl.BlockSpec((L, GW), lambda g, c: (c, g)),
                  pl.BlockSpec((L, SN), lambda g, c: (c, bcol + g)),
                  pl.BlockSpec((L, SN), lambda g, c: (c, ccol + g)),
                  pl.BlockSpec((L, LANES), lambda g, c: (c, 0)),
                  pl.BlockSpec((L, LANES), lambda g, c: (c, 0)),
                  pl.BlockSpec((HPG, L), lambda g, c: (g, c)),
                  pl.BlockSpec((1, LANES, GW), lambda g, c: (g, 0, 0)),
                  pl.BlockSpec((1, LANES, HPG * LANES), lambda g, c: (g, 0, 0))],
        out_specs=[pl.BlockSpec((L, GW), lambda g, c: (c, g)),
                   pl.BlockSpec((1, HPG, P, SN), lambda g, c: (c, g, 0, 0))],
        out_shape=[jax.ShapeDtypeStruct((S, SI), F32), jax.ShapeDtypeStruct((nc, cfg.SH, P, SN), F32)],
        scratch_shapes=[pltpu.VMEM((HPG, P, SN), F32)],
        compiler_params=_cp(("parallel", "arbitrary")),
    )(xc, xc, xc, dt, acum, act, sel_p, sel_l)
    return y, states


def ssd_scan_bwd(xc, dt, acum, act, sel_p, sel_l, states, y, dy, dvec, cfg, name, side):
    S, L, P, SN, HPG, SG, SI = cfg.S, cfg.L, cfg.P, cfg.SN, cfg.HPG, cfg.SG, cfg.SI
    nc = S // L
    GW = HPG * P
    bcol, ccol = _blk(SI, SN), _blk(SI + cfg.GN, SN)

    def rc(c):
        return nc - 1 - c

    def body(xs_ref, b_ref, c_ref, dtn_ref, acn_ref, at_ref, sp_ref, sl_ref, st_ref, y_ref, dy_ref, dk_ref,
             dxs_ref, db_ref, dc_ref, dac_ref, dxsum_ref, dst):
        @pl.when(pl.program_id(1) == 0)
        def _():
            dst[...] = jnp.zeros_like(dst)

        dts = _spread_heads(dtn_ref[...], sp_ref[0])
        acs = _spread_heads(acn_ref[...], sl_ref[0])

        B = b_ref[...].astype(BF16)
        C = c_ref[...].astype(BF16)
        G = _dot(C, B, NT)
        causal = lax.broadcasted_iota(jnp.int32, (L, L), 0) >= lax.broadcasted_iota(jnp.int32, (L, L), 1)
        is_last = lax.broadcasted_iota(jnp.int32, (L, 1), 0) == L - 1
        dgsum = jnp.zeros((L, L), F32)
        dc_acc = jnp.zeros((L, SN), F32)
        db_acc = jnp.zeros((L, SN), F32)
        for j in range(HPG):
            sl = slice(j * P, (j + 1) * P)
            a_col = acs[:, j * LANES:(j + 1) * LANES]
            a_p = acs[:, j * LANES:j * LANES + P]
            dm = jnp.where(causal, jnp.exp(a_col - at_ref[j:j + 1, :]), 0.0)
            xs = xs_ref[:, sl]
            dtp = dts[:, sl]
            xdt = xs * dtp
            xdtb = xdt.astype(BF16)
            dY = dy_ref[:, sl]
            dYb = dY.astype(BF16)
            s0 = st_ref[0, j]
            s0b = s0.astype(BF16)
            ds1 = dst[j]
            ds1b = ds1.astype(BF16)
            ea_last = jnp.exp(a_col[L - 1:L, :])
            ws = jnp.exp(a_p[L - 1:L, :] - a_p)
            dR = (jnp.exp(a_p) * dY).astype(BF16)
            dgsum = dgsum + _dot(dYb, xdtb, NT) * dm
            Mb = (G * dm).astype(BF16)
            dX1 = _dot(Mb, dYb, TN)
            dX2 = ws * _dot(B, ds1b, NT)
            dX = dX1 + dX2
            pair = (dYb.astype(F32) - dY) * _dot(Mb, xdtb, NN) - xdtb.astype(F32) * dX1
            dc_acc = dc_acc + _dot(dR, s0b, NN)
            db_acc = db_acc + _dot((xdt * ws).astype(BF16), ds1b, NN)
            sc = jnp.sum(jnp.sum(ea_last * (ds1 * s0), axis=1, keepdims=True), axis=0, keepdims=True) + \
                jnp.sum(jnp.sum(xdt * dX2, axis=1, keepdims=True), axis=0, keepdims=True)
            dac_ref[0, :, j:j + 1] = jnp.sum(dY * y_ref[:, sl] + pair - xdt * dX2, axis=1, keepdims=True) + \
                jnp.where(is_last, sc, 0.0)
            dxsum_ref[0, :, j:j + 1] = jnp.sum(dX * xs, axis=1, keepdims=True)
            dxs_ref[:, sl] = dX * dtp + dk_ref[:, sl] * dY
            dst[j] = ea_last * ds1 + _dot(dR, C, TN)
        dgb = dgsum.astype(BF16)
        dc_ref[...] = dc_acc + _dot(dgb, B, NN)
        db_ref[...] = db_acc + _dot(dgb, C, TN)

    wide = pl.BlockSpec((L, GW), lambda g, c: (rc(c), g))
    colspec = pl.BlockSpec((1, L, HPG), lambda g, c: (g, rc(c), 0))
    whole = pl.BlockSpec(memory_space=pl.ANY)
    arrs, gathers = side
    grid = (SG, nc)
    res = pl.pallas_call(
        with_exchange(body, 12, 5, gathers, grid), name=name, grid=grid,
        in_specs=[wide,
                  pl.BlockSpec((L, SN), lambda g, c: (rc(c), bcol + g)),
                  pl.BlockSpec((L, SN), lambda g, c: (rc(c), ccol + g)),
                  pl.BlockSpec((L, LANES), lambda g, c: (rc(c), 0)),
                  pl.BlockSpec((L, LANES), lambda g, c: (rc(c), 0)),
                  pl.BlockSpec((HPG, L), lambda g, c: (g, rc(c))),
                  pl.BlockSpec((1, LANES, GW), lambda g, c: (g, 0, 0)),
                  pl.BlockSpec((1, LANES, HPG * LANES), lambda g, c: (g, 0, 0)),
                  pl.BlockSpec((1, HPG, P, SN), lambda g, c: (rc(c), g, 0, 0)),
                  wide, wide,
                  pl.BlockSpec((1, GW), lambda g, c: (0, g))] + [whole] * len(arrs),
        out_specs=[wide,
                   pl.BlockSpec((L, SN), lambda g, c: (rc(c), g)),
                   pl.BlockSpec((L, SN), lambda g, c: (rc(c), g)),
                   colspec, colspec] + [whole] * len(arrs),
        out_shape=[jax.ShapeDtypeStruct((S, SI), F32), jax.ShapeDtypeStruct((S, cfg.GN), F32),
                   jax.ShapeDtypeStruct((S, cfg.GN), F32),
                   jax.ShapeDtypeStruct((SG, S, HPG), F32), jax.ShapeDtypeStruct((SG, S, HPG), F32)] +
        _exchange_shapes(arrs, gathers),
        scratch_shapes=[pltpu.VMEM((HPG, P, SN), F32)] + _exchange_sems(len(arrs)),
        compiler_params=_cp(("arbitrary", "arbitrary")),
    )(xc, xc, xc, dt, acum, act, sel_p, sel_l, states, y, dy, dvec, *arrs)
    return res[:5], res[5:]


def dt_bwd(dac, dxsum, dt_raw, dt, dt_bias, a_log, cfg, name):
    S, L = cfg.S, cfg.L

    def body(da_ref, dx_ref, x_ref, dt_ref, b_ref, al_ref, o_ref, gb_ref, ga_ref):
        a = -jnp.exp(al_ref[...])
        dtv = dt_ref[...]
        dxs = dx_ref[...]
        upper = jnp.where(lax.broadcasted_iota(jnp.int32, (L, L), 1) >= lax.broadcasted_iota(jnp.int32, (L, L), 0),
                          1.0, 0.0).astype(F32)
        dda = lax.dot_general(upper, da_ref[...], (NN, ((), ())), precision=lax.Precision.HIGHEST,
                              preferred_element_type=F32)
        draw = (dxs + dda * a) * _sigmoid(x_ref[...] + b_ref[...])
        o_ref[...] = draw.astype(BF16)
        gb = jnp.sum(draw, axis=0, keepdims=True)
        ga = jnp.sum(dda * dtv, axis=0, keepdims=True) * a

        @pl.when(pl.program_id(0) == 0)
        def _():
            gb_ref[...] = gb
            ga_ref[...] = ga

        @pl.when(pl.program_id(0) > 0)
        def _():
            gb_ref[...] += gb
            ga_ref[...] += ga

    row = pl.BlockSpec((L, LANES), lambda i: (i, 0))
    vec = pl.BlockSpec((1, LANES), lambda i: (0, 0))
    return pl.pallas_call(
        body, name=name, grid=(S // L,), in_specs=[row, row, row, row, vec, vec],
        out_specs=[row, vec, vec],
        out_shape=[jax.ShapeDtypeStruct((S, LANES), BF16), jax.ShapeDtypeStruct((1, LANES), F32),
                   jax.ShapeDtypeStruct((1, LANES), F32)],
        compiler_params=_cp(("arbitrary",)),
    )(dac, dxsum, dt_raw, dt, dt_bias, a_log)


def gated_norm_fwd(y, xc, proj, dvec, nw, cfg, name, tm=128):
    S, SI = cfg.S, cfg.SI

    def body(y_ref, xs_ref, z_ref, d_ref, w_ref, o_ref):
        z = z_ref[...]
        yg = (y_ref[...] + d_ref[...] * xs_ref[...]) * (z * _sigmoid(z))
        r = lax.rsqrt(jnp.mean(yg * yg, axis=-1, keepdims=True) + RMS_EPS)
        o_ref[...] = ((yg * r) * w_ref[...]).astype(BF16)

    row = pl.BlockSpec((tm, SI), lambda i: (i, 0))
    vec = pl.BlockSpec((1, SI), lambda i: (0, 0))
    return pl.pallas_call(
        body, name=name, grid=(S // tm,),
        in_specs=[row, row, pl.BlockSpec((tm, SI), lambda i: (i, _blk(cfg.zs0, SI))), vec, vec],
        out_specs=row, out_shape=jax.ShapeDtypeStruct((S, SI), BF16),
        compiler_params=_cp(("parallel",)),
    )(y, xc, proj, dvec, nw)


def gated_norm_bwd(dyn, y, xc, proj, dvec, nw, cfg, name, tm=128):
    S, SI = cfg.S, cfg.SI

    def body(dn_ref, y_ref, xs_ref, z_ref, d_ref, w_ref, dy_ref, dz_ref, gw_ref, gd_ref):
        z = z_ref[...]
        s = _sigmoid(z)
        sz = z * s
        xs = xs_ref[...]
        yf = y_ref[...] + d_ref[...] * xs
        yg = yf * sz
        r = lax.rsqrt(jnp.mean(yg * yg, axis=-1, keepdims=True) + RMS_EPS)
        dn = dn_ref[...]
        g = dn * w_ref[...]
        dyg = r * g - yg * (r * r * r) * jnp.mean(g * yg, axis=-1, keepdims=True)
        dy = dyg * sz
        dy_ref[...] = dy
        dz_ref[...] = (dyg * yf * (s * (1.0 + z * (1.0 - s)))).astype(BF16)
        gw = jnp.sum(dn * (yg * r), axis=0, keepdims=True)
        gd = jnp.sum(dy * xs, axis=0, keepdims=True)

        @pl.when(pl.program_id(0) == 0)
        def _():
            gw_ref[...] = gw
            gd_ref[...] = gd

        @pl.when(pl.program_id(0) > 0)
        def _():
            gw_ref[...] += gw
            gd_ref[...] += gd

    row = pl.BlockSpec((tm, SI), lambda i: (i, 0))
    vec = pl.BlockSpec((1, SI), lambda i: (0, 0))
    return pl.pallas_call(
        body, name=name, grid=(S // tm,),
        in_specs=[row, row, row, pl.BlockSpec((tm, SI), lambda i: (i, _blk(cfg.zs0, SI))), vec, vec],
        out_specs=[row, row, vec, vec],
        out_shape=[jax.ShapeDtypeStruct((S, SI), F32), jax.ShapeDtypeStruct((S, SI), BF16),
                   jax.ShapeDtypeStruct((1, SI), F32), jax.ShapeDtypeStruct((1, SI), F32)],
        compiler_params=_cp(("arbitrary",)),
    )(dyn, y, xc, proj, dvec, nw)


def local_step(cfg, x, tgt, norm_w, conv_w, conv_b, dt_bias, a_log, d_skip, ssm_norm_w, final_norm_w,
               w_main, w_dt, shards, dt0):
    S, D = cfg.S, cfg.D
    slopes = _slope_table(cfg)
    dt_bias_p = _pad_lanes(dt_bias)
    a_log_p = _pad_lanes(a_log)
    dvec = _spread(d_skip, cfg.P)

    hn = rmsnorm_fwd(x, norm_w, "rmsnorm_fwd")
    proj, gathered = matmul(hn, w_main, 'nn', 512, 1024, 2048, F32, "in_proj", side=(shards, [True] * 3))
    w_attn, w_ssm, w_out = gathered[0].reshape(cfg.AW, D), gathered[1].reshape(cfg.SI, D), gathered[2].reshape(D, D)
    dt_raw = matmul(hn, w_dt, 'nn', 512, 128, 2048, F32, "in_proj_dt")
    o_a, o_mix, ltot = attn_fused_fwd(proj, slopes, cfg, "attn_fwd")
    xc = conv_fwd(proj, conv_w, conv_b, cfg, "conv_fwd")
    dt, acum = ssd_prep(dt_raw, dt_bias_p, a_log_p, cfg, "ssd_prep")
    act = acum[:, :cfg.SH].T
    sel_p, sel_l = _head_selectors(cfg)
    y, states = ssd_scan_fwd(xc, dt, acum, act, sel_p, sel_l, cfg, "ssd_scan_fwd")
    y_n = gated_norm_fwd(y, xc, proj, dvec, ssm_norm_w, cfg, "gated_norm_fwd")
    a_out = matmul(o_a, w_attn, 'nn', 512, 1024, 2048, F32, "attn_branch")
    s_out = matmul(y_n, w_ssm, 'nn', 512, 1024, 2048, F32, "ssm_branch")
    merged = merge_fwd(a_out, s_out, proj, cfg, "merge_fwd")
    res = matmul(merged, w_out, 'nn', 512, 1024, 2048, F32, "out_proj")
    dout, loss_p, g_final_w = final_fwd_bwd(x, res, final_norm_w.reshape(1, D), tgt, "final_fwd_bwd")

    g_w_out = matmul(merged, dout, 'tn', 1024, 1024, 2048, BF16, "g_w_out")
    dmerged = matmul(dout, w_out, 'nt', 512, 1024, 2048, F32, "d_merged")
    da_out, ds_out, dga, dgs = merge_bwd(dmerged, a_out, s_out, proj, cfg, "merge_bwd")
    g_w_attn = matmul(o_a, da_out, 'tn', 1024, 1024, 2048, BF16, "g_w_attn")
    g_w_ssm = matmul(y_n, ds_out, 'tn', 1024, 1024, 2048, BF16, "g_w_ssm")
    do_a = matmul(da_out, w_attn, 'nt', 512, 1024, 2048, F32, "d_o_a")
    dyn = matmul(ds_out, w_ssm, 'nt', 512, 1024, 2048, F32, "d_y_n")
    dy, dz_s, g_ssm_norm, g_dvec = gated_norm_bwd(dyn, y, xc, proj, dvec, ssm_norm_w, cfg, "gated_norm_bwd")
    sends = [g.reshape((N_DEV, g.shape[0] // N_DEV, D)) for g in (g_w_attn, g_w_ssm, g_w_out)]
    (dxs, dB, dC, dac_g, dxsum_g), (r_attn, r_ssm, r_out) = ssd_scan_bwd(
        xc, dt, acum, act, sel_p, sel_l, states, y, dy, dvec, cfg, "ssd_scan_bwd", side=(sends, [False] * 3))
    ddt_raw, g_dt_bias, g_a_log = dt_bwd(_from_group(dac_g, cfg), _from_group(dxsum_g, cfg), dt_raw, dt,
                                         dt_bias_p, a_log_p, cfg, "dt_bwd")
    dxbc, g_cw, g_cb = [], [], []
    for nm, piece, c_off in (("xs", dxs, 0), ("b", dB, cfg.SI), ("c", dC, cfg.SI + cfg.GN)):
        dpre, gw, gb = conv_bwd_a(proj, piece, conv_w, conv_b, cfg, "conv_bwd_a_" + nm, c_off)
        dxbc.append(conv_bwd_b(dpre, conv_w, cfg, "conv_bwd_b_" + nm, c_off))
        g_cw.append(gw)
        g_cb.append(gb)
    g_conv_w, g_conv_b = jnp.concatenate(g_cw, axis=1), jnp.concatenate(g_cb, axis=1)
    dq, dk, dv, dz_a = attn_fused_bwd(proj, do_a, o_mix, ltot, slopes, cfg, "attn_bwd")
    dproj = jnp.concatenate([dq, dk, dv, dz_a, dz_s] + dxbc + [dga, dgs], axis=1)
    g_w_main = matmul(hn, dproj, 'tn', 1024, 1024, 2048, BF16, "g_w_main")
    g_w_dt = matmul(hn, ddt_raw, 'tn', 1024, 128, 2048, BF16, "g_w_dt")
    g_w_in = jnp.concatenate([g_w_main[:, :dt0], g_w_dt[:, :cfg.SH], g_w_main[:, dt0:]], axis=1)
    send_in = g_w_in.reshape(D, N_DEV, cfg.N_IN // N_DEV).transpose(1, 0, 2)
    dhn_a, (r_in,) = matmul(dproj, w_main, 'nt', 512, 1024, 2048, F32, "d_hn", side=([send_in], [False]))
    dhn_b = matmul(ddt_raw, w_dt, 'nt', 512, 1024, 128, F32, "d_hn_dt")
    grad_x, g_norm_w = rmsnorm_bwd(dhn_a, dhn_b, x, norm_w, dout, "rmsnorm_bwd")

    g_d_skip = jnp.sum(g_dvec.reshape(cfg.SH, cfg.P), axis=1).reshape(1, cfg.SH)
    small = dict(norm_w=g_norm_w, conv_b=g_conv_b, dt_bias=g_dt_bias[:, :cfg.SH], a_log=g_a_log[:, :cfg.SH],
                 d_skip=g_d_skip, ssm_norm_w=g_ssm_norm, final_norm_w=g_final_w, conv_w=g_conv_w[:cfg.KC])
    return loss_p, grad_x, small, dict(w_in=r_in, w_attn=r_attn, w_ssm=r_ssm, w_out=r_out)


def _mesh_pos():
    return lax.axis_index("x"), lax.axis_index("y"), lax.axis_index("c")


def _flat(pos):
    return 4 * pos[0] + 2 * pos[1] + pos[2]


def _exchange_shapes(arrs, gathers):
    return [jax.ShapeDtypeStruct(((N_DEV,) + a.shape) if g else a.shape, a.dtype) for a, g in zip(arrs, gathers)]


def _exchange_sems(n):
    return [pltpu.SemaphoreType.DMA((n * (N_DEV - 1),)), pltpu.SemaphoreType.DMA((n * (N_DEV - 1),)),
            pltpu.SemaphoreType.DMA((n,))]


def _exchange_copies(ins, outs, gathers, send_sems, recv_sems, loc_sems):
    pos = _mesh_pos()
    me = _flat(pos)
    starts, waits = [], []
    for a in range(len(ins)):
        mine = ins[a] if gathers[a] else ins[a].at[me]
        loc = pltpu.make_async_copy(mine, outs[a].at[me], loc_sems.at[a])
        starts.append(loc)
        waits.append(loc)
        for k in range(1, N_DEV):
            flip = ((k >> 2) & 1, (k >> 1) & 1, k & 1)
            peer = tuple(1 - p if f else p for p, f in zip(pos, flip))
            pk = _flat(peer)
            src = ins[a] if gathers[a] else ins[a].at[pk]
            sems = dict(send_sem=send_sems.at[a * (N_DEV - 1) + k - 1], recv_sem=recv_sems.at[a * (N_DEV - 1) + k - 1],
                        device_id=peer, device_id_type=pl.DeviceIdType.MESH)
            starts.append(pltpu.make_async_remote_copy(src_ref=src, dst_ref=outs[a].at[me], **sems))
            waits.append(pltpu.make_async_remote_copy(src_ref=src, dst_ref=outs[a].at[pk], **sems))
    return starts, waits


def exchange(arrs, gathers, name):
    n = len(arrs)

    def body(*refs):
        starts, waits = _exchange_copies(refs[:n], refs[n:2 * n], gathers, *refs[2 * n:])
        for cp in starts:
            cp.start()
        for cp in waits:
            cp.wait()

    hbm = pl.BlockSpec(memory_space=pltpu.HBM)
    return pl.pallas_call(
        body, name=name, in_specs=[hbm] * n, out_specs=[hbm] * n, out_shape=_exchange_shapes(arrs, gathers),
        scratch_shapes=_exchange_sems(n),
    )(*arrs)


def with_exchange(body, n_in, n_out, gathers, grid):
    n = len(gathers)

    def wrapped(*refs):
        ins, sends = refs[:n_in], refs[n_in:n_in + n]
        outs, recvs = refs[n_in + n:n_in + n + n_out], refs[n_in + 2 * n + n_out - n:n_in + 2 * n + n_out]
        scratch, sems = refs[n_in + 2 * n + n_out:-3], refs[-3:]
        ids = [pl.program_id(d) for d in range(len(grid))]
        first = functools.reduce(jnp.logical_and, [i == 0 for i in ids])
        last = functools.reduce(jnp.logical_and, [i == g - 1 for i, g in zip(ids, grid)])

        @pl.when(first)
        def _():
            for cp in _exchange_copies(sends, recvs, gathers, *sems)[0]:
                cp.start()

        body(*ins, *outs, *scratch)

        @pl.when(last)
        def _():
            for cp in _exchange_copies(sends, recvs, gathers, *sems)[1]:
                cp.wait()

    return wrapped


def adamw(g_src, w, m, v, summed, name, tr=64):
    R, C = w.shape
    tr = min(tr, R)
    assert R % tr == 0

    def body(g_ref, w_ref, m_ref, v_ref, g_out, d_out, m_out, v_out):
        if summed:
            g = g_ref[0].astype(F32)
            for j in range(1, N_DEV):
                g = g + g_ref[j].astype(F32)
        else:
            g = g_ref[...]
        mn = ADAM_B1 * m_ref[...] + (1.0 - ADAM_B1) * g
        vn = ADAM_B2 * v_ref[...] + (1.0 - ADAM_B2) * (g * g)
        m_hat = mn / (1.0 - ADAM_B1 ** ADAM_STEP)
        v_hat = vn / (1.0 - ADAM_B2 ** ADAM_STEP)
        g_out[...] = g
        d_out[...] = -ADAM_LR * (m_hat / (jnp.sqrt(v_hat) + ADAM_EPS) + ADAM_WD * w_ref[...])
        m_out[...] = mn
        v_out[...] = vn

    row = pl.BlockSpec((tr, C), lambda i: (i, 0))
    gspec = pl.BlockSpec((N_DEV, tr, C), lambda i: (0, i, 0)) if summed else row
    sh = jax.ShapeDtypeStruct((R, C), F32)
    return pl.pallas_call(
        body, name=name, grid=(R // tr,), in_specs=[gspec, row, row, row], out_specs=[row] * 4, out_shape=[sh] * 4,
        compiler_params=_cp(("parallel",)),
    )(g_src, w, m, v)


SMALL = ('norm_w', 'conv_b', 'dt_bias', 'a_log', 'd_skip', 'ssm_norm_w', 'final_norm_w')


def _rows(n):
    return -(-n // (8 * LANES)) * 8


def _pack(vals):
    parts = []
    for a in vals:
        f = a.reshape(-1)
        parts.append(jnp.pad(f, (0, _rows(f.size) * LANES - f.size)).reshape(-1, LANES))
    return jnp.concatenate(parts, axis=0)


def _unpack(packed, shapes):
    out, r = [], 0
    for s in shapes:
        n = math.prod(s)
        out.append(packed[r:r + _rows(n)].reshape(-1)[:n].reshape(s))
        r += _rows(n)
    return out


def kernel(x, norm_w, w_in, conv_w, conv_b, dt_bias, a_log, d_skip, ssm_norm_w, w_attn_branch, w_ssm_branch, w_out, final_norm_w, loss_target, m_norm_w, m_w_in, m_conv_w, m_conv_b, m_dt_bias, m_a_log, m_d_skip, m_ssm_norm_w, m_w_attn_branch, m_w_ssm_branch, m_w_out, m_final_norm_w, v_norm_w, v_w_in, v_conv_w, v_conv_b, v_dt_bias, v_a_log, v_d_skip, v_ssm_norm_w, v_w_attn_branch, v_w_ssm_branch, v_w_out, v_final_norm_w):
    cfg = CFG
    D, SH = cfg.D, cfg.SH
    me = _flat(_mesh_pos())
    dt0 = 4 * cfg.AW + cfg.SI + cfg.CD
    ws = w_in.shape[-1]

    g_in, g_cw = exchange([w_in[0].astype(BF16), conv_w[0]], [True] * 2, "gather_w_in")
    w_full = g_in.transpose(1, 0, 2).reshape(D, N_DEV * ws)
    w_main = jnp.concatenate([w_full[:, :dt0], w_full[:, dt0 + SH:]], axis=1)
    w_dt = _pad_lanes(w_full[:, dt0:dt0 + SH])
    conv_full = g_cw.transpose(1, 0, 2).reshape(cfg.KC, cfg.CD)
    shards = [w_attn_branch[0].astype(BF16), w_ssm_branch[0].astype(BF16), w_out[0].astype(BF16)]

    loss_p, grad_x, small, recv = local_step(
        cfg, x[0], loss_target[0], norm_w, conv_full, conv_b, dt_bias, a_log, d_skip,
        ssm_norm_w, final_norm_w, w_main, w_dt, shards, dt0)

    upd = {}
    upd['w_in'] = adamw(recv['w_in'], w_in[0], m_w_in[0], v_w_in[0], True, "adamw_w_in")
    upd['w_attn_branch'] = adamw(recv['w_attn'], w_attn_branch[0], m_w_attn_branch[0], v_w_attn_branch[0], True,
                                 "adamw_w_attn")
    upd['w_ssm_branch'] = adamw(recv['w_ssm'], w_ssm_branch[0], m_w_ssm_branch[0], v_w_ssm_branch[0], True,
                                "adamw_w_ssm")
    upd['w_out'] = adamw(recv['w_out'], w_out[0], m_w_out[0], v_w_out[0], True, "adamw_w_out")

    extra = [jnp.zeros((cfg.KC, cfg.CD), F32), jnp.zeros((1, 1), F32)]
    shapes = [small[n].shape for n in SMALL] + [e.shape for e in extra]
    part = _pack([small[n] for n in SMALL] + [small['conv_w'], loss_p[:, :1]])
    gathered, = exchange([part], [True], "gather_small")
    given = dict(norm_w=(norm_w, m_norm_w, v_norm_w), conv_b=(conv_b, m_conv_b, v_conv_b),
                 dt_bias=(dt_bias, m_dt_bias, v_dt_bias), a_log=(a_log, m_a_log, v_a_log),
                 d_skip=(d_skip, m_d_skip, v_d_skip), ssm_norm_w=(ssm_norm_w, m_ssm_norm_w, v_ssm_norm_w),
                 final_norm_w=(final_norm_w, m_final_norm_w, v_final_norm_w))
    packed = [_pack([given[n][t] for n in SMALL] + extra) for t in range(3)]
    outs = adamw(gathered, *packed, True, "adamw_small", tr=part.shape[0])
    unpacked = [_unpack(o, shapes) for o in outs]
    for i, n in enumerate(SMALL):
        upd[n] = [u[i].reshape(given[n][0].shape) for u in unpacked]
    loss = unpacked[0][-1].reshape(())
    cw = conv_w.shape[-1]
    g_cw_mine = lax.dynamic_slice_in_dim(unpacked[0][-2], me * cw, cw, axis=1)
    upd['conv_w'] = adamw(g_cw_mine.reshape(-1, LANES), conv_w.reshape(-1, LANES), m_conv_w.reshape(-1, LANES),
                          v_conv_w.reshape(-1, LANES), False, "adamw_conv_w")

    order = ['norm_w', 'w_in', 'conv_w', 'conv_b', 'dt_bias', 'a_log', 'd_skip', 'ssm_norm_w', 'w_attn_branch',
             'w_ssm_branch', 'w_out', 'final_norm_w']
    like = dict(norm_w=norm_w, w_in=w_in, conv_w=conv_w, conv_b=conv_b, dt_bias=dt_bias, a_log=a_log, d_skip=d_skip,
                ssm_norm_w=ssm_norm_w, w_attn_branch=w_attn_branch, w_ssm_branch=w_ssm_branch, w_out=w_out,
                final_norm_w=final_norm_w)
    result = [loss, grad_x[None]]
    for t in range(4):
        result += [upd[n][t].reshape(like[n].shape) for n in order]
    return tuple(result)
```

```python
import functools
import math
from typing import NamedTuple

import jax
import jax.numpy as jnp
from jax import lax
from jax.experimental import pallas as pl
from jax.experimental.pallas import tpu as pltpu

F32 = jnp.float32
BF16 = jnp.bfloat16
RMS_EPS = 1e-6
NEG = -1e30
N_DEV = 8
LANES = 128
ATTN_BLOCK = 128
ADAM_LR, ADAM_B1, ADAM_B2, ADAM_EPS, ADAM_WD, ADAM_STEP = 0.001, 0.9, 0.999, 1e-08, 0.01, 10
VMEM_LIMIT = 56 * 1024 * 1024


class Cfg(NamedTuple):
    D: int = 2048
    S: int = 8192
    AH: int = 16
    E: int = 128
    HB: int = 4
    patterns: tuple = ((128, 1), (512, 4), (2048, 16))
    SI: int = 4096
    P: int = 64
    SG: int = 8
    SN: int = 128
    KC: int = 4
    L: int = 128

    @property
    def AW(self): return self.AH * self.E
    @property
    def SH(self): return self.SI // self.P
    @property
    def HPG(self): return self.SH // self.SG
    @property
    def GN(self): return self.SG * self.SN
    @property
    def CD(self): return self.SI + 2 * self.GN
    @property
    def k0(self): return self.AW
    @property
    def v0(self): return 2 * self.AW
    @property
    def za0(self): return 3 * self.AW
    @property
    def zs0(self): return 4 * self.AW
    @property
    def xbc0(self): return 4 * self.AW + self.SI
    @property
    def ga0(self): return self.xbc0 + self.CD
    @property
    def gs0(self): return self.ga0 + self.D
    @property
    def NP(self): return self.gs0 + self.D
    @property
    def N_IN(self): return self.NP + self.SH


CFG = Cfg()


def _cp(sem=None, vmem=VMEM_LIMIT):
    return pltpu.CompilerParams(dimension_semantics=sem, vmem_limit_bytes=vmem)


def _sigmoid(z):
    return 1.0 / (1.0 + jnp.exp(-z))


def _dot(a, b, dims):
    return lax.dot_general(a, b, (dims, ((), ())), preferred_element_type=F32)


NN = ((1,), (0,))
NT = ((1,), (1,))
TN = ((0,), (0,))


def _blk(off, width):
    assert off % width == 0, (off, width)
    return off // width


def matmul(a, b, mode, tm, tn, tk, out_dtype, name, side=None):
    if mode == 'nn':
        (M, K), (_, N) = a.shape, b.shape
    elif mode == 'nt':
        (M, K), (N, _) = a.shape, b.shape
    else:
        (K, M), (_, N) = a.shape, b.shape
    tm, tn, tk = min(tm, M), min(tn, N), min(tk, K)
    assert M % tm == 0 and N % tn == 0 and K % tk == 0, (M, N, K, tm, tn, tk)
    nk = K // tk
    dims = {'nn': NN, 'nt': NT, 'tn': TN}[mode]

    def body(a_ref, b_ref, o_ref, *acc):
        part = _dot(a_ref[...].astype(BF16), b_ref[...].astype(BF16), dims)
        if nk == 1:
            o_ref[...] = part.astype(out_dtype)
        else:
            acc_ref, = acc
            k = pl.program_id(2)

            @pl.when(k == 0)
            def _():
                acc_ref[...] = part

            @pl.when(k > 0)
            def _():
                acc_ref[...] += part

            @pl.when(k == nk - 1)
            def _():
                o_ref[...] = acc_ref[...].astype(out_dtype)

    if mode == 'tn':
        a_spec = pl.BlockSpec((tk, tm), lambda n, m, k: (k, m))
    else:
        a_spec = pl.BlockSpec((tm, tk), lambda n, m, k: (m, k))
    if mode == 'nt':
        b_spec = pl.BlockSpec((tn, tk), lambda n, m, k: (n, k))
    else:
        b_spec = pl.BlockSpec((tk, tn), lambda n, m, k: (k, n))
    grid = (N // tn, M // tm, nk)
    o_spec = pl.BlockSpec((tm, tn), lambda n, m, k: (m, n))
    o_shape = jax.ShapeDtypeStruct((M, N), out_dtype)
    acc = [] if nk == 1 else [pltpu.VMEM((tm, tn), F32)]
    if side is None:
        return pl.pallas_call(
            body, name=name, grid=grid, in_specs=[a_spec, b_spec], out_specs=o_spec, out_shape=o_shape,
            scratch_shapes=acc, compiler_params=_cp(("parallel", "parallel", "arbitrary")),
        )(a, b)
    arrs, gathers = side
    whole = pl.BlockSpec(memory_space=pl.ANY)
    res = pl.pallas_call(
        with_exchange(body, 2, 1, gathers, grid), name=name, grid=grid,
        in_specs=[a_spec, b_spec] + [whole] * len(arrs), out_specs=[o_spec] + [whole] * len(arrs),
        out_shape=[o_shape] + _exchange_shapes(arrs, gathers),
        scratch_shapes=acc + _exchange_sems(len(arrs)),
        compiler_params=_cp(("arbitrary", "arbitrary", "arbitrary")),
    )(a, b, *arrs)
    return res[0], res[1:]


def rmsnorm_fwd(x, w, name, tm=256):
    S, D = x.shape

    def body(x_ref, w_ref, o_ref):
        xv = x_ref[...]
        r = lax.rsqrt(jnp.mean(xv * xv, axis=-1, keepdims=True) + RMS_EPS)
        o_ref[...] = ((xv * r) * w_ref[...]).astype(BF16)

    return pl.pallas_call(
        body, name=name, grid=(S // tm,),
        in_specs=[pl.BlockSpec((tm, D), lambda i: (i, 0)), pl.BlockSpec((1, D), lambda i: (0, 0))],
        out_specs=pl.BlockSpec((tm, D), lambda i: (i, 0)),
        out_shape=jax.ShapeDtypeStruct((S, D), BF16),
        compiler_params=_cp(("parallel",)),
    )(x, w)


def rmsnorm_bwd(dh_a, dh_b, x, w, dout, name, tm=128):
    S, D = x.shape

    def body(da_ref, db_ref, x_ref, w_ref, do_ref, gx_ref, gw_ref):
        xv = x_ref[...]
        dh = da_ref[...] + db_ref[...]
        r = lax.rsqrt(jnp.mean(xv * xv, axis=-1, keepdims=True) + RMS_EPS)
        g = dh * w_ref[...]
        dx = r * g - xv * (r * r * r) * jnp.mean(g * xv, axis=-1, keepdims=True)
        gx_ref[...] = do_ref[...] + dx
        gw = jnp.sum(dh * (xv * r), axis=0, keepdims=True)

        @pl.when(pl.program_id(0) == 0)
        def _():
            gw_ref[...] = gw

        @pl.when(pl.program_id(0) > 0)
        def _():
            gw_ref[...] += gw

    row = pl.BlockSpec((tm, D), lambda i: (i, 0))
    vec = pl.BlockSpec((1, D), lambda i: (0, 0))
    return pl.pallas_call(
        body, name=name, grid=(S // tm,),
        in_specs=[row, row, row, vec, row],
        out_specs=[row, vec],
        out_shape=[jax.ShapeDtypeStruct((S, D), F32), jax.ShapeDtypeStruct((1, D), F32)],
        compiler_params=_cp(("arbitrary",)),
    )(dh_a, dh_b, x, w, dout)


def final_fwd_bwd(x, res, fw, tgt, name, tm=128):
    S, D = x.shape

    def body(x_ref, r_ref, w_ref, t_ref, do_ref, loss_ref, gw_ref):
        out = x_ref[...] + r_ref[...]
        w = w_ref[...]
        r = lax.rsqrt(jnp.mean(out * out, axis=-1, keepdims=True) + RMS_EPS)
        yn = out * r
        err = yn * w - t_ref[...]
        lrow = 0.5 * jnp.mean(err * err, axis=-1, keepdims=True)
        lsum = jnp.zeros((1, LANES), F32) + jnp.sum(lrow, axis=0, keepdims=True)
        dfin = err * (1.0 / D)
        g = dfin * w
        do_ref[...] = r * g - out * (r * r * r) * jnp.mean(g * out, axis=-1, keepdims=True)
        gw = jnp.sum(dfin * yn, axis=0, keepdims=True)

        @pl.when(pl.program_id(0) == 0)
        def _():
            gw_ref[...] = gw
            loss_ref[...] = lsum

        @pl.when(pl.program_id(0) > 0)
        def _():
            gw_ref[...] += gw
            loss_ref[...] += lsum

    row = pl.BlockSpec((tm, D), lambda i: (i, 0))
    vec = pl.BlockSpec((1, D), lambda i: (0, 0))
    return pl.pallas_call(
        body, name=name, grid=(S // tm,),
        in_specs=[row, row, vec, row],
        out_specs=[row, pl.BlockSpec((1, LANES), lambda i: (0, 0)), vec],
        out_shape=[jax.ShapeDtypeStruct((S, D), F32), jax.ShapeDtypeStruct((1, LANES), F32),
                   jax.ShapeDtypeStruct((1, D), F32)],
        compiler_params=_cp(("arbitrary",)),
    )(x, res, fw, tgt)


def merge_fwd(a_out, s_out, proj, cfg, name, tm=256):
    S, D = a_out.shape

    def body(a_ref, s_ref, ga_ref, gs_ref, o_ref):
        o_ref[...] = (_sigmoid(ga_ref[...]) * a_ref[...] + _sigmoid(gs_ref[...]) * s_ref[...]).astype(BF16)

    row = pl.BlockSpec((tm, D), lambda i: (i, 0))
    return pl.pallas_call(
        body, name=name, grid=(S // tm,),
        in_specs=[row, row, pl.BlockSpec((tm, D), lambda i: (i, _blk(cfg.ga0, D))),
                  pl.BlockSpec((tm, D), lambda i: (i, _blk(cfg.gs0, D)))],
        out_specs=row, out_shape=jax.ShapeDtypeStruct((S, D), BF16),
        compiler_params=_cp(("parallel",)),
    )(a_out, s_out, proj, proj)


def merge_bwd(dm, a_out, s_out, proj, cfg, name, tm=128):
    S, D = dm.shape

    def body(dm_ref, a_ref, s_ref, ga_ref, gs_ref, da_ref, ds_ref, dga_ref, dgs_ref):
        dmv = dm_ref[...]
        sa = _sigmoid(ga_ref[...])
        ss = _sigmoid(gs_ref[...])
        da_ref[...] = (dmv * sa).astype(BF16)
        ds_ref[...] = (dmv * ss).astype(BF16)
        dga_ref[...] = (dmv * a_ref[...] * (sa * (1.0 - sa))).astype(BF16)
        dgs_ref[...] = (dmv * s_ref[...] * (ss * (1.0 - ss))).astype(BF16)

    row = pl.BlockSpec((tm, D), lambda i: (i, 0))
    sh = jax.ShapeDtypeStruct((S, D), BF16)
    return pl.pallas_call(
        body, name=name, grid=(S // tm,),
        in_specs=[row, row, row, pl.BlockSpec((tm, D), lambda i: (i, _blk(cfg.ga0, D))),
                  pl.BlockSpec((tm, D), lambda i: (i, _blk(cfg.gs0, D)))],
        out_specs=[row] * 4, out_shape=[sh] * 4,
        compiler_params=_cp(("parallel",)),
    )(dm, a_out, s_out, proj, proj)


def _attn_rows(base, d):
    return pl.ds(base, ATTN_BLOCK) if d == 1 else pl.ds(base, ATTN_BLOCK, stride=d)


def _attn_units(cfg):
    dmax = max(d for _, d in cfg.patterns)
    units = []
    for p, (window, d) in enumerate(cfg.patterns):
        assert window // d == ATTN_BLOCK and dmax % d == 0
        nsub = dmax // d
        for b in range(nsub):
            for r in range(d):
                base = b * ATTN_BLOCK * d + r
                if b > 0:
                    units.append((p, d, base, (b - 1) * ATTN_BLOCK * d + r, False))
                else:
                    units.append((p, d, base, (nsub - 1) * ATTN_BLOCK * d + r, True))
    return units, ATTN_BLOCK * dmax


def _unit_scores(q, kp, kc, slope, d, prev_ok, scale):
    qi = lax.broadcasted_iota(jnp.int32, (ATTN_BLOCK, ATTN_BLOCK), 0)
    ki = lax.broadcasted_iota(jnp.int32, (ATTN_BLOCK, ATTN_BLOCK), 1)
    valid_p = ki >= qi if prev_ok is None else jnp.logical_and(ki >= qi, prev_ok)
    valid_c = ki <= qi
    sp = _dot(q, kp, NT) * scale + (-slope) * ((ATTN_BLOCK + qi - ki) * d).astype(F32)
    sc = _dot(q, kc, NT) * scale + (-slope) * ((qi - ki) * d).astype(F32)
    return jnp.where(valid_p, sp, NEG), jnp.where(valid_c, sc, NEG), valid_p, valid_c


def _slope_table(cfg):
    slopes = jnp.asarray([2.0 ** (-8.0 * (h + 1) / cfg.AH) for h in range(cfg.AH)], F32)
    return jnp.broadcast_to(slopes.reshape(cfg.AH, 1, 1), (cfg.AH, 8, LANES))


def attn_fused_fwd(proj, slopes, cfg, name):
    S, E, AH = cfg.S, cfg.E, cfg.AH
    units, SB = _attn_units(cfg)
    assert S % SB == 0
    npat = len(cfg.patterns)
    scale = E ** -0.5

    def spec(off, prev):
        c0 = _blk(off, E)
        if prev:
            return pl.BlockSpec((SB, E), lambda h, i: (jnp.maximum(i - 1, 0), c0 + h))
        return pl.BlockSpec((SB, E), lambda h, i: (i, c0 + h))

    def body(q_ref, kp_ref, kc_ref, vp_ref, vc_ref, z_ref, sl_ref, oa_ref, om_ref, lt_ref, *scr):
        o_s, l_s = scr[:npat], scr[npat:]
        i = pl.program_id(1)
        slope = sl_ref[0, 0:1, :]
        for p, d, base, pbase, from_prev in units:
            rows, prows = _attn_rows(base, d), _attn_rows(pbase, d)
            q = q_ref[rows, :].astype(BF16)
            kp = (kp_ref if from_prev else kc_ref)[prows, :].astype(BF16)
            vp = (vp_ref if from_prev else vc_ref)[prows, :].astype(BF16)
            sp, sc, _, _ = _unit_scores(q, kp, kc_ref[rows, :].astype(BF16), slope, d, (i > 0) if from_prev else None,
                                        scale)
            m = jnp.maximum(jnp.max(sp, axis=1, keepdims=True), jnp.max(sc, axis=1, keepdims=True))
            pp = jnp.exp(sp - m)
            pc = jnp.exp(sc - m)
            l = jnp.sum(pp, axis=1, keepdims=True) + jnp.sum(pc, axis=1, keepdims=True)
            inv = 1.0 / l
            o_s[p][rows, :] = _dot((pp * inv).astype(BF16), vp, NN) + \
                _dot((pc * inv).astype(BF16), vc_ref[rows, :].astype(BF16), NN)
            l_s[p][rows, :] = m + jnp.log(l)
        ls = [l_s[p][...] for p in range(npat)]
        m = functools.reduce(jnp.maximum, ls)
        lt = m + jnp.log(sum(jnp.exp(l_ - m) for l_ in ls))
        lt_ref[...] = lt
        mix = sum(jnp.exp(ls[p] - lt) * o_s[p][...] for p in range(npat))
        om_ref[...] = mix
        z = z_ref[...]
        oa_ref[...] = (mix * (z * _sigmoid(z))).astype(BF16)

    out = pl.BlockSpec((SB, E), lambda h, i: (i, h))
    return pl.pallas_call(
        body, name=name, grid=(AH, S // SB),
        in_specs=[spec(0, False), spec(cfg.k0, True), spec(cfg.k0, False), spec(cfg.v0, True), spec(cfg.v0, False),
                  spec(cfg.za0, False), pl.BlockSpec((1, 8, LANES), lambda h, i: (h, 0, 0))],
        out_specs=[out, out, pl.BlockSpec((SB, 1), lambda h, i: (h * (S // SB) + i, 0))],
        out_shape=[jax.ShapeDtypeStruct((S, cfg.AW), BF16), jax.ShapeDtypeStruct((S, cfg.AW), F32),
                   jax.ShapeDtypeStruct((AH * S, 1), F32)],
        scratch_shapes=[pltpu.VMEM((SB, E), F32)] * npat + [pltpu.VMEM((SB, 1), F32)] * npat,
        compiler_params=_cp(("parallel", "arbitrary")),
    )(proj, proj, proj, proj, proj, proj, slopes)


def attn_fused_bwd(proj, do_a, o_mix, ltot, slopes, cfg, name):
    S, E, AH = cfg.S, cfg.E, cfg.AH
    units, SB = _attn_units(cfg)
    nsb = S // SB
    last = nsb - 1
    scale = E ** -0.5

    def spec(off, prev):
        c0 = _blk(off, E)
        if prev:
            return pl.BlockSpec((SB, E), lambda h, i: (jnp.maximum(i - 1, 0), c0 + h))
        return pl.BlockSpec((SB, E), lambda h, i: (jnp.minimum(i, last), c0 + h))

    cur = pl.BlockSpec((SB, E), lambda h, i: (jnp.minimum(i, last), h))
    prev = pl.BlockSpec((SB, E), lambda h, i: (jnp.maximum(i - 1, 0), h))

    def body(q_ref, kp_ref, kc_ref, vp_ref, vc_ref, z_ref, doa_ref, om_ref, lt_ref, sl_ref,
             dq_ref, dk_ref, dv_ref, dz_ref, dmix_s, dl_s, dq_s, dkp_s, dvp_s, dkc_s, dvc_s):
        i = pl.program_id(1)

        @pl.when(i == 0)
        def _():
            dkc_s[...] = jnp.zeros_like(dkc_s)
            dvc_s[...] = jnp.zeros_like(dvc_s)

        @pl.when(i < nsb)
        def _():
            z = z_ref[...]
            s = _sigmoid(z)
            doa = doa_ref[...]
            om = om_ref[...]
            dmix = doa * (z * s)
            dmix_s[...] = dmix
            dz_ref[...] = (doa * om * (s * (1.0 + z * (1.0 - s)))).astype(BF16)
            dl_s[...] = jnp.sum(dmix * om, axis=1, keepdims=True)
            dkp_s[...] = dkc_s[...]
            dvp_s[...] = dvc_s[...]
            dkc_s[...] = jnp.zeros_like(dkc_s)
            dvc_s[...] = jnp.zeros_like(dvc_s)
            dq_s[...] = jnp.zeros_like(dq_s)
            slope = sl_ref[0, 0:1, :]
            for p, d, base, pbase, from_prev in units:
                rows, prows = _attn_rows(base, d), _attn_rows(pbase, d)
                q = q_ref[rows, :].astype(BF16)
                kc = kc_ref[rows, :].astype(BF16)
                kp = (kp_ref if from_prev else kc_ref)[prows, :].astype(BF16)
                vp = (vp_ref if from_prev else vc_ref)[prows, :].astype(BF16)
                do = dmix_s[rows, :].astype(BF16)
                lt = lt_ref[rows, :]
                dlt = dl_s[rows, :]
                sp, sc, valid_p, valid_c = _unit_scores(q, kp, kc, slope, d, (i > 0) if from_prev else None, scale)
                pp = jnp.where(valid_p, jnp.exp(sp - lt), 0.0)
                pc = jnp.where(valid_c, jnp.exp(sc - lt), 0.0)
                dsp = (pp * (_dot(do, vp, NT) - dlt) * scale).astype(BF16)
                dsc = (pc * (_dot(do, vc_ref[rows, :].astype(BF16), NT) - dlt) * scale).astype(BF16)
                dq_s[rows, :] += _dot(dsp, kp, NN) + _dot(dsc, kc, NN)
                dkc_s[rows, :] += _dot(dsc, q, TN)
                dvc_s[rows, :] += _dot(pc.astype(BF16), do, TN)
                dk_t, dv_t = (dkp_s, dvp_s) if from_prev else (dkc_s, dvc_s)
                dk_t[prows, :] += _dot(dsp, q, TN)
                dv_t[prows, :] += _dot(pp.astype(BF16), do, TN)
            dq_ref[...] = dq_s[...].astype(BF16)
            dk_ref[...] = dkp_s[...].astype(BF16)
            dv_ref[...] = dvp_s[...].astype(BF16)

        @pl.when(i == nsb)
        def _():
            dk_ref[...] = dkc_s[...].astype(BF16)
            dv_ref[...] = dvc_s[...].astype(BF16)

    sh = jax.ShapeDtypeStruct((S, cfg.AW), BF16)
    acc = pltpu.VMEM((SB, E), F32)
    return pl.pallas_call(
        body, name=name, grid=(AH, nsb + 1),
        in_specs=[spec(0, False), spec(cfg.k0, True), spec(cfg.k0, False), spec(cfg.v0, True), spec(cfg.v0, False),
                  spec(cfg.za0, False), cur, cur,
                  pl.BlockSpec((SB, 1), lambda h, i: (h * nsb + jnp.minimum(i, last), 0)),
                  pl.BlockSpec((1, 8, LANES), lambda h, i: (h, 0, 0))],
        out_specs=[cur, prev, prev, cur], out_shape=[sh] * 4,
        scratch_shapes=[acc, pltpu.VMEM((SB, 1), F32), acc, acc, acc, acc, acc],
        compiler_params=_cp(("parallel", "arbitrary")),
    )(proj, proj, proj, proj, proj, proj, do_a, o_mix, ltot, slopes)


HALO = 8


def _conv_pre(ext_ref, x_ref, h_ref, w_ref, b_ref, tm, kc):
    first = pl.program_id(1) == 0
    ext_ref[0:HALO, :] = jnp.where(first, 0.0, h_ref[...])
    ext_ref[HALO:, :] = x_ref[...]
    pre = b_ref[...] + jnp.zeros_like(x_ref[...])
    for k in range(kc):
        pre = pre + w_ref[k:k + 1, :] * ext_ref[pl.ds(HALO - (kc - 1) + k, tm), :]
    return pre


def conv_fwd(proj, w, b, cfg, name, tm=512, tc=512):
    S, CD, KC = cfg.S, cfg.CD, cfg.KC
    tc = min(tc, CD)
    c0 = _blk(cfg.xbc0, tc)
    hb = tm // HALO

    def body(x_ref, h_ref, w_ref, b_ref, o_ref, ext_ref):
        pre = _conv_pre(ext_ref, x_ref, h_ref, w_ref, b_ref, tm, KC)
        o_ref[...] = pre * _sigmoid(pre)

    return pl.pallas_call(
        body, name=name, grid=(CD // tc, S // tm),
        in_specs=[pl.BlockSpec((tm, tc), lambda c, i: (i, c0 + c)),
                  pl.BlockSpec((HALO, tc), lambda c, i: (jnp.maximum(i * hb - 1, 0), c0 + c)),
                  pl.BlockSpec((KC, tc), lambda c, i: (0, c)),
                  pl.BlockSpec((1, tc), lambda c, i: (0, c))],
        out_specs=pl.BlockSpec((tm, tc), lambda c, i: (i, c)),
        out_shape=jax.ShapeDtypeStruct((S, CD), F32),
        scratch_shapes=[pltpu.VMEM((tm + HALO, tc), F32)],
        compiler_params=_cp(("parallel", "arbitrary")),
    )(proj, proj, w, b)


def conv_bwd_a(proj, dxc, w, b, cfg, name, c_off, tm=512, tc=512):
    S, KC = cfg.S, cfg.KC
    CD = dxc.shape[1]
    tc = min(tc, CD)
    c0 = _blk(cfg.xbc0 + c_off, tc)
    w0 = _blk(c_off, tc)
    hb = tm // HALO

    def body(x_ref, h_ref, d_ref, w_ref, b_ref, dp_ref, gw_ref, gb_ref, ext_ref):
        pre = _conv_pre(ext_ref, x_ref, h_ref, w_ref, b_ref, tm, KC)
        s = _sigmoid(pre)
        dpre = d_ref[...] * (s * (1.0 + pre * (1.0 - s)))
        dp_ref[...] = dpre
        gb = jnp.sum(dpre, axis=0, keepdims=True)
        gws = [jnp.sum(dpre * ext_ref[pl.ds(HALO - (KC - 1) + k, tm), :], axis=0, keepdims=True) for k in range(KC)]
        gw = jnp.concatenate(gws + [jnp.zeros((8 - KC, tc), F32)], axis=0)

        @pl.when(pl.program_id(1) == 0)
        def _():
            gw_ref[...] = gw
            gb_ref[...] = gb

        @pl.when(pl.program_id(1) > 0)
        def _():
            gw_ref[...] += gw
            gb_ref[...] += gb

    return pl.pallas_call(
        body, name=name, grid=(CD // tc, S // tm),
        in_specs=[pl.BlockSpec((tm, tc), lambda c, i: (i, c0 + c)),
                  pl.BlockSpec((HALO, tc), lambda c, i: (jnp.maximum(i * hb - 1, 0), c0 + c)),
                  pl.BlockSpec((tm, tc), lambda c, i: (i, c)),
                  pl.BlockSpec((KC, tc), lambda c, i: (0, w0 + c)),
                  pl.BlockSpec((1, tc), lambda c, i: (0, w0 + c))],
        out_specs=[pl.BlockSpec((tm, tc), lambda c, i: (i, c)),
                   pl.BlockSpec((8, tc), lambda c, i: (0, c)),
                   pl.BlockSpec((1, tc), lambda c, i: (0, c))],
        out_shape=[jax.ShapeDtypeStruct((S, CD), F32), jax.ShapeDtypeStruct((8, CD), F32),
                   jax.ShapeDtypeStruct((1, CD), F32)],
        scratch_shapes=[pltpu.VMEM((tm + HALO, tc), F32)],
        compiler_params=_cp(("parallel", "arbitrary")),
    )(proj, proj, dxc, w, b)


def conv_bwd_b(dpre, w, cfg, name, c_off, tm=512, tc=512):
    S, KC = cfg.S, cfg.KC
    CD = dpre.shape[1]
    tc = min(tc, CD)
    w0 = _blk(c_off, tc)
    hb = tm // HALO
    nrb = S // tm
    last_h = S // HALO - 1

    def body(d_ref, h_ref, w_ref, o_ref, ext_ref):
        is_last = pl.program_id(1) == nrb - 1
        ext_ref[0:tm, :] = d_ref[...]
        ext_ref[tm:, :] = jnp.where(is_last, 0.0, h_ref[...])
        acc = w_ref[KC - 1:KC, :] * d_ref[...]
        for j in range(1, KC):
            acc = acc + w_ref[KC - 1 - j:KC - j, :] * ext_ref[pl.ds(j, tm), :]
        o_ref[...] = acc.astype(BF16)

    return pl.pallas_call(
        body, name=name, grid=(CD // tc, nrb),
        in_specs=[pl.BlockSpec((tm, tc), lambda c, i: (i, c)),
                  pl.BlockSpec((HALO, tc), lambda c, i: (jnp.minimum((i + 1) * hb, last_h), c)),
                  pl.BlockSpec((KC, tc), lambda c, i: (0, w0 + c))],
        out_specs=pl.BlockSpec((tm, tc), lambda c, i: (i, c)),
        out_shape=jax.ShapeDtypeStruct((S, CD), BF16),
        scratch_shapes=[pltpu.VMEM((tm + HALO, tc), F32)],
        compiler_params=_cp(("parallel", "arbitrary")),
    )(dpre, dpre, w)


def _pad_lanes(v, width=LANES):
    return jnp.pad(v, ((0, 0), (0, width - v.shape[1])))


def ssd_prep(dt_raw, dt_bias, a_log, cfg, name):
    S, L = cfg.S, cfg.L

    def body(x_ref, b_ref, al_ref, dt_ref, ac_ref):
        x = x_ref[...] + b_ref[...]
        dt = jnp.maximum(x, 0.0) + jnp.log(1.0 + jnp.exp(-jnp.abs(x)))
        da = dt * (-jnp.exp(al_ref[...]))
        li = lax.broadcasted_iota(jnp.int32, (L, L), 0)
        si = lax.broadcasted_iota(jnp.int32, (L, L), 1)
        tri = jnp.where(li >= si, 1.0, 0.0).astype(F32)
        dt_ref[...] = dt
        ac_ref[...] = lax.dot_general(tri, da, ((NN), ((), ())), precision=lax.Precision.HIGHEST,
                                      preferred_element_type=F32)

    row = pl.BlockSpec((L, LANES), lambda i: (i, 0))
    vec = pl.BlockSpec((1, LANES), lambda i: (0, 0))
    sh = jax.ShapeDtypeStruct((S, LANES), F32)
    return pl.pallas_call(
        body, name=name, grid=(S // L,), in_specs=[row, vec, vec], out_specs=[row, row], out_shape=[sh, sh],
        compiler_params=_cp(("parallel",)),
    )(dt_raw, dt_bias, a_log)


def _spread(v, n):
    return jnp.broadcast_to(v[:, :, None], v.shape + (n,)).reshape(v.shape[0], v.shape[1] * n)


def _head_selectors(cfg):
    def sel(width):
        head = jnp.arange(LANES)[None, :, None]
        slot = jnp.arange(cfg.SG)[:, None, None] * cfg.HPG + (jnp.arange(cfg.HPG * width) // width)[None, None, :]
        return (head == slot).astype(BF16)
    return sel(cfg.P), sel(LANES)


def _spread_heads(v, sel):
    hi = v.astype(BF16)
    r1 = v - hi.astype(F32)
    mid = r1.astype(BF16)
    lo = (r1 - mid.astype(F32)).astype(BF16)
    return _dot(hi, sel, NN) + _dot(mid, sel, NN) + _dot(lo, sel, NN)


def _pair_select(halves, p):
    low = lax.broadcasted_iota(jnp.int32, halves[0].shape, 1) < p
    return jnp.where(low, halves[0], halves[1])


def _head_rows(row, hpg, p):
    return jnp.concatenate([jnp.broadcast_to(row[:, j * LANES:(j + 1) * LANES], (p, LANES)) for j in range(hpg)],
                           axis=0)


def _segment_sums(t, sel):
    hi = t.astype(BF16)
    lo = (t - hi.astype(F32)).astype(BF16)
    return _dot(hi, sel, NT) + _dot(lo, sel, NT)


def ssd_scan_fwd(xc, dt, acum, act, sel_p, sel_l, cfg, name):
    S, L, P, SN, HPG, SG, SI = cfg.S, cfg.L, cfg.P, cfg.SN, cfg.HPG, cfg.SG, cfg.SI
    nc = S // L
    GW = HPG * P
    bcol, ccol = _blk(SI, SN), _blk(SI + cfg.GN, SN)

    def body(xs_ref, b_ref, c_ref, dtn_ref, acn_ref, at_ref, sp_ref, sl_ref, y_ref, st_ref, st):
        @pl.when(pl.program_id(1) == 0)
        def _():
            st[...] = jnp.zeros_like(st)

        acn = acn_ref[...]
        dts = _spread_heads(dtn_ref[...], sp_ref[0])
        a_p = _spread_heads(acn, sp_ref[0])
        acs = _spread_heads(acn, sl_ref[0])
        s0 = st[...]
        st_ref[0] = s0.reshape(HPG, P, SN)
        B = b_ref[...].astype(BF16)
        C = c_ref[...].astype(BF16)
        G = _dot(C, B, NT)
        causal = lax.broadcasted_iota(jnp.int32, (L, L), 0) >= lax.broadcasted_iota(jnp.int32, (L, L), 1)
        xdt = xs_ref[...] * dts
        xdtb = xdt.astype(BF16)
        ws = jnp.exp(a_p[L - 1:L, :] - a_p)
        yo = jnp.exp(a_p) * _dot(C, s0.astype(BF16), NT)
        yd = []
        for jp in range(HPG // 2):
            x_pair = xdtb[:, jp * LANES:(jp + 1) * LANES]
            halves = []
            for j in (2 * jp, 2 * jp + 1):
                dm = jnp.where(causal, jnp.exp(acs[:, j * LANES:(j + 1) * LANES] - at_ref[j:j + 1, :]), 0.0)
                halves.append(_dot((G * dm).astype(BF16), x_pair, NN))
            yd.append(_pair_select(halves, P))
        y_ref[...] = jnp.concatenate(yd, axis=1) + yo
        st[...] = _head_rows(jnp.exp(acs[L - 1:L, :]), HPG, P) * s0 + _dot((xdt * ws).astype(BF16), B, TN)

    y, states = pl.pallas_call(
        body, name=name, grid=(SG, nc),
        in_specs=[pl.BlockSpec((L, GW), lambda g, c: (c, g)),
                  pl.BlockSpec((L, SN), lambda g, c: (c, bcol + g)),
                  pl.BlockSpec((L, SN), lambda g, c: (c, ccol + g)),
                  pl.BlockSpec((L, LANES), lambda g, c: (c, 0)),
                  pl.BlockSpec((L, LANES), lambda g, c: (c, 0)),
                  pl.BlockSpec((HPG, L), lambda g, c: (g, c)),
                  pl.BlockSpec((1, LANES, GW), lambda g, c: (g, 0, 0)),
                  pl.BlockSpec((1, LANES, HPG * LANES), lambda g, c: (g, 0, 0))],
        out_specs=[pl.BlockSpec((L, GW), lambda g, c: (c, g)),
                   pl.BlockSpec((1, HPG, P, SN), lambda g, c: (c, g, 0, 0))],
        out_shape=[jax.ShapeDtypeStruct((S, SI), F32), jax.ShapeDtypeStruct((nc, cfg.SH, P, SN), F32)],
        scratch_shapes=[pltpu.VMEM((GW, SN), F32)],
        compiler_params=_cp(("parallel", "arbitrary")),
    )(xc, xc, xc, dt, acum, act, sel_p, sel_l)
    return y, states


def ssd_scan_bwd(xc, dt, acum, act, sel_p, sel_l, states, y, dy, dvec, cfg, name, side):
    S, L, P, SN, HPG, SG, SI = cfg.S, cfg.L, cfg.P, cfg.SN, cfg.HPG, cfg.SG, cfg.SI
    nc = S // L
    GW = HPG * P
    bcol, ccol = _blk(SI, SN), _blk(SI + cfg.GN, SN)

    def rc(c):
        return nc - 1 - c

    def body(xs_ref, b_ref, c_ref, dtn_ref, acn_ref, at_ref, sp_ref, sl_ref, st_ref, y_ref, dy_ref, dk_ref,
             dxs_ref, db_ref, dc_ref, dac_ref, dxsum_ref, dst):
        @pl.when(pl.program_id(1) == 0)
        def _():
            dst[...] = jnp.zeros_like(dst)

        sel = sp_ref[0]
        acn = acn_ref[...]
        dts = _spread_heads(dtn_ref[...], sel)
        a_p = _spread_heads(acn, sel)
        acs = _spread_heads(acn, sl_ref[0])
        B = b_ref[...].astype(BF16)
        C = c_ref[...].astype(BF16)
        G = _dot(C, B, NT)
        causal = lax.broadcasted_iota(jnp.int32, (L, L), 0) >= lax.broadcasted_iota(jnp.int32, (L, L), 1)
        low = lax.broadcasted_iota(jnp.int32, (L, LANES), 1) < P
        xs = xs_ref[...]
        dY = dy_ref[...]
        xdt = xs * dts
        xdtb = xdt.astype(BF16)
        dYb = dY.astype(BF16)
        s0 = st_ref[0].reshape(GW, SN)
        s0b = s0.astype(BF16)
        ds1 = dst[...]
        ds1b = ds1.astype(BF16)
        ws = jnp.exp(a_p[L - 1:L, :] - a_p)
        dR = (jnp.exp(a_p) * dY).astype(BF16)
        dX2 = ws * _dot(B, ds1b, NT)
        dgsum = jnp.zeros((L, L), F32)
        dX1, yd = [], []
        for jp in range(HPG // 2):
            lanes = slice(jp * LANES, (jp + 1) * LANES)
            x_pair, dy_pair = xdtb[:, lanes], dYb[:, lanes]
            h1, h2 = [], []
            for h, j in enumerate((2 * jp, 2 * jp + 1)):
                dm = jnp.where(causal, jnp.exp(acs[:, j * LANES:(j + 1) * LANES] - at_ref[j:j + 1, :]), 0.0)
                mine = low if h == 0 else jnp.logical_not(low)
                dgsum = dgsum + _dot(jnp.where(mine, dy_pair, jnp.zeros_like(dy_pair)), x_pair, NT) * dm
                Mb = (G * dm).astype(BF16)
                h1.append(_dot(Mb, dy_pair, TN))
                h2.append(_dot(Mb, x_pair, NN))
            dX1.append(_pair_select(h1, P))
            yd.append(_pair_select(h2, P))
        dX1 = jnp.concatenate(dX1, axis=1)
        dX = dX1 + dX2
        pair = (dYb.astype(F32) - dY) * jnp.concatenate(yd, axis=1) - xdtb.astype(F32) * dX1
        through = _segment_sums(xdt * dX2, sel)
        u = ds1 * s0
        u_hi = u.astype(BF16)
        ones = jnp.ones((16, SN), BF16)
        u_rows = _dot(ones, u_hi, NT) + _dot(ones, (u - u_hi.astype(F32)).astype(BF16), NT)
        at_end = jnp.exp(acn[L - 1:L, :]) * _segment_sums(u_rows, sel)[0:1, :] + jnp.sum(through, axis=0, keepdims=True)
        is_last = lax.broadcasted_iota(jnp.int32, (L, LANES), 0) == L - 1
        dac_ref[0] = _segment_sums(dY * y_ref[...] + pair, sel) - through + jnp.where(is_last, at_end, 0.0)
        dxsum_ref[0] = _segment_sums(dX * xs, sel)
        dxs_ref[...] = dX * dts + dk_ref[...] * dY
        dst[...] = _head_rows(jnp.exp(acs[L - 1:L, :]), HPG, P) * ds1 + _dot(dR, C, TN)
        dgb = dgsum.astype(BF16)
        dc_ref[...] = _dot(dR, s0b, NN) + _dot(dgb, B, NN)
        db_ref[...] = _dot((xdt * ws).astype(BF16), ds1b, NN) + _dot(dgb, C, TN)

    wide = pl.BlockSpec((L, GW), lambda g, c: (rc(c), g))
    colspec = pl.BlockSpec((1, L, LANES), lambda g, c: (g, rc(c), 0))
    whole = pl.BlockSpec(memory_space=pl.ANY)
    arrs, gathers = side
    grid = (SG, nc)
    res = pl.pallas_call(
        with_exchange(body, 12, 5, gathers, grid), name=name, grid=grid,
        in_specs=[wide,
                  pl.BlockSpec((L, SN), lambda g, c: (rc(c), bcol + g)),
                  pl.BlockSpec((L, SN), lambda g, c: (rc(c), ccol + g)),
                  pl.BlockSpec((L, LANES), lambda g, c: (rc(c), 0)),
                  pl.BlockSpec((L, LANES), lambda g, c: (rc(c), 0)),
                  pl.BlockSpec((HPG, L), lambda g, c: (g, rc(c))),
                  pl.BlockSpec((1, LANES, GW), lambda g, c: (g, 0, 0)),
                  pl.BlockSpec((1, LANES, HPG * LANES), lambda g, c: (g, 0, 0)),
                  pl.BlockSpec((1, HPG, P, SN), lambda g, c: (rc(c), g, 0, 0)),
                  wide, wide,
                  pl.BlockSpec((1, GW), lambda g, c: (0, g))] + [whole] * len(arrs),
        out_specs=[wide,
                   pl.BlockSpec((L, SN), lambda g, c: (rc(c), g)),
                   pl.BlockSpec((L, SN), lambda g, c: (rc(c), g)),
                   colspec, colspec] + [whole] * len(arrs),
        out_shape=[jax.ShapeDtypeStruct((S, SI), F32), jax.ShapeDtypeStruct((S, cfg.GN), F32),
                   jax.ShapeDtypeStruct((S, cfg.GN), F32),
                   jax.ShapeDtypeStruct((SG, S, LANES), F32), jax.ShapeDtypeStruct((SG, S, LANES), F32)] +
        _exchange_shapes(arrs, gathers),
        scratch_shapes=[pltpu.VMEM((GW, SN), F32)] + _exchange_sems(len(arrs)),
        compiler_params=_cp(("arbitrary", "arbitrary")),
    )(xc, xc, xc, dt, acum, act, sel_p, sel_l, states, y, dy, dvec, *arrs)
    return res[:5], res[5:]


def dt_bwd(dac, dxsum, dt_raw, dt, dt_bias, a_log, cfg, name):
    S, L, SG = cfg.S, cfg.L, cfg.SG

    def body(da_ref, dx_ref, x_ref, dt_ref, b_ref, al_ref, o_ref, gb_ref, ga_ref):
        a = -jnp.exp(al_ref[...])
        dtv = dt_ref[...]
        dxs = jnp.sum(dx_ref[...], axis=0)
        upper = jnp.where(lax.broadcasted_iota(jnp.int32, (L, L), 1) >= lax.broadcasted_iota(jnp.int32, (L, L), 0),
                          1.0, 0.0).astype(F32)
        dda = lax.dot_general(upper, jnp.sum(da_ref[...], axis=0), (NN, ((), ())), precision=lax.Precision.HIGHEST,
                              preferred_element_type=F32)
        draw = (dxs + dda * a) * _sigmoid(x_ref[...] + b_ref[...])
        o_ref[...] = draw.astype(BF16)
        gb = jnp.sum(draw, axis=0, keepdims=True)
        ga = jnp.sum(dda * dtv, axis=0, keepdims=True) * a

        @pl.when(pl.program_id(0) == 0)
        def _():
            gb_ref[...] = gb
            ga_ref[...] = ga

        @pl.when(pl.program_id(0) > 0)
        def _():
            gb_ref[...] += gb
            ga_ref[...] += ga

    row = pl.BlockSpec((L, LANES), lambda i: (i, 0))
    vec = pl.BlockSpec((1, LANES), lambda i: (0, 0))
    return pl.pallas_call(
        body, name=name, grid=(S // L,),
        in_specs=[pl.BlockSpec((SG, L, LANES), lambda i: (0, i, 0))] * 2 + [row, row, vec, vec],
        out_specs=[row, vec, vec],
        out_shape=[jax.ShapeDtypeStruct((S, LANES), BF16), jax.ShapeDtypeStruct((1, LANES), F32),
                   jax.ShapeDtypeStruct((1, LANES), F32)],
        compiler_params=_cp(("arbitrary",)),
    )(dac, dxsum, dt_raw, dt, dt_bias, a_log)


def gated_norm_fwd(y, xc, proj, dvec, nw, cfg, name, tm=128):
    S, SI = cfg.S, cfg.SI

    def body(y_ref, xs_ref, z_ref, d_ref, w_ref, o_ref):
        z = z_ref[...]
        yg = (y_ref[...] + d_ref[...] * xs_ref[...]) * (z * _sigmoid(z))
        r = lax.rsqrt(jnp.mean(yg * yg, axis=-1, keepdims=True) + RMS_EPS)
        o_ref[...] = ((yg * r) * w_ref[...]).astype(BF16)

    row = pl.BlockSpec((tm, SI), lambda i: (i, 0))
    vec = pl.BlockSpec((1, SI), lambda i: (0, 0))
    return pl.pallas_call(
        body, name=name, grid=(S // tm,),
        in_specs=[row, row, pl.BlockSpec((tm, SI), lambda i: (i, _blk(cfg.zs0, SI))), vec, vec],
        out_specs=row, out_shape=jax.ShapeDtypeStruct((S, SI), BF16),
        compiler_params=_cp(("parallel",)),
    )(y, xc, proj, dvec, nw)


def gated_norm_bwd(dyn, y, xc, proj, dvec, nw, cfg, name, tm=128):
    S, SI = cfg.S, cfg.SI

    def body(dn_ref, y_ref, xs_ref, z_ref, d_ref, w_ref, dy_ref, dz_ref, gw_ref, gd_ref):
        z = z_ref[...]
        s = _sigmoid(z)
        sz = z * s
        xs = xs_ref[...]
        yf = y_ref[...] + d_ref[...] * xs
        yg = yf * sz
        r = lax.rsqrt(jnp.mean(yg * yg, axis=-1, keepdims=True) + RMS_EPS)
        dn = dn_ref[...]
        g = dn * w_ref[...]
        dyg = r * g - yg * (r * r * r) * jnp.mean(g * yg, axis=-1, keepdims=True)
        dy = dyg * sz
        dy_ref[...] = dy
        dz_ref[...] = (dyg * yf * (s * (1.0 + z * (1.0 - s)))).astype(BF16)
        gw = jnp.sum(dn * (yg * r), axis=0, keepdims=True)
        gd = jnp.sum(dy * xs, axis=0, keepdims=True)

        @pl.when(pl.program_id(0) == 0)
        def _():
            gw_ref[...] = gw
            gd_ref[...] = gd

        @pl.when(pl.program_id(0) > 0)
        def _():
            gw_ref[...] += gw
            gd_ref[...] += gd

    row = pl.BlockSpec((tm, SI), lambda i: (i, 0))
    vec = pl.BlockSpec((1, SI), lambda i: (0, 0))
    return pl.pallas_call(
        body, name=name, grid=(S // tm,),
        in_specs=[row, row, row, pl.BlockSpec((tm, SI), lambda i: (i, _blk(cfg.zs0, SI))), vec, vec],
        out_specs=[row, row, vec, vec],
        out_shape=[jax.ShapeDtypeStruct((S, SI), F32), jax.ShapeDtypeStruct((S, SI), BF16),
                   jax.ShapeDtypeStruct((1, SI), F32), jax.ShapeDtypeStruct((1, SI), F32)],
        compiler_params=_cp(("arbitrary",)),
    )(dyn, y, xc, proj, dvec, nw)


def local_step(cfg, x, tgt, norm_w, conv_w, conv_b, dt_bias, a_log, d_skip, ssm_norm_w, final_norm_w,
               w_main, w_dt, shards, dt0):
    S, D = cfg.S, cfg.D
    slopes = _slope_table(cfg)
    dt_bias_p = _pad_lanes(dt_bias)
    a_log_p = _pad_lanes(a_log)
    dvec = _spread(d_skip, cfg.P)

    hn = rmsnorm_fwd(x, norm_w, "rmsnorm_fwd")
    proj, gathered = matmul(hn, w_main, 'nn', 512, 1024, 2048, F32, "in_proj", side=(shards, [True] * 3))
    w_attn, w_ssm, w_out = gathered[0].reshape(cfg.AW, D), gathered[1].reshape(cfg.SI, D), gathered[2].reshape(D, D)
    dt_raw = matmul(hn, w_dt, 'nn', 512, 128, 2048, F32, "in_proj_dt")
    o_a, o_mix, ltot = attn_fused_fwd(proj, slopes, cfg, "attn_fwd")
    xc = conv_fwd(proj, conv_w, conv_b, cfg, "conv_fwd")
    dt, acum = ssd_prep(dt_raw, dt_bias_p, a_log_p, cfg, "ssd_prep")
    act = acum[:, :cfg.SH].T
    sel_p, sel_l = _head_selectors(cfg)
    y, states = ssd_scan_fwd(xc, dt, acum, act, sel_p, sel_l, cfg, "ssd_scan_fwd")
    y_n = gated_norm_fwd(y, xc, proj, dvec, ssm_norm_w, cfg, "gated_norm_fwd")
    a_out = matmul(o_a, w_attn, 'nn', 512, 1024, 2048, F32, "attn_branch")
    s_out = matmul(y_n, w_ssm, 'nn', 512, 1024, 2048, F32, "ssm_branch")
    merged = merge_fwd(a_out, s_out, proj, cfg, "merge_fwd")
    res = matmul(merged, w_out, 'nn', 512, 1024, 2048, F32, "out_proj")
    dout, loss_p, g_final_w = final_fwd_bwd(x, res, final_norm_w.reshape(1, D), tgt, "final_fwd_bwd")

    g_w_out = matmul(merged, dout, 'tn', 1024, 1024, 2048, BF16, "g_w_out")
    dmerged = matmul(dout, w_out, 'nt', 512, 1024, 2048, F32, "d_merged")
    da_out, ds_out, dga, dgs = merge_bwd(dmerged, a_out, s_out, proj, cfg, "merge_bwd")
    g_w_attn = matmul(o_a, da_out, 'tn', 1024, 1024, 2048, BF16, "g_w_attn")
    g_w_ssm = matmul(y_n, ds_out, 'tn', 1024, 1024, 2048, BF16, "g_w_ssm")
    do_a = matmul(da_out, w_attn, 'nt', 512, 1024, 2048, F32, "d_o_a")
    dyn = matmul(ds_out, w_ssm, 'nt', 512, 1024, 2048, F32, "d_y_n")
    dy, dz_s, g_ssm_norm, g_dvec = gated_norm_bwd(dyn, y, xc, proj, dvec, ssm_norm_w, cfg, "gated_norm_bwd")
    sends = [g.reshape((N_DEV, g.shape[0] // N_DEV, D)) for g in (g_w_attn, g_w_ssm, g_w_out)]
    (dxs, dB, dC, dac_g, dxsum_g), (r_attn, r_ssm, r_out) = ssd_scan_bwd(
        xc, dt, acum, act, sel_p, sel_l, states, y, dy, dvec, cfg, "ssd_scan_bwd", side=(sends, [False] * 3))
    ddt_raw, g_dt_bias, g_a_log = dt_bwd(dac_g, dxsum_g, dt_raw, dt, dt_bias_p, a_log_p, cfg, "dt_bwd")
    dxbc, g_cw, g_cb = [], [], []
    for nm, piece, c_off in (("xs", dxs, 0), ("b", dB, cfg.SI), ("c", dC, cfg.SI + cfg.GN)):
        dpre, gw, gb = conv_bwd_a(proj, piece, conv_w, conv_b, cfg, "conv_bwd_a_" + nm, c_off)
        dxbc.append(conv_bwd_b(dpre, conv_w, cfg, "conv_bwd_b_" + nm, c_off))
        g_cw.append(gw)
        g_cb.append(gb)
    g_conv_w, g_conv_b = jnp.concatenate(g_cw, axis=1), jnp.concatenate(g_cb, axis=1)
    dq, dk, dv, dz_a = attn_fused_bwd(proj, do_a, o_mix, ltot, slopes, cfg, "attn_bwd")
    dproj = jnp.concatenate([dq, dk, dv, dz_a, dz_s] + dxbc + [dga, dgs], axis=1)
    g_w_main = matmul(hn, dproj, 'tn', 1024, 1024, 2048, BF16, "g_w_main")
    g_w_dt = matmul(hn, ddt_raw, 'tn', 1024, 128, 2048, BF16, "g_w_dt")
    g_w_in = jnp.concatenate([g_w_main[:, :dt0], g_w_dt[:, :cfg.SH], g_w_main[:, dt0:]], axis=1)
    send_in = g_w_in.reshape(D, N_DEV, cfg.N_IN // N_DEV).transpose(1, 0, 2)
    dhn_a, (r_in,) = matmul(dproj, w_main, 'nt', 512, 1024, 2048, F32, "d_hn", side=([send_in], [False]))
    dhn_b = matmul(ddt_raw, w_dt, 'nt', 512, 1024, 128, F32, "d_hn_dt")
    grad_x, g_norm_w = rmsnorm_bwd(dhn_a, dhn_b, x, norm_w, dout, "rmsnorm_bwd")

    g_d_skip = jnp.sum(g_dvec.reshape(cfg.SH, cfg.P), axis=1).reshape(1, cfg.SH)
    small = dict(norm_w=g_norm_w, conv_b=g_conv_b, dt_bias=g_dt_bias[:, :cfg.SH], a_log=g_a_log[:, :cfg.SH],
                 d_skip=g_d_skip, ssm_norm_w=g_ssm_norm, final_norm_w=g_final_w, conv_w=g_conv_w[:cfg.KC])
    return loss_p, grad_x, small, dict(w_in=r_in, w_attn=r_attn, w_ssm=r_ssm, w_out=r_out)


def _mesh_pos():
    return lax.axis_index("x"), lax.axis_index("y"), lax.axis_index("c")


def _flat(pos):
    return 4 * pos[0] + 2 * pos[1] + pos[2]


def _exchange_shapes(arrs, gathers):
    return [jax.ShapeDtypeStruct(((N_DEV,) + a.shape) if g else a.shape, a.dtype) for a, g in zip(arrs, gathers)]


def _exchange_sems(n):
    return [pltpu.SemaphoreType.DMA((n * (N_DEV - 1),)), pltpu.SemaphoreType.DMA((n * (N_DEV - 1),)),
            pltpu.SemaphoreType.DMA((n,))]


def _exchange_copies(ins, outs, gathers, send_sems, recv_sems, loc_sems):
    pos = _mesh_pos()
    me = _flat(pos)
    starts, waits = [], []
    for a in range(len(ins)):
        mine = ins[a] if gathers[a] else ins[a].at[me]
        loc = pltpu.make_async_copy(mine, outs[a].at[me], loc_sems.at[a])
        starts.append(loc)
        waits.append(loc)
        for k in range(1, N_DEV):
            flip = ((k >> 2) & 1, (k >> 1) & 1, k & 1)
            peer = tuple(1 - p if f else p for p, f in zip(pos, flip))
            pk = _flat(peer)
            src = ins[a] if gathers[a] else ins[a].at[pk]
            sems = dict(send_sem=send_sems.at[a * (N_DEV - 1) + k - 1], recv_sem=recv_sems.at[a * (N_DEV - 1) + k - 1],
                        device_id=peer, device_id_type=pl.DeviceIdType.MESH)
            starts.append(pltpu.make_async_remote_copy(src_ref=src, dst_ref=outs[a].at[me], **sems))
            waits.append(pltpu.make_async_remote_copy(src_ref=src, dst_ref=outs[a].at[pk], **sems))
    return starts, waits


def exchange(arrs, gathers, name):
    n = len(arrs)

    def body(*refs):
        starts, waits = _exchange_copies(refs[:n], refs[n:2 * n], gathers, *refs[2 * n:])
        for cp in starts:
            cp.start()
        for cp in waits:
            cp.wait()

    hbm = pl.BlockSpec(memory_space=pltpu.HBM)
    return pl.pallas_call(
        body, name=name, in_specs=[hbm] * n, out_specs=[hbm] * n, out_shape=_exchange_shapes(arrs, gathers),
        scratch_shapes=_exchange_sems(n),
    )(*arrs)


def with_exchange(body, n_in, n_out, gathers, grid):
    n = len(gathers)

    def wrapped(*refs):
        ins, sends = refs[:n_in], refs[n_in:n_in + n]
        outs, recvs = refs[n_in + n:n_in + n + n_out], refs[n_in + 2 * n + n_out - n:n_in + 2 * n + n_out]
        scratch, sems = refs[n_in + 2 * n + n_out:-3], refs[-3:]
        ids = [pl.program_id(d) for d in range(len(grid))]
        first = functools.reduce(jnp.logical_and, [i == 0 for i in ids])
        last = functools.reduce(jnp.logical_and, [i == g - 1 for i, g in zip(ids, grid)])

        @pl.when(first)
        def _():
            for cp in _exchange_copies(sends, recvs, gathers, *sems)[0]:
                cp.start()

        body(*ins, *outs, *scratch)

        @pl.when(last)
        def _():
            for cp in _exchange_copies(sends, recvs, gathers, *sems)[1]:
                cp.wait()

    return wrapped


def adamw(g_src, w, m, v, summed, name, tr=64):
    R, C = w.shape
    tr = min(tr, R)
    assert R % tr == 0

    def body(g_ref, w_ref, m_ref, v_ref, g_out, d_out, m_out, v_out):
        if summed:
            g = g_ref[0].astype(F32)
            for j in range(1, N_DEV):
                g = g + g_ref[j].astype(F32)
        else:
            g = g_ref[...]
        mn = ADAM_B1 * m_ref[...] + (1.0 - ADAM_B1) * g
        vn = ADAM_B2 * v_ref[...] + (1.0 - ADAM_B2) * (g * g)
        m_hat = mn / (1.0 - ADAM_B1 ** ADAM_STEP)
        v_hat = vn / (1.0 - ADAM_B2 ** ADAM_STEP)
        g_out[...] = g
        d_out[...] = -ADAM_LR * (m_hat / (jnp.sqrt(v_hat) + ADAM_EPS) + ADAM_WD * w_ref[...])
        m_out[...] = mn
        v_out[...] = vn

    row = pl.BlockSpec((tr, C), lambda i: (i, 0))
    gspec = pl.BlockSpec((N_DEV, tr, C), lambda i: (0, i, 0)) if summed else row
    sh = jax.ShapeDtypeStruct((R, C), F32)
    return pl.pallas_call(
        body, name=name, grid=(R // tr,), in_specs=[gspec, row, row, row], out_specs=[row] * 4, out_shape=[sh] * 4,
        compiler_params=_cp(("parallel",)),
    )(g_src, w, m, v)


SMALL = ('norm_w', 'conv_b', 'dt_bias', 'a_log', 'd_skip', 'ssm_norm_w', 'final_norm_w')


def _rows(n):
    return -(-n // (8 * LANES)) * 8


def _pack(vals):
    parts = []
    for a in vals:
        f = a.reshape(-1)
        parts.append(jnp.pad(f, (0, _rows(f.size) * LANES - f.size)).reshape(-1, LANES))
    return jnp.concatenate(parts, axis=0)


def _unpack(packed, shapes):
    out, r = [], 0
    for s in shapes:
        n = math.prod(s)
        out.append(packed[r:r + _rows(n)].reshape(-1)[:n].reshape(s))
        r += _rows(n)
    return out


def kernel(x, norm_w, w_in, conv_w, conv_b, dt_bias, a_log, d_skip, ssm_norm_w, w_attn_branch, w_ssm_branch, w_out, final_norm_w, loss_target, m_norm_w, m_w_in, m_conv_w, m_conv_b, m_dt_bias, m_a_log, m_d_skip, m_ssm_norm_w, m_w_attn_branch, m_w_ssm_branch, m_w_out, m_final_norm_w, v_norm_w, v_w_in, v_conv_w, v_conv_b, v_dt_bias, v_a_log, v_d_skip, v_ssm_norm_w, v_w_attn_branch, v_w_ssm_branch, v_w_out, v_final_norm_w):
    cfg = CFG
    D, SH = cfg.D, cfg.SH
    me = _flat(_mesh_pos())
    dt0 = 4 * cfg.AW + cfg.SI + cfg.CD
    ws = w_in.shape[-1]

    g_in, g_cw = exchange([w_in[0].astype(BF16), conv_w[0]], [True] * 2, "gather_w_in")
    w_full = g_in.transpose(1, 0, 2).reshape(D, N_DEV * ws)
    w_main = jnp.concatenate([w_full[:, :dt0], w_full[:, dt0 + SH:]], axis=1)
    w_dt = _pad_lanes(w_full[:, dt0:dt0 + SH])
    conv_full = g_cw.transpose(1, 0, 2).reshape(cfg.KC, cfg.CD)
    shards = [w_attn_branch[0].astype(BF16), w_ssm_branch[0].astype(BF16), w_out[0].astype(BF16)]

    loss_p, grad_x, small, recv = local_step(
        cfg, x[0], loss_target[0], norm_w, conv_full, conv_b, dt_bias, a_log, d_skip,
        ssm_norm_w, final_norm_w, w_main, w_dt, shards, dt0)

    upd = {}
    upd['w_in'] = adamw(recv['w_in'], w_in[0], m_w_in[0], v_w_in[0], True, "adamw_w_in")
    upd['w_attn_branch'] = adamw(recv['w_attn'], w_attn_branch[0], m_w_attn_branch[0], v_w_attn_branch[0], True,
                                 "adamw_w_attn")
    upd['w_ssm_branch'] = adamw(recv['w_ssm'], w_ssm_branch[0], m_w_ssm_branch[0], v_w_ssm_branch[0], True,
                                "adamw_w_ssm")
    upd['w_out'] = adamw(recv['w_out'], w_out[0], m_w_out[0], v_w_out[0], True, "adamw_w_out")

    extra = [jnp.zeros((cfg.KC, cfg.CD), F32), jnp.zeros((1, 1), F32)]
    shapes = [small[n].shape for n in SMALL] + [e.shape for e in extra]
    part = _pack([small[n] for n in SMALL] + [small['conv_w'], loss_p[:, :1]])
    gathered, = exchange([part], [True], "gather_small")
    given = dict(norm_w=(norm_w, m_norm_w, v_norm_w), conv_b=(conv_b, m_conv_b, v_conv_b),
                 dt_bias=(dt_bias, m_dt_bias, v_dt_bias), a_log=(a_log, m_a_log, v_a_log),
                 d_skip=(d_skip, m_d_skip, v_d_skip), ssm_norm_w=(ssm_norm_w, m_ssm_norm_w, v_ssm_norm_w),
                 final_norm_w=(final_norm_w, m_final_norm_w, v_final_norm_w))
    packed = [_pack([given[n][t] for n in SMALL] + extra) for t in range(3)]
    outs = adamw(gathered, *packed, True, "adamw_small", tr=part.shape[0])
    unpacked = [_unpack(o, shapes) for o in outs]
    for i, n in enumerate(SMALL):
        upd[n] = [u[i].reshape(given[n][0].shape) for u in unpacked]
    loss = unpacked[0][-1].reshape(())
    cw = conv_w.shape[-1]
    g_cw_mine = lax.dynamic_slice_in_dim(unpacked[0][-2], me * cw, cw, axis=1)
    upd['conv_w'] = adamw(g_cw_mine.reshape(-1, LANES), conv_w.reshape(-1, LANES), m_conv_w.reshape(-1, LANES),
                          v_conv_w.reshape(-1, LANES), False, "adamw_conv_w")

    order = ['norm_w', 'w_in', 'conv_w', 'conv_b', 'dt_bias', 'a_log', 'd_skip', 'ssm_norm_w', 'w_attn_branch',
             'w_ssm_branch', 'w_out', 'final_norm_w']
    like = dict(norm_w=norm_w, w_in=w_in, conv_w=conv_w, conv_b=conv_b, dt_bias=dt_bias, a_log=a_log, d_skip=d_skip,
                ssm_norm_w=ssm_norm_w, w_attn_branch=w_attn_branch, w_ssm_branch=w_ssm_branch, w_out=w_out,
                final_norm_w=final_norm_w)
    result = [loss, grad_x[None]]
    for t in range(4):
        result += [upd[n][t].reshape(like[n].shape) for n in order]
    return tuple(result)
```

```python
import functools
import math
from typing import NamedTuple

import jax
import jax.numpy as jnp
from jax import lax
from jax.experimental import pallas as pl
from jax.experimental.pallas import tpu as pltpu

F32 = jnp.float32
BF16 = jnp.bfloat16
RMS_EPS = 1e-6
NEG = -1e30
N_DEV = 8
LANES = 128
ATTN_BLOCK = 128
ADAM_LR, ADAM_B1, ADAM_B2, ADAM_EPS, ADAM_WD, ADAM_STEP = 0.001, 0.9, 0.999, 1e-08, 0.01, 10
VMEM_LIMIT = 56 * 1024 * 1024


class Cfg(NamedTuple):
    D: int = 2048
    S: int = 8192
    AH: int = 16
    E: int = 128
    HB: int = 4
    patterns: tuple = ((128, 1), (512, 4), (2048, 16))
    SI: int = 4096
    P: int = 64
    SG: int = 8
    SN: int = 128
    KC: int = 4
    L: int = 128

    @property
    def AW(self): return self.AH * self.E
    @property
    def SH(self): return self.SI // self.P
    @property
    def HPG(self): return self.SH // self.SG
    @property
    def GN(self): return self.SG * self.SN
    @property
    def CD(self): return self.SI + 2 * self.GN
    @property
    def k0(self): return self.AW
    @property
    def v0(self): return 2 * self.AW
    @property
    def za0(self): return 3 * self.AW
    @property
    def zs0(self): return 4 * self.AW
    @property
    def xbc0(self): return 4 * self.AW + self.SI
    @property
    def ga0(self): return self.xbc0 + self.CD
    @property
    def gs0(self): return self.ga0 + self.D
    @property
    def NP(self): return self.gs0 + self.D
    @property
    def N_IN(self): return self.NP + self.SH


CFG = Cfg()


def _cp(sem=None, vmem=VMEM_LIMIT):
    return pltpu.CompilerParams(dimension_semantics=sem, vmem_limit_bytes=vmem)


def _sigmoid(z):
    return 1.0 / (1.0 + jnp.exp(-z))


def _dot(a, b, dims):
    return lax.dot_general(a, b, (dims, ((), ())), preferred_element_type=F32)


NN = ((1,), (0,))
NT = ((1,), (1,))
TN = ((0,), (0,))


def _blk(off, width):
    assert off % width == 0, (off, width)
    return off // width


def matmul(a, b, mode, tm, tn, tk, out_dtype, name, side=None):
    if mode == 'nn':
        (M, K), (_, N) = a.shape, b.shape
    elif mode == 'nt':
        (M, K), (N, _) = a.shape, b.shape
    else:
        (K, M), (_, N) = a.shape, b.shape
    tm, tn, tk = min(tm, M), min(tn, N), min(tk, K)
    assert M % tm == 0 and N % tn == 0 and K % tk == 0, (M, N, K, tm, tn, tk)
    nk = K // tk
    dims = {'nn': NN, 'nt': NT, 'tn': TN}[mode]

    def body(a_ref, b_ref, o_ref, *acc):
        part = _dot(a_ref[...].astype(BF16), b_ref[...].astype(BF16), dims)
        if nk == 1:
            o_ref[...] = part.astype(out_dtype)
        else:
            acc_ref, = acc
            k = pl.program_id(2)

            @pl.when(k == 0)
            def _():
                acc_ref[...] = part

            @pl.when(k > 0)
            def _():
                acc_ref[...] += part

            @pl.when(k == nk - 1)
            def _():
                o_ref[...] = acc_ref[...].astype(out_dtype)

    if mode == 'tn':
        a_spec = pl.BlockSpec((tk, tm), lambda n, m, k: (k, m))
    else:
        a_spec = pl.BlockSpec((tm, tk), lambda n, m, k: (m, k))
    if mode == 'nt':
        b_spec = pl.BlockSpec((tn, tk), lambda n, m, k: (n, k))
    else:
        b_spec = pl.BlockSpec((tk, tn), lambda n, m, k: (k, n))
    grid = (N // tn, M // tm, nk)
    o_spec = pl.BlockSpec((tm, tn), lambda n, m, k: (m, n))
    o_shape = jax.ShapeDtypeStruct((M, N), out_dtype)
    acc = [] if nk == 1 else [pltpu.VMEM((tm, tn), F32)]
    if side is None:
        return pl.pallas_call(
            body, name=name, grid=grid, in_specs=[a_spec, b_spec], out_specs=o_spec, out_shape=o_shape,
            scratch_shapes=acc, compiler_params=_cp(("parallel", "parallel", "arbitrary")),
        )(a, b)
    arrs, gathers = side
    whole = pl.BlockSpec(memory_space=pl.ANY)
    res = pl.pallas_call(
        with_exchange(body, 2, 1, gathers, grid), name=name, grid=grid,
        in_specs=[a_spec, b_spec] + [whole] * len(arrs), out_specs=[o_spec] + [whole] * len(arrs),
        out_shape=[o_shape] + _exchange_shapes(arrs, gathers),
        scratch_shapes=acc + _exchange_sems(len(arrs)),
        compiler_params=_cp(("arbitrary", "arbitrary", "arbitrary")),
    )(a, b, *arrs)
    return res[0], res[1:]


def rmsnorm_fwd(x, w, name, tm=256):
    S, D = x.shape

    def body(x_ref, w_ref, o_ref):
        xv = x_ref[...]
        r = lax.rsqrt(jnp.mean(xv * xv, axis=-1, keepdims=True) + RMS_EPS)
        o_ref[...] = ((xv * r) * w_ref[...]).astype(BF16)

    return pl.pallas_call(
        body, name=name, grid=(S // tm,),
        in_specs=[pl.BlockSpec((tm, D), lambda i: (i, 0)), pl.BlockSpec((1, D), lambda i: (0, 0))],
        out_specs=pl.BlockSpec((tm, D), lambda i: (i, 0)),
        out_shape=jax.ShapeDtypeStruct((S, D), BF16),
        compiler_params=_cp(("parallel",)),
    )(x, w)


def rmsnorm_bwd(dh_a, dh_b, x, w, dout, name, tm=128):
    S, D = x.shape

    def body(da_ref, db_ref, x_ref, w_ref, do_ref, gx_ref, gw_ref):
        xv = x_ref[...]
        dh = da_ref[...] + db_ref[...]
        r = lax.rsqrt(jnp.mean(xv * xv, axis=-1, keepdims=True) + RMS_EPS)
        g = dh * w_ref[...]
        dx = r * g - xv * (r * r * r) * jnp.mean(g * xv, axis=-1, keepdims=True)
        gx_ref[...] = do_ref[...] + dx
        gw = jnp.sum(dh * (xv * r), axis=0, keepdims=True)

        @pl.when(pl.program_id(0) == 0)
        def _():
            gw_ref[...] = gw

        @pl.when(pl.program_id(0) > 0)
        def _():
            gw_ref[...] += gw

    row = pl.BlockSpec((tm, D), lambda i: (i, 0))
    vec = pl.BlockSpec((1, D), lambda i: (0, 0))
    return pl.pallas_call(
        body, name=name, grid=(S // tm,),
        in_specs=[row, row, row, vec, row],
        out_specs=[row, vec],
        out_shape=[jax.ShapeDtypeStruct((S, D), F32), jax.ShapeDtypeStruct((1, D), F32)],
        compiler_params=_cp(("arbitrary",)),
    )(dh_a, dh_b, x, w, dout)


def final_fwd_bwd(x, res, fw, tgt, name, tm=128):
    S, D = x.shape

    def body(x_ref, r_ref, w_ref, t_ref, do_ref, loss_ref, gw_ref):
        out = x_ref[...] + r_ref[...]
        w = w_ref[...]
        r = lax.rsqrt(jnp.mean(out * out, axis=-1, keepdims=True) + RMS_EPS)
        yn = out * r
        err = yn * w - t_ref[...]
        lrow = 0.5 * jnp.mean(err * err, axis=-1, keepdims=True)
        lsum = jnp.zeros((1, LANES), F32) + jnp.sum(lrow, axis=0, keepdims=True)
        dfin = err * (1.0 / D)
        g = dfin * w
        do_ref[...] = r * g - out * (r * r * r) * jnp.mean(g * out, axis=-1, keepdims=True)
        gw = jnp.sum(dfin * yn, axis=0, keepdims=True)

        @pl.when(pl.program_id(0) == 0)
        def _():
            gw_ref[...] = gw
            loss_ref[...] = lsum

        @pl.when(pl.program_id(0) > 0)
        def _():
            gw_ref[...] += gw
            loss_ref[...] += lsum

    row = pl.BlockSpec((tm, D), lambda i: (i, 0))
    vec = pl.BlockSpec((1, D), lambda i: (0, 0))
    return pl.pallas_call(
        body, name=name, grid=(S // tm,),
        in_specs=[row, row, vec, row],
        out_specs=[row, pl.BlockSpec((1, LANES), lambda i: (0, 0)), vec],
        out_shape=[jax.ShapeDtypeStruct((S, D), F32), jax.ShapeDtypeStruct((1, LANES), F32),
                   jax.ShapeDtypeStruct((1, D), F32)],
        compiler_params=_cp(("arbitrary",)),
    )(x, res, fw, tgt)


def merge_fwd(a_out, s_out, proj, cfg, name, tm=256):
    S, D = a_out.shape

    def body(a_ref, s_ref, ga_ref, gs_ref, o_ref):
        o_ref[...] = (_sigmoid(ga_ref[...]) * a_ref[...] + _sigmoid(gs_ref[...]) * s_ref[...]).astype(BF16)

    row = pl.BlockSpec((tm, D), lambda i: (i, 0))
    return pl.pallas_call(
        body, name=name, grid=(S // tm,),
        in_specs=[row, row, pl.BlockSpec((tm, D), lambda i: (i, _blk(cfg.ga0, D))),
                  pl.BlockSpec((tm, D), lambda i: (i, _blk(cfg.gs0, D)))],
        out_specs=row, out_shape=jax.ShapeDtypeStruct((S, D), BF16),
        compiler_params=_cp(("parallel",)),
    )(a_out, s_out, proj, proj)


def merge_bwd(dm, a_out, s_out, proj, cfg, name, tm=128):
    S, D = dm.shape

    def body(dm_ref, a_ref, s_ref, ga_ref, gs_ref, da_ref, ds_ref, dga_ref, dgs_ref):
        dmv = dm_ref[...]
        sa = _sigmoid(ga_ref[...])
        ss = _sigmoid(gs_ref[...])
        da_ref[...] = (dmv * sa).astype(BF16)
        ds_ref[...] = (dmv * ss).astype(BF16)
        dga_ref[...] = (dmv * a_ref[...] * (sa * (1.0 - sa))).astype(BF16)
        dgs_ref[...] = (dmv * s_ref[...] * (ss * (1.0 - ss))).astype(BF16)

    row = pl.BlockSpec((tm, D), lambda i: (i, 0))
    sh = jax.ShapeDtypeStruct((S, D), BF16)
    return pl.pallas_call(
        body, name=name, grid=(S // tm,),
        in_specs=[row, row, row, pl.BlockSpec((tm, D), lambda i: (i, _blk(cfg.ga0, D))),
                  pl.BlockSpec((tm, D), lambda i: (i, _blk(cfg.gs0, D)))],
        out_specs=[row] * 4, out_shape=[sh] * 4,
        compiler_params=_cp(("parallel",)),
    )(dm, a_out, s_out, proj, proj)


def _attn_rows(base, d):
    return pl.ds(base, ATTN_BLOCK) if d == 1 else pl.ds(base, ATTN_BLOCK, stride=d)


def _attn_units(cfg):
    dmax = max(d for _, d in cfg.patterns)
    units = []
    for p, (window, d) in enumerate(cfg.patterns):
        assert window // d == ATTN_BLOCK and dmax % d == 0
        nsub = dmax // d
        for b in range(nsub):
            for r in range(d):
                base = b * ATTN_BLOCK * d + r
                if b > 0:
                    units.append((p, d, base, (b - 1) * ATTN_BLOCK * d + r, False))
                else:
                    units.append((p, d, base, (nsub - 1) * ATTN_BLOCK * d + r, True))
    return units, ATTN_BLOCK * dmax


def _unit_scores(q, kp, kc, slope, d, prev_ok, scale):
    qi = lax.broadcasted_iota(jnp.int32, (ATTN_BLOCK, ATTN_BLOCK), 0)
    ki = lax.broadcasted_iota(jnp.int32, (ATTN_BLOCK, ATTN_BLOCK), 1)
    valid_p = ki >= qi if prev_ok is None else jnp.logical_and(ki >= qi, prev_ok)
    valid_c = ki <= qi
    sp = _dot(q, kp, NT) * scale + (-slope) * ((ATTN_BLOCK + qi - ki) * d).astype(F32)
    sc = _dot(q, kc, NT) * scale + (-slope) * ((qi - ki) * d).astype(F32)
    return jnp.where(valid_p, sp, NEG), jnp.where(valid_c, sc, NEG), valid_p, valid_c


def _slope_table(cfg):
    slopes = jnp.asarray([2.0 ** (-8.0 * (h + 1) / cfg.AH) for h in range(cfg.AH)], F32)
    return jnp.broadcast_to(slopes.reshape(cfg.AH, 1, 1), (cfg.AH, 8, LANES))


def attn_fused_fwd(proj, slopes, cfg, name):
    S, E, AH = cfg.S, cfg.E, cfg.AH
    units, SB = _attn_units(cfg)
    assert S % SB == 0
    npat = len(cfg.patterns)
    scale = E ** -0.5

    def spec(off, prev):
        c0 = _blk(off, E)
        if prev:
            return pl.BlockSpec((SB, E), lambda h, i: (jnp.maximum(i - 1, 0), c0 + h))
        return pl.BlockSpec((SB, E), lambda h, i: (i, c0 + h))

    def body(q_ref, kp_ref, kc_ref, vp_ref, vc_ref, z_ref, sl_ref, oa_ref, om_ref, lt_ref, *scr):
        o_s, l_s = scr[:npat], scr[npat:]
        i = pl.program_id(1)
        slope = sl_ref[0, 0:1, :]
        for p, d, base, pbase, from_prev in units:
            rows, prows = _attn_rows(base, d), _attn_rows(pbase, d)
            q = q_ref[rows, :].astype(BF16)
            kp = (kp_ref if from_prev else kc_ref)[prows, :].astype(BF16)
            vp = (vp_ref if from_prev else vc_ref)[prows, :].astype(BF16)
            sp, sc, _, _ = _unit_scores(q, kp, kc_ref[rows, :].astype(BF16), slope, d, (i > 0) if from_prev else None,
                                        scale)
            m = jnp.maximum(jnp.max(sp, axis=1, keepdims=True), jnp.max(sc, axis=1, keepdims=True))
            pp = jnp.exp(sp - m)
            pc = jnp.exp(sc - m)
            l = jnp.sum(pp, axis=1, keepdims=True) + jnp.sum(pc, axis=1, keepdims=True)
            inv = 1.0 / l
            o_s[p][rows, :] = _dot((pp * inv).astype(BF16), vp, NN) + \
                _dot((pc * inv).astype(BF16), vc_ref[rows, :].astype(BF16), NN)
            l_s[p][rows, :] = m + jnp.log(l)
        ls = [l_s[p][...] for p in range(npat)]
        m = functools.reduce(jnp.maximum, ls)
        lt = m + jnp.log(sum(jnp.exp(l_ - m) for l_ in ls))
        lt_ref[...] = lt
        mix = sum(jnp.exp(ls[p] - lt) * o_s[p][...] for p in range(npat))
        om_ref[...] = mix
        z = z_ref[...]
        oa_ref[...] = (mix * (z * _sigmoid(z))).astype(BF16)

    out = pl.BlockSpec((SB, E), lambda h, i: (i, h))
    return pl.pallas_call(
        body, name=name, grid=(AH, S // SB),
        in_specs=[spec(0, False), spec(cfg.k0, True), spec(cfg.k0, False), spec(cfg.v0, True), spec(cfg.v0, False),
                  spec(cfg.za0, False), pl.BlockSpec((1, 8, LANES), lambda h, i: (h, 0, 0))],
        out_specs=[out, out, pl.BlockSpec((SB, 1), lambda h, i: (h * (S // SB) + i, 0))],
        out_shape=[jax.ShapeDtypeStruct((S, cfg.AW), BF16), jax.ShapeDtypeStruct((S, cfg.AW), F32),
                   jax.ShapeDtypeStruct((AH * S, 1), F32)],
        scratch_shapes=[pltpu.VMEM((SB, E), F32)] * npat + [pltpu.VMEM((SB, 1), F32)] * npat,
        compiler_params=_cp(("parallel", "arbitrary")),
    )(proj, proj, proj, proj, proj, proj, slopes)


def attn_fused_bwd(proj, do_a, o_mix, ltot, slopes, cfg, name):
    S, E, AH = cfg.S, cfg.E, cfg.AH
    units, SB = _attn_units(cfg)
    nsb = S // SB
    last = nsb - 1
    scale = E ** -0.5

    def spec(off, prev):
        c0 = _blk(off, E)
        if prev:
            return pl.BlockSpec((SB, E), lambda h, i: (jnp.maximum(i - 1, 0), c0 + h))
        return pl.BlockSpec((SB, E), lambda h, i: (jnp.minimum(i, last), c0 + h))

    cur = pl.BlockSpec((SB, E), lambda h, i: (jnp.minimum(i, last), h))
    prev = pl.BlockSpec((SB, E), lambda h, i: (jnp.maximum(i - 1, 0), h))

    def body(q_ref, kp_ref, kc_ref, vp_ref, vc_ref, z_ref, doa_ref, om_ref, lt_ref, sl_ref,
             dq_ref, dk_ref, dv_ref, dz_ref, dmix_s, dl_s, dq_s, dkp_s, dvp_s, dkc_s, dvc_s):
        i = pl.program_id(1)

        @pl.when(i == 0)
        def _():
            dkc_s[...] = jnp.zeros_like(dkc_s)
            dvc_s[...] = jnp.zeros_like(dvc_s)

        @pl.when(i < nsb)
        def _():
            z = z_ref[...]
            s = _sigmoid(z)
            doa = doa_ref[...]
            om = om_ref[...]
            dmix = doa * (z * s)
            dmix_s[...] = dmix
            dz_ref[...] = (doa * om * (s * (1.0 + z * (1.0 - s)))).astype(BF16)
            dl_s[...] = jnp.sum(dmix * om, axis=1, keepdims=True)
            dkp_s[...] = dkc_s[...]
            dvp_s[...] = dvc_s[...]
            dkc_s[...] = jnp.zeros_like(dkc_s)
            dvc_s[...] = jnp.zeros_like(dvc_s)
            dq_s[...] = jnp.zeros_like(dq_s)
            slope = sl_ref[0, 0:1, :]
            for p, d, base, pbase, from_prev in units:
                rows, prows = _attn_rows(base, d), _attn_rows(pbase, d)
                q = q_ref[rows, :].astype(BF16)
                kc = kc_ref[rows, :].astype(BF16)
                kp = (kp_ref if from_prev else kc_ref)[prows, :].astype(BF16)
                vp = (vp_ref if from_prev else vc_ref)[prows, :].astype(BF16)
                do = dmix_s[rows, :].astype(BF16)
                lt = lt_ref[rows, :]
                dlt = dl_s[rows, :]
                sp, sc, valid_p, valid_c = _unit_scores(q, kp, kc, slope, d, (i > 0) if from_prev else None, scale)
                pp = jnp.where(valid_p, jnp.exp(sp - lt), 0.0)
                pc = jnp.where(valid_c, jnp.exp(sc - lt), 0.0)
                dsp = (pp * (_dot(do, vp, NT) - dlt) * scale).astype(BF16)
                dsc = (pc * (_dot(do, vc_ref[rows, :].astype(BF16), NT) - dlt) * scale).astype(BF16)
                dq_s[rows, :] += _dot(dsp, kp, NN) + _dot(dsc, kc, NN)
                dkc_s[rows, :] += _dot(dsc, q, TN)
                dvc_s[rows, :] += _dot(pc.astype(BF16), do, TN)
                dk_t, dv_t = (dkp_s, dvp_s) if from_prev else (dkc_s, dvc_s)
                dk_t[prows, :] += _dot(dsp, q, TN)
                dv_t[prows, :] += _dot(pp.astype(BF16), do, TN)
            dq_ref[...] = dq_s[...].astype(BF16)
            dk_ref[...] = dkp_s[...].astype(BF16)
            dv_ref[...] = dvp_s[...].astype(BF16)

        @pl.when(i == nsb)
        def _():
            dk_ref[...] = dkc_s[...].astype(BF16)
            dv_ref[...] = dvc_s[...].astype(BF16)

    sh = jax.ShapeDtypeStruct((S, cfg.AW), BF16)
    acc = pltpu.VMEM((SB, E), F32)
    return pl.pallas_call(
        body, name=name, grid=(AH, nsb + 1),
        in_specs=[spec(0, False), spec(cfg.k0, True), spec(cfg.k0, False), spec(cfg.v0, True), spec(cfg.v0, False),
                  spec(cfg.za0, False), cur, cur,
                  pl.BlockSpec((SB, 1), lambda h, i: (h * nsb + jnp.minimum(i, last), 0)),
                  pl.BlockSpec((1, 8, LANES), lambda h, i: (h, 0, 0))],
        out_specs=[cur, prev, prev, cur], out_shape=[sh] * 4,
        scratch_shapes=[acc, pltpu.VMEM((SB, 1), F32), acc, acc, acc, acc, acc],
        compiler_params=_cp(("parallel", "arbitrary")),
    )(proj, proj, proj, proj, proj, proj, do_a, o_mix, ltot, slopes)


HALO = 8


def _conv_pre(ext_ref, x_ref, h_ref, w_ref, b_ref, tm, kc):
    first = pl.program_id(1) == 0
    ext_ref[0:HALO, :] = jnp.where(first, 0.0, h_ref[...])
    ext_ref[HALO:, :] = x_ref[...]
    pre = b_ref[...] + jnp.zeros_like(x_ref[...])
    for k in range(kc):
        pre = pre + w_ref[k:k + 1, :] * ext_ref[pl.ds(HALO - (kc - 1) + k, tm), :]
    return pre


def conv_fwd(proj, w, b, cfg, name, tm=512, tc=512):
    S, CD, KC = cfg.S, cfg.CD, cfg.KC
    tc = min(tc, CD)
    c0 = _blk(cfg.xbc0, tc)
    hb = tm // HALO

    def body(x_ref, h_ref, w_ref, b_ref, o_ref, ext_ref):
        pre = _conv_pre(ext_ref, x_ref, h_ref, w_ref, b_ref, tm, KC)
        o_ref[...] = pre * _sigmoid(pre)

    return pl.pallas_call(
        body, name=name, grid=(CD // tc, S // tm),
        in_specs=[pl.BlockSpec((tm, tc), lambda c, i: (i, c0 + c)),
                  pl.BlockSpec((HALO, tc), lambda c, i: (jnp.maximum(i * hb - 1, 0), c0 + c)),
                  pl.BlockSpec((KC, tc), lambda c, i: (0, c)),
                  pl.BlockSpec((1, tc), lambda c, i: (0, c))],
        out_specs=pl.BlockSpec((tm, tc), lambda c, i: (i, c)),
        out_shape=jax.ShapeDtypeStruct((S, CD), F32),
        scratch_shapes=[pltpu.VMEM((tm + HALO, tc), F32)],
        compiler_params=_cp(("parallel", "arbitrary")),
    )(proj, proj, w, b)


def conv_bwd_a(proj, dxc, w, b, cfg, name, c_off, tm=512, tc=512):
    S, KC = cfg.S, cfg.KC
    CD = dxc.shape[1]
    tc = min(tc, CD)
    c0 = _blk(cfg.xbc0 + c_off, tc)
    w0 = _blk(c_off, tc)
    hb = tm // HALO

    def body(x_ref, h_ref, d_ref, w_ref, b_ref, dp_ref, gw_ref, gb_ref, ext_ref):
        pre = _conv_pre(ext_ref, x_ref, h_ref, w_ref, b_ref, tm, KC)
        s = _sigmoid(pre)
        dpre = d_ref[...] * (s * (1.0 + pre * (1.0 - s)))
        dp_ref[...] = dpre
        gb = jnp.sum(dpre, axis=0, keepdims=True)
        gws = [jnp.sum(dpre * ext_ref[pl.ds(HALO - (KC - 1) + k, tm), :], axis=0, keepdims=True) for k in range(KC)]
        gw = jnp.concatenate(gws + [jnp.zeros((8 - KC, tc), F32)], axis=0)

        @pl.when(pl.program_id(1) == 0)
        def _():
            gw_ref[...] = gw
            gb_ref[...] = gb

        @pl.when(pl.program_id(1) > 0)
        def _():
            gw_ref[...] += gw
            gb_ref[...] += gb

    return pl.pallas_call(
        body, name=name, grid=(CD // tc, S // tm),
        in_specs=[pl.BlockSpec((tm, tc), lambda c, i: (i, c0 + c)),
                  pl.BlockSpec((HALO, tc), lambda c, i: (jnp.maximum(i * hb - 1, 0), c0 + c)),
                  pl.BlockSpec((tm, tc), lambda c, i: (i, c)),
                  pl.BlockSpec((KC, tc), lambda c, i: (0, w0 + c)),
                  pl.BlockSpec((1, tc), lambda c, i: (0, w0 + c))],
        out_specs=[pl.BlockSpec((tm, tc), lambda c, i: (i, c)),
                   pl.BlockSpec((8, tc), lambda c, i: (0, c)),
                   pl.BlockSpec((1, tc), lambda c, i: (0, c))],
        out_shape=[jax.ShapeDtypeStruct((S, CD), F32), jax.ShapeDtypeStruct((8, CD), F32),
                   jax.ShapeDtypeStruct((1, CD), F32)],
        scratch_shapes=[pltpu.VMEM((tm + HALO, tc), F32)],
        compiler_params=_cp(("parallel", "arbitrary")),
    )(proj, proj, dxc, w, b)


def conv_bwd_b(dpre, w, cfg, name, c_off, tm=512, tc=512):
    S, KC = cfg.S, cfg.KC
    CD = dpre.shape[1]
    tc = min(tc, CD)
    w0 = _blk(c_off, tc)
    hb = tm // HALO
    nrb = S // tm
    last_h = S // HALO - 1

    def body(d_ref, h_ref, w_ref, o_ref, ext_ref):
        is_last = pl.program_id(1) == nrb - 1
        ext_ref[0:tm, :] = d_ref[...]
        ext_ref[tm:, :] = jnp.where(is_last, 0.0, h_ref[...])
        acc = w_ref[KC - 1:KC, :] * d_ref[...]
        for j in range(1, KC):
            acc = acc + w_ref[KC - 1 - j:KC - j, :] * ext_ref[pl.ds(j, tm), :]
        o_ref[...] = acc.astype(BF16)

    return pl.pallas_call(
        body, name=name, grid=(CD // tc, nrb),
        in_specs=[pl.BlockSpec((tm, tc), lambda c, i: (i, c)),
                  pl.BlockSpec((HALO, tc), lambda c, i: (jnp.minimum((i + 1) * hb, last_h), c)),
                  pl.BlockSpec((KC, tc), lambda c, i: (0, w0 + c))],
        out_specs=pl.BlockSpec((tm, tc), lambda c, i: (i, c)),
        out_shape=jax.ShapeDtypeStruct((S, CD), BF16),
        scratch_shapes=[pltpu.VMEM((tm + HALO, tc), F32)],
        compiler_params=_cp(("parallel", "arbitrary")),
    )(dpre, dpre, w)


def _pad_lanes(v, width=LANES):
    return jnp.pad(v, ((0, 0), (0, width - v.shape[1])))


def ssd_prep(dt_raw, dt_bias, a_log, cfg, name):
    S, L = cfg.S, cfg.L

    def body(x_ref, b_ref, al_ref, dt_ref, ac_ref):
        x = x_ref[...] + b_ref[...]
        dt = jnp.maximum(x, 0.0) + jnp.log(1.0 + jnp.exp(-jnp.abs(x)))
        da = dt * (-jnp.exp(al_ref[...]))
        li = lax.broadcasted_iota(jnp.int32, (L, L), 0)
        si = lax.broadcasted_iota(jnp.int32, (L, L), 1)
        tri = jnp.where(li >= si, 1.0, 0.0).astype(F32)
        dt_ref[...] = dt
        ac_ref[...] = lax.dot_general(tri, da, ((NN), ((), ())), precision=lax.Precision.HIGHEST,
                                      preferred_element_type=F32)

    row = pl.BlockSpec((L, LANES), lambda i: (i, 0))
    vec = pl.BlockSpec((1, LANES), lambda i: (0, 0))
    sh = jax.ShapeDtypeStruct((S, LANES), F32)
    return pl.pallas_call(
        body, name=name, grid=(S // L,), in_specs=[row, vec, vec], out_specs=[row, row], out_shape=[sh, sh],
        compiler_params=_cp(("parallel",)),
    )(dt_raw, dt_bias, a_log)


def _spread(v, n):
    return jnp.broadcast_to(v[:, :, None], v.shape + (n,)).reshape(v.shape[0], v.shape[1] * n)


def _head_selectors(cfg):
    def sel(width):
        head = jnp.arange(LANES)[None, :, None]
        slot = jnp.arange(cfg.SG)[:, None, None] * cfg.HPG + (jnp.arange(cfg.HPG * width) // width)[None, None, :]
        return (head == slot).astype(BF16)
    return sel(cfg.P), sel(LANES)


def _spread_heads(v, sel):
    hi = v.astype(BF16)
    r1 = v - hi.astype(F32)
    mid = r1.astype(BF16)
    lo = (r1 - mid.astype(F32)).astype(BF16)
    return _dot(hi, sel, NN) + _dot(mid, sel, NN) + _dot(lo, sel, NN)


def _pair_select(halves, p):
    low = lax.broadcasted_iota(jnp.int32, halves[0].shape, 1) < p
    return jnp.where(low, halves[0], halves[1])


def _head_rows(row, hpg, p):
    return jnp.concatenate([jnp.broadcast_to(row[:, j * LANES:(j + 1) * LANES], (p, LANES)) for j in range(hpg)],
                           axis=0)


def _segment_sums(t, sel):
    hi = t.astype(BF16)
    lo = (t - hi.astype(F32)).astype(BF16)
    return _dot(hi, sel, NT) + _dot(lo, sel, NT)


def ssd_scan_fwd(xc, dt, acum, act, sel_p, sel_l, cfg, name):
    S, L, P, SN, HPG, SG, SI = cfg.S, cfg.L, cfg.P, cfg.SN, cfg.HPG, cfg.SG, cfg.SI
    nc = S // L
    GW = HPG * P
    bcol, ccol = _blk(SI, SN), _blk(SI + cfg.GN, SN)

    def body(xs_ref, b_ref, c_ref, dtn_ref, acn_ref, at_ref, sp_ref, sl_ref, y_ref, st_ref, st):
        @pl.when(pl.program_id(1) == 0)
        def _():
            st[...] = jnp.zeros_like(st)

        acn = acn_ref[...]
        dts = _spread_heads(dtn_ref[...], sp_ref[0])
        a_p = _spread_heads(acn, sp_ref[0])
        acs = _spread_heads(acn, sl_ref[0])
        s0 = st[...]
        st_ref[0] = s0.reshape(HPG, P, SN)
        B = b_ref[...].astype(BF16)
        C = c_ref[...].astype(BF16)
        G = _dot(C, B, NT)
        causal = lax.broadcasted_iota(jnp.int32, (L, L), 0) >= lax.broadcasted_iota(jnp.int32, (L, L), 1)
        xdt = xs_ref[...] * dts
        xdtb = xdt.astype(BF16)
        ws = jnp.exp(a_p[L - 1:L, :] - a_p)
        yo = jnp.exp(a_p) * _dot(C, s0.astype(BF16), NT)
        yd = []
        for jp in range(HPG // 2):
            x_pair = xdtb[:, jp * LANES:(jp + 1) * LANES]
            halves = []
            for j in (2 * jp, 2 * jp + 1):
                dm = jnp.where(causal, jnp.exp(acs[:, j * LANES:(j + 1) * LANES] - at_ref[j:j + 1, :]), 0.0)
                halves.append(_dot((G * dm).astype(BF16), x_pair, NN))
            yd.append(_pair_select(halves, P))
        y_ref[...] = jnp.concatenate(yd, axis=1) + yo
        st[...] = _head_rows(jnp.exp(acs[L - 1:L, :]), HPG, P) * s0 + _dot((xdt * ws).astype(BF16), B, TN)

    y, states = pl.pallas_call(
        body, name=name, grid=(SG, nc),
        in_specs=[pl.BlockSpec((L, GW), lambda g, c: (c, g)),
                  pl.BlockSpec((L, SN), lambda g, c: (c, bcol + g)),
                  pl.BlockSpec((L, SN), lambda g, c: (c, ccol + g)),
                  pl.BlockSpec((L, LANES), lambda g, c: (c, 0)),
                  pl.BlockSpec((L, LANES), lambda g, c: (c, 0)),
                  pl.BlockSpec((HPG, L), lambda g, c: (g, c)),
                  pl.BlockSpec((1, LANES, GW), lambda g, c: (g, 0, 0)),
                  pl.BlockSpec((1, LANES, HPG * LANES), lambda g, c: (g, 0, 0))],
        out_specs=[pl.BlockSpec((L, GW), lambda g, c: (c, g)),
                   pl.BlockSpec((1, HPG, P, SN), lambda g, c: (c, g, 0, 0))],
        out_shape=[jax.ShapeDtypeStruct((S, SI), F32), jax.ShapeDtypeStruct((nc, cfg.SH, P, SN), F32)],
        scratch_shapes=[pltpu.VMEM((GW, SN), F32)],
        compiler_params=_cp(("parallel", "arbitrary")),
    )(xc, xc, xc, dt, acum, act, sel_p, sel_l)
    return y, states


def ssd_scan_bwd(xc, dt, acum, act, sel_p, sel_l, states, y, dy, dvec, cfg, name, side):
    S, L, P, SN, HPG, SG, SI = cfg.S, cfg.L, cfg.P, cfg.SN, cfg.HPG, cfg.SG, cfg.SI
    nc = S // L
    GW = HPG * P
    bcol, ccol = _blk(SI, SN), _blk(SI + cfg.GN, SN)

    def rc(c):
        return nc - 1 - c

    def body(xs_ref, b_ref, c_ref, dtn_ref, acn_ref, at_ref, sp_ref, sl_ref, st_ref, y_ref, dy_ref, dk_ref,
             dxs_ref, db_ref, dc_ref, dac_ref, dxsum_ref, dst):
        @pl.when(pl.program_id(1) == 0)
        def _():
            dst[...] = jnp.zeros_like(dst)

        sel = sp_ref[0]
        acn = acn_ref[...]
        dts = _spread_heads(dtn_ref[...], sel)
        a_p = _spread_heads(acn, sel)
        acs = _spread_heads(acn, sl_ref[0])
        B = b_ref[...].astype(BF16)
        C = c_ref[...].astype(BF16)
        G = _dot(C, B, NT)
        causal = lax.broadcasted_iota(jnp.int32, (L, L), 0) >= lax.broadcasted_iota(jnp.int32, (L, L), 1)
        low = lax.broadcasted_iota(jnp.int32, (L, LANES), 1) < P
        xs = xs_ref[...]
        dY = dy_ref[...]
        xdt = xs * dts
        xdtb = xdt.astype(BF16)
        dYb = dY.astype(BF16)
        s0 = st_ref[0].reshape(GW, SN)
        s0b = s0.astype(BF16)
        ds1 = dst[...]
        ds1b = ds1.astype(BF16)
        ws = jnp.exp(a_p[L - 1:L, :] - a_p)
        dR = (jnp.exp(a_p) * dY).astype(BF16)
        dX2 = ws * _dot(B, ds1b, NT)
        dgsum = jnp.zeros((L, L), F32)
        dX1, yd = [], []
        for jp in range(HPG // 2):
            lanes = slice(jp * LANES, (jp + 1) * LANES)
            x_pair, dy_pair = xdtb[:, lanes], dYb[:, lanes]
            h1, h2 = [], []
            for h, j in enumerate((2 * jp, 2 * jp + 1)):
                dm = jnp.where(causal, jnp.exp(acs[:, j * LANES:(j + 1) * LANES] - at_ref[j:j + 1, :]), 0.0)
                mine = low if h == 0 else jnp.logical_not(low)
                dgsum = dgsum + _dot(jnp.where(mine, dy_pair, jnp.zeros_like(dy_pair)), x_pair, NT) * dm
                Mb = (G * dm).astype(BF16)
                h1.append(_dot(Mb, dy_pair, TN))
                h2.append(_dot(Mb, x_pair, NN))
            dX1.append(_pair_select(h1, P))
            yd.append(_pair_select(h2, P))
        dX1 = jnp.concatenate(dX1, axis=1)
        dX = dX1 + dX2
        pair = (dYb.astype(F32) - dY) * jnp.concatenate(yd, axis=1) - xdtb.astype(F32) * dX1
        through = _segment_sums(xdt * dX2, sel)
        u = ds1 * s0
        u_hi = u.astype(BF16)
        ones = jnp.ones((16, SN), BF16)
        u_rows = _dot(ones, u_hi, NT) + _dot(ones, (u - u_hi.astype(F32)).astype(BF16), NT)
        at_end = jnp.exp(acn[L - 1:L, :]) * _segment_sums(u_rows, sel)[0:1, :] + jnp.sum(through, axis=0, keepdims=True)
        is_last = lax.broadcasted_iota(jnp.int32, (L, LANES), 0) == L - 1
        dac_ref[0] = _segment_sums(dY * y_ref[...] + pair, sel) - through + jnp.where(is_last, at_end, 0.0)
        dxsum_ref[0] = _segment_sums(dX * xs, sel)
        dxs_ref[...] = dX * dts + dk_ref[...] * dY
        dst[...] = _head_rows(jnp.exp(acs[L - 1:L, :]), HPG, P) * ds1 + _dot(dR, C, TN)
        dgb = dgsum.astype(BF16)
        dc_ref[...] = _dot(dR, s0b, NN) + _dot(dgb, B, NN)
        db_ref[...] = _dot((xdt * ws).astype(BF16), ds1b, NN) + _dot(dgb, C, TN)

    wide = pl.BlockSpec((L, GW), lambda g, c: (rc(c), g))
    colspec = pl.BlockSpec((1, L, LANES), lambda g, c: (g, rc(c), 0))
    whole = pl.BlockSpec(memory_space=pl.ANY)
    arrs, gathers = side
    grid = (SG, nc)
    res = pl.pallas_call(
        with_exchange(body, 12, 5, gathers, grid), name=name, grid=grid,
        in_specs=[wide,
                  pl.BlockSpec((L, SN), lambda g, c: (rc(c), bcol + g)),
                  pl.BlockSpec((L, SN), lambda g, c: (rc(c), ccol + g)),
                  pl.BlockSpec((L, LANES), lambda g, c: (rc(c), 0)),
                  pl.BlockSpec((L, LANES), lambda g, c: (rc(c), 0)),
                  pl.BlockSpec((HPG, L), lambda g, c: (g, rc(c))),
                  pl.BlockSpec((1, LANES, GW), lambda g, c: (g, 0, 0)),
                  pl.BlockSpec((1, LANES, HPG * LANES), lambda g, c: (g, 0, 0)),
                  pl.BlockSpec((1, HPG, P, SN), lambda g, c: (rc(c), g, 0, 0)),
                  wide, wide,
                  pl.BlockSpec((1, GW), lambda g, c: (0, g))] + [whole] * len(arrs),
        out_specs=[wide,
                   pl.BlockSpec((L, SN), lambda g, c: (rc(c), g)),
                   pl.BlockSpec((L, SN), lambda g, c: (rc(c), g)),
                   colspec, colspec] + [whole] * len(arrs),
        out_shape=[jax.ShapeDtypeStruct((S, SI), F32), jax.ShapeDtypeStruct((S, cfg.GN), F32),
                   jax.ShapeDtypeStruct((S, cfg.GN), F32),
                   jax.ShapeDtypeStruct((SG, S, LANES), F32), jax.ShapeDtypeStruct((SG, S, LANES), F32)] +
        _exchange_shapes(arrs, gathers),
        scratch_shapes=[pltpu.VMEM((GW, SN), F32)] + _exchange_sems(len(arrs)),
        compiler_params=_cp(("arbitrary", "arbitrary")),
    )(xc, xc, xc, dt, acum, act, sel_p, sel_l, states, y, dy, dvec, *arrs)
    return res[:5], res[5:]


def dt_bwd(dac, dxsum, dt_raw, dt, dt_bias, a_log, cfg, name):
    S, L, SG = cfg.S, cfg.L, cfg.SG

    def body(da_ref, dx_ref, x_ref, dt_ref, b_ref, al_ref, o_ref, gb_ref, ga_ref):
        a = -jnp.exp(al_ref[...])
        dtv = dt_ref[...]
        dxs = jnp.sum(dx_ref[...], axis=0)
        upper = jnp.where(lax.broadcasted_iota(jnp.int32, (L, L), 1) >= lax.broadcasted_iota(jnp.int32, (L, L), 0),
                          1.0, 0.0).astype(F32)
        dda = lax.dot_general(upper, jnp.sum(da_ref[...], axis=0), (NN, ((), ())), precision=lax.Precision.HIGHEST,
                              preferred_element_type=F32)
        draw = (dxs + dda * a) * _sigmoid(x_ref[...] + b_ref[...])
        o_ref[...] = draw.astype(BF16)
        gb = jnp.sum(draw, axis=0, keepdims=True)
        ga = jnp.sum(dda * dtv, axis=0, keepdims=True) * a

        @pl.when(pl.program_id(0) == 0)
        def _():
            gb_ref[...] = gb
            ga_ref[...] = ga

        @pl.when(pl.program_id(0) > 0)
        def _():
            gb_ref[...] += gb
            ga_ref[...] += ga

    row = pl.BlockSpec((L, LANES), lambda i: (i, 0))
    vec = pl.BlockSpec((1, LANES), lambda i: (0, 0))
    return pl.pallas_call(
        body, name=name, grid=(S // L,),
        in_specs=[pl.BlockSpec((SG, L, LANES), lambda i: (0, i, 0))] * 2 + [row, row, vec, vec],
        out_specs=[row, vec, vec],
        out_shape=[jax.ShapeDtypeStruct((S, LANES), BF16), jax.ShapeDtypeStruct((1, LANES), F32),
                   jax.ShapeDtypeStruct((1, LANES), F32)],
        compiler_params=_cp(("arbitrary",)),
    )(dac, dxsum, dt_raw, dt, dt_bias, a_log)


def gated_norm_fwd(y, xc, proj, dvec, nw, cfg, name, tm=128):
    S, SI = cfg.S, cfg.SI

    def body(y_ref, xs_ref, z_ref, d_ref, w_ref, o_ref):
        z = z_ref[...]
        yg = (y_ref[...] + d_ref[...] * xs_ref[...]) * (z * _sigmoid(z))
        r = lax.rsqrt(jnp.mean(yg * yg, axis=-1, keepdims=True) + RMS_EPS)
        o_ref[...] = ((yg * r) * w_ref[...]).astype(BF16)

    row = pl.BlockSpec((tm, SI), lambda i: (i, 0))
    vec = pl.BlockSpec((1, SI), lambda i: (0, 0))
    return pl.pallas_call(
        body, name=name, grid=(S // tm,),
        in_specs=[row, row, pl.BlockSpec((tm, SI), lambda i: (i, _blk(cfg.zs0, SI))), vec, vec],
        out_specs=row, out_shape=jax.ShapeDtypeStruct((S, SI), BF16),
        compiler_params=_cp(("parallel",)),
    )(y, xc, proj, dvec, nw)


def gated_norm_bwd(dyn, y, xc, proj, dvec, nw, cfg, name, tm=128):
    S, SI = cfg.S, cfg.SI

    def body(dn_ref, y_ref, xs_ref, z_ref, d_ref, w_ref, dy_ref, dz_ref, gw_ref, gd_ref):
        z = z_ref[...]
        s = _sigmoid(z)
        sz = z * s
        xs = xs_ref[...]
        yf = y_ref[...] + d_ref[...] * xs
        yg = yf * sz
        r = lax.rsqrt(jnp.mean(yg * yg, axis=-1, keepdims=True) + RMS_EPS)
        dn = dn_ref[...]
        g = dn * w_ref[...]
        dyg = r * g - yg * (r * r * r) * jnp.mean(g * yg, axis=-1, keepdims=True)
        dy = dyg * sz
        dy_ref[...] = dy
        dz_ref[...] = (dyg * yf * (s * (1.0 + z * (1.0 - s)))).astype(BF16)
        gw = jnp.sum(dn * (yg * r), axis=0, keepdims=True)
        gd = jnp.sum(dy * xs, axis=0, keepdims=True)

        @pl.when(pl.program_id(0) == 0)
        def _():
            gw_ref[...] = gw
            gd_ref[...] = gd

        @pl.when(pl.program_id(0) > 0)
        def _():
            gw_ref[...] += gw
            gd_ref[...] += gd

    row = pl.BlockSpec((tm, SI), lambda i: (i, 0))
    vec = pl.BlockSpec((1, SI), lambda i: (0, 0))
    return pl.pallas_call(
        body, name=name, grid=(S // tm,),
        in_specs=[row, row, row, pl.BlockSpec((tm, SI), lambda i: (i, _blk(cfg.zs0, SI))), vec, vec],
        out_specs=[row, row, vec, vec],
        out_shape=[jax.ShapeDtypeStruct((S, SI), F32), jax.ShapeDtypeStruct((S, SI), BF16),
                   jax.ShapeDtypeStruct((1, SI), F32), jax.ShapeDtypeStruct((1, SI), F32)],
        compiler_params=_cp(("arbitrary",)),
    )(dyn, y, xc, proj, dvec, nw)


def local_step(cfg, x, tgt, norm_w, conv_w, conv_b, dt_bias, a_log, d_skip, ssm_norm_w, final_norm_w,
               w_main, w_dt, shards, dt0):
    S, D = cfg.S, cfg.D
    slopes = _slope_table(cfg)
    dt_bias_p = _pad_lanes(dt_bias)
    a_log_p = _pad_lanes(a_log)
    dvec = _spread(d_skip, cfg.P)

    hn = rmsnorm_fwd(x, norm_w, "rmsnorm_fwd")
    proj, gathered = matmul(hn, w_main, 'nn', 1024, 1024, 2048, F32, "in_proj", side=(shards, [True] * 3))
    w_attn, w_ssm, w_out = gathered[0].reshape(cfg.AW, D), gathered[1].reshape(cfg.SI, D), gathered[2].reshape(D, D)
    dt_raw = matmul(hn, w_dt, 'nn', 512, 128, 2048, F32, "in_proj_dt")
    o_a, o_mix, ltot = attn_fused_fwd(proj, slopes, cfg, "attn_fwd")
    xc = conv_fwd(proj, conv_w, conv_b, cfg, "conv_fwd")
    dt, acum = ssd_prep(dt_raw, dt_bias_p, a_log_p, cfg, "ssd_prep")
    act = acum[:, :cfg.SH].T
    sel_p, sel_l = _head_selectors(cfg)
    y, states = ssd_scan_fwd(xc, dt, acum, act, sel_p, sel_l, cfg, "ssd_scan_fwd")
    y_n = gated_norm_fwd(y, xc, proj, dvec, ssm_norm_w, cfg, "gated_norm_fwd")
    a_out = matmul(o_a, w_attn, 'nn', 512, 1024, 2048, F32, "attn_branch")
    s_out = matmul(y_n, w_ssm, 'nn', 512, 1024, 2048, F32, "ssm_branch")
    merged = merge_fwd(a_out, s_out, proj, cfg, "merge_fwd")
    res = matmul(merged, w_out, 'nn', 512, 1024, 2048, F32, "out_proj")
    dout, loss_p, g_final_w = final_fwd_bwd(x, res, final_norm_w.reshape(1, D), tgt, "final_fwd_bwd")

    g_w_out = matmul(merged, dout, 'tn', 1024, 1024, 2048, BF16, "g_w_out")
    dmerged = matmul(dout, w_out, 'nt', 512, 1024, 2048, F32, "d_merged")
    da_out, ds_out, dga, dgs = merge_bwd(dmerged, a_out, s_out, proj, cfg, "merge_bwd")
    g_w_attn = matmul(o_a, da_out, 'tn', 1024, 1024, 2048, BF16, "g_w_attn")
    g_w_ssm = matmul(y_n, ds_out, 'tn', 1024, 1024, 2048, BF16, "g_w_ssm")
    do_a = matmul(da_out, w_attn, 'nt', 512, 1024, 2048, F32, "d_o_a")
    dyn = matmul(ds_out, w_ssm, 'nt', 512, 1024, 2048, F32, "d_y_n")
    dy, dz_s, g_ssm_norm, g_dvec = gated_norm_bwd(dyn, y, xc, proj, dvec, ssm_norm_w, cfg, "gated_norm_bwd")
    sends = [g.reshape((N_DEV, g.shape[0] // N_DEV, D)) for g in (g_w_attn, g_w_ssm, g_w_out)]
    (dxs, dB, dC, dac_g, dxsum_g), (r_attn, r_ssm, r_out) = ssd_scan_bwd(
        xc, dt, acum, act, sel_p, sel_l, states, y, dy, dvec, cfg, "ssd_scan_bwd", side=(sends, [False] * 3))
    ddt_raw, g_dt_bias, g_a_log = dt_bwd(dac_g, dxsum_g, dt_raw, dt, dt_bias_p, a_log_p, cfg, "dt_bwd")
    dxbc, g_cw, g_cb = [], [], []
    for nm, piece, c_off in (("xs", dxs, 0), ("b", dB, cfg.SI), ("c", dC, cfg.SI + cfg.GN)):
        dpre, gw, gb = conv_bwd_a(proj, piece, conv_w, conv_b, cfg, "conv_bwd_a_" + nm, c_off)
        dxbc.append(conv_bwd_b(dpre, conv_w, cfg, "conv_bwd_b_" + nm, c_off))
        g_cw.append(gw)
        g_cb.append(gb)
    g_conv_w, g_conv_b = jnp.concatenate(g_cw, axis=1), jnp.concatenate(g_cb, axis=1)
    dq, dk, dv, dz_a = attn_fused_bwd(proj, do_a, o_mix, ltot, slopes, cfg, "attn_bwd")
    dproj = jnp.concatenate([dq, dk, dv, dz_a, dz_s] + dxbc + [dga, dgs], axis=1)
    g_w_main = matmul(hn, dproj, 'tn', 1024, 1024, 2048, BF16, "g_w_main")
    g_w_dt = matmul(hn, ddt_raw, 'tn', 1024, 128, 2048, BF16, "g_w_dt")
    g_w_in = jnp.concatenate([g_w_main[:, :dt0], g_w_dt[:, :cfg.SH], g_w_main[:, dt0:]], axis=1)
    send_in = g_w_in.reshape(D, N_DEV, cfg.N_IN // N_DEV).transpose(1, 0, 2)
    dhn_a, (r_in,) = matmul(dproj, w_main, 'nt', 1024, 1024, 2048, F32, "d_hn", side=([send_in], [False]))
    dhn_b = matmul(ddt_raw, w_dt, 'nt', 512, 1024, 128, F32, "d_hn_dt")
    grad_x, g_norm_w = rmsnorm_bwd(dhn_a, dhn_b, x, norm_w, dout, "rmsnorm_bwd")

    g_d_skip = jnp.sum(g_dvec.reshape(cfg.SH, cfg.P), axis=1).reshape(1, cfg.SH)
    small = dict(norm_w=g_norm_w, conv_b=g_conv_b, dt_bias=g_dt_bias[:, :cfg.SH], a_log=g_a_log[:, :cfg.SH],
                 d_skip=g_d_skip, ssm_norm_w=g_ssm_norm, final_norm_w=g_final_w, conv_w=g_conv_w[:cfg.KC])
    return loss_p, grad_x, small, dict(w_in=r_in, w_attn=r_attn, w_ssm=r_ssm, w_out=r_out)


def _mesh_pos():
    return lax.axis_index("x"), lax.axis_index("y"), lax.axis_index("c")


def _flat(pos):
    return 4 * pos[0] + 2 * pos[1] + pos[2]


def _exchange_shapes(arrs, gathers):
    return [jax.ShapeDtypeStruct(((N_DEV,) + a.shape) if g else a.shape, a.dtype) for a, g in zip(arrs, gathers)]


def _exchange_sems(n):
    return [pltpu.SemaphoreType.DMA((n * (N_DEV - 1),)), pltpu.SemaphoreType.DMA((n * (N_DEV - 1),)),
            pltpu.SemaphoreType.DMA((n,))]


def _exchange_copies(ins, outs, gathers, send_sems, recv_sems, loc_sems):
    pos = _mesh_pos()
    me = _flat(pos)
    starts, waits = [], []
    for a in range(len(ins)):
        mine = ins[a] if gathers[a] else ins[a].at[me]
        loc = pltpu.make_async_copy(mine, outs[a].at[me], loc_sems.at[a])
        starts.append(loc)
        waits.append(loc)
        for k in range(1, N_DEV):
            flip = ((k >> 2) & 1, (k >> 1) & 1, k & 1)
            peer = tuple(1 - p if f else p for p, f in zip(pos, flip))
            pk = _flat(peer)
            src = ins[a] if gathers[a] else ins[a].at[pk]
            sems = dict(send_sem=send_sems.at[a * (N_DEV - 1) + k - 1], recv_sem=recv_sems.at[a * (N_DEV - 1) + k - 1],
                        device_id=peer, device_id_type=pl.DeviceIdType.MESH)
            starts.append(pltpu.make_async_remote_copy(src_ref=src, dst_ref=outs[a].at[me], **sems))
            waits.append(pltpu.make_async_remote_copy(src_ref=src, dst_ref=outs[a].at[pk], **sems))
    return starts, waits


def exchange(arrs, gathers, name):
    n = len(arrs)

    def body(*refs):
        starts, waits = _exchange_copies(refs[:n], refs[n:2 * n], gathers, *refs[2 * n:])
        for cp in starts:
            cp.start()
        for cp in waits:
            cp.wait()

    hbm = pl.BlockSpec(memory_space=pltpu.HBM)
    return pl.pallas_call(
        body, name=name, in_specs=[hbm] * n, out_specs=[hbm] * n, out_shape=_exchange_shapes(arrs, gathers),
        scratch_shapes=_exchange_sems(n),
    )(*arrs)


def gather_two_level(arrs, chunks, name):
    n = len(arrs)

    def body(*refs):
        ins, outs = refs[:n], refs[n:2 * n]
        send_sems, recv_sems, loc_sems = refs[2 * n:]
        x, y, c = _mesh_pos()
        me, sib = (x, y, c), (x, y, 1 - c)
        chips = [(1 - x, y), (x, 1 - y), (1 - x, 1 - y)]
        plan, base = [], 0
        for a in range(n):
            step = arrs[a].shape[0] // chunks[a]
            for q in range(chunks[a]):
                plan.append((a, pl.ds(q * step, step), base))
                base += N_DEV - 1

        def copy(a, rows, sem, block, to, own=False):
            dst = outs[a].at[_flat(block), rows]
            return pltpu.make_async_remote_copy(
                src_ref=ins[a].at[rows] if own else dst, dst_ref=dst, send_sem=send_sems.at[sem],
                recv_sem=recv_sems.at[sem], device_id=to, device_id_type=pl.DeviceIdType.MESH)

        local = [pltpu.make_async_copy(ins[a], outs[a].at[_flat(me)], loc_sems.at[a]) for a in range(n)]
        for cp in local:
            cp.start()
        sent = []
        for a, rows, s in plan:
            sent.append(copy(a, rows, s, me, sib, own=True))
            sent += [copy(a, rows, s + 1 + j, me, (*chip, c), own=True) for j, chip in enumerate(chips)]
        for cp in sent:
            cp.start()
        for a, rows, s in plan:
            for j, chip in enumerate(chips):
                copy(a, rows, s + 1 + j, (*chip, c), me).wait_recv()
                passed = copy(a, rows, s + 4 + j, (*chip, c), sib)
                passed.start()
                sent.append(passed)
        for a, rows, s in plan:
            copy(a, rows, s, sib, me).wait_recv()
            for j, chip in enumerate(chips):
                copy(a, rows, s + 4 + j, (*chip, 1 - c), me).wait_recv()
        for cp in sent:
            cp.wait_send()
        for cp in local:
            cp.wait()

    hbm = pl.BlockSpec(memory_space=pltpu.HBM)
    nsem = (N_DEV - 1) * sum(chunks)
    return pl.pallas_call(
        body, name=name, in_specs=[hbm] * n, out_specs=[hbm] * n, out_shape=_exchange_shapes(arrs, [True] * n),
        scratch_shapes=[pltpu.SemaphoreType.DMA((nsem,)), pltpu.SemaphoreType.DMA((nsem,)),
                        pltpu.SemaphoreType.DMA((n,))],
    )(*arrs)


def with_exchange(body, n_in, n_out, gathers, grid):
    n = len(gathers)

    def wrapped(*refs):
        ins, sends = refs[:n_in], refs[n_in:n_in + n]
        outs, recvs = refs[n_in + n:n_in + n + n_out], refs[n_in + 2 * n + n_out - n:n_in + 2 * n + n_out]
        scratch, sems = refs[n_in + 2 * n + n_out:-3], refs[-3:]
        ids = [pl.program_id(d) for d in range(len(grid))]
        first = functools.reduce(jnp.logical_and, [i == 0 for i in ids])
        last = functools.reduce(jnp.logical_and, [i == g - 1 for i, g in zip(ids, grid)])

        @pl.when(first)
        def _():
            for cp in _exchange_copies(sends, recvs, gathers, *sems)[0]:
                cp.start()

        body(*ins, *outs, *scratch)

        @pl.when(last)
        def _():
            for cp in _exchange_copies(sends, recvs, gathers, *sems)[1]:
                cp.wait()

    return wrapped


def adamw(g_src, w, m, v, summed, name, tr=64):
    R, C = w.shape
    tr = min(tr, R)
    assert R % tr == 0

    def body(g_ref, w_ref, m_ref, v_ref, g_out, d_out, m_out, v_out):
        if summed:
            g = g_ref[0].astype(F32)
            for j in range(1, N_DEV):
                g = g + g_ref[j].astype(F32)
        else:
            g = g_ref[...]
        mn = ADAM_B1 * m_ref[...] + (1.0 - ADAM_B1) * g
        vn = ADAM_B2 * v_ref[...] + (1.0 - ADAM_B2) * (g * g)
        m_hat = mn / (1.0 - ADAM_B1 ** ADAM_STEP)
        v_hat = vn / (1.0 - ADAM_B2 ** ADAM_STEP)
        g_out[...] = g
        d_out[...] = -ADAM_LR * (m_hat / (jnp.sqrt(v_hat) + ADAM_EPS) + ADAM_WD * w_ref[...])
        m_out[...] = mn
        v_out[...] = vn

    row = pl.BlockSpec((tr, C), lambda i: (i, 0))
    gspec = pl.BlockSpec((N_DEV, tr, C), lambda i: (0, i, 0)) if summed else row
    sh = jax.ShapeDtypeStruct((R, C), F32)
    return pl.pallas_call(
        body, name=name, grid=(R // tr,), in_specs=[gspec, row, row, row], out_specs=[row] * 4, out_shape=[sh] * 4,
        compiler_params=_cp(("parallel",)),
    )(g_src, w, m, v)


SMALL = ('norm_w', 'conv_b', 'dt_bias', 'a_log', 'd_skip', 'ssm_norm_w', 'final_norm_w')


def _rows(n):
    return -(-n // (8 * LANES)) * 8


def _pack(vals):
    parts = []
    for a in vals:
        f = a.reshape(-1)
        parts.append(jnp.pad(f, (0, _rows(f.size) * LANES - f.size)).reshape(-1, LANES))
    return jnp.concatenate(parts, axis=0)


def _unpack(packed, shapes):
    out, r = [], 0
    for s in shapes:
        n = math.prod(s)
        out.append(packed[r:r + _rows(n)].reshape(-1)[:n].reshape(s))
        r += _rows(n)
    return out


def kernel(x, norm_w, w_in, conv_w, conv_b, dt_bias, a_log, d_skip, ssm_norm_w, w_attn_branch, w_ssm_branch, w_out, final_norm_w, loss_target, m_norm_w, m_w_in, m_conv_w, m_conv_b, m_dt_bias, m_a_log, m_d_skip, m_ssm_norm_w, m_w_attn_branch, m_w_ssm_branch, m_w_out, m_final_norm_w, v_norm_w, v_w_in, v_conv_w, v_conv_b, v_dt_bias, v_a_log, v_d_skip, v_ssm_norm_w, v_w_attn_branch, v_w_ssm_branch, v_w_out, v_final_norm_w):
    cfg = CFG
    D, SH = cfg.D, cfg.SH
    me = _flat(_mesh_pos())
    dt0 = 4 * cfg.AW + cfg.SI + cfg.CD
    ws = w_in.shape[-1]

    g_in, g_cw = gather_two_level([w_in[0].astype(BF16), conv_w[0]], [4, 1], "gather_w_in")
    w_full = g_in.transpose(1, 0, 2).reshape(D, N_DEV * ws)
    w_main = jnp.concatenate([w_full[:, :dt0], w_full[:, dt0 + SH:]], axis=1)
    w_dt = _pad_lanes(w_full[:, dt0:dt0 + SH])
    conv_full = g_cw.transpose(1, 0, 2).reshape(cfg.KC, cfg.CD)
    shards = [w_attn_branch[0].astype(BF16), w_ssm_branch[0].astype(BF16), w_out[0].astype(BF16)]

    loss_p, grad_x, small, recv = local_step(
        cfg, x[0], loss_target[0], norm_w, conv_full, conv_b, dt_bias, a_log, d_skip,
        ssm_norm_w, final_norm_w, w_main, w_dt, shards, dt0)

    upd = {}
    upd['w_in'] = adamw(recv['w_in'], w_in[0], m_w_in[0], v_w_in[0], True, "adamw_w_in")
    upd['w_attn_branch'] = adamw(recv['w_attn'], w_attn_branch[0], m_w_attn_branch[0], v_w_attn_branch[0], True,
                                 "adamw_w_attn")
    upd['w_ssm_branch'] = adamw(recv['w_ssm'], w_ssm_branch[0], m_w_ssm_branch[0], v_w_ssm_branch[0], True,
                                "adamw_w_ssm")
    upd['w_out'] = adamw(recv['w_out'], w_out[0], m_w_out[0], v_w_out[0], True, "adamw_w_out")

    extra = [jnp.zeros((cfg.KC, cfg.CD), F32), jnp.zeros((1, 1), F32)]
    shapes = [small[n].shape for n in SMALL] + [e.shape for e in extra]
    part = _pack([small[n] for n in SMALL] + [small['conv_w'], loss_p[:, :1]])
    gathered, = exchange([part], [True], "gather_small")
    given = dict(norm_w=(norm_w, m_norm_w, v_norm_w), conv_b=(conv_b, m_conv_b, v_conv_b),
                 dt_bias=(dt_bias, m_dt_bias, v_dt_bias), a_log=(a_log, m_a_log, v_a_log),
                 d_skip=(d_skip, m_d_skip, v_d_skip), ssm_norm_w=(ssm_norm_w, m_ssm_norm_w, v_ssm_norm_w),
                 final_norm_w=(final_norm_w, m_final_norm_w, v_final_norm_w))
    packed = [_pack([given[n][t] for n in SMALL] + extra) for t in range(3)]
    outs = adamw(gathered, *packed, True, "adamw_small", tr=part.shape[0])
    unpacked = [_unpack(o, shapes) for o in outs]
    for i, n in enumerate(SMALL):
        upd[n] = [u[i].reshape(given[n][0].shape) for u in unpacked]
    loss = unpacked[0][-1].reshape(())
    cw = conv_w.shape[-1]
    g_cw_mine = lax.dynamic_slice_in_dim(unpacked[0][-2], me * cw, cw, axis=1)
    upd['conv_w'] = adamw(g_cw_mine.reshape(-1, LANES), conv_w.reshape(-1, LANES), m_conv_w.reshape(-1, LANES),
                          v_conv_w.reshape(-1, LANES), False, "adamw_conv_w")

    order = ['norm_w', 'w_in', 'conv_w', 'conv_b', 'dt_bias', 'a_log', 'd_skip', 'ssm_norm_w', 'w_attn_branch',
             'w_ssm_branch', 'w_out', 'final_norm_w']
    like = dict(norm_w=norm_w, w_in=w_in, conv_w=conv_w, conv_b=conv_b, dt_bias=dt_bias, a_log=a_log, d_skip=d_skip,
                ssm_norm_w=ssm_norm_w, w_attn_branch=w_attn_branch, w_ssm_branch=w_ssm_branch, w_out=w_out,
                final_norm_w=final_norm_w)
    result = [loss, grad_x[None]]
    for t in range(4):
        result += [upd[n][t].reshape(like[n].shape) for n in order]
    return tuple(result)
```

```python
import functools
import math
from typing import NamedTuple

import jax
import jax.numpy as jnp
from jax import lax
from jax.experimental import pallas as pl
from jax.experimental.pallas import tpu as pltpu

F32 = jnp.float32
BF16 = jnp.bfloat16
RMS_EPS = 1e-6
NEG = -1e30
N_DEV = 8
LANES = 128
ATTN_BLOCK = 128
ADAM_LR, ADAM_B1, ADAM_B2, ADAM_EPS, ADAM_WD, ADAM_STEP = 0.001, 0.9, 0.999, 1e-08, 0.01, 10
VMEM_LIMIT = 56 * 1024 * 1024


class Cfg(NamedTuple):
    D: int = 2048
    S: int = 8192
    AH: int = 16
    E: int = 128
    HB: int = 4
    patterns: tuple = ((128, 1), (512, 4), (2048, 16))
    SI: int = 4096
    P: int = 64
    SG: int = 8
    SN: int = 128
    KC: int = 4
    L: int = 128

    @property
    def AW(self): return self.AH * self.E
    @property
    def SH(self): return self.SI // self.P
    @property
    def HPG(self): return self.SH // self.SG
    @property
    def GN(self): return self.SG * self.SN
    @property
    def CD(self): return self.SI + 2 * self.GN
    @property
    def k0(self): return self.AW
    @property
    def v0(self): return 2 * self.AW
    @property
    def za0(self): return 3 * self.AW
    @property
    def zs0(self): return 4 * self.AW
    @property
    def xbc0(self): return 4 * self.AW + self.SI
    @property
    def ga0(self): return self.xbc0 + self.CD
    @property
    def gs0(self): return self.ga0 + self.D
    @property
    def NP(self): return self.gs0 + self.D
    @property
    def N_IN(self): return self.NP + self.SH


CFG = Cfg()


def _cp(sem=None, vmem=VMEM_LIMIT):
    return pltpu.CompilerParams(dimension_semantics=sem, vmem_limit_bytes=vmem)


def _sigmoid(z):
    return 1.0 / (1.0 + jnp.exp(-z))


def _dot(a, b, dims):
    return lax.dot_general(a, b, (dims, ((), ())), preferred_element_type=F32)


NN = ((1,), (0,))
NT = ((1,), (1,))
TN = ((0,), (0,))


def _blk(off, width):
    assert off % width == 0, (off, width)
    return off // width


def matmul(a, b, mode, tm, tn, tk, out_dtype, name, side=None):
    if mode == 'nn':
        (M, K), (_, N) = a.shape, b.shape
    elif mode == 'nt':
        (M, K), (N, _) = a.shape, b.shape
    else:
        (K, M), (_, N) = a.shape, b.shape
    tm, tn, tk = min(tm, M), min(tn, N), min(tk, K)
    assert M % tm == 0 and N % tn == 0 and K % tk == 0, (M, N, K, tm, tn, tk)
    nk = K // tk
    dims = {'nn': NN, 'nt': NT, 'tn': TN}[mode]

    def body(a_ref, b_ref, o_ref, *acc):
        part = _dot(a_ref[...].astype(BF16), b_ref[...].astype(BF16), dims)
        if nk == 1:
            o_ref[...] = part.astype(out_dtype)
        else:
            acc_ref, = acc
            k = pl.program_id(2)

            @pl.when(k == 0)
            def _():
                acc_ref[...] = part

            @pl.when(k > 0)
            def _():
                acc_ref[...] += part

            @pl.when(k == nk - 1)
            def _():
                o_ref[...] = acc_ref[...].astype(out_dtype)

    if mode == 'tn':
        a_spec = pl.BlockSpec((tk, tm), lambda n, m, k: (k, m))
    else:
        a_spec = pl.BlockSpec((tm, tk), lambda n, m, k: (m, k))
    if mode == 'nt':
        b_spec = pl.BlockSpec((tn, tk), lambda n, m, k: (n, k))
    else:
        b_spec = pl.BlockSpec((tk, tn), lambda n, m, k: (k, n))
    grid = (N // tn, M // tm, nk)
    o_spec = pl.BlockSpec((tm, tn), lambda n, m, k: (m, n))
    o_shape = jax.ShapeDtypeStruct((M, N), out_dtype)
    acc = [] if nk == 1 else [pltpu.VMEM((tm, tn), F32)]
    if side is None:
        return pl.pallas_call(
            body, name=name, grid=grid, in_specs=[a_spec, b_spec], out_specs=o_spec, out_shape=o_shape,
            scratch_shapes=acc, compiler_params=_cp(("parallel", "parallel", "arbitrary")),
        )(a, b)
    arrs, gathers = side
    whole = pl.BlockSpec(memory_space=pl.ANY)
    res = pl.pallas_call(
        with_exchange(body, 2, 1, gathers, grid), name=name, grid=grid,
        in_specs=[a_spec, b_spec] + [whole] * len(arrs), out_specs=[o_spec] + [whole] * len(arrs),
        out_shape=[o_shape] + _exchange_shapes(arrs, gathers),
        scratch_shapes=acc + _exchange_sems(len(arrs)),
        compiler_params=_cp(("arbitrary", "arbitrary", "arbitrary")),
    )(a, b, *arrs)
    return res[0], res[1:]


def rmsnorm_fwd(x, w, name, tm=256):
    S, D = x.shape

    def body(x_ref, w_ref, o_ref):
        xv = x_ref[...]
        r = lax.rsqrt(jnp.mean(xv * xv, axis=-1, keepdims=True) + RMS_EPS)
        o_ref[...] = ((xv * r) * w_ref[...]).astype(BF16)

    return pl.pallas_call(
        body, name=name, grid=(S // tm,),
        in_specs=[pl.BlockSpec((tm, D), lambda i: (i, 0)), pl.BlockSpec((1, D), lambda i: (0, 0))],
        out_specs=pl.BlockSpec((tm, D), lambda i: (i, 0)),
        out_shape=jax.ShapeDtypeStruct((S, D), BF16),
        compiler_params=_cp(("parallel",)),
    )(x, w)


def rmsnorm_bwd(dh_a, dh_b, x, w, dout, name, tm=128):
    S, D = x.shape

    def body(da_ref, db_ref, x_ref, w_ref, do_ref, gx_ref, gw_ref):
        xv = x_ref[...]
        dh = da_ref[...] + db_ref[...]
        r = lax.rsqrt(jnp.mean(xv * xv, axis=-1, keepdims=True) + RMS_EPS)
        g = dh * w_ref[...]
        dx = r * g - xv * (r * r * r) * jnp.mean(g * xv, axis=-1, keepdims=True)
        gx_ref[...] = do_ref[...] + dx
        gw = jnp.sum(dh * (xv * r), axis=0, keepdims=True)

        @pl.when(pl.program_id(0) == 0)
        def _():
            gw_ref[...] = gw

        @pl.when(pl.program_id(0) > 0)
        def _():
            gw_ref[...] += gw

    row = pl.BlockSpec((tm, D), lambda i: (i, 0))
    vec = pl.BlockSpec((1, D), lambda i: (0, 0))
    return pl.pallas_call(
        body, name=name, grid=(S // tm,),
        in_specs=[row, row, row, vec, row],
        out_specs=[row, vec],
        out_shape=[jax.ShapeDtypeStruct((S, D), F32), jax.ShapeDtypeStruct((1, D), F32)],
        compiler_params=_cp(("arbitrary",)),
    )(dh_a, dh_b, x, w, dout)


def final_fwd_bwd(x, res, fw, tgt, name, tm=128):
    S, D = x.shape

    def body(x_ref, r_ref, w_ref, t_ref, do_ref, loss_ref, gw_ref):
        out = x_ref[...] + r_ref[...]
        w = w_ref[...]
        r = lax.rsqrt(jnp.mean(out * out, axis=-1, keepdims=True) + RMS_EPS)
        yn = out * r
        err = yn * w - t_ref[...]
        lrow = 0.5 * jnp.mean(err * err, axis=-1, keepdims=True)
        lsum = jnp.zeros((1, LANES), F32) + jnp.sum(lrow, axis=0, keepdims=True)
        dfin = err * (1.0 / D)
        g = dfin * w
        do_ref[...] = r * g - out * (r * r * r) * jnp.mean(g * out, axis=-1, keepdims=True)
        gw = jnp.sum(dfin * yn, axis=0, keepdims=True)

        @pl.when(pl.program_id(0) == 0)
        def _():
            gw_ref[...] = gw
            loss_ref[...] = lsum

        @pl.when(pl.program_id(0) > 0)
        def _():
            gw_ref[...] += gw
            loss_ref[...] += lsum

    row = pl.BlockSpec((tm, D), lambda i: (i, 0))
    vec = pl.BlockSpec((1, D), lambda i: (0, 0))
    return pl.pallas_call(
        body, name=name, grid=(S // tm,),
        in_specs=[row, row, vec, row],
        out_specs=[row, pl.BlockSpec((1, LANES), lambda i: (0, 0)), vec],
        out_shape=[jax.ShapeDtypeStruct((S, D), F32), jax.ShapeDtypeStruct((1, LANES), F32),
                   jax.ShapeDtypeStruct((1, D), F32)],
        compiler_params=_cp(("arbitrary",)),
    )(x, res, fw, tgt)


def merge_fwd(a_out, s_out, proj, cfg, name, tm=256):
    S, D = a_out.shape

    def body(a_ref, s_ref, ga_ref, gs_ref, o_ref):
        o_ref[...] = (_sigmoid(ga_ref[...]) * a_ref[...] + _sigmoid(gs_ref[...]) * s_ref[...]).astype(BF16)

    row = pl.BlockSpec((tm, D), lambda i: (i, 0))
    return pl.pallas_call(
        body, name=name, grid=(S // tm,),
        in_specs=[row, row, pl.BlockSpec((tm, D), lambda i: (i, _blk(cfg.ga0, D))),
                  pl.BlockSpec((tm, D), lambda i: (i, _blk(cfg.gs0, D)))],
        out_specs=row, out_shape=jax.ShapeDtypeStruct((S, D), BF16),
        compiler_params=_cp(("parallel",)),
    )(a_out, s_out, proj, proj)


def merge_bwd(dm, a_out, s_out, proj, cfg, name, tm=128):
    S, D = dm.shape

    def body(dm_ref, a_ref, s_ref, ga_ref, gs_ref, da_ref, ds_ref, dga_ref, dgs_ref):
        dmv = dm_ref[...]
        sa = _sigmoid(ga_ref[...])
        ss = _sigmoid(gs_ref[...])
        da_ref[...] = (dmv * sa).astype(BF16)
        ds_ref[...] = (dmv * ss).astype(BF16)
        dga_ref[...] = (dmv * a_ref[...] * (sa * (1.0 - sa))).astype(BF16)
        dgs_ref[...] = (dmv * s_ref[...] * (ss * (1.0 - ss))).astype(BF16)

    row = pl.BlockSpec((tm, D), lambda i: (i, 0))
    sh = jax.ShapeDtypeStruct((S, D), BF16)
    return pl.pallas_call(
        body, name=name, grid=(S // tm,),
        in_specs=[row, row, row, pl.BlockSpec((tm, D), lambda i: (i, _blk(cfg.ga0, D))),
                  pl.BlockSpec((tm, D), lambda i: (i, _blk(cfg.gs0, D)))],
        out_specs=[row] * 4, out_shape=[sh] * 4,
        compiler_params=_cp(("parallel",)),
    )(dm, a_out, s_out, proj, proj)


def _attn_rows(base, d):
    return pl.ds(base, ATTN_BLOCK) if d == 1 else pl.ds(base, ATTN_BLOCK, stride=d)


def _attn_units(cfg):
    dmax = max(d for _, d in cfg.patterns)
    units = []
    for p, (window, d) in enumerate(cfg.patterns):
        assert window // d == ATTN_BLOCK and dmax % d == 0
        nsub = dmax // d
        for b in range(nsub):
            for r in range(d):
                base = b * ATTN_BLOCK * d + r
                if b > 0:
                    units.append((p, d, base, (b - 1) * ATTN_BLOCK * d + r, False))
                else:
                    units.append((p, d, base, (nsub - 1) * ATTN_BLOCK * d + r, True))
    return units, ATTN_BLOCK * dmax


def _set_bias_tiles(bias_s, slope, cfg):
    qi = lax.broadcasted_iota(jnp.int32, (ATTN_BLOCK, ATTN_BLOCK), 0)
    ki = lax.broadcasted_iota(jnp.int32, (ATTN_BLOCK, ATTN_BLOCK), 1)
    for p, (_, d) in enumerate(cfg.patterns):
        bias_s[2 * p] = jnp.where(ki >= qi, (-slope) * ((ATTN_BLOCK + qi - ki) * d).astype(F32), NEG)
        bias_s[2 * p + 1] = jnp.where(ki <= qi, (-slope) * ((qi - ki) * d).astype(F32), NEG)


def _unit_scores(q, kp, kc, bias_s, p, prev_ok, scale):
    sp = _dot(q, kp, NT) * scale + bias_s[2 * p]
    if prev_ok is not None:
        sp = jnp.where(prev_ok, sp, NEG)
    return sp, _dot(q, kc, NT) * scale + bias_s[2 * p + 1]


def _slope_table(cfg):
    slopes = jnp.asarray([2.0 ** (-8.0 * (h + 1) / cfg.AH) for h in range(cfg.AH)], F32)
    return jnp.broadcast_to(slopes.reshape(cfg.AH, 1, 1), (cfg.AH, 8, LANES))


def attn_fused_fwd(proj, slopes, cfg, name):
    S, E, AH = cfg.S, cfg.E, cfg.AH
    units, SB = _attn_units(cfg)
    assert S % SB == 0
    npat = len(cfg.patterns)
    scale = E ** -0.5

    def spec(off, prev):
        c0 = _blk(off, E)
        if prev:
            return pl.BlockSpec((SB, E), lambda h, i: (jnp.maximum(i - 1, 0), c0 + h))
        return pl.BlockSpec((SB, E), lambda h, i: (i, c0 + h))

    def body(q_ref, kp_ref, kc_ref, vp_ref, vc_ref, z_ref, sl_ref, oa_ref, om_ref, lt_ref, *scr):
        o_s, l_s, bias_s = scr[:npat], scr[npat:2 * npat], scr[2 * npat]
        i = pl.program_id(1)

        @pl.when(i == 0)
        def _():
            _set_bias_tiles(bias_s, sl_ref[0, 0:1, :], cfg)

        for p, d, base, pbase, from_prev in units:
            rows, prows = _attn_rows(base, d), _attn_rows(pbase, d)
            q = q_ref[rows, :].astype(BF16)
            kp = (kp_ref if from_prev else kc_ref)[prows, :].astype(BF16)
            vp = (vp_ref if from_prev else vc_ref)[prows, :].astype(BF16)
            sp, sc = _unit_scores(q, kp, kc_ref[rows, :].astype(BF16), bias_s, p, (i > 0) if from_prev else None, scale)
            m = jnp.max(jnp.maximum(sp, sc), axis=1, keepdims=True)
            pp = jnp.exp(sp - m)
            pc = jnp.exp(sc - m)
            l = jnp.sum(pp + pc, axis=1, keepdims=True)
            o_s[p][rows, :] = (_dot(pp.astype(BF16), vp, NN) +
                               _dot(pc.astype(BF16), vc_ref[rows, :].astype(BF16), NN)) * (1.0 / l)
            l_s[p][rows, :] = m + jnp.log(l)
        ls = [l_s[p][...] for p in range(npat)]
        m = functools.reduce(jnp.maximum, ls)
        lt = m + jnp.log(sum(jnp.exp(l_ - m) for l_ in ls))
        lt_ref[...] = lt
        mix = sum(jnp.exp(ls[p] - lt) * o_s[p][...] for p in range(npat))
        om_ref[...] = mix
        z = z_ref[...]
        oa_ref[...] = (mix * (z * _sigmoid(z))).astype(BF16)

    out = pl.BlockSpec((SB, E), lambda h, i: (i, h))
    return pl.pallas_call(
        body, name=name, grid=(AH, S // SB),
        in_specs=[spec(0, False), spec(cfg.k0, True), spec(cfg.k0, False), spec(cfg.v0, True), spec(cfg.v0, False),
                  spec(cfg.za0, False), pl.BlockSpec((1, 8, LANES), lambda h, i: (h, 0, 0))],
        out_specs=[out, out, pl.BlockSpec((SB, 1), lambda h, i: (h * (S // SB) + i, 0))],
        out_shape=[jax.ShapeDtypeStruct((S, cfg.AW), BF16), jax.ShapeDtypeStruct((S, cfg.AW), F32),
                   jax.ShapeDtypeStruct((AH * S, 1), F32)],
        scratch_shapes=[pltpu.VMEM((SB, E), F32)] * npat + [pltpu.VMEM((SB, 1), F32)] * npat +
        [pltpu.VMEM((2 * npat, ATTN_BLOCK, ATTN_BLOCK), F32)],
        compiler_params=_cp(("parallel", "arbitrary")),
    )(proj, proj, proj, proj, proj, proj, slopes)


def attn_fused_bwd(proj, do_a, o_mix, ltot, slopes, cfg, name):
    S, E, AH = cfg.S, cfg.E, cfg.AH
    units, SB = _attn_units(cfg)
    nsb = S // SB
    last = nsb - 1
    scale = E ** -0.5

    def spec(off, prev):
        c0 = _blk(off, E)
        if prev:
            return pl.BlockSpec((SB, E), lambda h, i: (jnp.maximum(i - 1, 0), c0 + h))
        return pl.BlockSpec((SB, E), lambda h, i: (jnp.minimum(i, last), c0 + h))

    cur = pl.BlockSpec((SB, E), lambda h, i: (jnp.minimum(i, last), h))
    prev = pl.BlockSpec((SB, E), lambda h, i: (jnp.maximum(i - 1, 0), h))

    def body(q_ref, kp_ref, kc_ref, vp_ref, vc_ref, z_ref, doa_ref, om_ref, lt_ref, sl_ref,
             dq_ref, dk_ref, dv_ref, dz_ref, dmix_s, dl_s, dq_s, dkp_s, dvp_s, dkc_s, dvc_s, bias_s):
        i = pl.program_id(1)

        @pl.when(i == 0)
        def _():
            dkc_s[...] = jnp.zeros_like(dkc_s)
            dvc_s[...] = jnp.zeros_like(dvc_s)
            _set_bias_tiles(bias_s, sl_ref[0, 0:1, :], cfg)

        @pl.when(i < nsb)
        def _():
            z = z_ref[...]
            s = _sigmoid(z)
            doa = doa_ref[...]
            om = om_ref[...]
            dmix = doa * (z * s)
            dmix_s[...] = dmix
            dz_ref[...] = (doa * om * (s * (1.0 + z * (1.0 - s)))).astype(BF16)
            dl_s[...] = jnp.sum(dmix * om, axis=1, keepdims=True)
            dkp_s[...] = dkc_s[...]
            dvp_s[...] = dvc_s[...]
            dkc_s[...] = jnp.zeros_like(dkc_s)
            dvc_s[...] = jnp.zeros_like(dvc_s)
            dq_s[...] = jnp.zeros_like(dq_s)
            for p, d, base, pbase, from_prev in units:
                rows, prows = _attn_rows(base, d), _attn_rows(pbase, d)
                q = q_ref[rows, :].astype(BF16)
                kc = kc_ref[rows, :].astype(BF16)
                kp = (kp_ref if from_prev else kc_ref)[prows, :].astype(BF16)
                vp = (vp_ref if from_prev else vc_ref)[prows, :].astype(BF16)
                do = dmix_s[rows, :].astype(BF16)
                lt = lt_ref[rows, :]
                dlt = dl_s[rows, :]
                sp, sc = _unit_scores(q, kp, kc, bias_s, p, (i > 0) if from_prev else None, scale)
                pp = jnp.exp(sp - lt)
                pc = jnp.exp(sc - lt)
                dsp = (pp * (_dot(do, vp, NT) - dlt) * scale).astype(BF16)
                dsc = (pc * (_dot(do, vc_ref[rows, :].astype(BF16), NT) - dlt) * scale).astype(BF16)
                dq_s[rows, :] += _dot(dsp, kp, NN) + _dot(dsc, kc, NN)
                dkc_s[rows, :] += _dot(dsc, q, TN)
                dvc_s[rows, :] += _dot(pc.astype(BF16), do, TN)
                dk_t, dv_t = (dkp_s, dvp_s) if from_prev else (dkc_s, dvc_s)
                dk_t[prows, :] += _dot(dsp, q, TN)
                dv_t[prows, :] += _dot(pp.astype(BF16), do, TN)
            dq_ref[...] = dq_s[...].astype(BF16)
            dk_ref[...] = dkp_s[...].astype(BF16)
            dv_ref[...] = dvp_s[...].astype(BF16)

        @pl.when(i == nsb)
        def _():
            dk_ref[...] = dkc_s[...].astype(BF16)
            dv_ref[...] = dvc_s[...].astype(BF16)

    sh = jax.ShapeDtypeStruct((S, cfg.AW), BF16)
    acc = pltpu.VMEM((SB, E), F32)
    return pl.pallas_call(
        body, name=name, grid=(AH, nsb + 1),
        in_specs=[spec(0, False), spec(cfg.k0, True), spec(cfg.k0, False), spec(cfg.v0, True), spec(cfg.v0, False),
                  spec(cfg.za0, False), cur, cur,
                  pl.BlockSpec((SB, 1), lambda h, i: (h * nsb + jnp.minimum(i, last), 0)),
                  pl.BlockSpec((1, 8, LANES), lambda h, i: (h, 0, 0))],
        out_specs=[cur, prev, prev, cur], out_shape=[sh] * 4,
        scratch_shapes=[acc, pltpu.VMEM((SB, 1), F32), acc, acc, acc, acc, acc,
                        pltpu.VMEM((2 * len(cfg.patterns), ATTN_BLOCK, ATTN_BLOCK), F32)],
        compiler_params=_cp(("parallel", "arbitrary")),
    )(proj, proj, proj, proj, proj, proj, do_a, o_mix, ltot, slopes)


HALO = 8


def _conv_taps(x_ref, h_ref, kc):
    x = x_ref[...]
    full = jnp.concatenate([jnp.where(pl.program_id(1) == 0, 0.0, h_ref[...]), x], axis=0)
    return [pltpu.roll(full, s, axis=0)[HALO:, :] for s in range(kc - 1, 0, -1)] + [x]


def _conv_pre(taps, w_ref, b_ref):
    pre = b_ref[...] + w_ref[0:1, :] * taps[0]
    for k in range(1, len(taps)):
        pre = pre + w_ref[k:k + 1, :] * taps[k]
    return pre


def conv_fwd(proj, w, b, cfg, name, tm=512, tc=512):
    S, CD, KC = cfg.S, cfg.CD, cfg.KC
    tc = min(tc, CD)
    c0 = _blk(cfg.xbc0, tc)
    hb = tm // HALO

    def body(x_ref, h_ref, w_ref, b_ref, o_ref):
        pre = _conv_pre(_conv_taps(x_ref, h_ref, KC), w_ref, b_ref)
        o_ref[...] = pre * _sigmoid(pre)

    return pl.pallas_call(
        body, name=name, grid=(CD // tc, S // tm),
        in_specs=[pl.BlockSpec((tm, tc), lambda c, i: (i, c0 + c)),
                  pl.BlockSpec((HALO, tc), lambda c, i: (jnp.maximum(i * hb - 1, 0), c0 + c)),
                  pl.BlockSpec((KC, tc), lambda c, i: (0, c)),
                  pl.BlockSpec((1, tc), lambda c, i: (0, c))],
        out_specs=pl.BlockSpec((tm, tc), lambda c, i: (i, c)),
        out_shape=jax.ShapeDtypeStruct((S, CD), F32),
        compiler_params=_cp(("parallel", "arbitrary")),
    )(proj, proj, w, b)


def conv_bwd_a(proj, dxc, w, b, cfg, name, c_off, tm=512, tc=512):
    S, KC = cfg.S, cfg.KC
    CD = dxc.shape[1]
    tc = min(tc, CD)
    c0 = _blk(cfg.xbc0 + c_off, tc)
    w0 = _blk(c_off, tc)
    hb = tm // HALO

    def body(x_ref, h_ref, d_ref, w_ref, b_ref, dp_ref, gw_ref, gb_ref):
        taps = _conv_taps(x_ref, h_ref, KC)
        pre = _conv_pre(taps, w_ref, b_ref)
        s = _sigmoid(pre)
        dpre = d_ref[...] * (s * (1.0 + pre * (1.0 - s)))
        dp_ref[...] = dpre
        gb = jnp.sum(dpre, axis=0, keepdims=True)
        gws = [jnp.sum(dpre * taps[k], axis=0, keepdims=True) for k in range(KC)]
        gw = jnp.concatenate(gws + [jnp.zeros((8 - KC, tc), F32)], axis=0)

        @pl.when(pl.program_id(1) == 0)
        def _():
            gw_ref[...] = gw
            gb_ref[...] = gb

        @pl.when(pl.program_id(1) > 0)
        def _():
            gw_ref[...] += gw
            gb_ref[...] += gb

    return pl.pallas_call(
        body, name=name, grid=(CD // tc, S // tm),
        in_specs=[pl.BlockSpec((tm, tc), lambda c, i: (i, c0 + c)),
                  pl.BlockSpec((HALO, tc), lambda c, i: (jnp.maximum(i * hb - 1, 0), c0 + c)),
                  pl.BlockSpec((tm, tc), lambda c, i: (i, c)),
                  pl.BlockSpec((KC, tc), lambda c, i: (0, w0 + c)),
                  pl.BlockSpec((1, tc), lambda c, i: (0, w0 + c))],
        out_specs=[pl.BlockSpec((tm, tc), lambda c, i: (i, c)),
                   pl.BlockSpec((8, tc), lambda c, i: (0, c)),
                   pl.BlockSpec((1, tc), lambda c, i: (0, c))],
        out_shape=[jax.ShapeDtypeStruct((S, CD), F32), jax.ShapeDtypeStruct((8, CD), F32),
                   jax.ShapeDtypeStruct((1, CD), F32)],
        compiler_params=_cp(("parallel", "arbitrary")),
    )(proj, proj, dxc, w, b)


def conv_bwd_b(dpre, w, cfg, name, c_off, tm=512, tc=512):
    S, KC = cfg.S, cfg.KC
    CD = dpre.shape[1]
    tc = min(tc, CD)
    w0 = _blk(c_off, tc)
    hb = tm // HALO
    nrb = S // tm
    last_h = S // HALO - 1

    def body(d_ref, h_ref, w_ref, o_ref):
        d = d_ref[...]
        full = jnp.concatenate([d, jnp.where(pl.program_id(1) == nrb - 1, 0.0, h_ref[...])], axis=0)
        acc = w_ref[KC - 1:KC, :] * d
        for j in range(1, KC):
            acc = acc + w_ref[KC - 1 - j:KC - j, :] * pltpu.roll(full, tm + HALO - j, axis=0)[:tm, :]
        o_ref[...] = acc.astype(BF16)

    return pl.pallas_call(
        body, name=name, grid=(CD // tc, nrb),
        in_specs=[pl.BlockSpec((tm, tc), lambda c, i: (i, c)),
                  pl.BlockSpec((HALO, tc), lambda c, i: (jnp.minimum((i + 1) * hb, last_h), c)),
                  pl.BlockSpec((KC, tc), lambda c, i: (0, w0 + c))],
        out_specs=pl.BlockSpec((tm, tc), lambda c, i: (i, c)),
        out_shape=jax.ShapeDtypeStruct((S, CD), BF16),
        compiler_params=_cp(("parallel", "arbitrary")),
    )(dpre, dpre, w)


def _pad_lanes(v, width=LANES):
    return jnp.pad(v, ((0, 0), (0, width - v.shape[1])))


def ssd_prep(dt_raw, dt_bias, a_log, cfg, name):
    S, L = cfg.S, cfg.L

    def body(x_ref, b_ref, al_ref, dt_ref, ac_ref):
        x = x_ref[...] + b_ref[...]
        dt = jnp.maximum(x, 0.0) + jnp.log(1.0 + jnp.exp(-jnp.abs(x)))
        da = dt * (-jnp.exp(al_ref[...]))
        li = lax.broadcasted_iota(jnp.int32, (L, L), 0)
        si = lax.broadcasted_iota(jnp.int32, (L, L), 1)
        tri = jnp.where(li >= si, 1.0, 0.0).astype(F32)
        dt_ref[...] = dt
        ac_ref[...] = lax.dot_general(tri, da, ((NN), ((), ())), precision=lax.Precision.HIGHEST,
                                      preferred_element_type=F32)

    row = pl.BlockSpec((L, LANES), lambda i: (i, 0))
    vec = pl.BlockSpec((1, LANES), lambda i: (0, 0))
    sh = jax.ShapeDtypeStruct((S, LANES), F32)
    return pl.pallas_call(
        body, name=name, grid=(S // L,), in_specs=[row, vec, vec], out_specs=[row, row], out_shape=[sh, sh],
        compiler_params=_cp(("parallel",)),
    )(dt_raw, dt_bias, a_log)


def _spread(v, n):
    return jnp.broadcast_to(v[:, :, None], v.shape + (n,)).reshape(v.shape[0], v.shape[1] * n)


def _head_selectors(cfg):
    def sel(width):
        head = jnp.arange(LANES)[None, :, None]
        slot = jnp.arange(cfg.SG)[:, None, None] * cfg.HPG + (jnp.arange(cfg.HPG * width) // width)[None, None, :]
        return (head == slot).astype(BF16)
    return sel(cfg.P), sel(LANES)


def _spread_heads(v, sel):
    hi = v.astype(BF16)
    r1 = v - hi.astype(F32)
    mid = r1.astype(BF16)
    lo = (r1 - mid.astype(F32)).astype(BF16)
    return _dot(hi, sel, NN) + _dot(mid, sel, NN) + _dot(lo, sel, NN)


def _pair_select(halves, p):
    low = lax.broadcasted_iota(jnp.int32, halves[0].shape, 1) < p
    return jnp.where(low, halves[0], halves[1])


def _head_rows(row, hpg, p):
    return jnp.concatenate([jnp.broadcast_to(row[:, j * LANES:(j + 1) * LANES], (p, LANES)) for j in range(hpg)],
                           axis=0)


def _segment_sums(t, sel):
    hi = t.astype(BF16)
    lo = (t - hi.astype(F32)).astype(BF16)
    return _dot(hi, sel, NT) + _dot(lo, sel, NT)


def ssd_scan_fwd(xc, dt, acum, act, sel_p, sel_l, cfg, name):
    S, L, P, SN, HPG, SG, SI = cfg.S, cfg.L, cfg.P, cfg.SN, cfg.HPG, cfg.SG, cfg.SI
    nc = S // L
    GW = HPG * P
    bcol, ccol = _blk(SI, SN), _blk(SI + cfg.GN, SN)

    def body(xs_ref, b_ref, c_ref, dtn_ref, acn_ref, at_ref, sp_ref, sl_ref, y_ref, st_ref, st):
        @pl.when(pl.program_id(1) == 0)
        def _():
            st[...] = jnp.zeros_like(st)

        acn = acn_ref[...]
        dts = _spread_heads(dtn_ref[...], sp_ref[0])
        a_p = _spread_heads(acn, sp_ref[0])
        acs = _spread_heads(acn, sl_ref[0])
        s0 = st[...]
        st_ref[0] = s0.reshape(HPG, P, SN)
        B = b_ref[...].astype(BF16)
        C = c_ref[...].astype(BF16)
        G = _dot(C, B, NT)
        causal = lax.broadcasted_iota(jnp.int32, (L, L), 0) >= lax.broadcasted_iota(jnp.int32, (L, L), 1)
        xdt = xs_ref[...] * dts
        xdtb = xdt.astype(BF16)
        ws = jnp.exp(a_p[L - 1:L, :] - a_p)
        yo = jnp.exp(a_p) * _dot(C, s0.astype(BF16), NT)
        yd = []
        for jp in range(HPG // 2):
            x_pair = xdtb[:, jp * LANES:(jp + 1) * LANES]
            halves = []
            for j in (2 * jp, 2 * jp + 1):
                dm = jnp.where(causal, jnp.exp(acs[:, j * LANES:(j + 1) * LANES] - at_ref[j:j + 1, :]), 0.0)
                halves.append(_dot((G * dm).astype(BF16), x_pair, NN))
            yd.append(_pair_select(halves, P))
        y_ref[...] = jnp.concatenate(yd, axis=1) + yo
        st[...] = _head_rows(jnp.exp(acs[L - 1:L, :]), HPG, P) * s0 + _dot((xdt * ws).astype(BF16), B, TN)

    y, states = pl.pallas_call(
        body, name=name, grid=(SG, nc),
        in_specs=[pl.BlockSpec((L, GW), lambda g, c: (c, g)),
                  pl.BlockSpec((L, SN), lambda g, c: (c, bcol + g)),
                  pl.BlockSpec((L, SN), lambda g, c: (c, ccol + g)),
                  pl.BlockSpec((L, LANES), lambda g, c: (c, 0)),
                  pl.BlockSpec((L, LANES), lambda g, c: (c, 0)),
                  pl.BlockSpec((HPG, L), lambda g, c: (g, c)),
                  pl.BlockSpec((1, LANES, GW), lambda g, c: (g, 0, 0)),
                  pl.BlockSpec((1, LANES, HPG * LANES), lambda g, c: (g, 0, 0))],
        out_specs=[pl.BlockSpec((L, GW), lambda g, c: (c, g)),
                   pl.BlockSpec((1, HPG, P, SN), lambda g, c: (c, g, 0, 0))],
        out_shape=[jax.ShapeDtypeStruct((S, SI), F32), jax.ShapeDtypeStruct((nc, cfg.SH, P, SN), F32)],
        scratch_shapes=[pltpu.VMEM((GW, SN), F32)],
        compiler_params=_cp(("parallel", "arbitrary")),
    )(xc, xc, xc, dt, acum, act, sel_p, sel_l)
    return y, states


def ssd_scan_bwd(xc, dt, acum, act, sel_p, sel_l, states, y, dy, dvec, cfg, name, side):
    S, L, P, SN, HPG, SG, SI = cfg.S, cfg.L, cfg.P, cfg.SN, cfg.HPG, cfg.SG, cfg.SI
    nc = S // L
    GW = HPG * P
    bcol, ccol = _blk(SI, SN), _blk(SI + cfg.GN, SN)

    def rc(c):
        return nc - 1 - c

    def body(xs_ref, b_ref, c_ref, dtn_ref, acn_ref, at_ref, sp_ref, sl_ref, st_ref, y_ref, dy_ref, dk_ref,
             dxs_ref, db_ref, dc_ref, dac_ref, dxsum_ref, dst):
        @pl.when(pl.program_id(1) == 0)
        def _():
            dst[...] = jnp.zeros_like(dst)

        sel = sp_ref[0]
        acn = acn_ref[...]
        dts = _spread_heads(dtn_ref[...], sel)
        a_p = _spread_heads(acn, sel)
        acs = _spread_heads(acn, sl_ref[0])
        B = b_ref[...].astype(BF16)
        C = c_ref[...].astype(BF16)
        G = _dot(C, B, NT)
        causal = lax.broadcasted_iota(jnp.int32, (L, L), 0) >= lax.broadcasted_iota(jnp.int32, (L, L), 1)
        low = lax.broadcasted_iota(jnp.int32, (L, LANES), 1) < P
        xs = xs_ref[...]
        dY = dy_ref[...]
        xdt = xs * dts
        xdtb = xdt.astype(BF16)
        dYb = dY.astype(BF16)
        s0 = st_ref[0].reshape(GW, SN)
        s0b = s0.astype(BF16)
        ds1 = dst[...]
        ds1b = ds1.astype(BF16)
        ws = jnp.exp(a_p[L - 1:L, :] - a_p)
        dR = (jnp.exp(a_p) * dY).astype(BF16)
        dX2 = ws * _dot(B, ds1b, NT)
        dgsum = jnp.zeros((L, L), F32)
        dX1, yd = [], []
        for jp in range(HPG // 2):
            lanes = slice(jp * LANES, (jp + 1) * LANES)
            x_pair, dy_pair = xdtb[:, lanes], dYb[:, lanes]
            h1, h2 = [], []
            for h, j in enumerate((2 * jp, 2 * jp + 1)):
                dm = jnp.where(causal, jnp.exp(acs[:, j * LANES:(j + 1) * LANES] - at_ref[j:j + 1, :]), 0.0)
                mine = low if h == 0 else jnp.logical_not(low)
                dgsum = dgsum + _dot(jnp.where(mine, dy_pair, jnp.zeros_like(dy_pair)), x_pair, NT) * dm
                Mb = (G * dm).astype(BF16)
                h1.append(_dot(Mb, dy_pair, TN))
                h2.append(_dot(Mb, x_pair, NN))
            dX1.append(_pair_select(h1, P))
            yd.append(_pair_select(h2, P))
        dX1 = jnp.concatenate(dX1, axis=1)
        dX = dX1 + dX2
        pair = (dYb.astype(F32) - dY) * jnp.concatenate(yd, axis=1) - xdtb.astype(F32) * dX1
        through = _segment_sums(xdt * dX2, sel)
        u = ds1 * s0
        u_hi = u.astype(BF16)
        ones = jnp.ones((16, SN), BF16)
        u_rows = _dot(ones, u_hi, NT) + _dot(ones, (u - u_hi.astype(F32)).astype(BF16), NT)
        at_end = jnp.exp(acn[L - 1:L, :]) * _segment_sums(u_rows, sel)[0:1, :] + jnp.sum(through, axis=0, keepdims=True)
        is_last = lax.broadcasted_iota(jnp.int32, (L, LANES), 0) == L - 1
        dac_ref[0] = _segment_sums(dY * y_ref[...] + pair, sel) - through + jnp.where(is_last, at_end, 0.0)
        dxsum_ref[0] = _segment_sums(dX * xs, sel)
        dxs_ref[...] = dX * dts + dk_ref[...] * dY
        dst[...] = _head_rows(jnp.exp(acs[L - 1:L, :]), HPG, P) * ds1 + _dot(dR, C, TN)
        dgb = dgsum.astype(BF16)
        dc_ref[...] = _dot(dR, s0b, NN) + _dot(dgb, B, NN)
        db_ref[...] = _dot((xdt * ws).astype(BF16), ds1b, NN) + _dot(dgb, C, TN)

    wide = pl.BlockSpec((L, GW), lambda g, c: (rc(c), g))
    colspec = pl.BlockSpec((1, L, LANES), lambda g, c: (g, rc(c), 0))
    whole = pl.BlockSpec(memory_space=pl.ANY)
    arrs, gathers = side
    grid = (SG, nc)
    res = pl.pallas_call(
        with_exchange(body, 12, 5, gathers, grid), name=name, grid=grid,
        in_specs=[wide,
                  pl.BlockSpec((L, SN), lambda g, c: (rc(c), bcol + g)),
                  pl.BlockSpec((L, SN), lambda g, c: (rc(c), ccol + g)),
                  pl.BlockSpec((L, LANES), lambda g, c: (rc(c), 0)),
                  pl.BlockSpec((L, LANES), lambda g, c: (rc(c), 0)),
                  pl.BlockSpec((HPG, L), lambda g, c: (g, rc(c))),
                  pl.BlockSpec((1, LANES, GW), lambda g, c: (g, 0, 0)),
                  pl.BlockSpec((1, LANES, HPG * LANES), lambda g, c: (g, 0, 0)),
                  pl.BlockSpec((1, HPG, P, SN), lambda g, c: (rc(c), g, 0, 0)),
                  wide, wide,
                  pl.BlockSpec((1, GW), lambda g, c: (0, g))] + [whole] * len(arrs),
        out_specs=[wide,
                   pl.BlockSpec((L, SN), lambda g, c: (rc(c), g)),
                   pl.BlockSpec((L, SN), lambda g, c: (rc(c), g)),
                   colspec, colspec] + [whole] * len(arrs),
        out_shape=[jax.ShapeDtypeStruct((S, SI), F32), jax.ShapeDtypeStruct((S, cfg.GN), F32),
                   jax.ShapeDtypeStruct((S, cfg.GN), F32),
                   jax.ShapeDtypeStruct((SG, S, LANES), F32), jax.ShapeDtypeStruct((SG, S, LANES), F32)] +
        _exchange_shapes(arrs, gathers),
        scratch_shapes=[pltpu.VMEM((GW, SN), F32)] + _exchange_sems(len(arrs)),
        compiler_params=_cp(("arbitrary", "arbitrary")),
    )(xc, xc, xc, dt, acum, act, sel_p, sel_l, states, y, dy, dvec, *arrs)
    return res[:5], res[5:]


def dt_bwd(dac, dxsum, dt_raw, dt, dt_bias, a_log, cfg, name):
    S, L, SG = cfg.S, cfg.L, cfg.SG

    def body(da_ref, dx_ref, x_ref, dt_ref, b_ref, al_ref, o_ref, gb_ref, ga_ref):
        a = -jnp.exp(al_ref[...])
        dtv = dt_ref[...]
        dxs = jnp.sum(dx_ref[...], axis=0)
        upper = jnp.where(lax.broadcasted_iota(jnp.int32, (L, L), 1) >= lax.broadcasted_iota(jnp.int32, (L, L), 0),
                          1.0, 0.0).astype(F32)
        dda = lax.dot_general(upper, jnp.sum(da_ref[...], axis=0), (NN, ((), ())), precision=lax.Precision.HIGHEST,
                              preferred_element_type=F32)
        draw = (dxs + dda * a) * _sigmoid(x_ref[...] + b_ref[...])
        o_ref[...] = draw.astype(BF16)
        gb = jnp.sum(draw, axis=0, keepdims=True)
        ga = jnp.sum(dda * dtv, axis=0, keepdims=True) * a

        @pl.when(pl.program_id(0) == 0)
        def _():
            gb_ref[...] = gb
            ga_ref[...] = ga

        @pl.when(pl.program_id(0) > 0)
        def _():
            gb_ref[...] += gb
            ga_ref[...] += ga

    row = pl.BlockSpec((L, LANES), lambda i: (i, 0))
    vec = pl.BlockSpec((1, LANES), lambda i: (0, 0))
    return pl.pallas_call(
        body, name=name, grid=(S // L,),
        in_specs=[pl.BlockSpec((SG, L, LANES), lambda i: (0, i, 0))] * 2 + [row, row, vec, vec],
        out_specs=[row, vec, vec],
        out_shape=[jax.ShapeDtypeStruct((S, LANES), BF16), jax.ShapeDtypeStruct((1, LANES), F32),
                   jax.ShapeDtypeStruct((1, LANES), F32)],
        compiler_params=_cp(("arbitrary",)),
    )(dac, dxsum, dt_raw, dt, dt_bias, a_log)


def gated_norm_fwd(y, xc, proj, dvec, nw, cfg, name, tm=128):
    S, SI = cfg.S, cfg.SI

    def body(y_ref, xs_ref, z_ref, d_ref, w_ref, o_ref):
        z = z_ref[...]
        yg = (y_ref[...] + d_ref[...] * xs_ref[...]) * (z * _sigmoid(z))
        r = lax.rsqrt(jnp.mean(yg * yg, axis=-1, keepdims=True) + RMS_EPS)
        o_ref[...] = ((yg * r) * w_ref[...]).astype(BF16)

    row = pl.BlockSpec((tm, SI), lambda i: (i, 0))
    vec = pl.BlockSpec((1, SI), lambda i: (0, 0))
    return pl.pallas_call(
        body, name=name, grid=(S // tm,),
        in_specs=[row, row, pl.BlockSpec((tm, SI), lambda i: (i, _blk(cfg.zs0, SI))), vec, vec],
        out_specs=row, out_shape=jax.ShapeDtypeStruct((S, SI), BF16),
        compiler_params=_cp(("parallel",)),
    )(y, xc, proj, dvec, nw)


def gated_norm_bwd(dyn, y, xc, proj, dvec, nw, cfg, name, tm=128):
    S, SI = cfg.S, cfg.SI

    def body(dn_ref, y_ref, xs_ref, z_ref, d_ref, w_ref, dy_ref, dz_ref, gw_ref, gd_ref):
        z = z_ref[...]
        s = _sigmoid(z)
        sz = z * s
        xs = xs_ref[...]
        yf = y_ref[...] + d_ref[...] * xs
        yg = yf * sz
        r = lax.rsqrt(jnp.mean(yg * yg, axis=-1, keepdims=True) + RMS_EPS)
        dn = dn_ref[...]
        g = dn * w_ref[...]
        dyg = r * g - yg * (r * r * r) * jnp.mean(g * yg, axis=-1, keepdims=True)
        dy = dyg * sz
        dy_ref[...] = dy
        dz_ref[...] = (dyg * yf * (s * (1.0 + z * (1.0 - s)))).astype(BF16)
        gw = jnp.sum(dn * (yg * r), axis=0, keepdims=True)
        gd = jnp.sum(dy * xs, axis=0, keepdims=True)

        @pl.when(pl.program_id(0) == 0)
        def _():
            gw_ref[...] = gw
            gd_ref[...] = gd

        @pl.when(pl.program_id(0) > 0)
        def _():
            gw_ref[...] += gw
            gd_ref[...] += gd

    row = pl.BlockSpec((tm, SI), lambda i: (i, 0))
    vec = pl.BlockSpec((1, SI), lambda i: (0, 0))
    return pl.pallas_call(
        body, name=name, grid=(S // tm,),
        in_specs=[row, row, row, pl.BlockSpec((tm, SI), lambda i: (i, _blk(cfg.zs0, SI))), vec, vec],
        out_specs=[row, row, vec, vec],
        out_shape=[jax.ShapeDtypeStruct((S, SI), F32), jax.ShapeDtypeStruct((S, SI), BF16),
                   jax.ShapeDtypeStruct((1, SI), F32), jax.ShapeDtypeStruct((1, SI), F32)],
        compiler_params=_cp(("arbitrary",)),
    )(dyn, y, xc, proj, dvec, nw)


def local_step(cfg, x, tgt, norm_w, conv_w, conv_b, dt_bias, a_log, d_skip, ssm_norm_w, final_norm_w,
               w_main, w_dt, shards, dt0):
    S, D = cfg.S, cfg.D
    slopes = _slope_table(cfg)
    dt_bias_p = _pad_lanes(dt_bias)
    a_log_p = _pad_lanes(a_log)
    dvec = _spread(d_skip, cfg.P)

    hn = rmsnorm_fwd(x, norm_w, "rmsnorm_fwd")
    proj, gathered = matmul(hn, w_main, 'nn', 1024, 1024, 2048, F32, "in_proj", side=(shards, [True] * 3))
    w_attn, w_ssm, w_out = gathered[0].reshape(cfg.AW, D), gathered[1].reshape(cfg.SI, D), gathered[2].reshape(D, D)
    dt_raw = matmul(hn, w_dt, 'nn', 512, 128, 2048, F32, "in_proj_dt")
    o_a, o_mix, ltot = attn_fused_fwd(proj, slopes, cfg, "attn_fwd")
    xc = conv_fwd(proj, conv_w, conv_b, cfg, "conv_fwd")
    dt, acum = ssd_prep(dt_raw, dt_bias_p, a_log_p, cfg, "ssd_prep")
    act = acum[:, :cfg.SH].T
    sel_p, sel_l = _head_selectors(cfg)
    y, states = ssd_scan_fwd(xc, dt, acum, act, sel_p, sel_l, cfg, "ssd_scan_fwd")
    y_n = gated_norm_fwd(y, xc, proj, dvec, ssm_norm_w, cfg, "gated_norm_fwd")
    a_out = matmul(o_a, w_attn, 'nn', 512, 1024, 2048, F32, "attn_branch")
    s_out = matmul(y_n, w_ssm, 'nn', 512, 1024, 2048, F32, "ssm_branch")
    merged = merge_fwd(a_out, s_out, proj, cfg, "merge_fwd")
    res = matmul(merged, w_out, 'nn', 512, 1024, 2048, F32, "out_proj")
    dout, loss_p, g_final_w = final_fwd_bwd(x, res, final_norm_w.reshape(1, D), tgt, "final_fwd_bwd")

    g_w_out = matmul(merged, dout, 'tn', 1024, 1024, 2048, BF16, "g_w_out")
    dmerged = matmul(dout, w_out, 'nt', 512, 1024, 2048, F32, "d_merged")
    da_out, ds_out, dga, dgs = merge_bwd(dmerged, a_out, s_out, proj, cfg, "merge_bwd")
    g_w_attn = matmul(o_a, da_out, 'tn', 1024, 1024, 2048, BF16, "g_w_attn")
    g_w_ssm = matmul(y_n, ds_out, 'tn', 1024, 1024, 2048, BF16, "g_w_ssm")
    do_a = matmul(da_out, w_attn, 'nt', 512, 1024, 2048, F32, "d_o_a")
    dyn = matmul(ds_out, w_ssm, 'nt', 512, 1024, 2048, F32, "d_y_n")
    dy, dz_s, g_ssm_norm, g_dvec = gated_norm_bwd(dyn, y, xc, proj, dvec, ssm_norm_w, cfg, "gated_norm_bwd")
    sends = [g.reshape((N_DEV, g.shape[0] // N_DEV, D)) for g in (g_w_attn, g_w_ssm, g_w_out)]
    (dxs, dB, dC, dac_g, dxsum_g), (r_attn, r_ssm, r_out) = ssd_scan_bwd(
        xc, dt, acum, act, sel_p, sel_l, states, y, dy, dvec, cfg, "ssd_scan_bwd", side=(sends, [False] * 3))
    ddt_raw, g_dt_bias, g_a_log = dt_bwd(dac_g, dxsum_g, dt_raw, dt, dt_bias_p, a_log_p, cfg, "dt_bwd")
    dxbc, g_cw, g_cb = [], [], []
    for nm, piece, c_off in (("xs", dxs, 0), ("b", dB, cfg.SI), ("c", dC, cfg.SI + cfg.GN)):
        dpre, gw, gb = conv_bwd_a(proj, piece, conv_w, conv_b, cfg, "conv_bwd_a_" + nm, c_off)
        dxbc.append(conv_bwd_b(dpre, conv_w, cfg, "conv_bwd_b_" + nm, c_off))
        g_cw.append(gw)
        g_cb.append(gb)
    g_conv_w, g_conv_b = jnp.concatenate(g_cw, axis=1), jnp.concatenate(g_cb, axis=1)
    dq, dk, dv, dz_a = attn_fused_bwd(proj, do_a, o_mix, ltot, slopes, cfg, "attn_bwd")
    dproj = jnp.concatenate([dq, dk, dv, dz_a, dz_s] + dxbc + [dga, dgs], axis=1)
    g_w_main = matmul(hn, dproj, 'tn', 1024, 1024, 2048, BF16, "g_w_main")
    g_w_dt = matmul(hn, ddt_raw, 'tn', 1024, 128, 2048, BF16, "g_w_dt")
    g_w_in = jnp.concatenate([g_w_main[:, :dt0], g_w_dt[:, :cfg.SH], g_w_main[:, dt0:]], axis=1)
    send_in = g_w_in.reshape(D, N_DEV, cfg.N_IN // N_DEV).transpose(1, 0, 2)
    dhn_a, (r_in,) = matmul(dproj, w_main, 'nt', 1024, 1024, 2048, F32, "d_hn", side=([send_in], [False]))
    dhn_b = matmul(ddt_raw, w_dt, 'nt', 512, 1024, 128, F32, "d_hn_dt")
    grad_x, g_norm_w = rmsnorm_bwd(dhn_a, dhn_b, x, norm_w, dout, "rmsnorm_bwd")

    g_d_skip = jnp.sum(g_dvec.reshape(cfg.SH, cfg.P), axis=1).reshape(1, cfg.SH)
    small = dict(norm_w=g_norm_w, conv_b=g_conv_b, dt_bias=g_dt_bias[:, :cfg.SH], a_log=g_a_log[:, :cfg.SH],
                 d_skip=g_d_skip, ssm_norm_w=g_ssm_norm, final_norm_w=g_final_w, conv_w=g_conv_w[:cfg.KC])
    return loss_p, grad_x, small, dict(w_in=r_in, w_attn=r_attn, w_ssm=r_ssm, w_out=r_out)


def _mesh_pos():
    return lax.axis_index("x"), lax.axis_index("y"), lax.axis_index("c")


def _flat(pos):
    return 4 * pos[0] + 2 * pos[1] + pos[2]


def _exchange_shapes(arrs, gathers):
    return [jax.ShapeDtypeStruct(((N_DEV,) + a.shape) if g else a.shape, a.dtype) for a, g in zip(arrs, gathers)]


def _exchange_sems(n):
    return [pltpu.SemaphoreType.DMA((n * (N_DEV - 1),)), pltpu.SemaphoreType.DMA((n * (N_DEV - 1),)),
            pltpu.SemaphoreType.DMA((n,))]


def _exchange_copies(ins, outs, gathers, send_sems, recv_sems, loc_sems):
    pos = _mesh_pos()
    me = _flat(pos)
    starts, waits = [], []
    for a in range(len(ins)):
        mine = ins[a] if gathers[a] else ins[a].at[me]
        loc = pltpu.make_async_copy(mine, outs[a].at[me], loc_sems.at[a])
        starts.append(loc)
        waits.append(loc)
        for k in range(1, N_DEV):
            flip = ((k >> 2) & 1, (k >> 1) & 1, k & 1)
            peer = tuple(1 - p if f else p for p, f in zip(pos, flip))
            pk = _flat(peer)
            src = ins[a] if gathers[a] else ins[a].at[pk]
            sems = dict(send_sem=send_sems.at[a * (N_DEV - 1) + k - 1], recv_sem=recv_sems.at[a * (N_DEV - 1) + k - 1],
                        device_id=peer, device_id_type=pl.DeviceIdType.MESH)
            starts.append(pltpu.make_async_remote_copy(src_ref=src, dst_ref=outs[a].at[me], **sems))
            waits.append(pltpu.make_async_remote_copy(src_ref=src, dst_ref=outs[a].at[pk], **sems))
    return starts, waits


def exchange(arrs, gathers, name):
    n = len(arrs)

    def body(*refs):
        starts, waits = _exchange_copies(refs[:n], refs[n:2 * n], gathers, *refs[2 * n:])
        for cp in starts:
            cp.start()
        for cp in waits:
            cp.wait()

    hbm = pl.BlockSpec(memory_space=pltpu.HBM)
    return pl.pallas_call(
        body, name=name, in_specs=[hbm] * n, out_specs=[hbm] * n, out_shape=_exchange_shapes(arrs, gathers),
        scratch_shapes=_exchange_sems(n),
    )(*arrs)


def gather_two_level(arrs, chunks, name):
    n = len(arrs)

    def body(*refs):
        ins, outs = refs[:n], refs[n:2 * n]
        send_sems, recv_sems, loc_sems = refs[2 * n:]
        x, y, c = _mesh_pos()
        me, sib = (x, y, c), (x, y, 1 - c)
        chips = [(1 - x, y), (x, 1 - y), (1 - x, 1 - y)]
        plan, base = [], 0
        for a in range(n):
            step = arrs[a].shape[0] // chunks[a]
            for q in range(chunks[a]):
                plan.append((a, pl.ds(q * step, step), base))
                base += N_DEV - 1

        def copy(a, rows, sem, block, to, own=False):
            dst = outs[a].at[_flat(block), rows]
            return pltpu.make_async_remote_copy(
                src_ref=ins[a].at[rows] if own else dst, dst_ref=dst, send_sem=send_sems.at[sem],
                recv_sem=recv_sems.at[sem], device_id=to, device_id_type=pl.DeviceIdType.MESH)

        local = [pltpu.make_async_copy(ins[a], outs[a].at[_flat(me)], loc_sems.at[a]) for a in range(n)]
        for cp in local:
            cp.start()
        sent = []
        for a, rows, s in plan:
            sent.append(copy(a, rows, s, me, sib, own=True))
            sent += [copy(a, rows, s + 1 + j, me, (*chip, c), own=True) for j, chip in enumerate(chips)]
        for cp in sent:
            cp.start()
        for a, rows, s in plan:
            for j, chip in enumerate(chips):
                copy(a, rows, s + 1 + j, (*chip, c), me).wait_recv()
                passed = copy(a, rows, s + 4 + j, (*chip, c), sib)
                passed.start()
                sent.append(passed)
        for a, rows, s in plan:
            copy(a, rows, s, sib, me).wait_recv()
            for j, chip in enumerate(chips):
                copy(a, rows, s + 4 + j, (*chip, 1 - c), me).wait_recv()
        for cp in sent:
            cp.wait_send()
        for cp in local:
            cp.wait()

    hbm = pl.BlockSpec(memory_space=pltpu.HBM)
    nsem = (N_DEV - 1) * sum(chunks)
    return pl.pallas_call(
        body, name=name, in_specs=[hbm] * n, out_specs=[hbm] * n, out_shape=_exchange_shapes(arrs, [True] * n),
        scratch_shapes=[pltpu.SemaphoreType.DMA((nsem,)), pltpu.SemaphoreType.DMA((nsem,)),
                        pltpu.SemaphoreType.DMA((n,))],
    )(*arrs)


def with_exchange(body, n_in, n_out, gathers, grid):
    n = len(gathers)

    def wrapped(*refs):
        ins, sends = refs[:n_in], refs[n_in:n_in + n]
        outs, recvs = refs[n_in + n:n_in + n + n_out], refs[n_in + 2 * n + n_out - n:n_in + 2 * n + n_out]
        scratch, sems = refs[n_in + 2 * n + n_out:-3], refs[-3:]
        ids = [pl.program_id(d) for d in range(len(grid))]
        first = functools.reduce(jnp.logical_and, [i == 0 for i in ids])
        last = functools.reduce(jnp.logical_and, [i == g - 1 for i, g in zip(ids, grid)])

        @pl.when(first)
        def _():
            for cp in _exchange_copies(sends, recvs, gathers, *sems)[0]:
                cp.start()

        body(*ins, *outs, *scratch)

        @pl.when(last)
        def _():
            for cp in _exchange_copies(sends, recvs, gathers, *sems)[1]:
                cp.wait()

    return wrapped


def adamw(g_src, w, m, v, summed, name, tr=64):
    R, C = w.shape
    tr = min(tr, R)
    assert R % tr == 0

    def body(g_ref, w_ref, m_ref, v_ref, g_out, d_out, m_out, v_out):
        if summed:
            g = g_ref[0].astype(F32)
            for j in range(1, N_DEV):
                g = g + g_ref[j].astype(F32)
        else:
            g = g_ref[...]
        mn = ADAM_B1 * m_ref[...] + (1.0 - ADAM_B1) * g
        vn = ADAM_B2 * v_ref[...] + (1.0 - ADAM_B2) * (g * g)
        m_hat = mn / (1.0 - ADAM_B1 ** ADAM_STEP)
        v_hat = vn / (1.0 - ADAM_B2 ** ADAM_STEP)
        g_out[...] = g
        d_out[...] = -ADAM_LR * (m_hat / (jnp.sqrt(v_hat) + ADAM_EPS) + ADAM_WD * w_ref[...])
        m_out[...] = mn
        v_out[...] = vn

    row = pl.BlockSpec((tr, C), lambda i: (i, 0))
    gspec = pl.BlockSpec((N_DEV, tr, C), lambda i: (0, i, 0)) if summed else row
    sh = jax.ShapeDtypeStruct((R, C), F32)
    return pl.pallas_call(
        body, name=name, grid=(R // tr,), in_specs=[gspec, row, row, row], out_specs=[row] * 4, out_shape=[sh] * 4,
        compiler_params=_cp(("parallel",)),
    )(g_src, w, m, v)


SMALL = ('norm_w', 'conv_b', 'dt_bias', 'a_log', 'd_skip', 'ssm_norm_w', 'final_norm_w')


def _rows(n):
    return -(-n // (8 * LANES)) * 8


def _pack(vals):
    parts = []
    for a in vals:
        f = a.reshape(-1)
        parts.append(jnp.pad(f, (0, _rows(f.size) * LANES - f.size)).reshape(-1, LANES))
    return jnp.concatenate(parts, axis=0)


def _unpack(packed, shapes):
    out, r = [], 0
    for s in shapes:
        n = math.prod(s)
        out.append(packed[r:r + _rows(n)].reshape(-1)[:n].reshape(s))
        r += _rows(n)
    return out


def kernel(x, norm_w, w_in, conv_w, conv_b, dt_bias, a_log, d_skip, ssm_norm_w, w_attn_branch, w_ssm_branch, w_out, final_norm_w, loss_target, m_norm_w, m_w_in, m_conv_w, m_conv_b, m_dt_bias, m_a_log, m_d_skip, m_ssm_norm_w, m_w_attn_branch, m_w_ssm_branch, m_w_out, m_final_norm_w, v_norm_w, v_w_in, v_conv_w, v_conv_b, v_dt_bias, v_a_log, v_d_skip, v_ssm_norm_w, v_w_attn_branch, v_w_ssm_branch, v_w_out, v_final_norm_w):
    cfg = CFG
    D, SH = cfg.D, cfg.SH
    me = _flat(_mesh_pos())
    dt0 = 4 * cfg.AW + cfg.SI + cfg.CD
    ws = w_in.shape[-1]

    g_in, g_cw = gather_two_level([w_in[0].astype(BF16), conv_w[0]], [4, 1], "gather_w_in")
    w_full = g_in.transpose(1, 0, 2).reshape(D, N_DEV * ws)
    w_main = jnp.concatenate([w_full[:, :dt0], w_full[:, dt0 + SH:]], axis=1)
    w_dt = _pad_lanes(w_full[:, dt0:dt0 + SH])
    conv_full = g_cw.transpose(1, 0, 2).reshape(cfg.KC, cfg.CD)
    shards = [w_attn_branch[0].astype(BF16), w_ssm_branch[0].astype(BF16), w_out[0].astype(BF16)]

    loss_p, grad_x, small, recv = local_step(
        cfg, x[0], loss_target[0], norm_w, conv_full, conv_b, dt_bias, a_log, d_skip,
        ssm_norm_w, final_norm_w, w_main, w_dt, shards, dt0)

    upd = {}
    upd['w_in'] = adamw(recv['w_in'], w_in[0], m_w_in[0], v_w_in[0], True, "adamw_w_in")
    upd['w_attn_branch'] = adamw(recv['w_attn'], w_attn_branch[0], m_w_attn_branch[0], v_w_attn_branch[0], True,
                                 "adamw_w_attn")
    upd['w_ssm_branch'] = adamw(recv['w_ssm'], w_ssm_branch[0], m_w_ssm_branch[0], v_w_ssm_branch[0], True,
                                "adamw_w_ssm")
    upd['w_out'] = adamw(recv['w_out'], w_out[0], m_w_out[0], v_w_out[0], True, "adamw_w_out")

    extra = [jnp.zeros((cfg.KC, cfg.CD), F32), jnp.zeros((1, 1), F32)]
    shapes = [small[n].shape for n in SMALL] + [e.shape for e in extra]
    part = _pack([small[n] for n in SMALL] + [small['conv_w'], loss_p[:, :1]])
    gathered, = exchange([part], [True], "gather_small")
    given = dict(norm_w=(norm_w, m_norm_w, v_norm_w), conv_b=(conv_b, m_conv_b, v_conv_b),
                 dt_bias=(dt_bias, m_dt_bias, v_dt_bias), a_log=(a_log, m_a_log, v_a_log),
                 d_skip=(d_skip, m_d_skip, v_d_skip), ssm_norm_w=(ssm_norm_w, m_ssm_norm_w, v_ssm_norm_w),
                 final_norm_w=(final_norm_w, m_final_norm_w, v_final_norm_w))
    packed = [_pack([given[n][t] for n in SMALL] + extra) for t in range(3)]
    outs = adamw(gathered, *packed, True, "adamw_small", tr=part.shape[0])
    unpacked = [_unpack(o, shapes) for o in outs]
    for i, n in enumerate(SMALL):
        upd[n] = [u[i].reshape(given[n][0].shape) for u in unpacked]
    loss = unpacked[0][-1].reshape(())
    cw = conv_w.shape[-1]
    g_cw_mine = lax.dynamic_slice_in_dim(unpacked[0][-2], me * cw, cw, axis=1)
    upd['conv_w'] = adamw(g_cw_mine.reshape(-1, LANES), conv_w.reshape(-1, LANES), m_conv_w.reshape(-1, LANES),
                          v_conv_w.reshape(-1, LANES), False, "adamw_conv_w")

    order = ['norm_w', 'w_in', 'conv_w', 'conv_b', 'dt_bias', 'a_log', 'd_skip', 'ssm_norm_w', 'w_attn_branch',
             'w_ssm_branch', 'w_out', 'final_norm_w']
    like = dict(norm_w=norm_w, w_in=w_in, conv_w=conv_w, conv_b=conv_b, dt_bias=dt_bias, a_log=a_log, d_skip=d_skip,
                ssm_norm_w=ssm_norm_w, w_attn_branch=w_attn_branch, w_ssm_branch=w_ssm_branch, w_out=w_out,
                final_norm_w=final_norm_w)
    result = [loss, grad_x[None]]
    for t in range(4):
        result += [upd[n][t].reshape(like[n].shape) for n in order]
    return tuple(result)
```

```python
import functools
import math
from typing import NamedTuple

import jax
import jax.numpy as jnp
from jax import lax
from jax.experimental import pallas as pl
from jax.experimental.pallas import tpu as pltpu

F32 = jnp.float32
BF16 = jnp.bfloat16
RMS_EPS = 1e-6
NEG = -1e30
N_DEV = 8
LANES = 128
ATTN_BLOCK = 128
ADAM_LR, ADAM_B1, ADAM_B2, ADAM_EPS, ADAM_WD, ADAM_STEP = 0.001, 0.9, 0.999, 1e-08, 0.01, 10
VMEM_LIMIT = 56 * 1024 * 1024


class Cfg(NamedTuple):
    D: int = 2048
    S: int = 8192
    AH: int = 16
    E: int = 128
    HB: int = 4
    patterns: tuple = ((128, 1), (512, 4), (2048, 16))
    SI: int = 4096
    P: int = 64
    SG: int = 8
    SN: int = 128
    KC: int = 4
    L: int = 128

    @property
    def AW(self): return self.AH * self.E
    @property
    def SH(self): return self.SI // self.P
    @property
    def HPG(self): return self.SH // self.SG
    @property
    def GN(self): return self.SG * self.SN
    @property
    def CD(self): return self.SI + 2 * self.GN
    @property
    def k0(self): return self.AW
    @property
    def v0(self): return 2 * self.AW
    @property
    def za0(self): return 3 * self.AW
    @property
    def zs0(self): return 4 * self.AW
    @property
    def xbc0(self): return 4 * self.AW + self.SI
    @property
    def ga0(self): return self.xbc0 + self.CD
    @property
    def gs0(self): return self.ga0 + self.D
    @property
    def NP(self): return self.gs0 + self.D
    @property
    def N_IN(self): return self.NP + self.SH


CFG = Cfg()


def _cp(sem=None, vmem=VMEM_LIMIT):
    return pltpu.CompilerParams(dimension_semantics=sem, vmem_limit_bytes=vmem)


def _sigmoid(z):
    return 1.0 / (1.0 + jnp.exp(-z))


def _dot(a, b, dims):
    return lax.dot_general(a, b, (dims, ((), ())), preferred_element_type=F32)


NN = ((1,), (0,))
NT = ((1,), (1,))
TN = ((0,), (0,))


def _blk(off, width):
    assert off % width == 0, (off, width)
    return off // width


def matmul(a, b, mode, tm, tn, tk, out_dtype, name, side=None):
    if mode == 'nn':
        (M, K), (_, N) = a.shape, b.shape
    elif mode == 'nt':
        (M, K), (N, _) = a.shape, b.shape
    else:
        (K, M), (_, N) = a.shape, b.shape
    tm, tn, tk = min(tm, M), min(tn, N), min(tk, K)
    assert M % tm == 0 and N % tn == 0 and K % tk == 0, (M, N, K, tm, tn, tk)
    nk = K // tk
    dims = {'nn': NN, 'nt': NT, 'tn': TN}[mode]

    def body(a_ref, b_ref, o_ref, *acc):
        part = _dot(a_ref[...].astype(BF16), b_ref[...].astype(BF16), dims)
        if nk == 1:
            o_ref[...] = part.astype(out_dtype)
        else:
            acc_ref, = acc
            k = pl.program_id(2)

            @pl.when(k == 0)
            def _():
                acc_ref[...] = part

            @pl.when(k > 0)
            def _():
                acc_ref[...] += part

            @pl.when(k == nk - 1)
            def _():
                o_ref[...] = acc_ref[...].astype(out_dtype)

    if mode == 'tn':
        a_spec = pl.BlockSpec((tk, tm), lambda n, m, k: (k, m))
    else:
        a_spec = pl.BlockSpec((tm, tk), lambda n, m, k: (m, k))
    if mode == 'nt':
        b_spec = pl.BlockSpec((tn, tk), lambda n, m, k: (n, k))
    else:
        b_spec = pl.BlockSpec((tk, tn), lambda n, m, k: (k, n))
    grid = (N // tn, M // tm, nk)
    o_spec = pl.BlockSpec((tm, tn), lambda n, m, k: (m, n))
    o_shape = jax.ShapeDtypeStruct((M, N), out_dtype)
    acc = [] if nk == 1 else [pltpu.VMEM((tm, tn), F32)]
    if side is None:
        return pl.pallas_call(
            body, name=name, grid=grid, in_specs=[a_spec, b_spec], out_specs=o_spec, out_shape=o_shape,
            scratch_shapes=acc, compiler_params=_cp(("parallel", "parallel", "arbitrary")),
        )(a, b)
    arrs, gathers = side
    whole = pl.BlockSpec(memory_space=pl.ANY)
    res = pl.pallas_call(
        with_exchange(body, 2, 1, gathers, grid), name=name, grid=grid,
        in_specs=[a_spec, b_spec] + [whole] * len(arrs), out_specs=[o_spec] + [whole] * len(arrs),
        out_shape=[o_shape] + _exchange_shapes(arrs, gathers),
        scratch_shapes=acc + _exchange_sems(len(arrs)),
        compiler_params=_cp(("arbitrary", "arbitrary", "arbitrary")),
    )(a, b, *arrs)
    return res[0], res[1:]


def rmsnorm_fwd(x, w, name, tm=256):
    S, D = x.shape

    def body(x_ref, w_ref, o_ref):
        xv = x_ref[...]
        r = lax.rsqrt(jnp.mean(xv * xv, axis=-1, keepdims=True) + RMS_EPS)
        o_ref[...] = ((xv * r) * w_ref[...]).astype(BF16)

    return pl.pallas_call(
        body, name=name, grid=(S // tm,),
        in_specs=[pl.BlockSpec((tm, D), lambda i: (i, 0)), pl.BlockSpec((1, D), lambda i: (0, 0))],
        out_specs=pl.BlockSpec((tm, D), lambda i: (i, 0)),
        out_shape=jax.ShapeDtypeStruct((S, D), BF16),
        compiler_params=_cp(("parallel",)),
    )(x, w)


def rmsnorm_bwd(dh_a, dh_b, x, w, dout, name, tm=128):
    S, D = x.shape

    def body(da_ref, db_ref, x_ref, w_ref, do_ref, gx_ref, gw_ref):
        xv = x_ref[...]
        dh = da_ref[...] + db_ref[...]
        r = lax.rsqrt(jnp.mean(xv * xv, axis=-1, keepdims=True) + RMS_EPS)
        g = dh * w_ref[...]
        dx = r * g - xv * (r * r * r) * jnp.mean(g * xv, axis=-1, keepdims=True)
        gx_ref[...] = do_ref[...] + dx
        gw = jnp.sum(dh * (xv * r), axis=0, keepdims=True)

        @pl.when(pl.program_id(0) == 0)
        def _():
            gw_ref[...] = gw

        @pl.when(pl.program_id(0) > 0)
        def _():
            gw_ref[...] += gw

    row = pl.BlockSpec((tm, D), lambda i: (i, 0))
    vec = pl.BlockSpec((1, D), lambda i: (0, 0))
    return pl.pallas_call(
        body, name=name, grid=(S // tm,),
        in_specs=[row, row, row, vec, row],
        out_specs=[row, vec],
        out_shape=[jax.ShapeDtypeStruct((S, D), F32), jax.ShapeDtypeStruct((1, D), F32)],
        compiler_params=_cp(("arbitrary",)),
    )(dh_a, dh_b, x, w, dout)


def out_proj_final(merged, w_out, x, fw, tgt, name, tm=256):
    S, D = x.shape

    def body(m_ref, wo_ref, x_ref, w_ref, t_ref, do_ref, loss_ref, gw_ref):
        out = x_ref[...] + _dot(m_ref[...], wo_ref[...], NN)
        w = w_ref[...]
        r = lax.rsqrt(jnp.mean(out * out, axis=-1, keepdims=True) + RMS_EPS)
        yn = out * r
        err = yn * w - t_ref[...]
        lrow = 0.5 * jnp.mean(err * err, axis=-1, keepdims=True)
        lsum = jnp.zeros((1, LANES), F32) + jnp.sum(lrow, axis=0, keepdims=True)
        dfin = err * (1.0 / D)
        g = dfin * w
        do_ref[...] = r * g - out * (r * r * r) * jnp.mean(g * out, axis=-1, keepdims=True)
        gw = jnp.sum(dfin * yn, axis=0, keepdims=True)

        @pl.when(pl.program_id(0) == 0)
        def _():
            gw_ref[...] = gw
            loss_ref[...] = lsum

        @pl.when(pl.program_id(0) > 0)
        def _():
            gw_ref[...] += gw
            loss_ref[...] += lsum

    row = pl.BlockSpec((tm, D), lambda i: (i, 0))
    vec = pl.BlockSpec((1, D), lambda i: (0, 0))
    return pl.pallas_call(
        body, name=name, grid=(S // tm,),
        in_specs=[row, pl.BlockSpec((D, D), lambda i: (0, 0)), row, vec, row],
        out_specs=[row, pl.BlockSpec((1, LANES), lambda i: (0, 0)), vec],
        out_shape=[jax.ShapeDtypeStruct((S, D), F32), jax.ShapeDtypeStruct((1, LANES), F32),
                   jax.ShapeDtypeStruct((1, D), F32)],
        compiler_params=_cp(("arbitrary",)),
    )(merged, w_out, x, fw, tgt)


def branch_merge(o_a, w_attn, y_n, w_ssm, proj, cfg, name, tm=512, tn=512):
    S, D = cfg.S, cfg.D
    tm, tn = min(tm, S), min(tn, D)

    def body(oa_ref, wa_ref, yn_ref, ws_ref, ga_ref, gs_ref, a_ref, s_ref, m_ref):
        a = _dot(oa_ref[...], wa_ref[...], NN)
        sv = _dot(yn_ref[...], ws_ref[...], NN)
        a_ref[...] = a
        s_ref[...] = sv
        m_ref[...] = (_sigmoid(ga_ref[...]) * a + _sigmoid(gs_ref[...]) * sv).astype(BF16)

    tile = pl.BlockSpec((tm, tn), lambda n, m: (m, n))
    return pl.pallas_call(
        body, name=name, grid=(D // tn, S // tm),
        in_specs=[pl.BlockSpec((tm, cfg.AW), lambda n, m: (m, 0)), pl.BlockSpec((cfg.AW, tn), lambda n, m: (0, n)),
                  pl.BlockSpec((tm, cfg.SI), lambda n, m: (m, 0)), pl.BlockSpec((cfg.SI, tn), lambda n, m: (0, n)),
                  pl.BlockSpec((tm, tn), lambda n, m: (m, _blk(cfg.ga0, tn) + n)),
                  pl.BlockSpec((tm, tn), lambda n, m: (m, _blk(cfg.gs0, tn) + n))],
        out_specs=[tile, tile, tile],
        out_shape=[jax.ShapeDtypeStruct((S, D), F32), jax.ShapeDtypeStruct((S, D), F32),
                   jax.ShapeDtypeStruct((S, D), BF16)],
        compiler_params=_cp(("parallel", "parallel")),
    )(o_a, w_attn, y_n, w_ssm, proj, proj)


def merge_bwd(dout, w_out, a_out, s_out, proj, cfg, name, tm=512, tn=1024):
    S, D = cfg.S, cfg.D
    tm, tn = min(tm, S), min(tn, D)

    def body(do_ref, wo_ref, a_ref, s_ref, ga_ref, gs_ref, da_ref, ds_ref, dga_ref, dgs_ref):
        dmv = _dot(do_ref[...].astype(BF16), wo_ref[...], NT)
        sa = _sigmoid(ga_ref[...])
        ss = _sigmoid(gs_ref[...])
        da_ref[...] = (dmv * sa).astype(BF16)
        ds_ref[...] = (dmv * ss).astype(BF16)
        dga_ref[...] = (dmv * a_ref[...] * (sa * (1.0 - sa))).astype(BF16)
        dgs_ref[...] = (dmv * s_ref[...] * (ss * (1.0 - ss))).astype(BF16)

    tile = pl.BlockSpec((tm, tn), lambda n, m: (m, n))
    sh = jax.ShapeDtypeStruct((S, D), BF16)
    return pl.pallas_call(
        body, name=name, grid=(D // tn, S // tm),
        in_specs=[pl.BlockSpec((tm, D), lambda n, m: (m, 0)), pl.BlockSpec((tn, D), lambda n, m: (n, 0)), tile, tile,
                  pl.BlockSpec((tm, tn), lambda n, m: (m, _blk(cfg.ga0, tn) + n)),
                  pl.BlockSpec((tm, tn), lambda n, m: (m, _blk(cfg.gs0, tn) + n))],
        out_specs=[tile] * 4, out_shape=[sh] * 4,
        compiler_params=_cp(("parallel", "parallel")),
    )(dout, w_out, a_out, s_out, proj, proj)


def _attn_rows(base, d):
    return pl.ds(base, ATTN_BLOCK) if d == 1 else pl.ds(base, ATTN_BLOCK, stride=d)


def _attn_units(cfg):
    dmax = max(d for _, d in cfg.patterns)
    units = []
    for p, (window, d) in enumerate(cfg.patterns):
        assert window // d == ATTN_BLOCK and dmax % d == 0
        nsub = dmax // d
        for b in range(nsub):
            for r in range(d):
                base = b * ATTN_BLOCK * d + r
                if b > 0:
                    units.append((p, d, base, (b - 1) * ATTN_BLOCK * d + r, False))
                else:
                    units.append((p, d, base, (nsub - 1) * ATTN_BLOCK * d + r, True))
    return units, ATTN_BLOCK * dmax


def _set_bias_tiles(bias_s, slope, cfg):
    qi = lax.broadcasted_iota(jnp.int32, (ATTN_BLOCK, ATTN_BLOCK), 0)
    ki = lax.broadcasted_iota(jnp.int32, (ATTN_BLOCK, ATTN_BLOCK), 1)
    for p, (_, d) in enumerate(cfg.patterns):
        bias_s[2 * p] = jnp.where(ki >= qi, (-slope) * ((ATTN_BLOCK + qi - ki) * d).astype(F32), NEG)
        bias_s[2 * p + 1] = jnp.where(ki <= qi, (-slope) * ((qi - ki) * d).astype(F32), NEG)


def _unit_scores(q, kp, kc, bias_s, p, prev_ok, scale):
    sp = _dot(q, kp, NT) * scale + bias_s[2 * p]
    if prev_ok is not None:
        sp = jnp.where(prev_ok, sp, NEG)
    return sp, _dot(q, kc, NT) * scale + bias_s[2 * p + 1]


def _slope_table(cfg):
    slopes = jnp.asarray([2.0 ** (-8.0 * (h + 1) / cfg.AH) for h in range(cfg.AH)], F32)
    return jnp.broadcast_to(slopes.reshape(cfg.AH, 1, 1), (cfg.AH, 8, LANES))


def attn_fused_fwd(proj, slopes, cfg, name):
    S, E, AH = cfg.S, cfg.E, cfg.AH
    units, SB = _attn_units(cfg)
    assert S % SB == 0
    npat = len(cfg.patterns)
    scale = E ** -0.5

    def spec(off, prev):
        c0 = _blk(off, E)
        if prev:
            return pl.BlockSpec((SB, E), lambda h, i: (jnp.maximum(i - 1, 0), c0 + h))
        return pl.BlockSpec((SB, E), lambda h, i: (i, c0 + h))

    def body(q_ref, kp_ref, kc_ref, vp_ref, vc_ref, z_ref, sl_ref, oa_ref, om_ref, lt_ref, *scr):
        o_s, l_s, bias_s = scr[:npat], scr[npat:2 * npat], scr[2 * npat]
        i = pl.program_id(1)

        @pl.when(i == 0)
        def _():
            _set_bias_tiles(bias_s, sl_ref[0, 0:1, :], cfg)

        for p, d, base, pbase, from_prev in units:
            rows, prows = _attn_rows(base, d), _attn_rows(pbase, d)
            q = q_ref[rows, :].astype(BF16)
            kp = (kp_ref if from_prev else kc_ref)[prows, :].astype(BF16)
            vp = (vp_ref if from_prev else vc_ref)[prows, :].astype(BF16)
            sp, sc = _unit_scores(q, kp, kc_ref[rows, :].astype(BF16), bias_s, p, (i > 0) if from_prev else None, scale)
            m = jnp.max(jnp.maximum(sp, sc), axis=1, keepdims=True)
            pp = jnp.exp(sp - m)
            pc = jnp.exp(sc - m)
            l = jnp.sum(pp + pc, axis=1, keepdims=True)
            o_s[p][rows, :] = (_dot(pp.astype(BF16), vp, NN) +
                               _dot(pc.astype(BF16), vc_ref[rows, :].astype(BF16), NN)) * (1.0 / l)
            l_s[p][rows, :] = m + jnp.log(l)
        ls = [l_s[p][...] for p in range(npat)]
        m = functools.reduce(jnp.maximum, ls)
        lt = m + jnp.log(sum(jnp.exp(l_ - m) for l_ in ls))
        lt_ref[...] = lt
        mix = sum(jnp.exp(ls[p] - lt) * o_s[p][...] for p in range(npat))
        om_ref[...] = mix
        z = z_ref[...]
        oa_ref[...] = (mix * (z * _sigmoid(z))).astype(BF16)

    out = pl.BlockSpec((SB, E), lambda h, i: (i, h))
    return pl.pallas_call(
        body, name=name, grid=(AH, S // SB),
        in_specs=[spec(0, False), spec(cfg.k0, True), spec(cfg.k0, False), spec(cfg.v0, True), spec(cfg.v0, False),
                  spec(cfg.za0, False), pl.BlockSpec((1, 8, LANES), lambda h, i: (h, 0, 0))],
        out_specs=[out, out, pl.BlockSpec((SB, 1), lambda h, i: (h * (S // SB) + i, 0))],
        out_shape=[jax.ShapeDtypeStruct((S, cfg.AW), BF16), jax.ShapeDtypeStruct((S, cfg.AW), F32),
                   jax.ShapeDtypeStruct((AH * S, 1), F32)],
        scratch_shapes=[pltpu.VMEM((SB, E), F32)] * npat + [pltpu.VMEM((SB, 1), F32)] * npat +
        [pltpu.VMEM((2 * npat, ATTN_BLOCK, ATTN_BLOCK), F32)],
        compiler_params=_cp(("parallel", "arbitrary")),
    )(proj, proj, proj, proj, proj, proj, slopes)


def attn_fused_bwd(proj, do_a, o_mix, ltot, slopes, cfg, name):
    S, E, AH = cfg.S, cfg.E, cfg.AH
    units, SB = _attn_units(cfg)
    nsb = S // SB
    last = nsb - 1
    scale = E ** -0.5

    def spec(off, prev):
        c0 = _blk(off, E)
        if prev:
            return pl.BlockSpec((SB, E), lambda h, i: (jnp.maximum(i - 1, 0), c0 + h))
        return pl.BlockSpec((SB, E), lambda h, i: (jnp.minimum(i, last), c0 + h))

    cur = pl.BlockSpec((SB, E), lambda h, i: (jnp.minimum(i, last), h))
    prev = pl.BlockSpec((SB, E), lambda h, i: (jnp.maximum(i - 1, 0), h))

    def body(q_ref, kp_ref, kc_ref, vp_ref, vc_ref, z_ref, doa_ref, om_ref, lt_ref, sl_ref,
             dq_ref, dk_ref, dv_ref, dz_ref, dmix_s, dl_s, dq_s, dkp_s, dvp_s, dkc_s, dvc_s, bias_s):
        i = pl.program_id(1)

        @pl.when(i == 0)
        def _():
            dkc_s[...] = jnp.zeros_like(dkc_s)
            dvc_s[...] = jnp.zeros_like(dvc_s)
            _set_bias_tiles(bias_s, sl_ref[0, 0:1, :], cfg)

        @pl.when(i < nsb)
        def _():
            z = z_ref[...]
            s = _sigmoid(z)
            doa = doa_ref[...]
            om = om_ref[...]
            dmix = doa * (z * s)
            dmix_s[...] = dmix
            dz_ref[...] = (doa * om * (s * (1.0 + z * (1.0 - s)))).astype(BF16)
            dl_s[...] = jnp.sum(dmix * om, axis=1, keepdims=True)
            dkp_s[...] = dkc_s[...]
            dvp_s[...] = dvc_s[...]
            dkc_s[...] = jnp.zeros_like(dkc_s)
            dvc_s[...] = jnp.zeros_like(dvc_s)
            dq_s[...] = jnp.zeros_like(dq_s)
            for p, d, base, pbase, from_prev in units:
                rows, prows = _attn_rows(base, d), _attn_rows(pbase, d)
                q = q_ref[rows, :].astype(BF16)
                kc = kc_ref[rows, :].astype(BF16)
                kp = (kp_ref if from_prev else kc_ref)[prows, :].astype(BF16)
                vp = (vp_ref if from_prev else vc_ref)[prows, :].astype(BF16)
                do = dmix_s[rows, :].astype(BF16)
                lt = lt_ref[rows, :]
                dlt = dl_s[rows, :]
                sp, sc = _unit_scores(q, kp, kc, bias_s, p, (i > 0) if from_prev else None, scale)
                pp = jnp.exp(sp - lt)
                pc = jnp.exp(sc - lt)
                dsp = (pp * (_dot(do, vp, NT) - dlt) * scale).astype(BF16)
                dsc = (pc * (_dot(do, vc_ref[rows, :].astype(BF16), NT) - dlt) * scale).astype(BF16)
                dq_s[rows, :] += _dot(dsp, kp, NN) + _dot(dsc, kc, NN)
                dkc_s[rows, :] += _dot(dsc, q, TN)
                dvc_s[rows, :] += _dot(pc.astype(BF16), do, TN)
                dk_t, dv_t = (dkp_s, dvp_s) if from_prev else (dkc_s, dvc_s)
                dk_t[prows, :] += _dot(dsp, q, TN)
                dv_t[prows, :] += _dot(pp.astype(BF16), do, TN)
            dq_ref[...] = dq_s[...].astype(BF16)
            dk_ref[...] = dkp_s[...].astype(BF16)
            dv_ref[...] = dvp_s[...].astype(BF16)

        @pl.when(i == nsb)
        def _():
            dk_ref[...] = dkc_s[...].astype(BF16)
            dv_ref[...] = dvc_s[...].astype(BF16)

    sh = jax.ShapeDtypeStruct((S, cfg.AW), BF16)
    acc = pltpu.VMEM((SB, E), F32)
    return pl.pallas_call(
        body, name=name, grid=(AH, nsb + 1),
        in_specs=[spec(0, False), spec(cfg.k0, True), spec(cfg.k0, False), spec(cfg.v0, True), spec(cfg.v0, False),
                  spec(cfg.za0, False), cur, cur,
                  pl.BlockSpec((SB, 1), lambda h, i: (h * nsb + jnp.minimum(i, last), 0)),
                  pl.BlockSpec((1, 8, LANES), lambda h, i: (h, 0, 0))],
        out_specs=[cur, prev, prev, cur], out_shape=[sh] * 4,
        scratch_shapes=[acc, pltpu.VMEM((SB, 1), F32), acc, acc, acc, acc, acc,
                        pltpu.VMEM((2 * len(cfg.patterns), ATTN_BLOCK, ATTN_BLOCK), F32)],
        compiler_params=_cp(("parallel", "arbitrary")),
    )(proj, proj, proj, proj, proj, proj, do_a, o_mix, ltot, slopes)


HALO = 8


def _conv_taps(x_ref, h_ref, kc):
    x = x_ref[...]
    full = jnp.concatenate([jnp.where(pl.program_id(1) == 0, 0.0, h_ref[...]), x], axis=0)
    return [pltpu.roll(full, s, axis=0)[HALO:, :] for s in range(kc - 1, 0, -1)] + [x]


def _conv_pre(taps, w_ref, b_ref):
    pre = b_ref[...] + w_ref[0:1, :] * taps[0]
    for k in range(1, len(taps)):
        pre = pre + w_ref[k:k + 1, :] * taps[k]
    return pre


def conv_fwd(proj, w, b, cfg, name, tm=1024, tc=512):
    S, CD, KC = cfg.S, cfg.CD, cfg.KC
    tc = min(tc, CD)
    c0 = _blk(cfg.xbc0, tc)
    hb = tm // HALO

    def body(x_ref, h_ref, w_ref, b_ref, o_ref):
        pre = _conv_pre(_conv_taps(x_ref, h_ref, KC), w_ref, b_ref)
        o_ref[...] = pre * _sigmoid(pre)

    return pl.pallas_call(
        body, name=name, grid=(CD // tc, S // tm),
        in_specs=[pl.BlockSpec((tm, tc), lambda c, i: (i, c0 + c)),
                  pl.BlockSpec((HALO, tc), lambda c, i: (jnp.maximum(i * hb - 1, 0), c0 + c)),
                  pl.BlockSpec((KC, tc), lambda c, i: (0, c)),
                  pl.BlockSpec((1, tc), lambda c, i: (0, c))],
        out_specs=pl.BlockSpec((tm, tc), lambda c, i: (i, c)),
        out_shape=jax.ShapeDtypeStruct((S, CD), F32),
        compiler_params=_cp(("parallel", "arbitrary")),
    )(proj, proj, w, b)


def conv_bwd_a(proj, dxc, w, b, cfg, name, c_off, tm=1024, tc=512):
    S, KC = cfg.S, cfg.KC
    CD = dxc.shape[1]
    tc = min(tc, CD)
    c0 = _blk(cfg.xbc0 + c_off, tc)
    w0 = _blk(c_off, tc)
    hb = tm // HALO

    def body(x_ref, h_ref, d_ref, w_ref, b_ref, dp_ref, gw_ref, gb_ref):
        taps = _conv_taps(x_ref, h_ref, KC)
        pre = _conv_pre(taps, w_ref, b_ref)
        s = _sigmoid(pre)
        dpre = d_ref[...] * (s * (1.0 + pre * (1.0 - s)))
        dp_ref[...] = dpre
        gb = jnp.sum(dpre, axis=0, keepdims=True)
        gws = [jnp.sum(dpre * taps[k], axis=0, keepdims=True) for k in range(KC)]
        gw = jnp.concatenate(gws + [jnp.zeros((8 - KC, tc), F32)], axis=0)

        @pl.when(pl.program_id(1) == 0)
        def _():
            gw_ref[...] = gw
            gb_ref[...] = gb

        @pl.when(pl.program_id(1) > 0)
        def _():
            gw_ref[...] += gw
            gb_ref[...] += gb

    return pl.pallas_call(
        body, name=name, grid=(CD // tc, S // tm),
        in_specs=[pl.BlockSpec((tm, tc), lambda c, i: (i, c0 + c)),
                  pl.BlockSpec((HALO, tc), lambda c, i: (jnp.maximum(i * hb - 1, 0), c0 + c)),
                  pl.BlockSpec((tm, tc), lambda c, i: (i, c)),
                  pl.BlockSpec((KC, tc), lambda c, i: (0, w0 + c)),
                  pl.BlockSpec((1, tc), lambda c, i: (0, w0 + c))],
        out_specs=[pl.BlockSpec((tm, tc), lambda c, i: (i, c)),
                   pl.BlockSpec((8, tc), lambda c, i: (0, c)),
                   pl.BlockSpec((1, tc), lambda c, i: (0, c))],
        out_shape=[jax.ShapeDtypeStruct((S, CD), F32), jax.ShapeDtypeStruct((8, CD), F32),
                   jax.ShapeDtypeStruct((1, CD), F32)],
        compiler_params=_cp(("parallel", "arbitrary")),
    )(proj, proj, dxc, w, b)


def conv_bwd_b(dpre, w, cfg, name, c_off, tm=1024, tc=512):
    S, KC = cfg.S, cfg.KC
    CD = dpre.shape[1]
    tc = min(tc, CD)
    w0 = _blk(c_off, tc)
    hb = tm // HALO
    nrb = S // tm
    last_h = S // HALO - 1

    def body(d_ref, h_ref, w_ref, o_ref):
        d = d_ref[...]
        full = jnp.concatenate([d, jnp.where(pl.program_id(1) == nrb - 1, 0.0, h_ref[...])], axis=0)
        acc = w_ref[KC - 1:KC, :] * d
        for j in range(1, KC):
            acc = acc + w_ref[KC - 1 - j:KC - j, :] * pltpu.roll(full, tm + HALO - j, axis=0)[:tm, :]
        o_ref[...] = acc.astype(BF16)

    return pl.pallas_call(
        body, name=name, grid=(CD // tc, nrb),
        in_specs=[pl.BlockSpec((tm, tc), lambda c, i: (i, c)),
                  pl.BlockSpec((HALO, tc), lambda c, i: (jnp.minimum((i + 1) * hb, last_h), c)),
                  pl.BlockSpec((KC, tc), lambda c, i: (0, w0 + c))],
        out_specs=pl.BlockSpec((tm, tc), lambda c, i: (i, c)),
        out_shape=jax.ShapeDtypeStruct((S, CD), BF16),
        compiler_params=_cp(("parallel", "arbitrary")),
    )(dpre, dpre, w)


def _pad_lanes(v, width=LANES):
    return jnp.pad(v, ((0, 0), (0, width - v.shape[1])))


def ssd_prep(dt_raw, dt_bias, a_log, cfg, name):
    S, L = cfg.S, cfg.L

    def body(x_ref, b_ref, al_ref, dt_ref, ac_ref):
        x = x_ref[...] + b_ref[...]
        dt = jnp.maximum(x, 0.0) + jnp.log(1.0 + jnp.exp(-jnp.abs(x)))
        da = dt * (-jnp.exp(al_ref[...]))
        li = lax.broadcasted_iota(jnp.int32, (L, L), 0)
        si = lax.broadcasted_iota(jnp.int32, (L, L), 1)
        tri = jnp.where(li >= si, 1.0, 0.0).astype(F32)
        dt_ref[...] = dt
        ac_ref[...] = lax.dot_general(tri, da, ((NN), ((), ())), precision=lax.Precision.HIGHEST,
                                      preferred_element_type=F32)

    row = pl.BlockSpec((L, LANES), lambda i: (i, 0))
    vec = pl.BlockSpec((1, LANES), lambda i: (0, 0))
    sh = jax.ShapeDtypeStruct((S, LANES), F32)
    return pl.pallas_call(
        body, name=name, grid=(S // L,), in_specs=[row, vec, vec], out_specs=[row, row], out_shape=[sh, sh],
        compiler_params=_cp(("parallel",)),
    )(dt_raw, dt_bias, a_log)


def _spread(v, n):
    return jnp.broadcast_to(v[:, :, None], v.shape + (n,)).reshape(v.shape[0], v.shape[1] * n)


def _head_selectors(cfg):
    def sel(width):
        head = jnp.arange(LANES)[None, :, None]
        slot = jnp.arange(cfg.SG)[:, None, None] * cfg.HPG + (jnp.arange(cfg.HPG * width) // width)[None, None, :]
        return (head == slot).astype(BF16)
    return sel(cfg.P), sel(LANES)


def _spread_heads(v, sel):
    hi = v.astype(BF16)
    r1 = v - hi.astype(F32)
    mid = r1.astype(BF16)
    lo = (r1 - mid.astype(F32)).astype(BF16)
    return _dot(hi, sel, NN) + _dot(mid, sel, NN) + _dot(lo, sel, NN)


def _pair_select(halves, p):
    low = lax.broadcasted_iota(jnp.int32, halves[0].shape, 1) < p
    return jnp.where(low, halves[0], halves[1])


def _head_rows(row, hpg, p):
    return jnp.concatenate([jnp.broadcast_to(row[:, j * LANES:(j + 1) * LANES], (p, LANES)) for j in range(hpg)],
                           axis=0)


def _segment_sums(t, sel):
    hi = t.astype(BF16)
    lo = (t - hi.astype(F32)).astype(BF16)
    return _dot(hi, sel, NT) + _dot(lo, sel, NT)


def ssd_scan_fwd(xc, dt, acum, act, sel_p, sel_l, cfg, name):
    S, L, P, SN, HPG, SG, SI = cfg.S, cfg.L, cfg.P, cfg.SN, cfg.HPG, cfg.SG, cfg.SI
    nc = S // L
    GW = HPG * P
    bcol, ccol = _blk(SI, SN), _blk(SI + cfg.GN, SN)

    def body(xs_ref, b_ref, c_ref, dtn_ref, acn_ref, at_ref, sp_ref, sl_ref, y_ref, st_ref, st):
        @pl.when(pl.program_id(1) == 0)
        def _():
            st[...] = jnp.zeros_like(st)

        acn = acn_ref[...]
        dts = _spread_heads(dtn_ref[...], sp_ref[0])
        a_p = _spread_heads(acn, sp_ref[0])
        acs = _spread_heads(acn, sl_ref[0])
        s0 = st[...]
        st_ref[0] = s0.reshape(HPG, P, SN)
        B = b_ref[...].astype(BF16)
        C = c_ref[...].astype(BF16)
        G = _dot(C, B, NT)
        causal = lax.broadcasted_iota(jnp.int32, (L, L), 0) >= lax.broadcasted_iota(jnp.int32, (L, L), 1)
        xdt = xs_ref[...] * dts
        xdtb = xdt.astype(BF16)
        ws = jnp.exp(a_p[L - 1:L, :] - a_p)
        yo = jnp.exp(a_p) * _dot(C, s0.astype(BF16), NT)
        yd = []
        for jp in range(HPG // 2):
            x_pair = xdtb[:, jp * LANES:(jp + 1) * LANES]
            halves = []
            for j in (2 * jp, 2 * jp + 1):
                dm = jnp.where(causal, jnp.exp(acs[:, j * LANES:(j + 1) * LANES] - at_ref[j:j + 1, :]), 0.0)
                halves.append(_dot((G * dm).astype(BF16), x_pair, NN))
            yd.append(_pair_select(halves, P))
        y_ref[...] = jnp.concatenate(yd, axis=1) + yo
        st[...] = _head_rows(jnp.exp(acs[L - 1:L, :]), HPG, P) * s0 + _dot((xdt * ws).astype(BF16), B, TN)

    y, states = pl.pallas_call(
        body, name=name, grid=(SG, nc),
        in_specs=[pl.BlockSpec((L, GW), lambda g, c: (c, g)),
                  pl.BlockSpec((L, SN), lambda g, c: (c, bcol + g)),
                  pl.BlockSpec((L, SN), lambda g, c: (c, ccol + g)),
                  pl.BlockSpec((L, LANES), lambda g, c: (c, 0)),
                  pl.BlockSpec((L, LANES), lambda g, c: (c, 0)),
                  pl.BlockSpec((HPG, L), lambda g, c: (g, c)),
                  pl.BlockSpec((1, LANES, GW), lambda g, c: (g, 0, 0)),
                  pl.BlockSpec((1, LANES, HPG * LANES), lambda g, c: (g, 0, 0))],
        out_specs=[pl.BlockSpec((L, GW), lambda g, c: (c, g)),
                   pl.BlockSpec((1, HPG, P, SN), lambda g, c: (c, g, 0, 0))],
        out_shape=[jax.ShapeDtypeStruct((S, SI), F32), jax.ShapeDtypeStruct((nc, cfg.SH, P, SN), F32)],
        scratch_shapes=[pltpu.VMEM((GW, SN), F32)],
        compiler_params=_cp(("parallel", "arbitrary")),
    )(xc, xc, xc, dt, acum, act, sel_p, sel_l)
    return y, states


def ssd_scan_bwd(xc, dt, acum, act, sel_p, sel_l, states, y, dy, dvec, cfg, name, side):
    S, L, P, SN, HPG, SG, SI = cfg.S, cfg.L, cfg.P, cfg.SN, cfg.HPG, cfg.SG, cfg.SI
    nc = S // L
    GW = HPG * P
    bcol, ccol = _blk(SI, SN), _blk(SI + cfg.GN, SN)

    def rc(c):
        return nc - 1 - c

    def body(xs_ref, b_ref, c_ref, dtn_ref, acn_ref, at_ref, sp_ref, sl_ref, st_ref, y_ref, dy_ref, dk_ref,
             dxs_ref, db_ref, dc_ref, dac_ref, dxsum_ref, dst):
        @pl.when(pl.program_id(1) == 0)
        def _():
            dst[...] = jnp.zeros_like(dst)

        sel = sp_ref[0]
        acn = acn_ref[...]
        dts = _spread_heads(dtn_ref[...], sel)
        a_p = _spread_heads(acn, sel)
        acs = _spread_heads(acn, sl_ref[0])
        B = b_ref[...].astype(BF16)
        C = c_ref[...].astype(BF16)
        G = _dot(C, B, NT)
        causal = lax.broadcasted_iota(jnp.int32, (L, L), 0) >= lax.broadcasted_iota(jnp.int32, (L, L), 1)
        low = lax.broadcasted_iota(jnp.int32, (L, LANES), 1) < P
        xs = xs_ref[...]
        dY = dy_ref[...]
        xdt = xs * dts
        xdtb = xdt.astype(BF16)
        dYb = dY.astype(BF16)
        s0 = st_ref[0].reshape(GW, SN)
        s0b = s0.astype(BF16)
        ds1 = dst[...]
        ds1b = ds1.astype(BF16)
        ws = jnp.exp(a_p[L - 1:L, :] - a_p)
        dR = (jnp.exp(a_p) * dY).astype(BF16)
        dX2 = ws * _dot(B, ds1b, NT)
        dgsum = jnp.zeros((L, L), F32)
        dX1, yd = [], []
        for jp in range(HPG // 2):
            lanes = slice(jp * LANES, (jp + 1) * LANES)
            x_pair, dy_pair = xdtb[:, lanes], dYb[:, lanes]
            h1, h2 = [], []
            for h, j in enumerate((2 * jp, 2 * jp + 1)):
                dm = jnp.where(causal, jnp.exp(acs[:, j * LANES:(j + 1) * LANES] - at_ref[j:j + 1, :]), 0.0)
                mine = low if h == 0 else jnp.logical_not(low)
                dgsum = dgsum + _dot(jnp.where(mine, dy_pair, jnp.zeros_like(dy_pair)), x_pair, NT) * dm
                Mb = (G * dm).astype(BF16)
                h1.append(_dot(Mb, dy_pair, TN))
                h2.append(_dot(Mb, x_pair, NN))
            dX1.append(_pair_select(h1, P))
            yd.append(_pair_select(h2, P))
        dX1 = jnp.concatenate(dX1, axis=1)
        dX = dX1 + dX2
        pair = (dYb.astype(F32) - dY) * jnp.concatenate(yd, axis=1) - xdtb.astype(F32) * dX1
        through = _segment_sums(xdt * dX2, sel)
        u = ds1 * s0
        u_hi = u.astype(BF16)
        ones = jnp.ones((16, SN), BF16)
        u_rows = _dot(ones, u_hi, NT) + _dot(ones, (u - u_hi.astype(F32)).astype(BF16), NT)
        at_end = jnp.exp(acn[L - 1:L, :]) * _segment_sums(u_rows, sel)[0:1, :] + jnp.sum(through, axis=0, keepdims=True)
        is_last = lax.broadcasted_iota(jnp.int32, (L, LANES), 0) == L - 1
        dac_ref[0] = _segment_sums(dY * y_ref[...] + pair, sel) - through + jnp.where(is_last, at_end, 0.0)
        dxsum_ref[0] = _segment_sums(dX * xs, sel)
        dxs_ref[...] = dX * dts + dk_ref[...] * dY
        dst[...] = _head_rows(jnp.exp(acs[L - 1:L, :]), HPG, P) * ds1 + _dot(dR, C, TN)
        dgb = dgsum.astype(BF16)
        dc_ref[...] = _dot(dR, s0b, NN) + _dot(dgb, B, NN)
        db_ref[...] = _dot((xdt * ws).astype(BF16), ds1b, NN) + _dot(dgb, C, TN)

    wide = pl.BlockSpec((L, GW), lambda g, c: (rc(c), g))
    colspec = pl.BlockSpec((1, L, LANES), lambda g, c: (g, rc(c), 0))
    whole = pl.BlockSpec(memory_space=pl.ANY)
    arrs, gathers = side
    grid = (SG, nc)
    res = pl.pallas_call(
        with_exchange(body, 12, 5, gathers, grid), name=name, grid=grid,
        in_specs=[wide,
                  pl.BlockSpec((L, SN), lambda g, c: (rc(c), bcol + g)),
                  pl.BlockSpec((L, SN), lambda g, c: (rc(c), ccol + g)),
                  pl.BlockSpec((L, LANES), lambda g, c: (rc(c), 0)),
                  pl.BlockSpec((L, LANES), lambda g, c: (rc(c), 0)),
                  pl.BlockSpec((HPG, L), lambda g, c: (g, rc(c))),
                  pl.BlockSpec((1, LANES, GW), lambda g, c: (g, 0, 0)),
                  pl.BlockSpec((1, LANES, HPG * LANES), lambda g, c: (g, 0, 0)),
                  pl.BlockSpec((1, HPG, P, SN), lambda g, c: (rc(c), g, 0, 0)),
                  wide, wide,
                  pl.BlockSpec((1, GW), lambda g, c: (0, g))] + [whole] * len(arrs),
        out_specs=[wide,
                   pl.BlockSpec((L, SN), lambda g, c: (rc(c), g)),
                   pl.BlockSpec((L, SN), lambda g, c: (rc(c), g)),
                   colspec, colspec] + [whole] * len(arrs),
        out_shape=[jax.ShapeDtypeStruct((S, SI), F32), jax.ShapeDtypeStruct((S, cfg.GN), F32),
                   jax.ShapeDtypeStruct((S, cfg.GN), F32),
                   jax.ShapeDtypeStruct((SG, S, LANES), F32), jax.ShapeDtypeStruct((SG, S, LANES), F32)] +
        _exchange_shapes(arrs, gathers),
        scratch_shapes=[pltpu.VMEM((GW, SN), F32)] + _exchange_sems(len(arrs)),
        compiler_params=_cp(("arbitrary", "arbitrary")),
    )(xc, xc, xc, dt, acum, act, sel_p, sel_l, states, y, dy, dvec, *arrs)
    return res[:5], res[5:]


def dt_bwd(dac, dxsum, dt_raw, dt, dt_bias, a_log, cfg, name):
    S, L, SG = cfg.S, cfg.L, cfg.SG

    def body(da_ref, dx_ref, x_ref, dt_ref, b_ref, al_ref, o_ref, gb_ref, ga_ref):
        a = -jnp.exp(al_ref[...])
        dtv = dt_ref[...]
        dxs = jnp.sum(dx_ref[...], axis=0)
        upper = jnp.where(lax.broadcasted_iota(jnp.int32, (L, L), 1) >= lax.broadcasted_iota(jnp.int32, (L, L), 0),
                          1.0, 0.0).astype(F32)
        dda = lax.dot_general(upper, jnp.sum(da_ref[...], axis=0), (NN, ((), ())), precision=lax.Precision.HIGHEST,
                              preferred_element_type=F32)
        draw = (dxs + dda * a) * _sigmoid(x_ref[...] + b_ref[...])
        o_ref[...] = draw.astype(BF16)
        gb = jnp.sum(draw, axis=0, keepdims=True)
        ga = jnp.sum(dda * dtv, axis=0, keepdims=True) * a

        @pl.when(pl.program_id(0) == 0)
        def _():
            gb_ref[...] = gb
            ga_ref[...] = ga

        @pl.when(pl.program_id(0) > 0)
        def _():
            gb_ref[...] += gb
            ga_ref[...] += ga

    row = pl.BlockSpec((L, LANES), lambda i: (i, 0))
    vec = pl.BlockSpec((1, LANES), lambda i: (0, 0))
    return pl.pallas_call(
        body, name=name, grid=(S // L,),
        in_specs=[pl.BlockSpec((SG, L, LANES), lambda i: (0, i, 0))] * 2 + [row, row, vec, vec],
        out_specs=[row, vec, vec],
        out_shape=[jax.ShapeDtypeStruct((S, LANES), BF16), jax.ShapeDtypeStruct((1, LANES), F32),
                   jax.ShapeDtypeStruct((1, LANES), F32)],
        compiler_params=_cp(("arbitrary",)),
    )(dac, dxsum, dt_raw, dt, dt_bias, a_log)


def gated_norm_fwd(y, xc, proj, dvec, nw, cfg, name, tm=128):
    S, SI = cfg.S, cfg.SI

    def body(y_ref, xs_ref, z_ref, d_ref, w_ref, o_ref):
        z = z_ref[...]
        yg = (y_ref[...] + d_ref[...] * xs_ref[...]) * (z * _sigmoid(z))
        r = lax.rsqrt(jnp.mean(yg * yg, axis=-1, keepdims=True) + RMS_EPS)
        o_ref[...] = ((yg * r) * w_ref[...]).astype(BF16)

    row = pl.BlockSpec((tm, SI), lambda i: (i, 0))
    vec = pl.BlockSpec((1, SI), lambda i: (0, 0))
    return pl.pallas_call(
        body, name=name, grid=(S // tm,),
        in_specs=[row, row, pl.BlockSpec((tm, SI), lambda i: (i, _blk(cfg.zs0, SI))), vec, vec],
        out_specs=row, out_shape=jax.ShapeDtypeStruct((S, SI), BF16),
        compiler_params=_cp(("parallel",)),
    )(y, xc, proj, dvec, nw)


def gated_norm_bwd(dyn, y, xc, proj, dvec, nw, cfg, name, tm=128):
    S, SI = cfg.S, cfg.SI

    def body(dn_ref, y_ref, xs_ref, z_ref, d_ref, w_ref, dy_ref, dz_ref, gw_ref, gd_ref):
        z = z_ref[...]
        s = _sigmoid(z)
        sz = z * s
        xs = xs_ref[...]
        yf = y_ref[...] + d_ref[...] * xs
        yg = yf * sz
        r = lax.rsqrt(jnp.mean(yg * yg, axis=-1, keepdims=True) + RMS_EPS)
        dn = dn_ref[...]
        g = dn * w_ref[...]
        dyg = r * g - yg * (r * r * r) * jnp.mean(g * yg, axis=-1, keepdims=True)
        dy = dyg * sz
        dy_ref[...] = dy
        dz_ref[...] = (dyg * yf * (s * (1.0 + z * (1.0 - s)))).astype(BF16)
        gw = jnp.sum(dn * (yg * r), axis=0, keepdims=True)
        gd = jnp.sum(dy * xs, axis=0, keepdims=True)

        @pl.when(pl.program_id(0) == 0)
        def _():
            gw_ref[...] = gw
            gd_ref[...] = gd

        @pl.when(pl.program_id(0) > 0)
        def _():
            gw_ref[...] += gw
            gd_ref[...] += gd

    row = pl.BlockSpec((tm, SI), lambda i: (i, 0))
    vec = pl.BlockSpec((1, SI), lambda i: (0, 0))
    return pl.pallas_call(
        body, name=name, grid=(S // tm,),
        in_specs=[row, row, row, pl.BlockSpec((tm, SI), lambda i: (i, _blk(cfg.zs0, SI))), vec, vec],
        out_specs=[row, row, vec, vec],
        out_shape=[jax.ShapeDtypeStruct((S, SI), F32), jax.ShapeDtypeStruct((S, SI), BF16),
                   jax.ShapeDtypeStruct((1, SI), F32), jax.ShapeDtypeStruct((1, SI), F32)],
        compiler_params=_cp(("arbitrary",)),
    )(dyn, y, xc, proj, dvec, nw)


def local_step(cfg, x, tgt, norm_w, conv_w, conv_b, dt_bias, a_log, d_skip, ssm_norm_w, final_norm_w,
               w_main, w_dt, shards, dt0):
    S, D = cfg.S, cfg.D
    slopes = _slope_table(cfg)
    dt_bias_p = _pad_lanes(dt_bias)
    a_log_p = _pad_lanes(a_log)
    dvec = _spread(d_skip, cfg.P)

    hn = rmsnorm_fwd(x, norm_w, "rmsnorm_fwd")
    proj, gathered = matmul(hn, w_main, 'nn', 1024, 1024, 2048, F32, "in_proj", side=(shards, [True] * 3))
    w_attn, w_ssm, w_out = gathered[0].reshape(cfg.AW, D), gathered[1].reshape(cfg.SI, D), gathered[2].reshape(D, D)
    dt_raw = matmul(hn, w_dt, 'nn', 512, 128, 2048, F32, "in_proj_dt")
    o_a, o_mix, ltot = attn_fused_fwd(proj, slopes, cfg, "attn_fwd")
    xc = conv_fwd(proj, conv_w, conv_b, cfg, "conv_fwd")
    dt, acum = ssd_prep(dt_raw, dt_bias_p, a_log_p, cfg, "ssd_prep")
    act = acum[:, :cfg.SH].T
    sel_p, sel_l = _head_selectors(cfg)
    y, states = ssd_scan_fwd(xc, dt, acum, act, sel_p, sel_l, cfg, "ssd_scan_fwd")
    y_n = gated_norm_fwd(y, xc, proj, dvec, ssm_norm_w, cfg, "gated_norm_fwd")
    a_out, s_out, merged = branch_merge(o_a, w_attn, y_n, w_ssm, proj, cfg, "branch_merge")
    dout, loss_p, g_final_w = out_proj_final(merged, w_out, x, final_norm_w.reshape(1, D), tgt, "out_proj_final")

    g_w_out = matmul(merged, dout, 'tn', 1024, 1024, 2048, BF16, "g_w_out")
    da_out, ds_out, dga, dgs = merge_bwd(dout, w_out, a_out, s_out, proj, cfg, "merge_bwd")
    g_w_attn = matmul(o_a, da_out, 'tn', 1024, 1024, 2048, BF16, "g_w_attn")
    g_w_ssm = matmul(y_n, ds_out, 'tn', 1024, 1024, 2048, BF16, "g_w_ssm")
    do_a = matmul(da_out, w_attn, 'nt', 512, 1024, 2048, F32, "d_o_a")
    dyn = matmul(ds_out, w_ssm, 'nt', 512, 1024, 2048, F32, "d_y_n")
    dy, dz_s, g_ssm_norm, g_dvec = gated_norm_bwd(dyn, y, xc, proj, dvec, ssm_norm_w, cfg, "gated_norm_bwd")
    sends = [g.reshape((N_DEV, g.shape[0] // N_DEV, D)) for g in (g_w_attn, g_w_ssm, g_w_out)]
    (dxs, dB, dC, dac_g, dxsum_g), (r_attn, r_ssm, r_out) = ssd_scan_bwd(
        xc, dt, acum, act, sel_p, sel_l, states, y, dy, dvec, cfg, "ssd_scan_bwd", side=(sends, [False] * 3))
    ddt_raw, g_dt_bias, g_a_log = dt_bwd(dac_g, dxsum_g, dt_raw, dt, dt_bias_p, a_log_p, cfg, "dt_bwd")
    dxbc, g_cw, g_cb = [], [], []
    for nm, piece, c_off in (("xs", dxs, 0), ("b", dB, cfg.SI), ("c", dC, cfg.SI + cfg.GN)):
        dpre, gw, gb = conv_bwd_a(proj, piece, conv_w, conv_b, cfg, "conv_bwd_a_" + nm, c_off)
        dxbc.append(conv_bwd_b(dpre, conv_w, cfg, "conv_bwd_b_" + nm, c_off))
        g_cw.append(gw)
        g_cb.append(gb)
    g_conv_w, g_conv_b = jnp.concatenate(g_cw, axis=1), jnp.concatenate(g_cb, axis=1)
    dq, dk, dv, dz_a = attn_fused_bwd(proj, do_a, o_mix, ltot, slopes, cfg, "attn_bwd")
    dproj = jnp.concatenate([dq, dk, dv, dz_a, dz_s] + dxbc + [dga, dgs], axis=1)
    g_w_main = matmul(hn, dproj, 'tn', 1024, 1024, 2048, BF16, "g_w_main")
    g_w_dt = matmul(hn, ddt_raw, 'tn', 1024, 128, 2048, BF16, "g_w_dt")
    g_w_in = jnp.concatenate([g_w_main[:, :dt0], g_w_dt[:, :cfg.SH], g_w_main[:, dt0:]], axis=1)
    send_in = g_w_in.reshape(D, N_DEV, cfg.N_IN // N_DEV).transpose(1, 0, 2)
    dhn_a, (r_in,) = matmul(dproj, w_main, 'nt', 1024, 1024, 2048, F32, "d_hn", side=([send_in], [False]))
    dhn_b = matmul(ddt_raw, w_dt, 'nt', 512, 1024, 128, F32, "d_hn_dt")
    grad_x, g_norm_w = rmsnorm_bwd(dhn_a, dhn_b, x, norm_w, dout, "rmsnorm_bwd")

    g_d_skip = jnp.sum(g_dvec.reshape(cfg.SH, cfg.P), axis=1).reshape(1, cfg.SH)
    small = dict(norm_w=g_norm_w, conv_b=g_conv_b, dt_bias=g_dt_bias[:, :cfg.SH], a_log=g_a_log[:, :cfg.SH],
                 d_skip=g_d_skip, ssm_norm_w=g_ssm_norm, final_norm_w=g_final_w, conv_w=g_conv_w[:cfg.KC])
    return loss_p, grad_x, small, dict(w_in=r_in, w_attn=r_attn, w_ssm=r_ssm, w_out=r_out)


def _mesh_pos():
    return lax.axis_index("x"), lax.axis_index("y"), lax.axis_index("c")


def _flat(pos):
    return 4 * pos[0] + 2 * pos[1] + pos[2]


def _exchange_shapes(arrs, gathers):
    return [jax.ShapeDtypeStruct(((N_DEV,) + a.shape) if g else a.shape, a.dtype) for a, g in zip(arrs, gathers)]


def _exchange_sems(n):
    return [pltpu.SemaphoreType.DMA((n * (N_DEV - 1),)), pltpu.SemaphoreType.DMA((n * (N_DEV - 1),)),
            pltpu.SemaphoreType.DMA((n,))]


def _exchange_copies(ins, outs, gathers, send_sems, recv_sems, loc_sems):
    pos = _mesh_pos()
    me = _flat(pos)
    starts, waits = [], []
    for a in range(len(ins)):
        mine = ins[a] if gathers[a] else ins[a].at[me]
        loc = pltpu.make_async_copy(mine, outs[a].at[me], loc_sems.at[a])
        starts.append(loc)
        waits.append(loc)
        for k in range(1, N_DEV):
            flip = ((k >> 2) & 1, (k >> 1) & 1, k & 1)
            peer = tuple(1 - p if f else p for p, f in zip(pos, flip))
            pk = _flat(peer)
            src = ins[a] if gathers[a] else ins[a].at[pk]
            sems = dict(send_sem=send_sems.at[a * (N_DEV - 1) + k - 1], recv_sem=recv_sems.at[a * (N_DEV - 1) + k - 1],
                        device_id=peer, device_id_type=pl.DeviceIdType.MESH)
            starts.append(pltpu.make_async_remote_copy(src_ref=src, dst_ref=outs[a].at[me], **sems))
            waits.append(pltpu.make_async_remote_copy(src_ref=src, dst_ref=outs[a].at[pk], **sems))
    return starts, waits


def exchange(arrs, gathers, name):
    n = len(arrs)

    def body(*refs):
        starts, waits = _exchange_copies(refs[:n], refs[n:2 * n], gathers, *refs[2 * n:])
        for cp in starts:
            cp.start()
        for cp in waits:
            cp.wait()

    hbm = pl.BlockSpec(memory_space=pltpu.HBM)
    return pl.pallas_call(
        body, name=name, in_specs=[hbm] * n, out_specs=[hbm] * n, out_shape=_exchange_shapes(arrs, gathers),
        scratch_shapes=_exchange_sems(n),
    )(*arrs)


def gather_two_level(arrs, chunks, name):
    n = len(arrs)

    def body(*refs):
        ins, outs = refs[:n], refs[n:2 * n]
        send_sems, recv_sems, loc_sems = refs[2 * n:]
        x, y, c = _mesh_pos()
        me, sib = (x, y, c), (x, y, 1 - c)
        chips = [(1 - x, y), (x, 1 - y), (1 - x, 1 - y)]
        plan, base = [], 0
        for a in range(n):
            step = arrs[a].shape[0] // chunks[a]
            for q in range(chunks[a]):
                plan.append((a, pl.ds(q * step, step), base))
                base += N_DEV - 1

        def copy(a, rows, sem, block, to, own=False):
            dst = outs[a].at[_flat(block), rows]
            return pltpu.make_async_remote_copy(
                src_ref=ins[a].at[rows] if own else dst, dst_ref=dst, send_sem=send_sems.at[sem],
                recv_sem=recv_sems.at[sem], device_id=to, device_id_type=pl.DeviceIdType.MESH)

        local = [pltpu.make_async_copy(ins[a], outs[a].at[_flat(me)], loc_sems.at[a]) for a in range(n)]
        for cp in local:
            cp.start()
        sent = []
        for a, rows, s in plan:
            sent.append(copy(a, rows, s, me, sib, own=True))
            sent += [copy(a, rows, s + 1 + j, me, (*chip, c), own=True) for j, chip in enumerate(chips)]
        for cp in sent:
            cp.start()
        for a, rows, s in plan:
            for j, chip in enumerate(chips):
                copy(a, rows, s + 1 + j, (*chip, c), me).wait_recv()
                passed = copy(a, rows, s + 4 + j, (*chip, c), sib)
                passed.start()
                sent.append(passed)
        for a, rows, s in plan:
            copy(a, rows, s, sib, me).wait_recv()
            for j, chip in enumerate(chips):
                copy(a, rows, s + 4 + j, (*chip, 1 - c), me).wait_recv()
        for cp in sent:
            cp.wait_send()
        for cp in local:
            cp.wait()

    hbm = pl.BlockSpec(memory_space=pltpu.HBM)
    nsem = (N_DEV - 1) * sum(chunks)
    return pl.pallas_call(
        body, name=name, in_specs=[hbm] * n, out_specs=[hbm] * n, out_shape=_exchange_shapes(arrs, [True] * n),
        scratch_shapes=[pltpu.SemaphoreType.DMA((nsem,)), pltpu.SemaphoreType.DMA((nsem,)),
                        pltpu.SemaphoreType.DMA((n,))],
    )(*arrs)


def with_exchange(body, n_in, n_out, gathers, grid):
    n = len(gathers)

    def wrapped(*refs):
        ins, sends = refs[:n_in], refs[n_in:n_in + n]
        outs, recvs = refs[n_in + n:n_in + n + n_out], refs[n_in + 2 * n + n_out - n:n_in + 2 * n + n_out]
        scratch, sems = refs[n_in + 2 * n + n_out:-3], refs[-3:]
        ids = [pl.program_id(d) for d in range(len(grid))]
        first = functools.reduce(jnp.logical_and, [i == 0 for i in ids])
        last = functools.reduce(jnp.logical_and, [i == g - 1 for i, g in zip(ids, grid)])

        @pl.when(first)
        def _():
            for cp in _exchange_copies(sends, recvs, gathers, *sems)[0]:
                cp.start()

        body(*ins, *outs, *scratch)

        @pl.when(last)
        def _():
            for cp in _exchange_copies(sends, recvs, gathers, *sems)[1]:
                cp.wait()

    return wrapped


def adamw(g_src, w, m, v, summed, name, tr=64):
    R, C = w.shape
    tr = min(tr, R)
    assert R % tr == 0

    def body(g_ref, w_ref, m_ref, v_ref, g_out, d_out, m_out, v_out):
        if summed:
            g = g_ref[0].astype(F32)
            for j in range(1, N_DEV):
                g = g + g_ref[j].astype(F32)
        else:
            g = g_ref[...]
        mn = ADAM_B1 * m_ref[...] + (1.0 - ADAM_B1) * g
        vn = ADAM_B2 * v_ref[...] + (1.0 - ADAM_B2) * (g * g)
        m_hat = mn / (1.0 - ADAM_B1 ** ADAM_STEP)
        v_hat = vn / (1.0 - ADAM_B2 ** ADAM_STEP)
        g_out[...] = g
        d_out[...] = -ADAM_LR * (m_hat / (jnp.sqrt(v_hat) + ADAM_EPS) + ADAM_WD * w_ref[...])
        m_out[...] = mn
        v_out[...] = vn

    row = pl.BlockSpec((tr, C), lambda i: (i, 0))
    gspec = pl.BlockSpec((N_DEV, tr, C), lambda i: (0, i, 0)) if summed else row
    sh = jax.ShapeDtypeStruct((R, C), F32)
    return pl.pallas_call(
        body, name=name, grid=(R // tr,), in_specs=[gspec, row, row, row], out_specs=[row] * 4, out_shape=[sh] * 4,
        compiler_params=_cp(("parallel",)),
    )(g_src, w, m, v)


SMALL = ('norm_w', 'conv_b', 'dt_bias', 'a_log', 'd_skip', 'ssm_norm_w', 'final_norm_w')


def _rows(n):
    return -(-n // (8 * LANES)) * 8


def _pack(vals):
    parts = []
    for a in vals:
        f = a.reshape(-1)
        parts.append(jnp.pad(f, (0, _rows(f.size) * LANES - f.size)).reshape(-1, LANES))
    return jnp.concatenate(parts, axis=0)


def _unpack(packed, shapes):
    out, r = [], 0
    for s in shapes:
        n = math.prod(s)
        out.append(packed[r:r + _rows(n)].reshape(-1)[:n].reshape(s))
        r += _rows(n)
    return out


def kernel(x, norm_w, w_in, conv_w, conv_b, dt_bias, a_log, d_skip, ssm_norm_w, w_attn_branch, w_ssm_branch, w_out, final_norm_w, loss_target, m_norm_w, m_w_in, m_conv_w, m_conv_b, m_dt_bias, m_a_log, m_d_skip, m_ssm_norm_w, m_w_attn_branch, m_w_ssm_branch, m_w_out, m_final_norm_w, v_norm_w, v_w_in, v_conv_w, v_conv_b, v_dt_bias, v_a_log, v_d_skip, v_ssm_norm_w, v_w_attn_branch, v_w_ssm_branch, v_w_out, v_final_norm_w):
    cfg = CFG
    D, SH = cfg.D, cfg.SH
    me = _flat(_mesh_pos())
    dt0 = 4 * cfg.AW + cfg.SI + cfg.CD
    ws = w_in.shape[-1]

    g_in, g_cw = gather_two_level([w_in[0].astype(BF16), conv_w[0]], [4, 1], "gather_w_in")
    w_full = g_in.transpose(1, 0, 2).reshape(D, N_DEV * ws)
    w_main = jnp.concatenate([w_full[:, :dt0], w_full[:, dt0 + SH:]], axis=1)
    w_dt = _pad_lanes(w_full[:, dt0:dt0 + SH])
    conv_full = g_cw.transpose(1, 0, 2).reshape(cfg.KC, cfg.CD)
    shards = [w_attn_branch[0].astype(BF16), w_ssm_branch[0].astype(BF16), w_out[0].astype(BF16)]

    loss_p, grad_x, small, recv = local_step(
        cfg, x[0], loss_target[0], norm_w, conv_full, conv_b, dt_bias, a_log, d_skip,
        ssm_norm_w, final_norm_w, w_main, w_dt, shards, dt0)

    upd = {}
    upd['w_in'] = adamw(recv['w_in'], w_in[0], m_w_in[0], v_w_in[0], True, "adamw_w_in")
    upd['w_attn_branch'] = adamw(recv['w_attn'], w_attn_branch[0], m_w_attn_branch[0], v_w_attn_branch[0], True,
                                 "adamw_w_attn")
    upd['w_ssm_branch'] = adamw(recv['w_ssm'], w_ssm_branch[0], m_w_ssm_branch[0], v_w_ssm_branch[0], True,
                                "adamw_w_ssm")
    upd['w_out'] = adamw(recv['w_out'], w_out[0], m_w_out[0], v_w_out[0], True, "adamw_w_out")

    extra = [jnp.zeros((cfg.KC, cfg.CD), F32), jnp.zeros((1, 1), F32)]
    shapes = [small[n].shape for n in SMALL] + [e.shape for e in extra]
    part = _pack([small[n] for n in SMALL] + [small['conv_w'], loss_p[:, :1]])
    gathered, = exchange([part], [True], "gather_small")
    given = dict(norm_w=(norm_w, m_norm_w, v_norm_w), conv_b=(conv_b, m_conv_b, v_conv_b),
                 dt_bias=(dt_bias, m_dt_bias, v_dt_bias), a_log=(a_log, m_a_log, v_a_log),
                 d_skip=(d_skip, m_d_skip, v_d_skip), ssm_norm_w=(ssm_norm_w, m_ssm_norm_w, v_ssm_norm_w),
                 final_norm_w=(final_norm_w, m_final_norm_w, v_final_norm_w))
    packed = [_pack([given[n][t] for n in SMALL] + extra) for t in range(3)]
    outs = adamw(gathered, *packed, True, "adamw_small", tr=part.shape[0])
    unpacked = [_unpack(o, shapes) for o in outs]
    for i, n in enumerate(SMALL):
        upd[n] = [u[i].reshape(given[n][0].shape) for u in unpacked]
    loss = unpacked[0][-1].reshape(())
    cw = conv_w.shape[-1]
    g_cw_mine = lax.dynamic_slice_in_dim(unpacked[0][-2], me * cw, cw, axis=1)
    upd['conv_w'] = adamw(g_cw_mine.reshape(-1, LANES), conv_w.reshape(-1, LANES), m_conv_w.reshape(-1, LANES),
                          v_conv_w.reshape(-1, LANES), False, "adamw_conv_w")

    order = ['norm_w', 'w_in', 'conv_w', 'conv_b', 'dt_bias', 'a_log', 'd_skip', 'ssm_norm_w', 'w_attn_branch',
             'w_ssm_branch', 'w_out', 'final_norm_w']
    like = dict(norm_w=norm_w, w_in=w_in, conv_w=conv_w, conv_b=conv_b, dt_bias=dt_bias, a_log=a_log, d_skip=d_skip,
                ssm_norm_w=ssm_norm_w, w_attn_branch=w_attn_branch, w_ssm_branch=w_ssm_branch, w_out=w_out,
                final_norm_w=final_norm_w)
    result = [loss, grad_x[None]]
    for t in range(4):
        result += [upd[n][t].reshape(like[n].shape) for n in order]
    return tuple(result)
```

```python
import functools
import math
from typing import NamedTuple

import jax
import jax.numpy as jnp
from jax import lax
from jax.experimental import pallas as pl
from jax.experimental.pallas import tpu as pltpu

F32 = jnp.float32
BF16 = jnp.bfloat16
RMS_EPS = 1e-6
NEG = -1e30
N_DEV = 8
LANES = 128
ATTN_BLOCK = 128
ADAM_LR, ADAM_B1, ADAM_B2, ADAM_EPS, ADAM_WD, ADAM_STEP = 0.001, 0.9, 0.999, 1e-08, 0.01, 10
VMEM_LIMIT = 56 * 1024 * 1024


class Cfg(NamedTuple):
    D: int = 2048
    S: int = 8192
    AH: int = 16
    E: int = 128
    HB: int = 4
    patterns: tuple = ((128, 1), (512, 4), (2048, 16))
    SI: int = 4096
    P: int = 64
    SG: int = 8
    SN: int = 128
    KC: int = 4
    L: int = 128

    @property
    def AW(self): return self.AH * self.E
    @property
    def SH(self): return self.SI // self.P
    @property
    def HPG(self): return self.SH // self.SG
    @property
    def GN(self): return self.SG * self.SN
    @property
    def CD(self): return self.SI + 2 * self.GN
    @property
    def k0(self): return self.AW
    @property
    def v0(self): return 2 * self.AW
    @property
    def za0(self): return 3 * self.AW
    @property
    def zs0(self): return 4 * self.AW
    @property
    def xbc0(self): return 4 * self.AW + self.SI
    @property
    def ga0(self): return self.xbc0 + self.CD
    @property
    def gs0(self): return self.ga0 + self.D
    @property
    def NP(self): return self.gs0 + self.D
    @property
    def N_IN(self): return self.NP + self.SH


CFG = Cfg()


def _cp(sem=None, vmem=VMEM_LIMIT):
    return pltpu.CompilerParams(dimension_semantics=sem, vmem_limit_bytes=vmem)


def _sigmoid(z):
    return 1.0 / (1.0 + jnp.exp(-z))


def _dot(a, b, dims):
    return lax.dot_general(a, b, (dims, ((), ())), preferred_element_type=F32)


NN = ((1,), (0,))
NT = ((1,), (1,))
TN = ((0,), (0,))


def _blk(off, width):
    assert off % width == 0, (off, width)
    return off // width


def matmul(a, b, mode, tm, tn, tk, out_dtype, name, side=None):
    if mode == 'nn':
        (M, K), (_, N) = a.shape, b.shape
    elif mode == 'nt':
        (M, K), (N, _) = a.shape, b.shape
    else:
        (K, M), (_, N) = a.shape, b.shape
    tm, tn, tk = min(tm, M), min(tn, N), min(tk, K)
    assert M % tm == 0 and N % tn == 0 and K % tk == 0, (M, N, K, tm, tn, tk)
    nk = K // tk
    dims = {'nn': NN, 'nt': NT, 'tn': TN}[mode]

    def body(a_ref, b_ref, o_ref, *acc):
        part = _dot(a_ref[...].astype(BF16), b_ref[...].astype(BF16), dims)
        if nk == 1:
            o_ref[...] = part.astype(out_dtype)
        else:
            acc_ref, = acc
            k = pl.program_id(2)

            @pl.when(k == 0)
            def _():
                acc_ref[...] = part

            @pl.when(k > 0)
            def _():
                acc_ref[...] += part

            @pl.when(k == nk - 1)
            def _():
                o_ref[...] = acc_ref[...].astype(out_dtype)

    if mode == 'tn':
        a_spec = pl.BlockSpec((tk, tm), lambda n, m, k: (k, m))
    else:
        a_spec = pl.BlockSpec((tm, tk), lambda n, m, k: (m, k))
    if mode == 'nt':
        b_spec = pl.BlockSpec((tn, tk), lambda n, m, k: (n, k))
    else:
        b_spec = pl.BlockSpec((tk, tn), lambda n, m, k: (k, n))
    grid = (N // tn, M // tm, nk)
    o_spec = pl.BlockSpec((tm, tn), lambda n, m, k: (m, n))
    o_shape = jax.ShapeDtypeStruct((M, N), out_dtype)
    acc = [] if nk == 1 else [pltpu.VMEM((tm, tn), F32)]
    if side is None:
        return pl.pallas_call(
            body, name=name, grid=grid, in_specs=[a_spec, b_spec], out_specs=o_spec, out_shape=o_shape,
            scratch_shapes=acc, compiler_params=_cp(("parallel", "parallel", "arbitrary")),
        )(a, b)
    arrs, gathers = side
    whole = pl.BlockSpec(memory_space=pl.ANY)
    res = pl.pallas_call(
        with_exchange(body, 2, 1, gathers, grid), name=name, grid=grid,
        in_specs=[a_spec, b_spec] + [whole] * len(arrs), out_specs=[o_spec] + [whole] * len(arrs),
        out_shape=[o_shape] + _exchange_shapes(arrs, gathers),
        scratch_shapes=acc + _exchange_sems(len(arrs)),
        compiler_params=_cp(("arbitrary", "arbitrary", "arbitrary")),
    )(a, b, *arrs)
    return res[0], res[1:]


def rmsnorm_fwd(x, w, name, tm=256):
    S, D = x.shape

    def body(x_ref, w_ref, o_ref):
        xv = x_ref[...]
        r = lax.rsqrt(jnp.mean(xv * xv, axis=-1, keepdims=True) + RMS_EPS)
        o_ref[...] = ((xv * r) * w_ref[...]).astype(BF16)

    return pl.pallas_call(
        body, name=name, grid=(S // tm,),
        in_specs=[pl.BlockSpec((tm, D), lambda i: (i, 0)), pl.BlockSpec((1, D), lambda i: (0, 0))],
        out_specs=pl.BlockSpec((tm, D), lambda i: (i, 0)),
        out_shape=jax.ShapeDtypeStruct((S, D), BF16),
        compiler_params=_cp(("parallel",)),
    )(x, w)


def rmsnorm_bwd(dh_a, dh_b, x, w, dout, name, tm=128):
    S, D = x.shape

    def body(da_ref, db_ref, x_ref, w_ref, do_ref, gx_ref, gw_ref):
        xv = x_ref[...]
        dh = da_ref[...] + db_ref[...]
        r = lax.rsqrt(jnp.mean(xv * xv, axis=-1, keepdims=True) + RMS_EPS)
        g = dh * w_ref[...]
        dx = r * g - xv * (r * r * r) * jnp.mean(g * xv, axis=-1, keepdims=True)
        gx_ref[...] = do_ref[...] + dx
        gw = jnp.sum(dh * (xv * r), axis=0, keepdims=True)

        @pl.when(pl.program_id(0) == 0)
        def _():
            gw_ref[...] = gw

        @pl.when(pl.program_id(0) > 0)
        def _():
            gw_ref[...] += gw

    row = pl.BlockSpec((tm, D), lambda i: (i, 0))
    vec = pl.BlockSpec((1, D), lambda i: (0, 0))
    return pl.pallas_call(
        body, name=name, grid=(S // tm,),
        in_specs=[row, row, row, vec, row],
        out_specs=[row, vec],
        out_shape=[jax.ShapeDtypeStruct((S, D), F32), jax.ShapeDtypeStruct((1, D), F32)],
        compiler_params=_cp(("arbitrary",)),
    )(dh_a, dh_b, x, w, dout)


def out_proj_final(merged, w_out, x, fw, tgt, name, tm=256):
    S, D = x.shape

    def body(m_ref, wo_ref, x_ref, w_ref, t_ref, do_ref, loss_ref, gw_ref):
        out = x_ref[...] + _dot(m_ref[...], wo_ref[...], NN)
        w = w_ref[...]
        r = lax.rsqrt(jnp.mean(out * out, axis=-1, keepdims=True) + RMS_EPS)
        yn = out * r
        err = yn * w - t_ref[...]
        lrow = 0.5 * jnp.mean(err * err, axis=-1, keepdims=True)
        lsum = jnp.zeros((1, LANES), F32) + jnp.sum(lrow, axis=0, keepdims=True)
        dfin = err * (1.0 / D)
        g = dfin * w
        do_ref[...] = r * g - out * (r * r * r) * jnp.mean(g * out, axis=-1, keepdims=True)
        gw = jnp.sum(dfin * yn, axis=0, keepdims=True)

        @pl.when(pl.program_id(0) == 0)
        def _():
            gw_ref[...] = gw
            loss_ref[...] = lsum

        @pl.when(pl.program_id(0) > 0)
        def _():
            gw_ref[...] += gw
            loss_ref[...] += lsum

    row = pl.BlockSpec((tm, D), lambda i: (i, 0))
    vec = pl.BlockSpec((1, D), lambda i: (0, 0))
    return pl.pallas_call(
        body, name=name, grid=(S // tm,),
        in_specs=[row, pl.BlockSpec((D, D), lambda i: (0, 0)), row, vec, row],
        out_specs=[row, pl.BlockSpec((1, LANES), lambda i: (0, 0)), vec],
        out_shape=[jax.ShapeDtypeStruct((S, D), F32), jax.ShapeDtypeStruct((1, LANES), F32),
                   jax.ShapeDtypeStruct((1, D), F32)],
        compiler_params=_cp(("arbitrary",)),
    )(merged, w_out, x, fw, tgt)


def branch_merge(o_a, w_attn, y_n, w_ssm, proj, cfg, name, tm=512, tn=512):
    S, D = cfg.S, cfg.D
    tm, tn = min(tm, S), min(tn, D)

    def body(oa_ref, wa_ref, yn_ref, ws_ref, ga_ref, gs_ref, a_ref, s_ref, m_ref):
        a = _dot(oa_ref[...], wa_ref[...], NN)
        sv = _dot(yn_ref[...], ws_ref[...], NN)
        a_ref[...] = a.astype(BF16)
        s_ref[...] = sv.astype(BF16)
        m_ref[...] = (_sigmoid(ga_ref[...]) * a + _sigmoid(gs_ref[...]) * sv).astype(BF16)

    tile = pl.BlockSpec((tm, tn), lambda n, m: (m, n))
    return pl.pallas_call(
        body, name=name, grid=(D // tn, S // tm),
        in_specs=[pl.BlockSpec((tm, cfg.AW), lambda n, m: (m, 0)), pl.BlockSpec((cfg.AW, tn), lambda n, m: (0, n)),
                  pl.BlockSpec((tm, cfg.SI), lambda n, m: (m, 0)), pl.BlockSpec((cfg.SI, tn), lambda n, m: (0, n)),
                  pl.BlockSpec((tm, tn), lambda n, m: (m, _blk(cfg.ga0, tn) + n)),
                  pl.BlockSpec((tm, tn), lambda n, m: (m, _blk(cfg.gs0, tn) + n))],
        out_specs=[tile, tile, tile],
        out_shape=[jax.ShapeDtypeStruct((S, D), BF16)] * 3,
        compiler_params=_cp(("parallel", "parallel")),
    )(o_a, w_attn, y_n, w_ssm, proj, proj)


def merge_bwd(dout, w_out, a_out, s_out, proj, cfg, name, tm=512, tn=1024):
    S, D = cfg.S, cfg.D
    tm, tn = min(tm, S), min(tn, D)

    def body(do_ref, wo_ref, a_ref, s_ref, ga_ref, gs_ref, da_ref, ds_ref, dga_ref, dgs_ref):
        dmv = _dot(do_ref[...].astype(BF16), wo_ref[...], NT)
        sa = _sigmoid(ga_ref[...])
        ss = _sigmoid(gs_ref[...])
        da_ref[...] = (dmv * sa).astype(BF16)
        ds_ref[...] = (dmv * ss).astype(BF16)
        dga_ref[...] = (dmv * a_ref[...] * (sa * (1.0 - sa))).astype(BF16)
        dgs_ref[...] = (dmv * s_ref[...] * (ss * (1.0 - ss))).astype(BF16)

    tile = pl.BlockSpec((tm, tn), lambda n, m: (m, n))
    sh = jax.ShapeDtypeStruct((S, D), BF16)
    return pl.pallas_call(
        body, name=name, grid=(D // tn, S // tm),
        in_specs=[pl.BlockSpec((tm, D), lambda n, m: (m, 0)), pl.BlockSpec((tn, D), lambda n, m: (n, 0)), tile, tile,
                  pl.BlockSpec((tm, tn), lambda n, m: (m, _blk(cfg.ga0, tn) + n)),
                  pl.BlockSpec((tm, tn), lambda n, m: (m, _blk(cfg.gs0, tn) + n))],
        out_specs=[tile] * 4, out_shape=[sh] * 4,
        compiler_params=_cp(("parallel", "parallel")),
    )(dout, w_out, a_out, s_out, proj, proj)


def _attn_rows(base, d):
    return pl.ds(base, ATTN_BLOCK) if d == 1 else pl.ds(base, ATTN_BLOCK, stride=d)


def _attn_units(cfg):
    dmax = max(d for _, d in cfg.patterns)
    units = []
    for p, (window, d) in enumerate(cfg.patterns):
        assert window // d == ATTN_BLOCK and dmax % d == 0
        nsub = dmax // d
        for b in range(nsub):
            for r in range(d):
                base = b * ATTN_BLOCK * d + r
                if b > 0:
                    units.append((p, d, base, (b - 1) * ATTN_BLOCK * d + r, False))
                else:
                    units.append((p, d, base, (nsub - 1) * ATTN_BLOCK * d + r, True))
    return units, ATTN_BLOCK * dmax


def _set_bias_tiles(bias_s, slope, cfg):
    qi = lax.broadcasted_iota(jnp.int32, (ATTN_BLOCK, ATTN_BLOCK), 0)
    ki = lax.broadcasted_iota(jnp.int32, (ATTN_BLOCK, ATTN_BLOCK), 1)
    for p, (_, d) in enumerate(cfg.patterns):
        bias_s[2 * p] = jnp.where(ki >= qi, (-slope) * ((ATTN_BLOCK + qi - ki) * d).astype(F32), NEG)
        bias_s[2 * p + 1] = jnp.where(ki <= qi, (-slope) * ((qi - ki) * d).astype(F32), NEG)


def _unit_scores(q, kp, kc, bias_s, p, prev_ok, scale):
    sp = _dot(q, kp, NT) * scale + bias_s[2 * p]
    if prev_ok is not None:
        sp = jnp.where(prev_ok, sp, NEG)
    return sp, _dot(q, kc, NT) * scale + bias_s[2 * p + 1]


def _slope_table(cfg):
    slopes = jnp.asarray([2.0 ** (-8.0 * (h + 1) / cfg.AH) for h in range(cfg.AH)], F32)
    return jnp.broadcast_to(slopes.reshape(cfg.AH, 1, 1), (cfg.AH, 8, LANES))


def attn_fused_fwd(proj, slopes, cfg, name):
    S, E, AH = cfg.S, cfg.E, cfg.AH
    units, SB = _attn_units(cfg)
    assert S % SB == 0
    npat = len(cfg.patterns)
    scale = E ** -0.5

    def spec(off, prev):
        c0 = _blk(off, E)
        if prev:
            return pl.BlockSpec((SB, E), lambda h, i: (jnp.maximum(i - 1, 0), c0 + h))
        return pl.BlockSpec((SB, E), lambda h, i: (i, c0 + h))

    def body(q_ref, kp_ref, kc_ref, vp_ref, vc_ref, z_ref, sl_ref, oa_ref, om_ref, lt_ref, *scr):
        o_s, l_s, bias_s = scr[:npat], scr[npat:2 * npat], scr[2 * npat]
        i = pl.program_id(1)

        @pl.when(i == 0)
        def _():
            _set_bias_tiles(bias_s, sl_ref[0, 0:1, :], cfg)

        for p, d, base, pbase, from_prev in units:
            rows, prows = _attn_rows(base, d), _attn_rows(pbase, d)
            q = q_ref[rows, :].astype(BF16)
            kp = (kp_ref if from_prev else kc_ref)[prows, :].astype(BF16)
            vp = (vp_ref if from_prev else vc_ref)[prows, :].astype(BF16)
            sp, sc = _unit_scores(q, kp, kc_ref[rows, :].astype(BF16), bias_s, p, (i > 0) if from_prev else None, scale)
            m = jnp.max(jnp.maximum(sp, sc), axis=1, keepdims=True)
            pp = jnp.exp(sp - m)
            pc = jnp.exp(sc - m)
            l = jnp.sum(pp + pc, axis=1, keepdims=True)
            o_s[p][rows, :] = (_dot(pp.astype(BF16), vp, NN) +
                               _dot(pc.astype(BF16), vc_ref[rows, :].astype(BF16), NN)) * (1.0 / l)
            l_s[p][rows, :] = m + jnp.log(l)
        ls = [l_s[p][...] for p in range(npat)]
        m = functools.reduce(jnp.maximum, ls)
        lt = m + jnp.log(sum(jnp.exp(l_ - m) for l_ in ls))
        lt_ref[...] = lt
        mix = sum(jnp.exp(ls[p] - lt) * o_s[p][...] for p in range(npat))
        om_ref[...] = mix
        z = z_ref[...]
        oa_ref[...] = (mix * (z * _sigmoid(z))).astype(BF16)

    out = pl.BlockSpec((SB, E), lambda h, i: (i, h))
    return pl.pallas_call(
        body, name=name, grid=(AH, S // SB),
        in_specs=[spec(0, False), spec(cfg.k0, True), spec(cfg.k0, False), spec(cfg.v0, True), spec(cfg.v0, False),
                  spec(cfg.za0, False), pl.BlockSpec((1, 8, LANES), lambda h, i: (h, 0, 0))],
        out_specs=[out, out, pl.BlockSpec((SB, 1), lambda h, i: (h * (S // SB) + i, 0))],
        out_shape=[jax.ShapeDtypeStruct((S, cfg.AW), BF16), jax.ShapeDtypeStruct((S, cfg.AW), F32),
                   jax.ShapeDtypeStruct((AH * S, 1), F32)],
        scratch_shapes=[pltpu.VMEM((SB, E), F32)] * npat + [pltpu.VMEM((SB, 1), F32)] * npat +
        [pltpu.VMEM((2 * npat, ATTN_BLOCK, ATTN_BLOCK), F32)],
        compiler_params=_cp(("parallel", "arbitrary")),
    )(proj, proj, proj, proj, proj, proj, slopes)


def attn_fused_bwd(proj, do_a, o_mix, ltot, slopes, cfg, name):
    S, E, AH = cfg.S, cfg.E, cfg.AH
    units, SB = _attn_units(cfg)
    nsb = S // SB
    last = nsb - 1
    scale = E ** -0.5

    def spec(off, prev):
        c0 = _blk(off, E)
        if prev:
            return pl.BlockSpec((SB, E), lambda h, i: (jnp.maximum(i - 1, 0), c0 + h))
        return pl.BlockSpec((SB, E), lambda h, i: (jnp.minimum(i, last), c0 + h))

    cur = pl.BlockSpec((SB, E), lambda h, i: (jnp.minimum(i, last), h))
    prev = pl.BlockSpec((SB, E), lambda h, i: (jnp.maximum(i - 1, 0), h))

    def body(q_ref, kp_ref, kc_ref, vp_ref, vc_ref, z_ref, doa_ref, om_ref, lt_ref, sl_ref,
             dq_ref, dk_ref, dv_ref, dz_ref, dmix_s, dl_s, dq_s, dkp_s, dvp_s, dkc_s, dvc_s, bias_s):
        i = pl.program_id(1)

        @pl.when(i == 0)
        def _():
            dkc_s[...] = jnp.zeros_like(dkc_s)
            dvc_s[...] = jnp.zeros_like(dvc_s)
            _set_bias_tiles(bias_s, sl_ref[0, 0:1, :], cfg)

        @pl.when(i < nsb)
        def _():
            z = z_ref[...]
            s = _sigmoid(z)
            doa = doa_ref[...].astype(F32)
            om = om_ref[...]
            dmix = doa * (z * s)
            dmix_s[...] = dmix
            dz_ref[...] = (doa * om * (s * (1.0 + z * (1.0 - s)))).astype(BF16)
            dl_s[...] = jnp.sum(dmix * om, axis=1, keepdims=True)
            dkp_s[...] = dkc_s[...]
            dvp_s[...] = dvc_s[...]
            dkc_s[...] = jnp.zeros_like(dkc_s)
            dvc_s[...] = jnp.zeros_like(dvc_s)
            dq_s[...] = jnp.zeros_like(dq_s)
            for p, d, base, pbase, from_prev in units:
                rows, prows = _attn_rows(base, d), _attn_rows(pbase, d)
                q = q_ref[rows, :].astype(BF16)
                kc = kc_ref[rows, :].astype(BF16)
                kp = (kp_ref if from_prev else kc_ref)[prows, :].astype(BF16)
                vp = (vp_ref if from_prev else vc_ref)[prows, :].astype(BF16)
                do = dmix_s[rows, :].astype(BF16)
                lt = lt_ref[rows, :]
                dlt = dl_s[rows, :]
                sp, sc = _unit_scores(q, kp, kc, bias_s, p, (i > 0) if from_prev else None, scale)
                pp = jnp.exp(sp - lt)
                pc = jnp.exp(sc - lt)
                dsp = (pp * (_dot(do, vp, NT) - dlt) * scale).astype(BF16)
                dsc = (pc * (_dot(do, vc_ref[rows, :].astype(BF16), NT) - dlt) * scale).astype(BF16)
                dq_s[rows, :] += _dot(dsp, kp, NN) + _dot(dsc, kc, NN)
                dkc_s[rows, :] += _dot(dsc, q, TN)
                dvc_s[rows, :] += _dot(pc.astype(BF16), do, TN)
                dk_t, dv_t = (dkp_s, dvp_s) if from_prev else (dkc_s, dvc_s)
                dk_t[prows, :] += _dot(dsp, q, TN)
                dv_t[prows, :] += _dot(pp.astype(BF16), do, TN)
            dq_ref[...] = dq_s[...].astype(BF16)
            dk_ref[...] = dkp_s[...].astype(BF16)
            dv_ref[...] = dvp_s[...].astype(BF16)

        @pl.when(i == nsb)
        def _():
            dk_ref[...] = dkc_s[...].astype(BF16)
            dv_ref[...] = dvc_s[...].astype(BF16)

    sh = jax.ShapeDtypeStruct((S, cfg.AW), BF16)
    acc = pltpu.VMEM((SB, E), F32)
    return pl.pallas_call(
        body, name=name, grid=(AH, nsb + 1),
        in_specs=[spec(0, False), spec(cfg.k0, True), spec(cfg.k0, False), spec(cfg.v0, True), spec(cfg.v0, False),
                  spec(cfg.za0, False), cur, cur,
                  pl.BlockSpec((SB, 1), lambda h, i: (h * nsb + jnp.minimum(i, last), 0)),
                  pl.BlockSpec((1, 8, LANES), lambda h, i: (h, 0, 0))],
        out_specs=[cur, prev, prev, cur], out_shape=[sh] * 4,
        scratch_shapes=[acc, pltpu.VMEM((SB, 1), F32), acc, acc, acc, acc, acc,
                        pltpu.VMEM((2 * len(cfg.patterns), ATTN_BLOCK, ATTN_BLOCK), F32)],
        compiler_params=_cp(("parallel", "arbitrary")),
    )(proj, proj, proj, proj, proj, proj, do_a, o_mix, ltot, slopes)


HALO = 8


def _conv_taps(x_ref, h_ref, kc):
    x = x_ref[...]
    full = jnp.concatenate([jnp.where(pl.program_id(1) == 0, 0.0, h_ref[...]), x], axis=0)
    return [pltpu.roll(full, s, axis=0)[HALO:, :] for s in range(kc - 1, 0, -1)] + [x]


def _conv_pre(taps, w_ref, b_ref):
    pre = b_ref[...] + w_ref[0:1, :] * taps[0]
    for k in range(1, len(taps)):
        pre = pre + w_ref[k:k + 1, :] * taps[k]
    return pre


def conv_fwd(proj, w, b, cfg, name, tm=1024, tc=512):
    S, CD, KC = cfg.S, cfg.CD, cfg.KC
    tc = min(tc, CD)
    c0 = _blk(cfg.xbc0, tc)
    hb = tm // HALO

    def body(x_ref, h_ref, w_ref, b_ref, o_ref):
        pre = _conv_pre(_conv_taps(x_ref, h_ref, KC), w_ref, b_ref)
        o_ref[...] = pre * _sigmoid(pre)

    return pl.pallas_call(
        body, name=name, grid=(CD // tc, S // tm),
        in_specs=[pl.BlockSpec((tm, tc), lambda c, i: (i, c0 + c)),
                  pl.BlockSpec((HALO, tc), lambda c, i: (jnp.maximum(i * hb - 1, 0), c0 + c)),
                  pl.BlockSpec((KC, tc), lambda c, i: (0, c)),
                  pl.BlockSpec((1, tc), lambda c, i: (0, c))],
        out_specs=pl.BlockSpec((tm, tc), lambda c, i: (i, c)),
        out_shape=jax.ShapeDtypeStruct((S, CD), F32),
        compiler_params=_cp(("parallel", "arbitrary")),
    )(proj, proj, w, b)


def conv_bwd_a(proj, dxc, w, b, cfg, name, c_off, tm=1024, tc=512):
    S, KC = cfg.S, cfg.KC
    CD = dxc.shape[1]
    tc = min(tc, CD)
    c0 = _blk(cfg.xbc0 + c_off, tc)
    w0 = _blk(c_off, tc)
    hb = tm // HALO

    def body(x_ref, h_ref, d_ref, w_ref, b_ref, dp_ref, gw_ref, gb_ref):
        taps = _conv_taps(x_ref, h_ref, KC)
        pre = _conv_pre(taps, w_ref, b_ref)
        s = _sigmoid(pre)
        dpre = d_ref[...] * (s * (1.0 + pre * (1.0 - s)))
        dp_ref[...] = dpre
        gb = jnp.sum(dpre, axis=0, keepdims=True)
        gws = [jnp.sum(dpre * taps[k], axis=0, keepdims=True) for k in range(KC)]
        gw = jnp.concatenate(gws + [jnp.zeros((8 - KC, tc), F32)], axis=0)

        @pl.when(pl.program_id(1) == 0)
        def _():
            gw_ref[...] = gw
            gb_ref[...] = gb

        @pl.when(pl.program_id(1) > 0)
        def _():
            gw_ref[...] += gw
            gb_ref[...] += gb

    return pl.pallas_call(
        body, name=name, grid=(CD // tc, S // tm),
        in_specs=[pl.BlockSpec((tm, tc), lambda c, i: (i, c0 + c)),
                  pl.BlockSpec((HALO, tc), lambda c, i: (jnp.maximum(i * hb - 1, 0), c0 + c)),
                  pl.BlockSpec((tm, tc), lambda c, i: (i, c)),
                  pl.BlockSpec((KC, tc), lambda c, i: (0, w0 + c)),
                  pl.BlockSpec((1, tc), lambda c, i: (0, w0 + c))],
        out_specs=[pl.BlockSpec((tm, tc), lambda c, i: (i, c)),
                   pl.BlockSpec((8, tc), lambda c, i: (0, c)),
                   pl.BlockSpec((1, tc), lambda c, i: (0, c))],
        out_shape=[jax.ShapeDtypeStruct((S, CD), F32), jax.ShapeDtypeStruct((8, CD), F32),
                   jax.ShapeDtypeStruct((1, CD), F32)],
        compiler_params=_cp(("parallel", "arbitrary")),
    )(proj, proj, dxc, w, b)


def conv_bwd_b(dpre, w, cfg, name, c_off, tm=1024, tc=512):
    S, KC = cfg.S, cfg.KC
    CD = dpre.shape[1]
    tc = min(tc, CD)
    w0 = _blk(c_off, tc)
    hb = tm // HALO
    nrb = S // tm
    last_h = S // HALO - 1

    def body(d_ref, h_ref, w_ref, o_ref):
        d = d_ref[...]
        full = jnp.concatenate([d, jnp.where(pl.program_id(1) == nrb - 1, 0.0, h_ref[...])], axis=0)
        acc = w_ref[KC - 1:KC, :] * d
        for j in range(1, KC):
            acc = acc + w_ref[KC - 1 - j:KC - j, :] * pltpu.roll(full, tm + HALO - j, axis=0)[:tm, :]
        o_ref[...] = acc.astype(BF16)

    return pl.pallas_call(
        body, name=name, grid=(CD // tc, nrb),
        in_specs=[pl.BlockSpec((tm, tc), lambda c, i: (i, c)),
                  pl.BlockSpec((HALO, tc), lambda c, i: (jnp.minimum((i + 1) * hb, last_h), c)),
                  pl.BlockSpec((KC, tc), lambda c, i: (0, w0 + c))],
        out_specs=pl.BlockSpec((tm, tc), lambda c, i: (i, c)),
        out_shape=jax.ShapeDtypeStruct((S, CD), BF16),
        compiler_params=_cp(("parallel", "arbitrary")),
    )(dpre, dpre, w)


def _pad_lanes(v, width=LANES):
    return jnp.pad(v, ((0, 0), (0, width - v.shape[1])))


def ssd_prep(dt_raw, dt_bias, a_log, cfg, name):
    S, L = cfg.S, cfg.L

    def body(x_ref, b_ref, al_ref, dt_ref, ac_ref):
        x = x_ref[...] + b_ref[...]
        dt = jnp.maximum(x, 0.0) + jnp.log(1.0 + jnp.exp(-jnp.abs(x)))
        da = dt * (-jnp.exp(al_ref[...]))
        li = lax.broadcasted_iota(jnp.int32, (L, L), 0)
        si = lax.broadcasted_iota(jnp.int32, (L, L), 1)
        tri = jnp.where(li >= si, 1.0, 0.0).astype(F32)
        dt_ref[...] = dt
        ac_ref[...] = lax.dot_general(tri, da, ((NN), ((), ())), precision=lax.Precision.HIGHEST,
                                      preferred_element_type=F32)

    row = pl.BlockSpec((L, LANES), lambda i: (i, 0))
    vec = pl.BlockSpec((1, LANES), lambda i: (0, 0))
    sh = jax.ShapeDtypeStruct((S, LANES), F32)
    return pl.pallas_call(
        body, name=name, grid=(S // L,), in_specs=[row, vec, vec], out_specs=[row, row], out_shape=[sh, sh],
        compiler_params=_cp(("parallel",)),
    )(dt_raw, dt_bias, a_log)


def _spread(v, n):
    return jnp.broadcast_to(v[:, :, None], v.shape + (n,)).reshape(v.shape[0], v.shape[1] * n)


def _head_selectors(cfg):
    def sel(width):
        head = jnp.arange(LANES)[None, :, None]
        slot = jnp.arange(cfg.SG)[:, None, None] * cfg.HPG + (jnp.arange(cfg.HPG * width) // width)[None, None, :]
        return (head == slot).astype(BF16)
    return sel(cfg.P), sel(LANES)


def _spread_heads(v, sel):
    hi = v.astype(BF16)
    r1 = v - hi.astype(F32)
    mid = r1.astype(BF16)
    lo = (r1 - mid.astype(F32)).astype(BF16)
    return _dot(hi, sel, NN) + _dot(mid, sel, NN) + _dot(lo, sel, NN)


def _pair_select(halves, p):
    low = lax.broadcasted_iota(jnp.int32, halves[0].shape, 1) < p
    return jnp.where(low, halves[0], halves[1])


def _head_rows(row, hpg, p):
    return jnp.concatenate([jnp.broadcast_to(row[:, j * LANES:(j + 1) * LANES], (p, LANES)) for j in range(hpg)],
                           axis=0)


def _segment_sums(t, sel):
    hi = t.astype(BF16)
    lo = (t - hi.astype(F32)).astype(BF16)
    return _dot(hi, sel, NT) + _dot(lo, sel, NT)


def ssd_scan_fwd(xc, dt, acum, act, sel_p, sel_l, cfg, name):
    S, L, P, SN, HPG, SG, SI = cfg.S, cfg.L, cfg.P, cfg.SN, cfg.HPG, cfg.SG, cfg.SI
    nc = S // L
    GW = HPG * P
    bcol, ccol = _blk(SI, SN), _blk(SI + cfg.GN, SN)

    def body(xs_ref, b_ref, c_ref, dtn_ref, acn_ref, at_ref, sp_ref, sl_ref, y_ref, st_ref, st):
        @pl.when(pl.program_id(1) == 0)
        def _():
            st[...] = jnp.zeros_like(st)

        acn = acn_ref[...]
        dts = _spread_heads(dtn_ref[...], sp_ref[0])
        a_p = _spread_heads(acn, sp_ref[0])
        acs = _spread_heads(acn, sl_ref[0])
        s0 = st[...]
        st_ref[0] = s0.reshape(HPG, P, SN)
        B = b_ref[...].astype(BF16)
        C = c_ref[...].astype(BF16)
        G = _dot(C, B, NT)
        causal = lax.broadcasted_iota(jnp.int32, (L, L), 0) >= lax.broadcasted_iota(jnp.int32, (L, L), 1)
        xdt = xs_ref[...] * dts
        xdtb = xdt.astype(BF16)
        ws = jnp.exp(a_p[L - 1:L, :] - a_p)
        yo = jnp.exp(a_p) * _dot(C, s0.astype(BF16), NT)
        yd = []
        for jp in range(HPG // 2):
            x_pair = xdtb[:, jp * LANES:(jp + 1) * LANES]
            halves = []
            for j in (2 * jp, 2 * jp + 1):
                dm = jnp.where(causal, jnp.exp(acs[:, j * LANES:(j + 1) * LANES] - at_ref[j:j + 1, :]), 0.0)
                halves.append(_dot((G * dm).astype(BF16), x_pair, NN))
            yd.append(_pair_select(halves, P))
        y_ref[...] = jnp.concatenate(yd, axis=1) + yo
        st[...] = _head_rows(jnp.exp(acs[L - 1:L, :]), HPG, P) * s0 + _dot((xdt * ws).astype(BF16), B, TN)

    y, states = pl.pallas_call(
        body, name=name, grid=(SG, nc),
        in_specs=[pl.BlockSpec((L, GW), lambda g, c: (c, g)),
                  pl.BlockSpec((L, SN), lambda g, c: (c, bcol + g)),
                  pl.BlockSpec((L, SN), lambda g, c: (c, ccol + g)),
                  pl.BlockSpec((L, LANES), lambda g, c: (c, 0)),
                  pl.BlockSpec((L, LANES), lambda g, c: (c, 0)),
                  pl.BlockSpec((HPG, L), lambda g, c: (g, c)),
                  pl.BlockSpec((1, LANES, GW), lambda g, c: (g, 0, 0)),
                  pl.BlockSpec((1, LANES, HPG * LANES), lambda g, c: (g, 0, 0))],
        out_specs=[pl.BlockSpec((L, GW), lambda g, c: (c, g)),
                   pl.BlockSpec((1, HPG, P, SN), lambda g, c: (c, g, 0, 0))],
        out_shape=[jax.ShapeDtypeStruct((S, SI), F32), jax.ShapeDtypeStruct((nc, cfg.SH, P, SN), F32)],
        scratch_shapes=[pltpu.VMEM((GW, SN), F32)],
        compiler_params=_cp(("parallel", "arbitrary")),
    )(xc, xc, xc, dt, acum, act, sel_p, sel_l)
    return y, states


def ssd_scan_bwd(xc, dt, acum, act, sel_p, sel_l, states, y, dy, dvec, cfg, name, side):
    S, L, P, SN, HPG, SG, SI = cfg.S, cfg.L, cfg.P, cfg.SN, cfg.HPG, cfg.SG, cfg.SI
    nc = S // L
    GW = HPG * P
    bcol, ccol = _blk(SI, SN), _blk(SI + cfg.GN, SN)

    def rc(c):
        return nc - 1 - c

    def body(xs_ref, b_ref, c_ref, dtn_ref, acn_ref, at_ref, sp_ref, sl_ref, st_ref, y_ref, dy_ref, dk_ref,
             dxs_ref, db_ref, dc_ref, dac_ref, dxsum_ref, dst):
        @pl.when(pl.program_id(1) == 0)
        def _():
            dst[...] = jnp.zeros_like(dst)

        sel = sp_ref[0]
        acn = acn_ref[...]
        dts = _spread_heads(dtn_ref[...], sel)
        a_p = _spread_heads(acn, sel)
        acs = _spread_heads(acn, sl_ref[0])
        B = b_ref[...].astype(BF16)
        C = c_ref[...].astype(BF16)
        G = _dot(C, B, NT)
        causal = lax.broadcasted_iota(jnp.int32, (L, L), 0) >= lax.broadcasted_iota(jnp.int32, (L, L), 1)
        low = lax.broadcasted_iota(jnp.int32, (L, LANES), 1) < P
        xs = xs_ref[...]
        dY = dy_ref[...].astype(F32)
        xdt = xs * dts
        xdtb = xdt.astype(BF16)
        dYb = dY.astype(BF16)
        s0 = st_ref[0].reshape(GW, SN)
        s0b = s0.astype(BF16)
        ds1 = dst[...]
        ds1b = ds1.astype(BF16)
        ws = jnp.exp(a_p[L - 1:L, :] - a_p)
        dR = (jnp.exp(a_p) * dY).astype(BF16)
        dX2 = ws * _dot(B, ds1b, NT)
        dgsum = jnp.zeros((L, L), F32)
        dX1, yd = [], []
        for jp in range(HPG // 2):
            lanes = slice(jp * LANES, (jp + 1) * LANES)
            x_pair, dy_pair = xdtb[:, lanes], dYb[:, lanes]
            h1, h2 = [], []
            for h, j in enumerate((2 * jp, 2 * jp + 1)):
                dm = jnp.where(causal, jnp.exp(acs[:, j * LANES:(j + 1) * LANES] - at_ref[j:j + 1, :]), 0.0)
                mine = low if h == 0 else jnp.logical_not(low)
                dgsum = dgsum + _dot(jnp.where(mine, dy_pair, jnp.zeros_like(dy_pair)), x_pair, NT) * dm
                Mb = (G * dm).astype(BF16)
                h1.append(_dot(Mb, dy_pair, TN))
                h2.append(_dot(Mb, x_pair, NN))
            dX1.append(_pair_select(h1, P))
            yd.append(_pair_select(h2, P))
        dX1 = jnp.concatenate(dX1, axis=1)
        dX = dX1 + dX2
        pair = (dYb.astype(F32) - dY) * jnp.concatenate(yd, axis=1) - xdtb.astype(F32) * dX1
        through = _segment_sums(xdt * dX2, sel)
        u = ds1 * s0
        u_hi = u.astype(BF16)
        ones = jnp.ones((16, SN), BF16)
        u_rows = _dot(ones, u_hi, NT) + _dot(ones, (u - u_hi.astype(F32)).astype(BF16), NT)
        at_end = jnp.exp(acn[L - 1:L, :]) * _segment_sums(u_rows, sel)[0:1, :] + jnp.sum(through, axis=0, keepdims=True)
        is_last = lax.broadcasted_iota(jnp.int32, (L, LANES), 0) == L - 1
        dac_ref[0] = _segment_sums(dY * y_ref[...] + pair, sel) - through + jnp.where(is_last, at_end, 0.0)
        dxsum_ref[0] = _segment_sums(dX * xs, sel)
        dxs_ref[...] = dX * dts + dk_ref[...] * dY
        dst[...] = _head_rows(jnp.exp(acs[L - 1:L, :]), HPG, P) * ds1 + _dot(dR, C, TN)
        dgb = dgsum.astype(BF16)
        dc_ref[...] = _dot(dR, s0b, NN) + _dot(dgb, B, NN)
        db_ref[...] = _dot((xdt * ws).astype(BF16), ds1b, NN) + _dot(dgb, C, TN)

    wide = pl.BlockSpec((L, GW), lambda g, c: (rc(c), g))
    colspec = pl.BlockSpec((1, L, LANES), lambda g, c: (g, rc(c), 0))
    whole = pl.BlockSpec(memory_space=pl.ANY)
    arrs, gathers = side
    grid = (SG, nc)
    res = pl.pallas_call(
        with_exchange(body, 12, 5, gathers, grid), name=name, grid=grid,
        in_specs=[wide,
                  pl.BlockSpec((L, SN), lambda g, c: (rc(c), bcol + g)),
                  pl.BlockSpec((L, SN), lambda g, c: (rc(c), ccol + g)),
                  pl.BlockSpec((L, LANES), lambda g, c: (rc(c), 0)),
                  pl.BlockSpec((L, LANES), lambda g, c: (rc(c), 0)),
                  pl.BlockSpec((HPG, L), lambda g, c: (g, rc(c))),
                  pl.BlockSpec((1, LANES, GW), lambda g, c: (g, 0, 0)),
                  pl.BlockSpec((1, LANES, HPG * LANES), lambda g, c: (g, 0, 0)),
                  pl.BlockSpec((1, HPG, P, SN), lambda g, c: (rc(c), g, 0, 0)),
                  wide, wide,
                  pl.BlockSpec((1, GW), lambda g, c: (0, g))] + [whole] * len(arrs),
        out_specs=[wide,
                   pl.BlockSpec((L, SN), lambda g, c: (rc(c), g)),
                   pl.BlockSpec((L, SN), lambda g, c: (rc(c), g)),
                   colspec, colspec] + [whole] * len(arrs),
        out_shape=[jax.ShapeDtypeStruct((S, SI), F32), jax.ShapeDtypeStruct((S, cfg.GN), F32),
                   jax.ShapeDtypeStruct((S, cfg.GN), F32),
                   jax.ShapeDtypeStruct((SG, S, LANES), F32), jax.ShapeDtypeStruct((SG, S, LANES), F32)] +
        _exchange_shapes(arrs, gathers),
        scratch_shapes=[pltpu.VMEM((GW, SN), F32)] + _exchange_sems(len(arrs)),
        compiler_params=_cp(("arbitrary", "arbitrary")),
    )(xc, xc, xc, dt, acum, act, sel_p, sel_l, states, y, dy, dvec, *arrs)
    return res[:5], res[5:]


def dt_bwd(dac, dxsum, dt_raw, dt, dt_bias, a_log, cfg, name):
    S, L, SG = cfg.S, cfg.L, cfg.SG

    def body(da_ref, dx_ref, x_ref, dt_ref, b_ref, al_ref, o_ref, gb_ref, ga_ref):
        a = -jnp.exp(al_ref[...])
        dtv = dt_ref[...]
        dxs = jnp.sum(dx_ref[...], axis=0)
        upper = jnp.where(lax.broadcasted_iota(jnp.int32, (L, L), 1) >= lax.broadcasted_iota(jnp.int32, (L, L), 0),
                          1.0, 0.0).astype(F32)
        dda = lax.dot_general(upper, jnp.sum(da_ref[...], axis=0), (NN, ((), ())), precision=lax.Precision.HIGHEST,
                              preferred_element_type=F32)
        draw = (dxs + dda * a) * _sigmoid(x_ref[...] + b_ref[...])
        o_ref[...] = draw.astype(BF16)
        gb = jnp.sum(draw, axis=0, keepdims=True)
        ga = jnp.sum(dda * dtv, axis=0, keepdims=True) * a

        @pl.when(pl.program_id(0) == 0)
        def _():
            gb_ref[...] = gb
            ga_ref[...] = ga

        @pl.when(pl.program_id(0) > 0)
        def _():
            gb_ref[...] += gb
            ga_ref[...] += ga

    row = pl.BlockSpec((L, LANES), lambda i: (i, 0))
    vec = pl.BlockSpec((1, LANES), lambda i: (0, 0))
    return pl.pallas_call(
        body, name=name, grid=(S // L,),
        in_specs=[pl.BlockSpec((SG, L, LANES), lambda i: (0, i, 0))] * 2 + [row, row, vec, vec],
        out_specs=[row, vec, vec],
        out_shape=[jax.ShapeDtypeStruct((S, LANES), BF16), jax.ShapeDtypeStruct((1, LANES), F32),
                   jax.ShapeDtypeStruct((1, LANES), F32)],
        compiler_params=_cp(("arbitrary",)),
    )(dac, dxsum, dt_raw, dt, dt_bias, a_log)


def gated_norm_fwd(y, xc, proj, dvec, nw, cfg, name, tm=128):
    S, SI = cfg.S, cfg.SI

    def body(y_ref, xs_ref, z_ref, d_ref, w_ref, o_ref):
        z = z_ref[...]
        yg = (y_ref[...] + d_ref[...] * xs_ref[...]) * (z * _sigmoid(z))
        r = lax.rsqrt(jnp.mean(yg * yg, axis=-1, keepdims=True) + RMS_EPS)
        o_ref[...] = ((yg * r) * w_ref[...]).astype(BF16)

    row = pl.BlockSpec((tm, SI), lambda i: (i, 0))
    vec = pl.BlockSpec((1, SI), lambda i: (0, 0))
    return pl.pallas_call(
        body, name=name, grid=(S // tm,),
        in_specs=[row, row, pl.BlockSpec((tm, SI), lambda i: (i, _blk(cfg.zs0, SI))), vec, vec],
        out_specs=row, out_shape=jax.ShapeDtypeStruct((S, SI), BF16),
        compiler_params=_cp(("parallel",)),
    )(y, xc, proj, dvec, nw)


def gated_norm_bwd(dyn, y, xc, proj, dvec, nw, cfg, name, tm=128):
    S, SI = cfg.S, cfg.SI

    def body(dn_ref, y_ref, xs_ref, z_ref, d_ref, w_ref, dy_ref, dz_ref, gw_ref, gd_ref):
        z = z_ref[...]
        s = _sigmoid(z)
        sz = z * s
        xs = xs_ref[...]
        yf = y_ref[...] + d_ref[...] * xs
        yg = yf * sz
        r = lax.rsqrt(jnp.mean(yg * yg, axis=-1, keepdims=True) + RMS_EPS)
        dn = dn_ref[...].astype(F32)
        g = dn * w_ref[...]
        dyg = r * g - yg * (r * r * r) * jnp.mean(g * yg, axis=-1, keepdims=True)
        dy = dyg * sz
        dy_ref[...] = dy.astype(BF16)
        dz_ref[...] = (dyg * yf * (s * (1.0 + z * (1.0 - s)))).astype(BF16)
        gw = jnp.sum(dn * (yg * r), axis=0, keepdims=True)
        gd = jnp.sum(dy * xs, axis=0, keepdims=True)

        @pl.when(pl.program_id(0) == 0)
        def _():
            gw_ref[...] = gw
            gd_ref[...] = gd

        @pl.when(pl.program_id(0) > 0)
        def _():
            gw_ref[...] += gw
            gd_ref[...] += gd

    row = pl.BlockSpec((tm, SI), lambda i: (i, 0))
    vec = pl.BlockSpec((1, SI), lambda i: (0, 0))
    return pl.pallas_call(
        body, name=name, grid=(S // tm,),
        in_specs=[row, row, row, pl.BlockSpec((tm, SI), lambda i: (i, _blk(cfg.zs0, SI))), vec, vec],
        out_specs=[row, row, vec, vec],
        out_shape=[jax.ShapeDtypeStruct((S, SI), BF16), jax.ShapeDtypeStruct((S, SI), BF16),
                   jax.ShapeDtypeStruct((1, SI), F32), jax.ShapeDtypeStruct((1, SI), F32)],
        compiler_params=_cp(("arbitrary",)),
    )(dyn, y, xc, proj, dvec, nw)


def _shard_columns(cfg, main, dt):
    dt0 = 4 * cfg.AW + cfg.SI + cfg.CD
    ws = cfg.N_IN // N_DEV
    out = []
    for k in range(N_DEV):
        lo, hi, parts = k * ws, (k + 1) * ws, []
        if lo < dt0:
            parts.append((main, lo, min(hi, dt0)))
        if lo < dt0 + cfg.SH and hi > dt0:
            parts.append((dt, max(lo, dt0) - dt0, min(hi, dt0 + cfg.SH) - dt0))
        if hi > dt0 + cfg.SH:
            parts.append((main, max(lo, dt0 + cfg.SH) - cfg.SH, hi - cfg.SH))
        out.append(parts)
    return out


def local_step(cfg, x, tgt, norm_w, conv_w, conv_b, dt_bias, a_log, d_skip, ssm_norm_w, final_norm_w,
               w_main, w_dt, shards, dt0):
    S, D = cfg.S, cfg.D
    slopes = _slope_table(cfg)
    dt_bias_p = _pad_lanes(dt_bias)
    a_log_p = _pad_lanes(a_log)
    dvec = _spread(d_skip, cfg.P)

    hn = rmsnorm_fwd(x, norm_w, "rmsnorm_fwd")
    proj, gathered = matmul(hn, w_main, 'nn', 1024, 1024, 2048, F32, "in_proj", side=(shards, [True] * 3))
    w_attn, w_ssm, w_out = gathered[0].reshape(cfg.AW, D), gathered[1].reshape(cfg.SI, D), gathered[2].reshape(D, D)
    dt_raw = matmul(hn, w_dt, 'nn', 512, 128, 2048, F32, "in_proj_dt")
    o_a, o_mix, ltot = attn_fused_fwd(proj, slopes, cfg, "attn_fwd")
    xc = conv_fwd(proj, conv_w, conv_b, cfg, "conv_fwd")
    dt, acum = ssd_prep(dt_raw, dt_bias_p, a_log_p, cfg, "ssd_prep")
    act = acum[:, :cfg.SH].T
    sel_p, sel_l = _head_selectors(cfg)
    y, states = ssd_scan_fwd(xc, dt, acum, act, sel_p, sel_l, cfg, "ssd_scan_fwd")
    y_n = gated_norm_fwd(y, xc, proj, dvec, ssm_norm_w, cfg, "gated_norm_fwd")
    a_out, s_out, merged = branch_merge(o_a, w_attn, y_n, w_ssm, proj, cfg, "branch_merge")
    dout, loss_p, g_final_w = out_proj_final(merged, w_out, x, final_norm_w.reshape(1, D), tgt, "out_proj_final")

    g_w_out = matmul(merged, dout, 'tn', 1024, 1024, 2048, BF16, "g_w_out")
    da_out, ds_out, dga, dgs = merge_bwd(dout, w_out, a_out, s_out, proj, cfg, "merge_bwd")
    g_w_attn = matmul(o_a, da_out, 'tn', 1024, 1024, 2048, BF16, "g_w_attn")
    g_w_ssm = matmul(y_n, ds_out, 'tn', 1024, 1024, 2048, BF16, "g_w_ssm")
    do_a = matmul(da_out, w_attn, 'nt', 512, 1024, 2048, BF16, "d_o_a")
    dyn = matmul(ds_out, w_ssm, 'nt', 512, 1024, 2048, BF16, "d_y_n")
    dy, dz_s, g_ssm_norm, g_dvec = gated_norm_bwd(dyn, y, xc, proj, dvec, ssm_norm_w, cfg, "gated_norm_bwd")
    sends = [g.reshape((N_DEV, g.shape[0] // N_DEV, D)) for g in (g_w_attn, g_w_ssm, g_w_out)]
    (dxs, dB, dC, dac_g, dxsum_g), (r_attn, r_ssm, r_out) = ssd_scan_bwd(
        xc, dt, acum, act, sel_p, sel_l, states, y, dy, dvec, cfg, "ssd_scan_bwd", side=(sends, [False] * 3))
    ddt_raw, g_dt_bias, g_a_log = dt_bwd(dac_g, dxsum_g, dt_raw, dt, dt_bias_p, a_log_p, cfg, "dt_bwd")
    dxbc, g_cw, g_cb = [], [], []
    for nm, piece, c_off in (("xs", dxs, 0), ("b", dB, cfg.SI), ("c", dC, cfg.SI + cfg.GN)):
        dpre, gw, gb = conv_bwd_a(proj, piece, conv_w, conv_b, cfg, "conv_bwd_a_" + nm, c_off)
        dxbc.append(conv_bwd_b(dpre, conv_w, cfg, "conv_bwd_b_" + nm, c_off))
        g_cw.append(gw)
        g_cb.append(gb)
    g_conv_w, g_conv_b = jnp.concatenate(g_cw, axis=1), jnp.concatenate(g_cb, axis=1)
    dq, dk, dv, dz_a = attn_fused_bwd(proj, do_a, o_mix, ltot, slopes, cfg, "attn_bwd")
    dproj = jnp.concatenate([dq, dk, dv, dz_a, dz_s] + dxbc + [dga, dgs], axis=1)
    g_w_main = matmul(hn, dproj, 'tn', 1024, 1024, 2048, BF16, "g_w_main")
    g_w_dt = matmul(hn, ddt_raw, 'tn', 1024, 128, 2048, BF16, "g_w_dt")
    send_in = jnp.stack([jnp.concatenate([g[:, lo:hi] for g, lo, hi in parts], axis=1)
                         for parts in _shard_columns(cfg, g_w_main, g_w_dt)])
    dhn_a, (r_in,) = matmul(dproj, w_main, 'nt', 1024, 1024, 2048, F32, "d_hn", side=([send_in], [False]))
    dhn_b = matmul(ddt_raw, w_dt, 'nt', 512, 1024, 128, F32, "d_hn_dt")
    grad_x, g_norm_w = rmsnorm_bwd(dhn_a, dhn_b, x, norm_w, dout, "rmsnorm_bwd")

    g_d_skip = jnp.sum(g_dvec.reshape(cfg.SH, cfg.P), axis=1).reshape(1, cfg.SH)
    small = dict(norm_w=g_norm_w, conv_b=g_conv_b, dt_bias=g_dt_bias[:, :cfg.SH], a_log=g_a_log[:, :cfg.SH],
                 d_skip=g_d_skip, ssm_norm_w=g_ssm_norm, final_norm_w=g_final_w, conv_w=g_conv_w[:cfg.KC])
    return loss_p, grad_x, small, dict(w_in=r_in, w_attn=r_attn, w_ssm=r_ssm, w_out=r_out)


def _mesh_pos():
    return lax.axis_index("x"), lax.axis_index("y"), lax.axis_index("c")


def _flat(pos):
    return 4 * pos[0] + 2 * pos[1] + pos[2]


def _exchange_shapes(arrs, gathers):
    return [jax.ShapeDtypeStruct(((N_DEV,) + a.shape) if g else a.shape, a.dtype) for a, g in zip(arrs, gathers)]


def _exchange_sems(n):
    return [pltpu.SemaphoreType.DMA((n * (N_DEV - 1),)), pltpu.SemaphoreType.DMA((n * (N_DEV - 1),)),
            pltpu.SemaphoreType.DMA((n,))]


def _exchange_copies(ins, outs, gathers, send_sems, recv_sems, loc_sems):
    pos = _mesh_pos()
    me = _flat(pos)
    starts, waits = [], []
    for a in range(len(ins)):
        mine = ins[a] if gathers[a] else ins[a].at[me]
        loc = pltpu.make_async_copy(mine, outs[a].at[me], loc_sems.at[a])
        starts.append(loc)
        waits.append(loc)
        for k in range(1, N_DEV):
            flip = ((k >> 2) & 1, (k >> 1) & 1, k & 1)
            peer = tuple(1 - p if f else p for p, f in zip(pos, flip))
            pk = _flat(peer)
            src = ins[a] if gathers[a] else ins[a].at[pk]
            sems = dict(send_sem=send_sems.at[a * (N_DEV - 1) + k - 1], recv_sem=recv_sems.at[a * (N_DEV - 1) + k - 1],
                        device_id=peer, device_id_type=pl.DeviceIdType.MESH)
            starts.append(pltpu.make_async_remote_copy(src_ref=src, dst_ref=outs[a].at[me], **sems))
            waits.append(pltpu.make_async_remote_copy(src_ref=src, dst_ref=outs[a].at[pk], **sems))
    return starts, waits


def exchange(arrs, gathers, name):
    n = len(arrs)

    def body(*refs):
        starts, waits = _exchange_copies(refs[:n], refs[n:2 * n], gathers, *refs[2 * n:])
        for cp in starts:
            cp.start()
        for cp in waits:
            cp.wait()

    hbm = pl.BlockSpec(memory_space=pltpu.HBM)
    return pl.pallas_call(
        body, name=name, in_specs=[hbm] * n, out_specs=[hbm] * n, out_shape=_exchange_shapes(arrs, gathers),
        scratch_shapes=_exchange_sems(n),
    )(*arrs)


def gather_two_level(arrs, chunks, name):
    n = len(arrs)

    def body(*refs):
        ins, outs = refs[:n], refs[n:2 * n]
        send_sems, recv_sems, loc_sems = refs[2 * n:]
        x, y, c = _mesh_pos()
        me, sib = (x, y, c), (x, y, 1 - c)
        chips = [(1 - x, y), (x, 1 - y), (1 - x, 1 - y)]
        plan, base = [], 0
        for a in range(n):
            step = arrs[a].shape[0] // chunks[a]
            for q in range(chunks[a]):
                plan.append((a, pl.ds(q * step, step), base))
                base += N_DEV - 1

        def copy(a, rows, sem, block, to, own=False):
            dst = outs[a].at[_flat(block), rows]
            return pltpu.make_async_remote_copy(
                src_ref=ins[a].at[rows] if own else dst, dst_ref=dst, send_sem=send_sems.at[sem],
                recv_sem=recv_sems.at[sem], device_id=to, device_id_type=pl.DeviceIdType.MESH)

        local = [pltpu.make_async_copy(ins[a], outs[a].at[_flat(me)], loc_sems.at[a]) for a in range(n)]
        for cp in local:
            cp.start()
        sent = []
        for a, rows, s in plan:
            sent.append(copy(a, rows, s, me, sib, own=True))
            sent += [copy(a, rows, s + 1 + j, me, (*chip, c), own=True) for j, chip in enumerate(chips)]
        for cp in sent:
            cp.start()
        for a, rows, s in plan:
            for j, chip in enumerate(chips):
                copy(a, rows, s + 1 + j, (*chip, c), me).wait_recv()
                passed = copy(a, rows, s + 4 + j, (*chip, c), sib)
                passed.start()
                sent.append(passed)
        for a, rows, s in plan:
            copy(a, rows, s, sib, me).wait_recv()
            for j, chip in enumerate(chips):
                copy(a, rows, s + 4 + j, (*chip, 1 - c), me).wait_recv()
        for cp in sent:
            cp.wait_send()
        for cp in local:
            cp.wait()

    hbm = pl.BlockSpec(memory_space=pltpu.HBM)
    nsem = (N_DEV - 1) * sum(chunks)
    return pl.pallas_call(
        body, name=name, in_specs=[hbm] * n, out_specs=[hbm] * n, out_shape=_exchange_shapes(arrs, [True] * n),
        scratch_shapes=[pltpu.SemaphoreType.DMA((nsem,)), pltpu.SemaphoreType.DMA((nsem,)),
                        pltpu.SemaphoreType.DMA((n,))],
    )(*arrs)


def with_exchange(body, n_in, n_out, gathers, grid):
    n = len(gathers)

    def wrapped(*refs):
        ins, sends = refs[:n_in], refs[n_in:n_in + n]
        outs, recvs = refs[n_in + n:n_in + n + n_out], refs[n_in + 2 * n + n_out - n:n_in + 2 * n + n_out]
        scratch, sems = refs[n_in + 2 * n + n_out:-3], refs[-3:]
        ids = [pl.program_id(d) for d in range(len(grid))]
        first = functools.reduce(jnp.logical_and, [i == 0 for i in ids])
        last = functools.reduce(jnp.logical_and, [i == g - 1 for i, g in zip(ids, grid)])

        @pl.when(first)
        def _():
            for cp in _exchange_copies(sends, recvs, gathers, *sems)[0]:
                cp.start()

        body(*ins, *outs, *scratch)

        @pl.when(last)
        def _():
            for cp in _exchange_copies(sends, recvs, gathers, *sems)[1]:
                cp.wait()

    return wrapped


def adamw(g_src, w, m, v, summed, name, tr=64):
    R, C = w.shape
    tr = min(tr, R)
    assert R % tr == 0

    def body(g_ref, w_ref, m_ref, v_ref, g_out, d_out, m_out, v_out):
        if summed:
            g = g_ref[0].astype(F32)
            for j in range(1, N_DEV):
                g = g + g_ref[j].astype(F32)
        else:
            g = g_ref[...]
        mn = ADAM_B1 * m_ref[...] + (1.0 - ADAM_B1) * g
        vn = ADAM_B2 * v_ref[...] + (1.0 - ADAM_B2) * (g * g)
        m_hat = mn / (1.0 - ADAM_B1 ** ADAM_STEP)
        v_hat = vn / (1.0 - ADAM_B2 ** ADAM_STEP)
        g_out[...] = g
        d_out[...] = -ADAM_LR * (m_hat / (jnp.sqrt(v_hat) + ADAM_EPS) + ADAM_WD * w_ref[...])
        m_out[...] = mn
        v_out[...] = vn

    row = pl.BlockSpec((tr, C), lambda i: (i, 0))
    gspec = pl.BlockSpec((N_DEV, tr, C), lambda i: (0, i, 0)) if summed else row
    sh = jax.ShapeDtypeStruct((R, C), F32)
    return pl.pallas_call(
        body, name=name, grid=(R // tr,), in_specs=[gspec, row, row, row], out_specs=[row] * 4, out_shape=[sh] * 4,
        compiler_params=_cp(("parallel",)),
    )(g_src, w, m, v)


SMALL = ('norm_w', 'conv_b', 'dt_bias', 'a_log', 'd_skip', 'ssm_norm_w', 'final_norm_w')


def _rows(n):
    return -(-n // (8 * LANES)) * 8


def _pack(vals):
    parts = []
    for a in vals:
        f = a.reshape(-1)
        parts.append(jnp.pad(f, (0, _rows(f.size) * LANES - f.size)).reshape(-1, LANES))
    return jnp.concatenate(parts, axis=0)


def _unpack(packed, shapes):
    out, r = [], 0
    for s in shapes:
        n = math.prod(s)
        out.append(packed[r:r + _rows(n)].reshape(-1)[:n].reshape(s))
        r += _rows(n)
    return out


def kernel(x, norm_w, w_in, conv_w, conv_b, dt_bias, a_log, d_skip, ssm_norm_w, w_attn_branch, w_ssm_branch, w_out, final_norm_w, loss_target, m_norm_w, m_w_in, m_conv_w, m_conv_b, m_dt_bias, m_a_log, m_d_skip, m_ssm_norm_w, m_w_attn_branch, m_w_ssm_branch, m_w_out, m_final_norm_w, v_norm_w, v_w_in, v_conv_w, v_conv_b, v_dt_bias, v_a_log, v_d_skip, v_ssm_norm_w, v_w_attn_branch, v_w_ssm_branch, v_w_out, v_final_norm_w):
    cfg = CFG
    D, SH = cfg.D, cfg.SH
    me = _flat(_mesh_pos())
    dt0 = 4 * cfg.AW + cfg.SI + cfg.CD
    ws = w_in.shape[-1]

    g_in, g_cw = gather_two_level([w_in[0].astype(BF16), conv_w[0]], [4, 1], "gather_w_in")
    main_cols, dt_cols = [], []
    for k, parts in enumerate(_shard_columns(cfg, "main", "dt")):
        at = 0
        for which, lo, hi in parts:
            (main_cols if which == "main" else dt_cols).append(g_in[k][:, at:at + hi - lo])
            at += hi - lo
    w_main = jnp.concatenate(main_cols, axis=1)
    w_dt = _pad_lanes(jnp.concatenate(dt_cols, axis=1))
    conv_full = g_cw.transpose(1, 0, 2).reshape(cfg.KC, cfg.CD)
    shards = [w_attn_branch[0].astype(BF16), w_ssm_branch[0].astype(BF16), w_out[0].astype(BF16)]

    loss_p, grad_x, small, recv = local_step(
        cfg, x[0], loss_target[0], norm_w, conv_full, conv_b, dt_bias, a_log, d_skip,
        ssm_norm_w, final_norm_w, w_main, w_dt, shards, dt0)

    upd = {}
    upd['w_in'] = adamw(recv['w_in'], w_in[0], m_w_in[0], v_w_in[0], True, "adamw_w_in")
    upd['w_attn_branch'] = adamw(recv['w_attn'], w_attn_branch[0], m_w_attn_branch[0], v_w_attn_branch[0], True,
                                 "adamw_w_attn")
    upd['w_ssm_branch'] = adamw(recv['w_ssm'], w_ssm_branch[0], m_w_ssm_branch[0], v_w_ssm_branch[0], True,
                                "adamw_w_ssm")
    upd['w_out'] = adamw(recv['w_out'], w_out[0], m_w_out[0], v_w_out[0], True, "adamw_w_out")

    extra = [jnp.zeros((cfg.KC, cfg.CD), F32), jnp.zeros((1, 1), F32)]
    shapes = [small[n].shape for n in SMALL] + [e.shape for e in extra]
    part = _pack([small[n] for n in SMALL] + [small['conv_w'], loss_p[:, :1]])
    gathered, = exchange([part], [True], "gather_small")
    given = dict(norm_w=(norm_w, m_norm_w, v_norm_w), conv_b=(conv_b, m_conv_b, v_conv_b),
                 dt_bias=(dt_bias, m_dt_bias, v_dt_bias), a_log=(a_log, m_a_log, v_a_log),
                 d_skip=(d_skip, m_d_skip, v_d_skip), ssm_norm_w=(ssm_norm_w, m_ssm_norm_w, v_ssm_norm_w),
                 final_norm_w=(final_norm_w, m_final_norm_w, v_final_norm_w))
    packed = [_pack([given[n][t] for n in SMALL] + extra) for t in range(3)]
    outs = adamw(gathered, *packed, True, "adamw_small", tr=part.shape[0])
    unpacked = [_unpack(o, shapes) for o in outs]
    for i, n in enumerate(SMALL):
        upd[n] = [u[i].reshape(given[n][0].shape) for u in unpacked]
    loss = unpacked[0][-1].reshape(())
    cw = conv_w.shape[-1]
    g_cw_mine = lax.dynamic_slice_in_dim(unpacked[0][-2], me * cw, cw, axis=1)
    upd['conv_w'] = adamw(g_cw_mine.reshape(-1, LANES), conv_w.reshape(-1, LANES), m_conv_w.reshape(-1, LANES),
                          v_conv_w.reshape(-1, LANES), False, "adamw_conv_w")

    order = ['norm_w', 'w_in', 'conv_w', 'conv_b', 'dt_bias', 'a_log', 'd_skip', 'ssm_norm_w', 'w_attn_branch',
             'w_ssm_branch', 'w_out', 'final_norm_w']
    like = dict(norm_w=norm_w, w_in=w_in, conv_w=conv_w, conv_b=conv_b, dt_bias=dt_bias, a_log=a_log, d_skip=d_skip,
                ssm_norm_w=ssm_norm_w, w_attn_branch=w_attn_branch, w_ssm_branch=w_ssm_branch, w_out=w_out,
                final_norm_w=final_norm_w)
    result = [loss, grad_x[None]]
    for t in range(4):
        result += [upd[n][t].reshape(like[n].shape) for n in order]
    return tuple(result)
```

```python
import functools
import math
from typing import NamedTuple

import jax
import jax.numpy as jnp
from jax import lax
from jax.experimental import pallas as pl
from jax.experimental.pallas import tpu as pltpu

F32 = jnp.float32
BF16 = jnp.bfloat16
RMS_EPS = 1e-6
NEG = -1e30
N_DEV = 8
LANES = 128
ATTN_BLOCK = 128
ADAM_LR, ADAM_B1, ADAM_B2, ADAM_EPS, ADAM_WD, ADAM_STEP = 0.001, 0.9, 0.999, 1e-08, 0.01, 10
VMEM_LIMIT = 56 * 1024 * 1024


class Cfg(NamedTuple):
    D: int = 2048
    S: int = 8192
    AH: int = 16
    E: int = 128
    HB: int = 4
    patterns: tuple = ((128, 1), (512, 4), (2048, 16))
    SI: int = 4096
    P: int = 64
    SG: int = 8
    SN: int = 128
    KC: int = 4
    L: int = 128

    @property
    def AW(self): return self.AH * self.E
    @property
    def SH(self): return self.SI // self.P
    @property
    def HPG(self): return self.SH // self.SG
    @property
    def GN(self): return self.SG * self.SN
    @property
    def CD(self): return self.SI + 2 * self.GN
    @property
    def k0(self): return self.AW
    @property
    def v0(self): return 2 * self.AW
    @property
    def za0(self): return 3 * self.AW
    @property
    def zs0(self): return 4 * self.AW
    @property
    def xbc0(self): return 4 * self.AW + self.SI
    @property
    def ga0(self): return self.xbc0 + self.CD
    @property
    def gs0(self): return self.ga0 + self.D
    @property
    def NP(self): return self.gs0 + self.D
    @property
    def N_IN(self): return self.NP + self.SH


CFG = Cfg()


def _cp(sem=None, vmem=VMEM_LIMIT):
    return pltpu.CompilerParams(dimension_semantics=sem, vmem_limit_bytes=vmem)


def _sigmoid(z):
    return 1.0 / (1.0 + jnp.exp(-z))


def _dot(a, b, dims):
    return lax.dot_general(a, b, (dims, ((), ())), preferred_element_type=F32)


NN = ((1,), (0,))
NT = ((1,), (1,))
TN = ((0,), (0,))


def _blk(off, width):
    assert off % width == 0, (off, width)
    return off // width


def matmul(a, b, mode, tm, tn, tk, out_dtype, name, side=None):
    if mode == 'nn':
        (M, K), (_, N) = a.shape, b.shape
    elif mode == 'nt':
        (M, K), (N, _) = a.shape, b.shape
    else:
        (K, M), (_, N) = a.shape, b.shape
    tm, tn, tk = min(tm, M), min(tn, N), min(tk, K)
    assert M % tm == 0 and N % tn == 0 and K % tk == 0, (M, N, K, tm, tn, tk)
    nk = K // tk
    dims = {'nn': NN, 'nt': NT, 'tn': TN}[mode]

    def body(a_ref, b_ref, o_ref, *acc):
        part = _dot(a_ref[...].astype(BF16), b_ref[...].astype(BF16), dims)
        if nk == 1:
            o_ref[...] = part.astype(out_dtype)
        else:
            acc_ref, = acc
            k = pl.program_id(2)

            @pl.when(k == 0)
            def _():
                acc_ref[...] = part

            @pl.when(k > 0)
            def _():
                acc_ref[...] += part

            @pl.when(k == nk - 1)
            def _():
                o_ref[...] = acc_ref[...].astype(out_dtype)

    if mode == 'tn':
        a_spec = pl.BlockSpec((tk, tm), lambda n, m, k: (k, m))
    else:
        a_spec = pl.BlockSpec((tm, tk), lambda n, m, k: (m, k))
    if mode == 'nt':
        b_spec = pl.BlockSpec((tn, tk), lambda n, m, k: (n, k))
    else:
        b_spec = pl.BlockSpec((tk, tn), lambda n, m, k: (k, n))
    grid = (N // tn, M // tm, nk)
    o_spec = pl.BlockSpec((tm, tn), lambda n, m, k: (m, n))
    o_shape = jax.ShapeDtypeStruct((M, N), out_dtype)
    acc = [] if nk == 1 else [pltpu.VMEM((tm, tn), F32)]
    if side is None:
        return pl.pallas_call(
            body, name=name, grid=grid, in_specs=[a_spec, b_spec], out_specs=o_spec, out_shape=o_shape,
            scratch_shapes=acc, compiler_params=_cp(("parallel", "parallel", "arbitrary")),
        )(a, b)
    arrs, gathers = side
    whole = pl.BlockSpec(memory_space=pl.ANY)
    res = pl.pallas_call(
        with_exchange(body, 2, 1, gathers, grid), name=name, grid=grid,
        in_specs=[a_spec, b_spec] + [whole] * len(arrs), out_specs=[o_spec] + [whole] * len(arrs),
        out_shape=[o_shape] + _exchange_shapes(arrs, gathers),
        scratch_shapes=acc + _exchange_sems(len(arrs)),
        compiler_params=_cp(("arbitrary", "arbitrary", "arbitrary")),
    )(a, b, *arrs)
    return res[0], res[1:]


def rmsnorm_fwd(x, w, name, tm=256):
    S, D = x.shape

    def body(x_ref, w_ref, o_ref):
        xv = x_ref[...]
        r = lax.rsqrt(jnp.mean(xv * xv, axis=-1, keepdims=True) + RMS_EPS)
        o_ref[...] = ((xv * r) * w_ref[...]).astype(BF16)

    return pl.pallas_call(
        body, name=name, grid=(S // tm,),
        in_specs=[pl.BlockSpec((tm, D), lambda i: (i, 0)), pl.BlockSpec((1, D), lambda i: (0, 0))],
        out_specs=pl.BlockSpec((tm, D), lambda i: (i, 0)),
        out_shape=jax.ShapeDtypeStruct((S, D), BF16),
        compiler_params=_cp(("parallel",)),
    )(x, w)


def rmsnorm_bwd(dh_a, dh_b, x, w, dout, name, tm=128):
    S, D = x.shape

    def body(da_ref, db_ref, x_ref, w_ref, do_ref, gx_ref, gw_ref):
        xv = x_ref[...]
        dh = da_ref[...] + db_ref[...]
        r = lax.rsqrt(jnp.mean(xv * xv, axis=-1, keepdims=True) + RMS_EPS)
        g = dh * w_ref[...]
        dx = r * g - xv * (r * r * r) * jnp.mean(g * xv, axis=-1, keepdims=True)
        gx_ref[...] = do_ref[...] + dx
        gw = jnp.sum(dh * (xv * r), axis=0, keepdims=True)

        @pl.when(pl.program_id(0) == 0)
        def _():
            gw_ref[...] = gw

        @pl.when(pl.program_id(0) > 0)
        def _():
            gw_ref[...] += gw

    row = pl.BlockSpec((tm, D), lambda i: (i, 0))
    vec = pl.BlockSpec((1, D), lambda i: (0, 0))
    return pl.pallas_call(
        body, name=name, grid=(S // tm,),
        in_specs=[row, row, row, vec, row],
        out_specs=[row, vec],
        out_shape=[jax.ShapeDtypeStruct((S, D), F32), jax.ShapeDtypeStruct((1, D), F32)],
        compiler_params=_cp(("arbitrary",)),
    )(dh_a, dh_b, x, w, dout)


def out_proj_final(merged, w_out, x, fw, tgt, name, tm=256):
    S, D = x.shape

    def body(m_ref, wo_ref, x_ref, w_ref, t_ref, do_ref, loss_ref, gw_ref):
        out = x_ref[...] + _dot(m_ref[...], wo_ref[...], NN)
        w = w_ref[...]
        r = lax.rsqrt(jnp.mean(out * out, axis=-1, keepdims=True) + RMS_EPS)
        yn = out * r
        err = yn * w - t_ref[...]
        lrow = 0.5 * jnp.mean(err * err, axis=-1, keepdims=True)
        lsum = jnp.zeros((1, LANES), F32) + jnp.sum(lrow, axis=0, keepdims=True)
        dfin = err * (1.0 / D)
        g = dfin * w
        do_ref[...] = r * g - out * (r * r * r) * jnp.mean(g * out, axis=-1, keepdims=True)
        gw = jnp.sum(dfin * yn, axis=0, keepdims=True)

        @pl.when(pl.program_id(0) == 0)
        def _():
            gw_ref[...] = gw
            loss_ref[...] = lsum

        @pl.when(pl.program_id(0) > 0)
        def _():
            gw_ref[...] += gw
            loss_ref[...] += lsum

    row = pl.BlockSpec((tm, D), lambda i: (i, 0))
    vec = pl.BlockSpec((1, D), lambda i: (0, 0))
    return pl.pallas_call(
        body, name=name, grid=(S // tm,),
        in_specs=[row, pl.BlockSpec((D, D), lambda i: (0, 0)), row, vec, row],
        out_specs=[row, pl.BlockSpec((1, LANES), lambda i: (0, 0)), vec],
        out_shape=[jax.ShapeDtypeStruct((S, D), F32), jax.ShapeDtypeStruct((1, LANES), F32),
                   jax.ShapeDtypeStruct((1, D), F32)],
        compiler_params=_cp(("arbitrary",)),
    )(merged, w_out, x, fw, tgt)


def branch_merge(o_a, w_attn, y_n, w_ssm, proj, cfg, name, tm=512, tn=512):
    S, D = cfg.S, cfg.D
    tm, tn = min(tm, S), min(tn, D)

    def body(oa_ref, wa_ref, yn_ref, ws_ref, ga_ref, gs_ref, a_ref, s_ref, m_ref):
        a = _dot(oa_ref[...], wa_ref[...], NN)
        sv = _dot(yn_ref[...], ws_ref[...], NN)
        a_ref[...] = a.astype(BF16)
        s_ref[...] = sv.astype(BF16)
        m_ref[...] = (_sigmoid(ga_ref[...]) * a + _sigmoid(gs_ref[...]) * sv).astype(BF16)

    tile = pl.BlockSpec((tm, tn), lambda n, m: (m, n))
    return pl.pallas_call(
        body, name=name, grid=(D // tn, S // tm),
        in_specs=[pl.BlockSpec((tm, cfg.AW), lambda n, m: (m, 0)), pl.BlockSpec((cfg.AW, tn), lambda n, m: (0, n)),
                  pl.BlockSpec((tm, cfg.SI), lambda n, m: (m, 0)), pl.BlockSpec((cfg.SI, tn), lambda n, m: (0, n)),
                  pl.BlockSpec((tm, tn), lambda n, m: (m, _blk(cfg.ga0, tn) + n)),
                  pl.BlockSpec((tm, tn), lambda n, m: (m, _blk(cfg.gs0, tn) + n))],
        out_specs=[tile, tile, tile],
        out_shape=[jax.ShapeDtypeStruct((S, D), BF16)] * 3,
        compiler_params=_cp(("parallel", "parallel")),
    )(o_a, w_attn, y_n, w_ssm, proj, proj)


def merge_bwd(dout, w_out, a_out, s_out, proj, cfg, name, tm=512, tn=1024):
    S, D = cfg.S, cfg.D
    tm, tn = min(tm, S), min(tn, D)

    def body(do_ref, wo_ref, a_ref, s_ref, ga_ref, gs_ref, da_ref, ds_ref, dga_ref, dgs_ref):
        dmv = _dot(do_ref[...].astype(BF16), wo_ref[...], NT)
        sa = _sigmoid(ga_ref[...])
        ss = _sigmoid(gs_ref[...])
        da_ref[...] = (dmv * sa).astype(BF16)
        ds_ref[...] = (dmv * ss).astype(BF16)
        dga_ref[...] = (dmv * a_ref[...] * (sa * (1.0 - sa))).astype(BF16)
        dgs_ref[...] = (dmv * s_ref[...] * (ss * (1.0 - ss))).astype(BF16)

    tile = pl.BlockSpec((tm, tn), lambda n, m: (m, n))
    sh = jax.ShapeDtypeStruct((S, D), BF16)
    return pl.pallas_call(
        body, name=name, grid=(D // tn, S // tm),
        in_specs=[pl.BlockSpec((tm, D), lambda n, m: (m, 0)), pl.BlockSpec((tn, D), lambda n, m: (n, 0)), tile, tile,
                  pl.BlockSpec((tm, tn), lambda n, m: (m, _blk(cfg.ga0, tn) + n)),
                  pl.BlockSpec((tm, tn), lambda n, m: (m, _blk(cfg.gs0, tn) + n))],
        out_specs=[tile] * 4, out_shape=[sh] * 4,
        compiler_params=_cp(("parallel", "parallel")),
    )(dout, w_out, a_out, s_out, proj, proj)


def _attn_rows(base, d):
    return pl.ds(base, ATTN_BLOCK) if d == 1 else pl.ds(base, ATTN_BLOCK, stride=d)


def _attn_units(cfg):
    dmax = max(d for _, d in cfg.patterns)
    units = []
    for p, (window, d) in enumerate(cfg.patterns):
        assert window // d == ATTN_BLOCK and dmax % d == 0
        nsub = dmax // d
        for b in range(nsub):
            for r in range(d):
                base = b * ATTN_BLOCK * d + r
                if b > 0:
                    units.append((p, d, base, (b - 1) * ATTN_BLOCK * d + r, False))
                else:
                    units.append((p, d, base, (nsub - 1) * ATTN_BLOCK * d + r, True))
    return units, ATTN_BLOCK * dmax


def _set_bias_tiles(bias_s, slope, cfg):
    qi = lax.broadcasted_iota(jnp.int32, (ATTN_BLOCK, ATTN_BLOCK), 0)
    ki = lax.broadcasted_iota(jnp.int32, (ATTN_BLOCK, ATTN_BLOCK), 1)
    for p, (_, d) in enumerate(cfg.patterns):
        bias_s[2 * p] = jnp.where(ki >= qi, (-slope) * ((ATTN_BLOCK + qi - ki) * d).astype(F32), NEG)
        bias_s[2 * p + 1] = jnp.where(ki <= qi, (-slope) * ((qi - ki) * d).astype(F32), NEG)


def _unit_scores(q, kcat, bias_s, p, prev_ok, scale):
    s = _dot(q, kcat, NT) * scale + jnp.concatenate([bias_s[2 * p], bias_s[2 * p + 1]], axis=1)
    if prev_ok is not None:
        cur_half = lax.broadcasted_iota(jnp.int32, s.shape, 1) >= ATTN_BLOCK
        s = jnp.where(jnp.logical_or(cur_half, prev_ok), s, NEG)
    return s


def _slope_table(cfg):
    slopes = jnp.asarray([2.0 ** (-8.0 * (h + 1) / cfg.AH) for h in range(cfg.AH)], F32)
    return jnp.broadcast_to(slopes.reshape(cfg.AH, 1, 1), (cfg.AH, 8, LANES))


def attn_fused_fwd(proj, slopes, cfg, name):
    S, E, AH = cfg.S, cfg.E, cfg.AH
    units, SB = _attn_units(cfg)
    assert S % SB == 0
    npat = len(cfg.patterns)
    scale = E ** -0.5

    def spec(off, prev):
        c0 = _blk(off, E)
        if prev:
            return pl.BlockSpec((SB, E), lambda h, i: (jnp.maximum(i - 1, 0), c0 + h))
        return pl.BlockSpec((SB, E), lambda h, i: (i, c0 + h))

    def body(q_ref, kp_ref, kc_ref, vp_ref, vc_ref, z_ref, sl_ref, oa_ref, om_ref, lt_ref, *scr):
        o_s, l_s, bias_s = scr[:npat], scr[npat:2 * npat], scr[2 * npat]
        i = pl.program_id(1)

        @pl.when(i == 0)
        def _():
            _set_bias_tiles(bias_s, sl_ref[0, 0:1, :], cfg)

        for p, d, base, pbase, from_prev in units:
            rows, prows = _attn_rows(base, d), _attn_rows(pbase, d)
            q = q_ref[rows, :].astype(BF16)
            kp = (kp_ref if from_prev else kc_ref)[prows, :].astype(BF16)
            vp = (vp_ref if from_prev else vc_ref)[prows, :].astype(BF16)
            kcat = jnp.concatenate([kp, kc_ref[rows, :].astype(BF16)], axis=0)
            vcat = jnp.concatenate([vp, vc_ref[rows, :].astype(BF16)], axis=0)
            s = _unit_scores(q, kcat, bias_s, p, (i > 0) if from_prev else None, scale)
            m = jnp.max(s, axis=1, keepdims=True)
            pr = jnp.exp(s - m)
            l = jnp.sum(pr, axis=1, keepdims=True)
            o_s[p][rows, :] = _dot(pr.astype(BF16), vcat, NN) * (1.0 / l)
            l_s[p][rows, :] = m + jnp.log(l)
        ls = [l_s[p][...] for p in range(npat)]
        m = functools.reduce(jnp.maximum, ls)
        lt = m + jnp.log(sum(jnp.exp(l_ - m) for l_ in ls))
        lt_ref[...] = lt
        mix = sum(jnp.exp(ls[p] - lt) * o_s[p][...] for p in range(npat))
        om_ref[...] = mix
        z = z_ref[...]
        oa_ref[...] = (mix * (z * _sigmoid(z))).astype(BF16)

    out = pl.BlockSpec((SB, E), lambda h, i: (i, h))
    return pl.pallas_call(
        body, name=name, grid=(AH, S // SB),
        in_specs=[spec(0, False), spec(cfg.k0, True), spec(cfg.k0, False), spec(cfg.v0, True), spec(cfg.v0, False),
                  spec(cfg.za0, False), pl.BlockSpec((1, 8, LANES), lambda h, i: (h, 0, 0))],
        out_specs=[out, out, pl.BlockSpec((SB, 1), lambda h, i: (h * (S // SB) + i, 0))],
        out_shape=[jax.ShapeDtypeStruct((S, cfg.AW), BF16), jax.ShapeDtypeStruct((S, cfg.AW), F32),
                   jax.ShapeDtypeStruct((AH * S, 1), F32)],
        scratch_shapes=[pltpu.VMEM((SB, E), F32)] * npat + [pltpu.VMEM((SB, 1), F32)] * npat +
        [pltpu.VMEM((2 * npat, ATTN_BLOCK, ATTN_BLOCK), F32)],
        compiler_params=_cp(("parallel", "arbitrary")),
    )(proj, proj, proj, proj, proj, proj, slopes)


def attn_fused_bwd(proj, do_a, o_mix, ltot, slopes, cfg, name):
    S, E, AH = cfg.S, cfg.E, cfg.AH
    units, SB = _attn_units(cfg)
    nsb = S // SB
    last = nsb - 1
    scale = E ** -0.5

    def spec(off, prev):
        c0 = _blk(off, E)
        if prev:
            return pl.BlockSpec((SB, E), lambda h, i: (jnp.maximum(i - 1, 0), c0 + h))
        return pl.BlockSpec((SB, E), lambda h, i: (jnp.minimum(i, last), c0 + h))

    cur = pl.BlockSpec((SB, E), lambda h, i: (jnp.minimum(i, last), h))
    prev = pl.BlockSpec((SB, E), lambda h, i: (jnp.maximum(i - 1, 0), h))

    def body(q_ref, kp_ref, kc_ref, vp_ref, vc_ref, z_ref, doa_ref, om_ref, lt_ref, sl_ref,
             dq_ref, dk_ref, dv_ref, dz_ref, dmix_s, dl_s, dq_s, dkp_s, dvp_s, dkc_s, dvc_s, bias_s):
        i = pl.program_id(1)

        @pl.when(i == 0)
        def _():
            dkc_s[...] = jnp.zeros_like(dkc_s)
            dvc_s[...] = jnp.zeros_like(dvc_s)
            _set_bias_tiles(bias_s, sl_ref[0, 0:1, :], cfg)

        @pl.when(i < nsb)
        def _():
            z = z_ref[...]
            s = _sigmoid(z)
            doa = doa_ref[...].astype(F32)
            om = om_ref[...]
            dmix = doa * (z * s)
            dmix_s[...] = dmix
            dz_ref[...] = (doa * om * (s * (1.0 + z * (1.0 - s)))).astype(BF16)
            dl_s[...] = jnp.sum(dmix * om, axis=1, keepdims=True)
            dkp_s[...] = dkc_s[...]
            dvp_s[...] = dvc_s[...]
            dkc_s[...] = jnp.zeros_like(dkc_s)
            dvc_s[...] = jnp.zeros_like(dvc_s)
            dq_s[...] = jnp.zeros_like(dq_s)
            for p, d, base, pbase, from_prev in units:
                rows, prows = _attn_rows(base, d), _attn_rows(pbase, d)
                q = q_ref[rows, :].astype(BF16)
                kc = kc_ref[rows, :].astype(BF16)
                kp = (kp_ref if from_prev else kc_ref)[prows, :].astype(BF16)
                vp = (vp_ref if from_prev else vc_ref)[prows, :].astype(BF16)
                do = dmix_s[rows, :].astype(BF16)
                lt = lt_ref[rows, :]
                dlt = dl_s[rows, :]
                kcat = jnp.concatenate([kp, kc], axis=0)
                vcat = jnp.concatenate([vp, vc_ref[rows, :].astype(BF16)], axis=0)
                pr = jnp.exp(_unit_scores(q, kcat, bias_s, p, (i > 0) if from_prev else None, scale) - lt)
                ds = (pr * (_dot(do, vcat, NT) - dlt) * scale).astype(BF16)
                dq_s[rows, :] += _dot(ds, kcat, NN)
                dkcat = _dot(ds, q, TN)
                dvcat = _dot(pr.astype(BF16), do, TN)
                dk_t, dv_t = (dkp_s, dvp_s) if from_prev else (dkc_s, dvc_s)
                dk_t[prows, :] += dkcat[:ATTN_BLOCK, :]
                dv_t[prows, :] += dvcat[:ATTN_BLOCK, :]
                dkc_s[rows, :] += dkcat[ATTN_BLOCK:, :]
                dvc_s[rows, :] += dvcat[ATTN_BLOCK:, :]
            dq_ref[...] = dq_s[...].astype(BF16)
            dk_ref[...] = dkp_s[...].astype(BF16)
            dv_ref[...] = dvp_s[...].astype(BF16)

        @pl.when(i == nsb)
        def _():
            dk_ref[...] = dkc_s[...].astype(BF16)
            dv_ref[...] = dvc_s[...].astype(BF16)

    sh = jax.ShapeDtypeStruct((S, cfg.AW), BF16)
    acc = pltpu.VMEM((SB, E), F32)
    return pl.pallas_call(
        body, name=name, grid=(AH, nsb + 1),
        in_specs=[spec(0, False), spec(cfg.k0, True), spec(cfg.k0, False), spec(cfg.v0, True), spec(cfg.v0, False),
                  spec(cfg.za0, False), cur, cur,
                  pl.BlockSpec((SB, 1), lambda h, i: (h * nsb + jnp.minimum(i, last), 0)),
                  pl.BlockSpec((1, 8, LANES), lambda h, i: (h, 0, 0))],
        out_specs=[cur, prev, prev, cur], out_shape=[sh] * 4,
        scratch_shapes=[acc, pltpu.VMEM((SB, 1), F32), acc, acc, acc, acc, acc,
                        pltpu.VMEM((2 * len(cfg.patterns), ATTN_BLOCK, ATTN_BLOCK), F32)],
        compiler_params=_cp(("parallel", "arbitrary")),
    )(proj, proj, proj, proj, proj, proj, do_a, o_mix, ltot, slopes)


HALO = 8


def _conv_taps(x_ref, h_ref, kc):
    x = x_ref[...]
    full = jnp.concatenate([jnp.where(pl.program_id(1) == 0, 0.0, h_ref[...]), x], axis=0)
    return [pltpu.roll(full, s, axis=0)[HALO:, :] for s in range(kc - 1, 0, -1)] + [x]


def _conv_pre(taps, w_ref, b_ref):
    pre = b_ref[...] + w_ref[0:1, :] * taps[0]
    for k in range(1, len(taps)):
        pre = pre + w_ref[k:k + 1, :] * taps[k]
    return pre


def conv_fwd(proj, w, b, cfg, name, tm=1024, tc=512):
    S, CD, KC = cfg.S, cfg.CD, cfg.KC
    tc = min(tc, CD)
    c0 = _blk(cfg.xbc0, tc)
    hb = tm // HALO

    def body(x_ref, h_ref, w_ref, b_ref, o_ref):
        pre = _conv_pre(_conv_taps(x_ref, h_ref, KC), w_ref, b_ref)
        o_ref[...] = pre * _sigmoid(pre)

    return pl.pallas_call(
        body, name=name, grid=(CD // tc, S // tm),
        in_specs=[pl.BlockSpec((tm, tc), lambda c, i: (i, c0 + c)),
                  pl.BlockSpec((HALO, tc), lambda c, i: (jnp.maximum(i * hb - 1, 0), c0 + c)),
                  pl.BlockSpec((KC, tc), lambda c, i: (0, c)),
                  pl.BlockSpec((1, tc), lambda c, i: (0, c))],
        out_specs=pl.BlockSpec((tm, tc), lambda c, i: (i, c)),
        out_shape=jax.ShapeDtypeStruct((S, CD), F32),
        compiler_params=_cp(("parallel", "arbitrary")),
    )(proj, proj, w, b)


def conv_bwd_a(proj, dxc, w, b, cfg, name, c_off, tm=1024, tc=512):
    S, KC = cfg.S, cfg.KC
    CD = dxc.shape[1]
    tc = min(tc, CD)
    c0 = _blk(cfg.xbc0 + c_off, tc)
    w0 = _blk(c_off, tc)
    hb = tm // HALO

    def body(x_ref, h_ref, d_ref, w_ref, b_ref, dp_ref, gw_ref, gb_ref):
        taps = _conv_taps(x_ref, h_ref, KC)
        pre = _conv_pre(taps, w_ref, b_ref)
        s = _sigmoid(pre)
        dpre = d_ref[...] * (s * (1.0 + pre * (1.0 - s)))
        dp_ref[...] = dpre
        gb = jnp.sum(dpre, axis=0, keepdims=True)
        gws = [jnp.sum(dpre * taps[k], axis=0, keepdims=True) for k in range(KC)]
        gw = jnp.concatenate(gws + [jnp.zeros((8 - KC, tc), F32)], axis=0)

        @pl.when(pl.program_id(1) == 0)
        def _():
            gw_ref[...] = gw
            gb_ref[...] = gb

        @pl.when(pl.program_id(1) > 0)
        def _():
            gw_ref[...] += gw
            gb_ref[...] += gb

    return pl.pallas_call(
        body, name=name, grid=(CD // tc, S // tm),
        in_specs=[pl.BlockSpec((tm, tc), lambda c, i: (i, c0 + c)),
                  pl.BlockSpec((HALO, tc), lambda c, i: (jnp.maximum(i * hb - 1, 0), c0 + c)),
                  pl.BlockSpec((tm, tc), lambda c, i: (i, c)),
                  pl.BlockSpec((KC, tc), lambda c, i: (0, w0 + c)),
                  pl.BlockSpec((1, tc), lambda c, i: (0, w0 + c))],
        out_specs=[pl.BlockSpec((tm, tc), lambda c, i: (i, c)),
                   pl.BlockSpec((8, tc), lambda c, i: (0, c)),
                   pl.BlockSpec((1, tc), lambda c, i: (0, c))],
        out_shape=[jax.ShapeDtypeStruct((S, CD), F32), jax.ShapeDtypeStruct((8, CD), F32),
                   jax.ShapeDtypeStruct((1, CD), F32)],
        compiler_params=_cp(("parallel", "arbitrary")),
    )(proj, proj, dxc, w, b)


def conv_bwd_b(dpre, w, cfg, name, c_off, tm=1024, tc=512):
    S, KC = cfg.S, cfg.KC
    CD = dpre.shape[1]
    tc = min(tc, CD)
    w0 = _blk(c_off, tc)
    hb = tm // HALO
    nrb = S // tm
    last_h = S // HALO - 1

    def body(d_ref, h_ref, w_ref, o_ref):
        d = d_ref[...]
        full = jnp.concatenate([d, jnp.where(pl.program_id(1) == nrb - 1, 0.0, h_ref[...])], axis=0)
        acc = w_ref[KC - 1:KC, :] * d
        for j in range(1, KC):
            acc = acc + w_ref[KC - 1 - j:KC - j, :] * pltpu.roll(full, tm + HALO - j, axis=0)[:tm, :]
        o_ref[...] = acc.astype(BF16)

    return pl.pallas_call(
        body, name=name, grid=(CD // tc, nrb),
        in_specs=[pl.BlockSpec((tm, tc), lambda c, i: (i, c)),
                  pl.BlockSpec((HALO, tc), lambda c, i: (jnp.minimum((i + 1) * hb, last_h), c)),
                  pl.BlockSpec((KC, tc), lambda c, i: (0, w0 + c))],
        out_specs=pl.BlockSpec((tm, tc), lambda c, i: (i, c)),
        out_shape=jax.ShapeDtypeStruct((S, CD), BF16),
        compiler_params=_cp(("parallel", "arbitrary")),
    )(dpre, dpre, w)


def _pad_lanes(v, width=LANES):
    return jnp.pad(v, ((0, 0), (0, width - v.shape[1])))


def ssd_prep(dt_raw, dt_bias, a_log, cfg, name):
    S, L = cfg.S, cfg.L

    def body(x_ref, b_ref, al_ref, dt_ref, ac_ref):
        x = x_ref[...] + b_ref[...]
        dt = jnp.maximum(x, 0.0) + jnp.log(1.0 + jnp.exp(-jnp.abs(x)))
        da = dt * (-jnp.exp(al_ref[...]))
        li = lax.broadcasted_iota(jnp.int32, (L, L), 0)
        si = lax.broadcasted_iota(jnp.int32, (L, L), 1)
        tri = jnp.where(li >= si, 1.0, 0.0).astype(F32)
        dt_ref[...] = dt
        ac_ref[...] = lax.dot_general(tri, da, ((NN), ((), ())), precision=lax.Precision.HIGHEST,
                                      preferred_element_type=F32)

    row = pl.BlockSpec((L, LANES), lambda i: (i, 0))
    vec = pl.BlockSpec((1, LANES), lambda i: (0, 0))
    sh = jax.ShapeDtypeStruct((S, LANES), F32)
    return pl.pallas_call(
        body, name=name, grid=(S // L,), in_specs=[row, vec, vec], out_specs=[row, row], out_shape=[sh, sh],
        compiler_params=_cp(("parallel",)),
    )(dt_raw, dt_bias, a_log)


def _spread(v, n):
    return jnp.broadcast_to(v[:, :, None], v.shape + (n,)).reshape(v.shape[0], v.shape[1] * n)


def _head_selectors(cfg):
    def sel(width):
        head = jnp.arange(LANES)[None, :, None]
        slot = jnp.arange(cfg.SG)[:, None, None] * cfg.HPG + (jnp.arange(cfg.HPG * width) // width)[None, None, :]
        return (head == slot).astype(BF16)
    return sel(cfg.P), sel(LANES)


def _spread_heads(v, sel):
    hi = v.astype(BF16)
    r1 = v - hi.astype(F32)
    mid = r1.astype(BF16)
    lo = (r1 - mid.astype(F32)).astype(BF16)
    return _dot(hi, sel, NN) + _dot(mid, sel, NN) + _dot(lo, sel, NN)


def _pair_select(halves, p):
    low = lax.broadcasted_iota(jnp.int32, halves[0].shape, 1) < p
    return jnp.where(low, halves[0], halves[1])


def _head_rows(row, hpg, p):
    return jnp.concatenate([jnp.broadcast_to(row[:, j * LANES:(j + 1) * LANES], (p, LANES)) for j in range(hpg)],
                           axis=0)


def _segment_sums(t, sel):
    hi = t.astype(BF16)
    lo = (t - hi.astype(F32)).astype(BF16)
    return _dot(hi, sel, NT) + _dot(lo, sel, NT)


def ssd_scan_fwd(xc, dt, acum, act, sel_p, sel_l, cfg, name):
    S, L, P, SN, HPG, SG, SI = cfg.S, cfg.L, cfg.P, cfg.SN, cfg.HPG, cfg.SG, cfg.SI
    nc = S // L
    GW = HPG * P
    bcol, ccol = _blk(SI, SN), _blk(SI + cfg.GN, SN)

    def body(xs_ref, b_ref, c_ref, dtn_ref, acn_ref, at_ref, sp_ref, sl_ref, y_ref, st_ref, st):
        @pl.when(pl.program_id(1) == 0)
        def _():
            st[...] = jnp.zeros_like(st)

        acn = acn_ref[...]
        dts = _spread_heads(dtn_ref[...], sp_ref[0])
        a_p = _spread_heads(acn, sp_ref[0])
        acs = _spread_heads(acn, sl_ref[0])
        s0 = st[...]
        st_ref[0] = s0.reshape(HPG, P, SN)
        B = b_ref[...].astype(BF16)
        C = c_ref[...].astype(BF16)
        G = _dot(C, B, NT)
        causal = lax.broadcasted_iota(jnp.int32, (L, L), 0) >= lax.broadcasted_iota(jnp.int32, (L, L), 1)
        xdt = xs_ref[...] * dts
        xdtb = xdt.astype(BF16)
        ws = jnp.exp(a_p[L - 1:L, :] - a_p)
        yo = jnp.exp(a_p) * _dot(C, s0.astype(BF16), NT)
        yd = []
        for jp in range(HPG // 2):
            x_pair = xdtb[:, jp * LANES:(jp + 1) * LANES]
            halves = []
            for j in (2 * jp, 2 * jp + 1):
                dm = jnp.where(causal, jnp.exp(acs[:, j * LANES:(j + 1) * LANES] - at_ref[j:j + 1, :]), 0.0)
                halves.append(_dot((G * dm).astype(BF16), x_pair, NN))
            yd.append(_pair_select(halves, P))
        y_ref[...] = jnp.concatenate(yd, axis=1) + yo
        st[...] = _head_rows(jnp.exp(acs[L - 1:L, :]), HPG, P) * s0 + _dot((xdt * ws).astype(BF16), B, TN)

    y, states = pl.pallas_call(
        body, name=name, grid=(SG, nc),
        in_specs=[pl.BlockSpec((L, GW), lambda g, c: (c, g)),
                  pl.BlockSpec((L, SN), lambda g, c: (c, bcol + g)),
                  pl.BlockSpec((L, SN), lambda g, c: (c, ccol + g)),
                  pl.BlockSpec((L, LANES), lambda g, c: (c, 0)),
                  pl.BlockSpec((L, LANES), lambda g, c: (c, 0)),
                  pl.BlockSpec((HPG, L), lambda g, c: (g, c)),
                  pl.BlockSpec((1, LANES, GW), lambda g, c: (g, 0, 0)),
                  pl.BlockSpec((1, LANES, HPG * LANES), lambda g, c: (g, 0, 0))],
        out_specs=[pl.BlockSpec((L, GW), lambda g, c: (c, g)),
                   pl.BlockSpec((1, HPG, P, SN), lambda g, c: (c, g, 0, 0))],
        out_shape=[jax.ShapeDtypeStruct((S, SI), F32), jax.ShapeDtypeStruct((nc, cfg.SH, P, SN), F32)],
        scratch_shapes=[pltpu.VMEM((GW, SN), F32)],
        compiler_params=_cp(("parallel", "arbitrary")),
    )(xc, xc, xc, dt, acum, act, sel_p, sel_l)
    return y, states


def ssd_scan_bwd(xc, dt, acum, act, sel_p, sel_l, states, y, dy, dvec, cfg, name, side):
    S, L, P, SN, HPG, SG, SI = cfg.S, cfg.L, cfg.P, cfg.SN, cfg.HPG, cfg.SG, cfg.SI
    nc = S // L
    GW = HPG * P
    bcol, ccol = _blk(SI, SN), _blk(SI + cfg.GN, SN)

    def rc(c):
        return nc - 1 - c

    def body(xs_ref, b_ref, c_ref, dtn_ref, acn_ref, at_ref, sp_ref, sl_ref, st_ref, y_ref, dy_ref, dk_ref,
             dxs_ref, db_ref, dc_ref, dac_ref, dxsum_ref, dst):
        @pl.when(pl.program_id(1) == 0)
        def _():
            dst[...] = jnp.zeros_like(dst)

        sel = sp_ref[0]
        acn = acn_ref[...]
        dts = _spread_heads(dtn_ref[...], sel)
        a_p = _spread_heads(acn, sel)
        acs = _spread_heads(acn, sl_ref[0])
        B = b_ref[...].astype(BF16)
        C = c_ref[...].astype(BF16)
        G = _dot(C, B, NT)
        causal = lax.broadcasted_iota(jnp.int32, (L, L), 0) >= lax.broadcasted_iota(jnp.int32, (L, L), 1)
        low = lax.broadcasted_iota(jnp.int32, (L, LANES), 1) < P
        xs = xs_ref[...]
        dY = dy_ref[...].astype(F32)
        xdt = xs * dts
        xdtb = xdt.astype(BF16)
        dYb = dY.astype(BF16)
        s0 = st_ref[0].reshape(GW, SN)
        s0b = s0.astype(BF16)
        ds1 = dst[...]
        ds1b = ds1.astype(BF16)
        ws = jnp.exp(a_p[L - 1:L, :] - a_p)
        dR = (jnp.exp(a_p) * dY).astype(BF16)
        dX2 = ws * _dot(B, ds1b, NT)
        dgsum = jnp.zeros((L, L), F32)
        dX1, yd = [], []
        for jp in range(HPG // 2):
            lanes = slice(jp * LANES, (jp + 1) * LANES)
            x_pair, dy_pair = xdtb[:, lanes], dYb[:, lanes]
            h1, h2 = [], []
            for h, j in enumerate((2 * jp, 2 * jp + 1)):
                dm = jnp.where(causal, jnp.exp(acs[:, j * LANES:(j + 1) * LANES] - at_ref[j:j + 1, :]), 0.0)
                mine = low if h == 0 else jnp.logical_not(low)
                dgsum = dgsum + _dot(jnp.where(mine, dy_pair, jnp.zeros_like(dy_pair)), x_pair, NT) * dm
                Mb = (G * dm).astype(BF16)
                h1.append(_dot(Mb, dy_pair, TN))
                h2.append(_dot(Mb, x_pair, NN))
            dX1.append(_pair_select(h1, P))
            yd.append(_pair_select(h2, P))
        dX1 = jnp.concatenate(dX1, axis=1)
        dX = dX1 + dX2
        pair = (dYb.astype(F32) - dY) * jnp.concatenate(yd, axis=1) - xdtb.astype(F32) * dX1
        through = _segment_sums(xdt * dX2, sel)
        u = ds1 * s0
        u_hi = u.astype(BF16)
        ones = jnp.ones((16, SN), BF16)
        u_rows = _dot(ones, u_hi, NT) + _dot(ones, (u - u_hi.astype(F32)).astype(BF16), NT)
        at_end = jnp.exp(acn[L - 1:L, :]) * _segment_sums(u_rows, sel)[0:1, :] + jnp.sum(through, axis=0, keepdims=True)
        is_last = lax.broadcasted_iota(jnp.int32, (L, LANES), 0) == L - 1
        dac_ref[0] = _segment_sums(dY * y_ref[...] + pair, sel) - through + jnp.where(is_last, at_end, 0.0)
        dxsum_ref[0] = _segment_sums(dX * xs, sel)
        dxs_ref[...] = dX * dts + dk_ref[...] * dY
        dst[...] = _head_rows(jnp.exp(acs[L - 1:L, :]), HPG, P) * ds1 + _dot(dR, C, TN)
        dgb = dgsum.astype(BF16)
        dc_ref[...] = _dot(dR, s0b, NN) + _dot(dgb, B, NN)
        db_ref[...] = _dot((xdt * ws).astype(BF16), ds1b, NN) + _dot(dgb, C, TN)

    wide = pl.BlockSpec((L, GW), lambda g, c: (rc(c), g))
    colspec = pl.BlockSpec((1, L, LANES), lambda g, c: (g, rc(c), 0))
    whole = pl.BlockSpec(memory_space=pl.ANY)
    arrs, gathers = side
    grid = (SG, nc)
    res = pl.pallas_call(
        with_exchange(body, 12, 5, gathers, grid), name=name, grid=grid,
        in_specs=[wide,
                  pl.BlockSpec((L, SN), lambda g, c: (rc(c), bcol + g)),
                  pl.BlockSpec((L, SN), lambda g, c: (rc(c), ccol + g)),
                  pl.BlockSpec((L, LANES), lambda g, c: (rc(c), 0)),
                  pl.BlockSpec((L, LANES), lambda g, c: (rc(c), 0)),
                  pl.BlockSpec((HPG, L), lambda g, c: (g, rc(c))),
                  pl.BlockSpec((1, LANES, GW), lambda g, c: (g, 0, 0)),
                  pl.BlockSpec((1, LANES, HPG * LANES), lambda g, c: (g, 0, 0)),
                  pl.BlockSpec((1, HPG, P, SN), lambda g, c: (rc(c), g, 0, 0)),
                  wide, wide,
                  pl.BlockSpec((1, GW), lambda g, c: (0, g))] + [whole] * len(arrs),
        out_specs=[wide,
                   pl.BlockSpec((L, SN), lambda g, c: (rc(c), g)),
                   pl.BlockSpec((L, SN), lambda g, c: (rc(c), g)),
                   colspec, colspec] + [whole] * len(arrs),
        out_shape=[jax.ShapeDtypeStruct((S, SI), F32), jax.ShapeDtypeStruct((S, cfg.GN), F32),
                   jax.ShapeDtypeStruct((S, cfg.GN), F32),
                   jax.ShapeDtypeStruct((SG, S, LANES), F32), jax.ShapeDtypeStruct((SG, S, LANES), F32)] +
        _exchange_shapes(arrs, gathers),
        scratch_shapes=[pltpu.VMEM((GW, SN), F32)] + _exchange_sems(len(arrs)),
        compiler_params=_cp(("arbitrary", "arbitrary")),
    )(xc, xc, xc, dt, acum, act, sel_p, sel_l, states, y, dy, dvec, *arrs)
    return res[:5], res[5:]


def dt_bwd(dac, dxsum, dt_raw, dt, dt_bias, a_log, cfg, name):
    S, L, SG = cfg.S, cfg.L, cfg.SG

    def body(da_ref, dx_ref, x_ref, dt_ref, b_ref, al_ref, o_ref, gb_ref, ga_ref):
        a = -jnp.exp(al_ref[...])
        dtv = dt_ref[...]
        dxs = jnp.sum(dx_ref[...], axis=0)
        upper = jnp.where(lax.broadcasted_iota(jnp.int32, (L, L), 1) >= lax.broadcasted_iota(jnp.int32, (L, L), 0),
                          1.0, 0.0).astype(F32)
        dda = lax.dot_general(upper, jnp.sum(da_ref[...], axis=0), (NN, ((), ())), precision=lax.Precision.HIGHEST,
                              preferred_element_type=F32)
        draw = (dxs + dda * a) * _sigmoid(x_ref[...] + b_ref[...])
        o_ref[...] = draw.astype(BF16)
        gb = jnp.sum(draw, axis=0, keepdims=True)
        ga = jnp.sum(dda * dtv, axis=0, keepdims=True) * a

        @pl.when(pl.program_id(0) == 0)
        def _():
            gb_ref[...] = gb
            ga_ref[...] = ga

        @pl.when(pl.program_id(0) > 0)
        def _():
            gb_ref[...] += gb
            ga_ref[...] += ga

    row = pl.BlockSpec((L, LANES), lambda i: (i, 0))
    vec = pl.BlockSpec((1, LANES), lambda i: (0, 0))
    return pl.pallas_call(
        body, name=name, grid=(S // L,),
        in_specs=[pl.BlockSpec((SG, L, LANES), lambda i: (0, i, 0))] * 2 + [row, row, vec, vec],
        out_specs=[row, vec, vec],
        out_shape=[jax.ShapeDtypeStruct((S, LANES), BF16), jax.ShapeDtypeStruct((1, LANES), F32),
                   jax.ShapeDtypeStruct((1, LANES), F32)],
        compiler_params=_cp(("arbitrary",)),
    )(dac, dxsum, dt_raw, dt, dt_bias, a_log)


def gated_norm_fwd(y, xc, proj, dvec, nw, cfg, name, tm=128):
    S, SI = cfg.S, cfg.SI

    def body(y_ref, xs_ref, z_ref, d_ref, w_ref, o_ref):
        z = z_ref[...]
        yg = (y_ref[...] + d_ref[...] * xs_ref[...]) * (z * _sigmoid(z))
        r = lax.rsqrt(jnp.mean(yg * yg, axis=-1, keepdims=True) + RMS_EPS)
        o_ref[...] = ((yg * r) * w_ref[...]).astype(BF16)

    row = pl.BlockSpec((tm, SI), lambda i: (i, 0))
    vec = pl.BlockSpec((1, SI), lambda i: (0, 0))
    return pl.pallas_call(
        body, name=name, grid=(S // tm,),
        in_specs=[row, row, pl.BlockSpec((tm, SI), lambda i: (i, _blk(cfg.zs0, SI))), vec, vec],
        out_specs=row, out_shape=jax.ShapeDtypeStruct((S, SI), BF16),
        compiler_params=_cp(("parallel",)),
    )(y, xc, proj, dvec, nw)


def gated_norm_bwd(dyn, y, xc, proj, dvec, nw, cfg, name, tm=128):
    S, SI = cfg.S, cfg.SI

    def body(dn_ref, y_ref, xs_ref, z_ref, d_ref, w_ref, dy_ref, dz_ref, gw_ref, gd_ref):
        z = z_ref[...]
        s = _sigmoid(z)
        sz = z * s
        xs = xs_ref[...]
        yf = y_ref[...] + d_ref[...] * xs
        yg = yf * sz
        r = lax.rsqrt(jnp.mean(yg * yg, axis=-1, keepdims=True) + RMS_EPS)
        dn = dn_ref[...].astype(F32)
        g = dn * w_ref[...]
        dyg = r * g - yg * (r * r * r) * jnp.mean(g * yg, axis=-1, keepdims=True)
        dy = dyg * sz
        dy_ref[...] = dy.astype(BF16)
        dz_ref[...] = (dyg * yf * (s * (1.0 + z * (1.0 - s)))).astype(BF16)
        gw = jnp.sum(dn * (yg * r), axis=0, keepdims=True)
        gd = jnp.sum(dy * xs, axis=0, keepdims=True)

        @pl.when(pl.program_id(0) == 0)
        def _():
            gw_ref[...] = gw
            gd_ref[...] = gd

        @pl.when(pl.program_id(0) > 0)
        def _():
            gw_ref[...] += gw
            gd_ref[...] += gd

    row = pl.BlockSpec((tm, SI), lambda i: (i, 0))
    vec = pl.BlockSpec((1, SI), lambda i: (0, 0))
    return pl.pallas_call(
        body, name=name, grid=(S // tm,),
        in_specs=[row, row, row, pl.BlockSpec((tm, SI), lambda i: (i, _blk(cfg.zs0, SI))), vec, vec],
        out_specs=[row, row, vec, vec],
        out_shape=[jax.ShapeDtypeStruct((S, SI), BF16), jax.ShapeDtypeStruct((S, SI), BF16),
                   jax.ShapeDtypeStruct((1, SI), F32), jax.ShapeDtypeStruct((1, SI), F32)],
        compiler_params=_cp(("arbitrary",)),
    )(dyn, y, xc, proj, dvec, nw)


def _shard_columns(cfg, main, dt):
    dt0 = 4 * cfg.AW + cfg.SI + cfg.CD
    ws = cfg.N_IN // N_DEV
    out = []
    for k in range(N_DEV):
        lo, hi, parts = k * ws, (k + 1) * ws, []
        if lo < dt0:
            parts.append((main, lo, min(hi, dt0)))
        if lo < dt0 + cfg.SH and hi > dt0:
            parts.append((dt, max(lo, dt0) - dt0, min(hi, dt0 + cfg.SH) - dt0))
        if hi > dt0 + cfg.SH:
            parts.append((main, max(lo, dt0 + cfg.SH) - cfg.SH, hi - cfg.SH))
        out.append(parts)
    return out


def local_step(cfg, x, tgt, norm_w, conv_w, conv_b, dt_bias, a_log, d_skip, ssm_norm_w, final_norm_w,
               w_main, w_dt, shards, dt0):
    S, D = cfg.S, cfg.D
    slopes = _slope_table(cfg)
    dt_bias_p = _pad_lanes(dt_bias)
    a_log_p = _pad_lanes(a_log)
    dvec = _spread(d_skip, cfg.P)

    hn = rmsnorm_fwd(x, norm_w, "rmsnorm_fwd")
    proj, gathered = matmul(hn, w_main, 'nn', 1024, 1024, 2048, F32, "in_proj", side=(shards, [True] * 3))
    w_attn, w_ssm, w_out = gathered[0].reshape(cfg.AW, D), gathered[1].reshape(cfg.SI, D), gathered[2].reshape(D, D)
    dt_raw = matmul(hn, w_dt, 'nn', 512, 128, 2048, F32, "in_proj_dt")
    o_a, o_mix, ltot = attn_fused_fwd(proj, slopes, cfg, "attn_fwd")
    xc = conv_fwd(proj, conv_w, conv_b, cfg, "conv_fwd")
    dt, acum = ssd_prep(dt_raw, dt_bias_p, a_log_p, cfg, "ssd_prep")
    act = acum[:, :cfg.SH].T
    sel_p, sel_l = _head_selectors(cfg)
    y, states = ssd_scan_fwd(xc, dt, acum, act, sel_p, sel_l, cfg, "ssd_scan_fwd")
    y_n = gated_norm_fwd(y, xc, proj, dvec, ssm_norm_w, cfg, "gated_norm_fwd")
    a_out, s_out, merged = branch_merge(o_a, w_attn, y_n, w_ssm, proj, cfg, "branch_merge")
    dout, loss_p, g_final_w = out_proj_final(merged, w_out, x, final_norm_w.reshape(1, D), tgt, "out_proj_final")

    g_w_out = matmul(merged, dout, 'tn', 1024, 1024, 2048, BF16, "g_w_out")
    da_out, ds_out, dga, dgs = merge_bwd(dout, w_out, a_out, s_out, proj, cfg, "merge_bwd")
    g_w_attn = matmul(o_a, da_out, 'tn', 1024, 1024, 2048, BF16, "g_w_attn")
    g_w_ssm = matmul(y_n, ds_out, 'tn', 1024, 1024, 2048, BF16, "g_w_ssm")
    do_a = matmul(da_out, w_attn, 'nt', 512, 1024, 2048, BF16, "d_o_a")
    dyn = matmul(ds_out, w_ssm, 'nt', 512, 1024, 2048, BF16, "d_y_n")
    dy, dz_s, g_ssm_norm, g_dvec = gated_norm_bwd(dyn, y, xc, proj, dvec, ssm_norm_w, cfg, "gated_norm_bwd")
    sends = [g.reshape((N_DEV, g.shape[0] // N_DEV, D)) for g in (g_w_attn, g_w_ssm, g_w_out)]
    (dxs, dB, dC, dac_g, dxsum_g), (r_attn, r_ssm, r_out) = ssd_scan_bwd(
        xc, dt, acum, act, sel_p, sel_l, states, y, dy, dvec, cfg, "ssd_scan_bwd", side=(sends, [False] * 3))
    ddt_raw, g_dt_bias, g_a_log = dt_bwd(dac_g, dxsum_g, dt_raw, dt, dt_bias_p, a_log_p, cfg, "dt_bwd")
    dxbc, g_cw, g_cb = [], [], []
    for nm, piece, c_off in (("xs", dxs, 0), ("b", dB, cfg.SI), ("c", dC, cfg.SI + cfg.GN)):
        dpre, gw, gb = conv_bwd_a(proj, piece, conv_w, conv_b, cfg, "conv_bwd_a_" + nm, c_off)
        dxbc.append(conv_bwd_b(dpre, conv_w, cfg, "conv_bwd_b_" + nm, c_off))
        g_cw.append(gw)
        g_cb.append(gb)
    g_conv_w, g_conv_b = jnp.concatenate(g_cw, axis=1), jnp.concatenate(g_cb, axis=1)
    dq, dk, dv, dz_a = attn_fused_bwd(proj, do_a, o_mix, ltot, slopes, cfg, "attn_bwd")
    dproj = jnp.concatenate([dq, dk, dv, dz_a, dz_s] + dxbc + [dga, dgs], axis=1)
    g_w_main = matmul(hn, dproj, 'tn', 1024, 1024, 2048, BF16, "g_w_main")
    g_w_dt = matmul(hn, ddt_raw, 'tn', 1024, 128, 2048, BF16, "g_w_dt")
    send_in = jnp.stack([jnp.concatenate([g[:, lo:hi] for g, lo, hi in parts], axis=1)
                         for parts in _shard_columns(cfg, g_w_main, g_w_dt)])
    dhn_a, (r_in,) = matmul(dproj, w_main, 'nt', 1024, 1024, 2048, F32, "d_hn", side=([send_in], [False]))
    dhn_b = matmul(ddt_raw, w_dt, 'nt', 512, 1024, 128, F32, "d_hn_dt")
    grad_x, g_norm_w = rmsnorm_bwd(dhn_a, dhn_b, x, norm_w, dout, "rmsnorm_bwd")

    g_d_skip = jnp.sum(g_dvec.reshape(cfg.SH, cfg.P), axis=1).reshape(1, cfg.SH)
    small = dict(norm_w=g_norm_w, conv_b=g_conv_b, dt_bias=g_dt_bias[:, :cfg.SH], a_log=g_a_log[:, :cfg.SH],
                 d_skip=g_d_skip, ssm_norm_w=g_ssm_norm, final_norm_w=g_final_w, conv_w=g_conv_w[:cfg.KC])
    return loss_p, grad_x, small, dict(w_in=r_in, w_attn=r_attn, w_ssm=r_ssm, w_out=r_out)


def _mesh_pos():
    return lax.axis_index("x"), lax.axis_index("y"), lax.axis_index("c")


def _flat(pos):
    return 4 * pos[0] + 2 * pos[1] + pos[2]


def _exchange_shapes(arrs, gathers):
    return [jax.ShapeDtypeStruct(((N_DEV,) + a.shape) if g else a.shape, a.dtype) for a, g in zip(arrs, gathers)]


def _exchange_sems(n):
    return [pltpu.SemaphoreType.DMA((n * (N_DEV - 1),)), pltpu.SemaphoreType.DMA((n * (N_DEV - 1),)),
            pltpu.SemaphoreType.DMA((n,))]


def _exchange_copies(ins, outs, gathers, send_sems, recv_sems, loc_sems):
    pos = _mesh_pos()
    me = _flat(pos)
    starts, waits = [], []
    for a in range(len(ins)):
        mine = ins[a] if gathers[a] else ins[a].at[me]
        loc = pltpu.make_async_copy(mine, outs[a].at[me], loc_sems.at[a])
        starts.append(loc)
        waits.append(loc)
        for k in range(1, N_DEV):
            flip = ((k >> 2) & 1, (k >> 1) & 1, k & 1)
            peer = tuple(1 - p if f else p for p, f in zip(pos, flip))
            pk = _flat(peer)
            src = ins[a] if gathers[a] else ins[a].at[pk]
            sems = dict(send_sem=send_sems.at[a * (N_DEV - 1) + k - 1], recv_sem=recv_sems.at[a * (N_DEV - 1) + k - 1],
                        device_id=peer, device_id_type=pl.DeviceIdType.MESH)
            starts.append(pltpu.make_async_remote_copy(src_ref=src, dst_ref=outs[a].at[me], **sems))
            waits.append(pltpu.make_async_remote_copy(src_ref=src, dst_ref=outs[a].at[pk], **sems))
    return starts, waits


def exchange(arrs, gathers, name):
    n = len(arrs)

    def body(*refs):
        starts, waits = _exchange_copies(refs[:n], refs[n:2 * n], gathers, *refs[2 * n:])
        for cp in starts:
            cp.start()
        for cp in waits:
            cp.wait()

    hbm = pl.BlockSpec(memory_space=pltpu.HBM)
    return pl.pallas_call(
        body, name=name, in_specs=[hbm] * n, out_specs=[hbm] * n, out_shape=_exchange_shapes(arrs, gathers),
        scratch_shapes=_exchange_sems(n),
    )(*arrs)


def gather_two_level(arrs, chunks, name):
    n = len(arrs)

    def body(*refs):
        ins, outs = refs[:n], refs[n:2 * n]
        send_sems, recv_sems, loc_sems = refs[2 * n:]
        x, y, c = _mesh_pos()
        me, sib = (x, y, c), (x, y, 1 - c)
        chips = [(1 - x, y), (x, 1 - y), (1 - x, 1 - y)]
        plan, base = [], 0
        for a in range(n):
            step = arrs[a].shape[0] // chunks[a]
            for q in range(chunks[a]):
                plan.append((a, pl.ds(q * step, step), base))
                base += N_DEV - 1

        def copy(a, rows, sem, block, to, own=False):
            dst = outs[a].at[_flat(block), rows]
            return pltpu.make_async_remote_copy(
                src_ref=ins[a].at[rows] if own else dst, dst_ref=dst, send_sem=send_sems.at[sem],
                recv_sem=recv_sems.at[sem], device_id=to, device_id_type=pl.DeviceIdType.MESH)

        local = [pltpu.make_async_copy(ins[a], outs[a].at[_flat(me)], loc_sems.at[a]) for a in range(n)]
        for cp in local:
            cp.start()
        sent = []
        for a, rows, s in plan:
            sent.append(copy(a, rows, s, me, sib, own=True))
            sent += [copy(a, rows, s + 1 + j, me, (*chip, c), own=True) for j, chip in enumerate(chips)]
        for cp in sent:
            cp.start()
        for a, rows, s in plan:
            for j, chip in enumerate(chips):
                copy(a, rows, s + 1 + j, (*chip, c), me).wait_recv()
                passed = copy(a, rows, s + 4 + j, (*chip, c), sib)
                passed.start()
                sent.append(passed)
        for a, rows, s in plan:
            copy(a, rows, s, sib, me).wait_recv()
            for j, chip in enumerate(chips):
                copy(a, rows, s + 4 + j, (*chip, 1 - c), me).wait_recv()
        for cp in sent:
            cp.wait_send()
        for cp in local:
            cp.wait()

    hbm = pl.BlockSpec(memory_space=pltpu.HBM)
    nsem = (N_DEV - 1) * sum(chunks)
    return pl.pallas_call(
        body, name=name, in_specs=[hbm] * n, out_specs=[hbm] * n, out_shape=_exchange_shapes(arrs, [True] * n),
        scratch_shapes=[pltpu.SemaphoreType.DMA((nsem,)), pltpu.SemaphoreType.DMA((nsem,)),
                        pltpu.SemaphoreType.DMA((n,))],
    )(*arrs)


def with_exchange(body, n_in, n_out, gathers, grid):
    n = len(gathers)

    def wrapped(*refs):
        ins, sends = refs[:n_in], refs[n_in:n_in + n]
        outs, recvs = refs[n_in + n:n_in + n + n_out], refs[n_in + 2 * n + n_out - n:n_in + 2 * n + n_out]
        scratch, sems = refs[n_in + 2 * n + n_out:-3], refs[-3:]
        ids = [pl.program_id(d) for d in range(len(grid))]
        first = functools.reduce(jnp.logical_and, [i == 0 for i in ids])
        last = functools.reduce(jnp.logical_and, [i == g - 1 for i, g in zip(ids, grid)])

        @pl.when(first)
        def _():
            for cp in _exchange_copies(sends, recvs, gathers, *sems)[0]:
                cp.start()

        body(*ins, *outs, *scratch)

        @pl.when(last)
        def _():
            for cp in _exchange_copies(sends, recvs, gathers, *sems)[1]:
                cp.wait()

    return wrapped


def adamw(g_src, w, m, v, summed, name, tr=64):
    R, C = w.shape
    tr = min(tr, R)
    assert R % tr == 0

    def body(g_ref, w_ref, m_ref, v_ref, g_out, d_out, m_out, v_out):
        if summed:
            g = g_ref[0].astype(F32)
            for j in range(1, N_DEV):
                g = g + g_ref[j].astype(F32)
        else:
            g = g_ref[...]
        mn = ADAM_B1 * m_ref[...] + (1.0 - ADAM_B1) * g
        vn = ADAM_B2 * v_ref[...] + (1.0 - ADAM_B2) * (g * g)
        m_hat = mn / (1.0 - ADAM_B1 ** ADAM_STEP)
        v_hat = vn / (1.0 - ADAM_B2 ** ADAM_STEP)
        g_out[...] = g
        d_out[...] = -ADAM_LR * (m_hat / (jnp.sqrt(v_hat) + ADAM_EPS) + ADAM_WD * w_ref[...])
        m_out[...] = mn
        v_out[...] = vn

    row = pl.BlockSpec((tr, C), lambda i: (i, 0))
    gspec = pl.BlockSpec((N_DEV, tr, C), lambda i: (0, i, 0)) if summed else row
    sh = jax.ShapeDtypeStruct((R, C), F32)
    return pl.pallas_call(
        body, name=name, grid=(R // tr,), in_specs=[gspec, row, row, row], out_specs=[row] * 4, out_shape=[sh] * 4,
        compiler_params=_cp(("parallel",)),
    )(g_src, w, m, v)


SMALL = ('norm_w', 'conv_b', 'dt_bias', 'a_log', 'd_skip', 'ssm_norm_w', 'final_norm_w')


def _rows(n):
    return -(-n // (8 * LANES)) * 8


def _pack(vals):
    parts = []
    for a in vals:
        f = a.reshape(-1)
        parts.append(jnp.pad(f, (0, _rows(f.size) * LANES - f.size)).reshape(-1, LANES))
    return jnp.concatenate(parts, axis=0)


def _unpack(packed, shapes):
    out, r = [], 0
    for s in shapes:
        n = math.prod(s)
        out.append(packed[r:r + _rows(n)].reshape(-1)[:n].reshape(s))
        r += _rows(n)
    return out


def kernel(x, norm_w, w_in, conv_w, conv_b, dt_bias, a_log, d_skip, ssm_norm_w, w_attn_branch, w_ssm_branch, w_out, final_norm_w, loss_target, m_norm_w, m_w_in, m_conv_w, m_conv_b, m_dt_bias, m_a_log, m_d_skip, m_ssm_norm_w, m_w_attn_branch, m_w_ssm_branch, m_w_out, m_final_norm_w, v_norm_w, v_w_in, v_conv_w, v_conv_b, v_dt_bias, v_a_log, v_d_skip, v_ssm_norm_w, v_w_attn_branch, v_w_ssm_branch, v_w_out, v_final_norm_w):
    cfg = CFG
    D, SH = cfg.D, cfg.SH
    me = _flat(_mesh_pos())
    dt0 = 4 * cfg.AW + cfg.SI + cfg.CD
    ws = w_in.shape[-1]

    g_in, g_cw = gather_two_level([w_in[0].astype(BF16), conv_w[0]], [4, 1], "gather_w_in")
    main_cols, dt_cols = [], []
    for k, parts in enumerate(_shard_columns(cfg, "main", "dt")):
        at = 0
        for which, lo, hi in parts:
            (main_cols if which == "main" else dt_cols).append(g_in[k][:, at:at + hi - lo])
            at += hi - lo
    w_main = jnp.concatenate(main_cols, axis=1)
    w_dt = _pad_lanes(jnp.concatenate(dt_cols, axis=1))
    conv_full = g_cw.transpose(1, 0, 2).reshape(cfg.KC, cfg.CD)
    shards = [w_attn_branch[0].astype(BF16), w_ssm_branch[0].astype(BF16), w_out[0].astype(BF16)]

    loss_p, grad_x, small, recv = local_step(
        cfg, x[0], loss_target[0], norm_w, conv_full, conv_b, dt_bias, a_log, d_skip,
        ssm_norm_w, final_norm_w, w_main, w_dt, shards, dt0)

    upd = {}
    upd['w_in'] = adamw(recv['w_in'], w_in[0], m_w_in[0], v_w_in[0], True, "adamw_w_in")
    upd['w_attn_branch'] = adamw(recv['w_attn'], w_attn_branch[0], m_w_attn_branch[0], v_w_attn_branch[0], True,
                                 "adamw_w_attn")
    upd['w_ssm_branch'] = adamw(recv['w_ssm'], w_ssm_branch[0], m_w_ssm_branch[0], v_w_ssm_branch[0], True,
                                "adamw_w_ssm")
    upd['w_out'] = adamw(recv['w_out'], w_out[0], m_w_out[0], v_w_out[0], True, "adamw_w_out")

    extra = [jnp.zeros((cfg.KC, cfg.CD), F32), jnp.zeros((1, 1), F32)]
    shapes = [small[n].shape for n in SMALL] + [e.shape for e in extra]
    part = _pack([small[n] for n in SMALL] + [small['conv_w'], loss_p[:, :1]])
    gathered, = exchange([part], [True], "gather_small")
    given = dict(norm_w=(norm_w, m_norm_w, v_norm_w), conv_b=(conv_b, m_conv_b, v_conv_b),
                 dt_bias=(dt_bias, m_dt_bias, v_dt_bias), a_log=(a_log, m_a_log, v_a_log),
                 d_skip=(d_skip, m_d_skip, v_d_skip), ssm_norm_w=(ssm_norm_w, m_ssm_norm_w, v_ssm_norm_w),
                 final_norm_w=(final_norm_w, m_final_norm_w, v_final_norm_w))
    packed = [_pack([given[n][t] for n in SMALL] + extra) for t in range(3)]
    outs = adamw(gathered, *packed, True, "adamw_small", tr=part.shape[0])
    unpacked = [_unpack(o, shapes) for o in outs]
    for i, n in enumerate(SMALL):
        upd[n] = [u[i].reshape(given[n][0].shape) for u in unpacked]
    loss = unpacked[0][-1].reshape(())
    cw = conv_w.shape[-1]
    g_cw_mine = lax.dynamic_slice_in_dim(unpacked[0][-2], me * cw, cw, axis=1)
    upd['conv_w'] = adamw(g_cw_mine.reshape(-1, LANES), conv_w.reshape(-1, LANES), m_conv_w.reshape(-1, LANES),
                          v_conv_w.reshape(-1, LANES), False, "adamw_conv_w")

    order = ['norm_w', 'w_in', 'conv_w', 'conv_b', 'dt_bias', 'a_log', 'd_skip', 'ssm_norm_w', 'w_attn_branch',
             'w_ssm_branch', 'w_out', 'final_norm_w']
    like = dict(norm_w=norm_w, w_in=w_in, conv_w=conv_w, conv_b=conv_b, dt_bias=dt_bias, a_log=a_log, d_skip=d_skip,
                ssm_norm_w=ssm_norm_w, w_attn_branch=w_attn_branch, w_ssm_branch=w_ssm_branch, w_out=w_out,
                final_norm_w=final_norm_w)
    result = [loss, grad_x[None]]
    for t in range(4):
        result += [upd[n][t].reshape(like[n].shape) for n in order]
    return tuple(result)
```

```python
import functools
import math
from typing import NamedTuple

import jax
import jax.numpy as jnp
from jax import lax
from jax.experimental import pallas as pl
from jax.experimental.pallas import tpu as pltpu

F32 = jnp.float32
BF16 = jnp.bfloat16
RMS_EPS = 1e-6
NEG = -1e30
N_DEV = 8
LANES = 128
ATTN_BLOCK = 128
ADAM_LR, ADAM_B1, ADAM_B2, ADAM_EPS, ADAM_WD, ADAM_STEP = 0.001, 0.9, 0.999, 1e-08, 0.01, 10
VMEM_LIMIT = 56 * 1024 * 1024


class Cfg(NamedTuple):
    D: int = 2048
    S: int = 8192
    AH: int = 16
    E: int = 128
    HB: int = 4
    patterns: tuple = ((128, 1), (512, 4), (2048, 16))
    SI: int = 4096
    P: int = 64
    SG: int = 8
    SN: int = 128
    KC: int = 4
    L: int = 128

    @property
    def AW(self): return self.AH * self.E
    @property
    def SH(self): return self.SI // self.P
    @property
    def HPG(self): return self.SH // self.SG
    @property
    def GN(self): return self.SG * self.SN
    @property
    def CD(self): return self.SI + 2 * self.GN
    @property
    def k0(self): return self.AW
    @property
    def v0(self): return 2 * self.AW
    @property
    def za0(self): return 3 * self.AW
    @property
    def zs0(self): return 4 * self.AW
    @property
    def xbc0(self): return 4 * self.AW + self.SI
    @property
    def ga0(self): return self.xbc0 + self.CD
    @property
    def gs0(self): return self.ga0 + self.D
    @property
    def NP(self): return self.gs0 + self.D
    @property
    def N_IN(self): return self.NP + self.SH


CFG = Cfg()


def _cp(sem=None, vmem=VMEM_LIMIT):
    return pltpu.CompilerParams(dimension_semantics=sem, vmem_limit_bytes=vmem)


def _sigmoid(z):
    return 1.0 / (1.0 + jnp.exp(-z))


def _dot(a, b, dims):
    return lax.dot_general(a, b, (dims, ((), ())), preferred_element_type=F32)


NN = ((1,), (0,))
NT = ((1,), (1,))
TN = ((0,), (0,))


def _blk(off, width):
    assert off % width == 0, (off, width)
    return off // width


def matmul(a, b, mode, tm, tn, tk, out_dtype, name, side=None):
    if mode == 'nn':
        (M, K), (_, N) = a.shape, b.shape
    elif mode == 'nt':
        (M, K), (N, _) = a.shape, b.shape
    else:
        (K, M), (_, N) = a.shape, b.shape
    tm, tn, tk = min(tm, M), min(tn, N), min(tk, K)
    assert M % tm == 0 and N % tn == 0 and K % tk == 0, (M, N, K, tm, tn, tk)
    nk = K // tk
    dims = {'nn': NN, 'nt': NT, 'tn': TN}[mode]

    def body(a_ref, b_ref, o_ref, *acc):
        part = _dot(a_ref[...].astype(BF16), b_ref[...].astype(BF16), dims)
        if nk == 1:
            o_ref[...] = part.astype(out_dtype)
        else:
            acc_ref, = acc
            k = pl.program_id(2)

            @pl.when(k == 0)
            def _():
                acc_ref[...] = part

            @pl.when(k > 0)
            def _():
                acc_ref[...] += part

            @pl.when(k == nk - 1)
            def _():
                o_ref[...] = acc_ref[...].astype(out_dtype)

    if mode == 'tn':
        a_spec = pl.BlockSpec((tk, tm), lambda n, m, k: (k, m))
    else:
        a_spec = pl.BlockSpec((tm, tk), lambda n, m, k: (m, k))
    if mode == 'nt':
        b_spec = pl.BlockSpec((tn, tk), lambda n, m, k: (n, k))
    else:
        b_spec = pl.BlockSpec((tk, tn), lambda n, m, k: (k, n))
    grid = (N // tn, M // tm, nk)
    o_spec = pl.BlockSpec((tm, tn), lambda n, m, k: (m, n))
    o_shape = jax.ShapeDtypeStruct((M, N), out_dtype)
    acc = [] if nk == 1 else [pltpu.VMEM((tm, tn), F32)]
    if side is None:
        return pl.pallas_call(
            body, name=name, grid=grid, in_specs=[a_spec, b_spec], out_specs=o_spec, out_shape=o_shape,
            scratch_shapes=acc, compiler_params=_cp(("parallel", "parallel", "arbitrary")),
        )(a, b)
    arrs, gathers = side
    whole = pl.BlockSpec(memory_space=pl.ANY)
    res = pl.pallas_call(
        with_exchange(body, 2, 1, gathers, grid), name=name, grid=grid,
        in_specs=[a_spec, b_spec] + [whole] * len(arrs), out_specs=[o_spec] + [whole] * len(arrs),
        out_shape=[o_shape] + _exchange_shapes(arrs, gathers),
        scratch_shapes=acc + _exchange_sems(len(arrs)),
        compiler_params=_cp(("arbitrary", "arbitrary", "arbitrary")),
    )(a, b, *arrs)
    return res[0], res[1:]


def rmsnorm_fwd(x, w, name, tm=256):
    S, D = x.shape

    def body(x_ref, w_ref, o_ref):
        xv = x_ref[...]
        r = lax.rsqrt(jnp.mean(xv * xv, axis=-1, keepdims=True) + RMS_EPS)
        o_ref[...] = ((xv * r) * w_ref[...]).astype(BF16)

    return pl.pallas_call(
        body, name=name, grid=(S // tm,),
        in_specs=[pl.BlockSpec((tm, D), lambda i: (i, 0)), pl.BlockSpec((1, D), lambda i: (0, 0))],
        out_specs=pl.BlockSpec((tm, D), lambda i: (i, 0)),
        out_shape=jax.ShapeDtypeStruct((S, D), BF16),
        compiler_params=_cp(("parallel",)),
    )(x, w)


def rmsnorm_bwd(dh_a, dh_b, x, w, dout, name, tm=128):
    S, D = x.shape

    def body(da_ref, db_ref, x_ref, w_ref, do_ref, gx_ref, gw_ref):
        xv = x_ref[...]
        dh = da_ref[...] + db_ref[...]
        r = lax.rsqrt(jnp.mean(xv * xv, axis=-1, keepdims=True) + RMS_EPS)
        g = dh * w_ref[...]
        dx = r * g - xv * (r * r * r) * jnp.mean(g * xv, axis=-1, keepdims=True)
        gx_ref[...] = do_ref[...] + dx
        gw = jnp.sum(dh * (xv * r), axis=0, keepdims=True)

        @pl.when(pl.program_id(0) == 0)
        def _():
            gw_ref[...] = gw

        @pl.when(pl.program_id(0) > 0)
        def _():
            gw_ref[...] += gw

    row = pl.BlockSpec((tm, D), lambda i: (i, 0))
    vec = pl.BlockSpec((1, D), lambda i: (0, 0))
    return pl.pallas_call(
        body, name=name, grid=(S // tm,),
        in_specs=[row, row, row, vec, row],
        out_specs=[row, vec],
        out_shape=[jax.ShapeDtypeStruct((S, D), F32), jax.ShapeDtypeStruct((1, D), F32)],
        compiler_params=_cp(("arbitrary",)),
    )(dh_a, dh_b, x, w, dout)


def out_proj_final(merged, w_out, x, fw, tgt, name, tm=256):
    S, D = x.shape

    def body(m_ref, wo_ref, x_ref, w_ref, t_ref, do_ref, loss_ref, gw_ref):
        out = x_ref[...] + _dot(m_ref[...], wo_ref[...], NN)
        w = w_ref[...]
        r = lax.rsqrt(jnp.mean(out * out, axis=-1, keepdims=True) + RMS_EPS)
        yn = out * r
        err = yn * w - t_ref[...]
        lrow = 0.5 * jnp.mean(err * err, axis=-1, keepdims=True)
        lsum = jnp.zeros((1, LANES), F32) + jnp.sum(lrow, axis=0, keepdims=True)
        dfin = err * (1.0 / D)
        g = dfin * w
        do_ref[...] = r * g - out * (r * r * r) * jnp.mean(g * out, axis=-1, keepdims=True)
        gw = jnp.sum(dfin * yn, axis=0, keepdims=True)

        @pl.when(pl.program_id(0) == 0)
        def _():
            gw_ref[...] = gw
            loss_ref[...] = lsum

        @pl.when(pl.program_id(0) > 0)
        def _():
            gw_ref[...] += gw
            loss_ref[...] += lsum

    row = pl.BlockSpec((tm, D), lambda i: (i, 0))
    vec = pl.BlockSpec((1, D), lambda i: (0, 0))
    return pl.pallas_call(
        body, name=name, grid=(S // tm,),
        in_specs=[row, pl.BlockSpec((D, D), lambda i: (0, 0)), row, vec, row],
        out_specs=[row, pl.BlockSpec((1, LANES), lambda i: (0, 0)), vec],
        out_shape=[jax.ShapeDtypeStruct((S, D), F32), jax.ShapeDtypeStruct((1, LANES), F32),
                   jax.ShapeDtypeStruct((1, D), F32)],
        compiler_params=_cp(("arbitrary",)),
    )(merged, w_out, x, fw, tgt)


def branch_merge(o_a, w_attn, y_n, w_ssm, proj, cfg, name, tm=512, tn=512):
    S, D = cfg.S, cfg.D
    tm, tn = min(tm, S), min(tn, D)

    def body(oa_ref, wa_ref, yn_ref, ws_ref, ga_ref, gs_ref, a_ref, s_ref, m_ref):
        a = _dot(oa_ref[...], wa_ref[...], NN)
        sv = _dot(yn_ref[...], ws_ref[...], NN)
        a_ref[...] = a.astype(BF16)
        s_ref[...] = sv.astype(BF16)
        m_ref[...] = (_sigmoid(ga_ref[...]) * a + _sigmoid(gs_ref[...]) * sv).astype(BF16)

    tile = pl.BlockSpec((tm, tn), lambda n, m: (m, n))
    return pl.pallas_call(
        body, name=name, grid=(D // tn, S // tm),
        in_specs=[pl.BlockSpec((tm, cfg.AW), lambda n, m: (m, 0)), pl.BlockSpec((cfg.AW, tn), lambda n, m: (0, n)),
                  pl.BlockSpec((tm, cfg.SI), lambda n, m: (m, 0)), pl.BlockSpec((cfg.SI, tn), lambda n, m: (0, n)),
                  pl.BlockSpec((tm, tn), lambda n, m: (m, _blk(cfg.ga0, tn) + n)),
                  pl.BlockSpec((tm, tn), lambda n, m: (m, _blk(cfg.gs0, tn) + n))],
        out_specs=[tile, tile, tile],
        out_shape=[jax.ShapeDtypeStruct((S, D), BF16)] * 3,
        compiler_params=_cp(("parallel", "parallel")),
    )(o_a, w_attn, y_n, w_ssm, proj, proj)


def merge_bwd(dout, w_out, a_out, s_out, proj, cfg, name, tm=512, tn=1024):
    S, D = cfg.S, cfg.D
    tm, tn = min(tm, S), min(tn, D)

    def body(do_ref, wo_ref, a_ref, s_ref, ga_ref, gs_ref, da_ref, ds_ref, dga_ref, dgs_ref):
        dmv = _dot(do_ref[...].astype(BF16), wo_ref[...], NT)
        sa = _sigmoid(ga_ref[...])
        ss = _sigmoid(gs_ref[...])
        da_ref[...] = (dmv * sa).astype(BF16)
        ds_ref[...] = (dmv * ss).astype(BF16)
        dga_ref[...] = (dmv * a_ref[...] * (sa * (1.0 - sa))).astype(BF16)
        dgs_ref[...] = (dmv * s_ref[...] * (ss * (1.0 - ss))).astype(BF16)

    tile = pl.BlockSpec((tm, tn), lambda n, m: (m, n))
    sh = jax.ShapeDtypeStruct((S, D), BF16)
    return pl.pallas_call(
        body, name=name, grid=(D // tn, S // tm),
        in_specs=[pl.BlockSpec((tm, D), lambda n, m: (m, 0)), pl.BlockSpec((tn, D), lambda n, m: (n, 0)), tile, tile,
                  pl.BlockSpec((tm, tn), lambda n, m: (m, _blk(cfg.ga0, tn) + n)),
                  pl.BlockSpec((tm, tn), lambda n, m: (m, _blk(cfg.gs0, tn) + n))],
        out_specs=[tile] * 4, out_shape=[sh] * 4,
        compiler_params=_cp(("parallel", "parallel")),
    )(dout, w_out, a_out, s_out, proj, proj)


def _attn_rows(base, d):
    return pl.ds(base, ATTN_BLOCK) if d == 1 else pl.ds(base, ATTN_BLOCK, stride=d)


def _attn_units(cfg):
    dmax = max(d for _, d in cfg.patterns)
    units = []
    for p, (window, d) in enumerate(cfg.patterns):
        assert window // d == ATTN_BLOCK and dmax % d == 0
        nsub = dmax // d
        for b in range(nsub):
            for r in range(d):
                base = b * ATTN_BLOCK * d + r
                if b > 0:
                    units.append((p, d, base, (b - 1) * ATTN_BLOCK * d + r, False))
                else:
                    units.append((p, d, base, (nsub - 1) * ATTN_BLOCK * d + r, True))
    return units, ATTN_BLOCK * dmax


def _set_bias_tiles(bias_s, slope, cfg):
    qi = lax.broadcasted_iota(jnp.int32, (ATTN_BLOCK, ATTN_BLOCK), 0)
    ki = lax.broadcasted_iota(jnp.int32, (ATTN_BLOCK, ATTN_BLOCK), 1)
    for p, (_, d) in enumerate(cfg.patterns):
        bias_s[2 * p] = jnp.where(ki >= qi, (-slope) * ((ATTN_BLOCK + qi - ki) * d).astype(F32), NEG)
        bias_s[2 * p + 1] = jnp.where(ki <= qi, (-slope) * ((qi - ki) * d).astype(F32), NEG)


def _unit_scores(q, kcat, bias_s, p, prev_ok, scale):
    s = _dot(q, kcat, NT) * scale + jnp.concatenate([bias_s[2 * p], bias_s[2 * p + 1]], axis=1)
    if prev_ok is not None:
        cur_half = lax.broadcasted_iota(jnp.int32, s.shape, 1) >= ATTN_BLOCK
        s = jnp.where(jnp.logical_or(cur_half, prev_ok), s, NEG)
    return s


def _slope_table(cfg):
    slopes = jnp.asarray([2.0 ** (-8.0 * (h + 1) / cfg.AH) for h in range(cfg.AH)], F32)
    return jnp.broadcast_to(slopes.reshape(cfg.AH, 1, 1), (cfg.AH, 8, LANES))


def attn_fused_fwd(proj, slopes, cfg, name):
    S, E, AH = cfg.S, cfg.E, cfg.AH
    units, SB = _attn_units(cfg)
    assert S % SB == 0
    npat = len(cfg.patterns)
    scale = E ** -0.5

    def spec(off, prev):
        c0 = _blk(off, E)
        if prev:
            return pl.BlockSpec((SB, E), lambda h, i: (jnp.maximum(i - 1, 0), c0 + h))
        return pl.BlockSpec((SB, E), lambda h, i: (i, c0 + h))

    def body(q_ref, kp_ref, kc_ref, vp_ref, vc_ref, z_ref, sl_ref, oa_ref, om_ref, lt_ref, *scr):
        o_s, l_s, bias_s = scr[:npat], scr[npat:2 * npat], scr[2 * npat]
        i = pl.program_id(1)

        @pl.when(i == 0)
        def _():
            _set_bias_tiles(bias_s, sl_ref[0, 0:1, :], cfg)

        for p, d, base, pbase, from_prev in units:
            rows, prows = _attn_rows(base, d), _attn_rows(pbase, d)
            q = q_ref[rows, :].astype(BF16)
            kp = (kp_ref if from_prev else kc_ref)[prows, :].astype(BF16)
            vp = (vp_ref if from_prev else vc_ref)[prows, :].astype(BF16)
            kcat = jnp.concatenate([kp, kc_ref[rows, :].astype(BF16)], axis=0)
            vcat = jnp.concatenate([vp, vc_ref[rows, :].astype(BF16)], axis=0)
            s = _unit_scores(q, kcat, bias_s, p, (i > 0) if from_prev else None, scale)
            m = jnp.max(s, axis=1, keepdims=True)
            pr = jnp.exp(s - m)
            l = jnp.sum(pr, axis=1, keepdims=True)
            o_s[p][rows, :] = _dot(pr.astype(BF16), vcat, NN) * (1.0 / l)
            l_s[p][rows, :] = m + jnp.log(l)
        ls = [l_s[p][...] for p in range(npat)]
        m = functools.reduce(jnp.maximum, ls)
        lt = m + jnp.log(sum(jnp.exp(l_ - m) for l_ in ls))
        lt_ref[...] = lt
        mix = sum(jnp.exp(ls[p] - lt) * o_s[p][...] for p in range(npat))
        om_ref[...] = mix
        z = z_ref[...]
        oa_ref[...] = (mix * (z * _sigmoid(z))).astype(BF16)

    out = pl.BlockSpec((SB, E), lambda h, i: (i, h))
    return pl.pallas_call(
        body, name=name, grid=(AH, S // SB),
        in_specs=[spec(0, False), spec(cfg.k0, True), spec(cfg.k0, False), spec(cfg.v0, True), spec(cfg.v0, False),
                  spec(cfg.za0, False), pl.BlockSpec((1, 8, LANES), lambda h, i: (h, 0, 0))],
        out_specs=[out, out, pl.BlockSpec((SB, 1), lambda h, i: (h * (S // SB) + i, 0))],
        out_shape=[jax.ShapeDtypeStruct((S, cfg.AW), BF16), jax.ShapeDtypeStruct((S, cfg.AW), F32),
                   jax.ShapeDtypeStruct((AH * S, 1), F32)],
        scratch_shapes=[pltpu.VMEM((SB, E), F32)] * npat + [pltpu.VMEM((SB, 1), F32)] * npat +
        [pltpu.VMEM((2 * npat, ATTN_BLOCK, ATTN_BLOCK), F32)],
        compiler_params=_cp(("parallel", "arbitrary")),
    )(proj, proj, proj, proj, proj, proj, slopes)


def attn_fused_bwd(proj, do_a, o_mix, ltot, slopes, cfg, name):
    S, E, AH = cfg.S, cfg.E, cfg.AH
    units, SB = _attn_units(cfg)
    nsb = S // SB
    last = nsb - 1
    scale = E ** -0.5

    def spec(off, prev):
        c0 = _blk(off, E)
        if prev:
            return pl.BlockSpec((SB, E), lambda h, i: (jnp.maximum(i - 1, 0), c0 + h))
        return pl.BlockSpec((SB, E), lambda h, i: (jnp.minimum(i, last), c0 + h))

    cur = pl.BlockSpec((SB, E), lambda h, i: (jnp.minimum(i, last), h))
    prev = pl.BlockSpec((SB, E), lambda h, i: (jnp.maximum(i - 1, 0), h))

    def body(q_ref, kp_ref, kc_ref, vp_ref, vc_ref, z_ref, doa_ref, om_ref, lt_ref, sl_ref,
             dq_ref, dk_ref, dv_ref, dz_ref, dmix_s, dl_s, dq_s, dkp_s, dvp_s, dkc_s, dvc_s, bias_s):
        i = pl.program_id(1)

        @pl.when(i == 0)
        def _():
            dkc_s[...] = jnp.zeros_like(dkc_s)
            dvc_s[...] = jnp.zeros_like(dvc_s)
            _set_bias_tiles(bias_s, sl_ref[0, 0:1, :], cfg)

        @pl.when(i < nsb)
        def _():
            z = z_ref[...]
            s = _sigmoid(z)
            doa = doa_ref[...].astype(F32)
            om = om_ref[...]
            dmix = doa * (z * s)
            dmix_s[...] = dmix
            dz_ref[...] = (doa * om * (s * (1.0 + z * (1.0 - s)))).astype(BF16)
            dl_s[...] = jnp.sum(dmix * om, axis=1, keepdims=True)
            dkp_s[...] = dkc_s[...]
            dvp_s[...] = dvc_s[...]
            dkc_s[...] = jnp.zeros_like(dkc_s)
            dvc_s[...] = jnp.zeros_like(dvc_s)
            dq_s[...] = jnp.zeros_like(dq_s)
            for p, d, base, pbase, from_prev in units:
                rows, prows = _attn_rows(base, d), _attn_rows(pbase, d)
                q = q_ref[rows, :].astype(BF16)
                kc = kc_ref[rows, :].astype(BF16)
                kp = (kp_ref if from_prev else kc_ref)[prows, :].astype(BF16)
                vp = (vp_ref if from_prev else vc_ref)[prows, :].astype(BF16)
                do = dmix_s[rows, :].astype(BF16)
                lt = lt_ref[rows, :]
                dlt = dl_s[rows, :]
                kcat = jnp.concatenate([kp, kc], axis=0)
                vcat = jnp.concatenate([vp, vc_ref[rows, :].astype(BF16)], axis=0)
                pr = jnp.exp(_unit_scores(q, kcat, bias_s, p, (i > 0) if from_prev else None, scale) - lt)
                ds = (pr * (_dot(do, vcat, NT) - dlt) * scale).astype(BF16)
                dq_s[rows, :] += _dot(ds, kcat, NN)
                dkcat = _dot(ds, q, TN)
                dvcat = _dot(pr.astype(BF16), do, TN)
                dk_t, dv_t = (dkp_s, dvp_s) if from_prev else (dkc_s, dvc_s)
                dk_t[prows, :] += dkcat[:ATTN_BLOCK, :]
                dv_t[prows, :] += dvcat[:ATTN_BLOCK, :]
                dkc_s[rows, :] += dkcat[ATTN_BLOCK:, :]
                dvc_s[rows, :] += dvcat[ATTN_BLOCK:, :]
            dq_ref[...] = dq_s[...].astype(BF16)
            dk_ref[...] = dkp_s[...].astype(BF16)
            dv_ref[...] = dvp_s[...].astype(BF16)

        @pl.when(i == nsb)
        def _():
            dk_ref[...] = dkc_s[...].astype(BF16)
            dv_ref[...] = dvc_s[...].astype(BF16)

    sh = jax.ShapeDtypeStruct((S, cfg.AW), BF16)
    acc = pltpu.VMEM((SB, E), F32)
    return pl.pallas_call(
        body, name=name, grid=(AH, nsb + 1),
        in_specs=[spec(0, False), spec(cfg.k0, True), spec(cfg.k0, False), spec(cfg.v0, True), spec(cfg.v0, False),
                  spec(cfg.za0, False), cur, cur,
                  pl.BlockSpec((SB, 1), lambda h, i: (h * nsb + jnp.minimum(i, last), 0)),
                  pl.BlockSpec((1, 8, LANES), lambda h, i: (h, 0, 0))],
        out_specs=[cur, prev, prev, cur], out_shape=[sh] * 4,
        scratch_shapes=[acc, pltpu.VMEM((SB, 1), F32), acc, acc, acc, acc, acc,
                        pltpu.VMEM((2 * len(cfg.patterns), ATTN_BLOCK, ATTN_BLOCK), F32)],
        compiler_params=_cp(("parallel", "arbitrary")),
    )(proj, proj, proj, proj, proj, proj, do_a, o_mix, ltot, slopes)


HALO = 8


def _conv_taps(x_ref, h_ref, kc):
    x = x_ref[...]
    full = jnp.concatenate([jnp.where(pl.program_id(1) == 0, 0.0, h_ref[...]), x], axis=0)
    return [pltpu.roll(full, s, axis=0)[HALO:, :] for s in range(kc - 1, 0, -1)] + [x]


def _conv_pre(taps, w_ref, b_ref):
    pre = b_ref[...] + w_ref[0:1, :] * taps[0]
    for k in range(1, len(taps)):
        pre = pre + w_ref[k:k + 1, :] * taps[k]
    return pre


def conv_fwd(proj, w, b, cfg, name, tm=1024, tc=512):
    S, CD, KC = cfg.S, cfg.CD, cfg.KC
    tc = min(tc, CD)
    c0 = _blk(cfg.xbc0, tc)
    hb = tm // HALO

    def body(x_ref, h_ref, w_ref, b_ref, o_ref):
        pre = _conv_pre(_conv_taps(x_ref, h_ref, KC), w_ref, b_ref)
        o_ref[...] = pre * _sigmoid(pre)

    return pl.pallas_call(
        body, name=name, grid=(CD // tc, S // tm),
        in_specs=[pl.BlockSpec((tm, tc), lambda c, i: (i, c0 + c)),
                  pl.BlockSpec((HALO, tc), lambda c, i: (jnp.maximum(i * hb - 1, 0), c0 + c)),
                  pl.BlockSpec((KC, tc), lambda c, i: (0, c)),
                  pl.BlockSpec((1, tc), lambda c, i: (0, c))],
        out_specs=pl.BlockSpec((tm, tc), lambda c, i: (i, c)),
        out_shape=jax.ShapeDtypeStruct((S, CD), F32),
        compiler_params=_cp(("parallel", "arbitrary")),
    )(proj, proj, w, b)


def conv_bwd_a(proj, dxc, w, b, cfg, name, c_off, tm=1024, tc=512):
    S, KC = cfg.S, cfg.KC
    CD = dxc.shape[1]
    tc = min(tc, CD)
    c0 = _blk(cfg.xbc0 + c_off, tc)
    w0 = _blk(c_off, tc)
    hb = tm // HALO

    def body(x_ref, h_ref, d_ref, w_ref, b_ref, dp_ref, gw_ref, gb_ref):
        taps = _conv_taps(x_ref, h_ref, KC)
        pre = _conv_pre(taps, w_ref, b_ref)
        s = _sigmoid(pre)
        dpre = d_ref[...] * (s * (1.0 + pre * (1.0 - s)))
        dp_ref[...] = dpre
        gb = jnp.sum(dpre, axis=0, keepdims=True)
        gws = [jnp.sum(dpre * taps[k], axis=0, keepdims=True) for k in range(KC)]
        gw = jnp.concatenate(gws + [jnp.zeros((8 - KC, tc), F32)], axis=0)

        @pl.when(pl.program_id(1) == 0)
        def _():
            gw_ref[...] = gw
            gb_ref[...] = gb

        @pl.when(pl.program_id(1) > 0)
        def _():
            gw_ref[...] += gw
            gb_ref[...] += gb

    return pl.pallas_call(
        body, name=name, grid=(CD // tc, S // tm),
        in_specs=[pl.BlockSpec((tm, tc), lambda c, i: (i, c0 + c)),
                  pl.BlockSpec((HALO, tc), lambda c, i: (jnp.maximum(i * hb - 1, 0), c0 + c)),
                  pl.BlockSpec((tm, tc), lambda c, i: (i, c)),
                  pl.BlockSpec((KC, tc), lambda c, i: (0, w0 + c)),
                  pl.BlockSpec((1, tc), lambda c, i: (0, w0 + c))],
        out_specs=[pl.BlockSpec((tm, tc), lambda c, i: (i, c)),
                   pl.BlockSpec((8, tc), lambda c, i: (0, c)),
                   pl.BlockSpec((1, tc), lambda c, i: (0, c))],
        out_shape=[jax.ShapeDtypeStruct((S, CD), F32), jax.ShapeDtypeStruct((8, CD), F32),
                   jax.ShapeDtypeStruct((1, CD), F32)],
        compiler_params=_cp(("parallel", "arbitrary")),
    )(proj, proj, dxc, w, b)


def conv_bwd_b(dpre, w, cfg, name, c_off, tm=1024, tc=512):
    S, KC = cfg.S, cfg.KC
    CD = dpre.shape[1]
    tc = min(tc, CD)
    w0 = _blk(c_off, tc)
    hb = tm // HALO
    nrb = S // tm
    last_h = S // HALO - 1

    def body(d_ref, h_ref, w_ref, o_ref):
        d = d_ref[...]
        full = jnp.concatenate([d, jnp.where(pl.program_id(1) == nrb - 1, 0.0, h_ref[...])], axis=0)
        acc = w_ref[KC - 1:KC, :] * d
        for j in range(1, KC):
            acc = acc + w_ref[KC - 1 - j:KC - j, :] * pltpu.roll(full, tm + HALO - j, axis=0)[:tm, :]
        o_ref[...] = acc.astype(BF16)

    return pl.pallas_call(
        body, name=name, grid=(CD // tc, nrb),
        in_specs=[pl.BlockSpec((tm, tc), lambda c, i: (i, c)),
                  pl.BlockSpec((HALO, tc), lambda c, i: (jnp.minimum((i + 1) * hb, last_h), c)),
                  pl.BlockSpec((KC, tc), lambda c, i: (0, w0 + c))],
        out_specs=pl.BlockSpec((tm, tc), lambda c, i: (i, c)),
        out_shape=jax.ShapeDtypeStruct((S, CD), BF16),
        compiler_params=_cp(("parallel", "arbitrary")),
    )(dpre, dpre, w)


def _pad_lanes(v, width=LANES):
    return jnp.pad(v, ((0, 0), (0, width - v.shape[1])))


def ssd_prep(dt_raw, dt_bias, a_log, cfg, name):
    S, L = cfg.S, cfg.L

    def body(x_ref, b_ref, al_ref, dt_ref, ac_ref):
        x = x_ref[...] + b_ref[...]
        dt = jnp.maximum(x, 0.0) + jnp.log(1.0 + jnp.exp(-jnp.abs(x)))
        da = dt * (-jnp.exp(al_ref[...]))
        li = lax.broadcasted_iota(jnp.int32, (L, L), 0)
        si = lax.broadcasted_iota(jnp.int32, (L, L), 1)
        tri = jnp.where(li >= si, 1.0, 0.0).astype(F32)
        dt_ref[...] = dt
        ac_ref[...] = lax.dot_general(tri, da, ((NN), ((), ())), precision=lax.Precision.HIGHEST,
                                      preferred_element_type=F32)

    row = pl.BlockSpec((L, LANES), lambda i: (i, 0))
    vec = pl.BlockSpec((1, LANES), lambda i: (0, 0))
    sh = jax.ShapeDtypeStruct((S, LANES), F32)
    return pl.pallas_call(
        body, name=name, grid=(S // L,), in_specs=[row, vec, vec], out_specs=[row, row], out_shape=[sh, sh],
        compiler_params=_cp(("parallel",)),
    )(dt_raw, dt_bias, a_log)


def _spread(v, n):
    return jnp.broadcast_to(v[:, :, None], v.shape + (n,)).reshape(v.shape[0], v.shape[1] * n)


def _head_selectors(cfg):
    def sel(width):
        head = jnp.arange(LANES)[None, :, None]
        slot = jnp.arange(cfg.SG)[:, None, None] * cfg.HPG + (jnp.arange(cfg.HPG * width) // width)[None, None, :]
        return (head == slot).astype(BF16)
    return sel(cfg.P), sel(LANES)


def _spread_heads(v, sel):
    n = v.shape[0]
    hi = v.astype(BF16)
    r1 = v - hi.astype(F32)
    mid = r1.astype(BF16)
    lo = (r1 - mid.astype(F32)).astype(BF16)
    out = _dot(jnp.concatenate([hi, mid, lo], axis=0), sel, NN)
    return out[:n] + out[n:2 * n] + out[2 * n:]


def _pair_lanes(wide, hpg, p):
    low = lax.broadcasted_iota(jnp.int32, (wide.shape[0], LANES), 1) < p
    return jnp.concatenate([jnp.where(low, wide[:, 2 * jp * LANES:(2 * jp + 1) * LANES],
                                      wide[:, (2 * jp + 1) * LANES:(2 * jp + 2) * LANES])
                            for jp in range(hpg // 2)], axis=1)


def _pair_select(halves, p):
    low = lax.broadcasted_iota(jnp.int32, halves[0].shape, 1) < p
    return jnp.where(low, halves[0], halves[1])


def _head_rows(row, hpg, p):
    return jnp.concatenate([jnp.broadcast_to(row[:, j * LANES:(j + 1) * LANES], (p, LANES)) for j in range(hpg)],
                           axis=0)


def _segment_sums(t, sel):
    r = t.shape[0]
    hi = t.astype(BF16)
    out = _dot(jnp.concatenate([hi, (t - hi.astype(F32)).astype(BF16)], axis=0), sel, NT)
    return out[:r] + out[r:]


def ssd_scan_fwd(xc, dt, acum, act, sel_p, sel_l, cfg, name):
    S, L, P, SN, HPG, SG, SI = cfg.S, cfg.L, cfg.P, cfg.SN, cfg.HPG, cfg.SG, cfg.SI
    nc = S // L
    GW = HPG * P
    bcol, ccol = _blk(SI, SN), _blk(SI + cfg.GN, SN)

    def body(xs_ref, b_ref, c_ref, dtn_ref, acn_ref, at_ref, sp_ref, sl_ref, y_ref, st_ref, st):
        @pl.when(pl.program_id(1) == 0)
        def _():
            st[...] = jnp.zeros_like(st)

        acn = acn_ref[...]
        dts = _spread_heads(dtn_ref[...], sp_ref[0])
        acs = _spread_heads(acn, sl_ref[0])
        a_p = _pair_lanes(acs, HPG, P)
        s0 = st[...]
        st_ref[0] = s0.reshape(HPG, P, SN)
        B = b_ref[...].astype(BF16)
        C = c_ref[...].astype(BF16)
        G = _dot(C, B, NT)
        causal = lax.broadcasted_iota(jnp.int32, (L, L), 0) >= lax.broadcasted_iota(jnp.int32, (L, L), 1)
        xdt = xs_ref[...] * dts
        xdtb = xdt.astype(BF16)
        ws = jnp.exp(a_p[L - 1:L, :] - a_p)
        yo = jnp.exp(a_p) * _dot(C, s0.astype(BF16), NT)
        yd = []
        for jp in range(HPG // 2):
            x_pair = xdtb[:, jp * LANES:(jp + 1) * LANES]
            halves = []
            for j in (2 * jp, 2 * jp + 1):
                dm = jnp.where(causal, jnp.exp(acs[:, j * LANES:(j + 1) * LANES] - at_ref[j:j + 1, :]), 0.0)
                halves.append(_dot((G * dm).astype(BF16), x_pair, NN))
            yd.append(_pair_select(halves, P))
        y_ref[...] = jnp.concatenate(yd, axis=1) + yo
        st[...] = _head_rows(jnp.exp(acs[L - 1:L, :]), HPG, P) * s0 + _dot((xdt * ws).astype(BF16), B, TN)

    y, states = pl.pallas_call(
        body, name=name, grid=(SG, nc),
        in_specs=[pl.BlockSpec((L, GW), lambda g, c: (c, g)),
                  pl.BlockSpec((L, SN), lambda g, c: (c, bcol + g)),
                  pl.BlockSpec((L, SN), lambda g, c: (c, ccol + g)),
                  pl.BlockSpec((L, LANES), lambda g, c: (c, 0)),
                  pl.BlockSpec((L, LANES), lambda g, c: (c, 0)),
                  pl.BlockSpec((HPG, L), lambda g, c: (g, c)),
                  pl.BlockSpec((1, LANES, GW), lambda g, c: (g, 0, 0)),
                  pl.BlockSpec((1, LANES, HPG * LANES), lambda g, c: (g, 0, 0))],
        out_specs=[pl.BlockSpec((L, GW), lambda g, c: (c, g)),
                   pl.BlockSpec((1, HPG, P, SN), lambda g, c: (c, g, 0, 0))],
        out_shape=[jax.ShapeDtypeStruct((S, SI), F32), jax.ShapeDtypeStruct((nc, cfg.SH, P, SN), F32)],
        scratch_shapes=[pltpu.VMEM((GW, SN), F32)],
        compiler_params=_cp(("parallel", "arbitrary")),
    )(xc, xc, xc, dt, acum, act, sel_p, sel_l)
    return y, states


def ssd_scan_bwd(xc, dt, acum, act, sel_p, sel_l, states, y, dy, dvec, cfg, name, side):
    S, L, P, SN, HPG, SG, SI = cfg.S, cfg.L, cfg.P, cfg.SN, cfg.HPG, cfg.SG, cfg.SI
    nc = S // L
    GW = HPG * P
    bcol, ccol = _blk(SI, SN), _blk(SI + cfg.GN, SN)

    def rc(c):
        return nc - 1 - c

    def body(xs_ref, b_ref, c_ref, dtn_ref, acn_ref, at_ref, sp_ref, sl_ref, st_ref, y_ref, dy_ref, dk_ref,
             dxs_ref, db_ref, dc_ref, dac_ref, dxsum_ref, dst):
        @pl.when(pl.program_id(1) == 0)
        def _():
            dst[...] = jnp.zeros_like(dst)

        sel = sp_ref[0]
        acn = acn_ref[...]
        dts = _spread_heads(dtn_ref[...], sel)
        acs = _spread_heads(acn, sl_ref[0])
        a_p = _pair_lanes(acs, HPG, P)
        B = b_ref[...].astype(BF16)
        C = c_ref[...].astype(BF16)
        G = _dot(C, B, NT)
        causal = lax.broadcasted_iota(jnp.int32, (L, L), 0) >= lax.broadcasted_iota(jnp.int32, (L, L), 1)
        low = lax.broadcasted_iota(jnp.int32, (L, LANES), 1) < P
        xs = xs_ref[...]
        dY = dy_ref[...].astype(F32)
        xdt = xs * dts
        xdtb = xdt.astype(BF16)
        dYb = dY.astype(BF16)
        s0 = st_ref[0].reshape(GW, SN)
        s0b = s0.astype(BF16)
        ds1 = dst[...]
        ds1b = ds1.astype(BF16)
        ws = jnp.exp(a_p[L - 1:L, :] - a_p)
        dR = (jnp.exp(a_p) * dY).astype(BF16)
        dX2 = ws * _dot(B, ds1b, NT)
        dgsum = jnp.zeros((L, L), F32)
        dX1, yd = [], []
        for jp in range(HPG // 2):
            lanes = slice(jp * LANES, (jp + 1) * LANES)
            x_pair, dy_pair = xdtb[:, lanes], dYb[:, lanes]
            h1, h2 = [], []
            for h, j in enumerate((2 * jp, 2 * jp + 1)):
                dm = jnp.where(causal, jnp.exp(acs[:, j * LANES:(j + 1) * LANES] - at_ref[j:j + 1, :]), 0.0)
                mine = low if h == 0 else jnp.logical_not(low)
                dgsum = dgsum + _dot(jnp.where(mine, dy_pair, jnp.zeros_like(dy_pair)), x_pair, NT) * dm
                Mb = (G * dm).astype(BF16)
                h1.append(_dot(Mb, dy_pair, TN))
                h2.append(_dot(Mb, x_pair, NN))
            dX1.append(_pair_select(h1, P))
            yd.append(_pair_select(h2, P))
        dX1 = jnp.concatenate(dX1, axis=1)
        dX = dX1 + dX2
        pair = (dYb.astype(F32) - dY) * jnp.concatenate(yd, axis=1) - xdtb.astype(F32) * dX1
        through = _segment_sums(xdt * dX2, sel)
        u = ds1 * s0
        u_hi = u.astype(BF16)
        ones = jnp.ones((16, SN), BF16)
        u_rows = _dot(ones, jnp.concatenate([u_hi, (u - u_hi.astype(F32)).astype(BF16)], axis=0), NT)
        u_rows = u_rows[:, :GW] + u_rows[:, GW:]
        at_end = jnp.exp(acn[L - 1:L, :]) * _segment_sums(u_rows, sel)[0:1, :] + jnp.sum(through, axis=0, keepdims=True)
        is_last = lax.broadcasted_iota(jnp.int32, (L, LANES), 0) == L - 1
        dac_ref[0] = _segment_sums(dY * y_ref[...] + pair, sel) - through + jnp.where(is_last, at_end, 0.0)
        dxsum_ref[0] = _segment_sums(dX * xs, sel)
        dxs_ref[...] = dX * dts + dk_ref[...] * dY
        dst[...] = _head_rows(jnp.exp(acs[L - 1:L, :]), HPG, P) * ds1 + _dot(dR, C, TN)
        dgb = dgsum.astype(BF16)
        dc_ref[...] = _dot(dR, s0b, NN) + _dot(dgb, B, NN)
        db_ref[...] = _dot((xdt * ws).astype(BF16), ds1b, NN) + _dot(dgb, C, TN)

    wide = pl.BlockSpec((L, GW), lambda g, c: (rc(c), g))
    colspec = pl.BlockSpec((1, L, LANES), lambda g, c: (g, rc(c), 0))
    whole = pl.BlockSpec(memory_space=pl.ANY)
    arrs, gathers = side
    grid = (SG, nc)
    res = pl.pallas_call(
        with_exchange(body, 12, 5, gathers, grid), name=name, grid=grid,
        in_specs=[wide,
                  pl.BlockSpec((L, SN), lambda g, c: (rc(c), bcol + g)),
                  pl.BlockSpec((L, SN), lambda g, c: (rc(c), ccol + g)),
                  pl.BlockSpec((L, LANES), lambda g, c: (rc(c), 0)),
                  pl.BlockSpec((L, LANES), lambda g, c: (rc(c), 0)),
                  pl.BlockSpec((HPG, L), lambda g, c: (g, rc(c))),
                  pl.BlockSpec((1, LANES, GW), lambda g, c: (g, 0, 0)),
                  pl.BlockSpec((1, LANES, HPG * LANES), lambda g, c: (g, 0, 0)),
                  pl.BlockSpec((1, HPG, P, SN), lambda g, c: (rc(c), g, 0, 0)),
                  wide, wide,
                  pl.BlockSpec((1, GW), lambda g, c: (0, g))] + [whole] * len(arrs),
        out_specs=[wide,
                   pl.BlockSpec((L, SN), lambda g, c: (rc(c), g)),
                   pl.BlockSpec((L, SN), lambda g, c: (rc(c), g)),
                   colspec, colspec] + [whole] * len(arrs),
        out_shape=[jax.ShapeDtypeStruct((S, SI), F32), jax.ShapeDtypeStruct((S, cfg.GN), F32),
                   jax.ShapeDtypeStruct((S, cfg.GN), F32),
                   jax.ShapeDtypeStruct((SG, S, LANES), F32), jax.ShapeDtypeStruct((SG, S, LANES), F32)] +
        _exchange_shapes(arrs, gathers),
        scratch_shapes=[pltpu.VMEM((GW, SN), F32)] + _exchange_sems(len(arrs)),
        compiler_params=_cp(("arbitrary", "arbitrary")),
    )(xc, xc, xc, dt, acum, act, sel_p, sel_l, states, y, dy, dvec, *arrs)
    return res[:5], res[5:]


def dt_bwd(dac, dxsum, dt_raw, dt, dt_bias, a_log, cfg, name):
    S, L, SG = cfg.S, cfg.L, cfg.SG

    def body(da_ref, dx_ref, x_ref, dt_ref, b_ref, al_ref, o_ref, gb_ref, ga_ref):
        a = -jnp.exp(al_ref[...])
        dtv = dt_ref[...]
        dxs = jnp.sum(dx_ref[...], axis=0)
        upper = jnp.where(lax.broadcasted_iota(jnp.int32, (L, L), 1) >= lax.broadcasted_iota(jnp.int32, (L, L), 0),
                          1.0, 0.0).astype(F32)
        dda = lax.dot_general(upper, jnp.sum(da_ref[...], axis=0), (NN, ((), ())), precision=lax.Precision.HIGHEST,
                              preferred_element_type=F32)
        draw = (dxs + dda * a) * _sigmoid(x_ref[...] + b_ref[...])
        o_ref[...] = draw.astype(BF16)
        gb = jnp.sum(draw, axis=0, keepdims=True)
        ga = jnp.sum(dda * dtv, axis=0, keepdims=True) * a

        @pl.when(pl.program_id(0) == 0)
        def _():
            gb_ref[...] = gb
            ga_ref[...] = ga

        @pl.when(pl.program_id(0) > 0)
        def _():
            gb_ref[...] += gb
            ga_ref[...] += ga

    row = pl.BlockSpec((L, LANES), lambda i: (i, 0))
    vec = pl.BlockSpec((1, LANES), lambda i: (0, 0))
    return pl.pallas_call(
        body, name=name, grid=(S // L,),
        in_specs=[pl.BlockSpec((SG, L, LANES), lambda i: (0, i, 0))] * 2 + [row, row, vec, vec],
        out_specs=[row, vec, vec],
        out_shape=[jax.ShapeDtypeStruct((S, LANES), BF16), jax.ShapeDtypeStruct((1, LANES), F32),
                   jax.ShapeDtypeStruct((1, LANES), F32)],
        compiler_params=_cp(("arbitrary",)),
    )(dac, dxsum, dt_raw, dt, dt_bias, a_log)


def gated_norm_fwd(y, xc, proj, dvec, nw, cfg, name, tm=128):
    S, SI = cfg.S, cfg.SI

    def body(y_ref, xs_ref, z_ref, d_ref, w_ref, o_ref):
        z = z_ref[...]
        yg = (y_ref[...] + d_ref[...] * xs_ref[...]) * (z * _sigmoid(z))
        r = lax.rsqrt(jnp.mean(yg * yg, axis=-1, keepdims=True) + RMS_EPS)
        o_ref[...] = ((yg * r) * w_ref[...]).astype(BF16)

    row = pl.BlockSpec((tm, SI), lambda i: (i, 0))
    vec = pl.BlockSpec((1, SI), lambda i: (0, 0))
    return pl.pallas_call(
        body, name=name, grid=(S // tm,),
        in_specs=[row, row, pl.BlockSpec((tm, SI), lambda i: (i, _blk(cfg.zs0, SI))), vec, vec],
        out_specs=row, out_shape=jax.ShapeDtypeStruct((S, SI), BF16),
        compiler_params=_cp(("parallel",)),
    )(y, xc, proj, dvec, nw)


def gated_norm_bwd(dyn, y, xc, proj, dvec, nw, cfg, name, tm=128):
    S, SI = cfg.S, cfg.SI

    def body(dn_ref, y_ref, xs_ref, z_ref, d_ref, w_ref, dy_ref, dz_ref, gw_ref, gd_ref):
        z = z_ref[...]
        s = _sigmoid(z)
        sz = z * s
        xs = xs_ref[...]
        yf = y_ref[...] + d_ref[...] * xs
        yg = yf * sz
        r = lax.rsqrt(jnp.mean(yg * yg, axis=-1, keepdims=True) + RMS_EPS)
        dn = dn_ref[...].astype(F32)
        g = dn * w_ref[...]
        dyg = r * g - yg * (r * r * r) * jnp.mean(g * yg, axis=-1, keepdims=True)
        dy = dyg * sz
        dy_ref[...] = dy.astype(BF16)
        dz_ref[...] = (dyg * yf * (s * (1.0 + z * (1.0 - s)))).astype(BF16)
        gw = jnp.sum(dn * (yg * r), axis=0, keepdims=True)
        gd = jnp.sum(dy * xs, axis=0, keepdims=True)

        @pl.when(pl.program_id(0) == 0)
        def _():
            gw_ref[...] = gw
            gd_ref[...] = gd

        @pl.when(pl.program_id(0) > 0)
        def _():
            gw_ref[...] += gw
            gd_ref[...] += gd

    row = pl.BlockSpec((tm, SI), lambda i: (i, 0))
    vec = pl.BlockSpec((1, SI), lambda i: (0, 0))
    return pl.pallas_call(
        body, name=name, grid=(S // tm,),
        in_specs=[row, row, row, pl.BlockSpec((tm, SI), lambda i: (i, _blk(cfg.zs0, SI))), vec, vec],
        out_specs=[row, row, vec, vec],
        out_shape=[jax.ShapeDtypeStruct((S, SI), BF16), jax.ShapeDtypeStruct((S, SI), BF16),
                   jax.ShapeDtypeStruct((1, SI), F32), jax.ShapeDtypeStruct((1, SI), F32)],
        compiler_params=_cp(("arbitrary",)),
    )(dyn, y, xc, proj, dvec, nw)


def _shard_columns(cfg, main, dt):
    dt0 = 4 * cfg.AW + cfg.SI + cfg.CD
    ws = cfg.N_IN // N_DEV
    out = []
    for k in range(N_DEV):
        lo, hi, parts = k * ws, (k + 1) * ws, []
        if lo < dt0:
            parts.append((main, lo, min(hi, dt0)))
        if lo < dt0 + cfg.SH and hi > dt0:
            parts.append((dt, max(lo, dt0) - dt0, min(hi, dt0 + cfg.SH) - dt0))
        if hi > dt0 + cfg.SH:
            parts.append((main, max(lo, dt0 + cfg.SH) - cfg.SH, hi - cfg.SH))
        out.append(parts)
    return out


def local_step(cfg, x, tgt, norm_w, conv_w, conv_b, dt_bias, a_log, d_skip, ssm_norm_w, final_norm_w,
               w_main, w_dt, shards, dt0):
    S, D = cfg.S, cfg.D
    slopes = _slope_table(cfg)
    dt_bias_p = _pad_lanes(dt_bias)
    a_log_p = _pad_lanes(a_log)
    dvec = _spread(d_skip, cfg.P)

    hn = rmsnorm_fwd(x, norm_w, "rmsnorm_fwd")
    proj, gathered = matmul(hn, w_main, 'nn', 1024, 2048, 2048, F32, "in_proj", side=(shards, [True] * 3))
    w_attn, w_ssm, w_out = gathered[0].reshape(cfg.AW, D), gathered[1].reshape(cfg.SI, D), gathered[2].reshape(D, D)
    dt_raw = matmul(hn, w_dt, 'nn', 512, 128, 2048, F32, "in_proj_dt")
    o_a, o_mix, ltot = attn_fused_fwd(proj, slopes, cfg, "attn_fwd")
    xc = conv_fwd(proj, conv_w, conv_b, cfg, "conv_fwd")
    dt, acum = ssd_prep(dt_raw, dt_bias_p, a_log_p, cfg, "ssd_prep")
    act = acum[:, :cfg.SH].T
    sel_p, sel_l = _head_selectors(cfg)
    y, states = ssd_scan_fwd(xc, dt, acum, act, sel_p, sel_l, cfg, "ssd_scan_fwd")
    y_n = gated_norm_fwd(y, xc, proj, dvec, ssm_norm_w, cfg, "gated_norm_fwd")
    a_out, s_out, merged = branch_merge(o_a, w_attn, y_n, w_ssm, proj, cfg, "branch_merge")
    dout, loss_p, g_final_w = out_proj_final(merged, w_out, x, final_norm_w.reshape(1, D), tgt, "out_proj_final")

    g_w_out = matmul(merged, dout, 'tn', 1024, 1024, 2048, BF16, "g_w_out")
    da_out, ds_out, dga, dgs = merge_bwd(dout, w_out, a_out, s_out, proj, cfg, "merge_bwd")
    g_w_attn = matmul(o_a, da_out, 'tn', 1024, 1024, 2048, BF16, "g_w_attn")
    g_w_ssm = matmul(y_n, ds_out, 'tn', 1024, 1024, 2048, BF16, "g_w_ssm")
    do_a = matmul(da_out, w_attn, 'nt', 512, 1024, 2048, BF16, "d_o_a")
    dyn = matmul(ds_out, w_ssm, 'nt', 512, 1024, 2048, BF16, "d_y_n")
    dy, dz_s, g_ssm_norm, g_dvec = gated_norm_bwd(dyn, y, xc, proj, dvec, ssm_norm_w, cfg, "gated_norm_bwd")
    sends = [g.reshape((N_DEV, g.shape[0] // N_DEV, D)) for g in (g_w_attn, g_w_ssm, g_w_out)]
    (dxs, dB, dC, dac_g, dxsum_g), (r_attn, r_ssm, r_out) = ssd_scan_bwd(
        xc, dt, acum, act, sel_p, sel_l, states, y, dy, dvec, cfg, "ssd_scan_bwd", side=(sends, [False] * 3))
    ddt_raw, g_dt_bias, g_a_log = dt_bwd(dac_g, dxsum_g, dt_raw, dt, dt_bias_p, a_log_p, cfg, "dt_bwd")
    dxbc, g_cw, g_cb = [], [], []
    for nm, piece, c_off in (("xs", dxs, 0), ("b", dB, cfg.SI), ("c", dC, cfg.SI + cfg.GN)):
        dpre, gw, gb = conv_bwd_a(proj, piece, conv_w, conv_b, cfg, "conv_bwd_a_" + nm, c_off)
        dxbc.append(conv_bwd_b(dpre, conv_w, cfg, "conv_bwd_b_" + nm, c_off))
        g_cw.append(gw)
        g_cb.append(gb)
    g_conv_w, g_conv_b = jnp.concatenate(g_cw, axis=1), jnp.concatenate(g_cb, axis=1)
    dq, dk, dv, dz_a = attn_fused_bwd(proj, do_a, o_mix, ltot, slopes, cfg, "attn_bwd")
    dproj = jnp.concatenate([dq, dk, dv, dz_a, dz_s] + dxbc + [dga, dgs], axis=1)
    g_w_main = matmul(hn, dproj, 'tn', 1024, 2048, 2048, BF16, "g_w_main")
    g_w_dt = matmul(hn, ddt_raw, 'tn', 1024, 128, 2048, BF16, "g_w_dt")
    send_in = jnp.stack([jnp.concatenate([g[:, lo:hi] for g, lo, hi in parts], axis=1)
                         for parts in _shard_columns(cfg, g_w_main, g_w_dt)])
    dhn_a, (r_in,) = matmul(dproj, w_main, 'nt', 1024, 1024, 2048, F32, "d_hn", side=([send_in], [False]))
    dhn_b = matmul(ddt_raw, w_dt, 'nt', 512, 1024, 128, F32, "d_hn_dt")
    grad_x, g_norm_w = rmsnorm_bwd(dhn_a, dhn_b, x, norm_w, dout, "rmsnorm_bwd")

    g_d_skip = jnp.sum(g_dvec.reshape(cfg.SH, cfg.P), axis=1).reshape(1, cfg.SH)
    small = dict(norm_w=g_norm_w, conv_b=g_conv_b, dt_bias=g_dt_bias[:, :cfg.SH], a_log=g_a_log[:, :cfg.SH],
                 d_skip=g_d_skip, ssm_norm_w=g_ssm_norm, final_norm_w=g_final_w, conv_w=g_conv_w[:cfg.KC])
    return loss_p, grad_x, small, dict(w_in=r_in, w_attn=r_attn, w_ssm=r_ssm, w_out=r_out)


def _mesh_pos():
    return lax.axis_index("x"), lax.axis_index("y"), lax.axis_index("c")


def _flat(pos):
    return 4 * pos[0] + 2 * pos[1] + pos[2]


def _exchange_shapes(arrs, gathers):
    return [jax.ShapeDtypeStruct(((N_DEV,) + a.shape) if g else a.shape, a.dtype) for a, g in zip(arrs, gathers)]


def _exchange_sems(n):
    return [pltpu.SemaphoreType.DMA((n * (N_DEV - 1),)), pltpu.SemaphoreType.DMA((n * (N_DEV - 1),)),
            pltpu.SemaphoreType.DMA((n,))]


def _exchange_copies(ins, outs, gathers, send_sems, recv_sems, loc_sems):
    pos = _mesh_pos()
    me = _flat(pos)
    starts, waits = [], []
    for a in range(len(ins)):
        mine = ins[a] if gathers[a] else ins[a].at[me]
        loc = pltpu.make_async_copy(mine, outs[a].at[me], loc_sems.at[a])
        starts.append(loc)
        waits.append(loc)
        for k in range(1, N_DEV):
            flip = ((k >> 2) & 1, (k >> 1) & 1, k & 1)
            peer = tuple(1 - p if f else p for p, f in zip(pos, flip))
            pk = _flat(peer)
            src = ins[a] if gathers[a] else ins[a].at[pk]
            sems = dict(send_sem=send_sems.at[a * (N_DEV - 1) + k - 1], recv_sem=recv_sems.at[a * (N_DEV - 1) + k - 1],
                        device_id=peer, device_id_type=pl.DeviceIdType.MESH)
            starts.append(pltpu.make_async_remote_copy(src_ref=src, dst_ref=outs[a].at[me], **sems))
            waits.append(pltpu.make_async_remote_copy(src_ref=src, dst_ref=outs[a].at[pk], **sems))
    return starts, waits


def exchange(arrs, gathers, name):
    n = len(arrs)

    def body(*refs):
        starts, waits = _exchange_copies(refs[:n], refs[n:2 * n], gathers, *refs[2 * n:])
        for cp in starts:
            cp.start()
        for cp in waits:
            cp.wait()

    hbm = pl.BlockSpec(memory_space=pltpu.HBM)
    return pl.pallas_call(
        body, name=name, in_specs=[hbm] * n, out_specs=[hbm] * n, out_shape=_exchange_shapes(arrs, gathers),
        scratch_shapes=_exchange_sems(n),
    )(*arrs)


def gather_two_level(arrs, chunks, name):
    n = len(arrs)

    def body(*refs):
        ins, outs = refs[:n], refs[n:2 * n]
        send_sems, recv_sems, loc_sems = refs[2 * n:]
        x, y, c = _mesh_pos()
        me, sib = (x, y, c), (x, y, 1 - c)
        chips = [(1 - x, y), (x, 1 - y), (1 - x, 1 - y)]
        plan, base = [], 0
        for a in range(n):
            step = arrs[a].shape[0] // chunks[a]
            for q in range(chunks[a]):
                plan.append((a, pl.ds(q * step, step), base))
                base += N_DEV - 1

        def copy(a, rows, sem, block, to, own=False):
            dst = outs[a].at[_flat(block), rows]
            return pltpu.make_async_remote_copy(
                src_ref=ins[a].at[rows] if own else dst, dst_ref=dst, send_sem=send_sems.at[sem],
                recv_sem=recv_sems.at[sem], device_id=to, device_id_type=pl.DeviceIdType.MESH)

        local = [pltpu.make_async_copy(ins[a], outs[a].at[_flat(me)], loc_sems.at[a]) for a in range(n)]
        for cp in local:
            cp.start()
        sent = []
        for a, rows, s in plan:
            sent.append(copy(a, rows, s, me, sib, own=True))
            sent += [copy(a, rows, s + 1 + j, me, (*chip, c), own=True) for j, chip in enumerate(chips)]
        for cp in sent:
            cp.start()
        for a, rows, s in plan:
            for j, chip in enumerate(chips):
                copy(a, rows, s + 1 + j, (*chip, c), me).wait_recv()
                passed = copy(a, rows, s + 4 + j, (*chip, c), sib)
                passed.start()
                sent.append(passed)
        for a, rows, s in plan:
            copy(a, rows, s, sib, me).wait_recv()
            for j, chip in enumerate(chips):
                copy(a, rows, s + 4 + j, (*chip, 1 - c), me).wait_recv()
        for cp in sent:
            cp.wait_send()
        for cp in local:
            cp.wait()

    hbm = pl.BlockSpec(memory_space=pltpu.HBM)
    nsem = (N_DEV - 1) * sum(chunks)
    return pl.pallas_call(
        body, name=name, in_specs=[hbm] * n, out_specs=[hbm] * n, out_shape=_exchange_shapes(arrs, [True] * n),
        scratch_shapes=[pltpu.SemaphoreType.DMA((nsem,)), pltpu.SemaphoreType.DMA((nsem,)),
                        pltpu.SemaphoreType.DMA((n,))],
    )(*arrs)


def with_exchange(body, n_in, n_out, gathers, grid):
    n = len(gathers)

    def wrapped(*refs):
        ins, sends = refs[:n_in], refs[n_in:n_in + n]
        outs, recvs = refs[n_in + n:n_in + n + n_out], refs[n_in + 2 * n + n_out - n:n_in + 2 * n + n_out]
        scratch, sems = refs[n_in + 2 * n + n_out:-3], refs[-3:]
        ids = [pl.program_id(d) for d in range(len(grid))]
        first = functools.reduce(jnp.logical_and, [i == 0 for i in ids])
        last = functools.reduce(jnp.logical_and, [i == g - 1 for i, g in zip(ids, grid)])

        @pl.when(first)
        def _():
            for cp in _exchange_copies(sends, recvs, gathers, *sems)[0]:
                cp.start()

        body(*ins, *outs, *scratch)

        @pl.when(last)
        def _():
            for cp in _exchange_copies(sends, recvs, gathers, *sems)[1]:
                cp.wait()

    return wrapped


def adamw(g_src, w, m, v, summed, name, tr=64):
    R, C = w.shape
    tr = min(tr, R)
    assert R % tr == 0

    def body(g_ref, w_ref, m_ref, v_ref, g_out, d_out, m_out, v_out):
        if summed:
            g = g_ref[0].astype(F32)
            for j in range(1, N_DEV):
                g = g + g_ref[j].astype(F32)
        else:
            g = g_ref[...]
        mn = ADAM_B1 * m_ref[...] + (1.0 - ADAM_B1) * g
        vn = ADAM_B2 * v_ref[...] + (1.0 - ADAM_B2) * (g * g)
        m_hat = mn / (1.0 - ADAM_B1 ** ADAM_STEP)
        v_hat = vn / (1.0 - ADAM_B2 ** ADAM_STEP)
        g_out[...] = g
        d_out[...] = -ADAM_LR * (m_hat / (jnp.sqrt(v_hat) + ADAM_EPS) + ADAM_WD * w_ref[...])
        m_out[...] = mn
        v_out[...] = vn

    row = pl.BlockSpec((tr, C), lambda i: (i, 0))
    gspec = pl.BlockSpec((N_DEV, tr, C), lambda i: (0, i, 0)) if summed else row
    sh = jax.ShapeDtypeStruct((R, C), F32)
    return pl.pallas_call(
        body, name=name, grid=(R // tr,), in_specs=[gspec, row, row, row], out_specs=[row] * 4, out_shape=[sh] * 4,
        compiler_params=_cp(("parallel",)),
    )(g_src, w, m, v)


SMALL = ('norm_w', 'conv_b', 'dt_bias', 'a_log', 'd_skip', 'ssm_norm_w', 'final_norm_w')


def _rows(n):
    return -(-n // (8 * LANES)) * 8


def _pack(vals):
    parts = []
    for a in vals:
        f = a.reshape(-1)
        parts.append(jnp.pad(f, (0, _rows(f.size) * LANES - f.size)).reshape(-1, LANES))
    return jnp.concatenate(parts, axis=0)


def _unpack(packed, shapes):
    out, r = [], 0
    for s in shapes:
        n = math.prod(s)
        out.append(packed[r:r + _rows(n)].reshape(-1)[:n].reshape(s))
        r += _rows(n)
    return out


def kernel(x, norm_w, w_in, conv_w, conv_b, dt_bias, a_log, d_skip, ssm_norm_w, w_attn_branch, w_ssm_branch, w_out, final_norm_w, loss_target, m_norm_w, m_w_in, m_conv_w, m_conv_b, m_dt_bias, m_a_log, m_d_skip, m_ssm_norm_w, m_w_attn_branch, m_w_ssm_branch, m_w_out, m_final_norm_w, v_norm_w, v_w_in, v_conv_w, v_conv_b, v_dt_bias, v_a_log, v_d_skip, v_ssm_norm_w, v_w_attn_branch, v_w_ssm_branch, v_w_out, v_final_norm_w):
    cfg = CFG
    D, SH = cfg.D, cfg.SH
    me = _flat(_mesh_pos())
    dt0 = 4 * cfg.AW + cfg.SI + cfg.CD
    ws = w_in.shape[-1]

    g_in, g_cw = gather_two_level([w_in[0].astype(BF16), conv_w[0]], [4, 1], "gather_w_in")
    main_cols, dt_cols = [], []
    for k, parts in enumerate(_shard_columns(cfg, "main", "dt")):
        at = 0
        for which, lo, hi in parts:
            (main_cols if which == "main" else dt_cols).append(g_in[k][:, at:at + hi - lo])
            at += hi - lo
    w_main = jnp.concatenate(main_cols, axis=1)
    w_dt = _pad_lanes(jnp.concatenate(dt_cols, axis=1))
    conv_full = g_cw.transpose(1, 0, 2).reshape(cfg.KC, cfg.CD)
    shards = [w_attn_branch[0].astype(BF16), w_ssm_branch[0].astype(BF16), w_out[0].astype(BF16)]

    loss_p, grad_x, small, recv = local_step(
        cfg, x[0], loss_target[0], norm_w, conv_full, conv_b, dt_bias, a_log, d_skip,
        ssm_norm_w, final_norm_w, w_main, w_dt, shards, dt0)

    upd = {}
    upd['w_in'] = adamw(recv['w_in'], w_in[0], m_w_in[0], v_w_in[0], True, "adamw_w_in")
    upd['w_attn_branch'] = adamw(recv['w_attn'], w_attn_branch[0], m_w_attn_branch[0], v_w_attn_branch[0], True,
                                 "adamw_w_attn")
    upd['w_ssm_branch'] = adamw(recv['w_ssm'], w_ssm_branch[0], m_w_ssm_branch[0], v_w_ssm_branch[0], True,
                                "adamw_w_ssm")
    upd['w_out'] = adamw(recv['w_out'], w_out[0], m_w_out[0], v_w_out[0], True, "adamw_w_out")

    extra = [jnp.zeros((cfg.KC, cfg.CD), F32), jnp.zeros((1, 1), F32)]
    shapes = [small[n].shape for n in SMALL] + [e.shape for e in extra]
    part = _pack([small[n] for n in SMALL] + [small['conv_w'], loss_p[:, :1]])
    gathered, = exchange([part], [True], "gather_small")
    given = dict(norm_w=(norm_w, m_norm_w, v_norm_w), conv_b=(conv_b, m_conv_b, v_conv_b),
                 dt_bias=(dt_bias, m_dt_bias, v_dt_bias), a_log=(a_log, m_a_log, v_a_log),
                 d_skip=(d_skip, m_d_skip, v_d_skip), ssm_norm_w=(ssm_norm_w, m_ssm_norm_w, v_ssm_norm_w),
                 final_norm_w=(final_norm_w, m_final_norm_w, v_final_norm_w))
    packed = [_pack([given[n][t] for n in SMALL] + extra) for t in range(3)]
    outs = adamw(gathered, *packed, True, "adamw_small", tr=part.shape[0])
    unpacked = [_unpack(o, shapes) for o in outs]
    for i, n in enumerate(SMALL):
        upd[n] = [u[i].reshape(given[n][0].shape) for u in unpacked]
    loss = unpacked[0][-1].reshape(())
    cw = conv_w.shape[-1]
    g_cw_mine = lax.dynamic_slice_in_dim(unpacked[0][-2], me * cw, cw, axis=1)
    upd['conv_w'] = adamw(g_cw_mine.reshape(-1, LANES), conv_w.reshape(-1, LANES), m_conv_w.reshape(-1, LANES),
                          v_conv_w.reshape(-1, LANES), False, "adamw_conv_w")

    order = ['norm_w', 'w_in', 'conv_w', 'conv_b', 'dt_bias', 'a_log', 'd_skip', 'ssm_norm_w', 'w_attn_branch',
             'w_ssm_branch', 'w_out', 'final_norm_w']
    like = dict(norm_w=norm_w, w_in=w_in, conv_w=conv_w, conv_b=conv_b, dt_bias=dt_bias, a_log=a_log, d_skip=d_skip,
                ssm_norm_w=ssm_norm_w, w_attn_branch=w_attn_branch, w_ssm_branch=w_ssm_branch, w_out=w_out,
                final_norm_w=final_norm_w)
    result = [loss, grad_x[None]]
    for t in range(4):
        result += [upd[n][t].reshape(like[n].shape) for n in order]
    return tuple(result)
```

```python
import functools
import math
from typing import NamedTuple

import jax
import jax.numpy as jnp
from jax import lax
from jax.experimental import pallas as pl
from jax.experimental.pallas import tpu as pltpu

F32 = jnp.float32
BF16 = jnp.bfloat16
RMS_EPS = 1e-6
NEG = -1e30
N_DEV = 8
CPS = 2
LANES = 128
ATTN_BLOCK = 128
ADAM_LR, ADAM_B1, ADAM_B2, ADAM_EPS, ADAM_WD, ADAM_STEP = 0.001, 0.9, 0.999, 1e-08, 0.01, 10
VMEM_LIMIT = 56 * 1024 * 1024


class Cfg(NamedTuple):
    D: int = 2048
    S: int = 8192
    AH: int = 16
    E: int = 128
    HB: int = 4
    patterns: tuple = ((128, 1), (512, 4), (2048, 16))
    SI: int = 4096
    P: int = 64
    SG: int = 8
    SN: int = 128
    KC: int = 4
    L: int = 128

    @property
    def AW(self): return self.AH * self.E
    @property
    def SH(self): return self.SI // self.P
    @property
    def HPG(self): return self.SH // self.SG
    @property
    def GN(self): return self.SG * self.SN
    @property
    def CD(self): return self.SI + 2 * self.GN
    @property
    def k0(self): return self.AW
    @property
    def v0(self): return 2 * self.AW
    @property
    def za0(self): return 3 * self.AW
    @property
    def zs0(self): return 4 * self.AW
    @property
    def xbc0(self): return 4 * self.AW + self.SI
    @property
    def ga0(self): return self.xbc0 + self.CD
    @property
    def gs0(self): return self.ga0 + self.D
    @property
    def NP(self): return self.gs0 + self.D
    @property
    def N_IN(self): return self.NP + self.SH


CFG = Cfg()


def _cp(sem=None, vmem=VMEM_LIMIT):
    return pltpu.CompilerParams(dimension_semantics=sem, vmem_limit_bytes=vmem)


def _sigmoid(z):
    return 1.0 / (1.0 + jnp.exp(-z))


def _dot(a, b, dims):
    return lax.dot_general(a, b, (dims, ((), ())), preferred_element_type=F32)


NN = ((1,), (0,))
NT = ((1,), (1,))
TN = ((0,), (0,))


def _blk(off, width):
    assert off % width == 0, (off, width)
    return off // width


def matmul(a, b, mode, tm, tn, tk, out_dtype, name, side=None):
    if mode == 'nn':
        (M, K), (_, N) = a.shape, b.shape
    elif mode == 'nt':
        (M, K), (N, _) = a.shape, b.shape
    else:
        (K, M), (_, N) = a.shape, b.shape
    tm, tn, tk = min(tm, M), min(tn, N), min(tk, K)
    assert M % tm == 0 and N % tn == 0 and K % tk == 0, (M, N, K, tm, tn, tk)
    nk = K // tk
    dims = {'nn': NN, 'nt': NT, 'tn': TN}[mode]

    def body(a_ref, b_ref, o_ref, *acc):
        part = _dot(a_ref[...].astype(BF16), b_ref[...].astype(BF16), dims)
        if nk == 1:
            o_ref[...] = part.astype(out_dtype)
        else:
            acc_ref, = acc
            k = pl.program_id(2)

            @pl.when(k == 0)
            def _():
                acc_ref[...] = part

            @pl.when(k > 0)
            def _():
                acc_ref[...] += part

            @pl.when(k == nk - 1)
            def _():
                o_ref[...] = acc_ref[...].astype(out_dtype)

    if mode == 'tn':
        a_spec = pl.BlockSpec((tk, tm), lambda n, m, k: (k, m))
    else:
        a_spec = pl.BlockSpec((tm, tk), lambda n, m, k: (m, k))
    if mode == 'nt':
        b_spec = pl.BlockSpec((tn, tk), lambda n, m, k: (n, k))
    else:
        b_spec = pl.BlockSpec((tk, tn), lambda n, m, k: (k, n))
    grid = (N // tn, M // tm, nk)
    o_spec = pl.BlockSpec((tm, tn), lambda n, m, k: (m, n))
    o_shape = jax.ShapeDtypeStruct((M, N), out_dtype)
    acc = [] if nk == 1 else [pltpu.VMEM((tm, tn), F32)]
    if side is None:
        return pl.pallas_call(
            body, name=name, grid=grid, in_specs=[a_spec, b_spec], out_specs=o_spec, out_shape=o_shape,
            scratch_shapes=acc, compiler_params=_cp(("parallel", "parallel", "arbitrary")),
        )(a, b)
    arrs, gathers = side
    whole = pl.BlockSpec(memory_space=pl.ANY)
    res = pl.pallas_call(
        with_exchange(body, 2, 1, gathers, grid), name=name, grid=grid,
        in_specs=[a_spec, b_spec] + [whole] * len(arrs), out_specs=[o_spec] + [whole] * len(arrs),
        out_shape=[o_shape] + _exchange_shapes(arrs, gathers),
        scratch_shapes=acc + _exchange_sems(len(arrs)),
        compiler_params=_cp(("arbitrary", "arbitrary", "arbitrary")),
    )(a, b, *arrs)
    return res[0], res[1:]


def rmsnorm_fwd(x, w, name, tm=256):
    S, D = x.shape

    def body(x_ref, w_ref, o_ref):
        xv = x_ref[...]
        r = lax.rsqrt(jnp.mean(xv * xv, axis=-1, keepdims=True) + RMS_EPS)
        o_ref[...] = ((xv * r) * w_ref[...]).astype(BF16)

    return pl.pallas_call(
        body, name=name, grid=(S // tm,),
        in_specs=[pl.BlockSpec((tm, D), lambda i: (i, 0)), pl.BlockSpec((1, D), lambda i: (0, 0))],
        out_specs=pl.BlockSpec((tm, D), lambda i: (i, 0)),
        out_shape=jax.ShapeDtypeStruct((S, D), BF16),
        compiler_params=_cp(("parallel",)),
    )(x, w)


def rmsnorm_bwd(dh_a, dh_b, x, w, dout, name, tm=128):
    S, D = x.shape

    def body(da_ref, db_ref, x_ref, w_ref, do_ref, gx_ref, gw_ref):
        xv = x_ref[...]
        dh = da_ref[...] + db_ref[...]
        r = lax.rsqrt(jnp.mean(xv * xv, axis=-1, keepdims=True) + RMS_EPS)
        g = dh * w_ref[...]
        dx = r * g - xv * (r * r * r) * jnp.mean(g * xv, axis=-1, keepdims=True)
        gx_ref[...] = do_ref[...] + dx
        gw = jnp.sum(dh * (xv * r), axis=0, keepdims=True)

        @pl.when(pl.program_id(0) == 0)
        def _():
            gw_ref[...] = gw

        @pl.when(pl.program_id(0) > 0)
        def _():
            gw_ref[...] += gw

    row = pl.BlockSpec((tm, D), lambda i: (i, 0))
    vec = pl.BlockSpec((1, D), lambda i: (0, 0))
    return pl.pallas_call(
        body, name=name, grid=(S // tm,),
        in_specs=[row, row, row, vec, row],
        out_specs=[row, vec],
        out_shape=[jax.ShapeDtypeStruct((S, D), F32), jax.ShapeDtypeStruct((1, D), F32)],
        compiler_params=_cp(("arbitrary",)),
    )(dh_a, dh_b, x, w, dout)


def out_proj_final(merged, w_out, x, fw, tgt, name, tm=256):
    S, D = x.shape

    def body(m_ref, wo_ref, x_ref, w_ref, t_ref, do_ref, loss_ref, gw_ref):
        out = x_ref[...] + _dot(m_ref[...], wo_ref[...], NN)
        w = w_ref[...]
        r = lax.rsqrt(jnp.mean(out * out, axis=-1, keepdims=True) + RMS_EPS)
        yn = out * r
        err = yn * w - t_ref[...]
        lrow = 0.5 * jnp.mean(err * err, axis=-1, keepdims=True)
        lsum = jnp.zeros((1, LANES), F32) + jnp.sum(lrow, axis=0, keepdims=True)
        dfin = err * (1.0 / D)
        g = dfin * w
        do_ref[...] = r * g - out * (r * r * r) * jnp.mean(g * out, axis=-1, keepdims=True)
        gw = jnp.sum(dfin * yn, axis=0, keepdims=True)

        @pl.when(pl.program_id(0) == 0)
        def _():
            gw_ref[...] = gw
            loss_ref[...] = lsum

        @pl.when(pl.program_id(0) > 0)
        def _():
            gw_ref[...] += gw
            loss_ref[...] += lsum

    row = pl.BlockSpec((tm, D), lambda i: (i, 0))
    vec = pl.BlockSpec((1, D), lambda i: (0, 0))
    return pl.pallas_call(
        body, name=name, grid=(S // tm,),
        in_specs=[row, pl.BlockSpec((D, D), lambda i: (0, 0)), row, vec, row],
        out_specs=[row, pl.BlockSpec((1, LANES), lambda i: (0, 0)), vec],
        out_shape=[jax.ShapeDtypeStruct((S, D), F32), jax.ShapeDtypeStruct((1, LANES), F32),
                   jax.ShapeDtypeStruct((1, D), F32)],
        compiler_params=_cp(("arbitrary",)),
    )(merged, w_out, x, fw, tgt)


def branch_merge(o_a, w_attn, y_n, w_ssm, proj, cfg, name, tm=512, tn=512):
    S, D = cfg.S, cfg.D
    tm, tn = min(tm, S), min(tn, D)

    def body(oa_ref, wa_ref, yn_ref, ws_ref, ga_ref, gs_ref, a_ref, s_ref, m_ref):
        a = _dot(oa_ref[...], wa_ref[...], NN)
        sv = _dot(yn_ref[...], ws_ref[...], NN)
        a_ref[...] = a.astype(BF16)
        s_ref[...] = sv.astype(BF16)
        m_ref[...] = (_sigmoid(ga_ref[...]) * a + _sigmoid(gs_ref[...]) * sv).astype(BF16)

    tile = pl.BlockSpec((tm, tn), lambda n, m: (m, n))
    return pl.pallas_call(
        body, name=name, grid=(D // tn, S // tm),
        in_specs=[pl.BlockSpec((tm, cfg.AW), lambda n, m: (m, 0)), pl.BlockSpec((cfg.AW, tn), lambda n, m: (0, n)),
                  pl.BlockSpec((tm, cfg.SI), lambda n, m: (m, 0)), pl.BlockSpec((cfg.SI, tn), lambda n, m: (0, n)),
                  pl.BlockSpec((tm, tn), lambda n, m: (m, _blk(cfg.ga0, tn) + n)),
                  pl.BlockSpec((tm, tn), lambda n, m: (m, _blk(cfg.gs0, tn) + n))],
        out_specs=[tile, tile, tile],
        out_shape=[jax.ShapeDtypeStruct((S, D), BF16)] * 3,
        compiler_params=_cp(("parallel", "parallel")),
    )(o_a, w_attn, y_n, w_ssm, proj, proj)


def merge_bwd(dout, w_out, a_out, s_out, proj, cfg, name, tm=512, tn=1024):
    S, D = cfg.S, cfg.D
    tm, tn = min(tm, S), min(tn, D)

    def body(do_ref, wo_ref, a_ref, s_ref, ga_ref, gs_ref, da_ref, ds_ref, dga_ref, dgs_ref):
        dmv = _dot(do_ref[...].astype(BF16), wo_ref[...], NT)
        sa = _sigmoid(ga_ref[...])
        ss = _sigmoid(gs_ref[...])
        da_ref[...] = (dmv * sa).astype(BF16)
        ds_ref[...] = (dmv * ss).astype(BF16)
        dga_ref[...] = (dmv * a_ref[...] * (sa * (1.0 - sa))).astype(BF16)
        dgs_ref[...] = (dmv * s_ref[...] * (ss * (1.0 - ss))).astype(BF16)

    tile = pl.BlockSpec((tm, tn), lambda n, m: (m, n))
    sh = jax.ShapeDtypeStruct((S, D), BF16)
    return pl.pallas_call(
        body, name=name, grid=(D // tn, S // tm),
        in_specs=[pl.BlockSpec((tm, D), lambda n, m: (m, 0)), pl.BlockSpec((tn, D), lambda n, m: (n, 0)), tile, tile,
                  pl.BlockSpec((tm, tn), lambda n, m: (m, _blk(cfg.ga0, tn) + n)),
                  pl.BlockSpec((tm, tn), lambda n, m: (m, _blk(cfg.gs0, tn) + n))],
        out_specs=[tile] * 4, out_shape=[sh] * 4,
        compiler_params=_cp(("parallel", "parallel")),
    )(dout, w_out, a_out, s_out, proj, proj)


def _attn_rows(base, d):
    return pl.ds(base, ATTN_BLOCK) if d == 1 else pl.ds(base, ATTN_BLOCK, stride=d)


def _attn_units(cfg):
    dmax = max(d for _, d in cfg.patterns)
    units = []
    for p, (window, d) in enumerate(cfg.patterns):
        assert window // d == ATTN_BLOCK and dmax % d == 0
        nsub = dmax // d
        for b in range(nsub):
            for r in range(d):
                base = b * ATTN_BLOCK * d + r
                if b > 0:
                    units.append((p, d, base, (b - 1) * ATTN_BLOCK * d + r, False))
                else:
                    units.append((p, d, base, (nsub - 1) * ATTN_BLOCK * d + r, True))
    return units, ATTN_BLOCK * dmax


def _set_bias_tiles(bias_s, slope, cfg):
    qi = lax.broadcasted_iota(jnp.int32, (ATTN_BLOCK, ATTN_BLOCK), 0)
    ki = lax.broadcasted_iota(jnp.int32, (ATTN_BLOCK, ATTN_BLOCK), 1)
    for p, (_, d) in enumerate(cfg.patterns):
        bias_s[2 * p] = jnp.where(ki >= qi, (-slope) * ((ATTN_BLOCK + qi - ki) * d).astype(F32), NEG)
        bias_s[2 * p + 1] = jnp.where(ki <= qi, (-slope) * ((qi - ki) * d).astype(F32), NEG)


def _unit_scores(q, kcat, bias_s, p, prev_ok, scale):
    s = _dot(q, kcat, NT) * scale + jnp.concatenate([bias_s[2 * p], bias_s[2 * p + 1]], axis=1)
    if prev_ok is not None:
        cur_half = lax.broadcasted_iota(jnp.int32, s.shape, 1) >= ATTN_BLOCK
        s = jnp.where(jnp.logical_or(cur_half, prev_ok), s, NEG)
    return s


def _slope_table(cfg):
    slopes = jnp.asarray([2.0 ** (-8.0 * (h + 1) / cfg.AH) for h in range(cfg.AH)], F32)
    return jnp.broadcast_to(slopes.reshape(cfg.AH, 1, 1), (cfg.AH, 8, LANES))


def attn_fused_fwd(proj, slopes, cfg, name):
    S, E, AH = cfg.S, cfg.E, cfg.AH
    units, SB = _attn_units(cfg)
    assert S % SB == 0
    npat = len(cfg.patterns)
    scale = E ** -0.5

    def spec(off, prev):
        c0 = _blk(off, E)
        if prev:
            return pl.BlockSpec((SB, E), lambda h, i: (jnp.maximum(i - 1, 0), c0 + h))
        return pl.BlockSpec((SB, E), lambda h, i: (i, c0 + h))

    def body(q_ref, kp_ref, kc_ref, vp_ref, vc_ref, z_ref, sl_ref, oa_ref, om_ref, lt_ref, *scr):
        o_s, l_s, bias_s = scr[:npat], scr[npat:2 * npat], scr[2 * npat]
        i = pl.program_id(1)

        @pl.when(i == 0)
        def _():
            _set_bias_tiles(bias_s, sl_ref[0, 0:1, :], cfg)

        for p, d, base, pbase, from_prev in units:
            rows, prows = _attn_rows(base, d), _attn_rows(pbase, d)
            q = q_ref[rows, :].astype(BF16)
            kp = (kp_ref if from_prev else kc_ref)[prows, :].astype(BF16)
            vp = (vp_ref if from_prev else vc_ref)[prows, :].astype(BF16)
            kcat = jnp.concatenate([kp, kc_ref[rows, :].astype(BF16)], axis=0)
            vcat = jnp.concatenate([vp, vc_ref[rows, :].astype(BF16)], axis=0)
            s = _unit_scores(q, kcat, bias_s, p, (i > 0) if from_prev else None, scale)
            m = jnp.max(s, axis=1, keepdims=True)
            pr = jnp.exp(s - m)
            l = jnp.sum(pr, axis=1, keepdims=True)
            o_s[p][rows, :] = _dot(pr.astype(BF16), vcat, NN) * (1.0 / l)
            l_s[p][rows, :] = m + jnp.log(l)
        ls = [l_s[p][...] for p in range(npat)]
        m = functools.reduce(jnp.maximum, ls)
        lt = m + jnp.log(sum(jnp.exp(l_ - m) for l_ in ls))
        lt_ref[...] = lt
        mix = sum(jnp.exp(ls[p] - lt) * o_s[p][...] for p in range(npat))
        om_ref[...] = mix
        z = z_ref[...]
        oa_ref[...] = (mix * (z * _sigmoid(z))).astype(BF16)

    out = pl.BlockSpec((SB, E), lambda h, i: (i, h))
    return pl.pallas_call(
        body, name=name, grid=(AH, S // SB),
        in_specs=[spec(0, False), spec(cfg.k0, True), spec(cfg.k0, False), spec(cfg.v0, True), spec(cfg.v0, False),
                  spec(cfg.za0, False), pl.BlockSpec((1, 8, LANES), lambda h, i: (h, 0, 0))],
        out_specs=[out, out, pl.BlockSpec((SB, 1), lambda h, i: (h * (S // SB) + i, 0))],
        out_shape=[jax.ShapeDtypeStruct((S, cfg.AW), BF16), jax.ShapeDtypeStruct((S, cfg.AW), F32),
                   jax.ShapeDtypeStruct((AH * S, 1), F32)],
        scratch_shapes=[pltpu.VMEM((SB, E), F32)] * npat + [pltpu.VMEM((SB, 1), F32)] * npat +
        [pltpu.VMEM((2 * npat, ATTN_BLOCK, ATTN_BLOCK), F32)],
        compiler_params=_cp(("parallel", "arbitrary")),
    )(proj, proj, proj, proj, proj, proj, slopes)


def attn_fused_bwd(proj, do_a, o_mix, ltot, slopes, cfg, name):
    S, E, AH = cfg.S, cfg.E, cfg.AH
    units, SB = _attn_units(cfg)
    nsb = S // SB
    last = nsb - 1
    scale = E ** -0.5

    def spec(off, prev):
        c0 = _blk(off, E)
        if prev:
            return pl.BlockSpec((SB, E), lambda h, i: (jnp.maximum(i - 1, 0), c0 + h))
        return pl.BlockSpec((SB, E), lambda h, i: (jnp.minimum(i, last), c0 + h))

    cur = pl.BlockSpec((SB, E), lambda h, i: (jnp.minimum(i, last), h))
    prev = pl.BlockSpec((SB, E), lambda h, i: (jnp.maximum(i - 1, 0), h))

    def body(q_ref, kp_ref, kc_ref, vp_ref, vc_ref, z_ref, doa_ref, om_ref, lt_ref, sl_ref,
             dq_ref, dk_ref, dv_ref, dz_ref, dmix_s, dl_s, dq_s, dkp_s, dvp_s, dkc_s, dvc_s, bias_s):
        i = pl.program_id(1)

        @pl.when(i == 0)
        def _():
            dkc_s[...] = jnp.zeros_like(dkc_s)
            dvc_s[...] = jnp.zeros_like(dvc_s)
            _set_bias_tiles(bias_s, sl_ref[0, 0:1, :], cfg)

        @pl.when(i < nsb)
        def _():
            z = z_ref[...]
            s = _sigmoid(z)
            doa = doa_ref[...].astype(F32)
            om = om_ref[...]
            dmix = doa * (z * s)
            dmix_s[...] = dmix
            dz_ref[...] = (doa * om * (s * (1.0 + z * (1.0 - s)))).astype(BF16)
            dl_s[...] = jnp.sum(dmix * om, axis=1, keepdims=True)
            dkp_s[...] = dkc_s[...]
            dvp_s[...] = dvc_s[...]
            dkc_s[...] = jnp.zeros_like(dkc_s)
            dvc_s[...] = jnp.zeros_like(dvc_s)
            dq_s[...] = jnp.zeros_like(dq_s)
            for p, d, base, pbase, from_prev in units:
                rows, prows = _attn_rows(base, d), _attn_rows(pbase, d)
                q = q_ref[rows, :].astype(BF16)
                kc = kc_ref[rows, :].astype(BF16)
                kp = (kp_ref if from_prev else kc_ref)[prows, :].astype(BF16)
                vp = (vp_ref if from_prev else vc_ref)[prows, :].astype(BF16)
                do = dmix_s[rows, :].astype(BF16)
                lt = lt_ref[rows, :]
                dlt = dl_s[rows, :]
                kcat = jnp.concatenate([kp, kc], axis=0)
                vcat = jnp.concatenate([vp, vc_ref[rows, :].astype(BF16)], axis=0)
                pr = jnp.exp(_unit_scores(q, kcat, bias_s, p, (i > 0) if from_prev else None, scale) - lt)
                ds = (pr * (_dot(do, vcat, NT) - dlt) * scale).astype(BF16)
                dq_s[rows, :] += _dot(ds, kcat, NN)
                dkcat = _dot(ds, q, TN)
                dvcat = _dot(pr.astype(BF16), do, TN)
                dk_t, dv_t = (dkp_s, dvp_s) if from_prev else (dkc_s, dvc_s)
                dk_t[prows, :] += dkcat[:ATTN_BLOCK, :]
                dv_t[prows, :] += dvcat[:ATTN_BLOCK, :]
                dkc_s[rows, :] += dkcat[ATTN_BLOCK:, :]
                dvc_s[rows, :] += dvcat[ATTN_BLOCK:, :]
            dq_ref[...] = dq_s[...].astype(BF16)
            dk_ref[...] = dkp_s[...].astype(BF16)
            dv_ref[...] = dvp_s[...].astype(BF16)

        @pl.when(i == nsb)
        def _():
            dk_ref[...] = dkc_s[...].astype(BF16)
            dv_ref[...] = dvc_s[...].astype(BF16)

    sh = jax.ShapeDtypeStruct((S, cfg.AW), BF16)
    acc = pltpu.VMEM((SB, E), F32)
    return pl.pallas_call(
        body, name=name, grid=(AH, nsb + 1),
        in_specs=[spec(0, False), spec(cfg.k0, True), spec(cfg.k0, False), spec(cfg.v0, True), spec(cfg.v0, False),
                  spec(cfg.za0, False), cur, cur,
                  pl.BlockSpec((SB, 1), lambda h, i: (h * nsb + jnp.minimum(i, last), 0)),
                  pl.BlockSpec((1, 8, LANES), lambda h, i: (h, 0, 0))],
        out_specs=[cur, prev, prev, cur], out_shape=[sh] * 4,
        scratch_shapes=[acc, pltpu.VMEM((SB, 1), F32), acc, acc, acc, acc, acc,
                        pltpu.VMEM((2 * len(cfg.patterns), ATTN_BLOCK, ATTN_BLOCK), F32)],
        compiler_params=_cp(("parallel", "arbitrary")),
    )(proj, proj, proj, proj, proj, proj, do_a, o_mix, ltot, slopes)


HALO = 8


def _conv_taps(x_ref, h_ref, kc):
    x = x_ref[...]
    full = jnp.concatenate([jnp.where(pl.program_id(1) == 0, 0.0, h_ref[...]), x], axis=0)
    return [pltpu.roll(full, s, axis=0)[HALO:, :] for s in range(kc - 1, 0, -1)] + [x]


def _conv_pre(taps, w_ref, b_ref):
    pre = b_ref[...] + w_ref[0:1, :] * taps[0]
    for k in range(1, len(taps)):
        pre = pre + w_ref[k:k + 1, :] * taps[k]
    return pre


def conv_fwd(proj, w, b, cfg, name, tm=1024, tc=512):
    S, CD, KC = cfg.S, cfg.CD, cfg.KC
    tc = min(tc, CD)
    c0 = _blk(cfg.xbc0, tc)
    hb = tm // HALO

    def body(x_ref, h_ref, w_ref, b_ref, o_ref):
        pre = _conv_pre(_conv_taps(x_ref, h_ref, KC), w_ref, b_ref)
        o_ref[...] = pre * _sigmoid(pre)

    return pl.pallas_call(
        body, name=name, grid=(CD // tc, S // tm),
        in_specs=[pl.BlockSpec((tm, tc), lambda c, i: (i, c0 + c)),
                  pl.BlockSpec((HALO, tc), lambda c, i: (jnp.maximum(i * hb - 1, 0), c0 + c)),
                  pl.BlockSpec((KC, tc), lambda c, i: (0, c)),
                  pl.BlockSpec((1, tc), lambda c, i: (0, c))],
        out_specs=pl.BlockSpec((tm, tc), lambda c, i: (i, c)),
        out_shape=jax.ShapeDtypeStruct((S, CD), F32),
        compiler_params=_cp(("parallel", "arbitrary")),
    )(proj, proj, w, b)


def conv_bwd_a(proj, dxc, w, b, cfg, name, c_off, tm=1024, tc=512):
    S, KC = cfg.S, cfg.KC
    CD = dxc.shape[1]
    tc = min(tc, CD)
    c0 = _blk(cfg.xbc0 + c_off, tc)
    w0 = _blk(c_off, tc)
    hb = tm // HALO

    def body(x_ref, h_ref, d_ref, w_ref, b_ref, dp_ref, gw_ref, gb_ref):
        taps = _conv_taps(x_ref, h_ref, KC)
        pre = _conv_pre(taps, w_ref, b_ref)
        s = _sigmoid(pre)
        dpre = d_ref[...] * (s * (1.0 + pre * (1.0 - s)))
        dp_ref[...] = dpre
        gb = jnp.sum(dpre, axis=0, keepdims=True)
        gws = [jnp.sum(dpre * taps[k], axis=0, keepdims=True) for k in range(KC)]
        gw = jnp.concatenate(gws + [jnp.zeros((8 - KC, tc), F32)], axis=0)

        @pl.when(pl.program_id(1) == 0)
        def _():
            gw_ref[...] = gw
            gb_ref[...] = gb

        @pl.when(pl.program_id(1) > 0)
        def _():
            gw_ref[...] += gw
            gb_ref[...] += gb

    return pl.pallas_call(
        body, name=name, grid=(CD // tc, S // tm),
        in_specs=[pl.BlockSpec((tm, tc), lambda c, i: (i, c0 + c)),
                  pl.BlockSpec((HALO, tc), lambda c, i: (jnp.maximum(i * hb - 1, 0), c0 + c)),
                  pl.BlockSpec((tm, tc), lambda c, i: (i, c)),
                  pl.BlockSpec((KC, tc), lambda c, i: (0, w0 + c)),
                  pl.BlockSpec((1, tc), lambda c, i: (0, w0 + c))],
        out_specs=[pl.BlockSpec((tm, tc), lambda c, i: (i, c)),
                   pl.BlockSpec((8, tc), lambda c, i: (0, c)),
                   pl.BlockSpec((1, tc), lambda c, i: (0, c))],
        out_shape=[jax.ShapeDtypeStruct((S, CD), F32), jax.ShapeDtypeStruct((8, CD), F32),
                   jax.ShapeDtypeStruct((1, CD), F32)],
        compiler_params=_cp(("parallel", "arbitrary")),
    )(proj, proj, dxc, w, b)


def conv_bwd_b(dpre, w, cfg, name, c_off, tm=1024, tc=512):
    S, KC = cfg.S, cfg.KC
    CD = dpre.shape[1]
    tc = min(tc, CD)
    w0 = _blk(c_off, tc)
    hb = tm // HALO
    nrb = S // tm
    last_h = S // HALO - 1

    def body(d_ref, h_ref, w_ref, o_ref):
        d = d_ref[...]
        full = jnp.concatenate([d, jnp.where(pl.program_id(1) == nrb - 1, 0.0, h_ref[...])], axis=0)
        acc = w_ref[KC - 1:KC, :] * d
        for j in range(1, KC):
            acc = acc + w_ref[KC - 1 - j:KC - j, :] * pltpu.roll(full, tm + HALO - j, axis=0)[:tm, :]
        o_ref[...] = acc.astype(BF16)

    return pl.pallas_call(
        body, name=name, grid=(CD // tc, nrb),
        in_specs=[pl.BlockSpec((tm, tc), lambda c, i: (i, c)),
                  pl.BlockSpec((HALO, tc), lambda c, i: (jnp.minimum((i + 1) * hb, last_h), c)),
                  pl.BlockSpec((KC, tc), lambda c, i: (0, w0 + c))],
        out_specs=pl.BlockSpec((tm, tc), lambda c, i: (i, c)),
        out_shape=jax.ShapeDtypeStruct((S, CD), BF16),
        compiler_params=_cp(("parallel", "arbitrary")),
    )(dpre, dpre, w)


def _pad_lanes(v, width=LANES):
    return jnp.pad(v, ((0, 0), (0, width - v.shape[1])))


def ssd_prep(dt_raw, dt_bias, a_log, cfg, name):
    S, L = cfg.S, cfg.L

    def body(x_ref, b_ref, al_ref, dt_ref, ac_ref):
        x = x_ref[...] + b_ref[...]
        dt = jnp.maximum(x, 0.0) + jnp.log(1.0 + jnp.exp(-jnp.abs(x)))
        da = dt * (-jnp.exp(al_ref[...]))
        li = lax.broadcasted_iota(jnp.int32, (L, L), 0)
        si = lax.broadcasted_iota(jnp.int32, (L, L), 1)
        tri = jnp.where(li >= si, 1.0, 0.0).astype(F32)
        dt_ref[...] = dt
        ac_ref[...] = lax.dot_general(tri, da, ((NN), ((), ())), precision=lax.Precision.HIGHEST,
                                      preferred_element_type=F32)

    row = pl.BlockSpec((L, LANES), lambda i: (i, 0))
    vec = pl.BlockSpec((1, LANES), lambda i: (0, 0))
    sh = jax.ShapeDtypeStruct((S, LANES), F32)
    return pl.pallas_call(
        body, name=name, grid=(S // L,), in_specs=[row, vec, vec], out_specs=[row, row], out_shape=[sh, sh],
        compiler_params=_cp(("parallel",)),
    )(dt_raw, dt_bias, a_log)


def _spread(v, n):
    return jnp.broadcast_to(v[:, :, None], v.shape + (n,)).reshape(v.shape[0], v.shape[1] * n)


def _head_selectors(cfg):
    def sel(width):
        head = jnp.arange(LANES)[None, :, None]
        slot = jnp.arange(cfg.SG)[:, None, None] * cfg.HPG + (jnp.arange(cfg.HPG * width) // width)[None, None, :]
        return (head == slot).astype(BF16)
    return sel(cfg.P), sel(LANES)


def _spread_heads(v, sel):
    n = v.shape[0]
    hi = v.astype(BF16)
    r1 = v - hi.astype(F32)
    mid = r1.astype(BF16)
    lo = (r1 - mid.astype(F32)).astype(BF16)
    out = _dot(jnp.concatenate([hi, mid, lo], axis=0), sel, NN)
    return out[:n] + out[n:2 * n] + out[2 * n:]


def _pair_lanes(wide, hpg, p):
    low = lax.broadcasted_iota(jnp.int32, (wide.shape[0], LANES), 1) < p
    return jnp.concatenate([jnp.where(low, wide[:, 2 * jp * LANES:(2 * jp + 1) * LANES],
                                      wide[:, (2 * jp + 1) * LANES:(2 * jp + 2) * LANES])
                            for jp in range(hpg // 2)], axis=1)


def _pair_select(halves, p):
    low = lax.broadcasted_iota(jnp.int32, halves[0].shape, 1) < p
    return jnp.where(low, halves[0], halves[1])


def _head_rows(row, hpg, p):
    return jnp.concatenate([jnp.broadcast_to(row[:, j * LANES:(j + 1) * LANES], (p, LANES)) for j in range(hpg)],
                           axis=0)


def _segment_sums(t, sel):
    r = t.shape[0]
    hi = t.astype(BF16)
    out = _dot(jnp.concatenate([hi, (t - hi.astype(F32)).astype(BF16)], axis=0), sel, NT)
    return out[:r] + out[r:]


def ssd_scan_fwd(xc, dt, acum, act, sel_p, sel_l, cfg, name):
    S, L, P, SN, HPG, SG, SI = cfg.S, cfg.L, cfg.P, cfg.SN, cfg.HPG, cfg.SG, cfg.SI
    nc = S // L
    GW = HPG * P
    bcol, ccol = _blk(SI, SN), _blk(SI + cfg.GN, SN)
    assert nc % CPS == 0

    def body(xs_ref, b_ref, c_ref, dtn_ref, acn_ref, at_ref, sp_ref, sl_ref, y_ref, st_ref, st):
        @pl.when(pl.program_id(1) == 0)
        def _():
            st[...] = jnp.zeros_like(st)

        causal = lax.broadcasted_iota(jnp.int32, (L, L), 0) >= lax.broadcasted_iota(jnp.int32, (L, L), 1)
        for ci in range(CPS):
            rows = slice(ci * L, (ci + 1) * L)
            acn = acn_ref[rows, :]
            dts = _spread_heads(dtn_ref[rows, :], sp_ref[0])
            acs = _spread_heads(acn, sl_ref[0])
            a_p = _pair_lanes(acs, HPG, P)
            s0 = st[...]
            st_ref[ci] = s0.reshape(HPG, P, SN)
            B = b_ref[rows, :].astype(BF16)
            C = c_ref[rows, :].astype(BF16)
            G = _dot(C, B, NT)
            xdt = xs_ref[rows, :] * dts
            xdtb = xdt.astype(BF16)
            ws = jnp.exp(a_p[L - 1:L, :] - a_p)
            yo = jnp.exp(a_p) * _dot(C, s0.astype(BF16), NT)
            yd = []
            for jp in range(HPG // 2):
                x_pair = xdtb[:, jp * LANES:(jp + 1) * LANES]
                halves = []
                for j in (2 * jp, 2 * jp + 1):
                    dm = jnp.where(causal, jnp.exp(acs[:, j * LANES:(j + 1) * LANES] - at_ref[j:j + 1, rows]), 0.0)
                    halves.append(_dot((G * dm).astype(BF16), x_pair, NN))
                yd.append(_pair_select(halves, P))
            y_ref[rows, :] = jnp.concatenate(yd, axis=1) + yo
            st[...] = _head_rows(jnp.exp(acs[L - 1:L, :]), HPG, P) * s0 + _dot((xdt * ws).astype(BF16), B, TN)

    R = CPS * L
    y, states = pl.pallas_call(
        body, name=name, grid=(SG, nc // CPS),
        in_specs=[pl.BlockSpec((R, GW), lambda g, c: (c, g)),
                  pl.BlockSpec((R, SN), lambda g, c: (c, bcol + g)),
                  pl.BlockSpec((R, SN), lambda g, c: (c, ccol + g)),
                  pl.BlockSpec((R, LANES), lambda g, c: (c, 0)),
                  pl.BlockSpec((R, LANES), lambda g, c: (c, 0)),
                  pl.BlockSpec((HPG, R), lambda g, c: (g, c)),
                  pl.BlockSpec((1, LANES, GW), lambda g, c: (g, 0, 0)),
                  pl.BlockSpec((1, LANES, HPG * LANES), lambda g, c: (g, 0, 0))],
        out_specs=[pl.BlockSpec((R, GW), lambda g, c: (c, g)),
                   pl.BlockSpec((CPS, HPG, P, SN), lambda g, c: (c, g, 0, 0))],
        out_shape=[jax.ShapeDtypeStruct((S, SI), F32), jax.ShapeDtypeStruct((nc, cfg.SH, P, SN), F32)],
        scratch_shapes=[pltpu.VMEM((GW, SN), F32)],
        compiler_params=_cp(("parallel", "arbitrary")),
    )(xc, xc, xc, dt, acum, act, sel_p, sel_l)
    return y, states


def ssd_scan_bwd(xc, dt, acum, act, sel_p, sel_l, states, y, dy, dvec, cfg, name, side):
    S, L, P, SN, HPG, SG, SI = cfg.S, cfg.L, cfg.P, cfg.SN, cfg.HPG, cfg.SG, cfg.SI
    nc = S // L
    GW = HPG * P
    bcol, ccol = _blk(SI, SN), _blk(SI + cfg.GN, SN)

    def rc(c):
        return nc // CPS - 1 - c

    def body(xs_ref, b_ref, c_ref, dtn_ref, acn_ref, at_ref, sp_ref, sl_ref, st_ref, y_ref, dy_ref, dk_ref,
             dxs_ref, db_ref, dc_ref, dac_ref, dxsum_ref, dst):
        @pl.when(pl.program_id(1) == 0)
        def _():
            dst[...] = jnp.zeros_like(dst)

        sel = sp_ref[0]
        causal = lax.broadcasted_iota(jnp.int32, (L, L), 0) >= lax.broadcasted_iota(jnp.int32, (L, L), 1)
        low = lax.broadcasted_iota(jnp.int32, (L, LANES), 1) < P
        is_last = lax.broadcasted_iota(jnp.int32, (L, LANES), 0) == L - 1
        ones = jnp.ones((16, SN), BF16)
        for ci in reversed(range(CPS)):
            rows = slice(ci * L, (ci + 1) * L)
            acn = acn_ref[rows, :]
            dts = _spread_heads(dtn_ref[rows, :], sel)
            acs = _spread_heads(acn, sl_ref[0])
            a_p = _pair_lanes(acs, HPG, P)
            B = b_ref[rows, :].astype(BF16)
            C = c_ref[rows, :].astype(BF16)
            G = _dot(C, B, NT)
            xs = xs_ref[rows, :]
            dY = dy_ref[rows, :].astype(F32)
            xdt = xs * dts
            xdtb = xdt.astype(BF16)
            dYb = dY.astype(BF16)
            s0 = st_ref[ci].reshape(GW, SN)
            s0b = s0.astype(BF16)
            ds1 = dst[...]
            ds1b = ds1.astype(BF16)
            ws = jnp.exp(a_p[L - 1:L, :] - a_p)
            dR = (jnp.exp(a_p) * dY).astype(BF16)
            dX2 = ws * _dot(B, ds1b, NT)
            dgsum = jnp.zeros((L, L), F32)
            dX1, yd = [], []
            for jp in range(HPG // 2):
                lanes = slice(jp * LANES, (jp + 1) * LANES)
                x_pair, dy_pair = xdtb[:, lanes], dYb[:, lanes]
                h1, h2 = [], []
                for h, j in enumerate((2 * jp, 2 * jp + 1)):
                    dm = jnp.where(causal, jnp.exp(acs[:, j * LANES:(j + 1) * LANES] - at_ref[j:j + 1, rows]), 0.0)
                    mine = low if h == 0 else jnp.logical_not(low)
                    dgsum = dgsum + _dot(jnp.where(mine, dy_pair, jnp.zeros_like(dy_pair)), x_pair, NT) * dm
                    Mb = (G * dm).astype(BF16)
                    h1.append(_dot(Mb, dy_pair, TN))
                    h2.append(_dot(Mb, x_pair, NN))
                dX1.append(_pair_select(h1, P))
                yd.append(_pair_select(h2, P))
            dX1 = jnp.concatenate(dX1, axis=1)
            dX = dX1 + dX2
            pair = (dYb.astype(F32) - dY) * jnp.concatenate(yd, axis=1) - xdtb.astype(F32) * dX1
            through = _segment_sums(xdt * dX2, sel)
            u = ds1 * s0
            u_hi = u.astype(BF16)
            u_rows = _dot(ones, jnp.concatenate([u_hi, (u - u_hi.astype(F32)).astype(BF16)], axis=0), NT)
            u_rows = u_rows[:, :GW] + u_rows[:, GW:]
            at_end = jnp.exp(acn[L - 1:L, :]) * _segment_sums(u_rows, sel)[0:1, :] + \
                jnp.sum(through, axis=0, keepdims=True)
            dac_ref[0, rows, :] = _segment_sums(dY * y_ref[rows, :] + pair, sel) - through + \
                jnp.where(is_last, at_end, 0.0)
            dxsum_ref[0, rows, :] = _segment_sums(dX * xs, sel)
            dxs_ref[rows, :] = dX * dts + dk_ref[...] * dY
            dst[...] = _head_rows(jnp.exp(acs[L - 1:L, :]), HPG, P) * ds1 + _dot(dR, C, TN)
            dgb = dgsum.astype(BF16)
            dc_ref[rows, :] = _dot(dR, s0b, NN) + _dot(dgb, B, NN)
            db_ref[rows, :] = _dot((xdt * ws).astype(BF16), ds1b, NN) + _dot(dgb, C, TN)

    R = CPS * L
    wide = pl.BlockSpec((R, GW), lambda g, c: (rc(c), g))
    colspec = pl.BlockSpec((1, R, LANES), lambda g, c: (g, rc(c), 0))
    whole = pl.BlockSpec(memory_space=pl.ANY)
    arrs, gathers = side
    grid = (SG, nc // CPS)
    res = pl.pallas_call(
        with_exchange(body, 12, 5, gathers, grid), name=name, grid=grid,
        in_specs=[wide,
                  pl.BlockSpec((R, SN), lambda g, c: (rc(c), bcol + g)),
                  pl.BlockSpec((R, SN), lambda g, c: (rc(c), ccol + g)),
                  pl.BlockSpec((R, LANES), lambda g, c: (rc(c), 0)),
                  pl.BlockSpec((R, LANES), lambda g, c: (rc(c), 0)),
                  pl.BlockSpec((HPG, R), lambda g, c: (g, rc(c))),
                  pl.BlockSpec((1, LANES, GW), lambda g, c: (g, 0, 0)),
                  pl.BlockSpec((1, LANES, HPG * LANES), lambda g, c: (g, 0, 0)),
                  pl.BlockSpec((CPS, HPG, P, SN), lambda g, c: (rc(c), g, 0, 0)),
                  wide, wide,
                  pl.BlockSpec((1, GW), lambda g, c: (0, g))] + [whole] * len(arrs),
        out_specs=[wide,
                   pl.BlockSpec((R, SN), lambda g, c: (rc(c), g)),
                   pl.BlockSpec((R, SN), lambda g, c: (rc(c), g)),
                   colspec, colspec] + [whole] * len(arrs),
        out_shape=[jax.ShapeDtypeStruct((S, SI), F32), jax.ShapeDtypeStruct((S, cfg.GN), F32),
                   jax.ShapeDtypeStruct((S, cfg.GN), F32),
                   jax.ShapeDtypeStruct((SG, S, LANES), F32), jax.ShapeDtypeStruct((SG, S, LANES), F32)] +
        _exchange_shapes(arrs, gathers),
        scratch_shapes=[pltpu.VMEM((GW, SN), F32)] + _exchange_sems(len(arrs)),
        compiler_params=_cp(("arbitrary", "arbitrary")),
    )(xc, xc, xc, dt, acum, act, sel_p, sel_l, states, y, dy, dvec, *arrs)
    return res[:5], res[5:]


def dt_bwd(dac, dxsum, dt_raw, dt, dt_bias, a_log, cfg, name):
    S, L, SG = cfg.S, cfg.L, cfg.SG

    def body(da_ref, dx_ref, x_ref, dt_ref, b_ref, al_ref, o_ref, gb_ref, ga_ref):
        a = -jnp.exp(al_ref[...])
        dtv = dt_ref[...]
        dxs = jnp.sum(dx_ref[...], axis=0)
        upper = jnp.where(lax.broadcasted_iota(jnp.int32, (L, L), 1) >= lax.broadcasted_iota(jnp.int32, (L, L), 0),
                          1.0, 0.0).astype(F32)
        dda = lax.dot_general(upper, jnp.sum(da_ref[...], axis=0), (NN, ((), ())), precision=lax.Precision.HIGHEST,
                              preferred_element_type=F32)
        draw = (dxs + dda * a) * _sigmoid(x_ref[...] + b_ref[...])
        o_ref[...] = draw.astype(BF16)
        gb = jnp.sum(draw, axis=0, keepdims=True)
        ga = jnp.sum(dda * dtv, axis=0, keepdims=True) * a

        @pl.when(pl.program_id(0) == 0)
        def _():
            gb_ref[...] = gb
            ga_ref[...] = ga

        @pl.when(pl.program_id(0) > 0)
        def _():
            gb_ref[...] += gb
            ga_ref[...] += ga

    row = pl.BlockSpec((L, LANES), lambda i: (i, 0))
    vec = pl.BlockSpec((1, LANES), lambda i: (0, 0))
    return pl.pallas_call(
        body, name=name, grid=(S // L,),
        in_specs=[pl.BlockSpec((SG, L, LANES), lambda i: (0, i, 0))] * 2 + [row, row, vec, vec],
        out_specs=[row, vec, vec],
        out_shape=[jax.ShapeDtypeStruct((S, LANES), BF16), jax.ShapeDtypeStruct((1, LANES), F32),
                   jax.ShapeDtypeStruct((1, LANES), F32)],
        compiler_params=_cp(("arbitrary",)),
    )(dac, dxsum, dt_raw, dt, dt_bias, a_log)


def gated_norm_fwd(y, xc, proj, dvec, nw, cfg, name, tm=128):
    S, SI = cfg.S, cfg.SI

    def body(y_ref, xs_ref, z_ref, d_ref, w_ref, o_ref):
        z = z_ref[...]
        yg = (y_ref[...] + d_ref[...] * xs_ref[...]) * (z * _sigmoid(z))
        r = lax.rsqrt(jnp.mean(yg * yg, axis=-1, keepdims=True) + RMS_EPS)
        o_ref[...] = ((yg * r) * w_ref[...]).astype(BF16)

    row = pl.BlockSpec((tm, SI), lambda i: (i, 0))
    vec = pl.BlockSpec((1, SI), lambda i: (0, 0))
    return pl.pallas_call(
        body, name=name, grid=(S // tm,),
        in_specs=[row, row, pl.BlockSpec((tm, SI), lambda i: (i, _blk(cfg.zs0, SI))), vec, vec],
        out_specs=row, out_shape=jax.ShapeDtypeStruct((S, SI), BF16),
        compiler_params=_cp(("parallel",)),
    )(y, xc, proj, dvec, nw)


def gated_norm_bwd(dyn, y, xc, proj, dvec, nw, cfg, name, tm=128):
    S, SI = cfg.S, cfg.SI

    def body(dn_ref, y_ref, xs_ref, z_ref, d_ref, w_ref, dy_ref, dz_ref, gw_ref, gd_ref):
        z = z_ref[...]
        s = _sigmoid(z)
        sz = z * s
        xs = xs_ref[...]
        yf = y_ref[...] + d_ref[...] * xs
        yg = yf * sz
        r = lax.rsqrt(jnp.mean(yg * yg, axis=-1, keepdims=True) + RMS_EPS)
        dn = dn_ref[...].astype(F32)
        g = dn * w_ref[...]
        dyg = r * g - yg * (r * r * r) * jnp.mean(g * yg, axis=-1, keepdims=True)
        dy = dyg * sz
        dy_ref[...] = dy.astype(BF16)
        dz_ref[...] = (dyg * yf * (s * (1.0 + z * (1.0 - s)))).astype(BF16)
        gw = jnp.sum(dn * (yg * r), axis=0, keepdims=True)
        gd = jnp.sum(dy * xs, axis=0, keepdims=True)

        @pl.when(pl.program_id(0) == 0)
        def _():
            gw_ref[...] = gw
            gd_ref[...] = gd

        @pl.when(pl.program_id(0) > 0)
        def _():
            gw_ref[...] += gw
            gd_ref[...] += gd

    row = pl.BlockSpec((tm, SI), lambda i: (i, 0))
    vec = pl.BlockSpec((1, SI), lambda i: (0, 0))
    return pl.pallas_call(
        body, name=name, grid=(S // tm,),
        in_specs=[row, row, row, pl.BlockSpec((tm, SI), lambda i: (i, _blk(cfg.zs0, SI))), vec, vec],
        out_specs=[row, row, vec, vec],
        out_shape=[jax.ShapeDtypeStruct((S, SI), BF16), jax.ShapeDtypeStruct((S, SI), BF16),
                   jax.ShapeDtypeStruct((1, SI), F32), jax.ShapeDtypeStruct((1, SI), F32)],
        compiler_params=_cp(("arbitrary",)),
    )(dyn, y, xc, proj, dvec, nw)


def _shard_columns(cfg, main, dt):
    dt0 = 4 * cfg.AW + cfg.SI + cfg.CD
    ws = cfg.N_IN // N_DEV
    out = []
    for k in range(N_DEV):
        lo, hi, parts = k * ws, (k + 1) * ws, []
        if lo < dt0:
            parts.append((main, lo, min(hi, dt0)))
        if lo < dt0 + cfg.SH and hi > dt0:
            parts.append((dt, max(lo, dt0) - dt0, min(hi, dt0 + cfg.SH) - dt0))
        if hi > dt0 + cfg.SH:
            parts.append((main, max(lo, dt0 + cfg.SH) - cfg.SH, hi - cfg.SH))
        out.append(parts)
    return out


def local_step(cfg, x, tgt, norm_w, conv_w, conv_b, dt_bias, a_log, d_skip, ssm_norm_w, final_norm_w,
               w_main, w_dt, shards, dt0):
    S, D = cfg.S, cfg.D
    slopes = _slope_table(cfg)
    dt_bias_p = _pad_lanes(dt_bias)
    a_log_p = _pad_lanes(a_log)
    dvec = _spread(d_skip, cfg.P)

    hn = rmsnorm_fwd(x, norm_w, "rmsnorm_fwd")
    proj, gathered = matmul(hn, w_main, 'nn', 1024, 2048, 2048, F32, "in_proj", side=(shards, [True] * 3))
    w_attn, w_ssm, w_out = gathered[0].reshape(cfg.AW, D), gathered[1].reshape(cfg.SI, D), gathered[2].reshape(D, D)
    dt_raw = matmul(hn, w_dt, 'nn', 512, 128, 2048, F32, "in_proj_dt")
    o_a, o_mix, ltot = attn_fused_fwd(proj, slopes, cfg, "attn_fwd")
    xc = conv_fwd(proj, conv_w, conv_b, cfg, "conv_fwd")
    dt, acum = ssd_prep(dt_raw, dt_bias_p, a_log_p, cfg, "ssd_prep")
    act = acum[:, :cfg.SH].T
    sel_p, sel_l = _head_selectors(cfg)
    y, states = ssd_scan_fwd(xc, dt, acum, act, sel_p, sel_l, cfg, "ssd_scan_fwd")
    y_n = gated_norm_fwd(y, xc, proj, dvec, ssm_norm_w, cfg, "gated_norm_fwd")
    a_out, s_out, merged = branch_merge(o_a, w_attn, y_n, w_ssm, proj, cfg, "branch_merge")
    dout, loss_p, g_final_w = out_proj_final(merged, w_out, x, final_norm_w.reshape(1, D), tgt, "out_proj_final")

    g_w_out = matmul(merged, dout, 'tn', 1024, 1024, 2048, BF16, "g_w_out")
    da_out, ds_out, dga, dgs = merge_bwd(dout, w_out, a_out, s_out, proj, cfg, "merge_bwd")
    g_w_attn = matmul(o_a, da_out, 'tn', 1024, 1024, 2048, BF16, "g_w_attn")
    g_w_ssm = matmul(y_n, ds_out, 'tn', 1024, 1024, 2048, BF16, "g_w_ssm")
    do_a = matmul(da_out, w_attn, 'nt', 512, 1024, 2048, BF16, "d_o_a")
    dyn = matmul(ds_out, w_ssm, 'nt', 512, 1024, 2048, BF16, "d_y_n")
    dy, dz_s, g_ssm_norm, g_dvec = gated_norm_bwd(dyn, y, xc, proj, dvec, ssm_norm_w, cfg, "gated_norm_bwd")
    sends = [g.reshape((N_DEV, g.shape[0] // N_DEV, D)) for g in (g_w_attn, g_w_ssm, g_w_out)]
    (dxs, dB, dC, dac_g, dxsum_g), (r_attn, r_ssm, r_out) = ssd_scan_bwd(
        xc, dt, acum, act, sel_p, sel_l, states, y, dy, dvec, cfg, "ssd_scan_bwd", side=(sends, [False] * 3))
    ddt_raw, g_dt_bias, g_a_log = dt_bwd(dac_g, dxsum_g, dt_raw, dt, dt_bias_p, a_log_p, cfg, "dt_bwd")
    dxbc, g_cw, g_cb = [], [], []
    for nm, piece, c_off in (("xs", dxs, 0), ("b", dB, cfg.SI), ("c", dC, cfg.SI + cfg.GN)):
        dpre, gw, gb = conv_bwd_a(proj, piece, conv_w, conv_b, cfg, "conv_bwd_a_" + nm, c_off)
        dxbc.append(conv_bwd_b(dpre, conv_w, cfg, "conv_bwd_b_" + nm, c_off))
        g_cw.append(gw)
        g_cb.append(gb)
    g_conv_w, g_conv_b = jnp.concatenate(g_cw, axis=1), jnp.concatenate(g_cb, axis=1)
    dq, dk, dv, dz_a = attn_fused_bwd(proj, do_a, o_mix, ltot, slopes, cfg, "attn_bwd")
    dproj = jnp.concatenate([dq, dk, dv, dz_a, dz_s] + dxbc + [dga, dgs], axis=1)
    g_w_main = matmul(hn, dproj, 'tn', 1024, 2048, 2048, BF16, "g_w_main")
    g_w_dt = matmul(hn, ddt_raw, 'tn', 1024, 128, 2048, BF16, "g_w_dt")
    send_in = jnp.stack([jnp.concatenate([g[:, lo:hi] for g, lo, hi in parts], axis=1)
                         for parts in _shard_columns(cfg, g_w_main, g_w_dt)])
    dhn_a, (r_in,) = matmul(dproj, w_main, 'nt', 1024, 1024, 2048, F32, "d_hn", side=([send_in], [False]))
    dhn_b = matmul(ddt_raw, w_dt, 'nt', 512, 1024, 128, F32, "d_hn_dt")
    grad_x, g_norm_w = rmsnorm_bwd(dhn_a, dhn_b, x, norm_w, dout, "rmsnorm_bwd")

    g_d_skip = jnp.sum(g_dvec.reshape(cfg.SH, cfg.P), axis=1).reshape(1, cfg.SH)
    small = dict(norm_w=g_norm_w, conv_b=g_conv_b, dt_bias=g_dt_bias[:, :cfg.SH], a_log=g_a_log[:, :cfg.SH],
                 d_skip=g_d_skip, ssm_norm_w=g_ssm_norm, final_norm_w=g_final_w, conv_w=g_conv_w[:cfg.KC])
    return loss_p, grad_x, small, dict(w_in=r_in, w_attn=r_attn, w_ssm=r_ssm, w_out=r_out)


def _mesh_pos():
    return lax.axis_index("x"), lax.axis_index("y"), lax.axis_index("c")


def _flat(pos):
    return 4 * pos[0] + 2 * pos[1] + pos[2]


def _exchange_shapes(arrs, gathers):
    return [jax.ShapeDtypeStruct(((N_DEV,) + a.shape) if g else a.shape, a.dtype) for a, g in zip(arrs, gathers)]


def _exchange_sems(n):
    return [pltpu.SemaphoreType.DMA((n * (N_DEV - 1),)), pltpu.SemaphoreType.DMA((n * (N_DEV - 1),)),
            pltpu.SemaphoreType.DMA((n,))]


def _exchange_copies(ins, outs, gathers, send_sems, recv_sems, loc_sems):
    pos = _mesh_pos()
    me = _flat(pos)
    starts, waits = [], []
    for a in range(len(ins)):
        mine = ins[a] if gathers[a] else ins[a].at[me]
        loc = pltpu.make_async_copy(mine, outs[a].at[me], loc_sems.at[a])
        starts.append(loc)
        waits.append(loc)
        for k in range(1, N_DEV):
            flip = ((k >> 2) & 1, (k >> 1) & 1, k & 1)
            peer = tuple(1 - p if f else p for p, f in zip(pos, flip))
            pk = _flat(peer)
            src = ins[a] if gathers[a] else ins[a].at[pk]
            sems = dict(send_sem=send_sems.at[a * (N_DEV - 1) + k - 1], recv_sem=recv_sems.at[a * (N_DEV - 1) + k - 1],
                        device_id=peer, device_id_type=pl.DeviceIdType.MESH)
            starts.append(pltpu.make_async_remote_copy(src_ref=src, dst_ref=outs[a].at[me], **sems))
            waits.append(pltpu.make_async_remote_copy(src_ref=src, dst_ref=outs[a].at[pk], **sems))
    return starts, waits


def exchange(arrs, gathers, name):
    n = len(arrs)

    def body(*refs):
        starts, waits = _exchange_copies(refs[:n], refs[n:2 * n], gathers, *refs[2 * n:])
        for cp in starts:
            cp.start()
        for cp in waits:
            cp.wait()

    hbm = pl.BlockSpec(memory_space=pltpu.HBM)
    return pl.pallas_call(
        body, name=name, in_specs=[hbm] * n, out_specs=[hbm] * n, out_shape=_exchange_shapes(arrs, gathers),
        scratch_shapes=_exchange_sems(n),
    )(*arrs)


def gather_two_level(arrs, chunks, name):
    n = len(arrs)

    def body(*refs):
        ins, outs = refs[:n], refs[n:2 * n]
        send_sems, recv_sems, loc_sems = refs[2 * n:]
        x, y, c = _mesh_pos()
        me, sib = (x, y, c), (x, y, 1 - c)
        chips = [(1 - x, y), (x, 1 - y), (1 - x, 1 - y)]
        plan, base = [], 0
        for a in range(n):
            step = arrs[a].shape[0] // chunks[a]
            for q in range(chunks[a]):
                plan.append((a, pl.ds(q * step, step), base))
                base += N_DEV - 1

        def copy(a, rows, sem, block, to, own=False):
            dst = outs[a].at[_flat(block), rows]
            return pltpu.make_async_remote_copy(
                src_ref=ins[a].at[rows] if own else dst, dst_ref=dst, send_sem=send_sems.at[sem],
                recv_sem=recv_sems.at[sem], device_id=to, device_id_type=pl.DeviceIdType.MESH)

        local = [pltpu.make_async_copy(ins[a], outs[a].at[_flat(me)], loc_sems.at[a]) for a in range(n)]
        for cp in local:
            cp.start()
        sent = []
        for a, rows, s in plan:
            sent.append(copy(a, rows, s, me, sib, own=True))
            sent += [copy(a, rows, s + 1 + j, me, (*chip, c), own=True) for j, chip in enumerate(chips)]
        for cp in sent:
            cp.start()
        for a, rows, s in plan:
            for j, chip in enumerate(chips):
                copy(a, rows, s + 1 + j, (*chip, c), me).wait_recv()
                passed = copy(a, rows, s + 4 + j, (*chip, c), sib)
                passed.start()
                sent.append(passed)
        for a, rows, s in plan:
            copy(a, rows, s, sib, me).wait_recv()
            for j, chip in enumerate(chips):
                copy(a, rows, s + 4 + j, (*chip, 1 - c), me).wait_recv()
        for cp in sent:
            cp.wait_send()
        for cp in local:
            cp.wait()

    hbm = pl.BlockSpec(memory_space=pltpu.HBM)
    nsem = (N_DEV - 1) * sum(chunks)
    return pl.pallas_call(
        body, name=name, in_specs=[hbm] * n, out_specs=[hbm] * n, out_shape=_exchange_shapes(arrs, [True] * n),
        scratch_shapes=[pltpu.SemaphoreType.DMA((nsem,)), pltpu.SemaphoreType.DMA((nsem,)),
                        pltpu.SemaphoreType.DMA((n,))],
    )(*arrs)


def with_exchange(body, n_in, n_out, gathers, grid):
    n = len(gathers)

    def wrapped(*refs):
        ins, sends = refs[:n_in], refs[n_in:n_in + n]
        outs, recvs = refs[n_in + n:n_in + n + n_out], refs[n_in + 2 * n + n_out - n:n_in + 2 * n + n_out]
        scratch, sems = refs[n_in + 2 * n + n_out:-3], refs[-3:]
        ids = [pl.program_id(d) for d in range(len(grid))]
        first = functools.reduce(jnp.logical_and, [i == 0 for i in ids])
        last = functools.reduce(jnp.logical_and, [i == g - 1 for i, g in zip(ids, grid)])

        @pl.when(first)
        def _():
            for cp in _exchange_copies(sends, recvs, gathers, *sems)[0]:
                cp.start()

        body(*ins, *outs, *scratch)

        @pl.when(last)
        def _():
            for cp in _exchange_copies(sends, recvs, gathers, *sems)[1]:
                cp.wait()

    return wrapped


def adamw(g_src, w, m, v, summed, name, tr=64):
    R, C = w.shape
    tr = min(tr, R)
    assert R % tr == 0

    def body(g_ref, w_ref, m_ref, v_ref, g_out, d_out, m_out, v_out):
        if summed:
            g = g_ref[0].astype(F32)
            for j in range(1, N_DEV):
                g = g + g_ref[j].astype(F32)
        else:
            g = g_ref[...]
        mn = ADAM_B1 * m_ref[...] + (1.0 - ADAM_B1) * g
        vn = ADAM_B2 * v_ref[...] + (1.0 - ADAM_B2) * (g * g)
        m_hat = mn / (1.0 - ADAM_B1 ** ADAM_STEP)
        v_hat = vn / (1.0 - ADAM_B2 ** ADAM_STEP)
        g_out[...] = g
        d_out[...] = -ADAM_LR * (m_hat / (jnp.sqrt(v_hat) + ADAM_EPS) + ADAM_WD * w_ref[...])
        m_out[...] = mn
        v_out[...] = vn

    row = pl.BlockSpec((tr, C), lambda i: (i, 0))
    gspec = pl.BlockSpec((N_DEV, tr, C), lambda i: (0, i, 0)) if summed else row
    sh = jax.ShapeDtypeStruct((R, C), F32)
    return pl.pallas_call(
        body, name=name, grid=(R // tr,), in_specs=[gspec, row, row, row], out_specs=[row] * 4, out_shape=[sh] * 4,
        compiler_params=_cp(("parallel",)),
    )(g_src, w, m, v)


SMALL = ('norm_w', 'conv_b', 'dt_bias', 'a_log', 'd_skip', 'ssm_norm_w', 'final_norm_w')


def _rows(n):
    return -(-n // (8 * LANES)) * 8


def _pack(vals):
    parts = []
    for a in vals:
        f = a.reshape(-1)
        parts.append(jnp.pad(f, (0, _rows(f.size) * LANES - f.size)).reshape(-1, LANES))
    return jnp.concatenate(parts, axis=0)


def _unpack(packed, shapes):
    out, r = [], 0
    for s in shapes:
        n = math.prod(s)
        out.append(packed[r:r + _rows(n)].reshape(-1)[:n].reshape(s))
        r += _rows(n)
    return out


def kernel(x, norm_w, w_in, conv_w, conv_b, dt_bias, a_log, d_skip, ssm_norm_w, w_attn_branch, w_ssm_branch, w_out, final_norm_w, loss_target, m_norm_w, m_w_in, m_conv_w, m_conv_b, m_dt_bias, m_a_log, m_d_skip, m_ssm_norm_w, m_w_attn_branch, m_w_ssm_branch, m_w_out, m_final_norm_w, v_norm_w, v_w_in, v_conv_w, v_conv_b, v_dt_bias, v_a_log, v_d_skip, v_ssm_norm_w, v_w_attn_branch, v_w_ssm_branch, v_w_out, v_final_norm_w):
    cfg = CFG
    D, SH = cfg.D, cfg.SH
    me = _flat(_mesh_pos())
    dt0 = 4 * cfg.AW + cfg.SI + cfg.CD
    ws = w_in.shape[-1]

    g_in, g_cw = gather_two_level([w_in[0].astype(BF16), conv_w[0]], [4, 1], "gather_w_in")
    main_cols, dt_cols = [], []
    for k, parts in enumerate(_shard_columns(cfg, "main", "dt")):
        at = 0
        for which, lo, hi in parts:
            (main_cols if which == "main" else dt_cols).append(g_in[k][:, at:at + hi - lo])
            at += hi - lo
    w_main = jnp.concatenate(main_cols, axis=1)
    w_dt = _pad_lanes(jnp.concatenate(dt_cols, axis=1))
    conv_full = g_cw.transpose(1, 0, 2).reshape(cfg.KC, cfg.CD)
    shards = [w_attn_branch[0].astype(BF16), w_ssm_branch[0].astype(BF16), w_out[0].astype(BF16)]

    loss_p, grad_x, small, recv = local_step(
        cfg, x[0], loss_target[0], norm_w, conv_full, conv_b, dt_bias, a_log, d_skip,
        ssm_norm_w, final_norm_w, w_main, w_dt, shards, dt0)

    upd = {}
    upd['w_in'] = adamw(recv['w_in'], w_in[0], m_w_in[0], v_w_in[0], True, "adamw_w_in")
    upd['w_attn_branch'] = adamw(recv['w_attn'], w_attn_branch[0], m_w_attn_branch[0], v_w_attn_branch[0], True,
                                 "adamw_w_attn")
    upd['w_ssm_branch'] = adamw(recv['w_ssm'], w_ssm_branch[0], m_w_ssm_branch[0], v_w_ssm_branch[0], True,
                                "adamw_w_ssm")
    upd['w_out'] = adamw(recv['w_out'], w_out[0], m_w_out[0], v_w_out[0], True, "adamw_w_out")

    extra = [jnp.zeros((cfg.KC, cfg.CD), F32), jnp.zeros((1, 1), F32)]
    shapes = [small[n].shape for n in SMALL] + [e.shape for e in extra]
    part = _pack([small[n] for n in SMALL] + [small['conv_w'], loss_p[:, :1]])
    gathered, = exchange([part], [True], "gather_small")
    given = dict(norm_w=(norm_w, m_norm_w, v_norm_w), conv_b=(conv_b, m_conv_b, v_conv_b),
                 dt_bias=(dt_bias, m_dt_bias, v_dt_bias), a_log=(a_log, m_a_log, v_a_log),
                 d_skip=(d_skip, m_d_skip, v_d_skip), ssm_norm_w=(ssm_norm_w, m_ssm_norm_w, v_ssm_norm_w),
                 final_norm_w=(final_norm_w, m_final_norm_w, v_final_norm_w))
    packed = [_pack([given[n][t] for n in SMALL] + extra) for t in range(3)]
    outs = adamw(gathered, *packed, True, "adamw_small", tr=part.shape[0])
    unpacked = [_unpack(o, shapes) for o in outs]
    for i, n in enumerate(SMALL):
        upd[n] = [u[i].reshape(given[n][0].shape) for u in unpacked]
    loss = unpacked[0][-1].reshape(())
    cw = conv_w.shape[-1]
    g_cw_mine = lax.dynamic_slice_in_dim(unpacked[0][-2], me * cw, cw, axis=1)
    upd['conv_w'] = adamw(g_cw_mine.reshape(-1, LANES), conv_w.reshape(-1, LANES), m_conv_w.reshape(-1, LANES),
                          v_conv_w.reshape(-1, LANES), False, "adamw_conv_w")

    order = ['norm_w', 'w_in', 'conv_w', 'conv_b', 'dt_bias', 'a_log', 'd_skip', 'ssm_norm_w', 'w_attn_branch',
             'w_ssm_branch', 'w_out', 'final_norm_w']
    like = dict(norm_w=norm_w, w_in=w_in, conv_w=conv_w, conv_b=conv_b, dt_bias=dt_bias, a_log=a_log, d_skip=d_skip,
                ssm_norm_w=ssm_norm_w, w_attn_branch=w_attn_branch, w_ssm_branch=w_ssm_branch, w_out=w_out,
                final_norm_w=final_norm_w)
    result = [loss, grad_x[None]]
    for t in range(4):
        result += [upd[n][t].reshape(like[n].shape) for n in order]
    return tuple(result)
```

```python
import functools
import math
from typing import NamedTuple

import jax
import jax.numpy as jnp
from jax import lax
from jax.experimental import pallas as pl
from jax.experimental.pallas import tpu as pltpu

F32 = jnp.float32
BF16 = jnp.bfloat16
RMS_EPS = 1e-6
NEG = -1e30
N_DEV = 8
CPS = 4
LANES = 128
ATTN_BLOCK = 128
ADAM_LR, ADAM_B1, ADAM_B2, ADAM_EPS, ADAM_WD, ADAM_STEP = 0.001, 0.9, 0.999, 1e-08, 0.01, 10
VMEM_LIMIT = 56 * 1024 * 1024


class Cfg(NamedTuple):
    D: int = 2048
    S: int = 8192
    AH: int = 16
    E: int = 128
    patterns: tuple = ((128, 1), (512, 4), (2048, 16))
    SI: int = 4096
    P: int = 64
    SG: int = 8
    SN: int = 128
    KC: int = 4
    L: int = 128

    @property
    def AW(self): return self.AH * self.E
    @property
    def SH(self): return self.SI // self.P
    @property
    def HPG(self): return self.SH // self.SG
    @property
    def GN(self): return self.SG * self.SN
    @property
    def CD(self): return self.SI + 2 * self.GN
    @property
    def k0(self): return self.AW
    @property
    def v0(self): return 2 * self.AW
    @property
    def za0(self): return 3 * self.AW
    @property
    def zs0(self): return 4 * self.AW
    @property
    def xbc0(self): return 4 * self.AW + self.SI
    @property
    def ga0(self): return self.xbc0 + self.CD
    @property
    def gs0(self): return self.ga0 + self.D
    @property
    def NP(self): return self.gs0 + self.D
    @property
    def N_IN(self): return self.NP + self.SH


CFG = Cfg()


def _cp(sem=None, vmem=VMEM_LIMIT):
    return pltpu.CompilerParams(dimension_semantics=sem, vmem_limit_bytes=vmem)


def _sigmoid(z):
    return 1.0 / (1.0 + jnp.exp(-z))


def _dot(a, b, dims):
    return lax.dot_general(a, b, (dims, ((), ())), preferred_element_type=F32)


NN = ((1,), (0,))
NT = ((1,), (1,))
TN = ((0,), (0,))


def _blk(off, width):
    assert off % width == 0, (off, width)
    return off // width


def matmul(a, b, mode, tm, tn, tk, out_dtype, name, side=None):
    if mode == 'nn':
        (M, K), (_, N) = a.shape, b.shape
    elif mode == 'nt':
        (M, K), (N, _) = a.shape, b.shape
    else:
        (K, M), (_, N) = a.shape, b.shape
    tm, tn, tk = min(tm, M), min(tn, N), min(tk, K)
    assert M % tm == 0 and N % tn == 0 and K % tk == 0, (M, N, K, tm, tn, tk)
    nk = K // tk
    dims = {'nn': NN, 'nt': NT, 'tn': TN}[mode]

    def body(a_ref, b_ref, o_ref, *acc):
        part = _dot(a_ref[...].astype(BF16), b_ref[...].astype(BF16), dims)
        if nk == 1:
            o_ref[...] = part.astype(out_dtype)
        else:
            acc_ref, = acc
            k = pl.program_id(2)

            @pl.when(k == 0)
            def _():
                acc_ref[...] = part

            @pl.when(k > 0)
            def _():
                acc_ref[...] += part

            @pl.when(k == nk - 1)
            def _():
                o_ref[...] = acc_ref[...].astype(out_dtype)

    if mode == 'tn':
        a_spec = pl.BlockSpec((tk, tm), lambda n, m, k: (k, m))
    else:
        a_spec = pl.BlockSpec((tm, tk), lambda n, m, k: (m, k))
    if mode == 'nt':
        b_spec = pl.BlockSpec((tn, tk), lambda n, m, k: (n, k))
    else:
        b_spec = pl.BlockSpec((tk, tn), lambda n, m, k: (k, n))
    grid = (N // tn, M // tm, nk)
    o_spec = pl.BlockSpec((tm, tn), lambda n, m, k: (m, n))
    o_shape = jax.ShapeDtypeStruct((M, N), out_dtype)
    acc = [] if nk == 1 else [pltpu.VMEM((tm, tn), F32)]
    if side is None:
        return pl.pallas_call(
            body, name=name, grid=grid, in_specs=[a_spec, b_spec], out_specs=o_spec, out_shape=o_shape,
            scratch_shapes=acc, compiler_params=_cp(("parallel", "parallel", "arbitrary")),
        )(a, b)
    arrs, gathers = side
    whole = pl.BlockSpec(memory_space=pl.ANY)
    res = pl.pallas_call(
        with_exchange(body, 2, 1, gathers, grid), name=name, grid=grid,
        in_specs=[a_spec, b_spec] + [whole] * len(arrs), out_specs=[o_spec] + [whole] * len(arrs),
        out_shape=[o_shape] + _exchange_shapes(arrs, gathers),
        scratch_shapes=acc + _exchange_sems(len(arrs)),
        compiler_params=_cp(("arbitrary", "arbitrary", "arbitrary")),
    )(a, b, *arrs)
    return res[0], res[1:]


def rmsnorm_fwd(x, w, name, tm=256):
    S, D = x.shape

    def body(x_ref, w_ref, o_ref):
        xv = x_ref[...]
        r = lax.rsqrt(jnp.mean(xv * xv, axis=-1, keepdims=True) + RMS_EPS)
        o_ref[...] = ((xv * r) * w_ref[...]).astype(BF16)

    return pl.pallas_call(
        body, name=name, grid=(S // tm,),
        in_specs=[pl.BlockSpec((tm, D), lambda i: (i, 0)), pl.BlockSpec((1, D), lambda i: (0, 0))],
        out_specs=pl.BlockSpec((tm, D), lambda i: (i, 0)),
        out_shape=jax.ShapeDtypeStruct((S, D), BF16),
        compiler_params=_cp(("parallel",)),
    )(x, w)


def rmsnorm_bwd(dh_a, dh_b, x, w, dout, name, tm=128):
    S, D = x.shape

    def body(da_ref, db_ref, x_ref, w_ref, do_ref, gx_ref, gw_ref):
        xv = x_ref[...]
        dh = da_ref[...] + db_ref[...]
        r = lax.rsqrt(jnp.mean(xv * xv, axis=-1, keepdims=True) + RMS_EPS)
        g = dh * w_ref[...]
        dx = r * g - xv * (r * r * r) * jnp.mean(g * xv, axis=-1, keepdims=True)
        gx_ref[...] = do_ref[...] + dx
        gw = jnp.sum(dh * (xv * r), axis=0, keepdims=True)

        @pl.when(pl.program_id(0) == 0)
        def _():
            gw_ref[...] = gw

        @pl.when(pl.program_id(0) > 0)
        def _():
            gw_ref[...] += gw

    row = pl.BlockSpec((tm, D), lambda i: (i, 0))
    vec = pl.BlockSpec((1, D), lambda i: (0, 0))
    return pl.pallas_call(
        body, name=name, grid=(S // tm,),
        in_specs=[row, row, row, vec, row],
        out_specs=[row, vec],
        out_shape=[jax.ShapeDtypeStruct((S, D), F32), jax.ShapeDtypeStruct((1, D), F32)],
        compiler_params=_cp(("arbitrary",)),
    )(dh_a, dh_b, x, w, dout)


def out_proj_final(merged, w_out, x, fw, tgt, name, tm=256):
    S, D = x.shape

    def body(m_ref, wo_ref, x_ref, w_ref, t_ref, do_ref, loss_ref, gw_ref):
        out = x_ref[...] + _dot(m_ref[...], wo_ref[...], NN)
        w = w_ref[...]
        r = lax.rsqrt(jnp.mean(out * out, axis=-1, keepdims=True) + RMS_EPS)
        yn = out * r
        err = yn * w - t_ref[...]
        lrow = 0.5 * jnp.mean(err * err, axis=-1, keepdims=True)
        lsum = jnp.zeros((1, LANES), F32) + jnp.sum(lrow, axis=0, keepdims=True)
        dfin = err * (1.0 / D)
        g = dfin * w
        do_ref[...] = r * g - out * (r * r * r) * jnp.mean(g * out, axis=-1, keepdims=True)
        gw = jnp.sum(dfin * yn, axis=0, keepdims=True)

        @pl.when(pl.program_id(0) == 0)
        def _():
            gw_ref[...] = gw
            loss_ref[...] = lsum

        @pl.when(pl.program_id(0) > 0)
        def _():
            gw_ref[...] += gw
            loss_ref[...] += lsum

    row = pl.BlockSpec((tm, D), lambda i: (i, 0))
    vec = pl.BlockSpec((1, D), lambda i: (0, 0))
    return pl.pallas_call(
        body, name=name, grid=(S // tm,),
        in_specs=[row, pl.BlockSpec((D, D), lambda i: (0, 0)), row, vec, row],
        out_specs=[row, pl.BlockSpec((1, LANES), lambda i: (0, 0)), vec],
        out_shape=[jax.ShapeDtypeStruct((S, D), F32), jax.ShapeDtypeStruct((1, LANES), F32),
                   jax.ShapeDtypeStruct((1, D), F32)],
        compiler_params=_cp(("arbitrary",)),
    )(merged, w_out, x, fw, tgt)


def branch_merge(o_a, w_attn, y_n, w_ssm, proj, cfg, name, tm=512, tn=512):
    S, D = cfg.S, cfg.D
    tm, tn = min(tm, S), min(tn, D)

    def body(oa_ref, wa_ref, yn_ref, ws_ref, ga_ref, gs_ref, a_ref, s_ref, m_ref):
        a = _dot(oa_ref[...], wa_ref[...], NN)
        sv = _dot(yn_ref[...], ws_ref[...], NN)
        a_ref[...] = a.astype(BF16)
        s_ref[...] = sv.astype(BF16)
        m_ref[...] = (_sigmoid(ga_ref[...]) * a + _sigmoid(gs_ref[...]) * sv).astype(BF16)

    tile = pl.BlockSpec((tm, tn), lambda n, m: (m, n))
    return pl.pallas_call(
        body, name=name, grid=(D // tn, S // tm),
        in_specs=[pl.BlockSpec((tm, cfg.AW), lambda n, m: (m, 0)), pl.BlockSpec((cfg.AW, tn), lambda n, m: (0, n)),
                  pl.BlockSpec((tm, cfg.SI), lambda n, m: (m, 0)), pl.BlockSpec((cfg.SI, tn), lambda n, m: (0, n)),
                  pl.BlockSpec((tm, tn), lambda n, m: (m, _blk(cfg.ga0, tn) + n)),
                  pl.BlockSpec((tm, tn), lambda n, m: (m, _blk(cfg.gs0, tn) + n))],
        out_specs=[tile, tile, tile],
        out_shape=[jax.ShapeDtypeStruct((S, D), BF16)] * 3,
        compiler_params=_cp(("parallel", "parallel")),
    )(o_a, w_attn, y_n, w_ssm, proj, proj)


def merge_bwd(dout, w_out, a_out, s_out, proj, cfg, name, tm=512, tn=1024):
    S, D = cfg.S, cfg.D
    tm, tn = min(tm, S), min(tn, D)

    def body(do_ref, wo_ref, a_ref, s_ref, ga_ref, gs_ref, da_ref, ds_ref, dga_ref, dgs_ref):
        dmv = _dot(do_ref[...].astype(BF16), wo_ref[...], NT)
        sa = _sigmoid(ga_ref[...])
        ss = _sigmoid(gs_ref[...])
        da_ref[...] = (dmv * sa).astype(BF16)
        ds_ref[...] = (dmv * ss).astype(BF16)
        dga_ref[...] = (dmv * a_ref[...] * (sa * (1.0 - sa))).astype(BF16)
        dgs_ref[...] = (dmv * s_ref[...] * (ss * (1.0 - ss))).astype(BF16)

    tile = pl.BlockSpec((tm, tn), lambda n, m: (m, n))
    sh = jax.ShapeDtypeStruct((S, D), BF16)
    return pl.pallas_call(
        body, name=name, grid=(D // tn, S // tm),
        in_specs=[pl.BlockSpec((tm, D), lambda n, m: (m, 0)), pl.BlockSpec((tn, D), lambda n, m: (n, 0)), tile, tile,
                  pl.BlockSpec((tm, tn), lambda n, m: (m, _blk(cfg.ga0, tn) + n)),
                  pl.BlockSpec((tm, tn), lambda n, m: (m, _blk(cfg.gs0, tn) + n))],
        out_specs=[tile] * 4, out_shape=[sh] * 4,
        compiler_params=_cp(("parallel", "parallel")),
    )(dout, w_out, a_out, s_out, proj, proj)


def _attn_rows(base, d):
    return pl.ds(base, ATTN_BLOCK) if d == 1 else pl.ds(base, ATTN_BLOCK, stride=d)


def _attn_units(cfg):
    dmax = max(d for _, d in cfg.patterns)
    units = []
    for p, (window, d) in enumerate(cfg.patterns):
        assert window // d == ATTN_BLOCK and dmax % d == 0
        nsub = dmax // d
        for b in range(nsub):
            for r in range(d):
                base = b * ATTN_BLOCK * d + r
                if b > 0:
                    units.append((p, d, base, (b - 1) * ATTN_BLOCK * d + r, False))
                else:
                    units.append((p, d, base, (nsub - 1) * ATTN_BLOCK * d + r, True))
    return units, ATTN_BLOCK * dmax


def _set_bias_tiles(bias_s, slope, cfg):
    qi = lax.broadcasted_iota(jnp.int32, (ATTN_BLOCK, ATTN_BLOCK), 0)
    ki = lax.broadcasted_iota(jnp.int32, (ATTN_BLOCK, ATTN_BLOCK), 1)
    for p, (_, d) in enumerate(cfg.patterns):
        bias_s[2 * p] = jnp.where(ki >= qi, (-slope) * ((ATTN_BLOCK + qi - ki) * d).astype(F32), NEG)
        bias_s[2 * p + 1] = jnp.where(ki <= qi, (-slope) * ((qi - ki) * d).astype(F32), NEG)


def _unit_scores(q, kcat, bias_s, p, prev_ok, scale):
    s = _dot(q, kcat, NT) * scale + jnp.concatenate([bias_s[2 * p], bias_s[2 * p + 1]], axis=1)
    if prev_ok is not None:
        cur_half = lax.broadcasted_iota(jnp.int32, s.shape, 1) >= ATTN_BLOCK
        s = jnp.where(jnp.logical_or(cur_half, prev_ok), s, NEG)
    return s


def _slope_table(cfg):
    slopes = jnp.asarray([2.0 ** (-8.0 * (h + 1) / cfg.AH) for h in range(cfg.AH)], F32)
    return jnp.broadcast_to(slopes.reshape(cfg.AH, 1, 1), (cfg.AH, 8, LANES))


def attn_fused_fwd(proj, slopes, cfg, name):
    S, E, AH = cfg.S, cfg.E, cfg.AH
    units, SB = _attn_units(cfg)
    assert S % SB == 0
    npat = len(cfg.patterns)
    scale = E ** -0.5

    def spec(off, prev):
        c0 = _blk(off, E)
        if prev:
            return pl.BlockSpec((SB, E), lambda h, i: (jnp.maximum(i - 1, 0), c0 + h))
        return pl.BlockSpec((SB, E), lambda h, i: (i, c0 + h))

    def body(q_ref, kp_ref, kc_ref, vp_ref, vc_ref, z_ref, sl_ref, oa_ref, om_ref, lt_ref, *scr):
        o_s, l_s, bias_s = scr[:npat], scr[npat:2 * npat], scr[2 * npat]
        i = pl.program_id(1)

        @pl.when(i == 0)
        def _():
            _set_bias_tiles(bias_s, sl_ref[0, 0:1, :], cfg)

        for p, d, base, pbase, from_prev in units:
            rows, prows = _attn_rows(base, d), _attn_rows(pbase, d)
            q = q_ref[rows, :].astype(BF16)
            kp = (kp_ref if from_prev else kc_ref)[prows, :].astype(BF16)
            vp = (vp_ref if from_prev else vc_ref)[prows, :].astype(BF16)
            kcat = jnp.concatenate([kp, kc_ref[rows, :].astype(BF16)], axis=0)
            vcat = jnp.concatenate([vp, vc_ref[rows, :].astype(BF16)], axis=0)
            s = _unit_scores(q, kcat, bias_s, p, (i > 0) if from_prev else None, scale)
            m = jnp.max(s, axis=1, keepdims=True)
            pr = jnp.exp(s - m)
            l = jnp.sum(pr, axis=1, keepdims=True)
            o_s[p][rows, :] = _dot(pr.astype(BF16), vcat, NN) * (1.0 / l)
            l_s[p][rows, :] = m + jnp.log(l)
        ls = [l_s[p][...] for p in range(npat)]
        m = functools.reduce(jnp.maximum, ls)
        lt = m + jnp.log(sum(jnp.exp(l_ - m) for l_ in ls))
        lt_ref[...] = lt
        mix = sum(jnp.exp(ls[p] - lt) * o_s[p][...] for p in range(npat))
        om_ref[...] = mix
        z = z_ref[...]
        oa_ref[...] = (mix * (z * _sigmoid(z))).astype(BF16)

    out = pl.BlockSpec((SB, E), lambda h, i: (i, h))
    return pl.pallas_call(
        body, name=name, grid=(AH, S // SB),
        in_specs=[spec(0, False), spec(cfg.k0, True), spec(cfg.k0, False), spec(cfg.v0, True), spec(cfg.v0, False),
                  spec(cfg.za0, False), pl.BlockSpec((1, 8, LANES), lambda h, i: (h, 0, 0))],
        out_specs=[out, out, pl.BlockSpec((SB, 1), lambda h, i: (h * (S // SB) + i, 0))],
        out_shape=[jax.ShapeDtypeStruct((S, cfg.AW), BF16), jax.ShapeDtypeStruct((S, cfg.AW), F32),
                   jax.ShapeDtypeStruct((AH * S, 1), F32)],
        scratch_shapes=[pltpu.VMEM((SB, E), F32)] * npat + [pltpu.VMEM((SB, 1), F32)] * npat +
        [pltpu.VMEM((2 * npat, ATTN_BLOCK, ATTN_BLOCK), F32)],
        compiler_params=_cp(("parallel", "arbitrary")),
    )(proj, proj, proj, proj, proj, proj, slopes)


def attn_fused_bwd(proj, do_a, o_mix, ltot, slopes, cfg, name):
    S, E, AH = cfg.S, cfg.E, cfg.AH
    units, SB = _attn_units(cfg)
    nsb = S // SB
    last = nsb - 1
    scale = E ** -0.5

    def spec(off, prev):
        c0 = _blk(off, E)
        if prev:
            return pl.BlockSpec((SB, E), lambda h, i: (jnp.maximum(i - 1, 0), c0 + h))
        return pl.BlockSpec((SB, E), lambda h, i: (jnp.minimum(i, last), c0 + h))

    cur = pl.BlockSpec((SB, E), lambda h, i: (jnp.minimum(i, last), h))
    prev = pl.BlockSpec((SB, E), lambda h, i: (jnp.maximum(i - 1, 0), h))

    def body(q_ref, kp_ref, kc_ref, vp_ref, vc_ref, z_ref, doa_ref, om_ref, lt_ref, sl_ref,
             dq_ref, dk_ref, dv_ref, dz_ref, dmix_s, dl_s, dq_s, dkp_s, dvp_s, dkc_s, dvc_s, bias_s):
        i = pl.program_id(1)

        @pl.when(i == 0)
        def _():
            dkc_s[...] = jnp.zeros_like(dkc_s)
            dvc_s[...] = jnp.zeros_like(dvc_s)
            _set_bias_tiles(bias_s, sl_ref[0, 0:1, :], cfg)

        @pl.when(i < nsb)
        def _():
            z = z_ref[...]
            s = _sigmoid(z)
            doa = doa_ref[...].astype(F32)
            om = om_ref[...]
            dmix = doa * (z * s)
            dmix_s[...] = dmix
            dz_ref[...] = (doa * om * (s * (1.0 + z * (1.0 - s)))).astype(BF16)
            dl_s[...] = jnp.sum(dmix * om, axis=1, keepdims=True)
            dkp_s[...] = dkc_s[...]
            dvp_s[...] = dvc_s[...]
            dkc_s[...] = jnp.zeros_like(dkc_s)
            dvc_s[...] = jnp.zeros_like(dvc_s)
            dq_s[...] = jnp.zeros_like(dq_s)
            for p, d, base, pbase, from_prev in units:
                rows, prows = _attn_rows(base, d), _attn_rows(pbase, d)
                q = q_ref[rows, :].astype(BF16)
                kc = kc_ref[rows, :].astype(BF16)
                kp = (kp_ref if from_prev else kc_ref)[prows, :].astype(BF16)
                vp = (vp_ref if from_prev else vc_ref)[prows, :].astype(BF16)
                do = dmix_s[rows, :].astype(BF16)
                lt = lt_ref[rows, :]
                dlt = dl_s[rows, :]
                kcat = jnp.concatenate([kp, kc], axis=0)
                vcat = jnp.concatenate([vp, vc_ref[rows, :].astype(BF16)], axis=0)
                pr = jnp.exp(_unit_scores(q, kcat, bias_s, p, (i > 0) if from_prev else None, scale) - lt)
                ds = (pr * (_dot(do, vcat, NT) - dlt) * scale).astype(BF16)
                dq_s[rows, :] += _dot(ds, kcat, NN)
                dkcat = _dot(ds, q, TN)
                dvcat = _dot(pr.astype(BF16), do, TN)
                dk_t, dv_t = (dkp_s, dvp_s) if from_prev else (dkc_s, dvc_s)
                dk_t[prows, :] += dkcat[:ATTN_BLOCK, :]
                dv_t[prows, :] += dvcat[:ATTN_BLOCK, :]
                dkc_s[rows, :] += dkcat[ATTN_BLOCK:, :]
                dvc_s[rows, :] += dvcat[ATTN_BLOCK:, :]
            dq_ref[...] = dq_s[...].astype(BF16)
            dk_ref[...] = dkp_s[...].astype(BF16)
            dv_ref[...] = dvp_s[...].astype(BF16)

        @pl.when(i == nsb)
        def _():
            dk_ref[...] = dkc_s[...].astype(BF16)
            dv_ref[...] = dvc_s[...].astype(BF16)

    sh = jax.ShapeDtypeStruct((S, cfg.AW), BF16)
    acc = pltpu.VMEM((SB, E), F32)
    return pl.pallas_call(
        body, name=name, grid=(AH, nsb + 1),
        in_specs=[spec(0, False), spec(cfg.k0, True), spec(cfg.k0, False), spec(cfg.v0, True), spec(cfg.v0, False),
                  spec(cfg.za0, False), cur, cur,
                  pl.BlockSpec((SB, 1), lambda h, i: (h * nsb + jnp.minimum(i, last), 0)),
                  pl.BlockSpec((1, 8, LANES), lambda h, i: (h, 0, 0))],
        out_specs=[cur, prev, prev, cur], out_shape=[sh] * 4,
        scratch_shapes=[acc, pltpu.VMEM((SB, 1), F32), acc, acc, acc, acc, acc,
                        pltpu.VMEM((2 * len(cfg.patterns), ATTN_BLOCK, ATTN_BLOCK), F32)],
        compiler_params=_cp(("parallel", "arbitrary")),
    )(proj, proj, proj, proj, proj, proj, do_a, o_mix, ltot, slopes)


HALO = 8


def _conv_taps(x_ref, h_ref, kc):
    x = x_ref[...]
    full = jnp.concatenate([jnp.where(pl.program_id(1) == 0, 0.0, h_ref[...]), x], axis=0)
    return [pltpu.roll(full, s, axis=0)[HALO:, :] for s in range(kc - 1, 0, -1)] + [x]


def _conv_pre(taps, w_ref, b_ref):
    pre = b_ref[...] + w_ref[0:1, :] * taps[0]
    for k in range(1, len(taps)):
        pre = pre + w_ref[k:k + 1, :] * taps[k]
    return pre


def conv_fwd(proj, w, b, cfg, name, tm=1024, tc=512):
    S, CD, KC = cfg.S, cfg.CD, cfg.KC
    tc = min(tc, CD)
    c0 = _blk(cfg.xbc0, tc)
    hb = tm // HALO

    def body(x_ref, h_ref, w_ref, b_ref, o_ref):
        pre = _conv_pre(_conv_taps(x_ref, h_ref, KC), w_ref, b_ref)
        o_ref[...] = pre * _sigmoid(pre)

    return pl.pallas_call(
        body, name=name, grid=(CD // tc, S // tm),
        in_specs=[pl.BlockSpec((tm, tc), lambda c, i: (i, c0 + c)),
                  pl.BlockSpec((HALO, tc), lambda c, i: (jnp.maximum(i * hb - 1, 0), c0 + c)),
                  pl.BlockSpec((KC, tc), lambda c, i: (0, c)),
                  pl.BlockSpec((1, tc), lambda c, i: (0, c))],
        out_specs=pl.BlockSpec((tm, tc), lambda c, i: (i, c)),
        out_shape=jax.ShapeDtypeStruct((S, CD), F32),
        compiler_params=_cp(("parallel", "arbitrary")),
    )(proj, proj, w, b)


def conv_bwd_a(proj, dxc, w, b, cfg, name, c_off, tm=1024, tc=512):
    S, KC = cfg.S, cfg.KC
    CD = dxc.shape[1]
    tc = min(tc, CD)
    c0 = _blk(cfg.xbc0 + c_off, tc)
    w0 = _blk(c_off, tc)
    hb = tm // HALO

    def body(x_ref, h_ref, d_ref, w_ref, b_ref, dp_ref, gw_ref, gb_ref):
        taps = _conv_taps(x_ref, h_ref, KC)
        pre = _conv_pre(taps, w_ref, b_ref)
        s = _sigmoid(pre)
        dpre = d_ref[...] * (s * (1.0 + pre * (1.0 - s)))
        dp_ref[...] = dpre
        gb = jnp.sum(dpre, axis=0, keepdims=True)
        gws = [jnp.sum(dpre * taps[k], axis=0, keepdims=True) for k in range(KC)]
        gw = jnp.concatenate(gws + [jnp.zeros((8 - KC, tc), F32)], axis=0)

        @pl.when(pl.program_id(1) == 0)
        def _():
            gw_ref[...] = gw
            gb_ref[...] = gb

        @pl.when(pl.program_id(1) > 0)
        def _():
            gw_ref[...] += gw
            gb_ref[...] += gb

    return pl.pallas_call(
        body, name=name, grid=(CD // tc, S // tm),
        in_specs=[pl.BlockSpec((tm, tc), lambda c, i: (i, c0 + c)),
                  pl.BlockSpec((HALO, tc), lambda c, i: (jnp.maximum(i * hb - 1, 0), c0 + c)),
                  pl.BlockSpec((tm, tc), lambda c, i: (i, c)),
                  pl.BlockSpec((KC, tc), lambda c, i: (0, w0 + c)),
                  pl.BlockSpec((1, tc), lambda c, i: (0, w0 + c))],
        out_specs=[pl.BlockSpec((tm, tc), lambda c, i: (i, c)),
                   pl.BlockSpec((8, tc), lambda c, i: (0, c)),
                   pl.BlockSpec((1, tc), lambda c, i: (0, c))],
        out_shape=[jax.ShapeDtypeStruct((S, CD), F32), jax.ShapeDtypeStruct((8, CD), F32),
                   jax.ShapeDtypeStruct((1, CD), F32)],
        compiler_params=_cp(("parallel", "arbitrary")),
    )(proj, proj, dxc, w, b)


def conv_bwd_b(dpre, w, cfg, name, c_off, tm=1024, tc=512):
    S, KC = cfg.S, cfg.KC
    CD = dpre.shape[1]
    tc = min(tc, CD)
    w0 = _blk(c_off, tc)
    hb = tm // HALO
    nrb = S // tm
    last_h = S // HALO - 1

    def body(d_ref, h_ref, w_ref, o_ref):
        d = d_ref[...]
        full = jnp.concatenate([d, jnp.where(pl.program_id(1) == nrb - 1, 0.0, h_ref[...])], axis=0)
        acc = w_ref[KC - 1:KC, :] * d
        for j in range(1, KC):
            acc = acc + w_ref[KC - 1 - j:KC - j, :] * pltpu.roll(full, tm + HALO - j, axis=0)[:tm, :]
        o_ref[...] = acc.astype(BF16)

    return pl.pallas_call(
        body, name=name, grid=(CD // tc, nrb),
        in_specs=[pl.BlockSpec((tm, tc), lambda c, i: (i, c)),
                  pl.BlockSpec((HALO, tc), lambda c, i: (jnp.minimum((i + 1) * hb, last_h), c)),
                  pl.BlockSpec((KC, tc), lambda c, i: (0, w0 + c))],
        out_specs=pl.BlockSpec((tm, tc), lambda c, i: (i, c)),
        out_shape=jax.ShapeDtypeStruct((S, CD), BF16),
        compiler_params=_cp(("parallel", "arbitrary")),
    )(dpre, dpre, w)


def _pad_lanes(v, width=LANES):
    return jnp.pad(v, ((0, 0), (0, width - v.shape[1])))


def ssd_prep(dt_raw, dt_bias, a_log, cfg, name):
    S, L = cfg.S, cfg.L

    def body(x_ref, b_ref, al_ref, dt_ref, ac_ref):
        x = x_ref[...] + b_ref[...]
        dt = jnp.maximum(x, 0.0) + jnp.log(1.0 + jnp.exp(-jnp.abs(x)))
        da = dt * (-jnp.exp(al_ref[...]))
        li = lax.broadcasted_iota(jnp.int32, (L, L), 0)
        si = lax.broadcasted_iota(jnp.int32, (L, L), 1)
        tri = jnp.where(li >= si, 1.0, 0.0).astype(F32)
        dt_ref[...] = dt
        ac_ref[...] = lax.dot_general(tri, da, ((NN), ((), ())), precision=lax.Precision.HIGHEST,
                                      preferred_element_type=F32)

    row = pl.BlockSpec((L, LANES), lambda i: (i, 0))
    vec = pl.BlockSpec((1, LANES), lambda i: (0, 0))
    sh = jax.ShapeDtypeStruct((S, LANES), F32)
    return pl.pallas_call(
        body, name=name, grid=(S // L,), in_specs=[row, vec, vec], out_specs=[row, row], out_shape=[sh, sh],
        compiler_params=_cp(("parallel",)),
    )(dt_raw, dt_bias, a_log)


def _spread(v, n):
    return jnp.broadcast_to(v[:, :, None], v.shape + (n,)).reshape(v.shape[0], v.shape[1] * n)


def _head_selectors(cfg):
    def sel(width):
        head = jnp.arange(LANES)[None, :, None]
        slot = jnp.arange(cfg.SG)[:, None, None] * cfg.HPG + (jnp.arange(cfg.HPG * width) // width)[None, None, :]
        return (head == slot).astype(BF16)
    return sel(cfg.P), sel(LANES)


def _spread_heads(v, sel):
    n = v.shape[0]
    hi = v.astype(BF16)
    r1 = v - hi.astype(F32)
    mid = r1.astype(BF16)
    lo = (r1 - mid.astype(F32)).astype(BF16)
    out = _dot(jnp.concatenate([hi, mid, lo], axis=0), sel, NN)
    return out[:n] + out[n:2 * n] + out[2 * n:]


def _pair_lanes(wide, hpg, p):
    low = lax.broadcasted_iota(jnp.int32, (wide.shape[0], LANES), 1) < p
    return jnp.concatenate([jnp.where(low, wide[:, 2 * jp * LANES:(2 * jp + 1) * LANES],
                                      wide[:, (2 * jp + 1) * LANES:(2 * jp + 2) * LANES])
                            for jp in range(hpg // 2)], axis=1)


def _pair_select(halves, p):
    low = lax.broadcasted_iota(jnp.int32, halves[0].shape, 1) < p
    return jnp.where(low, halves[0], halves[1])


def _head_rows(row, hpg, p):
    return jnp.concatenate([jnp.broadcast_to(row[:, j * LANES:(j + 1) * LANES], (p, LANES)) for j in range(hpg)],
                           axis=0)


def _segment_sums(t, sel):
    r = t.shape[0]
    hi = t.astype(BF16)
    out = _dot(jnp.concatenate([hi, (t - hi.astype(F32)).astype(BF16)], axis=0), sel, NT)
    return out[:r] + out[r:]


def ssd_scan_fwd(xc, dt, acum, act, sel_p, sel_l, cfg, name):
    S, L, P, SN, HPG, SG, SI = cfg.S, cfg.L, cfg.P, cfg.SN, cfg.HPG, cfg.SG, cfg.SI
    nc = S // L
    GW = HPG * P
    bcol, ccol = _blk(SI, SN), _blk(SI + cfg.GN, SN)
    assert nc % CPS == 0

    def body(xs_ref, b_ref, c_ref, dtn_ref, acn_ref, at_ref, sp_ref, sl_ref, y_ref, st_ref, st):
        @pl.when(pl.program_id(1) == 0)
        def _():
            st[...] = jnp.zeros_like(st)

        causal = lax.broadcasted_iota(jnp.int32, (L, L), 0) >= lax.broadcasted_iota(jnp.int32, (L, L), 1)
        for ci in range(CPS):
            rows = slice(ci * L, (ci + 1) * L)
            acn = acn_ref[rows, :]
            dts = _spread_heads(dtn_ref[rows, :], sp_ref[0])
            acs = _spread_heads(acn, sl_ref[0])
            a_p = _pair_lanes(acs, HPG, P)
            s0 = st[...]
            st_ref[ci] = s0.reshape(HPG, P, SN)
            B = b_ref[rows, :].astype(BF16)
            C = c_ref[rows, :].astype(BF16)
            G = _dot(C, B, NT)
            xdt = xs_ref[rows, :] * dts
            xdtb = xdt.astype(BF16)
            ws = jnp.exp(a_p[L - 1:L, :] - a_p)
            yo = jnp.exp(a_p) * _dot(C, s0.astype(BF16), NT)
            yd = []
            for jp in range(HPG // 2):
                x_pair = xdtb[:, jp * LANES:(jp + 1) * LANES]
                halves = []
                for j in (2 * jp, 2 * jp + 1):
                    dm = jnp.where(causal, jnp.exp(acs[:, j * LANES:(j + 1) * LANES] - at_ref[j:j + 1, rows]), 0.0)
                    halves.append(_dot((G * dm).astype(BF16), x_pair, NN))
                yd.append(_pair_select(halves, P))
            y_ref[rows, :] = jnp.concatenate(yd, axis=1) + yo
            st[...] = _head_rows(jnp.exp(acs[L - 1:L, :]), HPG, P) * s0 + _dot((xdt * ws).astype(BF16), B, TN)

    R = CPS * L
    y, states = pl.pallas_call(
        body, name=name, grid=(SG, nc // CPS),
        in_specs=[pl.BlockSpec((R, GW), lambda g, c: (c, g)),
                  pl.BlockSpec((R, SN), lambda g, c: (c, bcol + g)),
                  pl.BlockSpec((R, SN), lambda g, c: (c, ccol + g)),
                  pl.BlockSpec((R, LANES), lambda g, c: (c, 0)),
                  pl.BlockSpec((R, LANES), lambda g, c: (c, 0)),
                  pl.BlockSpec((HPG, R), lambda g, c: (g, c)),
                  pl.BlockSpec((1, LANES, GW), lambda g, c: (g, 0, 0)),
                  pl.BlockSpec((1, LANES, HPG * LANES), lambda g, c: (g, 0, 0))],
        out_specs=[pl.BlockSpec((R, GW), lambda g, c: (c, g)),
                   pl.BlockSpec((CPS, HPG, P, SN), lambda g, c: (c, g, 0, 0))],
        out_shape=[jax.ShapeDtypeStruct((S, SI), F32), jax.ShapeDtypeStruct((nc, cfg.SH, P, SN), F32)],
        scratch_shapes=[pltpu.VMEM((GW, SN), F32)],
        compiler_params=_cp(("parallel", "arbitrary")),
    )(xc, xc, xc, dt, acum, act, sel_p, sel_l)
    return y, states


def ssd_scan_bwd(xc, dt, acum, act, sel_p, sel_l, states, y, dy, dvec, cfg, name, side):
    S, L, P, SN, HPG, SG, SI = cfg.S, cfg.L, cfg.P, cfg.SN, cfg.HPG, cfg.SG, cfg.SI
    nc = S // L
    GW = HPG * P
    bcol, ccol = _blk(SI, SN), _blk(SI + cfg.GN, SN)

    def rc(c):
        return nc // CPS - 1 - c

    def body(xs_ref, b_ref, c_ref, dtn_ref, acn_ref, at_ref, sp_ref, sl_ref, st_ref, y_ref, dy_ref, dk_ref,
             dxs_ref, db_ref, dc_ref, dac_ref, dxsum_ref, dst):
        @pl.when(pl.program_id(1) == 0)
        def _():
            dst[...] = jnp.zeros_like(dst)

        sel = sp_ref[0]
        causal = lax.broadcasted_iota(jnp.int32, (L, L), 0) >= lax.broadcasted_iota(jnp.int32, (L, L), 1)
        low = lax.broadcasted_iota(jnp.int32, (L, LANES), 1) < P
        is_last = lax.broadcasted_iota(jnp.int32, (L, LANES), 0) == L - 1
        ones = jnp.ones((16, SN), BF16)
        for ci in reversed(range(CPS)):
            rows = slice(ci * L, (ci + 1) * L)
            acn = acn_ref[rows, :]
            dts = _spread_heads(dtn_ref[rows, :], sel)
            acs = _spread_heads(acn, sl_ref[0])
            a_p = _pair_lanes(acs, HPG, P)
            B = b_ref[rows, :].astype(BF16)
            C = c_ref[rows, :].astype(BF16)
            G = _dot(C, B, NT)
            xs = xs_ref[rows, :]
            dY = dy_ref[rows, :].astype(F32)
            xdt = xs * dts
            xdtb = xdt.astype(BF16)
            dYb = dY.astype(BF16)
            s0 = st_ref[ci].reshape(GW, SN)
            s0b = s0.astype(BF16)
            ds1 = dst[...]
            ds1b = ds1.astype(BF16)
            ws = jnp.exp(a_p[L - 1:L, :] - a_p)
            dR = (jnp.exp(a_p) * dY).astype(BF16)
            dX2 = ws * _dot(B, ds1b, NT)
            dgsum = jnp.zeros((L, L), F32)
            dX1, yd = [], []
            for jp in range(HPG // 2):
                lanes = slice(jp * LANES, (jp + 1) * LANES)
                x_pair, dy_pair = xdtb[:, lanes], dYb[:, lanes]
                h1, h2 = [], []
                for h, j in enumerate((2 * jp, 2 * jp + 1)):
                    dm = jnp.where(causal, jnp.exp(acs[:, j * LANES:(j + 1) * LANES] - at_ref[j:j + 1, rows]), 0.0)
                    mine = low if h == 0 else jnp.logical_not(low)
                    dgsum = dgsum + _dot(jnp.where(mine, dy_pair, jnp.zeros_like(dy_pair)), x_pair, NT) * dm
                    Mb = (G * dm).astype(BF16)
                    h1.append(_dot(Mb, dy_pair, TN))
                    h2.append(_dot(Mb, x_pair, NN))
                dX1.append(_pair_select(h1, P))
                yd.append(_pair_select(h2, P))
            dX1 = jnp.concatenate(dX1, axis=1)
            dX = dX1 + dX2
            pair = (dYb.astype(F32) - dY) * jnp.concatenate(yd, axis=1) - xdtb.astype(F32) * dX1
            through = _segment_sums(xdt * dX2, sel)
            u = ds1 * s0
            u_hi = u.astype(BF16)
            u_rows = _dot(ones, jnp.concatenate([u_hi, (u - u_hi.astype(F32)).astype(BF16)], axis=0), NT)
            u_rows = u_rows[:, :GW] + u_rows[:, GW:]
            at_end = jnp.exp(acn[L - 1:L, :]) * _segment_sums(u_rows, sel)[0:1, :] + \
                jnp.sum(through, axis=0, keepdims=True)
            dac_ref[0, rows, :] = _segment_sums(dY * y_ref[rows, :] + pair, sel) - through + \
                jnp.where(is_last, at_end, 0.0)
            dxsum_ref[0, rows, :] = _segment_sums(dX * xs, sel)
            dxs_ref[rows, :] = dX * dts + dk_ref[...] * dY
            dst[...] = _head_rows(jnp.exp(acs[L - 1:L, :]), HPG, P) * ds1 + _dot(dR, C, TN)
            dgb = dgsum.astype(BF16)
            dc_ref[rows, :] = _dot(dR, s0b, NN) + _dot(dgb, B, NN)
            db_ref[rows, :] = _dot((xdt * ws).astype(BF16), ds1b, NN) + _dot(dgb, C, TN)

    R = CPS * L
    wide = pl.BlockSpec((R, GW), lambda g, c: (rc(c), g))
    colspec = pl.BlockSpec((1, R, LANES), lambda g, c: (g, rc(c), 0))
    whole = pl.BlockSpec(memory_space=pl.ANY)
    arrs, gathers = side
    grid = (SG, nc // CPS)
    res = pl.pallas_call(
        with_exchange(body, 12, 5, gathers, grid), name=name, grid=grid,
        in_specs=[wide,
                  pl.BlockSpec((R, SN), lambda g, c: (rc(c), bcol + g)),
                  pl.BlockSpec((R, SN), lambda g, c: (rc(c), ccol + g)),
                  pl.BlockSpec((R, LANES), lambda g, c: (rc(c), 0)),
                  pl.BlockSpec((R, LANES), lambda g, c: (rc(c), 0)),
                  pl.BlockSpec((HPG, R), lambda g, c: (g, rc(c))),
                  pl.BlockSpec((1, LANES, GW), lambda g, c: (g, 0, 0)),
                  pl.BlockSpec((1, LANES, HPG * LANES), lambda g, c: (g, 0, 0)),
                  pl.BlockSpec((CPS, HPG, P, SN), lambda g, c: (rc(c), g, 0, 0)),
                  wide, wide,
                  pl.BlockSpec((1, GW), lambda g, c: (0, g))] + [whole] * len(arrs),
        out_specs=[wide,
                   pl.BlockSpec((R, SN), lambda g, c: (rc(c), g)),
                   pl.BlockSpec((R, SN), lambda g, c: (rc(c), g)),
                   colspec, colspec] + [whole] * len(arrs),
        out_shape=[jax.ShapeDtypeStruct((S, SI), F32), jax.ShapeDtypeStruct((S, cfg.GN), F32),
                   jax.ShapeDtypeStruct((S, cfg.GN), F32),
                   jax.ShapeDtypeStruct((SG, S, LANES), F32), jax.ShapeDtypeStruct((SG, S, LANES), F32)] +
        _exchange_shapes(arrs, gathers),
        scratch_shapes=[pltpu.VMEM((GW, SN), F32)] + _exchange_sems(len(arrs)),
        compiler_params=_cp(("arbitrary", "arbitrary")),
    )(xc, xc, xc, dt, acum, act, sel_p, sel_l, states, y, dy, dvec, *arrs)
    return res[:5], res[5:]


def dt_bwd(dac, dxsum, dt_raw, dt, dt_bias, a_log, cfg, name):
    S, L, SG = cfg.S, cfg.L, cfg.SG

    def body(da_ref, dx_ref, x_ref, dt_ref, b_ref, al_ref, o_ref, gb_ref, ga_ref):
        a = -jnp.exp(al_ref[...])
        dtv = dt_ref[...]
        dxs = jnp.sum(dx_ref[...], axis=0)
        upper = jnp.where(lax.broadcasted_iota(jnp.int32, (L, L), 1) >= lax.broadcasted_iota(jnp.int32, (L, L), 0),
                          1.0, 0.0).astype(F32)
        dda = lax.dot_general(upper, jnp.sum(da_ref[...], axis=0), (NN, ((), ())), precision=lax.Precision.HIGHEST,
                              preferred_element_type=F32)
        draw = (dxs + dda * a) * _sigmoid(x_ref[...] + b_ref[...])
        o_ref[...] = draw.astype(BF16)
        gb = jnp.sum(draw, axis=0, keepdims=True)
        ga = jnp.sum(dda * dtv, axis=0, keepdims=True) * a

        @pl.when(pl.program_id(0) == 0)
        def _():
            gb_ref[...] = gb
            ga_ref[...] = ga

        @pl.when(pl.program_id(0) > 0)
        def _():
            gb_ref[...] += gb
            ga_ref[...] += ga

    row = pl.BlockSpec((L, LANES), lambda i: (i, 0))
    vec = pl.BlockSpec((1, LANES), lambda i: (0, 0))
    return pl.pallas_call(
        body, name=name, grid=(S // L,),
        in_specs=[pl.BlockSpec((SG, L, LANES), lambda i: (0, i, 0))] * 2 + [row, row, vec, vec],
        out_specs=[row, vec, vec],
        out_shape=[jax.ShapeDtypeStruct((S, LANES), BF16), jax.ShapeDtypeStruct((1, LANES), F32),
                   jax.ShapeDtypeStruct((1, LANES), F32)],
        compiler_params=_cp(("arbitrary",)),
    )(dac, dxsum, dt_raw, dt, dt_bias, a_log)


def gated_norm_fwd(y, xc, proj, dvec, nw, cfg, name, tm=128):
    S, SI = cfg.S, cfg.SI

    def body(y_ref, xs_ref, z_ref, d_ref, w_ref, o_ref):
        z = z_ref[...]
        yg = (y_ref[...] + d_ref[...] * xs_ref[...]) * (z * _sigmoid(z))
        r = lax.rsqrt(jnp.mean(yg * yg, axis=-1, keepdims=True) + RMS_EPS)
        o_ref[...] = ((yg * r) * w_ref[...]).astype(BF16)

    row = pl.BlockSpec((tm, SI), lambda i: (i, 0))
    vec = pl.BlockSpec((1, SI), lambda i: (0, 0))
    return pl.pallas_call(
        body, name=name, grid=(S // tm,),
        in_specs=[row, row, pl.BlockSpec((tm, SI), lambda i: (i, _blk(cfg.zs0, SI))), vec, vec],
        out_specs=row, out_shape=jax.ShapeDtypeStruct((S, SI), BF16),
        compiler_params=_cp(("parallel",)),
    )(y, xc, proj, dvec, nw)


def gated_norm_bwd(dyn, y, xc, proj, dvec, nw, cfg, name, tm=128):
    S, SI = cfg.S, cfg.SI

    def body(dn_ref, y_ref, xs_ref, z_ref, d_ref, w_ref, dy_ref, dz_ref, gw_ref, gd_ref):
        z = z_ref[...]
        s = _sigmoid(z)
        sz = z * s
        xs = xs_ref[...]
        yf = y_ref[...] + d_ref[...] * xs
        yg = yf * sz
        r = lax.rsqrt(jnp.mean(yg * yg, axis=-1, keepdims=True) + RMS_EPS)
        dn = dn_ref[...].astype(F32)
        g = dn * w_ref[...]
        dyg = r * g - yg * (r * r * r) * jnp.mean(g * yg, axis=-1, keepdims=True)
        dy = dyg * sz
        dy_ref[...] = dy.astype(BF16)
        dz_ref[...] = (dyg * yf * (s * (1.0 + z * (1.0 - s)))).astype(BF16)
        gw = jnp.sum(dn * (yg * r), axis=0, keepdims=True)
        gd = jnp.sum(dy * xs, axis=0, keepdims=True)

        @pl.when(pl.program_id(0) == 0)
        def _():
            gw_ref[...] = gw
            gd_ref[...] = gd

        @pl.when(pl.program_id(0) > 0)
        def _():
            gw_ref[...] += gw
            gd_ref[...] += gd

    row = pl.BlockSpec((tm, SI), lambda i: (i, 0))
    vec = pl.BlockSpec((1, SI), lambda i: (0, 0))
    return pl.pallas_call(
        body, name=name, grid=(S // tm,),
        in_specs=[row, row, row, pl.BlockSpec((tm, SI), lambda i: (i, _blk(cfg.zs0, SI))), vec, vec],
        out_specs=[row, row, vec, vec],
        out_shape=[jax.ShapeDtypeStruct((S, SI), BF16), jax.ShapeDtypeStruct((S, SI), BF16),
                   jax.ShapeDtypeStruct((1, SI), F32), jax.ShapeDtypeStruct((1, SI), F32)],
        compiler_params=_cp(("arbitrary",)),
    )(dyn, y, xc, proj, dvec, nw)


def _shard_columns(cfg, main, dt):
    dt0 = 4 * cfg.AW + cfg.SI + cfg.CD
    ws = cfg.N_IN // N_DEV
    out = []
    for k in range(N_DEV):
        lo, hi, parts = k * ws, (k + 1) * ws, []
        if lo < dt0:
            parts.append((main, lo, min(hi, dt0)))
        if lo < dt0 + cfg.SH and hi > dt0:
            parts.append((dt, max(lo, dt0) - dt0, min(hi, dt0 + cfg.SH) - dt0))
        if hi > dt0 + cfg.SH:
            parts.append((main, max(lo, dt0 + cfg.SH) - cfg.SH, hi - cfg.SH))
        out.append(parts)
    return out


def local_step(cfg, x, tgt, norm_w, conv_w, conv_b, dt_bias, a_log, d_skip, ssm_norm_w, final_norm_w,
               w_main, w_dt, shards, dt0):
    S, D = cfg.S, cfg.D
    slopes = _slope_table(cfg)
    dt_bias_p = _pad_lanes(dt_bias)
    a_log_p = _pad_lanes(a_log)
    dvec = _spread(d_skip, cfg.P)

    hn = rmsnorm_fwd(x, norm_w, "rmsnorm_fwd")
    proj, gathered = matmul(hn, w_main, 'nn', 1024, 2048, 2048, F32, "in_proj", side=(shards, [True] * 3))
    w_attn, w_ssm, w_out = gathered[0].reshape(cfg.AW, D), gathered[1].reshape(cfg.SI, D), gathered[2].reshape(D, D)
    dt_raw = matmul(hn, w_dt, 'nn', 512, 128, 2048, F32, "in_proj_dt")
    o_a, o_mix, ltot = attn_fused_fwd(proj, slopes, cfg, "attn_fwd")
    xc = conv_fwd(proj, conv_w, conv_b, cfg, "conv_fwd")
    dt, acum = ssd_prep(dt_raw, dt_bias_p, a_log_p, cfg, "ssd_prep")
    act = acum[:, :cfg.SH].T
    sel_p, sel_l = _head_selectors(cfg)
    y, states = ssd_scan_fwd(xc, dt, acum, act, sel_p, sel_l, cfg, "ssd_scan_fwd")
    y_n = gated_norm_fwd(y, xc, proj, dvec, ssm_norm_w, cfg, "gated_norm_fwd")
    a_out, s_out, merged = branch_merge(o_a, w_attn, y_n, w_ssm, proj, cfg, "branch_merge")
    dout, loss_p, g_final_w = out_proj_final(merged, w_out, x, final_norm_w.reshape(1, D), tgt, "out_proj_final")

    g_w_out = matmul(merged, dout, 'tn', 1024, 1024, 2048, BF16, "g_w_out")
    da_out, ds_out, dga, dgs = merge_bwd(dout, w_out, a_out, s_out, proj, cfg, "merge_bwd")
    g_w_attn = matmul(o_a, da_out, 'tn', 1024, 1024, 2048, BF16, "g_w_attn")
    g_w_ssm = matmul(y_n, ds_out, 'tn', 1024, 1024, 2048, BF16, "g_w_ssm")
    do_a = matmul(da_out, w_attn, 'nt', 512, 1024, 2048, BF16, "d_o_a")
    dyn = matmul(ds_out, w_ssm, 'nt', 512, 1024, 2048, BF16, "d_y_n")
    dy, dz_s, g_ssm_norm, g_dvec = gated_norm_bwd(dyn, y, xc, proj, dvec, ssm_norm_w, cfg, "gated_norm_bwd")
    sends = [g.reshape((N_DEV, g.shape[0] // N_DEV, D)) for g in (g_w_attn, g_w_ssm, g_w_out)]
    (dxs, dB, dC, dac_g, dxsum_g), (r_attn, r_ssm, r_out) = ssd_scan_bwd(
        xc, dt, acum, act, sel_p, sel_l, states, y, dy, dvec, cfg, "ssd_scan_bwd", side=(sends, [False] * 3))
    ddt_raw, g_dt_bias, g_a_log = dt_bwd(dac_g, dxsum_g, dt_raw, dt, dt_bias_p, a_log_p, cfg, "dt_bwd")
    dxbc, g_cw, g_cb = [], [], []
    for nm, piece, c_off in (("xs", dxs, 0), ("b", dB, cfg.SI), ("c", dC, cfg.SI + cfg.GN)):
        dpre, gw, gb = conv_bwd_a(proj, piece, conv_w, conv_b, cfg, "conv_bwd_a_" + nm, c_off)
        dxbc.append(conv_bwd_b(dpre, conv_w, cfg, "conv_bwd_b_" + nm, c_off))
        g_cw.append(gw)
        g_cb.append(gb)
    g_conv_w, g_conv_b = jnp.concatenate(g_cw, axis=1), jnp.concatenate(g_cb, axis=1)
    dq, dk, dv, dz_a = attn_fused_bwd(proj, do_a, o_mix, ltot, slopes, cfg, "attn_bwd")
    dproj = jnp.concatenate([dq, dk, dv, dz_a, dz_s] + dxbc + [dga, dgs], axis=1)
    g_w_main = matmul(hn, dproj, 'tn', 1024, 2048, 2048, BF16, "g_w_main")
    g_w_dt = matmul(hn, ddt_raw, 'tn', 1024, 128, 2048, BF16, "g_w_dt")
    send_in = jnp.stack([jnp.concatenate([g[:, lo:hi] for g, lo, hi in parts], axis=1)
                         for parts in _shard_columns(cfg, g_w_main, g_w_dt)])
    dhn_a, (r_in,) = matmul(dproj, w_main, 'nt', 1024, 1024, 2048, F32, "d_hn", side=([send_in], [False]))
    dhn_b = matmul(ddt_raw, w_dt, 'nt', 512, 1024, 128, F32, "d_hn_dt")
    grad_x, g_norm_w = rmsnorm_bwd(dhn_a, dhn_b, x, norm_w, dout, "rmsnorm_bwd")

    g_d_skip = jnp.sum(g_dvec.reshape(cfg.SH, cfg.P), axis=1).reshape(1, cfg.SH)
    small = dict(norm_w=g_norm_w, conv_b=g_conv_b, dt_bias=g_dt_bias[:, :cfg.SH], a_log=g_a_log[:, :cfg.SH],
                 d_skip=g_d_skip, ssm_norm_w=g_ssm_norm, final_norm_w=g_final_w, conv_w=g_conv_w[:cfg.KC])
    return loss_p, grad_x, small, dict(w_in=r_in, w_attn=r_attn, w_ssm=r_ssm, w_out=r_out)


def _mesh_pos():
    return lax.axis_index("x"), lax.axis_index("y"), lax.axis_index("c")


def _flat(pos):
    return 4 * pos[0] + 2 * pos[1] + pos[2]


def _exchange_shapes(arrs, gathers):
    return [jax.ShapeDtypeStruct(((N_DEV,) + a.shape) if g else a.shape, a.dtype) for a, g in zip(arrs, gathers)]


def _exchange_sems(n):
    return [pltpu.SemaphoreType.DMA((n * (N_DEV - 1),)), pltpu.SemaphoreType.DMA((n * (N_DEV - 1),)),
            pltpu.SemaphoreType.DMA((n,))]


def _exchange_copies(ins, outs, gathers, send_sems, recv_sems, loc_sems):
    pos = _mesh_pos()
    me = _flat(pos)
    starts, waits = [], []
    for a in range(len(ins)):
        mine = ins[a] if gathers[a] else ins[a].at[me]
        loc = pltpu.make_async_copy(mine, outs[a].at[me], loc_sems.at[a])
        starts.append(loc)
        waits.append(loc)
        for k in range(1, N_DEV):
            flip = ((k >> 2) & 1, (k >> 1) & 1, k & 1)
            peer = tuple(1 - p if f else p for p, f in zip(pos, flip))
            pk = _flat(peer)
            src = ins[a] if gathers[a] else ins[a].at[pk]
            sems = dict(send_sem=send_sems.at[a * (N_DEV - 1) + k - 1], recv_sem=recv_sems.at[a * (N_DEV - 1) + k - 1],
                        device_id=peer, device_id_type=pl.DeviceIdType.MESH)
            starts.append(pltpu.make_async_remote_copy(src_ref=src, dst_ref=outs[a].at[me], **sems))
            waits.append(pltpu.make_async_remote_copy(src_ref=src, dst_ref=outs[a].at[pk], **sems))
    return starts, waits


def exchange(arrs, gathers, name):
    n = len(arrs)

    def body(*refs):
        starts, waits = _exchange_copies(refs[:n], refs[n:2 * n], gathers, *refs[2 * n:])
        for cp in starts:
            cp.start()
        for cp in waits:
            cp.wait()

    hbm = pl.BlockSpec(memory_space=pltpu.HBM)
    return pl.pallas_call(
        body, name=name, in_specs=[hbm] * n, out_specs=[hbm] * n, out_shape=_exchange_shapes(arrs, gathers),
        scratch_shapes=_exchange_sems(n),
    )(*arrs)


def gather_two_level(arrs, chunks, name):
    n = len(arrs)

    def body(*refs):
        ins, outs = refs[:n], refs[n:2 * n]
        send_sems, recv_sems, loc_sems = refs[2 * n:]
        x, y, c = _mesh_pos()
        me, sib = (x, y, c), (x, y, 1 - c)
        chips = [(1 - x, y), (x, 1 - y), (1 - x, 1 - y)]
        plan, base = [], 0
        for a in range(n):
            step = arrs[a].shape[0] // chunks[a]
            for q in range(chunks[a]):
                plan.append((a, pl.ds(q * step, step), base))
                base += N_DEV - 1

        def copy(a, rows, sem, block, to, own=False):
            dst = outs[a].at[_flat(block), rows]
            return pltpu.make_async_remote_copy(
                src_ref=ins[a].at[rows] if own else dst, dst_ref=dst, send_sem=send_sems.at[sem],
                recv_sem=recv_sems.at[sem], device_id=to, device_id_type=pl.DeviceIdType.MESH)

        local = [pltpu.make_async_copy(ins[a], outs[a].at[_flat(me)], loc_sems.at[a]) for a in range(n)]
        for cp in local:
            cp.start()
        sent = []
        for a, rows, s in plan:
            sent.append(copy(a, rows, s, me, sib, own=True))
            sent += [copy(a, rows, s + 1 + j, me, (*chip, c), own=True) for j, chip in enumerate(chips)]
        for cp in sent:
            cp.start()
        for a, rows, s in plan:
            for j, chip in enumerate(chips):
                copy(a, rows, s + 1 + j, (*chip, c), me).wait_recv()
                passed = copy(a, rows, s + 4 + j, (*chip, c), sib)
                passed.start()
                sent.append(passed)
        for a, rows, s in plan:
            copy(a, rows, s, sib, me).wait_recv()
            for j, chip in enumerate(chips):
                copy(a, rows, s + 4 + j, (*chip, 1 - c), me).wait_recv()
        for cp in sent:
            cp.wait_send()
        for cp in local:
            cp.wait()

    hbm = pl.BlockSpec(memory_space=pltpu.HBM)
    nsem = (N_DEV - 1) * sum(chunks)
    return pl.pallas_call(
        body, name=name, in_specs=[hbm] * n, out_specs=[hbm] * n, out_shape=_exchange_shapes(arrs, [True] * n),
        scratch_shapes=[pltpu.SemaphoreType.DMA((nsem,)), pltpu.SemaphoreType.DMA((nsem,)),
                        pltpu.SemaphoreType.DMA((n,))],
    )(*arrs)


def with_exchange(body, n_in, n_out, gathers, grid):
    n = len(gathers)

    def wrapped(*refs):
        ins, sends = refs[:n_in], refs[n_in:n_in + n]
        outs, recvs = refs[n_in + n:n_in + n + n_out], refs[n_in + 2 * n + n_out - n:n_in + 2 * n + n_out]
        scratch, sems = refs[n_in + 2 * n + n_out:-3], refs[-3:]
        ids = [pl.program_id(d) for d in range(len(grid))]
        first = functools.reduce(jnp.logical_and, [i == 0 for i in ids])
        last = functools.reduce(jnp.logical_and, [i == g - 1 for i, g in zip(ids, grid)])

        @pl.when(first)
        def _():
            for cp in _exchange_copies(sends, recvs, gathers, *sems)[0]:
                cp.start()

        body(*ins, *outs, *scratch)

        @pl.when(last)
        def _():
            for cp in _exchange_copies(sends, recvs, gathers, *sems)[1]:
                cp.wait()

    return wrapped


def adamw(g_src, w, m, v, summed, name, tr=64):
    R, C = w.shape
    tr = min(tr, R)
    assert R % tr == 0

    def body(g_ref, w_ref, m_ref, v_ref, g_out, d_out, m_out, v_out):
        if summed:
            g = g_ref[0].astype(F32)
            for j in range(1, N_DEV):
                g = g + g_ref[j].astype(F32)
        else:
            g = g_ref[...]
        mn = ADAM_B1 * m_ref[...] + (1.0 - ADAM_B1) * g
        vn = ADAM_B2 * v_ref[...] + (1.0 - ADAM_B2) * (g * g)
        m_hat = mn / (1.0 - ADAM_B1 ** ADAM_STEP)
        v_hat = vn / (1.0 - ADAM_B2 ** ADAM_STEP)
        g_out[...] = g
        d_out[...] = -ADAM_LR * (m_hat / (jnp.sqrt(v_hat) + ADAM_EPS) + ADAM_WD * w_ref[...])
        m_out[...] = mn
        v_out[...] = vn

    row = pl.BlockSpec((tr, C), lambda i: (i, 0))
    gspec = pl.BlockSpec((N_DEV, tr, C), lambda i: (0, i, 0)) if summed else row
    sh = jax.ShapeDtypeStruct((R, C), F32)
    return pl.pallas_call(
        body, name=name, grid=(R // tr,), in_specs=[gspec, row, row, row], out_specs=[row] * 4, out_shape=[sh] * 4,
        compiler_params=_cp(("parallel",)),
    )(g_src, w, m, v)


SMALL = ('norm_w', 'conv_b', 'dt_bias', 'a_log', 'd_skip', 'ssm_norm_w', 'final_norm_w')


def _rows(n):
    return -(-n // (8 * LANES)) * 8


def _pack(vals):
    parts = []
    for a in vals:
        f = a.reshape(-1)
        parts.append(jnp.pad(f, (0, _rows(f.size) * LANES - f.size)).reshape(-1, LANES))
    return jnp.concatenate(parts, axis=0)


def _unpack(packed, shapes):
    out, r = [], 0
    for s in shapes:
        n = math.prod(s)
        out.append(packed[r:r + _rows(n)].reshape(-1)[:n].reshape(s))
        r += _rows(n)
    return out


def kernel(x, norm_w, w_in, conv_w, conv_b, dt_bias, a_log, d_skip, ssm_norm_w, w_attn_branch, w_ssm_branch, w_out, final_norm_w, loss_target, m_norm_w, m_w_in, m_conv_w, m_conv_b, m_dt_bias, m_a_log, m_d_skip, m_ssm_norm_w, m_w_attn_branch, m_w_ssm_branch, m_w_out, m_final_norm_w, v_norm_w, v_w_in, v_conv_w, v_conv_b, v_dt_bias, v_a_log, v_d_skip, v_ssm_norm_w, v_w_attn_branch, v_w_ssm_branch, v_w_out, v_final_norm_w):
    cfg = CFG
    D, SH = cfg.D, cfg.SH
    me = _flat(_mesh_pos())
    dt0 = 4 * cfg.AW + cfg.SI + cfg.CD
    ws = w_in.shape[-1]

    g_in, g_cw = gather_two_level([w_in[0].astype(BF16), conv_w[0]], [4, 1], "gather_w_in")
    main_cols, dt_cols = [], []
    for k, parts in enumerate(_shard_columns(cfg, "main", "dt")):
        at = 0
        for which, lo, hi in parts:
            (main_cols if which == "main" else dt_cols).append(g_in[k][:, at:at + hi - lo])
            at += hi - lo
    w_main = jnp.concatenate(main_cols, axis=1)
    w_dt = _pad_lanes(jnp.concatenate(dt_cols, axis=1))
    conv_full = g_cw.transpose(1, 0, 2).reshape(cfg.KC, cfg.CD)
    shards = [w_attn_branch[0].astype(BF16), w_ssm_branch[0].astype(BF16), w_out[0].astype(BF16)]

    loss_p, grad_x, small, recv = local_step(
        cfg, x[0], loss_target[0], norm_w, conv_full, conv_b, dt_bias, a_log, d_skip,
        ssm_norm_w, final_norm_w, w_main, w_dt, shards, dt0)

    upd = {}
    upd['w_in'] = adamw(recv['w_in'], w_in[0], m_w_in[0], v_w_in[0], True, "adamw_w_in")
    upd['w_attn_branch'] = adamw(recv['w_attn'], w_attn_branch[0], m_w_attn_branch[0], v_w_attn_branch[0], True,
                                 "adamw_w_attn")
    upd['w_ssm_branch'] = adamw(recv['w_ssm'], w_ssm_branch[0], m_w_ssm_branch[0], v_w_ssm_branch[0], True,
                                "adamw_w_ssm")
    upd['w_out'] = adamw(recv['w_out'], w_out[0], m_w_out[0], v_w_out[0], True, "adamw_w_out")

    extra = [jnp.zeros((cfg.KC, cfg.CD), F32), jnp.zeros((1, 1), F32)]
    shapes = [small[n].shape for n in SMALL] + [e.shape for e in extra]
    part = _pack([small[n] for n in SMALL] + [small['conv_w'], loss_p[:, :1]])
    gathered, = exchange([part], [True], "gather_small")
    given = dict(norm_w=(norm_w, m_norm_w, v_norm_w), conv_b=(conv_b, m_conv_b, v_conv_b),
                 dt_bias=(dt_bias, m_dt_bias, v_dt_bias), a_log=(a_log, m_a_log, v_a_log),
                 d_skip=(d_skip, m_d_skip, v_d_skip), ssm_norm_w=(ssm_norm_w, m_ssm_norm_w, v_ssm_norm_w),
                 final_norm_w=(final_norm_w, m_final_norm_w, v_final_norm_w))
    packed = [_pack([given[n][t] for n in SMALL] + extra) for t in range(3)]
    outs = adamw(gathered, *packed, True, "adamw_small", tr=part.shape[0])
    unpacked = [_unpack(o, shapes) for o in outs]
    for i, n in enumerate(SMALL):
        upd[n] = [u[i].reshape(given[n][0].shape) for u in unpacked]
    loss = unpacked[0][-1].reshape(())
    cw = conv_w.shape[-1]
    g_cw_mine = lax.dynamic_slice_in_dim(unpacked[0][-2], me * cw, cw, axis=1)
    upd['conv_w'] = adamw(g_cw_mine.reshape(-1, LANES), conv_w.reshape(-1, LANES), m_conv_w.reshape(-1, LANES),
                          v_conv_w.reshape(-1, LANES), False, "adamw_conv_w")

    order = ['norm_w', 'w_in', 'conv_w', 'conv_b', 'dt_bias', 'a_log', 'd_skip', 'ssm_norm_w', 'w_attn_branch',
             'w_ssm_branch', 'w_out', 'final_norm_w']
    like = dict(norm_w=norm_w, w_in=w_in, conv_w=conv_w, conv_b=conv_b, dt_bias=dt_bias, a_log=a_log, d_skip=d_skip,
                ssm_norm_w=ssm_norm_w, w_attn_branch=w_attn_branch, w_ssm_branch=w_ssm_branch, w_out=w_out,
                final_norm_w=final_norm_w)
    result = [loss, grad_x[None]]
    for t in range(4):
        result += [upd[n][t].reshape(like[n].shape) for n in order]
    return tuple(result)
```

```python
import functools
import math
from typing import NamedTuple

import jax
import jax.numpy as jnp
from jax import lax
from jax.experimental import pallas as pl
from jax.experimental.pallas import tpu as pltpu

F32 = jnp.float32
BF16 = jnp.bfloat16
RMS_EPS = 1e-6
NEG = -1e30
N_DEV = 8
CPS = 4
LANES = 128
ATTN_BLOCK = 128
ADAM_LR, ADAM_B1, ADAM_B2, ADAM_EPS, ADAM_WD, ADAM_STEP = 0.001, 0.9, 0.999, 1e-08, 0.01, 10
VMEM_LIMIT = 56 * 1024 * 1024


class Cfg(NamedTuple):
    D: int = 2048
    S: int = 8192
    AH: int = 16
    E: int = 128
    patterns: tuple = ((128, 1), (512, 4), (2048, 16))
    SI: int = 4096
    P: int = 64
    SG: int = 8
    SN: int = 128
    KC: int = 4
    L: int = 128

    @property
    def AW(self): return self.AH * self.E
    @property
    def SH(self): return self.SI // self.P
    @property
    def HPG(self): return self.SH // self.SG
    @property
    def GN(self): return self.SG * self.SN
    @property
    def CD(self): return self.SI + 2 * self.GN
    @property
    def k0(self): return self.AW
    @property
    def v0(self): return 2 * self.AW
    @property
    def za0(self): return 3 * self.AW
    @property
    def zs0(self): return 4 * self.AW
    @property
    def xbc0(self): return 4 * self.AW + self.SI
    @property
    def ga0(self): return self.xbc0 + self.CD
    @property
    def gs0(self): return self.ga0 + self.D
    @property
    def NP(self): return self.gs0 + self.D
    @property
    def N_IN(self): return self.NP + self.SH


CFG = Cfg()


def _cp(sem=None, vmem=VMEM_LIMIT):
    return pltpu.CompilerParams(dimension_semantics=sem, vmem_limit_bytes=vmem)


def _sigmoid(z):
    return 1.0 / (1.0 + jnp.exp(-z))


def _dot(a, b, dims):
    return lax.dot_general(a, b, (dims, ((), ())), preferred_element_type=F32)


NN = ((1,), (0,))
NT = ((1,), (1,))
TN = ((0,), (0,))


def _blk(off, width):
    assert off % width == 0, (off, width)
    return off // width


def matmul(a, b, mode, tm, tn, tk, out_dtype, name, side=None):
    if mode == 'nn':
        (M, K), (_, N) = a.shape, b.shape
    elif mode == 'nt':
        (M, K), (N, _) = a.shape, b.shape
    else:
        (K, M), (_, N) = a.shape, b.shape
    tm, tn, tk = min(tm, M), min(tn, N), min(tk, K)
    assert M % tm == 0 and N % tn == 0 and K % tk == 0, (M, N, K, tm, tn, tk)
    nk = K // tk
    dims = {'nn': NN, 'nt': NT, 'tn': TN}[mode]

    def body(a_ref, b_ref, o_ref, *acc):
        part = _dot(a_ref[...].astype(BF16), b_ref[...].astype(BF16), dims)
        if nk == 1:
            o_ref[...] = part.astype(out_dtype)
        else:
            acc_ref, = acc
            k = pl.program_id(2)

            @pl.when(k == 0)
            def _():
                acc_ref[...] = part

            @pl.when(k > 0)
            def _():
                acc_ref[...] += part

            @pl.when(k == nk - 1)
            def _():
                o_ref[...] = acc_ref[...].astype(out_dtype)

    if mode == 'tn':
        a_spec = pl.BlockSpec((tk, tm), lambda n, m, k: (k, m))
    else:
        a_spec = pl.BlockSpec((tm, tk), lambda n, m, k: (m, k))
    if mode == 'nt':
        b_spec = pl.BlockSpec((tn, tk), lambda n, m, k: (n, k))
    else:
        b_spec = pl.BlockSpec((tk, tn), lambda n, m, k: (k, n))
    grid = (N // tn, M // tm, nk)
    o_spec = pl.BlockSpec((tm, tn), lambda n, m, k: (m, n))
    o_shape = jax.ShapeDtypeStruct((M, N), out_dtype)
    acc = [] if nk == 1 else [pltpu.VMEM((tm, tn), F32)]
    if side is None:
        return pl.pallas_call(
            body, name=name, grid=grid, in_specs=[a_spec, b_spec], out_specs=o_spec, out_shape=o_shape,
            scratch_shapes=acc, compiler_params=_cp(("parallel", "parallel", "arbitrary")),
        )(a, b)
    arrs, gathers = side
    whole = pl.BlockSpec(memory_space=pl.ANY)
    res = pl.pallas_call(
        with_exchange(body, 2, 1, gathers, grid), name=name, grid=grid,
        in_specs=[a_spec, b_spec] + [whole] * len(arrs), out_specs=[o_spec] + [whole] * len(arrs),
        out_shape=[o_shape] + _exchange_shapes(arrs, gathers),
        scratch_shapes=acc + _exchange_sems(len(arrs)),
        compiler_params=_cp(("arbitrary", "arbitrary", "arbitrary")),
    )(a, b, *arrs)
    return res[0], res[1:]


def rmsnorm_fwd(x, w, name, tm=256):
    S, D = x.shape

    def body(x_ref, w_ref, o_ref):
        xv = x_ref[...]
        r = lax.rsqrt(jnp.mean(xv * xv, axis=-1, keepdims=True) + RMS_EPS)
        o_ref[...] = ((xv * r) * w_ref[...]).astype(BF16)

    return pl.pallas_call(
        body, name=name, grid=(S // tm,),
        in_specs=[pl.BlockSpec((tm, D), lambda i: (i, 0)), pl.BlockSpec((1, D), lambda i: (0, 0))],
        out_specs=pl.BlockSpec((tm, D), lambda i: (i, 0)),
        out_shape=jax.ShapeDtypeStruct((S, D), BF16),
        compiler_params=_cp(("parallel",)),
    )(x, w)


def rmsnorm_bwd(dh_a, dh_b, x, w, dout, name, tm=128):
    S, D = x.shape

    def body(da_ref, db_ref, x_ref, w_ref, do_ref, gx_ref, gw_ref):
        xv = x_ref[...]
        dh = da_ref[...] + db_ref[...]
        r = lax.rsqrt(jnp.mean(xv * xv, axis=-1, keepdims=True) + RMS_EPS)
        g = dh * w_ref[...]
        dx = r * g - xv * (r * r * r) * jnp.mean(g * xv, axis=-1, keepdims=True)
        gx_ref[...] = do_ref[...] + dx
        gw = jnp.sum(dh * (xv * r), axis=0, keepdims=True)

        @pl.when(pl.program_id(0) == 0)
        def _():
            gw_ref[...] = gw

        @pl.when(pl.program_id(0) > 0)
        def _():
            gw_ref[...] += gw

    row = pl.BlockSpec((tm, D), lambda i: (i, 0))
    vec = pl.BlockSpec((1, D), lambda i: (0, 0))
    return pl.pallas_call(
        body, name=name, grid=(S // tm,),
        in_specs=[row, row, row, vec, row],
        out_specs=[row, vec],
        out_shape=[jax.ShapeDtypeStruct((S, D), F32), jax.ShapeDtypeStruct((1, D), F32)],
        compiler_params=_cp(("arbitrary",)),
    )(dh_a, dh_b, x, w, dout)


def out_proj_final(merged, w_out, x, fw, tgt, name, tm=256):
    S, D = x.shape

    def body(m_ref, wo_ref, x_ref, w_ref, t_ref, do_ref, loss_ref, gw_ref):
        out = x_ref[...] + _dot(m_ref[...], wo_ref[...], NN)
        w = w_ref[...]
        r = lax.rsqrt(jnp.mean(out * out, axis=-1, keepdims=True) + RMS_EPS)
        yn = out * r
        err = yn * w - t_ref[...]
        lrow = 0.5 * jnp.mean(err * err, axis=-1, keepdims=True)
        lsum = jnp.zeros((1, LANES), F32) + jnp.sum(lrow, axis=0, keepdims=True)
        dfin = err * (1.0 / D)
        g = dfin * w
        do_ref[...] = r * g - out * (r * r * r) * jnp.mean(g * out, axis=-1, keepdims=True)
        gw = jnp.sum(dfin * yn, axis=0, keepdims=True)

        @pl.when(pl.program_id(0) == 0)
        def _():
            gw_ref[...] = gw
            loss_ref[...] = lsum

        @pl.when(pl.program_id(0) > 0)
        def _():
            gw_ref[...] += gw
            loss_ref[...] += lsum

    row = pl.BlockSpec((tm, D), lambda i: (i, 0))
    vec = pl.BlockSpec((1, D), lambda i: (0, 0))
    return pl.pallas_call(
        body, name=name, grid=(S // tm,),
        in_specs=[row, pl.BlockSpec((D, D), lambda i: (0, 0)), row, vec, row],
        out_specs=[row, pl.BlockSpec((1, LANES), lambda i: (0, 0)), vec],
        out_shape=[jax.ShapeDtypeStruct((S, D), F32), jax.ShapeDtypeStruct((1, LANES), F32),
                   jax.ShapeDtypeStruct((1, D), F32)],
        compiler_params=_cp(("arbitrary",)),
    )(merged, w_out, x, fw, tgt)


def branch_merge(o_a, w_attn, y_n, w_ssm, proj, cfg, name, tm=512, tn=512):
    S, D = cfg.S, cfg.D
    tm, tn = min(tm, S), min(tn, D)

    def body(oa_ref, wa_ref, yn_ref, ws_ref, ga_ref, gs_ref, a_ref, s_ref, m_ref):
        a = _dot(oa_ref[...], wa_ref[...], NN)
        sv = _dot(yn_ref[...], ws_ref[...], NN)
        a_ref[...] = a.astype(BF16)
        s_ref[...] = sv.astype(BF16)
        m_ref[...] = (_sigmoid(ga_ref[...]) * a + _sigmoid(gs_ref[...]) * sv).astype(BF16)

    tile = pl.BlockSpec((tm, tn), lambda n, m: (m, n))
    return pl.pallas_call(
        body, name=name, grid=(D // tn, S // tm),
        in_specs=[pl.BlockSpec((tm, cfg.AW), lambda n, m: (m, 0)), pl.BlockSpec((cfg.AW, tn), lambda n, m: (0, n)),
                  pl.BlockSpec((tm, cfg.SI), lambda n, m: (m, 0)), pl.BlockSpec((cfg.SI, tn), lambda n, m: (0, n)),
                  pl.BlockSpec((tm, tn), lambda n, m: (m, _blk(cfg.ga0, tn) + n)),
                  pl.BlockSpec((tm, tn), lambda n, m: (m, _blk(cfg.gs0, tn) + n))],
        out_specs=[tile, tile, tile],
        out_shape=[jax.ShapeDtypeStruct((S, D), BF16)] * 3,
        compiler_params=_cp(("parallel", "parallel")),
    )(o_a, w_attn, y_n, w_ssm, proj, proj)


def merge_bwd(dout, w_out, a_out, s_out, proj, cfg, name, tm=512, tn=1024):
    S, D = cfg.S, cfg.D
    tm, tn = min(tm, S), min(tn, D)

    def body(do_ref, wo_ref, a_ref, s_ref, ga_ref, gs_ref, da_ref, ds_ref, dga_ref, dgs_ref):
        dmv = _dot(do_ref[...].astype(BF16), wo_ref[...], NT)
        sa = _sigmoid(ga_ref[...])
        ss = _sigmoid(gs_ref[...])
        da_ref[...] = (dmv * sa).astype(BF16)
        ds_ref[...] = (dmv * ss).astype(BF16)
        dga_ref[...] = (dmv * a_ref[...] * (sa * (1.0 - sa))).astype(BF16)
        dgs_ref[...] = (dmv * s_ref[...] * (ss * (1.0 - ss))).astype(BF16)

    tile = pl.BlockSpec((tm, tn), lambda n, m: (m, n))
    sh = jax.ShapeDtypeStruct((S, D), BF16)
    return pl.pallas_call(
        body, name=name, grid=(D // tn, S // tm),
        in_specs=[pl.BlockSpec((tm, D), lambda n, m: (m, 0)), pl.BlockSpec((tn, D), lambda n, m: (n, 0)), tile, tile,
                  pl.BlockSpec((tm, tn), lambda n, m: (m, _blk(cfg.ga0, tn) + n)),
                  pl.BlockSpec((tm, tn), lambda n, m: (m, _blk(cfg.gs0, tn) + n))],
        out_specs=[tile] * 4, out_shape=[sh] * 4,
        compiler_params=_cp(("parallel", "parallel")),
    )(dout, w_out, a_out, s_out, proj, proj)


def _attn_rows(base, d):
    return pl.ds(base, ATTN_BLOCK) if d == 1 else pl.ds(base, ATTN_BLOCK, stride=d)


def _attn_units(cfg):
    dmax = max(d for _, d in cfg.patterns)
    units = []
    for p, (window, d) in enumerate(cfg.patterns):
        assert window // d == ATTN_BLOCK and dmax % d == 0
        nsub = dmax // d
        for b in range(nsub):
            for r in range(d):
                base = b * ATTN_BLOCK * d + r
                if b > 0:
                    units.append((p, d, base, (b - 1) * ATTN_BLOCK * d + r, False))
                else:
                    units.append((p, d, base, (nsub - 1) * ATTN_BLOCK * d + r, True))
    return units, ATTN_BLOCK * dmax


def _set_bias_tiles(bias_s, slope, cfg):
    qi = lax.broadcasted_iota(jnp.int32, (ATTN_BLOCK, ATTN_BLOCK), 0)
    ki = lax.broadcasted_iota(jnp.int32, (ATTN_BLOCK, ATTN_BLOCK), 1)
    for p, (_, d) in enumerate(cfg.patterns):
        bias_s[2 * p] = jnp.where(ki >= qi, (-slope) * ((ATTN_BLOCK + qi - ki) * d).astype(F32), NEG)
        bias_s[2 * p + 1] = jnp.where(ki <= qi, (-slope) * ((qi - ki) * d).astype(F32), NEG)


def _unit_scores(q, kcat, bias_s, p, prev_ok, scale):
    s = _dot(q, kcat, NT) * scale + jnp.concatenate([bias_s[2 * p], bias_s[2 * p + 1]], axis=1)
    if prev_ok is not None:
        cur_half = lax.broadcasted_iota(jnp.int32, s.shape, 1) >= ATTN_BLOCK
        s = jnp.where(jnp.logical_or(cur_half, prev_ok), s, NEG)
    return s


def _slope_table(cfg):
    slopes = jnp.asarray([2.0 ** (-8.0 * (h + 1) / cfg.AH) for h in range(cfg.AH)], F32)
    return jnp.broadcast_to(slopes.reshape(cfg.AH, 1, 1), (cfg.AH, 8, LANES))


def attn_fused_fwd(proj, slopes, cfg, name):
    S, E, AH = cfg.S, cfg.E, cfg.AH
    units, SB = _attn_units(cfg)
    assert S % SB == 0
    npat = len(cfg.patterns)
    scale = E ** -0.5

    def spec(off, prev):
        c0 = _blk(off, E)
        if prev:
            return pl.BlockSpec((SB, E), lambda h, i: (jnp.maximum(i - 1, 0), c0 + h))
        return pl.BlockSpec((SB, E), lambda h, i: (i, c0 + h))

    def body(q_ref, kp_ref, kc_ref, vp_ref, vc_ref, z_ref, sl_ref, oa_ref, om_ref, lt_ref, *scr):
        o_s, l_s, bias_s = scr[:npat], scr[npat:2 * npat], scr[2 * npat]
        i = pl.program_id(1)

        @pl.when(i == 0)
        def _():
            _set_bias_tiles(bias_s, sl_ref[0, 0:1, :], cfg)

        for p, d, base, pbase, from_prev in units:
            rows, prows = _attn_rows(base, d), _attn_rows(pbase, d)
            q = q_ref[rows, :].astype(BF16)
            kp = (kp_ref if from_prev else kc_ref)[prows, :].astype(BF16)
            vp = (vp_ref if from_prev else vc_ref)[prows, :].astype(BF16)
            kcat = jnp.concatenate([kp, kc_ref[rows, :].astype(BF16)], axis=0)
            vcat = jnp.concatenate([vp, vc_ref[rows, :].astype(BF16)], axis=0)
            s = _unit_scores(q, kcat, bias_s, p, (i > 0) if from_prev else None, scale)
            m = jnp.max(s, axis=1, keepdims=True)
            pr = jnp.exp(s - m)
            l = jnp.sum(pr, axis=1, keepdims=True)
            o_s[p][rows, :] = _dot(pr.astype(BF16), vcat, NN) * (1.0 / l)
            l_s[p][rows, :] = m + jnp.log(l)
        ls = [l_s[p][...] for p in range(npat)]
        m = functools.reduce(jnp.maximum, ls)
        lt = m + jnp.log(sum(jnp.exp(l_ - m) for l_ in ls))
        lt_ref[...] = lt
        mix = sum(jnp.exp(ls[p] - lt) * o_s[p][...] for p in range(npat))
        om_ref[...] = mix
        z = z_ref[...]
        oa_ref[...] = (mix * (z * _sigmoid(z))).astype(BF16)

    out = pl.BlockSpec((SB, E), lambda h, i: (i, h))
    return pl.pallas_call(
        body, name=name, grid=(AH, S // SB),
        in_specs=[spec(0, False), spec(cfg.k0, True), spec(cfg.k0, False), spec(cfg.v0, True), spec(cfg.v0, False),
                  spec(cfg.za0, False), pl.BlockSpec((1, 8, LANES), lambda h, i: (h, 0, 0))],
        out_specs=[out, out, pl.BlockSpec((SB, 1), lambda h, i: (h * (S // SB) + i, 0))],
        out_shape=[jax.ShapeDtypeStruct((S, cfg.AW), BF16), jax.ShapeDtypeStruct((S, cfg.AW), F32),
                   jax.ShapeDtypeStruct((AH * S, 1), F32)],
        scratch_shapes=[pltpu.VMEM((SB, E), F32)] * npat + [pltpu.VMEM((SB, 1), F32)] * npat +
        [pltpu.VMEM((2 * npat, ATTN_BLOCK, ATTN_BLOCK), F32)],
        compiler_params=_cp(("parallel", "arbitrary")),
    )(proj, proj, proj, proj, proj, proj, slopes)


def attn_fused_bwd(proj, do_a, o_mix, ltot, slopes, cfg, name):
    S, E, AH = cfg.S, cfg.E, cfg.AH
    units, SB = _attn_units(cfg)
    nsb = S // SB
    last = nsb - 1
    scale = E ** -0.5

    def spec(off, prev):
        c0 = _blk(off, E)
        if prev:
            return pl.BlockSpec((SB, E), lambda h, i: (jnp.maximum(i - 1, 0), c0 + h))
        return pl.BlockSpec((SB, E), lambda h, i: (jnp.minimum(i, last), c0 + h))

    cur = pl.BlockSpec((SB, E), lambda h, i: (jnp.minimum(i, last), h))
    prev = pl.BlockSpec((SB, E), lambda h, i: (jnp.maximum(i - 1, 0), h))

    def body(q_ref, kp_ref, kc_ref, vp_ref, vc_ref, z_ref, doa_ref, om_ref, lt_ref, sl_ref,
             dq_ref, dk_ref, dv_ref, dz_ref, dmix_s, dl_s, dq_s, dkp_s, dvp_s, dkc_s, dvc_s, bias_s):
        i = pl.program_id(1)

        @pl.when(i == 0)
        def _():
            dkc_s[...] = jnp.zeros_like(dkc_s)
            dvc_s[...] = jnp.zeros_like(dvc_s)
            _set_bias_tiles(bias_s, sl_ref[0, 0:1, :], cfg)

        @pl.when(i < nsb)
        def _():
            z = z_ref[...]
            s = _sigmoid(z)
            doa = doa_ref[...].astype(F32)
            om = om_ref[...]
            dmix = doa * (z * s)
            dmix_s[...] = dmix
            dz_ref[...] = (doa * om * (s * (1.0 + z * (1.0 - s)))).astype(BF16)
            dl_s[...] = jnp.sum(dmix * om, axis=1, keepdims=True)
            dkp_s[...] = dkc_s[...]
            dvp_s[...] = dvc_s[...]
            dkc_s[...] = jnp.zeros_like(dkc_s)
            dvc_s[...] = jnp.zeros_like(dvc_s)
            dq_s[...] = jnp.zeros_like(dq_s)
            for p, d, base, pbase, from_prev in units:
                rows, prows = _attn_rows(base, d), _attn_rows(pbase, d)
                q = q_ref[rows, :].astype(BF16)
                kc = kc_ref[rows, :].astype(BF16)
                kp = (kp_ref if from_prev else kc_ref)[prows, :].astype(BF16)
                vp = (vp_ref if from_prev else vc_ref)[prows, :].astype(BF16)
                do = dmix_s[rows, :].astype(BF16)
                lt = lt_ref[rows, :]
                dlt = dl_s[rows, :]
                kcat = jnp.concatenate([kp, kc], axis=0)
                vcat = jnp.concatenate([vp, vc_ref[rows, :].astype(BF16)], axis=0)
                pr = jnp.exp(_unit_scores(q, kcat, bias_s, p, (i > 0) if from_prev else None, scale) - lt)
                ds = (pr * (_dot(do, vcat, NT) - dlt) * scale).astype(BF16)
                dq_s[rows, :] += _dot(ds, kcat, NN)
                dkcat = _dot(ds, q, TN)
                dvcat = _dot(pr.astype(BF16), do, TN)
                dk_t, dv_t = (dkp_s, dvp_s) if from_prev else (dkc_s, dvc_s)
                dk_t[prows, :] += dkcat[:ATTN_BLOCK, :]
                dv_t[prows, :] += dvcat[:ATTN_BLOCK, :]
                dkc_s[rows, :] += dkcat[ATTN_BLOCK:, :]
                dvc_s[rows, :] += dvcat[ATTN_BLOCK:, :]
            dq_ref[...] = dq_s[...].astype(BF16)
            dk_ref[...] = dkp_s[...].astype(BF16)
            dv_ref[...] = dvp_s[...].astype(BF16)

        @pl.when(i == nsb)
        def _():
            dk_ref[...] = dkc_s[...].astype(BF16)
            dv_ref[...] = dvc_s[...].astype(BF16)

    sh = jax.ShapeDtypeStruct((S, cfg.AW), BF16)
    acc = pltpu.VMEM((SB, E), F32)
    return pl.pallas_call(
        body, name=name, grid=(AH, nsb + 1),
        in_specs=[spec(0, False), spec(cfg.k0, True), spec(cfg.k0, False), spec(cfg.v0, True), spec(cfg.v0, False),
                  spec(cfg.za0, False), cur, cur,
                  pl.BlockSpec((SB, 1), lambda h, i: (h * nsb + jnp.minimum(i, last), 0)),
                  pl.BlockSpec((1, 8, LANES), lambda h, i: (h, 0, 0))],
        out_specs=[cur, prev, prev, cur], out_shape=[sh] * 4,
        scratch_shapes=[acc, pltpu.VMEM((SB, 1), F32), acc, acc, acc, acc, acc,
                        pltpu.VMEM((2 * len(cfg.patterns), ATTN_BLOCK, ATTN_BLOCK), F32)],
        compiler_params=_cp(("parallel", "arbitrary")),
    )(proj, proj, proj, proj, proj, proj, do_a, o_mix, ltot, slopes)


HALO = 8


def _conv_taps(x_ref, h_ref, kc):
    x = x_ref[...]
    full = jnp.concatenate([jnp.where(pl.program_id(1) == 0, 0.0, h_ref[...]), x], axis=0)
    return [pltpu.roll(full, s, axis=0)[HALO:, :] for s in range(kc - 1, 0, -1)] + [x]


def _conv_pre(taps, w_ref, b_ref):
    pre = b_ref[...] + w_ref[0:1, :] * taps[0]
    for k in range(1, len(taps)):
        pre = pre + w_ref[k:k + 1, :] * taps[k]
    return pre


def conv_fwd(proj, w, b, cfg, name, tm=1024, tc=512):
    S, CD, KC = cfg.S, cfg.CD, cfg.KC
    tc = min(tc, CD)
    c0 = _blk(cfg.xbc0, tc)
    hb = tm // HALO

    def body(x_ref, h_ref, w_ref, b_ref, o_ref):
        pre = _conv_pre(_conv_taps(x_ref, h_ref, KC), w_ref, b_ref)
        o_ref[...] = pre * _sigmoid(pre)

    return pl.pallas_call(
        body, name=name, grid=(CD // tc, S // tm),
        in_specs=[pl.BlockSpec((tm, tc), lambda c, i: (i, c0 + c)),
                  pl.BlockSpec((HALO, tc), lambda c, i: (jnp.maximum(i * hb - 1, 0), c0 + c)),
                  pl.BlockSpec((KC, tc), lambda c, i: (0, c)),
                  pl.BlockSpec((1, tc), lambda c, i: (0, c))],
        out_specs=pl.BlockSpec((tm, tc), lambda c, i: (i, c)),
        out_shape=jax.ShapeDtypeStruct((S, CD), F32),
        compiler_params=_cp(("parallel", "arbitrary")),
    )(proj, proj, w, b)


def conv_bwd_a(proj, dxc, w, b, cfg, name, c_off, tm=1024, tc=512):
    S, KC = cfg.S, cfg.KC
    CD = dxc.shape[1]
    tc = min(tc, CD)
    c0 = _blk(cfg.xbc0 + c_off, tc)
    w0 = _blk(c_off, tc)
    hb = tm // HALO

    def body(x_ref, h_ref, d_ref, w_ref, b_ref, dp_ref, gw_ref, gb_ref):
        taps = _conv_taps(x_ref, h_ref, KC)
        pre = _conv_pre(taps, w_ref, b_ref)
        s = _sigmoid(pre)
        dpre = d_ref[...] * (s * (1.0 + pre * (1.0 - s)))
        dp_ref[...] = dpre
        gb = jnp.sum(dpre, axis=0, keepdims=True)
        gws = [jnp.sum(dpre * taps[k], axis=0, keepdims=True) for k in range(KC)]
        gw = jnp.concatenate(gws + [jnp.zeros((8 - KC, tc), F32)], axis=0)

        @pl.when(pl.program_id(1) == 0)
        def _():
            gw_ref[...] = gw
            gb_ref[...] = gb

        @pl.when(pl.program_id(1) > 0)
        def _():
            gw_ref[...] += gw
            gb_ref[...] += gb

    return pl.pallas_call(
        body, name=name, grid=(CD // tc, S // tm),
        in_specs=[pl.BlockSpec((tm, tc), lambda c, i: (i, c0 + c)),
                  pl.BlockSpec((HALO, tc), lambda c, i: (jnp.maximum(i * hb - 1, 0), c0 + c)),
                  pl.BlockSpec((tm, tc), lambda c, i: (i, c)),
                  pl.BlockSpec((KC, tc), lambda c, i: (0, w0 + c)),
                  pl.BlockSpec((1, tc), lambda c, i: (0, w0 + c))],
        out_specs=[pl.BlockSpec((tm, tc), lambda c, i: (i, c)),
                   pl.BlockSpec((8, tc), lambda c, i: (0, c)),
                   pl.BlockSpec((1, tc), lambda c, i: (0, c))],
        out_shape=[jax.ShapeDtypeStruct((S, CD), F32), jax.ShapeDtypeStruct((8, CD), F32),
                   jax.ShapeDtypeStruct((1, CD), F32)],
        compiler_params=_cp(("parallel", "arbitrary")),
    )(proj, proj, dxc, w, b)


def conv_bwd_b(dpre, w, cfg, name, c_off, tm=1024, tc=512):
    S, KC = cfg.S, cfg.KC
    CD = dpre.shape[1]
    tc = min(tc, CD)
    w0 = _blk(c_off, tc)
    hb = tm // HALO
    nrb = S // tm
    last_h = S // HALO - 1

    def body(d_ref, h_ref, w_ref, o_ref):
        d = d_ref[...]
        full = jnp.concatenate([d, jnp.where(pl.program_id(1) == nrb - 1, 0.0, h_ref[...])], axis=0)
        acc = w_ref[KC - 1:KC, :] * d
        for j in range(1, KC):
            acc = acc + w_ref[KC - 1 - j:KC - j, :] * pltpu.roll(full, tm + HALO - j, axis=0)[:tm, :]
        o_ref[...] = acc.astype(BF16)

    return pl.pallas_call(
        body, name=name, grid=(CD // tc, nrb),
        in_specs=[pl.BlockSpec((tm, tc), lambda c, i: (i, c)),
                  pl.BlockSpec((HALO, tc), lambda c, i: (jnp.minimum((i + 1) * hb, last_h), c)),
                  pl.BlockSpec((KC, tc), lambda c, i: (0, w0 + c))],
        out_specs=pl.BlockSpec((tm, tc), lambda c, i: (i, c)),
        out_shape=jax.ShapeDtypeStruct((S, CD), BF16),
        compiler_params=_cp(("parallel", "arbitrary")),
    )(dpre, dpre, w)


def _pad_lanes(v, width=LANES):
    return jnp.pad(v, ((0, 0), (0, width - v.shape[1])))


def ssd_prep(dt_raw, dt_bias, a_log, cfg, name):
    S, L = cfg.S, cfg.L

    def body(x_ref, b_ref, al_ref, dt_ref, ac_ref):
        x = x_ref[...] + b_ref[...]
        dt = jnp.maximum(x, 0.0) + jnp.log(1.0 + jnp.exp(-jnp.abs(x)))
        da = dt * (-jnp.exp(al_ref[...]))
        li = lax.broadcasted_iota(jnp.int32, (L, L), 0)
        si = lax.broadcasted_iota(jnp.int32, (L, L), 1)
        tri = jnp.where(li >= si, 1.0, 0.0).astype(F32)
        dt_ref[...] = dt
        ac_ref[...] = lax.dot_general(tri, da, ((NN), ((), ())), precision=lax.Precision.HIGHEST,
                                      preferred_element_type=F32)

    row = pl.BlockSpec((L, LANES), lambda i: (i, 0))
    vec = pl.BlockSpec((1, LANES), lambda i: (0, 0))
    sh = jax.ShapeDtypeStruct((S, LANES), F32)
    return pl.pallas_call(
        body, name=name, grid=(S // L,), in_specs=[row, vec, vec], out_specs=[row, row], out_shape=[sh, sh],
        compiler_params=_cp(("parallel",)),
    )(dt_raw, dt_bias, a_log)


def _spread(v, n):
    return jnp.broadcast_to(v[:, :, None], v.shape + (n,)).reshape(v.shape[0], v.shape[1] * n)


def _head_selectors(cfg):
    def sel(width):
        head = jnp.arange(LANES)[None, :, None]
        slot = jnp.arange(cfg.SG)[:, None, None] * cfg.HPG + (jnp.arange(cfg.HPG * width) // width)[None, None, :]
        return (head == slot).astype(BF16)
    return sel(cfg.P), sel(LANES)


def _spread_heads(v, sel):
    n = v.shape[0]
    hi = v.astype(BF16)
    r1 = v - hi.astype(F32)
    mid = r1.astype(BF16)
    lo = (r1 - mid.astype(F32)).astype(BF16)
    out = _dot(jnp.concatenate([hi, mid, lo], axis=0), sel, NN)
    return out[:n] + out[n:2 * n] + out[2 * n:]


def _pair_lanes(wide, hpg, p):
    low = lax.broadcasted_iota(jnp.int32, (wide.shape[0], LANES), 1) < p
    return jnp.concatenate([jnp.where(low, wide[:, 2 * jp * LANES:(2 * jp + 1) * LANES],
                                      wide[:, (2 * jp + 1) * LANES:(2 * jp + 2) * LANES])
                            for jp in range(hpg // 2)], axis=1)


def _pair_select(halves, p):
    low = lax.broadcasted_iota(jnp.int32, halves[0].shape, 1) < p
    return jnp.where(low, halves[0], halves[1])


def _head_rows(row, hpg, p):
    return jnp.concatenate([jnp.broadcast_to(row[:, j * LANES:(j + 1) * LANES], (p, LANES)) for j in range(hpg)],
                           axis=0)


def _segment_sums(t, sel):
    r = t.shape[0]
    hi = t.astype(BF16)
    out = _dot(jnp.concatenate([hi, (t - hi.astype(F32)).astype(BF16)], axis=0), sel, NT)
    return out[:r] + out[r:]


def ssd_scan_fwd(xc, dt, acum, act, sel_p, sel_l, cfg, name):
    S, L, P, SN, HPG, SG, SI = cfg.S, cfg.L, cfg.P, cfg.SN, cfg.HPG, cfg.SG, cfg.SI
    nc = S // L
    GW = HPG * P
    bcol, ccol = _blk(SI, SN), _blk(SI + cfg.GN, SN)
    assert nc % CPS == 0

    def body(xs_ref, b_ref, c_ref, dtn_ref, acn_ref, at_ref, sp_ref, sl_ref, y_ref, st_ref, st):
        @pl.when(pl.program_id(1) == 0)
        def _():
            st[...] = jnp.zeros_like(st)

        causal = lax.broadcasted_iota(jnp.int32, (L, L), 0) >= lax.broadcasted_iota(jnp.int32, (L, L), 1)
        for ci in range(CPS):
            rows = slice(ci * L, (ci + 1) * L)
            acn = acn_ref[rows, :]
            dts = _spread_heads(dtn_ref[rows, :], sp_ref[0])
            acs = _spread_heads(acn, sl_ref[0])
            a_p = _pair_lanes(acs, HPG, P)
            s0 = st[...]
            st_ref[ci] = s0.reshape(HPG, P, SN)
            B = b_ref[rows, :].astype(BF16)
            C = c_ref[rows, :].astype(BF16)
            G = _dot(C, B, NT)
            xdt = xs_ref[rows, :] * dts
            xdtb = xdt.astype(BF16)
            ws = jnp.exp(a_p[L - 1:L, :] - a_p)
            yo = jnp.exp(a_p) * _dot(C, s0.astype(BF16), NT)
            yd = []
            for jp in range(HPG // 2):
                x_pair = xdtb[:, jp * LANES:(jp + 1) * LANES]
                halves = []
                for j in (2 * jp, 2 * jp + 1):
                    dm = jnp.where(causal, jnp.exp(acs[:, j * LANES:(j + 1) * LANES] - at_ref[j:j + 1, rows]), 0.0)
                    halves.append(_dot((G * dm).astype(BF16), x_pair, NN))
                yd.append(_pair_select(halves, P))
            y_ref[rows, :] = jnp.concatenate(yd, axis=1) + yo
            st[...] = _head_rows(jnp.exp(acs[L - 1:L, :]), HPG, P) * s0 + _dot((xdt * ws).astype(BF16), B, TN)

    R = CPS * L
    y, states = pl.pallas_call(
        body, name=name, grid=(SG, nc // CPS),
        in_specs=[pl.BlockSpec((R, GW), lambda g, c: (c, g)),
                  pl.BlockSpec((R, SN), lambda g, c: (c, bcol + g)),
                  pl.BlockSpec((R, SN), lambda g, c: (c, ccol + g)),
                  pl.BlockSpec((R, LANES), lambda g, c: (c, 0)),
                  pl.BlockSpec((R, LANES), lambda g, c: (c, 0)),
                  pl.BlockSpec((HPG, R), lambda g, c: (g, c)),
                  pl.BlockSpec((1, LANES, GW), lambda g, c: (g, 0, 0)),
                  pl.BlockSpec((1, LANES, HPG * LANES), lambda g, c: (g, 0, 0))],
        out_specs=[pl.BlockSpec((R, GW), lambda g, c: (c, g)),
                   pl.BlockSpec((CPS, HPG, P, SN), lambda g, c: (c, g, 0, 0))],
        out_shape=[jax.ShapeDtypeStruct((S, SI), F32), jax.ShapeDtypeStruct((nc, cfg.SH, P, SN), F32)],
        scratch_shapes=[pltpu.VMEM((GW, SN), F32)],
        compiler_params=_cp(("parallel", "arbitrary")),
    )(xc, xc, xc, dt, acum, act, sel_p, sel_l)
    return y, states


def ssd_scan_bwd(xc, dt, acum, act, sel_p, sel_l, states, y, dy, dvec, cfg, name, side):
    S, L, P, SN, HPG, SG, SI = cfg.S, cfg.L, cfg.P, cfg.SN, cfg.HPG, cfg.SG, cfg.SI
    nc = S // L
    GW = HPG * P
    bcol, ccol = _blk(SI, SN), _blk(SI + cfg.GN, SN)

    def rc(c):
        return nc // CPS - 1 - c

    def body(xs_ref, b_ref, c_ref, dtn_ref, acn_ref, at_ref, sp_ref, sl_ref, st_ref, y_ref, dy_ref, dk_ref,
             dxs_ref, db_ref, dc_ref, dac_ref, dxsum_ref, dst):
        @pl.when(pl.program_id(1) == 0)
        def _():
            dst[...] = jnp.zeros_like(dst)

        sel = sp_ref[0]
        causal = lax.broadcasted_iota(jnp.int32, (L, L), 0) >= lax.broadcasted_iota(jnp.int32, (L, L), 1)
        low = lax.broadcasted_iota(jnp.int32, (L, LANES), 1) < P
        is_last = lax.broadcasted_iota(jnp.int32, (L, LANES), 0) == L - 1
        ones = jnp.ones((16, SN), BF16)
        for ci in reversed(range(CPS)):
            rows = slice(ci * L, (ci + 1) * L)
            acn = acn_ref[rows, :]
            dts = _spread_heads(dtn_ref[rows, :], sel)
            acs = _spread_heads(acn, sl_ref[0])
            a_p = _pair_lanes(acs, HPG, P)
            B = b_ref[rows, :].astype(BF16)
            C = c_ref[rows, :].astype(BF16)
            G = _dot(C, B, NT)
            xs = xs_ref[rows, :]
            dY = dy_ref[rows, :].astype(F32)
            xdt = xs * dts
            xdtb = xdt.astype(BF16)
            dYb = dY.astype(BF16)
            s0 = st_ref[ci].reshape(GW, SN)
            s0b = s0.astype(BF16)
            ds1 = dst[...]
            ds1b = ds1.astype(BF16)
            ws = jnp.exp(a_p[L - 1:L, :] - a_p)
            dR = (jnp.exp(a_p) * dY).astype(BF16)
            dX2 = ws * _dot(B, ds1b, NT)
            dgsum = jnp.zeros((L, L), F32)
            dX1, yd = [], []
            for jp in range(HPG // 2):
                lanes = slice(jp * LANES, (jp + 1) * LANES)
                x_pair, dy_pair = xdtb[:, lanes], dYb[:, lanes]
                h1, h2 = [], []
                for h, j in enumerate((2 * jp, 2 * jp + 1)):
                    dm = jnp.where(causal, jnp.exp(acs[:, j * LANES:(j + 1) * LANES] - at_ref[j:j + 1, rows]), 0.0)
                    mine = low if h == 0 else jnp.logical_not(low)
                    dgsum = dgsum + _dot(jnp.where(mine, dy_pair, jnp.zeros_like(dy_pair)), x_pair, NT) * dm
                    Mb = (G * dm).astype(BF16)
                    h1.append(_dot(Mb, dy_pair, TN))
                    h2.append(_dot(Mb, x_pair, NN))
                dX1.append(_pair_select(h1, P))
                yd.append(_pair_select(h2, P))
            dX1 = jnp.concatenate(dX1, axis=1)
            dX = dX1 + dX2
            pair = (dYb.astype(F32) - dY) * jnp.concatenate(yd, axis=1) - xdtb.astype(F32) * dX1
            through = _segment_sums(xdt * dX2, sel)
            u = ds1 * s0
            u_hi = u.astype(BF16)
            u_rows = _dot(ones, jnp.concatenate([u_hi, (u - u_hi.astype(F32)).astype(BF16)], axis=0), NT)
            u_rows = u_rows[:, :GW] + u_rows[:, GW:]
            at_end = jnp.exp(acn[L - 1:L, :]) * _segment_sums(u_rows, sel)[0:1, :] + \
                jnp.sum(through, axis=0, keepdims=True)
            dac_ref[0, rows, :] = _segment_sums(dY * y_ref[rows, :] + pair, sel) - through + \
                jnp.where(is_last, at_end, 0.0)
            dxsum_ref[0, rows, :] = _segment_sums(dX * xs, sel)
            dxs_ref[rows, :] = dX * dts + dk_ref[...] * dY
            dst[...] = _head_rows(jnp.exp(acs[L - 1:L, :]), HPG, P) * ds1 + _dot(dR, C, TN)
            dgb = dgsum.astype(BF16)
            dc_ref[rows, :] = _dot(dR, s0b, NN) + _dot(dgb, B, NN)
            db_ref[rows, :] = _dot((xdt * ws).astype(BF16), ds1b, NN) + _dot(dgb, C, TN)

    R = CPS * L
    wide = pl.BlockSpec((R, GW), lambda g, c: (rc(c), g))
    colspec = pl.BlockSpec((1, R, LANES), lambda g, c: (g, rc(c), 0))
    whole = pl.BlockSpec(memory_space=pl.ANY)
    arrs, gathers = side
    grid = (SG, nc // CPS)
    res = pl.pallas_call(
        with_exchange(body, 12, 5, gathers, grid), name=name, grid=grid,
        in_specs=[wide,
                  pl.BlockSpec((R, SN), lambda g, c: (rc(c), bcol + g)),
                  pl.BlockSpec((R, SN), lambda g, c: (rc(c), ccol + g)),
                  pl.BlockSpec((R, LANES), lambda g, c: (rc(c), 0)),
                  pl.BlockSpec((R, LANES), lambda g, c: (rc(c), 0)),
                  pl.BlockSpec((HPG, R), lambda g, c: (g, rc(c))),
                  pl.BlockSpec((1, LANES, GW), lambda g, c: (g, 0, 0)),
                  pl.BlockSpec((1, LANES, HPG * LANES), lambda g, c: (g, 0, 0)),
                  pl.BlockSpec((CPS, HPG, P, SN), lambda g, c: (rc(c), g, 0, 0)),
                  wide, wide,
                  pl.BlockSpec((1, GW), lambda g, c: (0, g))] + [whole] * len(arrs),
        out_specs=[wide,
                   pl.BlockSpec((R, SN), lambda g, c: (rc(c), g)),
                   pl.BlockSpec((R, SN), lambda g, c: (rc(c), g)),
                   colspec, colspec] + [whole] * len(arrs),
        out_shape=[jax.ShapeDtypeStruct((S, SI), F32), jax.ShapeDtypeStruct((S, cfg.GN), F32),
                   jax.ShapeDtypeStruct((S, cfg.GN), F32),
                   jax.ShapeDtypeStruct((SG, S, LANES), F32), jax.ShapeDtypeStruct((SG, S, LANES), F32)] +
        _exchange_shapes(arrs, gathers),
        scratch_shapes=[pltpu.VMEM((GW, SN), F32)] + _exchange_sems(len(arrs)),
        compiler_params=_cp(("arbitrary", "arbitrary")),
    )(xc, xc, xc, dt, acum, act, sel_p, sel_l, states, y, dy, dvec, *arrs)
    return res[:5], res[5:]


def dt_bwd(dac, dxsum, dt_raw, dt, dt_bias, a_log, cfg, name):
    S, L, SG = cfg.S, cfg.L, cfg.SG

    def body(da_ref, dx_ref, x_ref, dt_ref, b_ref, al_ref, o_ref, gb_ref, ga_ref):
        a = -jnp.exp(al_ref[...])
        dtv = dt_ref[...]
        dxs = jnp.sum(dx_ref[...], axis=0)
        upper = jnp.where(lax.broadcasted_iota(jnp.int32, (L, L), 1) >= lax.broadcasted_iota(jnp.int32, (L, L), 0),
                          1.0, 0.0).astype(F32)
        dda = lax.dot_general(upper, jnp.sum(da_ref[...], axis=0), (NN, ((), ())), precision=lax.Precision.HIGHEST,
                              preferred_element_type=F32)
        draw = (dxs + dda * a) * _sigmoid(x_ref[...] + b_ref[...])
        o_ref[...] = draw.astype(BF16)
        gb = jnp.sum(draw, axis=0, keepdims=True)
        ga = jnp.sum(dda * dtv, axis=0, keepdims=True) * a

        @pl.when(pl.program_id(0) == 0)
        def _():
            gb_ref[...] = gb
            ga_ref[...] = ga

        @pl.when(pl.program_id(0) > 0)
        def _():
            gb_ref[...] += gb
            ga_ref[...] += ga

    row = pl.BlockSpec((L, LANES), lambda i: (i, 0))
    vec = pl.BlockSpec((1, LANES), lambda i: (0, 0))
    return pl.pallas_call(
        body, name=name, grid=(S // L,),
        in_specs=[pl.BlockSpec((SG, L, LANES), lambda i: (0, i, 0))] * 2 + [row, row, vec, vec],
        out_specs=[row, vec, vec],
        out_shape=[jax.ShapeDtypeStruct((S, LANES), BF16), jax.ShapeDtypeStruct((1, LANES), F32),
                   jax.ShapeDtypeStruct((1, LANES), F32)],
        compiler_params=_cp(("arbitrary",)),
    )(dac, dxsum, dt_raw, dt, dt_bias, a_log)


def gated_norm_fwd(y, xc, proj, dvec, nw, cfg, name, tm=128):
    S, SI = cfg.S, cfg.SI

    def body(y_ref, xs_ref, z_ref, d_ref, w_ref, o_ref):
        z = z_ref[...]
        yg = (y_ref[...] + d_ref[...] * xs_ref[...]) * (z * _sigmoid(z))
        r = lax.rsqrt(jnp.mean(yg * yg, axis=-1, keepdims=True) + RMS_EPS)
        o_ref[...] = ((yg * r) * w_ref[...]).astype(BF16)

    row = pl.BlockSpec((tm, SI), lambda i: (i, 0))
    vec = pl.BlockSpec((1, SI), lambda i: (0, 0))
    return pl.pallas_call(
        body, name=name, grid=(S // tm,),
        in_specs=[row, row, pl.BlockSpec((tm, SI), lambda i: (i, _blk(cfg.zs0, SI))), vec, vec],
        out_specs=row, out_shape=jax.ShapeDtypeStruct((S, SI), BF16),
        compiler_params=_cp(("parallel",)),
    )(y, xc, proj, dvec, nw)


def gated_norm_bwd(dyn, y, xc, proj, dvec, nw, cfg, name, tm=128):
    S, SI = cfg.S, cfg.SI

    def body(dn_ref, y_ref, xs_ref, z_ref, d_ref, w_ref, dy_ref, dz_ref, gw_ref, gd_ref):
        z = z_ref[...]
        s = _sigmoid(z)
        sz = z * s
        xs = xs_ref[...]
        yf = y_ref[...] + d_ref[...] * xs
        yg = yf * sz
        r = lax.rsqrt(jnp.mean(yg * yg, axis=-1, keepdims=True) + RMS_EPS)
        dn = dn_ref[...].astype(F32)
        g = dn * w_ref[...]
        dyg = r * g - yg * (r * r * r) * jnp.mean(g * yg, axis=-1, keepdims=True)
        dy = dyg * sz
        dy_ref[...] = dy.astype(BF16)
        dz_ref[...] = (dyg * yf * (s * (1.0 + z * (1.0 - s)))).astype(BF16)
        gw = jnp.sum(dn * (yg * r), axis=0, keepdims=True)
        gd = jnp.sum(dy * xs, axis=0, keepdims=True)

        @pl.when(pl.program_id(0) == 0)
        def _():
            gw_ref[...] = gw
            gd_ref[...] = gd

        @pl.when(pl.program_id(0) > 0)
        def _():
            gw_ref[...] += gw
            gd_ref[...] += gd

    row = pl.BlockSpec((tm, SI), lambda i: (i, 0))
    vec = pl.BlockSpec((1, SI), lambda i: (0, 0))
    return pl.pallas_call(
        body, name=name, grid=(S // tm,),
        in_specs=[row, row, row, pl.BlockSpec((tm, SI), lambda i: (i, _blk(cfg.zs0, SI))), vec, vec],
        out_specs=[row, row, vec, vec],
        out_shape=[jax.ShapeDtypeStruct((S, SI), BF16), jax.ShapeDtypeStruct((S, SI), BF16),
                   jax.ShapeDtypeStruct((1, SI), F32), jax.ShapeDtypeStruct((1, SI), F32)],
        compiler_params=_cp(("arbitrary",)),
    )(dyn, y, xc, proj, dvec, nw)


def _shard_columns(cfg, main, dt):
    dt0 = 4 * cfg.AW + cfg.SI + cfg.CD
    ws = cfg.N_IN // N_DEV
    out = []
    for k in range(N_DEV):
        lo, hi, parts = k * ws, (k + 1) * ws, []
        if lo < dt0:
            parts.append((main, lo, min(hi, dt0)))
        if lo < dt0 + cfg.SH and hi > dt0:
            parts.append((dt, max(lo, dt0) - dt0, min(hi, dt0 + cfg.SH) - dt0))
        if hi > dt0 + cfg.SH:
            parts.append((main, max(lo, dt0 + cfg.SH) - cfg.SH, hi - cfg.SH))
        out.append(parts)
    return out


def local_step(cfg, x, tgt, norm_w, conv_w, conv_b, dt_bias, a_log, d_skip, ssm_norm_w, final_norm_w,
               w_main, w_dt, shards, dt0):
    S, D = cfg.S, cfg.D
    slopes = _slope_table(cfg)
    dt_bias_p = _pad_lanes(dt_bias)
    a_log_p = _pad_lanes(a_log)
    dvec = _spread(d_skip, cfg.P)

    hn = rmsnorm_fwd(x, norm_w, "rmsnorm_fwd")
    proj, gathered = matmul(hn, w_main, 'nn', 1024, 2048, 2048, F32, "in_proj", side=(shards, [True] * 3))
    w_attn, w_ssm, w_out = gathered[0].reshape(cfg.AW, D), gathered[1].reshape(cfg.SI, D), gathered[2].reshape(D, D)
    dt_raw = matmul(hn, w_dt, 'nn', 512, 128, 2048, F32, "in_proj_dt")
    o_a, o_mix, ltot = attn_fused_fwd(proj, slopes, cfg, "attn_fwd")
    xc = conv_fwd(proj, conv_w, conv_b, cfg, "conv_fwd")
    dt, acum = ssd_prep(dt_raw, dt_bias_p, a_log_p, cfg, "ssd_prep")
    act = acum[:, :cfg.SH].T
    sel_p, sel_l = _head_selectors(cfg)
    y, states = ssd_scan_fwd(xc, dt, acum, act, sel_p, sel_l, cfg, "ssd_scan_fwd")
    y_n = gated_norm_fwd(y, xc, proj, dvec, ssm_norm_w, cfg, "gated_norm_fwd")
    a_out, s_out, merged = branch_merge(o_a, w_attn, y_n, w_ssm, proj, cfg, "branch_merge")
    dout, loss_p, g_final_w = out_proj_final(merged, w_out, x, final_norm_w.reshape(1, D), tgt, "out_proj_final")

    g_w_out = matmul(merged, dout, 'tn', 1024, 1024, 2048, BF16, "g_w_out")
    da_out, ds_out, dga, dgs = merge_bwd(dout, w_out, a_out, s_out, proj, cfg, "merge_bwd")
    g_w_attn = matmul(o_a, da_out, 'tn', 1024, 1024, 2048, BF16, "g_w_attn")
    g_w_ssm = matmul(y_n, ds_out, 'tn', 1024, 1024, 2048, BF16, "g_w_ssm")
    do_a = matmul(da_out, w_attn, 'nt', 512, 1024, 2048, BF16, "d_o_a")
    dyn = matmul(ds_out, w_ssm, 'nt', 512, 1024, 2048, BF16, "d_y_n")
    dy, dz_s, g_ssm_norm, g_dvec = gated_norm_bwd(dyn, y, xc, proj, dvec, ssm_norm_w, cfg, "gated_norm_bwd")
    sends = [g.reshape((N_DEV, g.shape[0] // N_DEV, D)) for g in (g_w_attn, g_w_ssm, g_w_out)]
    (dxs, dB, dC, dac_g, dxsum_g), (r_attn, r_ssm, r_out) = ssd_scan_bwd(
        xc, dt, acum, act, sel_p, sel_l, states, y, dy, dvec, cfg, "ssd_scan_bwd", side=(sends, [False] * 3))
    ddt_raw, g_dt_bias, g_a_log = dt_bwd(dac_g, dxsum_g, dt_raw, dt, dt_bias_p, a_log_p, cfg, "dt_bwd")
    dxbc, g_cw, g_cb = [], [], []
    for nm, piece, c_off in (("xs", dxs, 0), ("b", dB, cfg.SI), ("c", dC, cfg.SI + cfg.GN)):
        dpre, gw, gb = conv_bwd_a(proj, piece, conv_w, conv_b, cfg, "conv_bwd_a_" + nm, c_off)
        dxbc.append(conv_bwd_b(dpre, conv_w, cfg, "conv_bwd_b_" + nm, c_off))
        g_cw.append(gw)
        g_cb.append(gb)
    g_conv_w, g_conv_b = jnp.concatenate(g_cw, axis=1), jnp.concatenate(g_cb, axis=1)
    dq, dk, dv, dz_a = attn_fused_bwd(proj, do_a, o_mix, ltot, slopes, cfg, "attn_bwd")
    dproj = jnp.concatenate([dq, dk, dv, dz_a, dz_s] + dxbc + [dga, dgs], axis=1)
    def slabs(g_main, g_dt):
        return jnp.stack([jnp.concatenate([g[:, lo:hi] for g, lo, hi in parts], axis=1)
                          for parts in _shard_columns(cfg, g_main, g_dt)])

    half = D // 2
    g_w_dt = matmul(hn, ddt_raw, 'tn', 1024, 128, 2048, BF16, "g_w_dt")
    g_w_top = matmul(hn[:, :half], dproj, 'tn', 1024, 2048, 2048, BF16, "g_w_main_top")
    g_w_bot, (r_top,) = matmul(hn[:, half:], dproj, 'tn', 1024, 2048, 2048, BF16, "g_w_main_bottom",
                               side=([slabs(g_w_top, g_w_dt[:half])], [False]))
    dhn_a, (r_bot,) = matmul(dproj, w_main, 'nt', 1024, 1024, 2048, F32, "d_hn",
                             side=([slabs(g_w_bot, g_w_dt[half:])], [False]))
    dhn_b = matmul(ddt_raw, w_dt, 'nt', 512, 1024, 128, F32, "d_hn_dt")
    grad_x, g_norm_w = rmsnorm_bwd(dhn_a, dhn_b, x, norm_w, dout, "rmsnorm_bwd")

    g_d_skip = jnp.sum(g_dvec.reshape(cfg.SH, cfg.P), axis=1).reshape(1, cfg.SH)
    small = dict(norm_w=g_norm_w, conv_b=g_conv_b, dt_bias=g_dt_bias[:, :cfg.SH], a_log=g_a_log[:, :cfg.SH],
                 d_skip=g_d_skip, ssm_norm_w=g_ssm_norm, final_norm_w=g_final_w, conv_w=g_conv_w[:cfg.KC])
    return loss_p, grad_x, small, dict(w_in=[r_top, r_bot], w_attn=[r_attn], w_ssm=[r_ssm], w_out=[r_out])


def _mesh_pos():
    return lax.axis_index("x"), lax.axis_index("y"), lax.axis_index("c")


def _flat(pos):
    return 4 * pos[0] + 2 * pos[1] + pos[2]


def _exchange_shapes(arrs, gathers):
    return [jax.ShapeDtypeStruct(((N_DEV,) + a.shape) if g else a.shape, a.dtype) for a, g in zip(arrs, gathers)]


def _exchange_sems(n):
    return [pltpu.SemaphoreType.DMA((n * (N_DEV - 1),)), pltpu.SemaphoreType.DMA((n * (N_DEV - 1),)),
            pltpu.SemaphoreType.DMA((n,))]


def _exchange_copies(ins, outs, gathers, send_sems, recv_sems, loc_sems):
    pos = _mesh_pos()
    me = _flat(pos)
    starts, waits = [], []
    for a in range(len(ins)):
        mine = ins[a] if gathers[a] else ins[a].at[me]
        loc = pltpu.make_async_copy(mine, outs[a].at[me], loc_sems.at[a])
        starts.append(loc)
        waits.append(loc)
        for k in range(1, N_DEV):
            flip = ((k >> 2) & 1, (k >> 1) & 1, k & 1)
            peer = tuple(1 - p if f else p for p, f in zip(pos, flip))
            pk = _flat(peer)
            src = ins[a] if gathers[a] else ins[a].at[pk]
            sems = dict(send_sem=send_sems.at[a * (N_DEV - 1) + k - 1], recv_sem=recv_sems.at[a * (N_DEV - 1) + k - 1],
                        device_id=peer, device_id_type=pl.DeviceIdType.MESH)
            starts.append(pltpu.make_async_remote_copy(src_ref=src, dst_ref=outs[a].at[me], **sems))
            waits.append(pltpu.make_async_remote_copy(src_ref=src, dst_ref=outs[a].at[pk], **sems))
    return starts, waits


def exchange(arrs, gathers, name):
    n = len(arrs)

    def body(*refs):
        starts, waits = _exchange_copies(refs[:n], refs[n:2 * n], gathers, *refs[2 * n:])
        for cp in starts:
            cp.start()
        for cp in waits:
            cp.wait()

    hbm = pl.BlockSpec(memory_space=pltpu.HBM)
    return pl.pallas_call(
        body, name=name, in_specs=[hbm] * n, out_specs=[hbm] * n, out_shape=_exchange_shapes(arrs, gathers),
        scratch_shapes=_exchange_sems(n),
    )(*arrs)


def gather_two_level(arrs, chunks, name):
    n = len(arrs)

    def body(*refs):
        ins, outs = refs[:n], refs[n:2 * n]
        send_sems, recv_sems, loc_sems = refs[2 * n:]
        x, y, c = _mesh_pos()
        me, sib = (x, y, c), (x, y, 1 - c)
        chips = [(1 - x, y), (x, 1 - y), (1 - x, 1 - y)]
        plan, base = [], 0
        for a in range(n):
            step = arrs[a].shape[0] // chunks[a]
            for q in range(chunks[a]):
                plan.append((a, pl.ds(q * step, step), base))
                base += N_DEV - 1

        def copy(a, rows, sem, block, to, own=False):
            dst = outs[a].at[_flat(block), rows]
            return pltpu.make_async_remote_copy(
                src_ref=ins[a].at[rows] if own else dst, dst_ref=dst, send_sem=send_sems.at[sem],
                recv_sem=recv_sems.at[sem], device_id=to, device_id_type=pl.DeviceIdType.MESH)

        local = [pltpu.make_async_copy(ins[a], outs[a].at[_flat(me)], loc_sems.at[a]) for a in range(n)]
        for cp in local:
            cp.start()
        sent = []
        for a, rows, s in plan:
            sent.append(copy(a, rows, s, me, sib, own=True))
            sent += [copy(a, rows, s + 1 + j, me, (*chip, c), own=True) for j, chip in enumerate(chips)]
        for cp in sent:
            cp.start()
        for a, rows, s in plan:
            for j, chip in enumerate(chips):
                copy(a, rows, s + 1 + j, (*chip, c), me).wait_recv()
                passed = copy(a, rows, s + 4 + j, (*chip, c), sib)
                passed.start()
                sent.append(passed)
        for a, rows, s in plan:
            copy(a, rows, s, sib, me).wait_recv()
            for j, chip in enumerate(chips):
                copy(a, rows, s + 4 + j, (*chip, 1 - c), me).wait_recv()
        for cp in sent:
            cp.wait_send()
        for cp in local:
            cp.wait()

    hbm = pl.BlockSpec(memory_space=pltpu.HBM)
    nsem = (N_DEV - 1) * sum(chunks)
    return pl.pallas_call(
        body, name=name, in_specs=[hbm] * n, out_specs=[hbm] * n, out_shape=_exchange_shapes(arrs, [True] * n),
        scratch_shapes=[pltpu.SemaphoreType.DMA((nsem,)), pltpu.SemaphoreType.DMA((nsem,)),
                        pltpu.SemaphoreType.DMA((n,))],
    )(*arrs)


def with_exchange(body, n_in, n_out, gathers, grid):
    n = len(gathers)

    def wrapped(*refs):
        ins, sends = refs[:n_in], refs[n_in:n_in + n]
        outs, recvs = refs[n_in + n:n_in + n + n_out], refs[n_in + 2 * n + n_out - n:n_in + 2 * n + n_out]
        scratch, sems = refs[n_in + 2 * n + n_out:-3], refs[-3:]
        ids = [pl.program_id(d) for d in range(len(grid))]
        first = functools.reduce(jnp.logical_and, [i == 0 for i in ids])
        last = functools.reduce(jnp.logical_and, [i == g - 1 for i, g in zip(ids, grid)])

        @pl.when(first)
        def _():
            for cp in _exchange_copies(sends, recvs, gathers, *sems)[0]:
                cp.start()

        body(*ins, *outs, *scratch)

        @pl.when(last)
        def _():
            for cp in _exchange_copies(sends, recvs, gathers, *sems)[1]:
                cp.wait()

    return wrapped


def adamw(g_src, w, m, v, summed, name, tr=64):
    R, C = w.shape
    tr = min(tr, R)
    assert R % tr == 0
    parts = g_src if summed else [g_src]
    starts = [sum(p.shape[1] for p in parts[:k]) // tr for k in range(len(parts))] if summed else [0]
    counts = [p.shape[1] // tr for p in parts] if summed else [R // tr]

    def body(*refs):
        g_refs, (w_ref, m_ref, v_ref, g_out, d_out, m_out, v_out) = refs[:len(parts)], refs[len(parts):]
        if summed:
            g = None
            for k, g_ref in enumerate(g_refs):
                gk = g_ref[0].astype(F32)
                for j in range(1, N_DEV):
                    gk = gk + g_ref[j].astype(F32)
                g = gk if g is None else jnp.where(pl.program_id(0) >= starts[k], gk, g)
        else:
            g = g_refs[0][...]
        mn = ADAM_B1 * m_ref[...] + (1.0 - ADAM_B1) * g
        vn = ADAM_B2 * v_ref[...] + (1.0 - ADAM_B2) * (g * g)
        m_hat = mn / (1.0 - ADAM_B1 ** ADAM_STEP)
        v_hat = vn / (1.0 - ADAM_B2 ** ADAM_STEP)
        g_out[...] = g
        d_out[...] = -ADAM_LR * (m_hat / (jnp.sqrt(v_hat) + ADAM_EPS) + ADAM_WD * w_ref[...])
        m_out[...] = mn
        v_out[...] = vn

    row = pl.BlockSpec((tr, C), lambda i: (i, 0))
    if summed:
        gspecs = [pl.BlockSpec((N_DEV, tr, C), lambda i, s=st, n=nb: (0, jnp.clip(i - s, 0, n - 1), 0))
                  for st, nb in zip(starts, counts)]
    else:
        gspecs = [row]
    sh = jax.ShapeDtypeStruct((R, C), F32)
    return pl.pallas_call(
        body, name=name, grid=(R // tr,), in_specs=gspecs + [row, row, row], out_specs=[row] * 4, out_shape=[sh] * 4,
        compiler_params=_cp(("parallel",)),
    )(*parts, w, m, v)


SMALL = ('norm_w', 'conv_b', 'dt_bias', 'a_log', 'd_skip', 'ssm_norm_w', 'final_norm_w')


def _rows(n):
    return -(-n // (8 * LANES)) * 8


def _pack(vals):
    parts = []
    for a in vals:
        f = a.reshape(-1)
        parts.append(jnp.pad(f, (0, _rows(f.size) * LANES - f.size)).reshape(-1, LANES))
    return jnp.concatenate(parts, axis=0)


def _unpack(packed, shapes):
    out, r = [], 0
    for s in shapes:
        n = math.prod(s)
        out.append(packed[r:r + _rows(n)].reshape(-1)[:n].reshape(s))
        r += _rows(n)
    return out


def kernel(x, norm_w, w_in, conv_w, conv_b, dt_bias, a_log, d_skip, ssm_norm_w, w_attn_branch, w_ssm_branch, w_out, final_norm_w, loss_target, m_norm_w, m_w_in, m_conv_w, m_conv_b, m_dt_bias, m_a_log, m_d_skip, m_ssm_norm_w, m_w_attn_branch, m_w_ssm_branch, m_w_out, m_final_norm_w, v_norm_w, v_w_in, v_conv_w, v_conv_b, v_dt_bias, v_a_log, v_d_skip, v_ssm_norm_w, v_w_attn_branch, v_w_ssm_branch, v_w_out, v_final_norm_w):
    cfg = CFG
    D, SH = cfg.D, cfg.SH
    me = _flat(_mesh_pos())
    dt0 = 4 * cfg.AW + cfg.SI + cfg.CD
    ws = w_in.shape[-1]

    g_in, g_cw = gather_two_level([w_in[0].astype(BF16), conv_w[0]], [4, 1], "gather_w_in")
    main_cols, dt_cols = [], []
    for k, parts in enumerate(_shard_columns(cfg, "main", "dt")):
        at = 0
        for which, lo, hi in parts:
            (main_cols if which == "main" else dt_cols).append(g_in[k][:, at:at + hi - lo])
            at += hi - lo
    w_main = jnp.concatenate(main_cols, axis=1)
    w_dt = _pad_lanes(jnp.concatenate(dt_cols, axis=1))
    conv_full = g_cw.transpose(1, 0, 2).reshape(cfg.KC, cfg.CD)
    shards = [w_attn_branch[0].astype(BF16), w_ssm_branch[0].astype(BF16), w_out[0].astype(BF16)]

    loss_p, grad_x, small, recv = local_step(
        cfg, x[0], loss_target[0], norm_w, conv_full, conv_b, dt_bias, a_log, d_skip,
        ssm_norm_w, final_norm_w, w_main, w_dt, shards, dt0)

    upd = {}
    upd['w_in'] = adamw(recv['w_in'], w_in[0], m_w_in[0], v_w_in[0], True, "adamw_w_in")
    upd['w_attn_branch'] = adamw(recv['w_attn'], w_attn_branch[0], m_w_attn_branch[0], v_w_attn_branch[0], True,
                                 "adamw_w_attn")
    upd['w_ssm_branch'] = adamw(recv['w_ssm'], w_ssm_branch[0], m_w_ssm_branch[0], v_w_ssm_branch[0], True,
                                "adamw_w_ssm")
    upd['w_out'] = adamw(recv['w_out'], w_out[0], m_w_out[0], v_w_out[0], True, "adamw_w_out")

    extra = [jnp.zeros((cfg.KC, cfg.CD), F32), jnp.zeros((1, 1), F32)]
    shapes = [small[n].shape for n in SMALL] + [e.shape for e in extra]
    part = _pack([small[n] for n in SMALL] + [small['conv_w'], loss_p[:, :1]])
    gathered, = exchange([part], [True], "gather_small")
    given = dict(norm_w=(norm_w, m_norm_w, v_norm_w), conv_b=(conv_b, m_conv_b, v_conv_b),
                 dt_bias=(dt_bias, m_dt_bias, v_dt_bias), a_log=(a_log, m_a_log, v_a_log),
                 d_skip=(d_skip, m_d_skip, v_d_skip), ssm_norm_w=(ssm_norm_w, m_ssm_norm_w, v_ssm_norm_w),
                 final_norm_w=(final_norm_w, m_final_norm_w, v_final_norm_w))
    packed = [_pack([given[n][t] for n in SMALL] + extra) for t in range(3)]
    outs = adamw([gathered], *packed, True, "adamw_small", tr=part.shape[0])
    unpacked = [_unpack(o, shapes) for o in outs]
    for i, n in enumerate(SMALL):
        upd[n] = [u[i].reshape(given[n][0].shape) for u in unpacked]
    loss = unpacked[0][-1].reshape(())
    cw = conv_w.shape[-1]
    g_cw_mine = lax.dynamic_slice_in_dim(unpacked[0][-2], me * cw, cw, axis=1)
    upd['conv_w'] = adamw(g_cw_mine.reshape(-1, LANES), conv_w.reshape(-1, LANES), m_conv_w.reshape(-1, LANES),
                          v_conv_w.reshape(-1, LANES), False, "adamw_conv_w")

    order = ['norm_w', 'w_in', 'conv_w', 'conv_b', 'dt_bias', 'a_log', 'd_skip', 'ssm_norm_w', 'w_attn_branch',
             'w_ssm_branch', 'w_out', 'final_norm_w']
    like = dict(norm_w=norm_w, w_in=w_in, conv_w=conv_w, conv_b=conv_b, dt_bias=dt_bias, a_log=a_log, d_skip=d_skip,
                ssm_norm_w=ssm_norm_w, w_attn_branch=w_attn_branch, w_ssm_branch=w_ssm_branch, w_out=w_out,
                final_norm_w=final_norm_w)
    result = [loss, grad_x[None]]
    for t in range(4):
        result += [upd[n][t].reshape(like[n].shape) for n in order]
    return tuple(result)
```

```python
import functools
import math
from typing import NamedTuple

import jax
import jax.numpy as jnp
from jax import lax
from jax.experimental import pallas as pl
from jax.experimental.pallas import tpu as pltpu

F32 = jnp.float32
BF16 = jnp.bfloat16
RMS_EPS = 1e-6
NEG = -1e30
N_DEV = 8
CPS = 8
LANES = 128
ATTN_BLOCK = 128
ADAM_LR, ADAM_B1, ADAM_B2, ADAM_EPS, ADAM_WD, ADAM_STEP = 0.001, 0.9, 0.999, 1e-08, 0.01, 10
VMEM_LIMIT = 56 * 1024 * 1024


class Cfg(NamedTuple):
    D: int = 2048
    S: int = 8192
    AH: int = 16
    E: int = 128
    patterns: tuple = ((128, 1), (512, 4), (2048, 16))
    SI: int = 4096
    P: int = 64
    SG: int = 8
    SN: int = 128
    KC: int = 4
    L: int = 128

    @property
    def AW(self): return self.AH * self.E
    @property
    def SH(self): return self.SI // self.P
    @property
    def HPG(self): return self.SH // self.SG
    @property
    def GN(self): return self.SG * self.SN
    @property
    def CD(self): return self.SI + 2 * self.GN
    @property
    def k0(self): return self.AW
    @property
    def v0(self): return 2 * self.AW
    @property
    def za0(self): return 3 * self.AW
    @property
    def zs0(self): return 4 * self.AW
    @property
    def xbc0(self): return 4 * self.AW + self.SI
    @property
    def ga0(self): return self.xbc0 + self.CD
    @property
    def gs0(self): return self.ga0 + self.D
    @property
    def NP(self): return self.gs0 + self.D
    @property
    def N_IN(self): return self.NP + self.SH


CFG = Cfg()


def _cp(sem=None, vmem=VMEM_LIMIT):
    return pltpu.CompilerParams(dimension_semantics=sem, vmem_limit_bytes=vmem)


def _sigmoid(z):
    return 1.0 / (1.0 + jnp.exp(-z))


def _dot(a, b, dims):
    return lax.dot_general(a, b, (dims, ((), ())), preferred_element_type=F32)


NN = ((1,), (0,))
NT = ((1,), (1,))
TN = ((0,), (0,))


def _blk(off, width):
    assert off % width == 0, (off, width)
    return off // width


def matmul(a, b, mode, tm, tn, tk, out_dtype, name, side=None):
    if mode == 'nn':
        (M, K), (_, N) = a.shape, b.shape
    elif mode == 'nt':
        (M, K), (N, _) = a.shape, b.shape
    else:
        (K, M), (_, N) = a.shape, b.shape
    tm, tn, tk = min(tm, M), min(tn, N), min(tk, K)
    assert M % tm == 0 and N % tn == 0 and K % tk == 0, (M, N, K, tm, tn, tk)
    nk = K // tk
    dims = {'nn': NN, 'nt': NT, 'tn': TN}[mode]

    def body(a_ref, b_ref, o_ref, *acc):
        part = _dot(a_ref[...].astype(BF16), b_ref[...].astype(BF16), dims)
        if nk == 1:
            o_ref[...] = part.astype(out_dtype)
        else:
            acc_ref, = acc
            k = pl.program_id(2)

            @pl.when(k == 0)
            def _():
                acc_ref[...] = part

            @pl.when(k > 0)
            def _():
                acc_ref[...] += part

            @pl.when(k == nk - 1)
            def _():
                o_ref[...] = acc_ref[...].astype(out_dtype)

    if mode == 'tn':
        a_spec = pl.BlockSpec((tk, tm), lambda n, m, k: (k, m))
    else:
        a_spec = pl.BlockSpec((tm, tk), lambda n, m, k: (m, k))
    if mode == 'nt':
        b_spec = pl.BlockSpec((tn, tk), lambda n, m, k: (n, k))
    else:
        b_spec = pl.BlockSpec((tk, tn), lambda n, m, k: (k, n))
    grid = (N // tn, M // tm, nk)
    o_spec = pl.BlockSpec((tm, tn), lambda n, m, k: (m, n))
    o_shape = jax.ShapeDtypeStruct((M, N), out_dtype)
    acc = [] if nk == 1 else [pltpu.VMEM((tm, tn), F32)]
    if side is None:
        return pl.pallas_call(
            body, name=name, grid=grid, in_specs=[a_spec, b_spec], out_specs=o_spec, out_shape=o_shape,
            scratch_shapes=acc, compiler_params=_cp(("parallel", "parallel", "arbitrary")),
        )(a, b)
    arrs, gathers = side
    whole = pl.BlockSpec(memory_space=pl.ANY)
    res = pl.pallas_call(
        with_exchange(body, 2, 1, gathers, grid), name=name, grid=grid,
        in_specs=[a_spec, b_spec] + [whole] * len(arrs), out_specs=[o_spec] + [whole] * len(arrs),
        out_shape=[o_shape] + _exchange_shapes(arrs, gathers),
        scratch_shapes=acc + _exchange_sems(len(arrs)),
        compiler_params=_cp(("arbitrary", "arbitrary", "arbitrary")),
    )(a, b, *arrs)
    return res[0], res[1:]


def rmsnorm_fwd(x, w, name, tm=256):
    S, D = x.shape

    def body(x_ref, w_ref, o_ref):
        xv = x_ref[...]
        r = lax.rsqrt(jnp.mean(xv * xv, axis=-1, keepdims=True) + RMS_EPS)
        o_ref[...] = ((xv * r) * w_ref[...]).astype(BF16)

    return pl.pallas_call(
        body, name=name, grid=(S // tm,),
        in_specs=[pl.BlockSpec((tm, D), lambda i: (i, 0)), pl.BlockSpec((1, D), lambda i: (0, 0))],
        out_specs=pl.BlockSpec((tm, D), lambda i: (i, 0)),
        out_shape=jax.ShapeDtypeStruct((S, D), BF16),
        compiler_params=_cp(("parallel",)),
    )(x, w)


def rmsnorm_bwd(dh_a, dh_b, x, w, dout, name, tm=128):
    S, D = x.shape

    def body(da_ref, db_ref, x_ref, w_ref, do_ref, gx_ref, gw_ref):
        xv = x_ref[...]
        dh = da_ref[...] + db_ref[...]
        r = lax.rsqrt(jnp.mean(xv * xv, axis=-1, keepdims=True) + RMS_EPS)
        g = dh * w_ref[...]
        dx = r * g - xv * (r * r * r) * jnp.mean(g * xv, axis=-1, keepdims=True)
        gx_ref[...] = do_ref[...] + dx
        gw = jnp.sum(dh * (xv * r), axis=0, keepdims=True)

        @pl.when(pl.program_id(0) == 0)
        def _():
            gw_ref[...] = gw

        @pl.when(pl.program_id(0) > 0)
        def _():
            gw_ref[...] += gw

    row = pl.BlockSpec((tm, D), lambda i: (i, 0))
    vec = pl.BlockSpec((1, D), lambda i: (0, 0))
    return pl.pallas_call(
        body, name=name, grid=(S // tm,),
        in_specs=[row, row, row, vec, row],
        out_specs=[row, vec],
        out_shape=[jax.ShapeDtypeStruct((S, D), F32), jax.ShapeDtypeStruct((1, D), F32)],
        compiler_params=_cp(("arbitrary",)),
    )(dh_a, dh_b, x, w, dout)


def out_proj_final(merged, w_out, x, fw, tgt, name, tm=256):
    S, D = x.shape

    def body(m_ref, wo_ref, x_ref, w_ref, t_ref, do_ref, loss_ref, gw_ref):
        out = x_ref[...] + _dot(m_ref[...], wo_ref[...], NN)
        w = w_ref[...]
        r = lax.rsqrt(jnp.mean(out * out, axis=-1, keepdims=True) + RMS_EPS)
        yn = out * r
        err = yn * w - t_ref[...]
        lrow = 0.5 * jnp.mean(err * err, axis=-1, keepdims=True)
        lsum = jnp.zeros((1, LANES), F32) + jnp.sum(lrow, axis=0, keepdims=True)
        dfin = err * (1.0 / D)
        g = dfin * w
        do_ref[...] = r * g - out * (r * r * r) * jnp.mean(g * out, axis=-1, keepdims=True)
        gw = jnp.sum(dfin * yn, axis=0, keepdims=True)

        @pl.when(pl.program_id(0) == 0)
        def _():
            gw_ref[...] = gw
            loss_ref[...] = lsum

        @pl.when(pl.program_id(0) > 0)
        def _():
            gw_ref[...] += gw
            loss_ref[...] += lsum

    row = pl.BlockSpec((tm, D), lambda i: (i, 0))
    vec = pl.BlockSpec((1, D), lambda i: (0, 0))
    return pl.pallas_call(
        body, name=name, grid=(S // tm,),
        in_specs=[row, pl.BlockSpec((D, D), lambda i: (0, 0)), row, vec, row],
        out_specs=[row, pl.BlockSpec((1, LANES), lambda i: (0, 0)), vec],
        out_shape=[jax.ShapeDtypeStruct((S, D), F32), jax.ShapeDtypeStruct((1, LANES), F32),
                   jax.ShapeDtypeStruct((1, D), F32)],
        compiler_params=_cp(("arbitrary",)),
    )(merged, w_out, x, fw, tgt)


def branch_merge(o_a, w_attn, y_n, w_ssm, proj, cfg, name, tm=512, tn=512):
    S, D = cfg.S, cfg.D
    tm, tn = min(tm, S), min(tn, D)

    def body(oa_ref, wa_ref, yn_ref, ws_ref, ga_ref, gs_ref, a_ref, s_ref, m_ref):
        a = _dot(oa_ref[...], wa_ref[...], NN)
        sv = _dot(yn_ref[...], ws_ref[...], NN)
        a_ref[...] = a.astype(BF16)
        s_ref[...] = sv.astype(BF16)
        m_ref[...] = (_sigmoid(ga_ref[...]) * a + _sigmoid(gs_ref[...]) * sv).astype(BF16)

    tile = pl.BlockSpec((tm, tn), lambda n, m: (m, n))
    return pl.pallas_call(
        body, name=name, grid=(D // tn, S // tm),
        in_specs=[pl.BlockSpec((tm, cfg.AW), lambda n, m: (m, 0)), pl.BlockSpec((cfg.AW, tn), lambda n, m: (0, n)),
                  pl.BlockSpec((tm, cfg.SI), lambda n, m: (m, 0)), pl.BlockSpec((cfg.SI, tn), lambda n, m: (0, n)),
                  pl.BlockSpec((tm, tn), lambda n, m: (m, _blk(cfg.ga0, tn) + n)),
                  pl.BlockSpec((tm, tn), lambda n, m: (m, _blk(cfg.gs0, tn) + n))],
        out_specs=[tile, tile, tile],
        out_shape=[jax.ShapeDtypeStruct((S, D), BF16)] * 3,
        compiler_params=_cp(("parallel", "parallel")),
    )(o_a, w_attn, y_n, w_ssm, proj, proj)


def merge_bwd(dout, w_out, a_out, s_out, proj, cfg, name, tm=512, tn=1024):
    S, D = cfg.S, cfg.D
    tm, tn = min(tm, S), min(tn, D)

    def body(do_ref, wo_ref, a_ref, s_ref, ga_ref, gs_ref, da_ref, ds_ref, dga_ref, dgs_ref):
        dmv = _dot(do_ref[...].astype(BF16), wo_ref[...], NT)
        sa = _sigmoid(ga_ref[...])
        ss = _sigmoid(gs_ref[...])
        da_ref[...] = (dmv * sa).astype(BF16)
        ds_ref[...] = (dmv * ss).astype(BF16)
        dga_ref[...] = (dmv * a_ref[...] * (sa * (1.0 - sa))).astype(BF16)
        dgs_ref[...] = (dmv * s_ref[...] * (ss * (1.0 - ss))).astype(BF16)

    tile = pl.BlockSpec((tm, tn), lambda n, m: (m, n))
    sh = jax.ShapeDtypeStruct((S, D), BF16)
    return pl.pallas_call(
        body, name=name, grid=(D // tn, S // tm),
        in_specs=[pl.BlockSpec((tm, D), lambda n, m: (m, 0)), pl.BlockSpec((tn, D), lambda n, m: (n, 0)), tile, tile,
                  pl.BlockSpec((tm, tn), lambda n, m: (m, _blk(cfg.ga0, tn) + n)),
                  pl.BlockSpec((tm, tn), lambda n, m: (m, _blk(cfg.gs0, tn) + n))],
        out_specs=[tile] * 4, out_shape=[sh] * 4,
        compiler_params=_cp(("parallel", "parallel")),
    )(dout, w_out, a_out, s_out, proj, proj)


def _attn_rows(base, d):
    return pl.ds(base, ATTN_BLOCK) if d == 1 else pl.ds(base, ATTN_BLOCK, stride=d)


def _attn_units(cfg):
    dmax = max(d for _, d in cfg.patterns)
    units = []
    for p, (window, d) in enumerate(cfg.patterns):
        assert window // d == ATTN_BLOCK and dmax % d == 0
        nsub = dmax // d
        for b in range(nsub):
            for r in range(d):
                base = b * ATTN_BLOCK * d + r
                if b > 0:
                    units.append((p, d, base, (b - 1) * ATTN_BLOCK * d + r, False))
                else:
                    units.append((p, d, base, (nsub - 1) * ATTN_BLOCK * d + r, True))
    return units, ATTN_BLOCK * dmax


def _set_bias_tiles(bias_s, slope, cfg):
    qi = lax.broadcasted_iota(jnp.int32, (ATTN_BLOCK, ATTN_BLOCK), 0)
    ki = lax.broadcasted_iota(jnp.int32, (ATTN_BLOCK, ATTN_BLOCK), 1)
    for p, (_, d) in enumerate(cfg.patterns):
        bias_s[2 * p] = jnp.where(ki >= qi, (-slope) * ((ATTN_BLOCK + qi - ki) * d).astype(F32), NEG)
        bias_s[2 * p + 1] = jnp.where(ki <= qi, (-slope) * ((qi - ki) * d).astype(F32), NEG)


def _unit_scores(q, kcat, bias_s, p, prev_ok, scale):
    s = _dot(q, kcat, NT) * scale + jnp.concatenate([bias_s[2 * p], bias_s[2 * p + 1]], axis=1)
    if prev_ok is not None:
        cur_half = lax.broadcasted_iota(jnp.int32, s.shape, 1) >= ATTN_BLOCK
        s = jnp.where(jnp.logical_or(cur_half, prev_ok), s, NEG)
    return s


def _slope_table(cfg):
    slopes = jnp.asarray([2.0 ** (-8.0 * (h + 1) / cfg.AH) for h in range(cfg.AH)], F32)
    return jnp.broadcast_to(slopes.reshape(cfg.AH, 1, 1), (cfg.AH, 8, LANES))


def attn_fused_fwd(proj, slopes, cfg, name):
    S, E, AH = cfg.S, cfg.E, cfg.AH
    units, SB = _attn_units(cfg)
    assert S % SB == 0
    npat = len(cfg.patterns)
    scale = E ** -0.5

    def spec(off, prev):
        c0 = _blk(off, E)
        if prev:
            return pl.BlockSpec((SB, E), lambda h, i: (jnp.maximum(i - 1, 0), c0 + h))
        return pl.BlockSpec((SB, E), lambda h, i: (i, c0 + h))

    def body(q_ref, kp_ref, kc_ref, vp_ref, vc_ref, z_ref, sl_ref, oa_ref, om_ref, lt_ref, *scr):
        o_s, l_s, bias_s = scr[:npat], scr[npat:2 * npat], scr[2 * npat]
        i = pl.program_id(1)

        @pl.when(i == 0)
        def _():
            _set_bias_tiles(bias_s, sl_ref[0, 0:1, :], cfg)

        for p, d, base, pbase, from_prev in units:
            rows, prows = _attn_rows(base, d), _attn_rows(pbase, d)
            q = q_ref[rows, :].astype(BF16)
            kp = (kp_ref if from_prev else kc_ref)[prows, :].astype(BF16)
            vp = (vp_ref if from_prev else vc_ref)[prows, :].astype(BF16)
            kcat = jnp.concatenate([kp, kc_ref[rows, :].astype(BF16)], axis=0)
            vcat = jnp.concatenate([vp, vc_ref[rows, :].astype(BF16)], axis=0)
            s = _unit_scores(q, kcat, bias_s, p, (i > 0) if from_prev else None, scale)
            m = jnp.max(s, axis=1, keepdims=True)
            pr = jnp.exp(s - m)
            l = jnp.sum(pr, axis=1, keepdims=True)
            o_s[p][rows, :] = _dot(pr.astype(BF16), vcat, NN) * (1.0 / l)
            l_s[p][rows, :] = m + jnp.log(l)
        ls = [l_s[p][...] for p in range(npat)]
        m = functools.reduce(jnp.maximum, ls)
        lt = m + jnp.log(sum(jnp.exp(l_ - m) for l_ in ls))
        lt_ref[...] = lt
        mix = sum(jnp.exp(ls[p] - lt) * o_s[p][...] for p in range(npat))
        om_ref[...] = mix
        z = z_ref[...]
        oa_ref[...] = (mix * (z * _sigmoid(z))).astype(BF16)

    out = pl.BlockSpec((SB, E), lambda h, i: (i, h))
    return pl.pallas_call(
        body, name=name, grid=(AH, S // SB),
        in_specs=[spec(0, False), spec(cfg.k0, True), spec(cfg.k0, False), spec(cfg.v0, True), spec(cfg.v0, False),
                  spec(cfg.za0, False), pl.BlockSpec((1, 8, LANES), lambda h, i: (h, 0, 0))],
        out_specs=[out, out, pl.BlockSpec((SB, 1), lambda h, i: (h * (S // SB) + i, 0))],
        out_shape=[jax.ShapeDtypeStruct((S, cfg.AW), BF16), jax.ShapeDtypeStruct((S, cfg.AW), F32),
                   jax.ShapeDtypeStruct((AH * S, 1), F32)],
        scratch_shapes=[pltpu.VMEM((SB, E), F32)] * npat + [pltpu.VMEM((SB, 1), F32)] * npat +
        [pltpu.VMEM((2 * npat, ATTN_BLOCK, ATTN_BLOCK), F32)],
        compiler_params=_cp(("parallel", "arbitrary")),
    )(proj, proj, proj, proj, proj, proj, slopes)


def attn_fused_bwd(proj, do_a, o_mix, ltot, slopes, cfg, name):
    S, E, AH = cfg.S, cfg.E, cfg.AH
    units, SB = _attn_units(cfg)
    nsb = S // SB
    last = nsb - 1
    scale = E ** -0.5

    def spec(off, prev):
        c0 = _blk(off, E)
        if prev:
            return pl.BlockSpec((SB, E), lambda h, i: (jnp.maximum(i - 1, 0), c0 + h))
        return pl.BlockSpec((SB, E), lambda h, i: (jnp.minimum(i, last), c0 + h))

    cur = pl.BlockSpec((SB, E), lambda h, i: (jnp.minimum(i, last), h))
    prev = pl.BlockSpec((SB, E), lambda h, i: (jnp.maximum(i - 1, 0), h))

    def body(q_ref, kp_ref, kc_ref, vp_ref, vc_ref, z_ref, doa_ref, om_ref, lt_ref, sl_ref,
             dq_ref, dk_ref, dv_ref, dz_ref, dmix_s, dl_s, dq_s, dkp_s, dvp_s, dkc_s, dvc_s, bias_s):
        i = pl.program_id(1)

        @pl.when(i == 0)
        def _():
            dkc_s[...] = jnp.zeros_like(dkc_s)
            dvc_s[...] = jnp.zeros_like(dvc_s)
            _set_bias_tiles(bias_s, sl_ref[0, 0:1, :], cfg)

        @pl.when(i < nsb)
        def _():
            z = z_ref[...]
            s = _sigmoid(z)
            doa = doa_ref[...].astype(F32)
            om = om_ref[...]
            dmix = doa * (z * s)
            dmix_s[...] = dmix
            dz_ref[...] = (doa * om * (s * (1.0 + z * (1.0 - s)))).astype(BF16)
            dl_s[...] = jnp.sum(dmix * om, axis=1, keepdims=True)
            dkp_s[...] = dkc_s[...]
            dvp_s[...] = dvc_s[...]
            dkc_s[...] = jnp.zeros_like(dkc_s)
            dvc_s[...] = jnp.zeros_like(dvc_s)
            dq_s[...] = jnp.zeros_like(dq_s)
            for p, d, base, pbase, from_prev in units:
                rows, prows = _attn_rows(base, d), _attn_rows(pbase, d)
                q = q_ref[rows, :].astype(BF16)
                kc = kc_ref[rows, :].astype(BF16)
                kp = (kp_ref if from_prev else kc_ref)[prows, :].astype(BF16)
                vp = (vp_ref if from_prev else vc_ref)[prows, :].astype(BF16)
                do = dmix_s[rows, :].astype(BF16)
                lt = lt_ref[rows, :]
                dlt = dl_s[rows, :]
                kcat = jnp.concatenate([kp, kc], axis=0)
                vcat = jnp.concatenate([vp, vc_ref[rows, :].astype(BF16)], axis=0)
                pr = jnp.exp(_unit_scores(q, kcat, bias_s, p, (i > 0) if from_prev else None, scale) - lt)
                ds = (pr * (_dot(do, vcat, NT) - dlt) * scale).astype(BF16)
                dq_s[rows, :] += _dot(ds, kcat, NN)
                dkcat = _dot(ds, q, TN)
                dvcat = _dot(pr.astype(BF16), do, TN)
                dk_t, dv_t = (dkp_s, dvp_s) if from_prev else (dkc_s, dvc_s)
                dk_t[prows, :] += dkcat[:ATTN_BLOCK, :]
                dv_t[prows, :] += dvcat[:ATTN_BLOCK, :]
                dkc_s[rows, :] += dkcat[ATTN_BLOCK:, :]
                dvc_s[rows, :] += dvcat[ATTN_BLOCK:, :]
            dq_ref[...] = dq_s[...].astype(BF16)
            dk_ref[...] = dkp_s[...].astype(BF16)
            dv_ref[...] = dvp_s[...].astype(BF16)

        @pl.when(i == nsb)
        def _():
            dk_ref[...] = dkc_s[...].astype(BF16)
            dv_ref[...] = dvc_s[...].astype(BF16)

    sh = jax.ShapeDtypeStruct((S, cfg.AW), BF16)
    acc = pltpu.VMEM((SB, E), F32)
    return pl.pallas_call(
        body, name=name, grid=(AH, nsb + 1),
        in_specs=[spec(0, False), spec(cfg.k0, True), spec(cfg.k0, False), spec(cfg.v0, True), spec(cfg.v0, False),
                  spec(cfg.za0, False), cur, cur,
                  pl.BlockSpec((SB, 1), lambda h, i: (h * nsb + jnp.minimum(i, last), 0)),
                  pl.BlockSpec((1, 8, LANES), lambda h, i: (h, 0, 0))],
        out_specs=[cur, prev, prev, cur], out_shape=[sh] * 4,
        scratch_shapes=[acc, pltpu.VMEM((SB, 1), F32), acc, acc, acc, acc, acc,
                        pltpu.VMEM((2 * len(cfg.patterns), ATTN_BLOCK, ATTN_BLOCK), F32)],
        compiler_params=_cp(("parallel", "arbitrary")),
    )(proj, proj, proj, proj, proj, proj, do_a, o_mix, ltot, slopes)


HALO = 8


def _conv_taps(x_ref, h_ref, kc):
    x = x_ref[...]
    full = jnp.concatenate([jnp.where(pl.program_id(1) == 0, 0.0, h_ref[...]), x], axis=0)
    return [pltpu.roll(full, s, axis=0)[HALO:, :] for s in range(kc - 1, 0, -1)] + [x]


def _conv_pre(taps, w_ref, b_ref):
    pre = b_ref[...] + w_ref[0:1, :] * taps[0]
    for k in range(1, len(taps)):
        pre = pre + w_ref[k:k + 1, :] * taps[k]
    return pre


def conv_fwd(proj, w, b, cfg, name, tm=1024, tc=512):
    S, CD, KC = cfg.S, cfg.CD, cfg.KC
    tc = min(tc, CD)
    c0 = _blk(cfg.xbc0, tc)
    hb = tm // HALO

    def body(x_ref, h_ref, w_ref, b_ref, o_ref):
        pre = _conv_pre(_conv_taps(x_ref, h_ref, KC), w_ref, b_ref)
        o_ref[...] = pre * _sigmoid(pre)

    return pl.pallas_call(
        body, name=name, grid=(CD // tc, S // tm),
        in_specs=[pl.BlockSpec((tm, tc), lambda c, i: (i, c0 + c)),
                  pl.BlockSpec((HALO, tc), lambda c, i: (jnp.maximum(i * hb - 1, 0), c0 + c)),
                  pl.BlockSpec((KC, tc), lambda c, i: (0, c)),
                  pl.BlockSpec((1, tc), lambda c, i: (0, c))],
        out_specs=pl.BlockSpec((tm, tc), lambda c, i: (i, c)),
        out_shape=jax.ShapeDtypeStruct((S, CD), F32),
        compiler_params=_cp(("parallel", "arbitrary")),
    )(proj, proj, w, b)


def conv_bwd(proj, dxc, w, b, cfg, name, c_off, tm=1024, tc=512):
    S, KC = cfg.S, cfg.KC
    CD = dxc.shape[1]
    tc = min(tc, CD)
    c0 = _blk(cfg.xbc0 + c_off, tc)
    w0 = _blk(c_off, tc)
    hb = tm // HALO
    nrb = S // tm
    last_h = S // HALO - 1

    def body(x_ref, hp_ref, hn_ref, d_ref, dn_ref, w_ref, b_ref, o_ref, gw_ref, gb_ref):
        i = pl.program_id(1)
        x = x_ref[...]
        full = jnp.concatenate([jnp.where(i == 0, 0.0, hp_ref[...]), x, hn_ref[...]], axis=0)
        rows = tm + HALO
        taps = [pltpu.roll(full, s_, axis=0)[HALO:, :] for s_ in range(KC - 1, 0, -1)] + [full[HALO:, :]]
        pre = _conv_pre(taps, w_ref, b_ref)
        sg = _sigmoid(pre)
        d_ext = jnp.concatenate([d_ref[...], jnp.where(i == nrb - 1, 0.0, dn_ref[...])], axis=0)
        dpre = d_ext * (sg * (1.0 + pre * (1.0 - sg)))
        own = dpre[:tm, :]
        acc = w_ref[KC - 1:KC, :] * own
        for j in range(1, KC):
            acc = acc + w_ref[KC - 1 - j:KC - j, :] * pltpu.roll(dpre, rows - j, axis=0)[:tm, :]
        o_ref[...] = acc.astype(BF16)
        gb = jnp.sum(own, axis=0, keepdims=True)
        gws = [jnp.sum(own * taps[k][:tm, :], axis=0, keepdims=True) for k in range(KC)]
        gw = jnp.concatenate(gws + [jnp.zeros((8 - KC, tc), F32)], axis=0)

        @pl.when(i == 0)
        def _():
            gw_ref[...] = gw
            gb_ref[...] = gb

        @pl.when(i > 0)
        def _():
            gw_ref[...] += gw
            gb_ref[...] += gb

    return pl.pallas_call(
        body, name=name, grid=(CD // tc, nrb),
        in_specs=[pl.BlockSpec((tm, tc), lambda c, i: (i, c0 + c)),
                  pl.BlockSpec((HALO, tc), lambda c, i: (jnp.maximum(i * hb - 1, 0), c0 + c)),
                  pl.BlockSpec((HALO, tc), lambda c, i: (jnp.minimum((i + 1) * hb, last_h), c0 + c)),
                  pl.BlockSpec((tm, tc), lambda c, i: (i, c)),
                  pl.BlockSpec((HALO, tc), lambda c, i: (jnp.minimum((i + 1) * hb, last_h), c)),
                  pl.BlockSpec((KC, tc), lambda c, i: (0, w0 + c)),
                  pl.BlockSpec((1, tc), lambda c, i: (0, w0 + c))],
        out_specs=[pl.BlockSpec((tm, tc), lambda c, i: (i, c)),
                   pl.BlockSpec((8, tc), lambda c, i: (0, c)),
                   pl.BlockSpec((1, tc), lambda c, i: (0, c))],
        out_shape=[jax.ShapeDtypeStruct((S, CD), BF16), jax.ShapeDtypeStruct((8, CD), F32),
                   jax.ShapeDtypeStruct((1, CD), F32)],
        compiler_params=_cp(("parallel", "arbitrary")),
    )(proj, proj, proj, dxc, dxc, w, b)


def _pad_lanes(v, width=LANES):
    return jnp.pad(v, ((0, 0), (0, width - v.shape[1])))


def ssd_prep(dt_raw, dt_bias, a_log, cfg, name):
    S, L = cfg.S, cfg.L

    def body(x_ref, b_ref, al_ref, dt_ref, ac_ref):
        x = x_ref[...] + b_ref[...]
        dt = jnp.maximum(x, 0.0) + jnp.log(1.0 + jnp.exp(-jnp.abs(x)))
        da = dt * (-jnp.exp(al_ref[...]))
        li = lax.broadcasted_iota(jnp.int32, (L, L), 0)
        si = lax.broadcasted_iota(jnp.int32, (L, L), 1)
        tri = jnp.where(li >= si, 1.0, 0.0).astype(F32)
        dt_ref[...] = dt
        ac_ref[...] = lax.dot_general(tri, da, ((NN), ((), ())), precision=lax.Precision.HIGHEST,
                                      preferred_element_type=F32)

    row = pl.BlockSpec((L, LANES), lambda i: (i, 0))
    vec = pl.BlockSpec((1, LANES), lambda i: (0, 0))
    sh = jax.ShapeDtypeStruct((S, LANES), F32)
    return pl.pallas_call(
        body, name=name, grid=(S // L,), in_specs=[row, vec, vec], out_specs=[row, row], out_shape=[sh, sh],
        compiler_params=_cp(("parallel",)),
    )(dt_raw, dt_bias, a_log)


def _spread(v, n):
    return jnp.broadcast_to(v[:, :, None], v.shape + (n,)).reshape(v.shape[0], v.shape[1] * n)


def _head_selectors(cfg):
    def sel(width):
        head = jnp.arange(LANES)[None, :, None]
        slot = jnp.arange(cfg.SG)[:, None, None] * cfg.HPG + (jnp.arange(cfg.HPG * width) // width)[None, None, :]
        return (head == slot).astype(BF16)
    return sel(cfg.P), sel(LANES)


def _spread_heads(v, sel):
    n = v.shape[0]
    hi = v.astype(BF16)
    r1 = v - hi.astype(F32)
    mid = r1.astype(BF16)
    lo = (r1 - mid.astype(F32)).astype(BF16)
    out = _dot(jnp.concatenate([hi, mid, lo], axis=0), sel, NN)
    return out[:n] + out[n:2 * n] + out[2 * n:]


def _pair_lanes(wide, hpg, p):
    low = lax.broadcasted_iota(jnp.int32, (wide.shape[0], LANES), 1) < p
    return jnp.concatenate([jnp.where(low, wide[:, 2 * jp * LANES:(2 * jp + 1) * LANES],
                                      wide[:, (2 * jp + 1) * LANES:(2 * jp + 2) * LANES])
                            for jp in range(hpg // 2)], axis=1)


def _pair_select(halves, p):
    low = lax.broadcasted_iota(jnp.int32, halves[0].shape, 1) < p
    return jnp.where(low, halves[0], halves[1])


def _head_rows(row, hpg, p):
    return jnp.concatenate([jnp.broadcast_to(row[:, j * LANES:(j + 1) * LANES], (p, LANES)) for j in range(hpg)],
                           axis=0)


def _segment_sums(t, sel):
    r = t.shape[0]
    hi = t.astype(BF16)
    out = _dot(jnp.concatenate([hi, (t - hi.astype(F32)).astype(BF16)], axis=0), sel, NT)
    return out[:r] + out[r:]


def ssd_scan_fwd(xc, dt, acum, act, sel_p, sel_l, cfg, name):
    S, L, P, SN, HPG, SG, SI = cfg.S, cfg.L, cfg.P, cfg.SN, cfg.HPG, cfg.SG, cfg.SI
    nc = S // L
    GW = HPG * P
    bcol, ccol = _blk(SI, SN), _blk(SI + cfg.GN, SN)
    assert nc % CPS == 0

    def body(xs_ref, b_ref, c_ref, dtn_ref, acn_ref, at_ref, sp_ref, sl_ref, y_ref, st_ref, st):
        @pl.when(pl.program_id(1) == 0)
        def _():
            st[...] = jnp.zeros_like(st)

        causal = lax.broadcasted_iota(jnp.int32, (L, L), 0) >= lax.broadcasted_iota(jnp.int32, (L, L), 1)
        for ci in range(CPS):
            rows = slice(ci * L, (ci + 1) * L)
            acn = acn_ref[rows, :]
            dts = _spread_heads(dtn_ref[rows, :], sp_ref[0])
            acs = _spread_heads(acn, sl_ref[0])
            a_p = _pair_lanes(acs, HPG, P)
            s0 = st[...]
            st_ref[ci] = s0.reshape(HPG, P, SN)
            B = b_ref[rows, :].astype(BF16)
            C = c_ref[rows, :].astype(BF16)
            G = _dot(C, B, NT)
            xdt = xs_ref[rows, :] * dts
            xdtb = xdt.astype(BF16)
            ws = jnp.exp(a_p[L - 1:L, :] - a_p)
            yo = jnp.exp(a_p) * _dot(C, s0.astype(BF16), NT)
            yd = []
            for jp in range(HPG // 2):
                x_pair = xdtb[:, jp * LANES:(jp + 1) * LANES]
                halves = []
                for j in (2 * jp, 2 * jp + 1):
                    dm = jnp.where(causal, jnp.exp(acs[:, j * LANES:(j + 1) * LANES] - at_ref[j:j + 1, rows]), 0.0)
                    halves.append(_dot((G * dm).astype(BF16), x_pair, NN))
                yd.append(_pair_select(halves, P))
            y_ref[rows, :] = jnp.concatenate(yd, axis=1) + yo
            st[...] = _head_rows(jnp.exp(acs[L - 1:L, :]), HPG, P) * s0 + _dot((xdt * ws).astype(BF16), B, TN)

    R = CPS * L
    y, states = pl.pallas_call(
        body, name=name, grid=(SG, nc // CPS),
        in_specs=[pl.BlockSpec((R, GW), lambda g, c: (c, g)),
                  pl.BlockSpec((R, SN), lambda g, c: (c, bcol + g)),
                  pl.BlockSpec((R, SN), lambda g, c: (c, ccol + g)),
                  pl.BlockSpec((R, LANES), lambda g, c: (c, 0)),
                  pl.BlockSpec((R, LANES), lambda g, c: (c, 0)),
                  pl.BlockSpec((HPG, R), lambda g, c: (g, c)),
                  pl.BlockSpec((1, LANES, GW), lambda g, c: (g, 0, 0)),
                  pl.BlockSpec((1, LANES, HPG * LANES), lambda g, c: (g, 0, 0))],
        out_specs=[pl.BlockSpec((R, GW), lambda g, c: (c, g)),
                   pl.BlockSpec((CPS, HPG, P, SN), lambda g, c: (c, g, 0, 0))],
        out_shape=[jax.ShapeDtypeStruct((S, SI), F32), jax.ShapeDtypeStruct((nc, cfg.SH, P, SN), F32)],
        scratch_shapes=[pltpu.VMEM((GW, SN), F32)],
        compiler_params=_cp(("parallel", "arbitrary")),
    )(xc, xc, xc, dt, acum, act, sel_p, sel_l)
    return y, states


def ssd_scan_bwd(xc, dt, acum, act, sel_p, sel_l, states, y, dy, dvec, cfg, name, side):
    S, L, P, SN, HPG, SG, SI = cfg.S, cfg.L, cfg.P, cfg.SN, cfg.HPG, cfg.SG, cfg.SI
    nc = S // L
    GW = HPG * P
    bcol, ccol = _blk(SI, SN), _blk(SI + cfg.GN, SN)

    def rc(c):
        return nc // CPS - 1 - c

    def body(xs_ref, b_ref, c_ref, dtn_ref, acn_ref, at_ref, sp_ref, sl_ref, st_ref, y_ref, dy_ref, dk_ref,
             dxs_ref, db_ref, dc_ref, dac_ref, dxsum_ref, dst):
        @pl.when(pl.program_id(1) == 0)
        def _():
            dst[...] = jnp.zeros_like(dst)

        sel = sp_ref[0]
        causal = lax.broadcasted_iota(jnp.int32, (L, L), 0) >= lax.broadcasted_iota(jnp.int32, (L, L), 1)
        low = lax.broadcasted_iota(jnp.int32, (L, LANES), 1) < P
        is_last = lax.broadcasted_iota(jnp.int32, (L, LANES), 0) == L - 1
        ones = jnp.ones((16, SN), BF16)
        for ci in reversed(range(CPS)):
            rows = slice(ci * L, (ci + 1) * L)
            acn = acn_ref[rows, :]
            dts = _spread_heads(dtn_ref[rows, :], sel)
            acs = _spread_heads(acn, sl_ref[0])
            a_p = _pair_lanes(acs, HPG, P)
            B = b_ref[rows, :].astype(BF16)
            C = c_ref[rows, :].astype(BF16)
            G = _dot(C, B, NT)
            xs = xs_ref[rows, :]
            dY = dy_ref[rows, :].astype(F32)
            xdt = xs * dts
            xdtb = xdt.astype(BF16)
            dYb = dY.astype(BF16)
            s0 = st_ref[ci].reshape(GW, SN)
            s0b = s0.astype(BF16)
            ds1 = dst[...]
            ds1b = ds1.astype(BF16)
            ws = jnp.exp(a_p[L - 1:L, :] - a_p)
            dR = (jnp.exp(a_p) * dY).astype(BF16)
            dX2 = ws * _dot(B, ds1b, NT)
            dgsum = jnp.zeros((L, L), F32)
            dX1, yd = [], []
            for jp in range(HPG // 2):
                lanes = slice(jp * LANES, (jp + 1) * LANES)
                x_pair, dy_pair = xdtb[:, lanes], dYb[:, lanes]
                h1, h2 = [], []
                for h, j in enumerate((2 * jp, 2 * jp + 1)):
                    dm = jnp.where(causal, jnp.exp(acs[:, j * LANES:(j + 1) * LANES] - at_ref[j:j + 1, rows]), 0.0)
                    mine = low if h == 0 else jnp.logical_not(low)
                    dgsum = dgsum + _dot(jnp.where(mine, dy_pair, jnp.zeros_like(dy_pair)), x_pair, NT) * dm
                    Mb = (G * dm).astype(BF16)
                    h1.append(_dot(Mb, dy_pair, TN))
                    h2.append(_dot(Mb, x_pair, NN))
                dX1.append(_pair_select(h1, P))
                yd.append(_pair_select(h2, P))
            dX1 = jnp.concatenate(dX1, axis=1)
            dX = dX1 + dX2
            pair = (dYb.astype(F32) - dY) * jnp.concatenate(yd, axis=1) - xdtb.astype(F32) * dX1
            through = _segment_sums(xdt * dX2, sel)
            u = ds1 * s0
            u_hi = u.astype(BF16)
            u_rows = _dot(ones, jnp.concatenate([u_hi, (u - u_hi.astype(F32)).astype(BF16)], axis=0), NT)
            u_rows = u_rows[:, :GW] + u_rows[:, GW:]
            at_end = jnp.exp(acn[L - 1:L, :]) * _segment_sums(u_rows, sel)[0:1, :] + \
                jnp.sum(through, axis=0, keepdims=True)
            dac_ref[0, rows, :] = _segment_sums(dY * y_ref[rows, :] + pair, sel) - through + \
                jnp.where(is_last, at_end, 0.0)
            dxsum_ref[0, rows, :] = _segment_sums(dX * xs, sel)
            dxs_ref[rows, :] = dX * dts + dk_ref[...] * dY
            dst[...] = _head_rows(jnp.exp(acs[L - 1:L, :]), HPG, P) * ds1 + _dot(dR, C, TN)
            dgb = dgsum.astype(BF16)
            dc_ref[rows, :] = _dot(dR, s0b, NN) + _dot(dgb, B, NN)
            db_ref[rows, :] = _dot((xdt * ws).astype(BF16), ds1b, NN) + _dot(dgb, C, TN)

    R = CPS * L
    wide = pl.BlockSpec((R, GW), lambda g, c: (rc(c), g))
    colspec = pl.BlockSpec((1, R, LANES), lambda g, c: (g, rc(c), 0))
    whole = pl.BlockSpec(memory_space=pl.ANY)
    arrs, gathers = side
    grid = (SG, nc // CPS)
    res = pl.pallas_call(
        with_exchange(body, 12, 5, gathers, grid), name=name, grid=grid,
        in_specs=[wide,
                  pl.BlockSpec((R, SN), lambda g, c: (rc(c), bcol + g)),
                  pl.BlockSpec((R, SN), lambda g, c: (rc(c), ccol + g)),
                  pl.BlockSpec((R, LANES), lambda g, c: (rc(c), 0)),
                  pl.BlockSpec((R, LANES), lambda g, c: (rc(c), 0)),
                  pl.BlockSpec((HPG, R), lambda g, c: (g, rc(c))),
                  pl.BlockSpec((1, LANES, GW), lambda g, c: (g, 0, 0)),
                  pl.BlockSpec((1, LANES, HPG * LANES), lambda g, c: (g, 0, 0)),
                  pl.BlockSpec((CPS, HPG, P, SN), lambda g, c: (rc(c), g, 0, 0)),
                  wide, wide,
                  pl.BlockSpec((1, GW), lambda g, c: (0, g))] + [whole] * len(arrs),
        out_specs=[wide,
                   pl.BlockSpec((R, SN), lambda g, c: (rc(c), g)),
                   pl.BlockSpec((R, SN), lambda g, c: (rc(c), g)),
                   colspec, colspec] + [whole] * len(arrs),
        out_shape=[jax.ShapeDtypeStruct((S, SI), F32), jax.ShapeDtypeStruct((S, cfg.GN), F32),
                   jax.ShapeDtypeStruct((S, cfg.GN), F32),
                   jax.ShapeDtypeStruct((SG, S, LANES), F32), jax.ShapeDtypeStruct((SG, S, LANES), F32)] +
        _exchange_shapes(arrs, gathers),
        scratch_shapes=[pltpu.VMEM((GW, SN), F32)] + _exchange_sems(len(arrs)),
        compiler_params=_cp(("arbitrary", "arbitrary")),
    )(xc, xc, xc, dt, acum, act, sel_p, sel_l, states, y, dy, dvec, *arrs)
    return res[:5], res[5:]


def dt_bwd(dac, dxsum, dt_raw, dt, dt_bias, a_log, cfg, name):
    S, L, SG = cfg.S, cfg.L, cfg.SG

    def body(da_ref, dx_ref, x_ref, dt_ref, b_ref, al_ref, o_ref, gb_ref, ga_ref):
        a = -jnp.exp(al_ref[...])
        dtv = dt_ref[...]
        dxs = jnp.sum(dx_ref[...], axis=0)
        upper = jnp.where(lax.broadcasted_iota(jnp.int32, (L, L), 1) >= lax.broadcasted_iota(jnp.int32, (L, L), 0),
                          1.0, 0.0).astype(F32)
        dda = lax.dot_general(upper, jnp.sum(da_ref[...], axis=0), (NN, ((), ())), precision=lax.Precision.HIGHEST,
                              preferred_element_type=F32)
        draw = (dxs + dda * a) * _sigmoid(x_ref[...] + b_ref[...])
        o_ref[...] = draw.astype(BF16)
        gb = jnp.sum(draw, axis=0, keepdims=True)
        ga = jnp.sum(dda * dtv, axis=0, keepdims=True) * a

        @pl.when(pl.program_id(0) == 0)
        def _():
            gb_ref[...] = gb
            ga_ref[...] = ga

        @pl.when(pl.program_id(0) > 0)
        def _():
            gb_ref[...] += gb
            ga_ref[...] += ga

    row = pl.BlockSpec((L, LANES), lambda i: (i, 0))
    vec = pl.BlockSpec((1, LANES), lambda i: (0, 0))
    return pl.pallas_call(
        body, name=name, grid=(S // L,),
        in_specs=[pl.BlockSpec((SG, L, LANES), lambda i: (0, i, 0))] * 2 + [row, row, vec, vec],
        out_specs=[row, vec, vec],
        out_shape=[jax.ShapeDtypeStruct((S, LANES), BF16), jax.ShapeDtypeStruct((1, LANES), F32),
                   jax.ShapeDtypeStruct((1, LANES), F32)],
        compiler_params=_cp(("arbitrary",)),
    )(dac, dxsum, dt_raw, dt, dt_bias, a_log)


def gated_norm_fwd(y, xc, proj, dvec, nw, cfg, name, tm=128):
    S, SI = cfg.S, cfg.SI

    def body(y_ref, xs_ref, z_ref, d_ref, w_ref, o_ref):
        z = z_ref[...]
        yg = (y_ref[...] + d_ref[...] * xs_ref[...]) * (z * _sigmoid(z))
        r = lax.rsqrt(jnp.mean(yg * yg, axis=-1, keepdims=True) + RMS_EPS)
        o_ref[...] = ((yg * r) * w_ref[...]).astype(BF16)

    row = pl.BlockSpec((tm, SI), lambda i: (i, 0))
    vec = pl.BlockSpec((1, SI), lambda i: (0, 0))
    return pl.pallas_call(
        body, name=name, grid=(S // tm,),
        in_specs=[row, row, pl.BlockSpec((tm, SI), lambda i: (i, _blk(cfg.zs0, SI))), vec, vec],
        out_specs=row, out_shape=jax.ShapeDtypeStruct((S, SI), BF16),
        compiler_params=_cp(("parallel",)),
    )(y, xc, proj, dvec, nw)


def gated_norm_bwd(dyn, y, xc, proj, dvec, nw, cfg, name, tm=128):
    S, SI = cfg.S, cfg.SI

    def body(dn_ref, y_ref, xs_ref, z_ref, d_ref, w_ref, dy_ref, dz_ref, gw_ref, gd_ref):
        z = z_ref[...]
        s = _sigmoid(z)
        sz = z * s
        xs = xs_ref[...]
        yf = y_ref[...] + d_ref[...] * xs
        yg = yf * sz
        r = lax.rsqrt(jnp.mean(yg * yg, axis=-1, keepdims=True) + RMS_EPS)
        dn = dn_ref[...].astype(F32)
        g = dn * w_ref[...]
        dyg = r * g - yg * (r * r * r) * jnp.mean(g * yg, axis=-1, keepdims=True)
        dy = dyg * sz
        dy_ref[...] = dy.astype(BF16)
        dz_ref[...] = (dyg * yf * (s * (1.0 + z * (1.0 - s)))).astype(BF16)
        gw = jnp.sum(dn * (yg * r), axis=0, keepdims=True)
        gd = jnp.sum(dy * xs, axis=0, keepdims=True)

        @pl.when(pl.program_id(0) == 0)
        def _():
            gw_ref[...] = gw
            gd_ref[...] = gd

        @pl.when(pl.program_id(0) > 0)
        def _():
            gw_ref[...] += gw
            gd_ref[...] += gd

    row = pl.BlockSpec((tm, SI), lambda i: (i, 0))
    vec = pl.BlockSpec((1, SI), lambda i: (0, 0))
    return pl.pallas_call(
        body, name=name, grid=(S // tm,),
        in_specs=[row, row, row, pl.BlockSpec((tm, SI), lambda i: (i, _blk(cfg.zs0, SI))), vec, vec],
        out_specs=[row, row, vec, vec],
        out_shape=[jax.ShapeDtypeStruct((S, SI), BF16), jax.ShapeDtypeStruct((S, SI), BF16),
                   jax.ShapeDtypeStruct((1, SI), F32), jax.ShapeDtypeStruct((1, SI), F32)],
        compiler_params=_cp(("arbitrary",)),
    )(dyn, y, xc, proj, dvec, nw)


def _shard_columns(cfg, main, dt):
    dt0 = 4 * cfg.AW + cfg.SI + cfg.CD
    ws = cfg.N_IN // N_DEV
    out = []
    for k in range(N_DEV):
        lo, hi, parts = k * ws, (k + 1) * ws, []
        if lo < dt0:
            parts.append((main, lo, min(hi, dt0)))
        if lo < dt0 + cfg.SH and hi > dt0:
            parts.append((dt, max(lo, dt0) - dt0, min(hi, dt0 + cfg.SH) - dt0))
        if hi > dt0 + cfg.SH:
            parts.append((main, max(lo, dt0 + cfg.SH) - cfg.SH, hi - cfg.SH))
        out.append(parts)
    return out


def local_step(cfg, x, tgt, norm_w, conv_w, conv_b, dt_bias, a_log, d_skip, ssm_norm_w, final_norm_w,
               w_main, w_dt, shards, dt0):
    S, D = cfg.S, cfg.D
    slopes = _slope_table(cfg)
    dt_bias_p = _pad_lanes(dt_bias)
    a_log_p = _pad_lanes(a_log)
    dvec = _spread(d_skip, cfg.P)

    hn = rmsnorm_fwd(x, norm_w, "rmsnorm_fwd")
    proj, gathered = matmul(hn, w_main, 'nn', 1024, 2048, 2048, F32, "in_proj", side=(shards, [True] * 3))
    w_attn, w_ssm, w_out = gathered[0].reshape(cfg.AW, D), gathered[1].reshape(cfg.SI, D), gathered[2].reshape(D, D)
    dt_raw = matmul(hn, w_dt, 'nn', 512, 128, 2048, F32, "in_proj_dt")
    o_a, o_mix, ltot = attn_fused_fwd(proj, slopes, cfg, "attn_fwd")
    xc = conv_fwd(proj, conv_w, conv_b, cfg, "conv_fwd")
    dt, acum = ssd_prep(dt_raw, dt_bias_p, a_log_p, cfg, "ssd_prep")
    act = acum[:, :cfg.SH].T
    sel_p, sel_l = _head_selectors(cfg)
    y, states = ssd_scan_fwd(xc, dt, acum, act, sel_p, sel_l, cfg, "ssd_scan_fwd")
    y_n = gated_norm_fwd(y, xc, proj, dvec, ssm_norm_w, cfg, "gated_norm_fwd")
    a_out, s_out, merged = branch_merge(o_a, w_attn, y_n, w_ssm, proj, cfg, "branch_merge")
    dout, loss_p, g_final_w = out_proj_final(merged, w_out, x, final_norm_w.reshape(1, D), tgt, "out_proj_final")

    g_w_out = matmul(merged, dout, 'tn', 1024, 1024, 2048, BF16, "g_w_out")
    da_out, ds_out, dga, dgs = merge_bwd(dout, w_out, a_out, s_out, proj, cfg, "merge_bwd")
    g_w_attn = matmul(o_a, da_out, 'tn', 1024, 1024, 2048, BF16, "g_w_attn")
    g_w_ssm = matmul(y_n, ds_out, 'tn', 1024, 1024, 2048, BF16, "g_w_ssm")
    do_a = matmul(da_out, w_attn, 'nt', 512, 1024, 2048, BF16, "d_o_a")
    dyn = matmul(ds_out, w_ssm, 'nt', 512, 1024, 2048, BF16, "d_y_n")
    dy, dz_s, g_ssm_norm, g_dvec = gated_norm_bwd(dyn, y, xc, proj, dvec, ssm_norm_w, cfg, "gated_norm_bwd")
    sends = [g.reshape((N_DEV, g.shape[0] // N_DEV, D)) for g in (g_w_attn, g_w_ssm, g_w_out)]
    (dxs, dB, dC, dac_g, dxsum_g), (r_attn, r_ssm, r_out) = ssd_scan_bwd(
        xc, dt, acum, act, sel_p, sel_l, states, y, dy, dvec, cfg, "ssd_scan_bwd", side=(sends, [False] * 3))
    ddt_raw, g_dt_bias, g_a_log = dt_bwd(dac_g, dxsum_g, dt_raw, dt, dt_bias_p, a_log_p, cfg, "dt_bwd")
    dxbc, g_cw, g_cb = [], [], []
    for nm, piece, c_off in (("xs", dxs, 0), ("b", dB, cfg.SI), ("c", dC, cfg.SI + cfg.GN)):
        dx, gw, gb = conv_bwd(proj, piece, conv_w, conv_b, cfg, "conv_bwd_" + nm, c_off)
        dxbc.append(dx)
        g_cw.append(gw)
        g_cb.append(gb)
    g_conv_w, g_conv_b = jnp.concatenate(g_cw, axis=1), jnp.concatenate(g_cb, axis=1)
    dq, dk, dv, dz_a = attn_fused_bwd(proj, do_a, o_mix, ltot, slopes, cfg, "attn_bwd")
    dproj = jnp.concatenate([dq, dk, dv, dz_a, dz_s] + dxbc + [dga, dgs], axis=1)
    g_w_main = matmul(hn, dproj, 'tn', 1024, 2048, 2048, BF16, "g_w_main")
    g_w_dt = matmul(hn, ddt_raw, 'tn', 1024, 128, 2048, BF16, "g_w_dt")
    send_in = jnp.stack([jnp.concatenate([g[:, lo:hi] for g, lo, hi in parts], axis=1)
                         for parts in _shard_columns(cfg, g_w_main, g_w_dt)])
    dhn_a, (r_in,) = matmul(dproj, w_main, 'nt', 1024, 1024, 2048, F32, "d_hn", side=([send_in], [False]))
    dhn_b = matmul(ddt_raw, w_dt, 'nt', 512, 1024, 128, F32, "d_hn_dt")
    grad_x, g_norm_w = rmsnorm_bwd(dhn_a, dhn_b, x, norm_w, dout, "rmsnorm_bwd")

    g_d_skip = jnp.sum(g_dvec.reshape(cfg.SH, cfg.P), axis=1).reshape(1, cfg.SH)
    small = dict(norm_w=g_norm_w, conv_b=g_conv_b, dt_bias=g_dt_bias[:, :cfg.SH], a_log=g_a_log[:, :cfg.SH],
                 d_skip=g_d_skip, ssm_norm_w=g_ssm_norm, final_norm_w=g_final_w, conv_w=g_conv_w[:cfg.KC])
    return loss_p, grad_x, small, dict(w_in=r_in, w_attn=r_attn, w_ssm=r_ssm, w_out=r_out)


def _mesh_pos():
    return lax.axis_index("x"), lax.axis_index("y"), lax.axis_index("c")


def _flat(pos):
    return 4 * pos[0] + 2 * pos[1] + pos[2]


def _exchange_shapes(arrs, gathers):
    return [jax.ShapeDtypeStruct(((N_DEV,) + a.shape) if g else a.shape, a.dtype) for a, g in zip(arrs, gathers)]


def _exchange_sems(n):
    return [pltpu.SemaphoreType.DMA((n * (N_DEV - 1),)), pltpu.SemaphoreType.DMA((n * (N_DEV - 1),)),
            pltpu.SemaphoreType.DMA((n,))]


def _exchange_copies(ins, outs, gathers, send_sems, recv_sems, loc_sems):
    pos = _mesh_pos()
    me = _flat(pos)
    starts, waits = [], []
    for a in range(len(ins)):
        mine = ins[a] if gathers[a] else ins[a].at[me]
        loc = pltpu.make_async_copy(mine, outs[a].at[me], loc_sems.at[a])
        starts.append(loc)
        waits.append(loc)
        for k in range(1, N_DEV):
            flip = ((k >> 2) & 1, (k >> 1) & 1, k & 1)
            peer = tuple(1 - p if f else p for p, f in zip(pos, flip))
            pk = _flat(peer)
            src = ins[a] if gathers[a] else ins[a].at[pk]
            sems = dict(send_sem=send_sems.at[a * (N_DEV - 1) + k - 1], recv_sem=recv_sems.at[a * (N_DEV - 1) + k - 1],
                        device_id=peer, device_id_type=pl.DeviceIdType.MESH)
            starts.append(pltpu.make_async_remote_copy(src_ref=src, dst_ref=outs[a].at[me], **sems))
            waits.append(pltpu.make_async_remote_copy(src_ref=src, dst_ref=outs[a].at[pk], **sems))
    return starts, waits


def exchange(arrs, gathers, name):
    n = len(arrs)

    def body(*refs):
        starts, waits = _exchange_copies(refs[:n], refs[n:2 * n], gathers, *refs[2 * n:])
        for cp in starts:
            cp.start()
        for cp in waits:
            cp.wait()

    hbm = pl.BlockSpec(memory_space=pltpu.HBM)
    return pl.pallas_call(
        body, name=name, in_specs=[hbm] * n, out_specs=[hbm] * n, out_shape=_exchange_shapes(arrs, gathers),
        scratch_shapes=_exchange_sems(n),
    )(*arrs)


def gather_two_level(arrs, chunks, name):
    n = len(arrs)

    def body(*refs):
        ins, outs = refs[:n], refs[n:2 * n]
        send_sems, recv_sems, loc_sems = refs[2 * n:]
        x, y, c = _mesh_pos()
        me, sib = (x, y, c), (x, y, 1 - c)
        chips = [(1 - x, y), (x, 1 - y), (1 - x, 1 - y)]
        plan, base = [], 0
        for a in range(n):
            step = arrs[a].shape[0] // chunks[a]
            for q in range(chunks[a]):
                plan.append((a, pl.ds(q * step, step), base))
                base += N_DEV - 1

        def copy(a, rows, sem, block, to, own=False):
            dst = outs[a].at[_flat(block), rows]
            return pltpu.make_async_remote_copy(
                src_ref=ins[a].at[rows] if own else dst, dst_ref=dst, send_sem=send_sems.at[sem],
                recv_sem=recv_sems.at[sem], device_id=to, device_id_type=pl.DeviceIdType.MESH)

        local = [pltpu.make_async_copy(ins[a], outs[a].at[_flat(me)], loc_sems.at[a]) for a in range(n)]
        for cp in local:
            cp.start()
        sent = []
        for a, rows, s in plan:
            sent.append(copy(a, rows, s, me, sib, own=True))
            sent += [copy(a, rows, s + 1 + j, me, (*chip, c), own=True) for j, chip in enumerate(chips)]
        for cp in sent:
            cp.start()
        for a, rows, s in plan:
            for j, chip in enumerate(chips):
                copy(a, rows, s + 1 + j, (*chip, c), me).wait_recv()
                passed = copy(a, rows, s + 4 + j, (*chip, c), sib)
                passed.start()
                sent.append(passed)
        for a, rows, s in plan:
            copy(a, rows, s, sib, me).wait_recv()
            for j, chip in enumerate(chips):
                copy(a, rows, s + 4 + j, (*chip, 1 - c), me).wait_recv()
        for cp in sent:
            cp.wait_send()
        for cp in local:
            cp.wait()

    hbm = pl.BlockSpec(memory_space=pltpu.HBM)
    nsem = (N_DEV - 1) * sum(chunks)
    return pl.pallas_call(
        body, name=name, in_specs=[hbm] * n, out_specs=[hbm] * n, out_shape=_exchange_shapes(arrs, [True] * n),
        scratch_shapes=[pltpu.SemaphoreType.DMA((nsem,)), pltpu.SemaphoreType.DMA((nsem,)),
                        pltpu.SemaphoreType.DMA((n,))],
    )(*arrs)


def with_exchange(body, n_in, n_out, gathers, grid):
    n = len(gathers)

    def wrapped(*refs):
        ins, sends = refs[:n_in], refs[n_in:n_in + n]
        outs, recvs = refs[n_in + n:n_in + n + n_out], refs[n_in + 2 * n + n_out - n:n_in + 2 * n + n_out]
        scratch, sems = refs[n_in + 2 * n + n_out:-3], refs[-3:]
        ids = [pl.program_id(d) for d in range(len(grid))]
        first = functools.reduce(jnp.logical_and, [i == 0 for i in ids])
        last = functools.reduce(jnp.logical_and, [i == g - 1 for i, g in zip(ids, grid)])

        @pl.when(first)
        def _():
            for cp in _exchange_copies(sends, recvs, gathers, *sems)[0]:
                cp.start()

        body(*ins, *outs, *scratch)

        @pl.when(last)
        def _():
            for cp in _exchange_copies(sends, recvs, gathers, *sems)[1]:
                cp.wait()

    return wrapped


def adamw(g_src, w, m, v, summed, name, tr=64):
    R, C = w.shape
    tr = min(tr, R)
    assert R % tr == 0

    def body(g_ref, w_ref, m_ref, v_ref, g_out, d_out, m_out, v_out):
        if summed:
            g = g_ref[0].astype(F32)
            for j in range(1, N_DEV):
                g = g + g_ref[j].astype(F32)
        else:
            g = g_ref[...]
        mn = ADAM_B1 * m_ref[...] + (1.0 - ADAM_B1) * g
        vn = ADAM_B2 * v_ref[...] + (1.0 - ADAM_B2) * (g * g)
        m_hat = mn / (1.0 - ADAM_B1 ** ADAM_STEP)
        v_hat = vn / (1.0 - ADAM_B2 ** ADAM_STEP)
        g_out[...] = g
        d_out[...] = -ADAM_LR * (m_hat / (jnp.sqrt(v_hat) + ADAM_EPS) + ADAM_WD * w_ref[...])
        m_out[...] = mn
        v_out[...] = vn

    row = pl.BlockSpec((tr, C), lambda i: (i, 0))
    gspec = pl.BlockSpec((N_DEV, tr, C), lambda i: (0, i, 0)) if summed else row
    sh = jax.ShapeDtypeStruct((R, C), F32)
    return pl.pallas_call(
        body, name=name, grid=(R // tr,), in_specs=[gspec, row, row, row], out_specs=[row] * 4, out_shape=[sh] * 4,
        compiler_params=_cp(("parallel",)),
    )(g_src, w, m, v)


SMALL = ('norm_w', 'conv_b', 'dt_bias', 'a_log', 'd_skip', 'ssm_norm_w', 'final_norm_w')


def _rows(n):
    return -(-n // (8 * LANES)) * 8


def _pack(vals):
    parts = []
    for a in vals:
        f = a.reshape(-1)
        parts.append(jnp.pad(f, (0, _rows(f.size) * LANES - f.size)).reshape(-1, LANES))
    return jnp.concatenate(parts, axis=0)


def _unpack(packed, shapes):
    out, r = [], 0
    for s in shapes:
        n = math.prod(s)
        out.append(packed[r:r + _rows(n)].reshape(-1)[:n].reshape(s))
        r += _rows(n)
    return out


def kernel(x, norm_w, w_in, conv_w, conv_b, dt_bias, a_log, d_skip, ssm_norm_w, w_attn_branch, w_ssm_branch, w_out, final_norm_w, loss_target, m_norm_w, m_w_in, m_conv_w, m_conv_b, m_dt_bias, m_a_log, m_d_skip, m_ssm_norm_w, m_w_attn_branch, m_w_ssm_branch, m_w_out, m_final_norm_w, v_norm_w, v_w_in, v_conv_w, v_conv_b, v_dt_bias, v_a_log, v_d_skip, v_ssm_norm_w, v_w_attn_branch, v_w_ssm_branch, v_w_out, v_final_norm_w):
    cfg = CFG
    D, SH = cfg.D, cfg.SH
    me = _flat(_mesh_pos())
    dt0 = 4 * cfg.AW + cfg.SI + cfg.CD
    ws = w_in.shape[-1]

    g_in, g_cw = gather_two_level([w_in[0].astype(BF16), conv_w[0]], [4, 1], "gather_w_in")
    main_cols, dt_cols = [], []
    for k, parts in enumerate(_shard_columns(cfg, "main", "dt")):
        at = 0
        for which, lo, hi in parts:
            (main_cols if which == "main" else dt_cols).append(g_in[k][:, at:at + hi - lo])
            at += hi - lo
    w_main = jnp.concatenate(main_cols, axis=1)
    w_dt = _pad_lanes(jnp.concatenate(dt_cols, axis=1))
    conv_full = g_cw.transpose(1, 0, 2).reshape(cfg.KC, cfg.CD)
    shards = [w_attn_branch[0].astype(BF16), w_ssm_branch[0].astype(BF16), w_out[0].astype(BF16)]

    loss_p, grad_x, small, recv = local_step(
        cfg, x[0], loss_target[0], norm_w, conv_full, conv_b, dt_bias, a_log, d_skip,
        ssm_norm_w, final_norm_w, w_main, w_dt, shards, dt0)

    upd = {}
    upd['w_in'] = adamw(recv['w_in'], w_in[0], m_w_in[0], v_w_in[0], True, "adamw_w_in")
    upd['w_attn_branch'] = adamw(recv['w_attn'], w_attn_branch[0], m_w_attn_branch[0], v_w_attn_branch[0], True,
                                 "adamw_w_attn")
    upd['w_ssm_branch'] = adamw(recv['w_ssm'], w_ssm_branch[0], m_w_ssm_branch[0], v_w_ssm_branch[0], True,
                                "adamw_w_ssm")
    upd['w_out'] = adamw(recv['w_out'], w_out[0], m_w_out[0], v_w_out[0], True, "adamw_w_out")

    extra = [jnp.zeros((cfg.KC, cfg.CD), F32), jnp.zeros((1, 1), F32)]
    shapes = [small[n].shape for n in SMALL] + [e.shape for e in extra]
    part = _pack([small[n] for n in SMALL] + [small['conv_w'], loss_p[:, :1]])
    gathered, = exchange([part], [True], "gather_small")
    given = dict(norm_w=(norm_w, m_norm_w, v_norm_w), conv_b=(conv_b, m_conv_b, v_conv_b),
                 dt_bias=(dt_bias, m_dt_bias, v_dt_bias), a_log=(a_log, m_a_log, v_a_log),
                 d_skip=(d_skip, m_d_skip, v_d_skip), ssm_norm_w=(ssm_norm_w, m_ssm_norm_w, v_ssm_norm_w),
                 final_norm_w=(final_norm_w, m_final_norm_w, v_final_norm_w))
    packed = [_pack([given[n][t] for n in SMALL] + extra) for t in range(3)]
    outs = adamw(gathered, *packed, True, "adamw_small", tr=part.shape[0])
    unpacked = [_unpack(o, shapes) for o in outs]
    for i, n in enumerate(SMALL):
        upd[n] = [u[i].reshape(given[n][0].shape) for u in unpacked]
    loss = unpacked[0][-1].reshape(())
    cw = conv_w.shape[-1]
    g_cw_mine = lax.dynamic_slice_in_dim(unpacked[0][-2], me * cw, cw, axis=1)
    upd['conv_w'] = adamw(g_cw_mine.reshape(-1, LANES), conv_w.reshape(-1, LANES), m_conv_w.reshape(-1, LANES),
                          v_conv_w.reshape(-1, LANES), False, "adamw_conv_w")

    order = ['norm_w', 'w_in', 'conv_w', 'conv_b', 'dt_bias', 'a_log', 'd_skip', 'ssm_norm_w', 'w_attn_branch',
             'w_ssm_branch', 'w_out', 'final_norm_w']
    like = dict(norm_w=norm_w, w_in=w_in, conv_w=conv_w, conv_b=conv_b, dt_bias=dt_bias, a_log=a_log, d_skip=d_skip,
                ssm_norm_w=ssm_norm_w, w_attn_branch=w_attn_branch, w_ssm_branch=w_ssm_branch, w_out=w_out,
                final_norm_w=final_norm_w)
    result = [loss, grad_x[None]]
    for t in range(4):
        result += [upd[n][t].reshape(like[n].shape) for n in order]
    return tuple(result)
```

```python
import functools
import math
from typing import NamedTuple

import jax
import jax.numpy as jnp
from jax import lax
from jax.experimental import pallas as pl
from jax.experimental.pallas import tpu as pltpu

F32 = jnp.float32
BF16 = jnp.bfloat16
RMS_EPS = 1e-6
NEG = -1e30
N_DEV = 8
CPS = 8
LANES = 128
ATTN_BLOCK = 128
ADAM_LR, ADAM_B1, ADAM_B2, ADAM_EPS, ADAM_WD, ADAM_STEP = 0.001, 0.9, 0.999, 1e-08, 0.01, 10
VMEM_LIMIT = 56 * 1024 * 1024


class Cfg(NamedTuple):
    D: int = 2048
    S: int = 8192
    AH: int = 16
    E: int = 128
    patterns: tuple = ((128, 1), (512, 4), (2048, 16))
    SI: int = 4096
    P: int = 64
    SG: int = 8
    SN: int = 128
    KC: int = 4
    L: int = 128

    @property
    def AW(self): return self.AH * self.E
    @property
    def SH(self): return self.SI // self.P
    @property
    def HPG(self): return self.SH // self.SG
    @property
    def GN(self): return self.SG * self.SN
    @property
    def CD(self): return self.SI + 2 * self.GN
    @property
    def k0(self): return self.AW
    @property
    def v0(self): return 2 * self.AW
    @property
    def za0(self): return 3 * self.AW
    @property
    def zs0(self): return 4 * self.AW
    @property
    def xbc0(self): return 4 * self.AW + self.SI
    @property
    def ga0(self): return self.xbc0 + self.CD
    @property
    def gs0(self): return self.ga0 + self.D
    @property
    def NP(self): return self.gs0 + self.D
    @property
    def N_IN(self): return self.NP + self.SH


CFG = Cfg()


def _cp(sem=None, vmem=VMEM_LIMIT):
    return pltpu.CompilerParams(dimension_semantics=sem, vmem_limit_bytes=vmem)


def _sigmoid(z):
    return 1.0 / (1.0 + jnp.exp(-z))


def _dot(a, b, dims):
    return lax.dot_general(a, b, (dims, ((), ())), preferred_element_type=F32)


NN = ((1,), (0,))
NT = ((1,), (1,))
TN = ((0,), (0,))


def _blk(off, width):
    assert off % width == 0, (off, width)
    return off // width


def matmul(a, b, mode, tm, tn, tk, out_dtype, name, side=None):
    if mode == 'nn':
        (M, K), (_, N) = a.shape, b.shape
    elif mode == 'nt':
        (M, K), (N, _) = a.shape, b.shape
    else:
        (K, M), (_, N) = a.shape, b.shape
    tm, tn, tk = min(tm, M), min(tn, N), min(tk, K)
    assert M % tm == 0 and N % tn == 0 and K % tk == 0, (M, N, K, tm, tn, tk)
    nk = K // tk
    dims = {'nn': NN, 'nt': NT, 'tn': TN}[mode]

    def body(a_ref, b_ref, o_ref, *acc):
        part = _dot(a_ref[...].astype(BF16), b_ref[...].astype(BF16), dims)
        if nk == 1:
            o_ref[...] = part.astype(out_dtype)
        else:
            acc_ref, = acc
            k = pl.program_id(2)

            @pl.when(k == 0)
            def _():
                acc_ref[...] = part

            @pl.when(k > 0)
            def _():
                acc_ref[...] += part

            @pl.when(k == nk - 1)
            def _():
                o_ref[...] = acc_ref[...].astype(out_dtype)

    if mode == 'tn':
        a_spec = pl.BlockSpec((tk, tm), lambda n, m, k: (k, m))
    else:
        a_spec = pl.BlockSpec((tm, tk), lambda n, m, k: (m, k))
    if mode == 'nt':
        b_spec = pl.BlockSpec((tn, tk), lambda n, m, k: (n, k))
    else:
        b_spec = pl.BlockSpec((tk, tn), lambda n, m, k: (k, n))
    grid = (N // tn, M // tm, nk)
    o_spec = pl.BlockSpec((tm, tn), lambda n, m, k: (m, n))
    o_shape = jax.ShapeDtypeStruct((M, N), out_dtype)
    acc = [] if nk == 1 else [pltpu.VMEM((tm, tn), F32)]
    if side is None:
        return pl.pallas_call(
            body, name=name, grid=grid, in_specs=[a_spec, b_spec], out_specs=o_spec, out_shape=o_shape,
            scratch_shapes=acc, compiler_params=_cp(("parallel", "parallel", "arbitrary")),
        )(a, b)
    arrs, gathers = side
    whole = pl.BlockSpec(memory_space=pl.ANY)
    res = pl.pallas_call(
        with_exchange(body, 2, 1, gathers, grid), name=name, grid=grid,
        in_specs=[a_spec, b_spec] + [whole] * len(arrs), out_specs=[o_spec] + [whole] * len(arrs),
        out_shape=[o_shape] + _exchange_shapes(arrs, gathers),
        scratch_shapes=acc + _exchange_sems(len(arrs)),
        compiler_params=_cp(("arbitrary", "arbitrary", "arbitrary")),
    )(a, b, *arrs)
    return res[0], res[1:]


def rmsnorm_fwd(x, w, name, tm=256):
    S, D = x.shape

    def body(x_ref, w_ref, o_ref):
        xv = x_ref[...]
        r = lax.rsqrt(jnp.mean(xv * xv, axis=-1, keepdims=True) + RMS_EPS)
        o_ref[...] = ((xv * r) * w_ref[...]).astype(BF16)

    return pl.pallas_call(
        body, name=name, grid=(S // tm,),
        in_specs=[pl.BlockSpec((tm, D), lambda i: (i, 0)), pl.BlockSpec((1, D), lambda i: (0, 0))],
        out_specs=pl.BlockSpec((tm, D), lambda i: (i, 0)),
        out_shape=jax.ShapeDtypeStruct((S, D), BF16),
        compiler_params=_cp(("parallel",)),
    )(x, w)


def rmsnorm_bwd(dh_a, dh_b, x, w, dout, name, tm=128):
    S, D = x.shape

    def body(da_ref, db_ref, x_ref, w_ref, do_ref, gx_ref, gw_ref):
        xv = x_ref[...]
        dh = da_ref[...] + db_ref[...]
        r = lax.rsqrt(jnp.mean(xv * xv, axis=-1, keepdims=True) + RMS_EPS)
        g = dh * w_ref[...]
        dx = r * g - xv * (r * r * r) * jnp.mean(g * xv, axis=-1, keepdims=True)
        gx_ref[...] = do_ref[...] + dx
        gw = jnp.sum(dh * (xv * r), axis=0, keepdims=True)

        @pl.when(pl.program_id(0) == 0)
        def _():
            gw_ref[...] = gw

        @pl.when(pl.program_id(0) > 0)
        def _():
            gw_ref[...] += gw

    row = pl.BlockSpec((tm, D), lambda i: (i, 0))
    vec = pl.BlockSpec((1, D), lambda i: (0, 0))
    return pl.pallas_call(
        body, name=name, grid=(S // tm,),
        in_specs=[row, row, row, vec, row],
        out_specs=[row, vec],
        out_shape=[jax.ShapeDtypeStruct((S, D), F32), jax.ShapeDtypeStruct((1, D), F32)],
        compiler_params=_cp(("arbitrary",)),
    )(dh_a, dh_b, x, w, dout)


def out_proj_final(merged, w_out, x, fw, tgt, name, tm=256):
    S, D = x.shape

    def body(m_ref, wo_ref, x_ref, w_ref, t_ref, do_ref, loss_ref, gw_ref):
        out = x_ref[...] + _dot(m_ref[...], wo_ref[...], NN)
        w = w_ref[...]
        r = lax.rsqrt(jnp.mean(out * out, axis=-1, keepdims=True) + RMS_EPS)
        yn = out * r
        err = yn * w - t_ref[...]
        lrow = 0.5 * jnp.mean(err * err, axis=-1, keepdims=True)
        lsum = jnp.zeros((1, LANES), F32) + jnp.sum(lrow, axis=0, keepdims=True)
        dfin = err * (1.0 / D)
        g = dfin * w
        do_ref[...] = r * g - out * (r * r * r) * jnp.mean(g * out, axis=-1, keepdims=True)
        gw = jnp.sum(dfin * yn, axis=0, keepdims=True)

        @pl.when(pl.program_id(0) == 0)
        def _():
            gw_ref[...] = gw
            loss_ref[...] = lsum

        @pl.when(pl.program_id(0) > 0)
        def _():
            gw_ref[...] += gw
            loss_ref[...] += lsum

    row = pl.BlockSpec((tm, D), lambda i: (i, 0))
    vec = pl.BlockSpec((1, D), lambda i: (0, 0))
    return pl.pallas_call(
        body, name=name, grid=(S // tm,),
        in_specs=[row, pl.BlockSpec((D, D), lambda i: (0, 0)), row, vec, row],
        out_specs=[row, pl.BlockSpec((1, LANES), lambda i: (0, 0)), vec],
        out_shape=[jax.ShapeDtypeStruct((S, D), F32), jax.ShapeDtypeStruct((1, LANES), F32),
                   jax.ShapeDtypeStruct((1, D), F32)],
        compiler_params=_cp(("arbitrary",)),
    )(merged, w_out, x, fw, tgt)


def branch_merge(o_a, w_attn, y_n, w_ssm, proj, cfg, name, tm=512, tn=512):
    S, D = cfg.S, cfg.D
    tm, tn = min(tm, S), min(tn, D)

    def body(oa_ref, wa_ref, yn_ref, ws_ref, ga_ref, gs_ref, a_ref, s_ref, m_ref):
        a = _dot(oa_ref[...], wa_ref[...], NN)
        sv = _dot(yn_ref[...], ws_ref[...], NN)
        a_ref[...] = a.astype(BF16)
        s_ref[...] = sv.astype(BF16)
        m_ref[...] = (_sigmoid(ga_ref[...]) * a + _sigmoid(gs_ref[...]) * sv).astype(BF16)

    tile = pl.BlockSpec((tm, tn), lambda n, m: (m, n))
    return pl.pallas_call(
        body, name=name, grid=(D // tn, S // tm),
        in_specs=[pl.BlockSpec((tm, cfg.AW), lambda n, m: (m, 0)), pl.BlockSpec((cfg.AW, tn), lambda n, m: (0, n)),
                  pl.BlockSpec((tm, cfg.SI), lambda n, m: (m, 0)), pl.BlockSpec((cfg.SI, tn), lambda n, m: (0, n)),
                  pl.BlockSpec((tm, tn), lambda n, m: (m, _blk(cfg.ga0, tn) + n)),
                  pl.BlockSpec((tm, tn), lambda n, m: (m, _blk(cfg.gs0, tn) + n))],
        out_specs=[tile, tile, tile],
        out_shape=[jax.ShapeDtypeStruct((S, D), BF16)] * 3,
        compiler_params=_cp(("parallel", "parallel")),
    )(o_a, w_attn, y_n, w_ssm, proj, proj)


def merge_bwd(dout, w_out, a_out, s_out, proj, cfg, name, tm=512, tn=1024):
    S, D = cfg.S, cfg.D
    tm, tn = min(tm, S), min(tn, D)

    def body(do_ref, wo_ref, a_ref, s_ref, ga_ref, gs_ref, da_ref, ds_ref, dga_ref, dgs_ref):
        dmv = _dot(do_ref[...].astype(BF16), wo_ref[...], NT)
        sa = _sigmoid(ga_ref[...])
        ss = _sigmoid(gs_ref[...])
        da_ref[...] = (dmv * sa).astype(BF16)
        ds_ref[...] = (dmv * ss).astype(BF16)
        dga_ref[...] = (dmv * a_ref[...] * (sa * (1.0 - sa))).astype(BF16)
        dgs_ref[...] = (dmv * s_ref[...] * (ss * (1.0 - ss))).astype(BF16)

    tile = pl.BlockSpec((tm, tn), lambda n, m: (m, n))
    sh = jax.ShapeDtypeStruct((S, D), BF16)
    return pl.pallas_call(
        body, name=name, grid=(D // tn, S // tm),
        in_specs=[pl.BlockSpec((tm, D), lambda n, m: (m, 0)), pl.BlockSpec((tn, D), lambda n, m: (n, 0)), tile, tile,
                  pl.BlockSpec((tm, tn), lambda n, m: (m, _blk(cfg.ga0, tn) + n)),
                  pl.BlockSpec((tm, tn), lambda n, m: (m, _blk(cfg.gs0, tn) + n))],
        out_specs=[tile] * 4, out_shape=[sh] * 4,
        compiler_params=_cp(("parallel", "parallel")),
    )(dout, w_out, a_out, s_out, proj, proj)


def _attn_rows(base, d):
    return pl.ds(base, ATTN_BLOCK) if d == 1 else pl.ds(base, ATTN_BLOCK, stride=d)


def _attn_units(cfg):
    dmax = max(d for _, d in cfg.patterns)
    units = []
    for p, (window, d) in enumerate(cfg.patterns):
        assert window // d == ATTN_BLOCK and dmax % d == 0
        nsub = dmax // d
        for b in range(nsub):
            for r in range(d):
                base = b * ATTN_BLOCK * d + r
                if b > 0:
                    units.append((p, d, base, (b - 1) * ATTN_BLOCK * d + r, False))
                else:
                    units.append((p, d, base, (nsub - 1) * ATTN_BLOCK * d + r, True))
    return units, ATTN_BLOCK * dmax


def _set_bias_tiles(bias_s, slope, cfg):
    qi = lax.broadcasted_iota(jnp.int32, (ATTN_BLOCK, ATTN_BLOCK), 0)
    ki = lax.broadcasted_iota(jnp.int32, (ATTN_BLOCK, ATTN_BLOCK), 1)
    for p, (_, d) in enumerate(cfg.patterns):
        bias_s[2 * p] = jnp.where(ki >= qi, (-slope) * ((ATTN_BLOCK + qi - ki) * d).astype(F32), NEG)
        bias_s[2 * p + 1] = jnp.where(ki <= qi, (-slope) * ((qi - ki) * d).astype(F32), NEG)


def _unit_scores(q, kcat, bias_s, p, prev_ok, scale):
    s = _dot(q, kcat, NT) * scale + jnp.concatenate([bias_s[2 * p], bias_s[2 * p + 1]], axis=1)
    if prev_ok is not None:
        cur_half = lax.broadcasted_iota(jnp.int32, s.shape, 1) >= ATTN_BLOCK
        s = jnp.where(jnp.logical_or(cur_half, prev_ok), s, NEG)
    return s


def _slope_table(cfg):
    slopes = jnp.asarray([2.0 ** (-8.0 * (h + 1) / cfg.AH) for h in range(cfg.AH)], F32)
    return jnp.broadcast_to(slopes.reshape(cfg.AH, 1, 1), (cfg.AH, 8, LANES))


def attn_fused_fwd(proj, slopes, cfg, name):
    S, E, AH = cfg.S, cfg.E, cfg.AH
    units, SB = _attn_units(cfg)
    assert S % SB == 0
    npat = len(cfg.patterns)
    scale = E ** -0.5

    def spec(off, prev):
        c0 = _blk(off, E)
        if prev:
            return pl.BlockSpec((SB, E), lambda h, i: (jnp.maximum(i - 1, 0), c0 + h))
        return pl.BlockSpec((SB, E), lambda h, i: (i, c0 + h))

    def body(q_ref, kp_ref, kc_ref, vp_ref, vc_ref, z_ref, sl_ref, oa_ref, om_ref, lt_ref, *scr):
        o_s, l_s, bias_s = scr[:npat], scr[npat:2 * npat], scr[2 * npat]
        i = pl.program_id(1)

        @pl.when(i == 0)
        def _():
            _set_bias_tiles(bias_s, sl_ref[0, 0:1, :], cfg)

        for p, d, base, pbase, from_prev in units:
            rows, prows = _attn_rows(base, d), _attn_rows(pbase, d)
            q = q_ref[rows, :].astype(BF16)
            kp = (kp_ref if from_prev else kc_ref)[prows, :].astype(BF16)
            vp = (vp_ref if from_prev else vc_ref)[prows, :].astype(BF16)
            kcat = jnp.concatenate([kp, kc_ref[rows, :].astype(BF16)], axis=0)
            vcat = jnp.concatenate([vp, vc_ref[rows, :].astype(BF16)], axis=0)
            s = _unit_scores(q, kcat, bias_s, p, (i > 0) if from_prev else None, scale)
            m = jnp.max(s, axis=1, keepdims=True)
            pr = jnp.exp(s - m)
            l = jnp.sum(pr, axis=1, keepdims=True)
            o_s[p][rows, :] = _dot(pr.astype(BF16), vcat, NN) * (1.0 / l)
            l_s[p][rows, :] = m + jnp.log(l)
        ls = [l_s[p][...] for p in range(npat)]
        m = functools.reduce(jnp.maximum, ls)
        lt = m + jnp.log(sum(jnp.exp(l_ - m) for l_ in ls))
        lt_ref[...] = lt
        mix = sum(jnp.exp(ls[p] - lt) * o_s[p][...] for p in range(npat))
        om_ref[...] = mix
        z = z_ref[...]
        oa_ref[...] = (mix * (z * _sigmoid(z))).astype(BF16)

    out = pl.BlockSpec((SB, E), lambda h, i: (i, h))
    return pl.pallas_call(
        body, name=name, grid=(AH, S // SB),
        in_specs=[spec(0, False), spec(cfg.k0, True), spec(cfg.k0, False), spec(cfg.v0, True), spec(cfg.v0, False),
                  spec(cfg.za0, False), pl.BlockSpec((1, 8, LANES), lambda h, i: (h, 0, 0))],
        out_specs=[out, out, pl.BlockSpec((SB, 1), lambda h, i: (h * (S // SB) + i, 0))],
        out_shape=[jax.ShapeDtypeStruct((S, cfg.AW), BF16), jax.ShapeDtypeStruct((S, cfg.AW), F32),
                   jax.ShapeDtypeStruct((AH * S, 1), F32)],
        scratch_shapes=[pltpu.VMEM((SB, E), F32)] * npat + [pltpu.VMEM((SB, 1), F32)] * npat +
        [pltpu.VMEM((2 * npat, ATTN_BLOCK, ATTN_BLOCK), F32)],
        compiler_params=_cp(("parallel", "arbitrary")),
    )(proj, proj, proj, proj, proj, proj, slopes)


def attn_fused_bwd(proj, do_a, o_mix, ltot, slopes, cfg, name):
    S, E, AH = cfg.S, cfg.E, cfg.AH
    units, SB = _attn_units(cfg)
    nsb = S // SB
    last = nsb - 1
    scale = E ** -0.5

    def spec(off, prev):
        c0 = _blk(off, E)
        if prev:
            return pl.BlockSpec((SB, E), lambda h, i: (jnp.maximum(i - 1, 0), c0 + h))
        return pl.BlockSpec((SB, E), lambda h, i: (jnp.minimum(i, last), c0 + h))

    cur = pl.BlockSpec((SB, E), lambda h, i: (jnp.minimum(i, last), h))
    prev = pl.BlockSpec((SB, E), lambda h, i: (jnp.maximum(i - 1, 0), h))

    def body(q_ref, kp_ref, kc_ref, vp_ref, vc_ref, z_ref, doa_ref, om_ref, lt_ref, sl_ref,
             dq_ref, dk_ref, dv_ref, dz_ref, dmix_s, dl_s, dq_s, dkp_s, dvp_s, dkc_s, dvc_s, bias_s):
        i = pl.program_id(1)

        @pl.when(i == 0)
        def _():
            dkc_s[...] = jnp.zeros_like(dkc_s)
            dvc_s[...] = jnp.zeros_like(dvc_s)
            _set_bias_tiles(bias_s, sl_ref[0, 0:1, :], cfg)

        @pl.when(i < nsb)
        def _():
            z = z_ref[...]
            s = _sigmoid(z)
            doa = doa_ref[...].astype(F32)
            om = om_ref[...]
            dmix = doa * (z * s)
            dmix_s[...] = dmix
            dz_ref[...] = (doa * om * (s * (1.0 + z * (1.0 - s)))).astype(BF16)
            dl_s[...] = jnp.sum(dmix * om, axis=1, keepdims=True)
            dkp_s[...] = dkc_s[...]
            dvp_s[...] = dvc_s[...]
            dkc_s[...] = jnp.zeros_like(dkc_s)
            dvc_s[...] = jnp.zeros_like(dvc_s)
            dq_s[...] = jnp.zeros_like(dq_s)
            for p, d, base, pbase, from_prev in units:
                rows, prows = _attn_rows(base, d), _attn_rows(pbase, d)
                q = q_ref[rows, :].astype(BF16)
                kc = kc_ref[rows, :].astype(BF16)
                kp = (kp_ref if from_prev else kc_ref)[prows, :].astype(BF16)
                vp = (vp_ref if from_prev else vc_ref)[prows, :].astype(BF16)
                do = dmix_s[rows, :].astype(BF16)
                lt = lt_ref[rows, :]
                dlt = dl_s[rows, :]
                kcat = jnp.concatenate([kp, kc], axis=0)
                vcat = jnp.concatenate([vp, vc_ref[rows, :].astype(BF16)], axis=0)
                pr = jnp.exp(_unit_scores(q, kcat, bias_s, p, (i > 0) if from_prev else None, scale) - lt)
                ds = (pr * (_dot(do, vcat, NT) - dlt) * scale).astype(BF16)
                dq_s[rows, :] += _dot(ds, kcat, NN)
                dkcat = _dot(ds, q, TN)
                dvcat = _dot(pr.astype(BF16), do, TN)
                dk_t, dv_t = (dkp_s, dvp_s) if from_prev else (dkc_s, dvc_s)
                dk_t[prows, :] += dkcat[:ATTN_BLOCK, :]
                dv_t[prows, :] += dvcat[:ATTN_BLOCK, :]
                dkc_s[rows, :] += dkcat[ATTN_BLOCK:, :]
                dvc_s[rows, :] += dvcat[ATTN_BLOCK:, :]
            dq_ref[...] = dq_s[...].astype(BF16)
            dk_ref[...] = dkp_s[...].astype(BF16)
            dv_ref[...] = dvp_s[...].astype(BF16)

        @pl.when(i == nsb)
        def _():
            dk_ref[...] = dkc_s[...].astype(BF16)
            dv_ref[...] = dvc_s[...].astype(BF16)

    sh = jax.ShapeDtypeStruct((S, cfg.AW), BF16)
    acc = pltpu.VMEM((SB, E), F32)
    return pl.pallas_call(
        body, name=name, grid=(AH, nsb + 1),
        in_specs=[spec(0, False), spec(cfg.k0, True), spec(cfg.k0, False), spec(cfg.v0, True), spec(cfg.v0, False),
                  spec(cfg.za0, False), cur, cur,
                  pl.BlockSpec((SB, 1), lambda h, i: (h * nsb + jnp.minimum(i, last), 0)),
                  pl.BlockSpec((1, 8, LANES), lambda h, i: (h, 0, 0))],
        out_specs=[cur, prev, prev, cur], out_shape=[sh] * 4,
        scratch_shapes=[acc, pltpu.VMEM((SB, 1), F32), acc, acc, acc, acc, acc,
                        pltpu.VMEM((2 * len(cfg.patterns), ATTN_BLOCK, ATTN_BLOCK), F32)],
        compiler_params=_cp(("parallel", "arbitrary")),
    )(proj, proj, proj, proj, proj, proj, do_a, o_mix, ltot, slopes)


HALO = 8


def _conv_taps(x_ref, h_ref, kc):
    x = x_ref[...]
    full = jnp.concatenate([jnp.where(pl.program_id(1) == 0, 0.0, h_ref[...]), x], axis=0)
    return [pltpu.roll(full, s, axis=0)[HALO:, :] for s in range(kc - 1, 0, -1)] + [x]


def _conv_pre(taps, w_ref, b_ref):
    pre = b_ref[...] + w_ref[0:1, :] * taps[0]
    for k in range(1, len(taps)):
        pre = pre + w_ref[k:k + 1, :] * taps[k]
    return pre


def conv_fwd(proj, w, b, cfg, name, tm=1024, tc=512):
    S, CD, KC = cfg.S, cfg.CD, cfg.KC
    tc = min(tc, CD)
    c0 = _blk(cfg.xbc0, tc)
    hb = tm // HALO

    def body(x_ref, h_ref, w_ref, b_ref, o_ref):
        pre = _conv_pre(_conv_taps(x_ref, h_ref, KC), w_ref, b_ref)
        o_ref[...] = pre * _sigmoid(pre)

    return pl.pallas_call(
        body, name=name, grid=(CD // tc, S // tm),
        in_specs=[pl.BlockSpec((tm, tc), lambda c, i: (i, c0 + c)),
                  pl.BlockSpec((HALO, tc), lambda c, i: (jnp.maximum(i * hb - 1, 0), c0 + c)),
                  pl.BlockSpec((KC, tc), lambda c, i: (0, c)),
                  pl.BlockSpec((1, tc), lambda c, i: (0, c))],
        out_specs=pl.BlockSpec((tm, tc), lambda c, i: (i, c)),
        out_shape=jax.ShapeDtypeStruct((S, CD), F32),
        compiler_params=_cp(("parallel", "arbitrary")),
    )(proj, proj, w, b)


def conv_bwd(proj, dxc, w, b, cfg, name, c_off, tm=1024, tc=512):
    S, KC = cfg.S, cfg.KC
    CD = dxc.shape[1]
    tc = min(tc, CD)
    c0 = _blk(cfg.xbc0 + c_off, tc)
    w0 = _blk(c_off, tc)
    hb = tm // HALO
    nrb = S // tm
    last_h = S // HALO - 1

    def body(x_ref, hp_ref, hn_ref, d_ref, dn_ref, w_ref, b_ref, o_ref, gw_ref, gb_ref):
        i = pl.program_id(1)
        x = x_ref[...]
        full = jnp.concatenate([jnp.where(i == 0, 0.0, hp_ref[...]), x, hn_ref[...]], axis=0)
        rows = tm + HALO
        taps = [pltpu.roll(full, s_, axis=0)[HALO:, :] for s_ in range(KC - 1, 0, -1)] + [full[HALO:, :]]
        pre = _conv_pre(taps, w_ref, b_ref)
        sg = _sigmoid(pre)
        d_ext = jnp.concatenate([d_ref[...], jnp.where(i == nrb - 1, 0.0, dn_ref[...])], axis=0)
        dpre = d_ext * (sg * (1.0 + pre * (1.0 - sg)))
        own = dpre[:tm, :]
        acc = w_ref[KC - 1:KC, :] * own
        for j in range(1, KC):
            acc = acc + w_ref[KC - 1 - j:KC - j, :] * pltpu.roll(dpre, rows - j, axis=0)[:tm, :]
        o_ref[...] = acc.astype(BF16)
        gb = jnp.sum(own, axis=0, keepdims=True)
        gws = [jnp.sum(own * taps[k][:tm, :], axis=0, keepdims=True) for k in range(KC)]
        gw = jnp.concatenate(gws + [jnp.zeros((8 - KC, tc), F32)], axis=0)

        @pl.when(i == 0)
        def _():
            gw_ref[...] = gw
            gb_ref[...] = gb

        @pl.when(i > 0)
        def _():
            gw_ref[...] += gw
            gb_ref[...] += gb

    return pl.pallas_call(
        body, name=name, grid=(CD // tc, nrb),
        in_specs=[pl.BlockSpec((tm, tc), lambda c, i: (i, c0 + c)),
                  pl.BlockSpec((HALO, tc), lambda c, i: (jnp.maximum(i * hb - 1, 0), c0 + c)),
                  pl.BlockSpec((HALO, tc), lambda c, i: (jnp.minimum((i + 1) * hb, last_h), c0 + c)),
                  pl.BlockSpec((tm, tc), lambda c, i: (i, c)),
                  pl.BlockSpec((HALO, tc), lambda c, i: (jnp.minimum((i + 1) * hb, last_h), c)),
                  pl.BlockSpec((KC, tc), lambda c, i: (0, w0 + c)),
                  pl.BlockSpec((1, tc), lambda c, i: (0, w0 + c))],
        out_specs=[pl.BlockSpec((tm, tc), lambda c, i: (i, c)),
                   pl.BlockSpec((8, tc), lambda c, i: (0, c)),
                   pl.BlockSpec((1, tc), lambda c, i: (0, c))],
        out_shape=[jax.ShapeDtypeStruct((S, CD), BF16), jax.ShapeDtypeStruct((8, CD), F32),
                   jax.ShapeDtypeStruct((1, CD), F32)],
        compiler_params=_cp(("parallel", "arbitrary")),
    )(proj, proj, proj, dxc, dxc, w, b)


def _pad_lanes(v, width=LANES):
    return jnp.pad(v, ((0, 0), (0, width - v.shape[1])))


def ssd_prep(dt_raw, dt_bias, a_log, cfg, name):
    S, L = cfg.S, cfg.L

    def body(x_ref, b_ref, al_ref, dt_ref, ac_ref):
        x = x_ref[...] + b_ref[...]
        dt = jnp.maximum(x, 0.0) + jnp.log(1.0 + jnp.exp(-jnp.abs(x)))
        da = dt * (-jnp.exp(al_ref[...]))
        li = lax.broadcasted_iota(jnp.int32, (L, L), 0)
        si = lax.broadcasted_iota(jnp.int32, (L, L), 1)
        tri = jnp.where(li >= si, 1.0, 0.0).astype(F32)
        dt_ref[...] = dt
        ac_ref[...] = lax.dot_general(tri, da, ((NN), ((), ())), precision=lax.Precision.HIGHEST,
                                      preferred_element_type=F32)

    row = pl.BlockSpec((L, LANES), lambda i: (i, 0))
    vec = pl.BlockSpec((1, LANES), lambda i: (0, 0))
    sh = jax.ShapeDtypeStruct((S, LANES), F32)
    return pl.pallas_call(
        body, name=name, grid=(S // L,), in_specs=[row, vec, vec], out_specs=[row, row], out_shape=[sh, sh],
        compiler_params=_cp(("parallel",)),
    )(dt_raw, dt_bias, a_log)


def _spread(v, n):
    return jnp.broadcast_to(v[:, :, None], v.shape + (n,)).reshape(v.shape[0], v.shape[1] * n)


def _head_selectors(cfg):
    def sel(width):
        head = jnp.arange(LANES)[None, :, None]
        slot = jnp.arange(cfg.SG)[:, None, None] * cfg.HPG + (jnp.arange(cfg.HPG * width) // width)[None, None, :]
        return (head == slot).astype(BF16)
    return sel(cfg.P), sel(LANES)


def _spread_heads(v, sel):
    n = v.shape[0]
    hi = v.astype(BF16)
    r1 = v - hi.astype(F32)
    mid = r1.astype(BF16)
    lo = (r1 - mid.astype(F32)).astype(BF16)
    out = _dot(jnp.concatenate([hi, mid, lo], axis=0), sel, NN)
    return out[:n] + out[n:2 * n] + out[2 * n:]


def _pair_lanes(wide, hpg, p):
    low = lax.broadcasted_iota(jnp.int32, (wide.shape[0], LANES), 1) < p
    return jnp.concatenate([jnp.where(low, wide[:, 2 * jp * LANES:(2 * jp + 1) * LANES],
                                      wide[:, (2 * jp + 1) * LANES:(2 * jp + 2) * LANES])
                            for jp in range(hpg // 2)], axis=1)


def _pair_select(halves, p):
    low = lax.broadcasted_iota(jnp.int32, halves[0].shape, 1) < p
    return jnp.where(low, halves[0], halves[1])


def _head_rows(row, hpg, p):
    return jnp.concatenate([jnp.broadcast_to(row[:, j * LANES:(j + 1) * LANES], (p, LANES)) for j in range(hpg)],
                           axis=0)


def _segment_sums(t, sel):
    r = t.shape[0]
    hi = t.astype(BF16)
    out = _dot(jnp.concatenate([hi, (t - hi.astype(F32)).astype(BF16)], axis=0), sel, NT)
    return out[:r] + out[r:]


def ssd_scan_fwd(xc, dt, acum, act, sel_p, sel_l, cfg, name):
    S, L, P, SN, HPG, SG, SI = cfg.S, cfg.L, cfg.P, cfg.SN, cfg.HPG, cfg.SG, cfg.SI
    nc = S // L
    GW = HPG * P
    bcol, ccol = _blk(SI, SN), _blk(SI + cfg.GN, SN)
    assert nc % CPS == 0

    def body(xs_ref, b_ref, c_ref, dtn_ref, acn_ref, at_ref, sp_ref, sl_ref, y_ref, st_ref, st):
        @pl.when(pl.program_id(1) == 0)
        def _():
            st[...] = jnp.zeros_like(st)

        causal = lax.broadcasted_iota(jnp.int32, (L, L), 0) >= lax.broadcasted_iota(jnp.int32, (L, L), 1)
        for ci in range(CPS):
            rows = slice(ci * L, (ci + 1) * L)
            acn = acn_ref[rows, :]
            dts = _spread_heads(dtn_ref[rows, :], sp_ref[0])
            acs = _spread_heads(acn, sl_ref[0])
            a_p = _pair_lanes(acs, HPG, P)
            s0 = st[...]
            st_ref[ci] = s0.reshape(HPG, P, SN)
            B = b_ref[rows, :].astype(BF16)
            C = c_ref[rows, :].astype(BF16)
            G = _dot(C, B, NT)
            xdt = xs_ref[rows, :] * dts
            xdtb = xdt.astype(BF16)
            ws = jnp.exp(a_p[L - 1:L, :] - a_p)
            yo = jnp.exp(a_p) * _dot(C, s0.astype(BF16), NT)
            yd = []
            for jp in range(HPG // 2):
                x_pair = xdtb[:, jp * LANES:(jp + 1) * LANES]
                halves = []
                for j in (2 * jp, 2 * jp + 1):
                    dm = jnp.where(causal, jnp.exp(acs[:, j * LANES:(j + 1) * LANES] - at_ref[j:j + 1, rows]), 0.0)
                    halves.append(_dot((G * dm).astype(BF16), x_pair, NN))
                yd.append(_pair_select(halves, P))
            y_ref[rows, :] = jnp.concatenate(yd, axis=1) + yo
            st[...] = _head_rows(jnp.exp(acs[L - 1:L, :]), HPG, P) * s0 + _dot((xdt * ws).astype(BF16), B, TN)

    R = CPS * L
    y, states = pl.pallas_call(
        body, name=name, grid=(SG, nc // CPS),
        in_specs=[pl.BlockSpec((R, GW), lambda g, c: (c, g)),
                  pl.BlockSpec((R, SN), lambda g, c: (c, bcol + g)),
                  pl.BlockSpec((R, SN), lambda g, c: (c, ccol + g)),
                  pl.BlockSpec((R, LANES), lambda g, c: (c, 0)),
                  pl.BlockSpec((R, LANES), lambda g, c: (c, 0)),
                  pl.BlockSpec((HPG, R), lambda g, c: (g, c)),
                  pl.BlockSpec((1, LANES, GW), lambda g, c: (g, 0, 0)),
                  pl.BlockSpec((1, LANES, HPG * LANES), lambda g, c: (g, 0, 0))],
        out_specs=[pl.BlockSpec((R, GW), lambda g, c: (c, g)),
                   pl.BlockSpec((CPS, HPG, P, SN), lambda g, c: (c, g, 0, 0))],
        out_shape=[jax.ShapeDtypeStruct((S, SI), F32), jax.ShapeDtypeStruct((nc, cfg.SH, P, SN), F32)],
        scratch_shapes=[pltpu.VMEM((GW, SN), F32)],
        compiler_params=_cp(("parallel", "arbitrary")),
    )(xc, xc, xc, dt, acum, act, sel_p, sel_l)
    return y, states


def ssd_scan_bwd(xc, dt, acum, act, sel_p, sel_l, states, y, dy, dvec, cfg, name, side):
    S, L, P, SN, HPG, SG, SI = cfg.S, cfg.L, cfg.P, cfg.SN, cfg.HPG, cfg.SG, cfg.SI
    nc = S // L
    GW = HPG * P
    bcol, ccol = _blk(SI, SN), _blk(SI + cfg.GN, SN)

    def rc(c):
        return nc // CPS - 1 - c

    def body(xs_ref, b_ref, c_ref, dtn_ref, acn_ref, at_ref, sp_ref, sl_ref, st_ref, y_ref, dy_ref, dk_ref,
             dxs_ref, db_ref, dc_ref, dac_ref, dxsum_ref, dst):
        @pl.when(pl.program_id(1) == 0)
        def _():
            dst[...] = jnp.zeros_like(dst)

        sel = sp_ref[0]
        causal = lax.broadcasted_iota(jnp.int32, (L, L), 0) >= lax.broadcasted_iota(jnp.int32, (L, L), 1)
        low = lax.broadcasted_iota(jnp.int32, (L, LANES), 1) < P
        is_last = lax.broadcasted_iota(jnp.int32, (L, LANES), 0) == L - 1
        ones = jnp.ones((16, SN), BF16)
        for ci in reversed(range(CPS)):
            rows = slice(ci * L, (ci + 1) * L)
            acn = acn_ref[rows, :]
            dts = _spread_heads(dtn_ref[rows, :], sel)
            acs = _spread_heads(acn, sl_ref[0])
            a_p = _pair_lanes(acs, HPG, P)
            B = b_ref[rows, :].astype(BF16)
            C = c_ref[rows, :].astype(BF16)
            G = _dot(C, B, NT)
            xs = xs_ref[rows, :]
            dY = dy_ref[rows, :].astype(F32)
            xdt = xs * dts
            xdtb = xdt.astype(BF16)
            dYb = dY.astype(BF16)
            s0 = st_ref[ci].reshape(GW, SN)
            s0b = s0.astype(BF16)
            ds1 = dst[...]
            ds1b = ds1.astype(BF16)
            ws = jnp.exp(a_p[L - 1:L, :] - a_p)
            dR = (jnp.exp(a_p) * dY).astype(BF16)
            dX2 = ws * _dot(B, ds1b, NT)
            dgsum = jnp.zeros((L, L), F32)
            dX1, yd = [], []
            for jp in range(HPG // 2):
                lanes = slice(jp * LANES, (jp + 1) * LANES)
                x_pair, dy_pair = xdtb[:, lanes], dYb[:, lanes]
                h1, h2 = [], []
                for h, j in enumerate((2 * jp, 2 * jp + 1)):
                    dm = jnp.where(causal, jnp.exp(acs[:, j * LANES:(j + 1) * LANES] - at_ref[j:j + 1, rows]), 0.0)
                    mine = low if h == 0 else jnp.logical_not(low)
                    dgsum = dgsum + _dot(jnp.where(mine, dy_pair, jnp.zeros_like(dy_pair)), x_pair, NT) * dm
                    Mb = (G * dm).astype(BF16)
                    h1.append(_dot(Mb, dy_pair, TN))
                    h2.append(_dot(Mb, x_pair, NN))
                dX1.append(_pair_select(h1, P))
                yd.append(_pair_select(h2, P))
            dX1 = jnp.concatenate(dX1, axis=1)
            dX = dX1 + dX2
            pair = (dYb.astype(F32) - dY) * jnp.concatenate(yd, axis=1) - xdtb.astype(F32) * dX1
            through = _segment_sums(xdt * dX2, sel)
            u = ds1 * s0
            u_hi = u.astype(BF16)
            u_rows = _dot(ones, jnp.concatenate([u_hi, (u - u_hi.astype(F32)).astype(BF16)], axis=0), NT)
            u_rows = u_rows[:, :GW] + u_rows[:, GW:]
            at_end = jnp.exp(acn[L - 1:L, :]) * _segment_sums(u_rows, sel)[0:1, :] + \
                jnp.sum(through, axis=0, keepdims=True)
            dac_ref[0, rows, :] = _segment_sums(dY * y_ref[rows, :] + pair, sel) - through + \
                jnp.where(is_last, at_end, 0.0)
            dxsum_ref[0, rows, :] = _segment_sums(dX * xs, sel)
            dxs_ref[rows, :] = dX * dts + dk_ref[...] * dY
            dst[...] = _head_rows(jnp.exp(acs[L - 1:L, :]), HPG, P) * ds1 + _dot(dR, C, TN)
            dgb = dgsum.astype(BF16)
            dc_ref[rows, :] = _dot(dR, s0b, NN) + _dot(dgb, B, NN)
            db_ref[rows, :] = _dot((xdt * ws).astype(BF16), ds1b, NN) + _dot(dgb, C, TN)

    R = CPS * L
    wide = pl.BlockSpec((R, GW), lambda g, c: (rc(c), g))
    colspec = pl.BlockSpec((1, R, LANES), lambda g, c: (g, rc(c), 0))
    whole = pl.BlockSpec(memory_space=pl.ANY)
    arrs, gathers = side
    grid = (SG, nc // CPS)
    res = pl.pallas_call(
        with_exchange(body, 12, 5, gathers, grid), name=name, grid=grid,
        in_specs=[wide,
                  pl.BlockSpec((R, SN), lambda g, c: (rc(c), bcol + g)),
                  pl.BlockSpec((R, SN), lambda g, c: (rc(c), ccol + g)),
                  pl.BlockSpec((R, LANES), lambda g, c: (rc(c), 0)),
                  pl.BlockSpec((R, LANES), lambda g, c: (rc(c), 0)),
                  pl.BlockSpec((HPG, R), lambda g, c: (g, rc(c))),
                  pl.BlockSpec((1, LANES, GW), lambda g, c: (g, 0, 0)),
                  pl.BlockSpec((1, LANES, HPG * LANES), lambda g, c: (g, 0, 0)),
                  pl.BlockSpec((CPS, HPG, P, SN), lambda g, c: (rc(c), g, 0, 0)),
                  wide, wide,
                  pl.BlockSpec((1, GW), lambda g, c: (0, g))] + [whole] * len(arrs),
        out_specs=[wide,
                   pl.BlockSpec((R, SN), lambda g, c: (rc(c), g)),
                   pl.BlockSpec((R, SN), lambda g, c: (rc(c), g)),
                   colspec, colspec] + [whole] * len(arrs),
        out_shape=[jax.ShapeDtypeStruct((S, SI), F32), jax.ShapeDtypeStruct((S, cfg.GN), F32),
                   jax.ShapeDtypeStruct((S, cfg.GN), F32),
                   jax.ShapeDtypeStruct((SG, S, LANES), F32), jax.ShapeDtypeStruct((SG, S, LANES), F32)] +
        _exchange_shapes(arrs, gathers),
        scratch_shapes=[pltpu.VMEM((GW, SN), F32)] + _exchange_sems(len(arrs)),
        compiler_params=_cp(("arbitrary", "arbitrary")),
    )(xc, xc, xc, dt, acum, act, sel_p, sel_l, states, y, dy, dvec, *arrs)
    return res[:5], res[5:]


def dt_bwd(dac, dxsum, dt_raw, dt, dt_bias, a_log, cfg, name):
    S, L, SG = cfg.S, cfg.L, cfg.SG

    def body(da_ref, dx_ref, x_ref, dt_ref, b_ref, al_ref, o_ref, gb_ref, ga_ref):
        a = -jnp.exp(al_ref[...])
        dtv = dt_ref[...]
        dxs = jnp.sum(dx_ref[...], axis=0)
        upper = jnp.where(lax.broadcasted_iota(jnp.int32, (L, L), 1) >= lax.broadcasted_iota(jnp.int32, (L, L), 0),
                          1.0, 0.0).astype(F32)
        dda = lax.dot_general(upper, jnp.sum(da_ref[...], axis=0), (NN, ((), ())), precision=lax.Precision.HIGHEST,
                              preferred_element_type=F32)
        draw = (dxs + dda * a) * _sigmoid(x_ref[...] + b_ref[...])
        o_ref[...] = draw.astype(BF16)
        gb = jnp.sum(draw, axis=0, keepdims=True)
        ga = jnp.sum(dda * dtv, axis=0, keepdims=True) * a

        @pl.when(pl.program_id(0) == 0)
        def _():
            gb_ref[...] = gb
            ga_ref[...] = ga

        @pl.when(pl.program_id(0) > 0)
        def _():
            gb_ref[...] += gb
            ga_ref[...] += ga

    row = pl.BlockSpec((L, LANES), lambda i: (i, 0))
    vec = pl.BlockSpec((1, LANES), lambda i: (0, 0))
    return pl.pallas_call(
        body, name=name, grid=(S // L,),
        in_specs=[pl.BlockSpec((SG, L, LANES), lambda i: (0, i, 0))] * 2 + [row, row, vec, vec],
        out_specs=[row, vec, vec],
        out_shape=[jax.ShapeDtypeStruct((S, LANES), BF16), jax.ShapeDtypeStruct((1, LANES), F32),
                   jax.ShapeDtypeStruct((1, LANES), F32)],
        compiler_params=_cp(("arbitrary",)),
    )(dac, dxsum, dt_raw, dt, dt_bias, a_log)


def gated_norm_fwd(y, xc, proj, dvec, nw, cfg, name, tm=128):
    S, SI = cfg.S, cfg.SI

    def body(y_ref, xs_ref, z_ref, d_ref, w_ref, o_ref):
        z = z_ref[...]
        yg = (y_ref[...] + d_ref[...] * xs_ref[...]) * (z * _sigmoid(z))
        r = lax.rsqrt(jnp.mean(yg * yg, axis=-1, keepdims=True) + RMS_EPS)
        o_ref[...] = ((yg * r) * w_ref[...]).astype(BF16)

    row = pl.BlockSpec((tm, SI), lambda i: (i, 0))
    vec = pl.BlockSpec((1, SI), lambda i: (0, 0))
    return pl.pallas_call(
        body, name=name, grid=(S // tm,),
        in_specs=[row, row, pl.BlockSpec((tm, SI), lambda i: (i, _blk(cfg.zs0, SI))), vec, vec],
        out_specs=row, out_shape=jax.ShapeDtypeStruct((S, SI), BF16),
        compiler_params=_cp(("parallel",)),
    )(y, xc, proj, dvec, nw)


def gated_norm_bwd(dyn, y, xc, proj, dvec, nw, cfg, name, tm=128):
    S, SI = cfg.S, cfg.SI

    def body(dn_ref, y_ref, xs_ref, z_ref, d_ref, w_ref, dy_ref, dz_ref, gw_ref, gd_ref):
        z = z_ref[...]
        s = _sigmoid(z)
        sz = z * s
        xs = xs_ref[...]
        yf = y_ref[...] + d_ref[...] * xs
        yg = yf * sz
        r = lax.rsqrt(jnp.mean(yg * yg, axis=-1, keepdims=True) + RMS_EPS)
        dn = dn_ref[...].astype(F32)
        g = dn * w_ref[...]
        dyg = r * g - yg * (r * r * r) * jnp.mean(g * yg, axis=-1, keepdims=True)
        dy = dyg * sz
        dy_ref[...] = dy.astype(BF16)
        dz_ref[...] = (dyg * yf * (s * (1.0 + z * (1.0 - s)))).astype(BF16)
        gw = jnp.sum(dn * (yg * r), axis=0, keepdims=True)
        gd = jnp.sum(dy * xs, axis=0, keepdims=True)

        @pl.when(pl.program_id(0) == 0)
        def _():
            gw_ref[...] = gw
            gd_ref[...] = gd

        @pl.when(pl.program_id(0) > 0)
        def _():
            gw_ref[...] += gw
            gd_ref[...] += gd

    row = pl.BlockSpec((tm, SI), lambda i: (i, 0))
    vec = pl.BlockSpec((1, SI), lambda i: (0, 0))
    return pl.pallas_call(
        body, name=name, grid=(S // tm,),
        in_specs=[row, row, row, pl.BlockSpec((tm, SI), lambda i: (i, _blk(cfg.zs0, SI))), vec, vec],
        out_specs=[row, row, vec, vec],
        out_shape=[jax.ShapeDtypeStruct((S, SI), BF16), jax.ShapeDtypeStruct((S, SI), BF16),
                   jax.ShapeDtypeStruct((1, SI), F32), jax.ShapeDtypeStruct((1, SI), F32)],
        compiler_params=_cp(("arbitrary",)),
    )(dyn, y, xc, proj, dvec, nw)


def _shard_columns(cfg, main, dt):
    dt0 = 4 * cfg.AW + cfg.SI + cfg.CD
    ws = cfg.N_IN // N_DEV
    out = []
    for k in range(N_DEV):
        lo, hi, parts = k * ws, (k + 1) * ws, []
        if lo < dt0:
            parts.append((main, lo, min(hi, dt0)))
        if lo < dt0 + cfg.SH and hi > dt0:
            parts.append((dt, max(lo, dt0) - dt0, min(hi, dt0 + cfg.SH) - dt0))
        if hi > dt0 + cfg.SH:
            parts.append((main, max(lo, dt0 + cfg.SH) - cfg.SH, hi - cfg.SH))
        out.append(parts)
    return out


def local_step(cfg, x, tgt, norm_w, conv_w, conv_b, dt_bias, a_log, d_skip, ssm_norm_w, final_norm_w,
               w_main, w_dt, shards, dt0):
    S, D = cfg.S, cfg.D
    slopes = _slope_table(cfg)
    dt_bias_p = _pad_lanes(dt_bias)
    a_log_p = _pad_lanes(a_log)
    dvec = _spread(d_skip, cfg.P)

    hn = rmsnorm_fwd(x, norm_w, "rmsnorm_fwd")
    proj, gathered = matmul(hn, w_main, 'nn', 1024, 2048, 2048, F32, "in_proj", side=(shards, [True] * 3))
    w_attn, w_ssm, w_out = gathered[0].reshape(cfg.AW, D), gathered[1].reshape(cfg.SI, D), gathered[2].reshape(D, D)
    dt_raw = matmul(hn, w_dt, 'nn', 512, 128, 2048, F32, "in_proj_dt")
    o_a, o_mix, ltot = attn_fused_fwd(proj, slopes, cfg, "attn_fwd")
    xc = conv_fwd(proj, conv_w, conv_b, cfg, "conv_fwd")
    dt, acum = ssd_prep(dt_raw, dt_bias_p, a_log_p, cfg, "ssd_prep")
    act = acum[:, :cfg.SH].T
    sel_p, sel_l = _head_selectors(cfg)
    y, states = ssd_scan_fwd(xc, dt, acum, act, sel_p, sel_l, cfg, "ssd_scan_fwd")
    y_n = gated_norm_fwd(y, xc, proj, dvec, ssm_norm_w, cfg, "gated_norm_fwd")
    a_out, s_out, merged = branch_merge(o_a, w_attn, y_n, w_ssm, proj, cfg, "branch_merge")
    dout, loss_p, g_final_w = out_proj_final(merged, w_out, x, final_norm_w.reshape(1, D), tgt, "out_proj_final")

    g_w_out = matmul(merged, dout, 'tn', 1024, 1024, 2048, BF16, "g_w_out")
    da_out, ds_out, dga, dgs = merge_bwd(dout, w_out, a_out, s_out, proj, cfg, "merge_bwd")
    g_w_attn = matmul(o_a, da_out, 'tn', 1024, 1024, 2048, BF16, "g_w_attn")
    g_w_ssm = matmul(y_n, ds_out, 'tn', 1024, 1024, 2048, BF16, "g_w_ssm")
    do_a = matmul(da_out, w_attn, 'nt', 512, 1024, 2048, BF16, "d_o_a")
    dyn = matmul(ds_out, w_ssm, 'nt', 512, 1024, 2048, BF16, "d_y_n")
    dy, dz_s, g_ssm_norm, g_dvec = gated_norm_bwd(dyn, y, xc, proj, dvec, ssm_norm_w, cfg, "gated_norm_bwd")
    sends = [g.reshape((N_DEV, g.shape[0] // N_DEV, D)) for g in (g_w_attn, g_w_ssm, g_w_out)]
    (dxs, dB, dC, dac_g, dxsum_g), (r_attn, r_ssm, r_out) = ssd_scan_bwd(
        xc, dt, acum, act, sel_p, sel_l, states, y, dy, dvec, cfg, "ssd_scan_bwd", side=(sends, [False] * 3))
    ddt_raw, g_dt_bias, g_a_log = dt_bwd(dac_g, dxsum_g, dt_raw, dt, dt_bias_p, a_log_p, cfg, "dt_bwd")
    dxbc, g_cw, g_cb = [], [], []
    for nm, piece, c_off in (("xs", dxs, 0), ("b", dB, cfg.SI), ("c", dC, cfg.SI + cfg.GN)):
        dx, gw, gb = conv_bwd(proj, piece, conv_w, conv_b, cfg, "conv_bwd_" + nm, c_off)
        dxbc.append(dx)
        g_cw.append(gw)
        g_cb.append(gb)
    g_conv_w, g_conv_b = jnp.concatenate(g_cw, axis=1), jnp.concatenate(g_cb, axis=1)
    dq, dk, dv, dz_a = attn_fused_bwd(proj, do_a, o_mix, ltot, slopes, cfg, "attn_bwd")
    dproj = jnp.concatenate([dq, dk, dv, dz_a, dz_s] + dxbc + [dga, dgs], axis=1)
    def slabs(g_main, g_dt):
        return jnp.stack([jnp.concatenate([g[:, lo:hi] for g, lo, hi in parts], axis=1)
                          for parts in _shard_columns(cfg, g_main, g_dt)])

    cut = D // 4
    g_w_dt = matmul(hn, ddt_raw, 'tn', 1024, 128, 2048, BF16, "g_w_dt")
    g_w_top = matmul(hn[:, :cut], dproj, 'tn', 512, 2048, 2048, BF16, "g_w_main_top")
    g_w_bot, (r_top,) = matmul(hn[:, cut:], dproj, 'tn', 512, 2048, 2048, BF16, "g_w_main_rest",
                               side=([slabs(g_w_top, g_w_dt[:cut])], [False]))
    dhn_a, (r_bot,) = matmul(dproj, w_main, 'nt', 1024, 1024, 2048, F32, "d_hn",
                             side=([slabs(g_w_bot, g_w_dt[cut:])], [False]))
    dhn_b = matmul(ddt_raw, w_dt, 'nt', 512, 1024, 128, F32, "d_hn_dt")
    grad_x, g_norm_w = rmsnorm_bwd(dhn_a, dhn_b, x, norm_w, dout, "rmsnorm_bwd")

    g_d_skip = jnp.sum(g_dvec.reshape(cfg.SH, cfg.P), axis=1).reshape(1, cfg.SH)
    small = dict(norm_w=g_norm_w, conv_b=g_conv_b, dt_bias=g_dt_bias[:, :cfg.SH], a_log=g_a_log[:, :cfg.SH],
                 d_skip=g_d_skip, ssm_norm_w=g_ssm_norm, final_norm_w=g_final_w, conv_w=g_conv_w[:cfg.KC])
    return loss_p, grad_x, small, dict(w_in=[r_top, r_bot], w_attn=[r_attn], w_ssm=[r_ssm], w_out=[r_out])


def _mesh_pos():
    return lax.axis_index("x"), lax.axis_index("y"), lax.axis_index("c")


def _flat(pos):
    return 4 * pos[0] + 2 * pos[1] + pos[2]


def _exchange_shapes(arrs, gathers):
    return [jax.ShapeDtypeStruct(((N_DEV,) + a.shape) if g else a.shape, a.dtype) for a, g in zip(arrs, gathers)]


def _exchange_sems(n):
    return [pltpu.SemaphoreType.DMA((n * (N_DEV - 1),)), pltpu.SemaphoreType.DMA((n * (N_DEV - 1),)),
            pltpu.SemaphoreType.DMA((n,))]


def _exchange_copies(ins, outs, gathers, send_sems, recv_sems, loc_sems):
    pos = _mesh_pos()
    me = _flat(pos)
    starts, waits = [], []
    for a in range(len(ins)):
        mine = ins[a] if gathers[a] else ins[a].at[me]
        loc = pltpu.make_async_copy(mine, outs[a].at[me], loc_sems.at[a])
        starts.append(loc)
        waits.append(loc)
        for k in range(1, N_DEV):
            flip = ((k >> 2) & 1, (k >> 1) & 1, k & 1)
            peer = tuple(1 - p if f else p for p, f in zip(pos, flip))
            pk = _flat(peer)
            src = ins[a] if gathers[a] else ins[a].at[pk]
            sems = dict(send_sem=send_sems.at[a * (N_DEV - 1) + k - 1], recv_sem=recv_sems.at[a * (N_DEV - 1) + k - 1],
                        device_id=peer, device_id_type=pl.DeviceIdType.MESH)
            starts.append(pltpu.make_async_remote_copy(src_ref=src, dst_ref=outs[a].at[me], **sems))
            waits.append(pltpu.make_async_remote_copy(src_ref=src, dst_ref=outs[a].at[pk], **sems))
    return starts, waits


def exchange(arrs, gathers, name):
    n = len(arrs)

    def body(*refs):
        starts, waits = _exchange_copies(refs[:n], refs[n:2 * n], gathers, *refs[2 * n:])
        for cp in starts:
            cp.start()
        for cp in waits:
            cp.wait()

    hbm = pl.BlockSpec(memory_space=pltpu.HBM)
    return pl.pallas_call(
        body, name=name, in_specs=[hbm] * n, out_specs=[hbm] * n, out_shape=_exchange_shapes(arrs, gathers),
        scratch_shapes=_exchange_sems(n),
    )(*arrs)


def gather_two_level(arrs, chunks, name):
    n = len(arrs)

    def body(*refs):
        ins, outs = refs[:n], refs[n:2 * n]
        send_sems, recv_sems, loc_sems = refs[2 * n:]
        x, y, c = _mesh_pos()
        me, sib = (x, y, c), (x, y, 1 - c)
        chips = [(1 - x, y), (x, 1 - y), (1 - x, 1 - y)]
        plan, base = [], 0
        for a in range(n):
            step = arrs[a].shape[0] // chunks[a]
            for q in range(chunks[a]):
                plan.append((a, pl.ds(q * step, step), base))
                base += N_DEV - 1

        def copy(a, rows, sem, block, to, own=False):
            dst = outs[a].at[_flat(block), rows]
            return pltpu.make_async_remote_copy(
                src_ref=ins[a].at[rows] if own else dst, dst_ref=dst, send_sem=send_sems.at[sem],
                recv_sem=recv_sems.at[sem], device_id=to, device_id_type=pl.DeviceIdType.MESH)

        local = [pltpu.make_async_copy(ins[a], outs[a].at[_flat(me)], loc_sems.at[a]) for a in range(n)]
        for cp in local:
            cp.start()
        sent = []
        for a, rows, s in plan:
            sent.append(copy(a, rows, s, me, sib, own=True))
            sent += [copy(a, rows, s + 1 + j, me, (*chip, c), own=True) for j, chip in enumerate(chips)]
        for cp in sent:
            cp.start()
        for a, rows, s in plan:
            for j, chip in enumerate(chips):
                copy(a, rows, s + 1 + j, (*chip, c), me).wait_recv()
                passed = copy(a, rows, s + 4 + j, (*chip, c), sib)
                passed.start()
                sent.append(passed)
        for a, rows, s in plan:
            copy(a, rows, s, sib, me).wait_recv()
            for j, chip in enumerate(chips):
                copy(a, rows, s + 4 + j, (*chip, 1 - c), me).wait_recv()
        for cp in sent:
            cp.wait_send()
        for cp in local:
            cp.wait()

    hbm = pl.BlockSpec(memory_space=pltpu.HBM)
    nsem = (N_DEV - 1) * sum(chunks)
    return pl.pallas_call(
        body, name=name, in_specs=[hbm] * n, out_specs=[hbm] * n, out_shape=_exchange_shapes(arrs, [True] * n),
        scratch_shapes=[pltpu.SemaphoreType.DMA((nsem,)), pltpu.SemaphoreType.DMA((nsem,)),
                        pltpu.SemaphoreType.DMA((n,))],
    )(*arrs)


def with_exchange(body, n_in, n_out, gathers, grid):
    n = len(gathers)

    def wrapped(*refs):
        ins, sends = refs[:n_in], refs[n_in:n_in + n]
        outs, recvs = refs[n_in + n:n_in + n + n_out], refs[n_in + 2 * n + n_out - n:n_in + 2 * n + n_out]
        scratch, sems = refs[n_in + 2 * n + n_out:-3], refs[-3:]
        ids = [pl.program_id(d) for d in range(len(grid))]
        first = functools.reduce(jnp.logical_and, [i == 0 for i in ids])
        last = functools.reduce(jnp.logical_and, [i == g - 1 for i, g in zip(ids, grid)])

        @pl.when(first)
        def _():
            for cp in _exchange_copies(sends, recvs, gathers, *sems)[0]:
                cp.start()

        body(*ins, *outs, *scratch)

        @pl.when(last)
        def _():
            for cp in _exchange_copies(sends, recvs, gathers, *sems)[1]:
                cp.wait()

    return wrapped


def adamw(g_src, w, m, v, summed, name, tr=64):
    R, C = w.shape
    tr = min(tr, R)
    assert R % tr == 0
    parts = g_src if summed else [g_src]
    starts = [sum(p.shape[1] for p in parts[:k]) // tr for k in range(len(parts))] if summed else [0]
    counts = [p.shape[1] // tr for p in parts] if summed else [R // tr]

    def body(*refs):
        g_refs, (w_ref, m_ref, v_ref, g_out, d_out, m_out, v_out) = refs[:len(parts)], refs[len(parts):]
        if summed:
            g = None
            for k, g_ref in enumerate(g_refs):
                gk = g_ref[0].astype(F32)
                for j in range(1, N_DEV):
                    gk = gk + g_ref[j].astype(F32)
                g = gk if g is None else jnp.where(pl.program_id(0) >= starts[k], gk, g)
        else:
            g = g_refs[0][...]
        mn = ADAM_B1 * m_ref[...] + (1.0 - ADAM_B1) * g
        vn = ADAM_B2 * v_ref[...] + (1.0 - ADAM_B2) * (g * g)
        m_hat = mn / (1.0 - ADAM_B1 ** ADAM_STEP)
        v_hat = vn / (1.0 - ADAM_B2 ** ADAM_STEP)
        g_out[...] = g
        d_out[...] = -ADAM_LR * (m_hat / (jnp.sqrt(v_hat) + ADAM_EPS) + ADAM_WD * w_ref[...])
        m_out[...] = mn
        v_out[...] = vn

    row = pl.BlockSpec((tr, C), lambda i: (i, 0))
    if summed:
        gspecs = [pl.BlockSpec((N_DEV, tr, C), lambda i, s=st, n=nb: (0, jnp.clip(i - s, 0, n - 1), 0))
                  for st, nb in zip(starts, counts)]
    else:
        gspecs = [row]
    sh = jax.ShapeDtypeStruct((R, C), F32)
    return pl.pallas_call(
        body, name=name, grid=(R // tr,), in_specs=gspecs + [row, row, row], out_specs=[row] * 4, out_shape=[sh] * 4,
        compiler_params=_cp(("parallel",)),
    )(*parts, w, m, v)


SMALL = ('norm_w', 'conv_b', 'dt_bias', 'a_log', 'd_skip', 'ssm_norm_w', 'final_norm_w')


def _rows(n):
    return -(-n // (8 * LANES)) * 8


def _pack(vals):
    parts = []
    for a in vals:
        f = a.reshape(-1)
        parts.append(jnp.pad(f, (0, _rows(f.size) * LANES - f.size)).reshape(-1, LANES))
    return jnp.concatenate(parts, axis=0)


def _unpack(packed, shapes):
    out, r = [], 0
    for s in shapes:
        n = math.prod(s)
        out.append(packed[r:r + _rows(n)].reshape(-1)[:n].reshape(s))
        r += _rows(n)
    return out


def kernel(x, norm_w, w_in, conv_w, conv_b, dt_bias, a_log, d_skip, ssm_norm_w, w_attn_branch, w_ssm_branch, w_out, final_norm_w, loss_target, m_norm_w, m_w_in, m_conv_w, m_conv_b, m_dt_bias, m_a_log, m_d_skip, m_ssm_norm_w, m_w_attn_branch, m_w_ssm_branch, m_w_out, m_final_norm_w, v_norm_w, v_w_in, v_conv_w, v_conv_b, v_dt_bias, v_a_log, v_d_skip, v_ssm_norm_w, v_w_attn_branch, v_w_ssm_branch, v_w_out, v_final_norm_w):
    cfg = CFG
    D, SH = cfg.D, cfg.SH
    me = _flat(_mesh_pos())
    dt0 = 4 * cfg.AW + cfg.SI + cfg.CD
    ws = w_in.shape[-1]

    g_in, g_cw = gather_two_level([w_in[0].astype(BF16), conv_w[0]], [4, 1], "gather_w_in")
    main_cols, dt_cols = [], []
    for k, parts in enumerate(_shard_columns(cfg, "main", "dt")):
        at = 0
        for which, lo, hi in parts:
            (main_cols if which == "main" else dt_cols).append(g_in[k][:, at:at + hi - lo])
            at += hi - lo
    w_main = jnp.concatenate(main_cols, axis=1)
    w_dt = _pad_lanes(jnp.concatenate(dt_cols, axis=1))
    conv_full = g_cw.transpose(1, 0, 2).reshape(cfg.KC, cfg.CD)
    shards = [w_attn_branch[0].astype(BF16), w_ssm_branch[0].astype(BF16), w_out[0].astype(BF16)]

    loss_p, grad_x, small, recv = local_step(
        cfg, x[0], loss_target[0], norm_w, conv_full, conv_b, dt_bias, a_log, d_skip,
        ssm_norm_w, final_norm_w, w_main, w_dt, shards, dt0)

    upd = {}
    upd['w_in'] = adamw(recv['w_in'], w_in[0], m_w_in[0], v_w_in[0], True, "adamw_w_in")
    upd['w_attn_branch'] = adamw(recv['w_attn'], w_attn_branch[0], m_w_attn_branch[0], v_w_attn_branch[0], True,
                                 "adamw_w_attn")
    upd['w_ssm_branch'] = adamw(recv['w_ssm'], w_ssm_branch[0], m_w_ssm_branch[0], v_w_ssm_branch[0], True,
                                "adamw_w_ssm")
    upd['w_out'] = adamw(recv['w_out'], w_out[0], m_w_out[0], v_w_out[0], True, "adamw_w_out")

    extra = [jnp.zeros((cfg.KC, cfg.CD), F32), jnp.zeros((1, 1), F32)]
    shapes = [small[n].shape for n in SMALL] + [e.shape for e in extra]
    part = _pack([small[n] for n in SMALL] + [small['conv_w'], loss_p[:, :1]])
    gathered, = exchange([part], [True], "gather_small")
    given = dict(norm_w=(norm_w, m_norm_w, v_norm_w), conv_b=(conv_b, m_conv_b, v_conv_b),
                 dt_bias=(dt_bias, m_dt_bias, v_dt_bias), a_log=(a_log, m_a_log, v_a_log),
                 d_skip=(d_skip, m_d_skip, v_d_skip), ssm_norm_w=(ssm_norm_w, m_ssm_norm_w, v_ssm_norm_w),
                 final_norm_w=(final_norm_w, m_final_norm_w, v_final_norm_w))
    packed = [_pack([given[n][t] for n in SMALL] + extra) for t in range(3)]
    outs = adamw([gathered], *packed, True, "adamw_small", tr=part.shape[0])
    unpacked = [_unpack(o, shapes) for o in outs]
    for i, n in enumerate(SMALL):
        upd[n] = [u[i].reshape(given[n][0].shape) for u in unpacked]
    loss = unpacked[0][-1].reshape(())
    cw = conv_w.shape[-1]
    g_cw_mine = lax.dynamic_slice_in_dim(unpacked[0][-2], me * cw, cw, axis=1)
    upd['conv_w'] = adamw(g_cw_mine.reshape(-1, LANES), conv_w.reshape(-1, LANES), m_conv_w.reshape(-1, LANES),
                          v_conv_w.reshape(-1, LANES), False, "adamw_conv_w")

    order = ['norm_w', 'w_in', 'conv_w', 'conv_b', 'dt_bias', 'a_log', 'd_skip', 'ssm_norm_w', 'w_attn_branch',
             'w_ssm_branch', 'w_out', 'final_norm_w']
    like = dict(norm_w=norm_w, w_in=w_in, conv_w=conv_w, conv_b=conv_b, dt_bias=dt_bias, a_log=a_log, d_skip=d_skip,
                ssm_norm_w=ssm_norm_w, w_attn_branch=w_attn_branch, w_ssm_branch=w_ssm_branch, w_out=w_out,
                final_norm_w=final_norm_w)
    result = [loss, grad_x[None]]
    for t in range(4):
        result += [upd[n][t].reshape(like[n].shape) for n in order]
    return tuple(result)
```

```python
import functools
import math
from typing import NamedTuple

import jax
import jax.numpy as jnp
from jax import lax
from jax.experimental import pallas as pl
from jax.experimental.pallas import tpu as pltpu

F32 = jnp.float32
BF16 = jnp.bfloat16
RMS_EPS = 1e-6
NEG = -1e30
N_DEV = 8
CPS = 8
LANES = 128
ATTN_BLOCK = 128
ADAM_LR, ADAM_B1, ADAM_B2, ADAM_EPS, ADAM_WD, ADAM_STEP = 0.001, 0.9, 0.999, 1e-08, 0.01, 10
VMEM_LIMIT = 56 * 1024 * 1024


class Cfg(NamedTuple):
    D: int = 2048
    S: int = 8192
    AH: int = 16
    E: int = 128
    patterns: tuple = ((128, 1), (512, 4), (2048, 16))
    SI: int = 4096
    P: int = 64
    SG: int = 8
    SN: int = 128
    KC: int = 4
    L: int = 128

    @property
    def AW(self): return self.AH * self.E
    @property
    def SH(self): return self.SI // self.P
    @property
    def HPG(self): return self.SH // self.SG
    @property
    def GN(self): return self.SG * self.SN
    @property
    def CD(self): return self.SI + 2 * self.GN
    @property
    def k0(self): return self.AW
    @property
    def v0(self): return 2 * self.AW
    @property
    def za0(self): return 3 * self.AW
    @property
    def zs0(self): return 4 * self.AW
    @property
    def xbc0(self): return 4 * self.AW + self.SI
    @property
    def ga0(self): return self.xbc0 + self.CD
    @property
    def gs0(self): return self.ga0 + self.D
    @property
    def NP(self): return self.gs0 + self.D
    @property
    def N_IN(self): return self.NP + self.SH


CFG = Cfg()


def _cp(sem=None, vmem=VMEM_LIMIT):
    return pltpu.CompilerParams(dimension_semantics=sem, vmem_limit_bytes=vmem)


def _sigmoid(z):
    return 1.0 / (1.0 + jnp.exp(-z))


def _dot(a, b, dims):
    return lax.dot_general(a, b, (dims, ((), ())), preferred_element_type=F32)


NN = ((1,), (0,))
NT = ((1,), (1,))
TN = ((0,), (0,))


def _blk(off, width):
    assert off % width == 0, (off, width)
    return off // width


def matmul(a, b, mode, tm, tn, tk, out_dtype, name, side=None, a_cols=None):
    m0 = 0
    if mode == 'nn':
        (M, K), (_, N) = a.shape, b.shape
    elif mode == 'nt':
        (M, K), (N, _) = a.shape, b.shape
    else:
        (K, M), (_, N) = a.shape, b.shape
        if a_cols is not None:
            m0, M = a_cols
    tm, tn, tk = math.gcd(min(tm, M), m0), min(tn, N), min(tk, K)
    assert M % tm == 0 and N % tn == 0 and K % tk == 0, (M, N, K, tm, tn, tk)
    nk = K // tk
    dims = {'nn': NN, 'nt': NT, 'tn': TN}[mode]

    def body(a_ref, b_ref, o_ref, *acc):
        part = _dot(a_ref[...].astype(BF16), b_ref[...].astype(BF16), dims)
        if nk == 1:
            o_ref[...] = part.astype(out_dtype)
        else:
            acc_ref, = acc
            k = pl.program_id(2)

            @pl.when(k == 0)
            def _():
                acc_ref[...] = part

            @pl.when(k > 0)
            def _():
                acc_ref[...] += part

            @pl.when(k == nk - 1)
            def _():
                o_ref[...] = acc_ref[...].astype(out_dtype)

    if mode == 'tn':
        a_spec = pl.BlockSpec((tk, tm), lambda n, m, k: (k, _blk(m0, tm) + m))
    else:
        a_spec = pl.BlockSpec((tm, tk), lambda n, m, k: (m, k))
    if mode == 'nt':
        b_spec = pl.BlockSpec((tn, tk), lambda n, m, k: (n, k))
    else:
        b_spec = pl.BlockSpec((tk, tn), lambda n, m, k: (k, n))
    grid = (N // tn, M // tm, nk)
    o_spec = pl.BlockSpec((tm, tn), lambda n, m, k: (m, n))
    o_shape = jax.ShapeDtypeStruct((M, N), out_dtype)
    acc = [] if nk == 1 else [pltpu.VMEM((tm, tn), F32)]
    if side is None:
        return pl.pallas_call(
            body, name=name, grid=grid, in_specs=[a_spec, b_spec], out_specs=o_spec, out_shape=o_shape,
            scratch_shapes=acc, compiler_params=_cp(("parallel", "parallel", "arbitrary")),
        )(a, b)
    arrs, gathers = side
    whole = pl.BlockSpec(memory_space=pl.ANY)
    res = pl.pallas_call(
        with_exchange(body, 2, 1, gathers, grid), name=name, grid=grid,
        in_specs=[a_spec, b_spec] + [whole] * len(arrs), out_specs=[o_spec] + [whole] * len(arrs),
        out_shape=[o_shape] + _exchange_shapes(arrs, gathers),
        scratch_shapes=acc + _exchange_sems(len(arrs)),
        compiler_params=_cp(("arbitrary", "arbitrary", "arbitrary")),
    )(a, b, *arrs)
    return res[0], res[1:]


def rmsnorm_fwd(x, w, name, tm=256):
    S, D = x.shape

    def body(x_ref, w_ref, o_ref):
        xv = x_ref[...]
        r = lax.rsqrt(jnp.mean(xv * xv, axis=-1, keepdims=True) + RMS_EPS)
        o_ref[...] = ((xv * r) * w_ref[...]).astype(BF16)

    return pl.pallas_call(
        body, name=name, grid=(S // tm,),
        in_specs=[pl.BlockSpec((tm, D), lambda i: (i, 0)), pl.BlockSpec((1, D), lambda i: (0, 0))],
        out_specs=pl.BlockSpec((tm, D), lambda i: (i, 0)),
        out_shape=jax.ShapeDtypeStruct((S, D), BF16),
        compiler_params=_cp(("parallel",)),
    )(x, w)


def rmsnorm_bwd(dh_a, dh_b, x, w, dout, name, tm=128):
    S, D = x.shape

    def body(da_ref, db_ref, x_ref, w_ref, do_ref, gx_ref, gw_ref):
        xv = x_ref[...]
        dh = da_ref[...] + db_ref[...]
        r = lax.rsqrt(jnp.mean(xv * xv, axis=-1, keepdims=True) + RMS_EPS)
        g = dh * w_ref[...]
        dx = r * g - xv * (r * r * r) * jnp.mean(g * xv, axis=-1, keepdims=True)
        gx_ref[...] = do_ref[...] + dx
        gw = jnp.sum(dh * (xv * r), axis=0, keepdims=True)

        @pl.when(pl.program_id(0) == 0)
        def _():
            gw_ref[...] = gw

        @pl.when(pl.program_id(0) > 0)
        def _():
            gw_ref[...] += gw

    row = pl.BlockSpec((tm, D), lambda i: (i, 0))
    vec = pl.BlockSpec((1, D), lambda i: (0, 0))
    return pl.pallas_call(
        body, name=name, grid=(S // tm,),
        in_specs=[row, row, row, vec, row],
        out_specs=[row, vec],
        out_shape=[jax.ShapeDtypeStruct((S, D), F32), jax.ShapeDtypeStruct((1, D), F32)],
        compiler_params=_cp(("arbitrary",)),
    )(dh_a, dh_b, x, w, dout)


def out_proj_final(merged, w_out, x, fw, tgt, name, tm=256):
    S, D = x.shape

    def body(m_ref, wo_ref, x_ref, w_ref, t_ref, do_ref, loss_ref, gw_ref):
        out = x_ref[...] + _dot(m_ref[...], wo_ref[...], NN)
        w = w_ref[...]
        r = lax.rsqrt(jnp.mean(out * out, axis=-1, keepdims=True) + RMS_EPS)
        yn = out * r
        err = yn * w - t_ref[...]
        lrow = 0.5 * jnp.mean(err * err, axis=-1, keepdims=True)
        lsum = jnp.zeros((1, LANES), F32) + jnp.sum(lrow, axis=0, keepdims=True)
        dfin = err * (1.0 / D)
        g = dfin * w
        do_ref[...] = r * g - out * (r * r * r) * jnp.mean(g * out, axis=-1, keepdims=True)
        gw = jnp.sum(dfin * yn, axis=0, keepdims=True)

        @pl.when(pl.program_id(0) == 0)
        def _():
            gw_ref[...] = gw
            loss_ref[...] = lsum

        @pl.when(pl.program_id(0) > 0)
        def _():
            gw_ref[...] += gw
            loss_ref[...] += lsum

    row = pl.BlockSpec((tm, D), lambda i: (i, 0))
    vec = pl.BlockSpec((1, D), lambda i: (0, 0))
    return pl.pallas_call(
        body, name=name, grid=(S // tm,),
        in_specs=[row, pl.BlockSpec((D, D), lambda i: (0, 0)), row, vec, row],
        out_specs=[row, pl.BlockSpec((1, LANES), lambda i: (0, 0)), vec],
        out_shape=[jax.ShapeDtypeStruct((S, D), F32), jax.ShapeDtypeStruct((1, LANES), F32),
                   jax.ShapeDtypeStruct((1, D), F32)],
        compiler_params=_cp(("arbitrary",)),
    )(merged, w_out, x, fw, tgt)


def branch_merge(o_a, w_attn, y_n, w_ssm, proj, cfg, name, tm=512, tn=512):
    S, D = cfg.S, cfg.D
    tm, tn = min(tm, S), min(tn, D)

    def body(oa_ref, wa_ref, yn_ref, ws_ref, ga_ref, gs_ref, a_ref, s_ref, m_ref):
        a = _dot(oa_ref[...], wa_ref[...], NN)
        sv = _dot(yn_ref[...], ws_ref[...], NN)
        a_ref[...] = a.astype(BF16)
        s_ref[...] = sv.astype(BF16)
        m_ref[...] = (_sigmoid(ga_ref[...]) * a + _sigmoid(gs_ref[...]) * sv).astype(BF16)

    tile = pl.BlockSpec((tm, tn), lambda n, m: (m, n))
    return pl.pallas_call(
        body, name=name, grid=(D // tn, S // tm),
        in_specs=[pl.BlockSpec((tm, cfg.AW), lambda n, m: (m, 0)), pl.BlockSpec((cfg.AW, tn), lambda n, m: (0, n)),
                  pl.BlockSpec((tm, cfg.SI), lambda n, m: (m, 0)), pl.BlockSpec((cfg.SI, tn), lambda n, m: (0, n)),
                  pl.BlockSpec((tm, tn), lambda n, m: (m, _blk(cfg.ga0, tn) + n)),
                  pl.BlockSpec((tm, tn), lambda n, m: (m, _blk(cfg.gs0, tn) + n))],
        out_specs=[tile, tile, tile],
        out_shape=[jax.ShapeDtypeStruct((S, D), BF16)] * 3,
        compiler_params=_cp(("parallel", "parallel")),
    )(o_a, w_attn, y_n, w_ssm, proj, proj)


def merge_bwd(dout, w_out, a_out, s_out, proj, cfg, name, tm=512, tn=1024):
    S, D = cfg.S, cfg.D
    tm, tn = min(tm, S), min(tn, D)

    def body(do_ref, wo_ref, a_ref, s_ref, ga_ref, gs_ref, da_ref, ds_ref, dga_ref, dgs_ref):
        dmv = _dot(do_ref[...].astype(BF16), wo_ref[...], NT)
        sa = _sigmoid(ga_ref[...])
        ss = _sigmoid(gs_ref[...])
        da_ref[...] = (dmv * sa).astype(BF16)
        ds_ref[...] = (dmv * ss).astype(BF16)
        dga_ref[...] = (dmv * a_ref[...] * (sa * (1.0 - sa))).astype(BF16)
        dgs_ref[...] = (dmv * s_ref[...] * (ss * (1.0 - ss))).astype(BF16)

    tile = pl.BlockSpec((tm, tn), lambda n, m: (m, n))
    sh = jax.ShapeDtypeStruct((S, D), BF16)
    return pl.pallas_call(
        body, name=name, grid=(D // tn, S // tm),
        in_specs=[pl.BlockSpec((tm, D), lambda n, m: (m, 0)), pl.BlockSpec((tn, D), lambda n, m: (n, 0)), tile, tile,
                  pl.BlockSpec((tm, tn), lambda n, m: (m, _blk(cfg.ga0, tn) + n)),
                  pl.BlockSpec((tm, tn), lambda n, m: (m, _blk(cfg.gs0, tn) + n))],
        out_specs=[tile] * 4, out_shape=[sh] * 4,
        compiler_params=_cp(("parallel", "parallel")),
    )(dout, w_out, a_out, s_out, proj, proj)


def _attn_rows(base, d):
    return pl.ds(base, ATTN_BLOCK) if d == 1 else pl.ds(base, ATTN_BLOCK, stride=d)


def _attn_units(cfg):
    dmax = max(d for _, d in cfg.patterns)
    units = []
    for p, (window, d) in enumerate(cfg.patterns):
        assert window // d == ATTN_BLOCK and dmax % d == 0
        nsub = dmax // d
        for b in range(nsub):
            for r in range(d):
                base = b * ATTN_BLOCK * d + r
                if b > 0:
                    units.append((p, d, base, (b - 1) * ATTN_BLOCK * d + r, False))
                else:
                    units.append((p, d, base, (nsub - 1) * ATTN_BLOCK * d + r, True))
    return units, ATTN_BLOCK * dmax


def _set_bias_tiles(bias_s, slope, cfg):
    qi = lax.broadcasted_iota(jnp.int32, (ATTN_BLOCK, ATTN_BLOCK), 0)
    ki = lax.broadcasted_iota(jnp.int32, (ATTN_BLOCK, ATTN_BLOCK), 1)
    for p, (_, d) in enumerate(cfg.patterns):
        bias_s[2 * p] = jnp.where(ki >= qi, (-slope) * ((ATTN_BLOCK + qi - ki) * d).astype(F32), NEG)
        bias_s[2 * p + 1] = jnp.where(ki <= qi, (-slope) * ((qi - ki) * d).astype(F32), NEG)


def _unit_scores(q, kcat, bias_s, p, prev_ok, scale):
    s = _dot(q, kcat, NT) * scale + jnp.concatenate([bias_s[2 * p], bias_s[2 * p + 1]], axis=1)
    if prev_ok is not None:
        cur_half = lax.broadcasted_iota(jnp.int32, s.shape, 1) >= ATTN_BLOCK
        s = jnp.where(jnp.logical_or(cur_half, prev_ok), s, NEG)
    return s


def _slope_table(cfg):
    slopes = jnp.asarray([2.0 ** (-8.0 * (h + 1) / cfg.AH) for h in range(cfg.AH)], F32)
    return jnp.broadcast_to(slopes.reshape(cfg.AH, 1, 1), (cfg.AH, 8, LANES))


def attn_fused_fwd(proj, slopes, cfg, name):
    S, E, AH = cfg.S, cfg.E, cfg.AH
    units, SB = _attn_units(cfg)
    assert S % SB == 0
    npat = len(cfg.patterns)
    scale = E ** -0.5

    def spec(off, prev):
        c0 = _blk(off, E)
        if prev:
            return pl.BlockSpec((SB, E), lambda h, i: (jnp.maximum(i - 1, 0), c0 + h))
        return pl.BlockSpec((SB, E), lambda h, i: (i, c0 + h))

    def body(q_ref, kp_ref, kc_ref, vp_ref, vc_ref, z_ref, sl_ref, oa_ref, om_ref, lt_ref, *scr):
        o_s, l_s, bias_s = scr[:npat], scr[npat:2 * npat], scr[2 * npat]
        i = pl.program_id(1)

        @pl.when(i == 0)
        def _():
            _set_bias_tiles(bias_s, sl_ref[0, 0:1, :], cfg)

        for p, d, base, pbase, from_prev in units:
            rows, prows = _attn_rows(base, d), _attn_rows(pbase, d)
            q = q_ref[rows, :].astype(BF16)
            kp = (kp_ref if from_prev else kc_ref)[prows, :].astype(BF16)
            vp = (vp_ref if from_prev else vc_ref)[prows, :].astype(BF16)
            kcat = jnp.concatenate([kp, kc_ref[rows, :].astype(BF16)], axis=0)
            vcat = jnp.concatenate([vp, vc_ref[rows, :].astype(BF16)], axis=0)
            s = _unit_scores(q, kcat, bias_s, p, (i > 0) if from_prev else None, scale)
            m = jnp.max(s, axis=1, keepdims=True)
            pr = jnp.exp(s - m)
            l = jnp.sum(pr, axis=1, keepdims=True)
            o_s[p][rows, :] = _dot(pr.astype(BF16), vcat, NN) * (1.0 / l)
            l_s[p][rows, :] = m + jnp.log(l)
        ls = [l_s[p][...] for p in range(npat)]
        m = functools.reduce(jnp.maximum, ls)
        lt = m + jnp.log(sum(jnp.exp(l_ - m) for l_ in ls))
        lt_ref[...] = lt
        mix = sum(jnp.exp(ls[p] - lt) * o_s[p][...] for p in range(npat))
        om_ref[...] = mix
        z = z_ref[...]
        oa_ref[...] = (mix * (z * _sigmoid(z))).astype(BF16)

    out = pl.BlockSpec((SB, E), lambda h, i: (i, h))
    return pl.pallas_call(
        body, name=name, grid=(AH, S // SB),
        in_specs=[spec(0, False), spec(cfg.k0, True), spec(cfg.k0, False), spec(cfg.v0, True), spec(cfg.v0, False),
                  spec(cfg.za0, False), pl.BlockSpec((1, 8, LANES), lambda h, i: (h, 0, 0))],
        out_specs=[out, out, pl.BlockSpec((SB, 1), lambda h, i: (h * (S // SB) + i, 0))],
        out_shape=[jax.ShapeDtypeStruct((S, cfg.AW), BF16), jax.ShapeDtypeStruct((S, cfg.AW), F32),
                   jax.ShapeDtypeStruct((AH * S, 1), F32)],
        scratch_shapes=[pltpu.VMEM((SB, E), F32)] * npat + [pltpu.VMEM((SB, 1), F32)] * npat +
        [pltpu.VMEM((2 * npat, ATTN_BLOCK, ATTN_BLOCK), F32)],
        compiler_params=_cp(("parallel", "arbitrary")),
    )(proj, proj, proj, proj, proj, proj, slopes)


def attn_fused_bwd(proj, do_a, o_mix, ltot, slopes, cfg, name):
    S, E, AH = cfg.S, cfg.E, cfg.AH
    units, SB = _attn_units(cfg)
    nsb = S // SB
    last = nsb - 1
    scale = E ** -0.5

    def spec(off, prev):
        c0 = _blk(off, E)
        if prev:
            return pl.BlockSpec((SB, E), lambda h, i: (jnp.maximum(i - 1, 0), c0 + h))
        return pl.BlockSpec((SB, E), lambda h, i: (jnp.minimum(i, last), c0 + h))

    cur = pl.BlockSpec((SB, E), lambda h, i: (jnp.minimum(i, last), h))
    prev = pl.BlockSpec((SB, E), lambda h, i: (jnp.maximum(i - 1, 0), h))

    def body(q_ref, kp_ref, kc_ref, vp_ref, vc_ref, z_ref, doa_ref, om_ref, lt_ref, sl_ref,
             dq_ref, dk_ref, dv_ref, dz_ref, dmix_s, dl_s, dq_s, dkp_s, dvp_s, dkc_s, dvc_s, bias_s):
        i = pl.program_id(1)

        @pl.when(i == 0)
        def _():
            dkc_s[...] = jnp.zeros_like(dkc_s)
            dvc_s[...] = jnp.zeros_like(dvc_s)
            _set_bias_tiles(bias_s, sl_ref[0, 0:1, :], cfg)

        @pl.when(i < nsb)
        def _():
            z = z_ref[...]
            s = _sigmoid(z)
            doa = doa_ref[...].astype(F32)
            om = om_ref[...]
            dmix = doa * (z * s)
            dmix_s[...] = dmix
            dz_ref[...] = (doa * om * (s * (1.0 + z * (1.0 - s)))).astype(BF16)
            dl_s[...] = jnp.sum(dmix * om, axis=1, keepdims=True)
            dkp_s[...] = dkc_s[...]
            dvp_s[...] = dvc_s[...]
            dkc_s[...] = jnp.zeros_like(dkc_s)
            dvc_s[...] = jnp.zeros_like(dvc_s)
            dq_s[...] = jnp.zeros_like(dq_s)
            for p, d, base, pbase, from_prev in units:
                rows, prows = _attn_rows(base, d), _attn_rows(pbase, d)
                q = q_ref[rows, :].astype(BF16)
                kc = kc_ref[rows, :].astype(BF16)
                kp = (kp_ref if from_prev else kc_ref)[prows, :].astype(BF16)
                vp = (vp_ref if from_prev else vc_ref)[prows, :].astype(BF16)
                do = dmix_s[rows, :].astype(BF16)
                lt = lt_ref[rows, :]
                dlt = dl_s[rows, :]
                kcat = jnp.concatenate([kp, kc], axis=0)
                vcat = jnp.concatenate([vp, vc_ref[rows, :].astype(BF16)], axis=0)
                pr = jnp.exp(_unit_scores(q, kcat, bias_s, p, (i > 0) if from_prev else None, scale) - lt)
                ds = (pr * (_dot(do, vcat, NT) - dlt) * scale).astype(BF16)
                dq_s[rows, :] += _dot(ds, kcat, NN)
                dkcat = _dot(ds, q, TN)
                dvcat = _dot(pr.astype(BF16), do, TN)
                dk_t, dv_t = (dkp_s, dvp_s) if from_prev else (dkc_s, dvc_s)
                dk_t[prows, :] += dkcat[:ATTN_BLOCK, :]
                dv_t[prows, :] += dvcat[:ATTN_BLOCK, :]
                dkc_s[rows, :] += dkcat[ATTN_BLOCK:, :]
                dvc_s[rows, :] += dvcat[ATTN_BLOCK:, :]
            dq_ref[...] = dq_s[...].astype(BF16)
            dk_ref[...] = dkp_s[...].astype(BF16)
            dv_ref[...] = dvp_s[...].astype(BF16)

        @pl.when(i == nsb)
        def _():
            dk_ref[...] = dkc_s[...].astype(BF16)
            dv_ref[...] = dvc_s[...].astype(BF16)

    sh = jax.ShapeDtypeStruct((S, cfg.AW), BF16)
    acc = pltpu.VMEM((SB, E), F32)
    return pl.pallas_call(
        body, name=name, grid=(AH, nsb + 1),
        in_specs=[spec(0, False), spec(cfg.k0, True), spec(cfg.k0, False), spec(cfg.v0, True), spec(cfg.v0, False),
                  spec(cfg.za0, False), cur, cur,
                  pl.BlockSpec((SB, 1), lambda h, i: (h * nsb + jnp.minimum(i, last), 0)),
                  pl.BlockSpec((1, 8, LANES), lambda h, i: (h, 0, 0))],
        out_specs=[cur, prev, prev, cur], out_shape=[sh] * 4,
        scratch_shapes=[acc, pltpu.VMEM((SB, 1), F32), acc, acc, acc, acc, acc,
                        pltpu.VMEM((2 * len(cfg.patterns), ATTN_BLOCK, ATTN_BLOCK), F32)],
        compiler_params=_cp(("parallel", "arbitrary")),
    )(proj, proj, proj, proj, proj, proj, do_a, o_mix, ltot, slopes)


HALO = 8


def _conv_taps(x_ref, h_ref, kc):
    x = x_ref[...]
    full = jnp.concatenate([jnp.where(pl.program_id(1) == 0, 0.0, h_ref[...]), x], axis=0)
    return [pltpu.roll(full, s, axis=0)[HALO:, :] for s in range(kc - 1, 0, -1)] + [x]


def _conv_pre(taps, w_ref, b_ref):
    pre = b_ref[...] + w_ref[0:1, :] * taps[0]
    for k in range(1, len(taps)):
        pre = pre + w_ref[k:k + 1, :] * taps[k]
    return pre


def conv_fwd(proj, w, b, cfg, name, tm=1024, tc=512):
    S, CD, KC = cfg.S, cfg.CD, cfg.KC
    tc = min(tc, CD)
    c0 = _blk(cfg.xbc0, tc)
    hb = tm // HALO

    def body(x_ref, h_ref, w_ref, b_ref, o_ref):
        pre = _conv_pre(_conv_taps(x_ref, h_ref, KC), w_ref, b_ref)
        o_ref[...] = pre * _sigmoid(pre)

    return pl.pallas_call(
        body, name=name, grid=(CD // tc, S // tm),
        in_specs=[pl.BlockSpec((tm, tc), lambda c, i: (i, c0 + c)),
                  pl.BlockSpec((HALO, tc), lambda c, i: (jnp.maximum(i * hb - 1, 0), c0 + c)),
                  pl.BlockSpec((KC, tc), lambda c, i: (0, c)),
                  pl.BlockSpec((1, tc), lambda c, i: (0, c))],
        out_specs=pl.BlockSpec((tm, tc), lambda c, i: (i, c)),
        out_shape=jax.ShapeDtypeStruct((S, CD), F32),
        compiler_params=_cp(("parallel", "arbitrary")),
    )(proj, proj, w, b)


def conv_bwd(proj, dxc, w, b, cfg, name, c_off, tm=1024, tc=512):
    S, KC = cfg.S, cfg.KC
    CD = dxc.shape[1]
    tc = min(tc, CD)
    c0 = _blk(cfg.xbc0 + c_off, tc)
    w0 = _blk(c_off, tc)
    hb = tm // HALO
    nrb = S // tm
    last_h = S // HALO - 1

    def body(x_ref, hp_ref, hn_ref, d_ref, dn_ref, w_ref, b_ref, o_ref, gw_ref, gb_ref):
        i = pl.program_id(1)
        x = x_ref[...]
        full = jnp.concatenate([jnp.where(i == 0, 0.0, hp_ref[...]), x, hn_ref[...]], axis=0)
        rows = tm + HALO
        taps = [pltpu.roll(full, s_, axis=0)[HALO:, :] for s_ in range(KC - 1, 0, -1)] + [full[HALO:, :]]
        pre = _conv_pre(taps, w_ref, b_ref)
        sg = _sigmoid(pre)
        d_ext = jnp.concatenate([d_ref[...], jnp.where(i == nrb - 1, 0.0, dn_ref[...])], axis=0)
        dpre = d_ext * (sg * (1.0 + pre * (1.0 - sg)))
        own = dpre[:tm, :]
        acc = w_ref[KC - 1:KC, :] * own
        for j in range(1, KC):
            acc = acc + w_ref[KC - 1 - j:KC - j, :] * pltpu.roll(dpre, rows - j, axis=0)[:tm, :]
        o_ref[...] = acc.astype(BF16)
        gb = jnp.sum(own, axis=0, keepdims=True)
        gws = [jnp.sum(own * taps[k][:tm, :], axis=0, keepdims=True) for k in range(KC)]
        gw = jnp.concatenate(gws + [jnp.zeros((8 - KC, tc), F32)], axis=0)

        @pl.when(i == 0)
        def _():
            gw_ref[...] = gw
            gb_ref[...] = gb

        @pl.when(i > 0)
        def _():
            gw_ref[...] += gw
            gb_ref[...] += gb

    return pl.pallas_call(
        body, name=name, grid=(CD // tc, nrb),
        in_specs=[pl.BlockSpec((tm, tc), lambda c, i: (i, c0 + c)),
                  pl.BlockSpec((HALO, tc), lambda c, i: (jnp.maximum(i * hb - 1, 0), c0 + c)),
                  pl.BlockSpec((HALO, tc), lambda c, i: (jnp.minimum((i + 1) * hb, last_h), c0 + c)),
                  pl.BlockSpec((tm, tc), lambda c, i: (i, c)),
                  pl.BlockSpec((HALO, tc), lambda c, i: (jnp.minimum((i + 1) * hb, last_h), c)),
                  pl.BlockSpec((KC, tc), lambda c, i: (0, w0 + c)),
                  pl.BlockSpec((1, tc), lambda c, i: (0, w0 + c))],
        out_specs=[pl.BlockSpec((tm, tc), lambda c, i: (i, c)),
                   pl.BlockSpec((8, tc), lambda c, i: (0, c)),
                   pl.BlockSpec((1, tc), lambda c, i: (0, c))],
        out_shape=[jax.ShapeDtypeStruct((S, CD), BF16), jax.ShapeDtypeStruct((8, CD), F32),
                   jax.ShapeDtypeStruct((1, CD), F32)],
        compiler_params=_cp(("parallel", "arbitrary")),
    )(proj, proj, proj, dxc, dxc, w, b)


def _pad_lanes(v, width=LANES):
    return jnp.pad(v, ((0, 0), (0, width - v.shape[1])))


def ssd_prep(dt_raw, dt_bias, a_log, cfg, name):
    S, L = cfg.S, cfg.L

    def body(x_ref, b_ref, al_ref, dt_ref, ac_ref):
        x = x_ref[...] + b_ref[...]
        dt = jnp.maximum(x, 0.0) + jnp.log(1.0 + jnp.exp(-jnp.abs(x)))
        da = dt * (-jnp.exp(al_ref[...]))
        li = lax.broadcasted_iota(jnp.int32, (L, L), 0)
        si = lax.broadcasted_iota(jnp.int32, (L, L), 1)
        tri = jnp.where(li >= si, 1.0, 0.0).astype(F32)
        dt_ref[...] = dt
        ac_ref[...] = lax.dot_general(tri, da, ((NN), ((), ())), precision=lax.Precision.HIGHEST,
                                      preferred_element_type=F32)

    row = pl.BlockSpec((L, LANES), lambda i: (i, 0))
    vec = pl.BlockSpec((1, LANES), lambda i: (0, 0))
    sh = jax.ShapeDtypeStruct((S, LANES), F32)
    return pl.pallas_call(
        body, name=name, grid=(S // L,), in_specs=[row, vec, vec], out_specs=[row, row], out_shape=[sh, sh],
        compiler_params=_cp(("parallel",)),
    )(dt_raw, dt_bias, a_log)


def _spread(v, n):
    return jnp.broadcast_to(v[:, :, None], v.shape + (n,)).reshape(v.shape[0], v.shape[1] * n)


def _head_selectors(cfg):
    def sel(width):
        head = jnp.arange(LANES)[None, :, None]
        slot = jnp.arange(cfg.SG)[:, None, None] * cfg.HPG + (jnp.arange(cfg.HPG * width) // width)[None, None, :]
        return (head == slot).astype(BF16)
    return sel(cfg.P), sel(LANES)


def _spread_heads(v, sel):
    n = v.shape[0]
    hi = v.astype(BF16)
    r1 = v - hi.astype(F32)
    mid = r1.astype(BF16)
    lo = (r1 - mid.astype(F32)).astype(BF16)
    out = _dot(jnp.concatenate([hi, mid, lo], axis=0), sel, NN)
    return out[:n] + out[n:2 * n] + out[2 * n:]


def _pair_lanes(wide, hpg, p):
    low = lax.broadcasted_iota(jnp.int32, (wide.shape[0], LANES), 1) < p
    return jnp.concatenate([jnp.where(low, wide[:, 2 * jp * LANES:(2 * jp + 1) * LANES],
                                      wide[:, (2 * jp + 1) * LANES:(2 * jp + 2) * LANES])
                            for jp in range(hpg // 2)], axis=1)


def _pair_select(halves, p):
    low = lax.broadcasted_iota(jnp.int32, halves[0].shape, 1) < p
    return jnp.where(low, halves[0], halves[1])


def _head_rows(row, hpg, p):
    return jnp.concatenate([jnp.broadcast_to(row[:, j * LANES:(j + 1) * LANES], (p, LANES)) for j in range(hpg)],
                           axis=0)


def _segment_sums(t, sel):
    r = t.shape[0]
    hi = t.astype(BF16)
    out = _dot(jnp.concatenate([hi, (t - hi.astype(F32)).astype(BF16)], axis=0), sel, NT)
    return out[:r] + out[r:]


def ssd_scan_fwd(xc, dt, acum, act, sel_p, sel_l, cfg, name):
    S, L, P, SN, HPG, SG, SI = cfg.S, cfg.L, cfg.P, cfg.SN, cfg.HPG, cfg.SG, cfg.SI
    nc = S // L
    GW = HPG * P
    bcol, ccol = _blk(SI, SN), _blk(SI + cfg.GN, SN)
    assert nc % CPS == 0

    def body(xs_ref, b_ref, c_ref, dtn_ref, acn_ref, at_ref, sp_ref, sl_ref, y_ref, st_ref, st):
        @pl.when(pl.program_id(1) == 0)
        def _():
            st[...] = jnp.zeros_like(st)

        causal = lax.broadcasted_iota(jnp.int32, (L, L), 0) >= lax.broadcasted_iota(jnp.int32, (L, L), 1)
        for ci in range(CPS):
            rows = slice(ci * L, (ci + 1) * L)
            acn = acn_ref[rows, :]
            dts = _spread_heads(dtn_ref[rows, :], sp_ref[0])
            acs = _spread_heads(acn, sl_ref[0])
            a_p = _pair_lanes(acs, HPG, P)
            s0 = st[...]
            st_ref[ci] = s0.reshape(HPG, P, SN)
            B = b_ref[rows, :].astype(BF16)
            C = c_ref[rows, :].astype(BF16)
            G = _dot(C, B, NT)
            xdt = xs_ref[rows, :] * dts
            xdtb = xdt.astype(BF16)
            ws = jnp.exp(a_p[L - 1:L, :] - a_p)
            yo = jnp.exp(a_p) * _dot(C, s0.astype(BF16), NT)
            yd = []
            for jp in range(HPG // 2):
                x_pair = xdtb[:, jp * LANES:(jp + 1) * LANES]
                halves = []
                for j in (2 * jp, 2 * jp + 1):
                    dm = jnp.where(causal, jnp.exp(acs[:, j * LANES:(j + 1) * LANES] - at_ref[j:j + 1, rows]), 0.0)
                    halves.append(_dot((G * dm).astype(BF16), x_pair, NN))
                yd.append(_pair_select(halves, P))
            y_ref[rows, :] = jnp.concatenate(yd, axis=1) + yo
            st[...] = _head_rows(jnp.exp(acs[L - 1:L, :]), HPG, P) * s0 + _dot((xdt * ws).astype(BF16), B, TN)

    R = CPS * L
    y, states = pl.pallas_call(
        body, name=name, grid=(SG, nc // CPS),
        in_specs=[pl.BlockSpec((R, GW), lambda g, c: (c, g)),
                  pl.BlockSpec((R, SN), lambda g, c: (c, bcol + g)),
                  pl.BlockSpec((R, SN), lambda g, c: (c, ccol + g)),
                  pl.BlockSpec((R, LANES), lambda g, c: (c, 0)),
                  pl.BlockSpec((R, LANES), lambda g, c: (c, 0)),
                  pl.BlockSpec((HPG, R), lambda g, c: (g, c)),
                  pl.BlockSpec((1, LANES, GW), lambda g, c: (g, 0, 0)),
                  pl.BlockSpec((1, LANES, HPG * LANES), lambda g, c: (g, 0, 0))],
        out_specs=[pl.BlockSpec((R, GW), lambda g, c: (c, g)),
                   pl.BlockSpec((CPS, HPG, P, SN), lambda g, c: (c, g, 0, 0))],
        out_shape=[jax.ShapeDtypeStruct((S, SI), F32), jax.ShapeDtypeStruct((nc, cfg.SH, P, SN), F32)],
        scratch_shapes=[pltpu.VMEM((GW, SN), F32)],
        compiler_params=_cp(("parallel", "arbitrary")),
    )(xc, xc, xc, dt, acum, act, sel_p, sel_l)
    return y, states


def ssd_scan_bwd(xc, dt, acum, act, sel_p, sel_l, states, y, dy, dvec, cfg, name, side):
    S, L, P, SN, HPG, SG, SI = cfg.S, cfg.L, cfg.P, cfg.SN, cfg.HPG, cfg.SG, cfg.SI
    nc = S // L
    GW = HPG * P
    bcol, ccol = _blk(SI, SN), _blk(SI + cfg.GN, SN)

    def rc(c):
        return nc // CPS - 1 - c

    def body(xs_ref, b_ref, c_ref, dtn_ref, acn_ref, at_ref, sp_ref, sl_ref, st_ref, y_ref, dy_ref, dk_ref,
             dxs_ref, db_ref, dc_ref, dac_ref, dxsum_ref, dst):
        @pl.when(pl.program_id(1) == 0)
        def _():
            dst[...] = jnp.zeros_like(dst)

        sel = sp_ref[0]
        causal = lax.broadcasted_iota(jnp.int32, (L, L), 0) >= lax.broadcasted_iota(jnp.int32, (L, L), 1)
        low = lax.broadcasted_iota(jnp.int32, (L, LANES), 1) < P
        is_last = lax.broadcasted_iota(jnp.int32, (L, LANES), 0) == L - 1
        ones = jnp.ones((16, SN), BF16)
        for ci in reversed(range(CPS)):
            rows = slice(ci * L, (ci + 1) * L)
            acn = acn_ref[rows, :]
            dts = _spread_heads(dtn_ref[rows, :], sel)
            acs = _spread_heads(acn, sl_ref[0])
            a_p = _pair_lanes(acs, HPG, P)
            B = b_ref[rows, :].astype(BF16)
            C = c_ref[rows, :].astype(BF16)
            G = _dot(C, B, NT)
            xs = xs_ref[rows, :]
            dY = dy_ref[rows, :].astype(F32)
            xdt = xs * dts
            xdtb = xdt.astype(BF16)
            dYb = dY.astype(BF16)
            s0 = st_ref[ci].reshape(GW, SN)
            s0b = s0.astype(BF16)
            ds1 = dst[...]
            ds1b = ds1.astype(BF16)
            ws = jnp.exp(a_p[L - 1:L, :] - a_p)
            dR = (jnp.exp(a_p) * dY).astype(BF16)
            dX2 = ws * _dot(B, ds1b, NT)
            dgsum = jnp.zeros((L, L), F32)
            dX1, yd = [], []
            for jp in range(HPG // 2):
                lanes = slice(jp * LANES, (jp + 1) * LANES)
                x_pair, dy_pair = xdtb[:, lanes], dYb[:, lanes]
                h1, h2 = [], []
                for h, j in enumerate((2 * jp, 2 * jp + 1)):
                    dm = jnp.where(causal, jnp.exp(acs[:, j * LANES:(j + 1) * LANES] - at_ref[j:j + 1, rows]), 0.0)
                    mine = low if h == 0 else jnp.logical_not(low)
                    dgsum = dgsum + _dot(jnp.where(mine, dy_pair, jnp.zeros_like(dy_pair)), x_pair, NT) * dm
                    Mb = (G * dm).astype(BF16)
                    h1.append(_dot(Mb, dy_pair, TN))
                    h2.append(_dot(Mb, x_pair, NN))
                dX1.append(_pair_select(h1, P))
                yd.append(_pair_select(h2, P))
            dX1 = jnp.concatenate(dX1, axis=1)
            dX = dX1 + dX2
            pair = (dYb.astype(F32) - dY) * jnp.concatenate(yd, axis=1) - xdtb.astype(F32) * dX1
            through = _segment_sums(xdt * dX2, sel)
            u = ds1 * s0
            u_hi = u.astype(BF16)
            u_rows = _dot(ones, jnp.concatenate([u_hi, (u - u_hi.astype(F32)).astype(BF16)], axis=0), NT)
            u_rows = u_rows[:, :GW] + u_rows[:, GW:]
            at_end = jnp.exp(acn[L - 1:L, :]) * _segment_sums(u_rows, sel)[0:1, :] + \
                jnp.sum(through, axis=0, keepdims=True)
            dac_ref[0, rows, :] = _segment_sums(dY * y_ref[rows, :] + pair, sel) - through + \
                jnp.where(is_last, at_end, 0.0)
            dxsum_ref[0, rows, :] = _segment_sums(dX * xs, sel)
            dxs_ref[rows, :] = dX * dts + dk_ref[...] * dY
            dst[...] = _head_rows(jnp.exp(acs[L - 1:L, :]), HPG, P) * ds1 + _dot(dR, C, TN)
            dgb = dgsum.astype(BF16)
            dc_ref[rows, :] = _dot(dR, s0b, NN) + _dot(dgb, B, NN)
            db_ref[rows, :] = _dot((xdt * ws).astype(BF16), ds1b, NN) + _dot(dgb, C, TN)

    R = CPS * L
    wide = pl.BlockSpec((R, GW), lambda g, c: (rc(c), g))
    colspec = pl.BlockSpec((1, R, LANES), lambda g, c: (g, rc(c), 0))
    whole = pl.BlockSpec(memory_space=pl.ANY)
    arrs, gathers = side
    grid = (SG, nc // CPS)
    res = pl.pallas_call(
        with_exchange(body, 12, 5, gathers, grid), name=name, grid=grid,
        in_specs=[wide,
                  pl.BlockSpec((R, SN), lambda g, c: (rc(c), bcol + g)),
                  pl.BlockSpec((R, SN), lambda g, c: (rc(c), ccol + g)),
                  pl.BlockSpec((R, LANES), lambda g, c: (rc(c), 0)),
                  pl.BlockSpec((R, LANES), lambda g, c: (rc(c), 0)),
                  pl.BlockSpec((HPG, R), lambda g, c: (g, rc(c))),
                  pl.BlockSpec((1, LANES, GW), lambda g, c: (g, 0, 0)),
                  pl.BlockSpec((1, LANES, HPG * LANES), lambda g, c: (g, 0, 0)),
                  pl.BlockSpec((CPS, HPG, P, SN), lambda g, c: (rc(c), g, 0, 0)),
                  wide, wide,
                  pl.BlockSpec((1, GW), lambda g, c: (0, g))] + [whole] * len(arrs),
        out_specs=[wide,
                   pl.BlockSpec((R, SN), lambda g, c: (rc(c), g)),
                   pl.BlockSpec((R, SN), lambda g, c: (rc(c), g)),
                   colspec, colspec] + [whole] * len(arrs),
        out_shape=[jax.ShapeDtypeStruct((S, SI), F32), jax.ShapeDtypeStruct((S, cfg.GN), F32),
                   jax.ShapeDtypeStruct((S, cfg.GN), F32),
                   jax.ShapeDtypeStruct((SG, S, LANES), F32), jax.ShapeDtypeStruct((SG, S, LANES), F32)] +
        _exchange_shapes(arrs, gathers),
        scratch_shapes=[pltpu.VMEM((GW, SN), F32)] + _exchange_sems(len(arrs)),
        compiler_params=_cp(("arbitrary", "arbitrary")),
    )(xc, xc, xc, dt, acum, act, sel_p, sel_l, states, y, dy, dvec, *arrs)
    return res[:5], res[5:]


def dt_bwd(dac, dxsum, dt_raw, dt, dt_bias, a_log, cfg, name):
    S, L, SG = cfg.S, cfg.L, cfg.SG

    def body(da_ref, dx_ref, x_ref, dt_ref, b_ref, al_ref, o_ref, gb_ref, ga_ref):
        a = -jnp.exp(al_ref[...])
        dtv = dt_ref[...]
        dxs = jnp.sum(dx_ref[...], axis=0)
        upper = jnp.where(lax.broadcasted_iota(jnp.int32, (L, L), 1) >= lax.broadcasted_iota(jnp.int32, (L, L), 0),
                          1.0, 0.0).astype(F32)
        dda = lax.dot_general(upper, jnp.sum(da_ref[...], axis=0), (NN, ((), ())), precision=lax.Precision.HIGHEST,
                              preferred_element_type=F32)
        draw = (dxs + dda * a) * _sigmoid(x_ref[...] + b_ref[...])
        o_ref[...] = draw.astype(BF16)
        gb = jnp.sum(draw, axis=0, keepdims=True)
        ga = jnp.sum(dda * dtv, axis=0, keepdims=True) * a

        @pl.when(pl.program_id(0) == 0)
        def _():
            gb_ref[...] = gb
            ga_ref[...] = ga

        @pl.when(pl.program_id(0) > 0)
        def _():
            gb_ref[...] += gb
            ga_ref[...] += ga

    row = pl.BlockSpec((L, LANES), lambda i: (i, 0))
    vec = pl.BlockSpec((1, LANES), lambda i: (0, 0))
    return pl.pallas_call(
        body, name=name, grid=(S // L,),
        in_specs=[pl.BlockSpec((SG, L, LANES), lambda i: (0, i, 0))] * 2 + [row, row, vec, vec],
        out_specs=[row, vec, vec],
        out_shape=[jax.ShapeDtypeStruct((S, LANES), BF16), jax.ShapeDtypeStruct((1, LANES), F32),
                   jax.ShapeDtypeStruct((1, LANES), F32)],
        compiler_params=_cp(("arbitrary",)),
    )(dac, dxsum, dt_raw, dt, dt_bias, a_log)


def gated_norm_fwd(y, xc, proj, dvec, nw, cfg, name, tm=128):
    S, SI = cfg.S, cfg.SI

    def body(y_ref, xs_ref, z_ref, d_ref, w_ref, o_ref):
        z = z_ref[...]
        yg = (y_ref[...] + d_ref[...] * xs_ref[...]) * (z * _sigmoid(z))
        r = lax.rsqrt(jnp.mean(yg * yg, axis=-1, keepdims=True) + RMS_EPS)
        o_ref[...] = ((yg * r) * w_ref[...]).astype(BF16)

    row = pl.BlockSpec((tm, SI), lambda i: (i, 0))
    vec = pl.BlockSpec((1, SI), lambda i: (0, 0))
    return pl.pallas_call(
        body, name=name, grid=(S // tm,),
        in_specs=[row, row, pl.BlockSpec((tm, SI), lambda i: (i, _blk(cfg.zs0, SI))), vec, vec],
        out_specs=row, out_shape=jax.ShapeDtypeStruct((S, SI), BF16),
        compiler_params=_cp(("parallel",)),
    )(y, xc, proj, dvec, nw)


def gated_norm_bwd(dyn, y, xc, proj, dvec, nw, cfg, name, tm=128):
    S, SI = cfg.S, cfg.SI

    def body(dn_ref, y_ref, xs_ref, z_ref, d_ref, w_ref, dy_ref, dz_ref, gw_ref, gd_ref):
        z = z_ref[...]
        s = _sigmoid(z)
        sz = z * s
        xs = xs_ref[...]
        yf = y_ref[...] + d_ref[...] * xs
        yg = yf * sz
        r = lax.rsqrt(jnp.mean(yg * yg, axis=-1, keepdims=True) + RMS_EPS)
        dn = dn_ref[...].astype(F32)
        g = dn * w_ref[...]
        dyg = r * g - yg * (r * r * r) * jnp.mean(g * yg, axis=-1, keepdims=True)
        dy = dyg * sz
        dy_ref[...] = dy.astype(BF16)
        dz_ref[...] = (dyg * yf * (s * (1.0 + z * (1.0 - s)))).astype(BF16)
        gw = jnp.sum(dn * (yg * r), axis=0, keepdims=True)
        gd = jnp.sum(dy * xs, axis=0, keepdims=True)

        @pl.when(pl.program_id(0) == 0)
        def _():
            gw_ref[...] = gw
            gd_ref[...] = gd

        @pl.when(pl.program_id(0) > 0)
        def _():
            gw_ref[...] += gw
            gd_ref[...] += gd

    row = pl.BlockSpec((tm, SI), lambda i: (i, 0))
    vec = pl.BlockSpec((1, SI), lambda i: (0, 0))
    return pl.pallas_call(
        body, name=name, grid=(S // tm,),
        in_specs=[row, row, row, pl.BlockSpec((tm, SI), lambda i: (i, _blk(cfg.zs0, SI))), vec, vec],
        out_specs=[row, row, vec, vec],
        out_shape=[jax.ShapeDtypeStruct((S, SI), BF16), jax.ShapeDtypeStruct((S, SI), BF16),
                   jax.ShapeDtypeStruct((1, SI), F32), jax.ShapeDtypeStruct((1, SI), F32)],
        compiler_params=_cp(("arbitrary",)),
    )(dyn, y, xc, proj, dvec, nw)


def _shard_columns(cfg, main, dt):
    dt0 = 4 * cfg.AW + cfg.SI + cfg.CD
    ws = cfg.N_IN // N_DEV
    out = []
    for k in range(N_DEV):
        lo, hi, parts = k * ws, (k + 1) * ws, []
        if lo < dt0:
            parts.append((main, lo, min(hi, dt0)))
        if lo < dt0 + cfg.SH and hi > dt0:
            parts.append((dt, max(lo, dt0) - dt0, min(hi, dt0 + cfg.SH) - dt0))
        if hi > dt0 + cfg.SH:
            parts.append((main, max(lo, dt0 + cfg.SH) - cfg.SH, hi - cfg.SH))
        out.append(parts)
    return out


def local_step(cfg, x, tgt, norm_w, conv_w, conv_b, dt_bias, a_log, d_skip, ssm_norm_w, final_norm_w,
               w_main, w_dt, shards, dt0):
    S, D = cfg.S, cfg.D
    slopes = _slope_table(cfg)
    dt_bias_p = _pad_lanes(dt_bias)
    a_log_p = _pad_lanes(a_log)
    dvec = _spread(d_skip, cfg.P)

    hn = rmsnorm_fwd(x, norm_w, "rmsnorm_fwd")
    proj, gathered = matmul(hn, w_main, 'nn', 1024, 2048, 2048, F32, "in_proj", side=(shards, [True] * 3))
    w_attn, w_ssm, w_out = gathered[0].reshape(cfg.AW, D), gathered[1].reshape(cfg.SI, D), gathered[2].reshape(D, D)
    dt_raw = matmul(hn, w_dt, 'nn', 512, 128, 2048, F32, "in_proj_dt")
    o_a, o_mix, ltot = attn_fused_fwd(proj, slopes, cfg, "attn_fwd")
    xc = conv_fwd(proj, conv_w, conv_b, cfg, "conv_fwd")
    dt, acum = ssd_prep(dt_raw, dt_bias_p, a_log_p, cfg, "ssd_prep")
    act = acum[:, :cfg.SH].T
    sel_p, sel_l = _head_selectors(cfg)
    y, states = ssd_scan_fwd(xc, dt, acum, act, sel_p, sel_l, cfg, "ssd_scan_fwd")
    y_n = gated_norm_fwd(y, xc, proj, dvec, ssm_norm_w, cfg, "gated_norm_fwd")
    a_out, s_out, merged = branch_merge(o_a, w_attn, y_n, w_ssm, proj, cfg, "branch_merge")
    dout, loss_p, g_final_w = out_proj_final(merged, w_out, x, final_norm_w.reshape(1, D), tgt, "out_proj_final")

    g_w_out = matmul(merged, dout, 'tn', 1024, 1024, 2048, BF16, "g_w_out")
    da_out, ds_out, dga, dgs = merge_bwd(dout, w_out, a_out, s_out, proj, cfg, "merge_bwd")
    g_w_attn = matmul(o_a, da_out, 'tn', 1024, 1024, 2048, BF16, "g_w_attn")
    g_w_ssm = matmul(y_n, ds_out, 'tn', 1024, 1024, 2048, BF16, "g_w_ssm")
    do_a = matmul(da_out, w_attn, 'nt', 512, 1024, 2048, BF16, "d_o_a")
    dyn = matmul(ds_out, w_ssm, 'nt', 512, 1024, 2048, BF16, "d_y_n")
    dy, dz_s, g_ssm_norm, g_dvec = gated_norm_bwd(dyn, y, xc, proj, dvec, ssm_norm_w, cfg, "gated_norm_bwd")
    sends = [g.reshape((N_DEV, g.shape[0] // N_DEV, D)) for g in (g_w_attn, g_w_ssm, g_w_out)]
    (dxs, dB, dC, dac_g, dxsum_g), (r_attn, r_ssm, r_out) = ssd_scan_bwd(
        xc, dt, acum, act, sel_p, sel_l, states, y, dy, dvec, cfg, "ssd_scan_bwd", side=(sends, [False] * 3))
    ddt_raw, g_dt_bias, g_a_log = dt_bwd(dac_g, dxsum_g, dt_raw, dt, dt_bias_p, a_log_p, cfg, "dt_bwd")
    dxbc, g_cw, g_cb = [], [], []
    for nm, piece, c_off in (("xs", dxs, 0), ("b", dB, cfg.SI), ("c", dC, cfg.SI + cfg.GN)):
        dx, gw, gb = conv_bwd(proj, piece, conv_w, conv_b, cfg, "conv_bwd_" + nm, c_off)
        dxbc.append(dx)
        g_cw.append(gw)
        g_cb.append(gb)
    g_conv_w, g_conv_b = jnp.concatenate(g_cw, axis=1), jnp.concatenate(g_cb, axis=1)
    dq, dk, dv, dz_a = attn_fused_bwd(proj, do_a, o_mix, ltot, slopes, cfg, "attn_bwd")
    dproj = jnp.concatenate([dq, dk, dv, dz_a, dz_s] + dxbc + [dga, dgs], axis=1)
    def slabs(g_main, g_dt):
        return jnp.stack([jnp.concatenate([g[:, lo:hi] for g, lo, hi in parts], axis=1)
                          for parts in _shard_columns(cfg, g_main, g_dt)])

    cut = D // 4
    g_w_dt = matmul(hn, ddt_raw, 'tn', 1024, 128, 2048, BF16, "g_w_dt")
    g_w_top = matmul(hn, dproj, 'tn', 512, 2048, 2048, BF16, "g_w_main_top", a_cols=(0, cut))
    g_w_bot, (r_top,) = matmul(hn, dproj, 'tn', 512, 2048, 2048, BF16, "g_w_main_rest", a_cols=(cut, D - cut),
                               side=([slabs(g_w_top, g_w_dt[:cut])], [False]))
    dhn_a, (r_bot,) = matmul(dproj, w_main, 'nt', 1024, 1024, 2048, F32, "d_hn",
                             side=([slabs(g_w_bot, g_w_dt[cut:])], [False]))
    dhn_b = matmul(ddt_raw, w_dt, 'nt', 512, 1024, 128, F32, "d_hn_dt")
    grad_x, g_norm_w = rmsnorm_bwd(dhn_a, dhn_b, x, norm_w, dout, "rmsnorm_bwd")

    g_d_skip = jnp.sum(g_dvec.reshape(cfg.SH, cfg.P), axis=1).reshape(1, cfg.SH)
    small = dict(norm_w=g_norm_w, conv_b=g_conv_b, dt_bias=g_dt_bias[:, :cfg.SH], a_log=g_a_log[:, :cfg.SH],
                 d_skip=g_d_skip, ssm_norm_w=g_ssm_norm, final_norm_w=g_final_w, conv_w=g_conv_w[:cfg.KC])
    return loss_p, grad_x, small, dict(w_in=[r_top, r_bot], w_attn=[r_attn], w_ssm=[r_ssm], w_out=[r_out])


def _mesh_pos():
    return lax.axis_index("x"), lax.axis_index("y"), lax.axis_index("c")


def _flat(pos):
    return 4 * pos[0] + 2 * pos[1] + pos[2]


def _exchange_shapes(arrs, gathers):
    return [jax.ShapeDtypeStruct(((N_DEV,) + a.shape) if g else a.shape, a.dtype) for a, g in zip(arrs, gathers)]


def _exchange_sems(n):
    return [pltpu.SemaphoreType.DMA((n * (N_DEV - 1),)), pltpu.SemaphoreType.DMA((n * (N_DEV - 1),)),
            pltpu.SemaphoreType.DMA((n,))]


def _exchange_copies(ins, outs, gathers, send_sems, recv_sems, loc_sems):
    pos = _mesh_pos()
    me = _flat(pos)
    starts, waits = [], []
    for a in range(len(ins)):
        mine = ins[a] if gathers[a] else ins[a].at[me]
        loc = pltpu.make_async_copy(mine, outs[a].at[me], loc_sems.at[a])
        starts.append(loc)
        waits.append(loc)
        for k in range(1, N_DEV):
            flip = ((k >> 2) & 1, (k >> 1) & 1, k & 1)
            peer = tuple(1 - p if f else p for p, f in zip(pos, flip))
            pk = _flat(peer)
            src = ins[a] if gathers[a] else ins[a].at[pk]
            sems = dict(send_sem=send_sems.at[a * (N_DEV - 1) + k - 1], recv_sem=recv_sems.at[a * (N_DEV - 1) + k - 1],
                        device_id=peer, device_id_type=pl.DeviceIdType.MESH)
            starts.append(pltpu.make_async_remote_copy(src_ref=src, dst_ref=outs[a].at[me], **sems))
            waits.append(pltpu.make_async_remote_copy(src_ref=src, dst_ref=outs[a].at[pk], **sems))
    return starts, waits


def exchange(arrs, gathers, name):
    n = len(arrs)

    def body(*refs):
        starts, waits = _exchange_copies(refs[:n], refs[n:2 * n], gathers, *refs[2 * n:])
        for cp in starts:
            cp.start()
        for cp in waits:
            cp.wait()

    hbm = pl.BlockSpec(memory_space=pltpu.HBM)
    return pl.pallas_call(
        body, name=name, in_specs=[hbm] * n, out_specs=[hbm] * n, out_shape=_exchange_shapes(arrs, gathers),
        scratch_shapes=_exchange_sems(n),
    )(*arrs)


def gather_two_level(arrs, chunks, name):
    n = len(arrs)

    def body(*refs):
        ins, outs = refs[:n], refs[n:2 * n]
        send_sems, recv_sems, loc_sems = refs[2 * n:]
        x, y, c = _mesh_pos()
        me, sib = (x, y, c), (x, y, 1 - c)
        chips = [(1 - x, y), (x, 1 - y), (1 - x, 1 - y)]
        plan, base = [], 0
        for a in range(n):
            step = arrs[a].shape[0] // chunks[a]
            for q in range(chunks[a]):
                plan.append((a, pl.ds(q * step, step), base))
                base += N_DEV - 1

        def copy(a, rows, sem, block, to, own=False):
            dst = outs[a].at[_flat(block), rows]
            return pltpu.make_async_remote_copy(
                src_ref=ins[a].at[rows] if own else dst, dst_ref=dst, send_sem=send_sems.at[sem],
                recv_sem=recv_sems.at[sem], device_id=to, device_id_type=pl.DeviceIdType.MESH)

        local = [pltpu.make_async_copy(ins[a], outs[a].at[_flat(me)], loc_sems.at[a]) for a in range(n)]
        for cp in local:
            cp.start()
        sent = []
        for a, rows, s in plan:
            sent.append(copy(a, rows, s, me, sib, own=True))
            sent += [copy(a, rows, s + 1 + j, me, (*chip, c), own=True) for j, chip in enumerate(chips)]
        for cp in sent:
            cp.start()
        for a, rows, s in plan:
            for j, chip in enumerate(chips):
                copy(a, rows, s + 1 + j, (*chip, c), me).wait_recv()
                passed = copy(a, rows, s + 4 + j, (*chip, c), sib)
                passed.start()
                sent.append(passed)
        for a, rows, s in plan:
            copy(a, rows, s, sib, me).wait_recv()
            for j, chip in enumerate(chips):
                copy(a, rows, s + 4 + j, (*chip, 1 - c), me).wait_recv()
        for cp in sent:
            cp.wait_send()
        for cp in local:
            cp.wait()

    hbm = pl.BlockSpec(memory_space=pltpu.HBM)
    nsem = (N_DEV - 1) * sum(chunks)
    return pl.pallas_call(
        body, name=name, in_specs=[hbm] * n, out_specs=[hbm] * n, out_shape=_exchange_shapes(arrs, [True] * n),
        scratch_shapes=[pltpu.SemaphoreType.DMA((nsem,)), pltpu.SemaphoreType.DMA((nsem,)),
                        pltpu.SemaphoreType.DMA((n,))],
    )(*arrs)


def with_exchange(body, n_in, n_out, gathers, grid):
    n = len(gathers)

    def wrapped(*refs):
        ins, sends = refs[:n_in], refs[n_in:n_in + n]
        outs, recvs = refs[n_in + n:n_in + n + n_out], refs[n_in + 2 * n + n_out - n:n_in + 2 * n + n_out]
        scratch, sems = refs[n_in + 2 * n + n_out:-3], refs[-3:]
        ids = [pl.program_id(d) for d in range(len(grid))]
        first = functools.reduce(jnp.logical_and, [i == 0 for i in ids])
        last = functools.reduce(jnp.logical_and, [i == g - 1 for i, g in zip(ids, grid)])

        @pl.when(first)
        def _():
            for cp in _exchange_copies(sends, recvs, gathers, *sems)[0]:
                cp.start()

        body(*ins, *outs, *scratch)

        @pl.when(last)
        def _():
            for cp in _exchange_copies(sends, recvs, gathers, *sems)[1]:
                cp.wait()

    return wrapped


def adamw(g_src, w, m, v, summed, name, tr=64):
    R, C = w.shape
    tr = min(tr, R)
    assert R % tr == 0
    parts = g_src if summed else [g_src]
    starts = [sum(p.shape[1] for p in parts[:k]) // tr for k in range(len(parts))] if summed else [0]
    counts = [p.shape[1] // tr for p in parts] if summed else [R // tr]

    def body(*refs):
        g_refs, (w_ref, m_ref, v_ref, g_out, d_out, m_out, v_out) = refs[:len(parts)], refs[len(parts):]
        if summed:
            g = None
            for k, g_ref in enumerate(g_refs):
                gk = g_ref[0].astype(F32)
                for j in range(1, N_DEV):
                    gk = gk + g_ref[j].astype(F32)
                g = gk if g is None else jnp.where(pl.program_id(0) >= starts[k], gk, g)
        else:
            g = g_refs[0][...]
        mn = ADAM_B1 * m_ref[...] + (1.0 - ADAM_B1) * g
        vn = ADAM_B2 * v_ref[...] + (1.0 - ADAM_B2) * (g * g)
        m_hat = mn / (1.0 - ADAM_B1 ** ADAM_STEP)
        v_hat = vn / (1.0 - ADAM_B2 ** ADAM_STEP)
        g_out[...] = g
        d_out[...] = -ADAM_LR * (m_hat / (jnp.sqrt(v_hat) + ADAM_EPS) + ADAM_WD * w_ref[...])
        m_out[...] = mn
        v_out[...] = vn

    row = pl.BlockSpec((tr, C), lambda i: (i, 0))
    if summed:
        gspecs = [pl.BlockSpec((N_DEV, tr, C), lambda i, s=st, n=nb: (0, jnp.clip(i - s, 0, n - 1), 0))
                  for st, nb in zip(starts, counts)]
    else:
        gspecs = [row]
    sh = jax.ShapeDtypeStruct((R, C), F32)
    return pl.pallas_call(
        body, name=name, grid=(R // tr,), in_specs=gspecs + [row, row, row], out_specs=[row] * 4, out_shape=[sh] * 4,
        compiler_params=_cp(("parallel",)),
    )(*parts, w, m, v)


SMALL = ('norm_w', 'conv_b', 'dt_bias', 'a_log', 'd_skip', 'ssm_norm_w', 'final_norm_w')


def _rows(n):
    return -(-n // (8 * LANES)) * 8


def _pack(vals):
    parts = []
    for a in vals:
        f = a.reshape(-1)
        parts.append(jnp.pad(f, (0, _rows(f.size) * LANES - f.size)).reshape(-1, LANES))
    return jnp.concatenate(parts, axis=0)


def _unpack(packed, shapes):
    out, r = [], 0
    for s in shapes:
        n = math.prod(s)
        out.append(packed[r:r + _rows(n)].reshape(-1)[:n].reshape(s))
        r += _rows(n)
    return out


def kernel(x, norm_w, w_in, conv_w, conv_b, dt_bias, a_log, d_skip, ssm_norm_w, w_attn_branch, w_ssm_branch, w_out, final_norm_w, loss_target, m_norm_w, m_w_in, m_conv_w, m_conv_b, m_dt_bias, m_a_log, m_d_skip, m_ssm_norm_w, m_w_attn_branch, m_w_ssm_branch, m_w_out, m_final_norm_w, v_norm_w, v_w_in, v_conv_w, v_conv_b, v_dt_bias, v_a_log, v_d_skip, v_ssm_norm_w, v_w_attn_branch, v_w_ssm_branch, v_w_out, v_final_norm_w):
    cfg = CFG
    D, SH = cfg.D, cfg.SH
    me = _flat(_mesh_pos())
    dt0 = 4 * cfg.AW + cfg.SI + cfg.CD
    ws = w_in.shape[-1]

    g_in, g_cw = gather_two_level([w_in[0].astype(BF16), conv_w[0]], [4, 1], "gather_w_in")
    main_cols, dt_cols = [], []
    for k, parts in enumerate(_shard_columns(cfg, "main", "dt")):
        at = 0
        for which, lo, hi in parts:
            (main_cols if which == "main" else dt_cols).append(g_in[k][:, at:at + hi - lo])
            at += hi - lo
    w_main = jnp.concatenate(main_cols, axis=1)
    w_dt = _pad_lanes(jnp.concatenate(dt_cols, axis=1))
    conv_full = g_cw.transpose(1, 0, 2).reshape(cfg.KC, cfg.CD)
    shards = [w_attn_branch[0].astype(BF16), w_ssm_branch[0].astype(BF16), w_out[0].astype(BF16)]

    loss_p, grad_x, small, recv = local_step(
        cfg, x[0], loss_target[0], norm_w, conv_full, conv_b, dt_bias, a_log, d_skip,
        ssm_norm_w, final_norm_w, w_main, w_dt, shards, dt0)

    upd = {}
    upd['w_in'] = adamw(recv['w_in'], w_in[0], m_w_in[0], v_w_in[0], True, "adamw_w_in")
    upd['w_attn_branch'] = adamw(recv['w_attn'], w_attn_branch[0], m_w_attn_branch[0], v_w_attn_branch[0], True,
                                 "adamw_w_attn")
    upd['w_ssm_branch'] = adamw(recv['w_ssm'], w_ssm_branch[0], m_w_ssm_branch[0], v_w_ssm_branch[0], True,
                                "adamw_w_ssm")
    upd['w_out'] = adamw(recv['w_out'], w_out[0], m_w_out[0], v_w_out[0], True, "adamw_w_out")

    extra = [jnp.zeros((cfg.KC, cfg.CD), F32), jnp.zeros((1, 1), F32)]
    shapes = [small[n].shape for n in SMALL] + [e.shape for e in extra]
    part = _pack([small[n] for n in SMALL] + [small['conv_w'], loss_p[:, :1]])
    gathered, = exchange([part], [True], "gather_small")
    given = dict(norm_w=(norm_w, m_norm_w, v_norm_w), conv_b=(conv_b, m_conv_b, v_conv_b),
                 dt_bias=(dt_bias, m_dt_bias, v_dt_bias), a_log=(a_log, m_a_log, v_a_log),
                 d_skip=(d_skip, m_d_skip, v_d_skip), ssm_norm_w=(ssm_norm_w, m_ssm_norm_w, v_ssm_norm_w),
                 final_norm_w=(final_norm_w, m_final_norm_w, v_final_norm_w))
    packed = [_pack([given[n][t] for n in SMALL] + extra) for t in range(3)]
    outs = adamw([gathered], *packed, True, "adamw_small", tr=part.shape[0])
    unpacked = [_unpack(o, shapes) for o in outs]
    for i, n in enumerate(SMALL):
        upd[n] = [u[i].reshape(given[n][0].shape) for u in unpacked]
    loss = unpacked[0][-1].reshape(())
    cw = conv_w.shape[-1]
    g_cw_mine = lax.dynamic_slice_in_dim(unpacked[0][-2], me * cw, cw, axis=1)
    upd['conv_w'] = adamw(g_cw_mine.reshape(-1, LANES), conv_w.reshape(-1, LANES), m_conv_w.reshape(-1, LANES),
                          v_conv_w.reshape(-1, LANES), False, "adamw_conv_w")

    order = ['norm_w', 'w_in', 'conv_w', 'conv_b', 'dt_bias', 'a_log', 'd_skip', 'ssm_norm_w', 'w_attn_branch',
             'w_ssm_branch', 'w_out', 'final_norm_w']
    like = dict(norm_w=norm_w, w_in=w_in, conv_w=conv_w, conv_b=conv_b, dt_bias=dt_bias, a_log=a_log, d_skip=d_skip,
                ssm_norm_w=ssm_norm_w, w_attn_branch=w_attn_branch, w_ssm_branch=w_ssm_branch, w_out=w_out,
                final_norm_w=final_norm_w)
    result = [loss, grad_x[None]]
    for t in range(4):
        result += [upd[n][t].reshape(like[n].shape) for n in order]
    return tuple(result)
```

```python
import functools
import math
from typing import NamedTuple

import jax
import jax.numpy as jnp
from jax import lax
from jax.experimental import pallas as pl
from jax.experimental.pallas import tpu as pltpu

F32 = jnp.float32
BF16 = jnp.bfloat16
RMS_EPS = 1e-6
NEG = -1e30
N_DEV = 8
CPS = 8
LANES = 128
ATTN_BLOCK = 128
ADAM_LR, ADAM_B1, ADAM_B2, ADAM_EPS, ADAM_WD, ADAM_STEP = 0.001, 0.9, 0.999, 1e-08, 0.01, 10
VMEM_LIMIT = 56 * 1024 * 1024


class Cfg(NamedTuple):
    D: int = 2048
    S: int = 8192
    AH: int = 16
    E: int = 128
    patterns: tuple = ((128, 1), (512, 4), (2048, 16))
    SI: int = 4096
    P: int = 64
    SG: int = 8
    SN: int = 128
    KC: int = 4
    L: int = 128

    @property
    def AW(self): return self.AH * self.E
    @property
    def SH(self): return self.SI // self.P
    @property
    def HPG(self): return self.SH // self.SG
    @property
    def GN(self): return self.SG * self.SN
    @property
    def CD(self): return self.SI + 2 * self.GN
    @property
    def k0(self): return self.AW
    @property
    def v0(self): return 2 * self.AW
    @property
    def za0(self): return 3 * self.AW
    @property
    def zs0(self): return 4 * self.AW
    @property
    def xbc0(self): return 4 * self.AW + self.SI
    @property
    def ga0(self): return self.xbc0 + self.CD
    @property
    def gs0(self): return self.ga0 + self.D
    @property
    def NP(self): return self.gs0 + self.D
    @property
    def N_IN(self): return self.NP + self.SH


CFG = Cfg()


def _cp(sem=None, vmem=VMEM_LIMIT):
    return pltpu.CompilerParams(dimension_semantics=sem, vmem_limit_bytes=vmem)


def _sigmoid(z):
    return 1.0 / (1.0 + jnp.exp(-z))


def _dot(a, b, dims):
    return lax.dot_general(a, b, (dims, ((), ())), preferred_element_type=F32)


NN = ((1,), (0,))
NT = ((1,), (1,))
TN = ((0,), (0,))


def _blk(off, width):
    assert off % width == 0, (off, width)
    return off // width


def matmul(a, b, mode, tm, tn, tk, out_dtype, name, side=None, a_cols=None):
    m0 = 0
    if mode == 'nn':
        (M, K), (_, N) = a.shape, b.shape
    elif mode == 'nt':
        (M, K), (N, _) = a.shape, b.shape
    else:
        (K, M), (_, N) = a.shape, b.shape
        if a_cols is not None:
            m0, M = a_cols
    tm, tn, tk = math.gcd(min(tm, M), m0), min(tn, N), min(tk, K)
    assert M % tm == 0 and N % tn == 0 and K % tk == 0, (M, N, K, tm, tn, tk)
    nk = K // tk
    dims = {'nn': NN, 'nt': NT, 'tn': TN}[mode]

    def body(a_ref, b_ref, o_ref, *acc):
        part = _dot(a_ref[...].astype(BF16), b_ref[...].astype(BF16), dims)
        if nk == 1:
            o_ref[...] = part.astype(out_dtype)
        else:
            acc_ref, = acc
            k = pl.program_id(2)

            @pl.when(k == 0)
            def _():
                acc_ref[...] = part

            @pl.when(k > 0)
            def _():
                acc_ref[...] += part

            @pl.when(k == nk - 1)
            def _():
                o_ref[...] = acc_ref[...].astype(out_dtype)

    if mode == 'tn':
        a_spec = pl.BlockSpec((tk, tm), lambda n, m, k: (k, _blk(m0, tm) + m))
    else:
        a_spec = pl.BlockSpec((tm, tk), lambda n, m, k: (m, k))
    if mode == 'nt':
        b_spec = pl.BlockSpec((tn, tk), lambda n, m, k: (n, k))
    else:
        b_spec = pl.BlockSpec((tk, tn), lambda n, m, k: (k, n))
    grid = (N // tn, M // tm, nk)
    o_spec = pl.BlockSpec((tm, tn), lambda n, m, k: (m, n))
    o_shape = jax.ShapeDtypeStruct((M, N), out_dtype)
    acc = [] if nk == 1 else [pltpu.VMEM((tm, tn), F32)]
    if side is None:
        return pl.pallas_call(
            body, name=name, grid=grid, in_specs=[a_spec, b_spec], out_specs=o_spec, out_shape=o_shape,
            scratch_shapes=acc, compiler_params=_cp(("parallel", "parallel", "arbitrary")),
        )(a, b)
    arrs, gathers = side
    whole = pl.BlockSpec(memory_space=pl.ANY)
    res = pl.pallas_call(
        with_exchange(body, 2, 1, gathers, grid), name=name, grid=grid,
        in_specs=[a_spec, b_spec] + [whole] * len(arrs), out_specs=[o_spec] + [whole] * len(arrs),
        out_shape=[o_shape] + _exchange_shapes(arrs, gathers),
        scratch_shapes=acc + _exchange_sems(len(arrs)),
        compiler_params=_cp(("arbitrary", "arbitrary", "arbitrary")),
    )(a, b, *arrs)
    return res[0], res[1:]


def rmsnorm_fwd(x, w, name, tm=256):
    S, D = x.shape

    def body(x_ref, w_ref, o_ref):
        xv = x_ref[...]
        r = lax.rsqrt(jnp.mean(xv * xv, axis=-1, keepdims=True) + RMS_EPS)
        o_ref[...] = ((xv * r) * w_ref[...]).astype(BF16)

    return pl.pallas_call(
        body, name=name, grid=(S // tm,),
        in_specs=[pl.BlockSpec((tm, D), lambda i: (i, 0)), pl.BlockSpec((1, D), lambda i: (0, 0))],
        out_specs=pl.BlockSpec((tm, D), lambda i: (i, 0)),
        out_shape=jax.ShapeDtypeStruct((S, D), BF16),
        compiler_params=_cp(("parallel",)),
    )(x, w)


def rmsnorm_bwd(dh_a, ddt, w_dt, x, w, dout, name, tm=128):
    S, D = x.shape

    def body(da_ref, dd_ref, wd_ref, x_ref, w_ref, do_ref, gx_ref, gw_ref):
        xv = x_ref[...]
        dh = da_ref[...] + _dot(dd_ref[...], wd_ref[...], NT)
        r = lax.rsqrt(jnp.mean(xv * xv, axis=-1, keepdims=True) + RMS_EPS)
        g = dh * w_ref[...]
        dx = r * g - xv * (r * r * r) * jnp.mean(g * xv, axis=-1, keepdims=True)
        gx_ref[...] = do_ref[...] + dx
        gw = jnp.sum(dh * (xv * r), axis=0, keepdims=True)

        @pl.when(pl.program_id(0) == 0)
        def _():
            gw_ref[...] = gw

        @pl.when(pl.program_id(0) > 0)
        def _():
            gw_ref[...] += gw

    row = pl.BlockSpec((tm, D), lambda i: (i, 0))
    vec = pl.BlockSpec((1, D), lambda i: (0, 0))
    return pl.pallas_call(
        body, name=name, grid=(S // tm,),
        in_specs=[row, pl.BlockSpec((tm, LANES), lambda i: (i, 0)), pl.BlockSpec((D, LANES), lambda i: (0, 0)), row,
                  vec, row],
        out_specs=[row, vec],
        out_shape=[jax.ShapeDtypeStruct((S, D), F32), jax.ShapeDtypeStruct((1, D), F32)],
        compiler_params=_cp(("arbitrary",)),
    )(dh_a, ddt, w_dt, x, w, dout)


def out_proj_final(merged, w_out, x, fw, tgt, name, tm=256):
    S, D = x.shape

    def body(m_ref, wo_ref, x_ref, w_ref, t_ref, do_ref, loss_ref, gw_ref):
        out = x_ref[...] + _dot(m_ref[...], wo_ref[...], NN)
        w = w_ref[...]
        r = lax.rsqrt(jnp.mean(out * out, axis=-1, keepdims=True) + RMS_EPS)
        yn = out * r
        err = yn * w - t_ref[...]
        lrow = 0.5 * jnp.mean(err * err, axis=-1, keepdims=True)
        lsum = jnp.zeros((1, LANES), F32) + jnp.sum(lrow, axis=0, keepdims=True)
        dfin = err * (1.0 / D)
        g = dfin * w
        do_ref[...] = r * g - out * (r * r * r) * jnp.mean(g * out, axis=-1, keepdims=True)
        gw = jnp.sum(dfin * yn, axis=0, keepdims=True)

        @pl.when(pl.program_id(0) == 0)
        def _():
            gw_ref[...] = gw
            loss_ref[...] = lsum

        @pl.when(pl.program_id(0) > 0)
        def _():
            gw_ref[...] += gw
            loss_ref[...] += lsum

    row = pl.BlockSpec((tm, D), lambda i: (i, 0))
    vec = pl.BlockSpec((1, D), lambda i: (0, 0))
    return pl.pallas_call(
        body, name=name, grid=(S // tm,),
        in_specs=[row, pl.BlockSpec((D, D), lambda i: (0, 0)), row, vec, row],
        out_specs=[row, pl.BlockSpec((1, LANES), lambda i: (0, 0)), vec],
        out_shape=[jax.ShapeDtypeStruct((S, D), F32), jax.ShapeDtypeStruct((1, LANES), F32),
                   jax.ShapeDtypeStruct((1, D), F32)],
        compiler_params=_cp(("arbitrary",)),
    )(merged, w_out, x, fw, tgt)


def branch_merge(o_a, w_attn, y_n, w_ssm, proj, cfg, name, tm=512, tn=512):
    S, D = cfg.S, cfg.D
    tm, tn = min(tm, S), min(tn, D)

    def body(oa_ref, wa_ref, yn_ref, ws_ref, ga_ref, gs_ref, a_ref, s_ref, m_ref):
        a = _dot(oa_ref[...], wa_ref[...], NN)
        sv = _dot(yn_ref[...], ws_ref[...], NN)
        a_ref[...] = a.astype(BF16)
        s_ref[...] = sv.astype(BF16)
        m_ref[...] = (_sigmoid(ga_ref[...]) * a + _sigmoid(gs_ref[...]) * sv).astype(BF16)

    tile = pl.BlockSpec((tm, tn), lambda n, m: (m, n))
    return pl.pallas_call(
        body, name=name, grid=(D // tn, S // tm),
        in_specs=[pl.BlockSpec((tm, cfg.AW), lambda n, m: (m, 0)), pl.BlockSpec((cfg.AW, tn), lambda n, m: (0, n)),
                  pl.BlockSpec((tm, cfg.SI), lambda n, m: (m, 0)), pl.BlockSpec((cfg.SI, tn), lambda n, m: (0, n)),
                  pl.BlockSpec((tm, tn), lambda n, m: (m, _blk(cfg.ga0, tn) + n)),
                  pl.BlockSpec((tm, tn), lambda n, m: (m, _blk(cfg.gs0, tn) + n))],
        out_specs=[tile, tile, tile],
        out_shape=[jax.ShapeDtypeStruct((S, D), BF16)] * 3,
        compiler_params=_cp(("parallel", "parallel")),
    )(o_a, w_attn, y_n, w_ssm, proj, proj)


def merge_bwd(dout, w_out, a_out, s_out, proj, cfg, name, tm=512, tn=1024):
    S, D = cfg.S, cfg.D
    tm, tn = min(tm, S), min(tn, D)

    def body(do_ref, wo_ref, a_ref, s_ref, ga_ref, gs_ref, da_ref, ds_ref, dga_ref, dgs_ref):
        dmv = _dot(do_ref[...].astype(BF16), wo_ref[...], NT)
        sa = _sigmoid(ga_ref[...])
        ss = _sigmoid(gs_ref[...])
        da_ref[...] = (dmv * sa).astype(BF16)
        ds_ref[...] = (dmv * ss).astype(BF16)
        dga_ref[...] = (dmv * a_ref[...] * (sa * (1.0 - sa))).astype(BF16)
        dgs_ref[...] = (dmv * s_ref[...] * (ss * (1.0 - ss))).astype(BF16)

    tile = pl.BlockSpec((tm, tn), lambda n, m: (m, n))
    sh = jax.ShapeDtypeStruct((S, D), BF16)
    return pl.pallas_call(
        body, name=name, grid=(D // tn, S // tm),
        in_specs=[pl.BlockSpec((tm, D), lambda n, m: (m, 0)), pl.BlockSpec((tn, D), lambda n, m: (n, 0)), tile, tile,
                  pl.BlockSpec((tm, tn), lambda n, m: (m, _blk(cfg.ga0, tn) + n)),
                  pl.BlockSpec((tm, tn), lambda n, m: (m, _blk(cfg.gs0, tn) + n))],
        out_specs=[tile] * 4, out_shape=[sh] * 4,
        compiler_params=_cp(("parallel", "parallel")),
    )(dout, w_out, a_out, s_out, proj, proj)


def _attn_rows(base, d):
    return pl.ds(base, ATTN_BLOCK) if d == 1 else pl.ds(base, ATTN_BLOCK, stride=d)


def _attn_units(cfg):
    dmax = max(d for _, d in cfg.patterns)
    units = []
    for p, (window, d) in enumerate(cfg.patterns):
        assert window // d == ATTN_BLOCK and dmax % d == 0
        nsub = dmax // d
        for b in range(nsub):
            for r in range(d):
                base = b * ATTN_BLOCK * d + r
                if b > 0:
                    units.append((p, d, base, (b - 1) * ATTN_BLOCK * d + r, False))
                else:
                    units.append((p, d, base, (nsub - 1) * ATTN_BLOCK * d + r, True))
    return units, ATTN_BLOCK * dmax


def _set_bias_tiles(bias_s, slope, cfg):
    qi = lax.broadcasted_iota(jnp.int32, (ATTN_BLOCK, ATTN_BLOCK), 0)
    ki = lax.broadcasted_iota(jnp.int32, (ATTN_BLOCK, ATTN_BLOCK), 1)
    for p, (_, d) in enumerate(cfg.patterns):
        bias_s[2 * p] = jnp.where(ki >= qi, (-slope) * ((ATTN_BLOCK + qi - ki) * d).astype(F32), NEG)
        bias_s[2 * p + 1] = jnp.where(ki <= qi, (-slope) * ((qi - ki) * d).astype(F32), NEG)


def _unit_scores(q, kcat, bias_s, p, prev_ok, scale):
    s = _dot(q, kcat, NT) * scale + jnp.concatenate([bias_s[2 * p], bias_s[2 * p + 1]], axis=1)
    if prev_ok is not None:
        cur_half = lax.broadcasted_iota(jnp.int32, s.shape, 1) >= ATTN_BLOCK
        s = jnp.where(jnp.logical_or(cur_half, prev_ok), s, NEG)
    return s


def _slope_table(cfg):
    slopes = jnp.asarray([2.0 ** (-8.0 * (h + 1) / cfg.AH) for h in range(cfg.AH)], F32)
    return jnp.broadcast_to(slopes.reshape(cfg.AH, 1, 1), (cfg.AH, 8, LANES))


def attn_fused_fwd(proj, slopes, cfg, name):
    S, E, AH = cfg.S, cfg.E, cfg.AH
    units, SB = _attn_units(cfg)
    assert S % SB == 0
    npat = len(cfg.patterns)
    scale = E ** -0.5

    def spec(off, prev):
        c0 = _blk(off, E)
        if prev:
            return pl.BlockSpec((SB, E), lambda h, i: (jnp.maximum(i - 1, 0), c0 + h))
        return pl.BlockSpec((SB, E), lambda h, i: (i, c0 + h))

    def body(q_ref, kp_ref, kc_ref, vp_ref, vc_ref, z_ref, sl_ref, oa_ref, om_ref, lt_ref, *scr):
        o_s, l_s, bias_s = scr[:npat], scr[npat:2 * npat], scr[2 * npat]
        i = pl.program_id(1)

        @pl.when(i == 0)
        def _():
            _set_bias_tiles(bias_s, sl_ref[0, 0:1, :], cfg)

        for p, d, base, pbase, from_prev in units:
            rows, prows = _attn_rows(base, d), _attn_rows(pbase, d)
            q = q_ref[rows, :].astype(BF16)
            kp = (kp_ref if from_prev else kc_ref)[prows, :].astype(BF16)
            vp = (vp_ref if from_prev else vc_ref)[prows, :].astype(BF16)
            kcat = jnp.concatenate([kp, kc_ref[rows, :].astype(BF16)], axis=0)
            vcat = jnp.concatenate([vp, vc_ref[rows, :].astype(BF16)], axis=0)
            s = _unit_scores(q, kcat, bias_s, p, (i > 0) if from_prev else None, scale)
            m = jnp.max(s, axis=1, keepdims=True)
            pr = jnp.exp(s - m)
            l = jnp.sum(pr, axis=1, keepdims=True)
            o_s[p][rows, :] = _dot(pr.astype(BF16), vcat, NN) * (1.0 / l)
            l_s[p][rows, :] = m + jnp.log(l)
        ls = [l_s[p][...] for p in range(npat)]
        m = functools.reduce(jnp.maximum, ls)
        lt = m + jnp.log(sum(jnp.exp(l_ - m) for l_ in ls))
        lt_ref[...] = lt
        mix = sum(jnp.exp(ls[p] - lt) * o_s[p][...] for p in range(npat))
        om_ref[...] = mix
        z = z_ref[...]
        oa_ref[...] = (mix * (z * _sigmoid(z))).astype(BF16)

    out = pl.BlockSpec((SB, E), lambda h, i: (i, h))
    return pl.pallas_call(
        body, name=name, grid=(AH, S // SB),
        in_specs=[spec(0, False), spec(cfg.k0, True), spec(cfg.k0, False), spec(cfg.v0, True), spec(cfg.v0, False),
                  spec(cfg.za0, False), pl.BlockSpec((1, 8, LANES), lambda h, i: (h, 0, 0))],
        out_specs=[out, out, pl.BlockSpec((SB, 1), lambda h, i: (h * (S // SB) + i, 0))],
        out_shape=[jax.ShapeDtypeStruct((S, cfg.AW), BF16), jax.ShapeDtypeStruct((S, cfg.AW), F32),
                   jax.ShapeDtypeStruct((AH * S, 1), F32)],
        scratch_shapes=[pltpu.VMEM((SB, E), F32)] * npat + [pltpu.VMEM((SB, 1), F32)] * npat +
        [pltpu.VMEM((2 * npat, ATTN_BLOCK, ATTN_BLOCK), F32)],
        compiler_params=_cp(("parallel", "arbitrary")),
    )(proj, proj, proj, proj, proj, proj, slopes)


def attn_fused_bwd(proj, do_a, o_mix, ltot, slopes, cfg, name):
    S, E, AH = cfg.S, cfg.E, cfg.AH
    units, SB = _attn_units(cfg)
    nsb = S // SB
    last = nsb - 1
    scale = E ** -0.5

    def spec(off, prev):
        c0 = _blk(off, E)
        if prev:
            return pl.BlockSpec((SB, E), lambda h, i: (jnp.maximum(i - 1, 0), c0 + h))
        return pl.BlockSpec((SB, E), lambda h, i: (jnp.minimum(i, last), c0 + h))

    cur = pl.BlockSpec((SB, E), lambda h, i: (jnp.minimum(i, last), h))
    prev = pl.BlockSpec((SB, E), lambda h, i: (jnp.maximum(i - 1, 0), h))

    def body(q_ref, kp_ref, kc_ref, vp_ref, vc_ref, z_ref, doa_ref, om_ref, lt_ref, sl_ref,
             dq_ref, dk_ref, dv_ref, dz_ref, dmix_s, dl_s, dq_s, dkp_s, dvp_s, dkc_s, dvc_s, bias_s):
        i = pl.program_id(1)

        @pl.when(i == 0)
        def _():
            dkc_s[...] = jnp.zeros_like(dkc_s)
            dvc_s[...] = jnp.zeros_like(dvc_s)
            _set_bias_tiles(bias_s, sl_ref[0, 0:1, :], cfg)

        @pl.when(i < nsb)
        def _():
            z = z_ref[...]
            s = _sigmoid(z)
            doa = doa_ref[...].astype(F32)
            om = om_ref[...]
            dmix = doa * (z * s)
            dmix_s[...] = dmix
            dz_ref[...] = (doa * om * (s * (1.0 + z * (1.0 - s)))).astype(BF16)
            dl_s[...] = jnp.sum(dmix * om, axis=1, keepdims=True)
            dkp_s[...] = dkc_s[...]
            dvp_s[...] = dvc_s[...]
            dkc_s[...] = jnp.zeros_like(dkc_s)
            dvc_s[...] = jnp.zeros_like(dvc_s)
            dq_s[...] = jnp.zeros_like(dq_s)
            for p, d, base, pbase, from_prev in units:
                rows, prows = _attn_rows(base, d), _attn_rows(pbase, d)
                q = q_ref[rows, :].astype(BF16)
                kc = kc_ref[rows, :].astype(BF16)
                kp = (kp_ref if from_prev else kc_ref)[prows, :].astype(BF16)
                vp = (vp_ref if from_prev else vc_ref)[prows, :].astype(BF16)
                do = dmix_s[rows, :].astype(BF16)
                lt = lt_ref[rows, :]
                dlt = dl_s[rows, :]
                kcat = jnp.concatenate([kp, kc], axis=0)
                vcat = jnp.concatenate([vp, vc_ref[rows, :].astype(BF16)], axis=0)
                pr = jnp.exp(_unit_scores(q, kcat, bias_s, p, (i > 0) if from_prev else None, scale) - lt)
                ds = (pr * (_dot(do, vcat, NT) - dlt) * scale).astype(BF16)
                dq_s[rows, :] += _dot(ds, kcat, NN)
                dkcat = _dot(ds, q, TN)
                dvcat = _dot(pr.astype(BF16), do, TN)
                dk_t, dv_t = (dkp_s, dvp_s) if from_prev else (dkc_s, dvc_s)
                dk_t[prows, :] += dkcat[:ATTN_BLOCK, :]
                dv_t[prows, :] += dvcat[:ATTN_BLOCK, :]
                dkc_s[rows, :] += dkcat[ATTN_BLOCK:, :]
                dvc_s[rows, :] += dvcat[ATTN_BLOCK:, :]
            dq_ref[...] = dq_s[...].astype(BF16)
            dk_ref[...] = dkp_s[...].astype(BF16)
            dv_ref[...] = dvp_s[...].astype(BF16)

        @pl.when(i == nsb)
        def _():
            dk_ref[...] = dkc_s[...].astype(BF16)
            dv_ref[...] = dvc_s[...].astype(BF16)

    sh = jax.ShapeDtypeStruct((S, cfg.AW), BF16)
    acc = pltpu.VMEM((SB, E), F32)
    return pl.pallas_call(
        body, name=name, grid=(AH, nsb + 1),
        in_specs=[spec(0, False), spec(cfg.k0, True), spec(cfg.k0, False), spec(cfg.v0, True), spec(cfg.v0, False),
                  spec(cfg.za0, False), cur, cur,
                  pl.BlockSpec((SB, 1), lambda h, i: (h * nsb + jnp.minimum(i, last), 0)),
                  pl.BlockSpec((1, 8, LANES), lambda h, i: (h, 0, 0))],
        out_specs=[cur, prev, prev, cur], out_shape=[sh] * 4,
        scratch_shapes=[acc, pltpu.VMEM((SB, 1), F32), acc, acc, acc, acc, acc,
                        pltpu.VMEM((2 * len(cfg.patterns), ATTN_BLOCK, ATTN_BLOCK), F32)],
        compiler_params=_cp(("parallel", "arbitrary")),
    )(proj, proj, proj, proj, proj, proj, do_a, o_mix, ltot, slopes)


HALO = 8


def _conv_taps(x_ref, h_ref, kc):
    x = x_ref[...]
    full = jnp.concatenate([jnp.where(pl.program_id(1) == 0, 0.0, h_ref[...]), x], axis=0)
    return [pltpu.roll(full, s, axis=0)[HALO:, :] for s in range(kc - 1, 0, -1)] + [x]


def _conv_pre(taps, w_ref, b_ref):
    pre = b_ref[...] + w_ref[0:1, :] * taps[0]
    for k in range(1, len(taps)):
        pre = pre + w_ref[k:k + 1, :] * taps[k]
    return pre


def conv_fwd(proj, w, b, cfg, name, tm=1024, tc=512):
    S, CD, KC = cfg.S, cfg.CD, cfg.KC
    tc = min(tc, CD)
    c0 = _blk(cfg.xbc0, tc)
    hb = tm // HALO

    def body(x_ref, h_ref, w_ref, b_ref, o_ref):
        pre = _conv_pre(_conv_taps(x_ref, h_ref, KC), w_ref, b_ref)
        o_ref[...] = pre * _sigmoid(pre)

    return pl.pallas_call(
        body, name=name, grid=(CD // tc, S // tm),
        in_specs=[pl.BlockSpec((tm, tc), lambda c, i: (i, c0 + c)),
                  pl.BlockSpec((HALO, tc), lambda c, i: (jnp.maximum(i * hb - 1, 0), c0 + c)),
                  pl.BlockSpec((KC, tc), lambda c, i: (0, c)),
                  pl.BlockSpec((1, tc), lambda c, i: (0, c))],
        out_specs=pl.BlockSpec((tm, tc), lambda c, i: (i, c)),
        out_shape=jax.ShapeDtypeStruct((S, CD), F32),
        compiler_params=_cp(("parallel", "arbitrary")),
    )(proj, proj, w, b)


def conv_bwd(proj, dxc, w, b, cfg, name, c_off, tm=1024, tc=512):
    S, KC = cfg.S, cfg.KC
    CD = dxc.shape[1]
    tc = min(tc, CD)
    c0 = _blk(cfg.xbc0 + c_off, tc)
    w0 = _blk(c_off, tc)
    hb = tm // HALO
    nrb = S // tm
    last_h = S // HALO - 1

    def body(x_ref, hp_ref, hn_ref, d_ref, dn_ref, w_ref, b_ref, o_ref, gw_ref, gb_ref):
        i = pl.program_id(1)
        x = x_ref[...]
        full = jnp.concatenate([jnp.where(i == 0, 0.0, hp_ref[...]), x, hn_ref[...]], axis=0)
        rows = tm + HALO
        taps = [pltpu.roll(full, s_, axis=0)[HALO:, :] for s_ in range(KC - 1, 0, -1)] + [full[HALO:, :]]
        pre = _conv_pre(taps, w_ref, b_ref)
        sg = _sigmoid(pre)
        d_ext = jnp.concatenate([d_ref[...], jnp.where(i == nrb - 1, 0.0, dn_ref[...])], axis=0)
        dpre = d_ext * (sg * (1.0 + pre * (1.0 - sg)))
        own = dpre[:tm, :]
        acc = w_ref[KC - 1:KC, :] * own
        for j in range(1, KC):
            acc = acc + w_ref[KC - 1 - j:KC - j, :] * pltpu.roll(dpre, rows - j, axis=0)[:tm, :]
        o_ref[...] = acc.astype(BF16)
        gb = jnp.sum(own, axis=0, keepdims=True)
        gws = [jnp.sum(own * taps[k][:tm, :], axis=0, keepdims=True) for k in range(KC)]
        gw = jnp.concatenate(gws + [jnp.zeros((8 - KC, tc), F32)], axis=0)

        @pl.when(i == 0)
        def _():
            gw_ref[...] = gw
            gb_ref[...] = gb

        @pl.when(i > 0)
        def _():
            gw_ref[...] += gw
            gb_ref[...] += gb

    return pl.pallas_call(
        body, name=name, grid=(CD // tc, nrb),
        in_specs=[pl.BlockSpec((tm, tc), lambda c, i: (i, c0 + c)),
                  pl.BlockSpec((HALO, tc), lambda c, i: (jnp.maximum(i * hb - 1, 0), c0 + c)),
                  pl.BlockSpec((HALO, tc), lambda c, i: (jnp.minimum((i + 1) * hb, last_h), c0 + c)),
                  pl.BlockSpec((tm, tc), lambda c, i: (i, c)),
                  pl.BlockSpec((HALO, tc), lambda c, i: (jnp.minimum((i + 1) * hb, last_h), c)),
                  pl.BlockSpec((KC, tc), lambda c, i: (0, w0 + c)),
                  pl.BlockSpec((1, tc), lambda c, i: (0, w0 + c))],
        out_specs=[pl.BlockSpec((tm, tc), lambda c, i: (i, c)),
                   pl.BlockSpec((8, tc), lambda c, i: (0, c)),
                   pl.BlockSpec((1, tc), lambda c, i: (0, c))],
        out_shape=[jax.ShapeDtypeStruct((S, CD), BF16), jax.ShapeDtypeStruct((8, CD), F32),
                   jax.ShapeDtypeStruct((1, CD), F32)],
        compiler_params=_cp(("parallel", "arbitrary")),
    )(proj, proj, proj, dxc, dxc, w, b)


def _pad_lanes(v, width=LANES):
    return jnp.pad(v, ((0, 0), (0, width - v.shape[1])))


def ssd_prep(dt_raw, dt_bias, a_log, cfg, name):
    S, L = cfg.S, cfg.L

    def body(x_ref, b_ref, al_ref, dt_ref, ac_ref):
        x = x_ref[...] + b_ref[...]
        dt = jnp.maximum(x, 0.0) + jnp.log(1.0 + jnp.exp(-jnp.abs(x)))
        da = dt * (-jnp.exp(al_ref[...]))
        li = lax.broadcasted_iota(jnp.int32, (L, L), 0)
        si = lax.broadcasted_iota(jnp.int32, (L, L), 1)
        tri = jnp.where(li >= si, 1.0, 0.0).astype(F32)
        dt_ref[...] = dt
        ac_ref[...] = lax.dot_general(tri, da, ((NN), ((), ())), precision=lax.Precision.HIGHEST,
                                      preferred_element_type=F32)

    row = pl.BlockSpec((L, LANES), lambda i: (i, 0))
    vec = pl.BlockSpec((1, LANES), lambda i: (0, 0))
    sh = jax.ShapeDtypeStruct((S, LANES), F32)
    return pl.pallas_call(
        body, name=name, grid=(S // L,), in_specs=[row, vec, vec], out_specs=[row, row], out_shape=[sh, sh],
        compiler_params=_cp(("parallel",)),
    )(dt_raw, dt_bias, a_log)


def _spread(v, n):
    return jnp.broadcast_to(v[:, :, None], v.shape + (n,)).reshape(v.shape[0], v.shape[1] * n)


def _head_selectors(cfg):
    def sel(width):
        head = jnp.arange(LANES)[None, :, None]
        slot = jnp.arange(cfg.SG)[:, None, None] * cfg.HPG + (jnp.arange(cfg.HPG * width) // width)[None, None, :]
        return (head == slot).astype(BF16)
    return sel(cfg.P), sel(LANES)


def _spread_heads(v, sel):
    n = v.shape[0]
    hi = v.astype(BF16)
    r1 = v - hi.astype(F32)
    mid = r1.astype(BF16)
    lo = (r1 - mid.astype(F32)).astype(BF16)
    out = _dot(jnp.concatenate([hi, mid, lo], axis=0), sel, NN)
    return out[:n] + out[n:2 * n] + out[2 * n:]


def _pair_lanes(wide, hpg, p):
    low = lax.broadcasted_iota(jnp.int32, (wide.shape[0], LANES), 1) < p
    return jnp.concatenate([jnp.where(low, wide[:, 2 * jp * LANES:(2 * jp + 1) * LANES],
                                      wide[:, (2 * jp + 1) * LANES:(2 * jp + 2) * LANES])
                            for jp in range(hpg // 2)], axis=1)


def _pair_select(halves, p):
    low = lax.broadcasted_iota(jnp.int32, halves[0].shape, 1) < p
    return jnp.where(low, halves[0], halves[1])


def _head_rows(row, hpg, p):
    return jnp.concatenate([jnp.broadcast_to(row[:, j * LANES:(j + 1) * LANES], (p, LANES)) for j in range(hpg)],
                           axis=0)


def _segment_sums(t, sel):
    r = t.shape[0]
    hi = t.astype(BF16)
    out = _dot(jnp.concatenate([hi, (t - hi.astype(F32)).astype(BF16)], axis=0), sel, NT)
    return out[:r] + out[r:]


def ssd_scan_fwd(xc, dt, acum, act, sel_p, sel_l, cfg, name):
    S, L, P, SN, HPG, SG, SI = cfg.S, cfg.L, cfg.P, cfg.SN, cfg.HPG, cfg.SG, cfg.SI
    nc = S // L
    GW = HPG * P
    bcol, ccol = _blk(SI, SN), _blk(SI + cfg.GN, SN)
    assert nc % CPS == 0

    def body(xs_ref, b_ref, c_ref, dtn_ref, acn_ref, at_ref, sp_ref, sl_ref, y_ref, st_ref, st):
        @pl.when(pl.program_id(1) == 0)
        def _():
            st[...] = jnp.zeros_like(st)

        causal = lax.broadcasted_iota(jnp.int32, (L, L), 0) >= lax.broadcasted_iota(jnp.int32, (L, L), 1)
        for ci in range(CPS):
            rows = slice(ci * L, (ci + 1) * L)
            acn = acn_ref[rows, :]
            dts = _spread_heads(dtn_ref[rows, :], sp_ref[0])
            acs = _spread_heads(acn, sl_ref[0])
            a_p = _pair_lanes(acs, HPG, P)
            s0 = st[...]
            st_ref[ci] = s0.reshape(HPG, P, SN)
            B = b_ref[rows, :].astype(BF16)
            C = c_ref[rows, :].astype(BF16)
            G = _dot(C, B, NT)
            xdt = xs_ref[rows, :] * dts
            xdtb = xdt.astype(BF16)
            ws = jnp.exp(a_p[L - 1:L, :] - a_p)
            yo = jnp.exp(a_p) * _dot(C, s0.astype(BF16), NT)
            yd = []
            for jp in range(HPG // 2):
                x_pair = xdtb[:, jp * LANES:(jp + 1) * LANES]
                halves = []
                for j in (2 * jp, 2 * jp + 1):
                    dm = jnp.where(causal, jnp.exp(acs[:, j * LANES:(j + 1) * LANES] - at_ref[j:j + 1, rows]), 0.0)
                    halves.append(_dot((G * dm).astype(BF16), x_pair, NN))
                yd.append(_pair_select(halves, P))
            y_ref[rows, :] = jnp.concatenate(yd, axis=1) + yo
            st[...] = _head_rows(jnp.exp(acs[L - 1:L, :]), HPG, P) * s0 + _dot((xdt * ws).astype(BF16), B, TN)

    R = CPS * L
    y, states = pl.pallas_call(
        body, name=name, grid=(SG, nc // CPS),
        in_specs=[pl.BlockSpec((R, GW), lambda g, c: (c, g)),
                  pl.BlockSpec((R, SN), lambda g, c: (c, bcol + g)),
                  pl.BlockSpec((R, SN), lambda g, c: (c, ccol + g)),
                  pl.BlockSpec((R, LANES), lambda g, c: (c, 0)),
                  pl.BlockSpec((R, LANES), lambda g, c: (c, 0)),
                  pl.BlockSpec((HPG, R), lambda g, c: (g, c)),
                  pl.BlockSpec((1, LANES, GW), lambda g, c: (g, 0, 0)),
                  pl.BlockSpec((1, LANES, HPG * LANES), lambda g, c: (g, 0, 0))],
        out_specs=[pl.BlockSpec((R, GW), lambda g, c: (c, g)),
                   pl.BlockSpec((CPS, HPG, P, SN), lambda g, c: (c, g, 0, 0))],
        out_shape=[jax.ShapeDtypeStruct((S, SI), F32), jax.ShapeDtypeStruct((nc, cfg.SH, P, SN), F32)],
        scratch_shapes=[pltpu.VMEM((GW, SN), F32)],
        compiler_params=_cp(("parallel", "arbitrary")),
    )(xc, xc, xc, dt, acum, act, sel_p, sel_l)
    return y, states


def ssd_scan_bwd(xc, dt, acum, act, sel_p, sel_l, states, y, dy, dvec, cfg, name, side):
    S, L, P, SN, HPG, SG, SI = cfg.S, cfg.L, cfg.P, cfg.SN, cfg.HPG, cfg.SG, cfg.SI
    nc = S // L
    GW = HPG * P
    bcol, ccol = _blk(SI, SN), _blk(SI + cfg.GN, SN)

    def rc(c):
        return nc // CPS - 1 - c

    def body(xs_ref, b_ref, c_ref, dtn_ref, acn_ref, at_ref, sp_ref, sl_ref, st_ref, y_ref, dy_ref, dk_ref,
             dxs_ref, db_ref, dc_ref, dac_ref, dxsum_ref, dst):
        @pl.when(pl.program_id(1) == 0)
        def _():
            dst[...] = jnp.zeros_like(dst)

        sel = sp_ref[0]
        causal = lax.broadcasted_iota(jnp.int32, (L, L), 0) >= lax.broadcasted_iota(jnp.int32, (L, L), 1)
        low = lax.broadcasted_iota(jnp.int32, (L, LANES), 1) < P
        is_last = lax.broadcasted_iota(jnp.int32, (L, LANES), 0) == L - 1
        ones = jnp.ones((16, SN), BF16)
        for ci in reversed(range(CPS)):
            rows = slice(ci * L, (ci + 1) * L)
            acn = acn_ref[rows, :]
            dts = _spread_heads(dtn_ref[rows, :], sel)
            acs = _spread_heads(acn, sl_ref[0])
            a_p = _pair_lanes(acs, HPG, P)
            B = b_ref[rows, :].astype(BF16)
            C = c_ref[rows, :].astype(BF16)
            G = _dot(C, B, NT)
            xs = xs_ref[rows, :]
            dY = dy_ref[rows, :].astype(F32)
            xdt = xs * dts
            xdtb = xdt.astype(BF16)
            dYb = dY.astype(BF16)
            s0 = st_ref[ci].reshape(GW, SN)
            s0b = s0.astype(BF16)
            ds1 = dst[...]
            ds1b = ds1.astype(BF16)
            ws = jnp.exp(a_p[L - 1:L, :] - a_p)
            dR = (jnp.exp(a_p) * dY).astype(BF16)
            dX2 = ws * _dot(B, ds1b, NT)
            dgsum = jnp.zeros((L, L), F32)
            dX1, yd = [], []
            for jp in range(HPG // 2):
                lanes = slice(jp * LANES, (jp + 1) * LANES)
                x_pair, dy_pair = xdtb[:, lanes], dYb[:, lanes]
                h1, h2 = [], []
                for h, j in enumerate((2 * jp, 2 * jp + 1)):
                    dm = jnp.where(causal, jnp.exp(acs[:, j * LANES:(j + 1) * LANES] - at_ref[j:j + 1, rows]), 0.0)
                    mine = low if h == 0 else jnp.logical_not(low)
                    dgsum = dgsum + _dot(jnp.where(mine, dy_pair, jnp.zeros_like(dy_pair)), x_pair, NT) * dm
                    Mb = (G * dm).astype(BF16)
                    h1.append(_dot(Mb, dy_pair, TN))
                    h2.append(_dot(Mb, x_pair, NN))
                dX1.append(_pair_select(h1, P))
                yd.append(_pair_select(h2, P))
            dX1 = jnp.concatenate(dX1, axis=1)
            dX = dX1 + dX2
            pair = (dYb.astype(F32) - dY) * jnp.concatenate(yd, axis=1) - xdtb.astype(F32) * dX1
            through = _segment_sums(xdt * dX2, sel)
            u = ds1 * s0
            u_hi = u.astype(BF16)
            u_rows = _dot(ones, jnp.concatenate([u_hi, (u - u_hi.astype(F32)).astype(BF16)], axis=0), NT)
            u_rows = u_rows[:, :GW] + u_rows[:, GW:]
            at_end = jnp.exp(acn[L - 1:L, :]) * _segment_sums(u_rows, sel)[0:1, :] + \
                jnp.sum(through, axis=0, keepdims=True)
            dac_ref[0, rows, :] = _segment_sums(dY * y_ref[rows, :] + pair, sel) - through + \
                jnp.where(is_last, at_end, 0.0)
            dxsum_ref[0, rows, :] = _segment_sums(dX * xs, sel)
            dxs_ref[rows, :] = dX * dts + dk_ref[...] * dY
            dst[...] = _head_rows(jnp.exp(acs[L - 1:L, :]), HPG, P) * ds1 + _dot(dR, C, TN)
            dgb = dgsum.astype(BF16)
            dc_ref[rows, :] = _dot(dR, s0b, NN) + _dot(dgb, B, NN)
            db_ref[rows, :] = _dot((xdt * ws).astype(BF16), ds1b, NN) + _dot(dgb, C, TN)

    R = CPS * L
    wide = pl.BlockSpec((R, GW), lambda g, c: (rc(c), g))
    colspec = pl.BlockSpec((1, R, LANES), lambda g, c: (g, rc(c), 0))
    whole = pl.BlockSpec(memory_space=pl.ANY)
    arrs, gathers = side
    grid = (SG, nc // CPS)
    res = pl.pallas_call(
        with_exchange(body, 12, 5, gathers, grid), name=name, grid=grid,
        in_specs=[wide,
                  pl.BlockSpec((R, SN), lambda g, c: (rc(c), bcol + g)),
                  pl.BlockSpec((R, SN), lambda g, c: (rc(c), ccol + g)),
                  pl.BlockSpec((R, LANES), lambda g, c: (rc(c), 0)),
                  pl.BlockSpec((R, LANES), lambda g, c: (rc(c), 0)),
                  pl.BlockSpec((HPG, R), lambda g, c: (g, rc(c))),
                  pl.BlockSpec((1, LANES, GW), lambda g, c: (g, 0, 0)),
                  pl.BlockSpec((1, LANES, HPG * LANES), lambda g, c: (g, 0, 0)),
                  pl.BlockSpec((CPS, HPG, P, SN), lambda g, c: (rc(c), g, 0, 0)),
                  wide, wide,
                  pl.BlockSpec((1, GW), lambda g, c: (0, g))] + [whole] * len(arrs),
        out_specs=[wide,
                   pl.BlockSpec((R, SN), lambda g, c: (rc(c), g)),
                   pl.BlockSpec((R, SN), lambda g, c: (rc(c), g)),
                   colspec, colspec] + [whole] * len(arrs),
        out_shape=[jax.ShapeDtypeStruct((S, SI), F32), jax.ShapeDtypeStruct((S, cfg.GN), F32),
                   jax.ShapeDtypeStruct((S, cfg.GN), F32),
                   jax.ShapeDtypeStruct((SG, S, LANES), F32), jax.ShapeDtypeStruct((SG, S, LANES), F32)] +
        _exchange_shapes(arrs, gathers),
        scratch_shapes=[pltpu.VMEM((GW, SN), F32)] + _exchange_sems(len(arrs)),
        compiler_params=_cp(("arbitrary", "arbitrary")),
    )(xc, xc, xc, dt, acum, act, sel_p, sel_l, states, y, dy, dvec, *arrs)
    return res[:5], res[5:]


def dt_bwd(dac, dxsum, dt_raw, dt, dt_bias, a_log, cfg, name):
    S, L, SG = cfg.S, cfg.L, cfg.SG

    def body(da_ref, dx_ref, x_ref, dt_ref, b_ref, al_ref, o_ref, gb_ref, ga_ref):
        a = -jnp.exp(al_ref[...])
        dtv = dt_ref[...]
        dxs = jnp.sum(dx_ref[...], axis=0)
        upper = jnp.where(lax.broadcasted_iota(jnp.int32, (L, L), 1) >= lax.broadcasted_iota(jnp.int32, (L, L), 0),
                          1.0, 0.0).astype(F32)
        dda = lax.dot_general(upper, jnp.sum(da_ref[...], axis=0), (NN, ((), ())), precision=lax.Precision.HIGHEST,
                              preferred_element_type=F32)
        draw = (dxs + dda * a) * _sigmoid(x_ref[...] + b_ref[...])
        o_ref[...] = draw.astype(BF16)
        gb = jnp.sum(draw, axis=0, keepdims=True)
        ga = jnp.sum(dda * dtv, axis=0, keepdims=True) * a

        @pl.when(pl.program_id(0) == 0)
        def _():
            gb_ref[...] = gb
            ga_ref[...] = ga

        @pl.when(pl.program_id(0) > 0)
        def _():
            gb_ref[...] += gb
            ga_ref[...] += ga

    row = pl.BlockSpec((L, LANES), lambda i: (i, 0))
    vec = pl.BlockSpec((1, LANES), lambda i: (0, 0))
    return pl.pallas_call(
        body, name=name, grid=(S // L,),
        in_specs=[pl.BlockSpec((SG, L, LANES), lambda i: (0, i, 0))] * 2 + [row, row, vec, vec],
        out_specs=[row, vec, vec],
        out_shape=[jax.ShapeDtypeStruct((S, LANES), BF16), jax.ShapeDtypeStruct((1, LANES), F32),
                   jax.ShapeDtypeStruct((1, LANES), F32)],
        compiler_params=_cp(("arbitrary",)),
    )(dac, dxsum, dt_raw, dt, dt_bias, a_log)


def gated_norm_fwd(y, xc, proj, dvec, nw, cfg, name, tm=128):
    S, SI = cfg.S, cfg.SI

    def body(y_ref, xs_ref, z_ref, d_ref, w_ref, o_ref):
        z = z_ref[...]
        yg = (y_ref[...] + d_ref[...] * xs_ref[...]) * (z * _sigmoid(z))
        r = lax.rsqrt(jnp.mean(yg * yg, axis=-1, keepdims=True) + RMS_EPS)
        o_ref[...] = ((yg * r) * w_ref[...]).astype(BF16)

    row = pl.BlockSpec((tm, SI), lambda i: (i, 0))
    vec = pl.BlockSpec((1, SI), lambda i: (0, 0))
    return pl.pallas_call(
        body, name=name, grid=(S // tm,),
        in_specs=[row, row, pl.BlockSpec((tm, SI), lambda i: (i, _blk(cfg.zs0, SI))), vec, vec],
        out_specs=row, out_shape=jax.ShapeDtypeStruct((S, SI), BF16),
        compiler_params=_cp(("parallel",)),
    )(y, xc, proj, dvec, nw)


def gated_norm_bwd(dyn, y, xc, proj, dvec, nw, cfg, name, tm=128):
    S, SI = cfg.S, cfg.SI

    def body(dn_ref, y_ref, xs_ref, z_ref, d_ref, w_ref, dy_ref, dz_ref, gw_ref, gd_ref):
        z = z_ref[...]
        s = _sigmoid(z)
        sz = z * s
        xs = xs_ref[...]
        yf = y_ref[...] + d_ref[...] * xs
        yg = yf * sz
        r = lax.rsqrt(jnp.mean(yg * yg, axis=-1, keepdims=True) + RMS_EPS)
        dn = dn_ref[...].astype(F32)
        g = dn * w_ref[...]
        dyg = r * g - yg * (r * r * r) * jnp.mean(g * yg, axis=-1, keepdims=True)
        dy = dyg * sz
        dy_ref[...] = dy.astype(BF16)
        dz_ref[...] = (dyg * yf * (s * (1.0 + z * (1.0 - s)))).astype(BF16)
        gw = jnp.sum(dn * (yg * r), axis=0, keepdims=True)
        gd = jnp.sum(dy * xs, axis=0, keepdims=True)

        @pl.when(pl.program_id(0) == 0)
        def _():
            gw_ref[...] = gw
            gd_ref[...] = gd

        @pl.when(pl.program_id(0) > 0)
        def _():
            gw_ref[...] += gw
            gd_ref[...] += gd

    row = pl.BlockSpec((tm, SI), lambda i: (i, 0))
    vec = pl.BlockSpec((1, SI), lambda i: (0, 0))
    return pl.pallas_call(
        body, name=name, grid=(S // tm,),
        in_specs=[row, row, row, pl.BlockSpec((tm, SI), lambda i: (i, _blk(cfg.zs0, SI))), vec, vec],
        out_specs=[row, row, vec, vec],
        out_shape=[jax.ShapeDtypeStruct((S, SI), BF16), jax.ShapeDtypeStruct((S, SI), BF16),
                   jax.ShapeDtypeStruct((1, SI), F32), jax.ShapeDtypeStruct((1, SI), F32)],
        compiler_params=_cp(("arbitrary",)),
    )(dyn, y, xc, proj, dvec, nw)


def _shard_columns(cfg, main, dt):
    dt0 = 4 * cfg.AW + cfg.SI + cfg.CD
    ws = cfg.N_IN // N_DEV
    out = []
    for k in range(N_DEV):
        lo, hi, parts = k * ws, (k + 1) * ws, []
        if lo < dt0:
            parts.append((main, lo, min(hi, dt0)))
        if lo < dt0 + cfg.SH and hi > dt0:
            parts.append((dt, max(lo, dt0) - dt0, min(hi, dt0 + cfg.SH) - dt0))
        if hi > dt0 + cfg.SH:
            parts.append((main, max(lo, dt0 + cfg.SH) - cfg.SH, hi - cfg.SH))
        out.append(parts)
    return out


def local_step(cfg, x, tgt, norm_w, conv_w, conv_b, dt_bias, a_log, d_skip, ssm_norm_w, final_norm_w,
               w_main, w_dt, shards, dt0):
    S, D = cfg.S, cfg.D
    slopes = _slope_table(cfg)
    dt_bias_p = _pad_lanes(dt_bias)
    a_log_p = _pad_lanes(a_log)
    dvec = _spread(d_skip, cfg.P)

    hn = rmsnorm_fwd(x, norm_w, "rmsnorm_fwd")
    proj, gathered = matmul(hn, w_main, 'nn', 1024, 2048, 2048, F32, "in_proj", side=(shards, [True] * 3))
    w_attn, w_ssm, w_out = gathered[0].reshape(cfg.AW, D), gathered[1].reshape(cfg.SI, D), gathered[2].reshape(D, D)
    dt_raw = matmul(hn, w_dt, 'nn', 512, 128, 2048, F32, "in_proj_dt")
    o_a, o_mix, ltot = attn_fused_fwd(proj, slopes, cfg, "attn_fwd")
    xc = conv_fwd(proj, conv_w, conv_b, cfg, "conv_fwd")
    dt, acum = ssd_prep(dt_raw, dt_bias_p, a_log_p, cfg, "ssd_prep")
    act = acum[:, :cfg.SH].T
    sel_p, sel_l = _head_selectors(cfg)
    y, states = ssd_scan_fwd(xc, dt, acum, act, sel_p, sel_l, cfg, "ssd_scan_fwd")
    y_n = gated_norm_fwd(y, xc, proj, dvec, ssm_norm_w, cfg, "gated_norm_fwd")
    a_out, s_out, merged = branch_merge(o_a, w_attn, y_n, w_ssm, proj, cfg, "branch_merge")
    dout, loss_p, g_final_w = out_proj_final(merged, w_out, x, final_norm_w.reshape(1, D), tgt, "out_proj_final")

    g_w_out = matmul(merged, dout, 'tn', 1024, 1024, 2048, BF16, "g_w_out")
    da_out, ds_out, dga, dgs = merge_bwd(dout, w_out, a_out, s_out, proj, cfg, "merge_bwd")
    g_w_attn = matmul(o_a, da_out, 'tn', 1024, 1024, 2048, BF16, "g_w_attn")
    g_w_ssm = matmul(y_n, ds_out, 'tn', 1024, 1024, 2048, BF16, "g_w_ssm")
    do_a = matmul(da_out, w_attn, 'nt', 512, 1024, 2048, BF16, "d_o_a")
    dyn = matmul(ds_out, w_ssm, 'nt', 512, 1024, 2048, BF16, "d_y_n")
    dy, dz_s, g_ssm_norm, g_dvec = gated_norm_bwd(dyn, y, xc, proj, dvec, ssm_norm_w, cfg, "gated_norm_bwd")
    sends = [g.reshape((N_DEV, g.shape[0] // N_DEV, D)) for g in (g_w_attn, g_w_ssm, g_w_out)]
    (dxs, dB, dC, dac_g, dxsum_g), (r_attn, r_ssm, r_out) = ssd_scan_bwd(
        xc, dt, acum, act, sel_p, sel_l, states, y, dy, dvec, cfg, "ssd_scan_bwd", side=(sends, [False] * 3))
    ddt_raw, g_dt_bias, g_a_log = dt_bwd(dac_g, dxsum_g, dt_raw, dt, dt_bias_p, a_log_p, cfg, "dt_bwd")
    dxbc, g_cw, g_cb = [], [], []
    for nm, piece, c_off in (("xs", dxs, 0), ("b", dB, cfg.SI), ("c", dC, cfg.SI + cfg.GN)):
        dx, gw, gb = conv_bwd(proj, piece, conv_w, conv_b, cfg, "conv_bwd_" + nm, c_off)
        dxbc.append(dx)
        g_cw.append(gw)
        g_cb.append(gb)
    g_conv_w, g_conv_b = jnp.concatenate(g_cw, axis=1), jnp.concatenate(g_cb, axis=1)
    dq, dk, dv, dz_a = attn_fused_bwd(proj, do_a, o_mix, ltot, slopes, cfg, "attn_bwd")
    dproj = jnp.concatenate([dq, dk, dv, dz_a, dz_s] + dxbc + [dga, dgs], axis=1)
    def slabs(g_main, g_dt):
        return jnp.stack([jnp.concatenate([g[:, lo:hi] for g, lo, hi in parts], axis=1)
                          for parts in _shard_columns(cfg, g_main, g_dt)])

    cut = D // 4
    g_w_dt = matmul(hn, ddt_raw, 'tn', 1024, 128, 2048, BF16, "g_w_dt")
    g_w_top = matmul(hn, dproj, 'tn', 512, 2048, 2048, BF16, "g_w_main_top", a_cols=(0, cut))
    g_w_bot, (r_top,) = matmul(hn, dproj, 'tn', 512, 2048, 2048, BF16, "g_w_main_rest", a_cols=(cut, D - cut),
                               side=([slabs(g_w_top, g_w_dt[:cut])], [False]))
    dhn_a, (r_bot,) = matmul(dproj, w_main, 'nt', 1024, 1024, 2048, F32, "d_hn",
                             side=([slabs(g_w_bot, g_w_dt[cut:])], [False]))
    grad_x, g_norm_w = rmsnorm_bwd(dhn_a, ddt_raw, w_dt, x, norm_w, dout, "rmsnorm_bwd")

    g_d_skip = jnp.sum(g_dvec.reshape(cfg.SH, cfg.P), axis=1).reshape(1, cfg.SH)
    small = dict(norm_w=g_norm_w, conv_b=g_conv_b, dt_bias=g_dt_bias[:, :cfg.SH], a_log=g_a_log[:, :cfg.SH],
                 d_skip=g_d_skip, ssm_norm_w=g_ssm_norm, final_norm_w=g_final_w, conv_w=g_conv_w[:cfg.KC])
    return loss_p, grad_x, small, dict(w_in=[r_top, r_bot], w_attn=[r_attn], w_ssm=[r_ssm], w_out=[r_out])


def _mesh_pos():
    return lax.axis_index("x"), lax.axis_index("y"), lax.axis_index("c")


def _flat(pos):
    return 4 * pos[0] + 2 * pos[1] + pos[2]


def _exchange_shapes(arrs, gathers):
    return [jax.ShapeDtypeStruct(((N_DEV,) + a.shape) if g else a.shape, a.dtype) for a, g in zip(arrs, gathers)]


def _exchange_sems(n):
    return [pltpu.SemaphoreType.DMA((n * (N_DEV - 1),)), pltpu.SemaphoreType.DMA((n * (N_DEV - 1),)),
            pltpu.SemaphoreType.DMA((n,))]


def _exchange_copies(ins, outs, gathers, send_sems, recv_sems, loc_sems):
    pos = _mesh_pos()
    me = _flat(pos)
    starts, waits = [], []
    for a in range(len(ins)):
        mine = ins[a] if gathers[a] else ins[a].at[me]
        loc = pltpu.make_async_copy(mine, outs[a].at[me], loc_sems.at[a])
        starts.append(loc)
        waits.append(loc)
        for k in range(1, N_DEV):
            flip = ((k >> 2) & 1, (k >> 1) & 1, k & 1)
            peer = tuple(1 - p if f else p for p, f in zip(pos, flip))
            pk = _flat(peer)
            src = ins[a] if gathers[a] else ins[a].at[pk]
            sems = dict(send_sem=send_sems.at[a * (N_DEV - 1) + k - 1], recv_sem=recv_sems.at[a * (N_DEV - 1) + k - 1],
                        device_id=peer, device_id_type=pl.DeviceIdType.MESH)
            starts.append(pltpu.make_async_remote_copy(src_ref=src, dst_ref=outs[a].at[me], **sems))
            waits.append(pltpu.make_async_remote_copy(src_ref=src, dst_ref=outs[a].at[pk], **sems))
    return starts, waits


def exchange(arrs, gathers, name):
    n = len(arrs)

    def body(*refs):
        starts, waits = _exchange_copies(refs[:n], refs[n:2 * n], gathers, *refs[2 * n:])
        for cp in starts:
            cp.start()
        for cp in waits:
            cp.wait()

    hbm = pl.BlockSpec(memory_space=pltpu.HBM)
    return pl.pallas_call(
        body, name=name, in_specs=[hbm] * n, out_specs=[hbm] * n, out_shape=_exchange_shapes(arrs, gathers),
        scratch_shapes=_exchange_sems(n),
    )(*arrs)


def gather_two_level(arrs, chunks, name):
    n = len(arrs)

    def body(*refs):
        ins, outs = refs[:n], refs[n:2 * n]
        send_sems, recv_sems, loc_sems = refs[2 * n:]
        x, y, c = _mesh_pos()
        me, sib = (x, y, c), (x, y, 1 - c)
        chips = [(1 - x, y), (x, 1 - y), (1 - x, 1 - y)]
        plan, base = [], 0
        for a in range(n):
            step = arrs[a].shape[0] // chunks[a]
            for q in range(chunks[a]):
                plan.append((a, pl.ds(q * step, step), base))
                base += N_DEV - 1

        def copy(a, rows, sem, block, to, own=False):
            dst = outs[a].at[_flat(block), rows]
            return pltpu.make_async_remote_copy(
                src_ref=ins[a].at[rows] if own else dst, dst_ref=dst, send_sem=send_sems.at[sem],
                recv_sem=recv_sems.at[sem], device_id=to, device_id_type=pl.DeviceIdType.MESH)

        local = [pltpu.make_async_copy(ins[a], outs[a].at[_flat(me)], loc_sems.at[a]) for a in range(n)]
        for cp in local:
            cp.start()
        sent = []
        for a, rows, s in plan:
            sent.append(copy(a, rows, s, me, sib, own=True))
            sent += [copy(a, rows, s + 1 + j, me, (*chip, c), own=True) for j, chip in enumerate(chips)]
        for cp in sent:
            cp.start()
        for a, rows, s in plan:
            for j, chip in enumerate(chips):
                copy(a, rows, s + 1 + j, (*chip, c), me).wait_recv()
                passed = copy(a, rows, s + 4 + j, (*chip, c), sib)
                passed.start()
                sent.append(passed)
        for a, rows, s in plan:
            copy(a, rows, s, sib, me).wait_recv()
            for j, chip in enumerate(chips):
                copy(a, rows, s + 4 + j, (*chip, 1 - c), me).wait_recv()
        for cp in sent:
            cp.wait_send()
        for cp in local:
            cp.wait()

    hbm = pl.BlockSpec(memory_space=pltpu.HBM)
    nsem = (N_DEV - 1) * sum(chunks)
    return pl.pallas_call(
        body, name=name, in_specs=[hbm] * n, out_specs=[hbm] * n, out_shape=_exchange_shapes(arrs, [True] * n),
        scratch_shapes=[pltpu.SemaphoreType.DMA((nsem,)), pltpu.SemaphoreType.DMA((nsem,)),
                        pltpu.SemaphoreType.DMA((n,))],
    )(*arrs)


def with_exchange(body, n_in, n_out, gathers, grid):
    n = len(gathers)

    def wrapped(*refs):
        ins, sends = refs[:n_in], refs[n_in:n_in + n]
        outs, recvs = refs[n_in + n:n_in + n + n_out], refs[n_in + 2 * n + n_out - n:n_in + 2 * n + n_out]
        scratch, sems = refs[n_in + 2 * n + n_out:-3], refs[-3:]
        ids = [pl.program_id(d) for d in range(len(grid))]
        first = functools.reduce(jnp.logical_and, [i == 0 for i in ids])
        last = functools.reduce(jnp.logical_and, [i == g - 1 for i, g in zip(ids, grid)])

        @pl.when(first)
        def _():
            for cp in _exchange_copies(sends, recvs, gathers, *sems)[0]:
                cp.start()

        body(*ins, *outs, *scratch)

        @pl.when(last)
        def _():
            for cp in _exchange_copies(sends, recvs, gathers, *sems)[1]:
                cp.wait()

    return wrapped


def adamw(g_src, w, m, v, summed, name, tr=64):
    R, C = w.shape
    tr = min(tr, R)
    assert R % tr == 0
    parts = g_src if summed else [g_src]
    starts = [sum(p.shape[1] for p in parts[:k]) // tr for k in range(len(parts))] if summed else [0]
    counts = [p.shape[1] // tr for p in parts] if summed else [R // tr]

    def body(*refs):
        g_refs, (w_ref, m_ref, v_ref, g_out, d_out, m_out, v_out) = refs[:len(parts)], refs[len(parts):]
        if summed:
            g = None
            for k, g_ref in enumerate(g_refs):
                gk = g_ref[0].astype(F32)
                for j in range(1, N_DEV):
                    gk = gk + g_ref[j].astype(F32)
                g = gk if g is None else jnp.where(pl.program_id(0) >= starts[k], gk, g)
        else:
            g = g_refs[0][...]
        mn = ADAM_B1 * m_ref[...] + (1.0 - ADAM_B1) * g
        vn = ADAM_B2 * v_ref[...] + (1.0 - ADAM_B2) * (g * g)
        m_hat = mn / (1.0 - ADAM_B1 ** ADAM_STEP)
        v_hat = vn / (1.0 - ADAM_B2 ** ADAM_STEP)
        g_out[...] = g
        d_out[...] = -ADAM_LR * (m_hat / (jnp.sqrt(v_hat) + ADAM_EPS) + ADAM_WD * w_ref[...])
        m_out[...] = mn
        v_out[...] = vn

    row = pl.BlockSpec((tr, C), lambda i: (i, 0))
    if summed:
        gspecs = [pl.BlockSpec((N_DEV, tr, C), lambda i, s=st, n=nb: (0, jnp.clip(i - s, 0, n - 1), 0))
                  for st, nb in zip(starts, counts)]
    else:
        gspecs = [row]
    sh = jax.ShapeDtypeStruct((R, C), F32)
    return pl.pallas_call(
        body, name=name, grid=(R // tr,), in_specs=gspecs + [row, row, row], out_specs=[row] * 4, out_shape=[sh] * 4,
        compiler_params=_cp(("parallel",)),
    )(*parts, w, m, v)


SMALL = ('norm_w', 'conv_b', 'dt_bias', 'a_log', 'd_skip', 'ssm_norm_w', 'final_norm_w')


def _rows(n):
    return -(-n // (8 * LANES)) * 8


def _pack(vals):
    parts = []
    for a in vals:
        f = a.reshape(-1)
        parts.append(jnp.pad(f, (0, _rows(f.size) * LANES - f.size)).reshape(-1, LANES))
    return jnp.concatenate(parts, axis=0)


def _unpack(packed, shapes):
    out, r = [], 0
    for s in shapes:
        n = math.prod(s)
        out.append(packed[r:r + _rows(n)].reshape(-1)[:n].reshape(s))
        r += _rows(n)
    return out


def kernel(x, norm_w, w_in, conv_w, conv_b, dt_bias, a_log, d_skip, ssm_norm_w, w_attn_branch, w_ssm_branch, w_out, final_norm_w, loss_target, m_norm_w, m_w_in, m_conv_w, m_conv_b, m_dt_bias, m_a_log, m_d_skip, m_ssm_norm_w, m_w_attn_branch, m_w_ssm_branch, m_w_out, m_final_norm_w, v_norm_w, v_w_in, v_conv_w, v_conv_b, v_dt_bias, v_a_log, v_d_skip, v_ssm_norm_w, v_w_attn_branch, v_w_ssm_branch, v_w_out, v_final_norm_w):
    cfg = CFG
    D, SH = cfg.D, cfg.SH
    me = _flat(_mesh_pos())
    dt0 = 4 * cfg.AW + cfg.SI + cfg.CD
    ws = w_in.shape[-1]

    g_in, g_cw = gather_two_level([w_in[0].astype(BF16), conv_w[0]], [4, 1], "gather_w_in")
    main_cols, dt_cols = [], []
    for k, parts in enumerate(_shard_columns(cfg, "main", "dt")):
        at = 0
        for which, lo, hi in parts:
            (main_cols if which == "main" else dt_cols).append(g_in[k][:, at:at + hi - lo])
            at += hi - lo
    w_main = jnp.concatenate(main_cols, axis=1)
    w_dt = _pad_lanes(jnp.concatenate(dt_cols, axis=1))
    conv_full = g_cw.transpose(1, 0, 2).reshape(cfg.KC, cfg.CD)
    shards = [w_attn_branch[0].astype(BF16), w_ssm_branch[0].astype(BF16), w_out[0].astype(BF16)]

    loss_p, grad_x, small, recv = local_step(
        cfg, x[0], loss_target[0], norm_w, conv_full, conv_b, dt_bias, a_log, d_skip,
        ssm_norm_w, final_norm_w, w_main, w_dt, shards, dt0)

    upd = {}
    upd['w_in'] = adamw(recv['w_in'], w_in[0], m_w_in[0], v_w_in[0], True, "adamw_w_in")
    upd['w_attn_branch'] = adamw(recv['w_attn'], w_attn_branch[0], m_w_attn_branch[0], v_w_attn_branch[0], True,
                                 "adamw_w_attn")
    upd['w_ssm_branch'] = adamw(recv['w_ssm'], w_ssm_branch[0], m_w_ssm_branch[0], v_w_ssm_branch[0], True,
                                "adamw_w_ssm")
    upd['w_out'] = adamw(recv['w_out'], w_out[0], m_w_out[0], v_w_out[0], True, "adamw_w_out")

    extra = [jnp.zeros((cfg.KC, cfg.CD), F32), jnp.zeros((1, 1), F32)]
    shapes = [small[n].shape for n in SMALL] + [e.shape for e in extra]
    part = _pack([small[n] for n in SMALL] + [small['conv_w'], loss_p[:, :1]])
    gathered, = exchange([part], [True], "gather_small")
    given = dict(norm_w=(norm_w, m_norm_w, v_norm_w), conv_b=(conv_b, m_conv_b, v_conv_b),
                 dt_bias=(dt_bias, m_dt_bias, v_dt_bias), a_log=(a_log, m_a_log, v_a_log),
                 d_skip=(d_skip, m_d_skip, v_d_skip), ssm_norm_w=(ssm_norm_w, m_ssm_norm_w, v_ssm_norm_w),
                 final_norm_w=(final_norm_w, m_final_norm_w, v_final_norm_w))
    packed = [_pack([given[n][t] for n in SMALL] + extra) for t in range(3)]
    outs = adamw([gathered], *packed, True, "adamw_small", tr=part.shape[0])
    unpacked = [_unpack(o, shapes) for o in outs]
    for i, n in enumerate(SMALL):
        upd[n] = [u[i].reshape(given[n][0].shape) for u in unpacked]
    loss = unpacked[0][-1].reshape(())
    cw = conv_w.shape[-1]
    g_cw_mine = lax.dynamic_slice_in_dim(unpacked[0][-2], me * cw, cw, axis=1)
    upd['conv_w'] = adamw(g_cw_mine.reshape(-1, LANES), conv_w.reshape(-1, LANES), m_conv_w.reshape(-1, LANES),
                          v_conv_w.reshape(-1, LANES), False, "adamw_conv_w")

    order = ['norm_w', 'w_in', 'conv_w', 'conv_b', 'dt_bias', 'a_log', 'd_skip', 'ssm_norm_w', 'w_attn_branch',
             'w_ssm_branch', 'w_out', 'final_norm_w']
    like = dict(norm_w=norm_w, w_in=w_in, conv_w=conv_w, conv_b=conv_b, dt_bias=dt_bias, a_log=a_log, d_skip=d_skip,
                ssm_norm_w=ssm_norm_w, w_attn_branch=w_attn_branch, w_ssm_branch=w_ssm_branch, w_out=w_out,
                final_norm_w=final_norm_w)
    result = [loss, grad_x[None]]
    for t in range(4):
        result += [upd[n][t].reshape(like[n].shape) for n in order]
    return tuple(result)
```

```python
import functools
import math
from typing import NamedTuple

import jax
import jax.numpy as jnp
from jax import lax
from jax.experimental import pallas as pl
from jax.experimental.pallas import tpu as pltpu

F32 = jnp.float32
BF16 = jnp.bfloat16
RMS_EPS = 1e-6
NEG = -1e30
N_DEV = 8
CPS = 8
LANES = 128
ATTN_BLOCK = 128
ADAM_LR, ADAM_B1, ADAM_B2, ADAM_EPS, ADAM_WD, ADAM_STEP = 0.001, 0.9, 0.999, 1e-08, 0.01, 10
VMEM_LIMIT = 56 * 1024 * 1024


class Cfg(NamedTuple):
    D: int = 2048
    S: int = 8192
    AH: int = 16
    E: int = 128
    patterns: tuple = ((128, 1), (512, 4), (2048, 16))
    SI: int = 4096
    P: int = 64
    SG: int = 8
    SN: int = 128
    KC: int = 4
    L: int = 128

    @property
    def AW(self): return self.AH * self.E
    @property
    def SH(self): return self.SI // self.P
    @property
    def HPG(self): return self.SH // self.SG
    @property
    def GN(self): return self.SG * self.SN
    @property
    def CD(self): return self.SI + 2 * self.GN
    @property
    def k0(self): return self.AW
    @property
    def v0(self): return 2 * self.AW
    @property
    def za0(self): return 3 * self.AW
    @property
    def zs0(self): return 4 * self.AW
    @property
    def xbc0(self): return 4 * self.AW + self.SI
    @property
    def ga0(self): return self.xbc0 + self.CD
    @property
    def gs0(self): return self.ga0 + self.D
    @property
    def NP(self): return self.gs0 + self.D
    @property
    def N_IN(self): return self.NP + self.SH


CFG = Cfg()


def _cp(sem=None, vmem=VMEM_LIMIT):
    return pltpu.CompilerParams(dimension_semantics=sem, vmem_limit_bytes=vmem)


def _sigmoid(z):
    return 1.0 / (1.0 + jnp.exp(-z))


def _dot(a, b, dims):
    return lax.dot_general(a, b, (dims, ((), ())), preferred_element_type=F32)


NN = ((1,), (0,))
NT = ((1,), (1,))
TN = ((0,), (0,))


def _blk(off, width):
    assert off % width == 0, (off, width)
    return off // width


def matmul(a, b, mode, tm, tn, tk, out_dtype, name, side=None, a_cols=None):
    m0 = 0
    if mode == 'nn':
        (M, K), (_, N) = a.shape, b.shape
    elif mode == 'nt':
        (M, K), (N, _) = a.shape, b.shape
    else:
        (K, M), (_, N) = a.shape, b.shape
        if a_cols is not None:
            m0, M = a_cols
    tm, tn, tk = math.gcd(min(tm, M), m0), min(tn, N), min(tk, K)
    assert M % tm == 0 and N % tn == 0 and K % tk == 0, (M, N, K, tm, tn, tk)
    nk = K // tk
    dims = {'nn': NN, 'nt': NT, 'tn': TN}[mode]

    def body(a_ref, b_ref, o_ref, *acc):
        part = _dot(a_ref[...].astype(BF16), b_ref[...].astype(BF16), dims)
        if nk == 1:
            o_ref[...] = part.astype(out_dtype)
        else:
            acc_ref, = acc
            k = pl.program_id(2)

            @pl.when(k == 0)
            def _():
                acc_ref[...] = part

            @pl.when(k > 0)
            def _():
                acc_ref[...] += part

            @pl.when(k == nk - 1)
            def _():
                o_ref[...] = acc_ref[...].astype(out_dtype)

    if mode == 'tn':
        a_spec = pl.BlockSpec((tk, tm), lambda n, m, k: (k, _blk(m0, tm) + m))
    else:
        a_spec = pl.BlockSpec((tm, tk), lambda n, m, k: (m, k))
    if mode == 'nt':
        b_spec = pl.BlockSpec((tn, tk), lambda n, m, k: (n, k))
    else:
        b_spec = pl.BlockSpec((tk, tn), lambda n, m, k: (k, n))
    grid = (N // tn, M // tm, nk)
    o_spec = pl.BlockSpec((tm, tn), lambda n, m, k: (m, n))
    o_shape = jax.ShapeDtypeStruct((M, N), out_dtype)
    acc = [] if nk == 1 else [pltpu.VMEM((tm, tn), F32)]
    if side is None:
        return pl.pallas_call(
            body, name=name, grid=grid, in_specs=[a_spec, b_spec], out_specs=o_spec, out_shape=o_shape,
            scratch_shapes=acc, compiler_params=_cp(("parallel", "parallel", "arbitrary")),
        )(a, b)
    arrs, gathers = side
    whole = pl.BlockSpec(memory_space=pl.ANY)
    res = pl.pallas_call(
        with_exchange(body, 2, 1, gathers, grid), name=name, grid=grid,
        in_specs=[a_spec, b_spec] + [whole] * len(arrs), out_specs=[o_spec] + [whole] * len(arrs),
        out_shape=[o_shape] + _exchange_shapes(arrs, gathers),
        scratch_shapes=acc + _exchange_sems(len(arrs)),
        compiler_params=_cp(("arbitrary", "arbitrary", "arbitrary")),
    )(a, b, *arrs)
    return res[0], res[1:]


def rmsnorm_fwd(x, w, name, tm=256):
    S, D = x.shape

    def body(x_ref, w_ref, o_ref):
        xv = x_ref[...]
        r = lax.rsqrt(jnp.mean(xv * xv, axis=-1, keepdims=True) + RMS_EPS)
        o_ref[...] = ((xv * r) * w_ref[...]).astype(BF16)

    return pl.pallas_call(
        body, name=name, grid=(S // tm,),
        in_specs=[pl.BlockSpec((tm, D), lambda i: (i, 0)), pl.BlockSpec((1, D), lambda i: (0, 0))],
        out_specs=pl.BlockSpec((tm, D), lambda i: (i, 0)),
        out_shape=jax.ShapeDtypeStruct((S, D), BF16),
        compiler_params=_cp(("parallel",)),
    )(x, w)


def rmsnorm_bwd(dh_a, ddt, w_dt, x, w, dout, name, tm=128):
    S, D = x.shape

    def body(da_ref, dd_ref, wd_ref, x_ref, w_ref, do_ref, gx_ref, gw_ref):
        xv = x_ref[...]
        dh = da_ref[...] + _dot(dd_ref[...], wd_ref[...], NT)
        r = lax.rsqrt(jnp.mean(xv * xv, axis=-1, keepdims=True) + RMS_EPS)
        g = dh * w_ref[...]
        dx = r * g - xv * (r * r * r) * jnp.mean(g * xv, axis=-1, keepdims=True)
        gx_ref[...] = do_ref[...] + dx
        gw = jnp.sum(dh * (xv * r), axis=0, keepdims=True)

        @pl.when(pl.program_id(0) == 0)
        def _():
            gw_ref[...] = gw

        @pl.when(pl.program_id(0) > 0)
        def _():
            gw_ref[...] += gw

    row = pl.BlockSpec((tm, D), lambda i: (i, 0))
    vec = pl.BlockSpec((1, D), lambda i: (0, 0))
    return pl.pallas_call(
        body, name=name, grid=(S // tm,),
        in_specs=[row, pl.BlockSpec((tm, LANES), lambda i: (i, 0)), pl.BlockSpec((D, LANES), lambda i: (0, 0)), row,
                  vec, row],
        out_specs=[row, vec],
        out_shape=[jax.ShapeDtypeStruct((S, D), F32), jax.ShapeDtypeStruct((1, D), F32)],
        compiler_params=_cp(("arbitrary",)),
    )(dh_a, ddt, w_dt, x, w, dout)


def out_proj_final(merged, w_out, x, fw, tgt, name, tm=256):
    S, D = x.shape

    def body(m_ref, wo_ref, x_ref, w_ref, t_ref, do_ref, loss_ref, gw_ref):
        out = x_ref[...] + _dot(m_ref[...], wo_ref[...], NN)
        w = w_ref[...]
        r = lax.rsqrt(jnp.mean(out * out, axis=-1, keepdims=True) + RMS_EPS)
        yn = out * r
        err = yn * w - t_ref[...]
        lrow = 0.5 * jnp.mean(err * err, axis=-1, keepdims=True)
        lsum = jnp.zeros((1, LANES), F32) + jnp.sum(lrow, axis=0, keepdims=True)
        dfin = err * (1.0 / D)
        g = dfin * w
        do_ref[...] = r * g - out * (r * r * r) * jnp.mean(g * out, axis=-1, keepdims=True)
        gw = jnp.sum(dfin * yn, axis=0, keepdims=True)

        @pl.when(pl.program_id(0) == 0)
        def _():
            gw_ref[...] = gw
            loss_ref[...] = lsum

        @pl.when(pl.program_id(0) > 0)
        def _():
            gw_ref[...] += gw
            loss_ref[...] += lsum

    row = pl.BlockSpec((tm, D), lambda i: (i, 0))
    vec = pl.BlockSpec((1, D), lambda i: (0, 0))
    return pl.pallas_call(
        body, name=name, grid=(S // tm,),
        in_specs=[row, pl.BlockSpec((D, D), lambda i: (0, 0)), row, vec, row],
        out_specs=[row, pl.BlockSpec((1, LANES), lambda i: (0, 0)), vec],
        out_shape=[jax.ShapeDtypeStruct((S, D), F32), jax.ShapeDtypeStruct((1, LANES), F32),
                   jax.ShapeDtypeStruct((1, D), F32)],
        compiler_params=_cp(("arbitrary",)),
    )(merged, w_out, x, fw, tgt)


def branch_merge(o_a, w_attn, y_n, w_ssm, proj, cfg, name, tm=512, tn=512):
    S, D = cfg.S, cfg.D
    tm, tn = min(tm, S), min(tn, D)

    def body(oa_ref, wa_ref, yn_ref, ws_ref, ga_ref, gs_ref, a_ref, s_ref, m_ref):
        a = _dot(oa_ref[...], wa_ref[...], NN)
        sv = _dot(yn_ref[...], ws_ref[...], NN)
        a_ref[...] = a.astype(BF16)
        s_ref[...] = sv.astype(BF16)
        m_ref[...] = (_sigmoid(ga_ref[...]) * a + _sigmoid(gs_ref[...]) * sv).astype(BF16)

    tile = pl.BlockSpec((tm, tn), lambda n, m: (m, n))
    return pl.pallas_call(
        body, name=name, grid=(D // tn, S // tm),
        in_specs=[pl.BlockSpec((tm, cfg.AW), lambda n, m: (m, 0)), pl.BlockSpec((cfg.AW, tn), lambda n, m: (0, n)),
                  pl.BlockSpec((tm, cfg.SI), lambda n, m: (m, 0)), pl.BlockSpec((cfg.SI, tn), lambda n, m: (0, n)),
                  pl.BlockSpec((tm, tn), lambda n, m: (m, _blk(cfg.ga0, tn) + n)),
                  pl.BlockSpec((tm, tn), lambda n, m: (m, _blk(cfg.gs0, tn) + n))],
        out_specs=[tile, tile, tile],
        out_shape=[jax.ShapeDtypeStruct((S, D), BF16)] * 3,
        compiler_params=_cp(("parallel", "parallel")),
    )(o_a, w_attn, y_n, w_ssm, proj, proj)


def merge_bwd(dout, w_out, a_out, s_out, proj, cfg, name, tm=512, tn=1024):
    S, D = cfg.S, cfg.D
    tm, tn = min(tm, S), min(tn, D)

    def body(do_ref, wo_ref, a_ref, s_ref, ga_ref, gs_ref, da_ref, ds_ref, dga_ref, dgs_ref):
        dmv = _dot(do_ref[...].astype(BF16), wo_ref[...], NT)
        sa = _sigmoid(ga_ref[...])
        ss = _sigmoid(gs_ref[...])
        da_ref[...] = (dmv * sa).astype(BF16)
        ds_ref[...] = (dmv * ss).astype(BF16)
        dga_ref[...] = (dmv * a_ref[...] * (sa * (1.0 - sa))).astype(BF16)
        dgs_ref[...] = (dmv * s_ref[...] * (ss * (1.0 - ss))).astype(BF16)

    tile = pl.BlockSpec((tm, tn), lambda n, m: (m, n))
    sh = jax.ShapeDtypeStruct((S, D), BF16)
    return pl.pallas_call(
        body, name=name, grid=(D // tn, S // tm),
        in_specs=[pl.BlockSpec((tm, D), lambda n, m: (m, 0)), pl.BlockSpec((tn, D), lambda n, m: (n, 0)), tile, tile,
                  pl.BlockSpec((tm, tn), lambda n, m: (m, _blk(cfg.ga0, tn) + n)),
                  pl.BlockSpec((tm, tn), lambda n, m: (m, _blk(cfg.gs0, tn) + n))],
        out_specs=[tile] * 4, out_shape=[sh] * 4,
        compiler_params=_cp(("parallel", "parallel")),
    )(dout, w_out, a_out, s_out, proj, proj)


def _attn_rows(base, d):
    return pl.ds(base, ATTN_BLOCK) if d == 1 else pl.ds(base, ATTN_BLOCK, stride=d)


def _attn_units(cfg):
    dmax = max(d for _, d in cfg.patterns)
    units = []
    for p, (window, d) in enumerate(cfg.patterns):
        assert window // d == ATTN_BLOCK and dmax % d == 0
        nsub = dmax // d
        for b in range(nsub):
            for r in range(d):
                base = b * ATTN_BLOCK * d + r
                if b > 0:
                    units.append((p, d, base, (b - 1) * ATTN_BLOCK * d + r, False))
                else:
                    units.append((p, d, base, (nsub - 1) * ATTN_BLOCK * d + r, True))
    return units, ATTN_BLOCK * dmax


def _set_bias_tiles(bias_s, slope, cfg):
    qi = lax.broadcasted_iota(jnp.int32, (ATTN_BLOCK, ATTN_BLOCK), 0)
    ki = lax.broadcasted_iota(jnp.int32, (ATTN_BLOCK, ATTN_BLOCK), 1)
    for p, (_, d) in enumerate(cfg.patterns):
        bias_s[2 * p] = jnp.where(ki >= qi, (-slope) * ((ATTN_BLOCK + qi - ki) * d).astype(F32), NEG)
        bias_s[2 * p + 1] = jnp.where(ki <= qi, (-slope) * ((qi - ki) * d).astype(F32), NEG)


def _unit_scores(q, kcat, bias_s, p, prev_ok, scale):
    s = _dot(q, kcat, NT) * scale + jnp.concatenate([bias_s[2 * p], bias_s[2 * p + 1]], axis=1)
    if prev_ok is not None:
        cur_half = lax.broadcasted_iota(jnp.int32, s.shape, 1) >= ATTN_BLOCK
        s = jnp.where(jnp.logical_or(cur_half, prev_ok), s, NEG)
    return s


def _slope_table(cfg):
    slopes = jnp.asarray([2.0 ** (-8.0 * (h + 1) / cfg.AH) for h in range(cfg.AH)], F32)
    return jnp.broadcast_to(slopes.reshape(cfg.AH, 1, 1), (cfg.AH, 8, LANES))


def attn_fused_fwd(proj, slopes, cfg, name):
    S, E, AH = cfg.S, cfg.E, cfg.AH
    units, SB = _attn_units(cfg)
    assert S % SB == 0
    npat = len(cfg.patterns)
    scale = E ** -0.5

    def spec(off, prev):
        c0 = _blk(off, E)
        if prev:
            return pl.BlockSpec((SB, E), lambda h, i: (jnp.maximum(i - 1, 0), c0 + h))
        return pl.BlockSpec((SB, E), lambda h, i: (i, c0 + h))

    def body(q_ref, kp_ref, kc_ref, vp_ref, vc_ref, z_ref, sl_ref, oa_ref, om_ref, lt_ref, *scr):
        o_s, l_s, bias_s = scr[:npat], scr[npat:2 * npat], scr[2 * npat]
        i = pl.program_id(1)

        @pl.when(i == 0)
        def _():
            _set_bias_tiles(bias_s, sl_ref[0, 0:1, :], cfg)

        for p, d, base, pbase, from_prev in units:
            rows, prows = _attn_rows(base, d), _attn_rows(pbase, d)
            q = q_ref[rows, :].astype(BF16)
            kp = (kp_ref if from_prev else kc_ref)[prows, :].astype(BF16)
            vp = (vp_ref if from_prev else vc_ref)[prows, :].astype(BF16)
            kcat = jnp.concatenate([kp, kc_ref[rows, :].astype(BF16)], axis=0)
            vcat = jnp.concatenate([vp, vc_ref[rows, :].astype(BF16)], axis=0)
            s = _unit_scores(q, kcat, bias_s, p, (i > 0) if from_prev else None, scale)
            m = jnp.max(s, axis=1, keepdims=True)
            pr = jnp.exp(s - m)
            l = jnp.sum(pr, axis=1, keepdims=True)
            o_s[p][rows, :] = _dot(pr.astype(BF16), vcat, NN) * (1.0 / l)
            l_s[p][rows, :] = m + jnp.log(l)
        ls = [l_s[p][...] for p in range(npat)]
        m = functools.reduce(jnp.maximum, ls)
        lt = m + jnp.log(sum(jnp.exp(l_ - m) for l_ in ls))
        lt_ref[...] = lt
        mix = sum(jnp.exp(ls[p] - lt) * o_s[p][...] for p in range(npat))
        om_ref[...] = mix
        z = z_ref[...]
        oa_ref[...] = (mix * (z * _sigmoid(z))).astype(BF16)

    out = pl.BlockSpec((SB, E), lambda h, i: (i, h))
    return pl.pallas_call(
        body, name=name, grid=(AH, S // SB),
        in_specs=[spec(0, False), spec(cfg.k0, True), spec(cfg.k0, False), spec(cfg.v0, True), spec(cfg.v0, False),
                  spec(cfg.za0, False), pl.BlockSpec((1, 8, LANES), lambda h, i: (h, 0, 0))],
        out_specs=[out, out, pl.BlockSpec((SB, 1), lambda h, i: (h * (S // SB) + i, 0))],
        out_shape=[jax.ShapeDtypeStruct((S, cfg.AW), BF16), jax.ShapeDtypeStruct((S, cfg.AW), F32),
                   jax.ShapeDtypeStruct((AH * S, 1), F32)],
        scratch_shapes=[pltpu.VMEM((SB, E), F32)] * npat + [pltpu.VMEM((SB, 1), F32)] * npat +
        [pltpu.VMEM((2 * npat, ATTN_BLOCK, ATTN_BLOCK), F32)],
        compiler_params=_cp(("parallel", "arbitrary")),
    )(proj, proj, proj, proj, proj, proj, slopes)


def attn_fused_bwd(proj, do_a, o_mix, ltot, slopes, cfg, name):
    S, E, AH = cfg.S, cfg.E, cfg.AH
    units, SB = _attn_units(cfg)
    nsb = S // SB
    last = nsb - 1
    scale = E ** -0.5

    def spec(off, prev):
        c0 = _blk(off, E)
        if prev:
            return pl.BlockSpec((SB, E), lambda h, i: (jnp.maximum(i - 1, 0), c0 + h))
        return pl.BlockSpec((SB, E), lambda h, i: (jnp.minimum(i, last), c0 + h))

    cur = pl.BlockSpec((SB, E), lambda h, i: (jnp.minimum(i, last), h))
    prev = pl.BlockSpec((SB, E), lambda h, i: (jnp.maximum(i - 1, 0), h))

    def body(q_ref, kp_ref, kc_ref, vp_ref, vc_ref, z_ref, doa_ref, om_ref, lt_ref, sl_ref,
             dq_ref, dk_ref, dv_ref, dz_ref, dmix_s, dl_s, dq_s, dkp_s, dvp_s, dkc_s, dvc_s, bias_s):
        i = pl.program_id(1)

        @pl.when(i == 0)
        def _():
            dkc_s[...] = jnp.zeros_like(dkc_s)
            dvc_s[...] = jnp.zeros_like(dvc_s)
            _set_bias_tiles(bias_s, sl_ref[0, 0:1, :], cfg)

        @pl.when(i < nsb)
        def _():
            z = z_ref[...]
            s = _sigmoid(z)
            doa = doa_ref[...].astype(F32)
            om = om_ref[...]
            dmix = doa * (z * s)
            dmix_s[...] = dmix
            dz_ref[...] = (doa * om * (s * (1.0 + z * (1.0 - s)))).astype(BF16)
            dl_s[...] = jnp.sum(dmix * om, axis=1, keepdims=True)
            dkp_s[...] = dkc_s[...]
            dvp_s[...] = dvc_s[...]
            dkc_s[...] = jnp.zeros_like(dkc_s)
            dvc_s[...] = jnp.zeros_like(dvc_s)
            dq_s[...] = jnp.zeros_like(dq_s)
            for p, d, base, pbase, from_prev in units:
                rows, prows = _attn_rows(base, d), _attn_rows(pbase, d)
                q = q_ref[rows, :].astype(BF16)
                kc = kc_ref[rows, :].astype(BF16)
                kp = (kp_ref if from_prev else kc_ref)[prows, :].astype(BF16)
                vp = (vp_ref if from_prev else vc_ref)[prows, :].astype(BF16)
                do = dmix_s[rows, :].astype(BF16)
                lt = lt_ref[rows, :]
                dlt = dl_s[rows, :]
                kcat = jnp.concatenate([kp, kc], axis=0)
                vcat = jnp.concatenate([vp, vc_ref[rows, :].astype(BF16)], axis=0)
                pr = jnp.exp(_unit_scores(q, kcat, bias_s, p, (i > 0) if from_prev else None, scale) - lt)
                ds = (pr * (_dot(do, vcat, NT) - dlt) * scale).astype(BF16)
                dq_s[rows, :] += _dot(ds, kcat, NN)
                dkcat = _dot(ds, q, TN)
                dvcat = _dot(pr.astype(BF16), do, TN)
                dk_t, dv_t = (dkp_s, dvp_s) if from_prev else (dkc_s, dvc_s)
                dk_t[prows, :] += dkcat[:ATTN_BLOCK, :]
                dv_t[prows, :] += dvcat[:ATTN_BLOCK, :]
                dkc_s[rows, :] += dkcat[ATTN_BLOCK:, :]
                dvc_s[rows, :] += dvcat[ATTN_BLOCK:, :]
            dq_ref[...] = dq_s[...].astype(BF16)
            dk_ref[...] = dkp_s[...].astype(BF16)
            dv_ref[...] = dvp_s[...].astype(BF16)

        @pl.when(i == nsb)
        def _():
            dk_ref[...] = dkc_s[...].astype(BF16)
            dv_ref[...] = dvc_s[...].astype(BF16)

    sh = jax.ShapeDtypeStruct((S, cfg.AW), BF16)
    acc = pltpu.VMEM((SB, E), F32)
    return pl.pallas_call(
        body, name=name, grid=(AH, nsb + 1),
        in_specs=[spec(0, False), spec(cfg.k0, True), spec(cfg.k0, False), spec(cfg.v0, True), spec(cfg.v0, False),
                  spec(cfg.za0, False), cur, cur,
                  pl.BlockSpec((SB, 1), lambda h, i: (h * nsb + jnp.minimum(i, last), 0)),
                  pl.BlockSpec((1, 8, LANES), lambda h, i: (h, 0, 0))],
        out_specs=[cur, prev, prev, cur], out_shape=[sh] * 4,
        scratch_shapes=[acc, pltpu.VMEM((SB, 1), F32), acc, acc, acc, acc, acc,
                        pltpu.VMEM((2 * len(cfg.patterns), ATTN_BLOCK, ATTN_BLOCK), F32)],
        compiler_params=_cp(("parallel", "arbitrary")),
    )(proj, proj, proj, proj, proj, proj, do_a, o_mix, ltot, slopes)


HALO = 8


def _conv_taps(x_ref, h_ref, kc):
    x = x_ref[...]
    full = jnp.concatenate([jnp.where(pl.program_id(1) == 0, 0.0, h_ref[...]), x], axis=0)
    return [pltpu.roll(full, s, axis=0)[HALO:, :] for s in range(kc - 1, 0, -1)] + [x]


def _conv_pre(taps, w_ref, b_ref):
    pre = b_ref[...] + w_ref[0:1, :] * taps[0]
    for k in range(1, len(taps)):
        pre = pre + w_ref[k:k + 1, :] * taps[k]
    return pre


def conv_fwd(proj, w, b, cfg, name, tm=1024, tc=512):
    S, CD, KC = cfg.S, cfg.CD, cfg.KC
    tc = min(tc, CD)
    c0 = _blk(cfg.xbc0, tc)
    hb = tm // HALO

    def body(x_ref, h_ref, w_ref, b_ref, o_ref):
        pre = _conv_pre(_conv_taps(x_ref, h_ref, KC), w_ref, b_ref)
        o_ref[...] = pre * _sigmoid(pre)

    return pl.pallas_call(
        body, name=name, grid=(CD // tc, S // tm),
        in_specs=[pl.BlockSpec((tm, tc), lambda c, i: (i, c0 + c)),
                  pl.BlockSpec((HALO, tc), lambda c, i: (jnp.maximum(i * hb - 1, 0), c0 + c)),
                  pl.BlockSpec((KC, tc), lambda c, i: (0, c)),
                  pl.BlockSpec((1, tc), lambda c, i: (0, c))],
        out_specs=pl.BlockSpec((tm, tc), lambda c, i: (i, c)),
        out_shape=jax.ShapeDtypeStruct((S, CD), F32),
        compiler_params=_cp(("parallel", "arbitrary")),
    )(proj, proj, w, b)


def conv_bwd(proj, dxc, w, b, cfg, name, c_off, tm=1024, tc=512):
    S, KC = cfg.S, cfg.KC
    CD = dxc.shape[1]
    tc = min(tc, CD)
    c0 = _blk(cfg.xbc0 + c_off, tc)
    w0 = _blk(c_off, tc)
    hb = tm // HALO
    nrb = S // tm
    last_h = S // HALO - 1

    def body(x_ref, hp_ref, hn_ref, d_ref, dn_ref, w_ref, b_ref, o_ref, gw_ref, gb_ref):
        i = pl.program_id(1)
        x = x_ref[...]
        full = jnp.concatenate([jnp.where(i == 0, 0.0, hp_ref[...]), x, hn_ref[...]], axis=0)
        rows = tm + HALO
        taps = [pltpu.roll(full, s_, axis=0)[HALO:, :] for s_ in range(KC - 1, 0, -1)] + [full[HALO:, :]]
        pre = _conv_pre(taps, w_ref, b_ref)
        sg = _sigmoid(pre)
        d_ext = jnp.concatenate([d_ref[...], jnp.where(i == nrb - 1, 0.0, dn_ref[...])], axis=0)
        dpre = d_ext * (sg * (1.0 + pre * (1.0 - sg)))
        own = dpre[:tm, :]
        acc = w_ref[KC - 1:KC, :] * own
        for j in range(1, KC):
            acc = acc + w_ref[KC - 1 - j:KC - j, :] * pltpu.roll(dpre, rows - j, axis=0)[:tm, :]
        o_ref[...] = acc.astype(BF16)
        gb = jnp.sum(own, axis=0, keepdims=True)
        gws = [jnp.sum(own * taps[k][:tm, :], axis=0, keepdims=True) for k in range(KC)]
        gw = jnp.concatenate(gws + [jnp.zeros((8 - KC, tc), F32)], axis=0)

        @pl.when(i == 0)
        def _():
            gw_ref[...] = gw
            gb_ref[...] = gb

        @pl.when(i > 0)
        def _():
            gw_ref[...] += gw
            gb_ref[...] += gb

    return pl.pallas_call(
        body, name=name, grid=(CD // tc, nrb),
        in_specs=[pl.BlockSpec((tm, tc), lambda c, i: (i, c0 + c)),
                  pl.BlockSpec((HALO, tc), lambda c, i: (jnp.maximum(i * hb - 1, 0), c0 + c)),
                  pl.BlockSpec((HALO, tc), lambda c, i: (jnp.minimum((i + 1) * hb, last_h), c0 + c)),
                  pl.BlockSpec((tm, tc), lambda c, i: (i, c)),
                  pl.BlockSpec((HALO, tc), lambda c, i: (jnp.minimum((i + 1) * hb, last_h), c)),
                  pl.BlockSpec((KC, tc), lambda c, i: (0, w0 + c)),
                  pl.BlockSpec((1, tc), lambda c, i: (0, w0 + c))],
        out_specs=[pl.BlockSpec((tm, tc), lambda c, i: (i, c)),
                   pl.BlockSpec((8, tc), lambda c, i: (0, c)),
                   pl.BlockSpec((1, tc), lambda c, i: (0, c))],
        out_shape=[jax.ShapeDtypeStruct((S, CD), BF16), jax.ShapeDtypeStruct((8, CD), F32),
                   jax.ShapeDtypeStruct((1, CD), F32)],
        compiler_params=_cp(("parallel", "arbitrary")),
    )(proj, proj, proj, dxc, dxc, w, b)


def _pad_lanes(v, width=LANES):
    return jnp.pad(v, ((0, 0), (0, width - v.shape[1])))


def ssd_prep(hn, w_dt, dt_bias, a_log, cfg, name):
    S, L, D = cfg.S, cfg.L, cfg.D
    R = CPS * L

    def body(h_ref, wd_ref, b_ref, al_ref, raw_ref, dt_ref, ac_ref):
        raw = _dot(h_ref[...], wd_ref[...], NN)
        raw_ref[...] = raw
        x = raw + b_ref[...]
        dt = jnp.maximum(x, 0.0) + jnp.log(1.0 + jnp.exp(-jnp.abs(x)))
        da = dt * (-jnp.exp(al_ref[...]))
        li = lax.broadcasted_iota(jnp.int32, (L, L), 0)
        si = lax.broadcasted_iota(jnp.int32, (L, L), 1)
        tri = jnp.where(li >= si, 1.0, 0.0).astype(F32)
        dt_ref[...] = dt
        for ci in range(CPS):
            rows = slice(ci * L, (ci + 1) * L)
            ac_ref[rows, :] = lax.dot_general(tri, da[rows, :], ((NN), ((), ())), precision=lax.Precision.HIGHEST,
                                              preferred_element_type=F32)

    row = pl.BlockSpec((R, LANES), lambda i: (i, 0))
    vec = pl.BlockSpec((1, LANES), lambda i: (0, 0))
    sh = jax.ShapeDtypeStruct((S, LANES), F32)
    return pl.pallas_call(
        body, name=name, grid=(S // R,),
        in_specs=[pl.BlockSpec((R, D), lambda i: (i, 0)), pl.BlockSpec((D, LANES), lambda i: (0, 0)), vec, vec],
        out_specs=[row, row, row], out_shape=[sh, sh, sh],
        compiler_params=_cp(("parallel",)),
    )(hn, w_dt, dt_bias, a_log)


def _spread(v, n):
    return jnp.broadcast_to(v[:, :, None], v.shape + (n,)).reshape(v.shape[0], v.shape[1] * n)


def _head_selectors(cfg):
    def sel(width):
        head = jnp.arange(LANES)[None, :, None]
        slot = jnp.arange(cfg.SG)[:, None, None] * cfg.HPG + (jnp.arange(cfg.HPG * width) // width)[None, None, :]
        return (head == slot).astype(BF16)
    return sel(cfg.P), sel(LANES)


def _spread_heads(v, sel):
    n = v.shape[0]
    hi = v.astype(BF16)
    r1 = v - hi.astype(F32)
    mid = r1.astype(BF16)
    lo = (r1 - mid.astype(F32)).astype(BF16)
    out = _dot(jnp.concatenate([hi, mid, lo], axis=0), sel, NN)
    return out[:n] + out[n:2 * n] + out[2 * n:]


def _pair_lanes(wide, hpg, p):
    low = lax.broadcasted_iota(jnp.int32, (wide.shape[0], LANES), 1) < p
    return jnp.concatenate([jnp.where(low, wide[:, 2 * jp * LANES:(2 * jp + 1) * LANES],
                                      wide[:, (2 * jp + 1) * LANES:(2 * jp + 2) * LANES])
                            for jp in range(hpg // 2)], axis=1)


def _pair_select(halves, p):
    low = lax.broadcasted_iota(jnp.int32, halves[0].shape, 1) < p
    return jnp.where(low, halves[0], halves[1])


def _head_rows(row, hpg, p):
    return jnp.concatenate([jnp.broadcast_to(row[:, j * LANES:(j + 1) * LANES], (p, LANES)) for j in range(hpg)],
                           axis=0)


def _segment_sums(t, sel):
    r = t.shape[0]
    hi = t.astype(BF16)
    out = _dot(jnp.concatenate([hi, (t - hi.astype(F32)).astype(BF16)], axis=0), sel, NT)
    return out[:r] + out[r:]


def ssd_scan_fwd(xc, dt, acum, act, sel_p, sel_l, cfg, name):
    S, L, P, SN, HPG, SG, SI = cfg.S, cfg.L, cfg.P, cfg.SN, cfg.HPG, cfg.SG, cfg.SI
    nc = S // L
    GW = HPG * P
    bcol, ccol = _blk(SI, SN), _blk(SI + cfg.GN, SN)
    assert nc % CPS == 0

    def body(xs_ref, b_ref, c_ref, dtn_ref, acn_ref, at_ref, sp_ref, sl_ref, y_ref, st_ref, st):
        @pl.when(pl.program_id(1) == 0)
        def _():
            st[...] = jnp.zeros_like(st)

        causal = lax.broadcasted_iota(jnp.int32, (L, L), 0) >= lax.broadcasted_iota(jnp.int32, (L, L), 1)
        for ci in range(CPS):
            rows = slice(ci * L, (ci + 1) * L)
            acn = acn_ref[rows, :]
            dts = _spread_heads(dtn_ref[rows, :], sp_ref[0])
            acs = _spread_heads(acn, sl_ref[0])
            a_p = _pair_lanes(acs, HPG, P)
            s0 = st[...]
            st_ref[ci] = s0.reshape(HPG, P, SN)
            B = b_ref[rows, :].astype(BF16)
            C = c_ref[rows, :].astype(BF16)
            G = _dot(C, B, NT)
            xdt = xs_ref[rows, :] * dts
            xdtb = xdt.astype(BF16)
            ws = jnp.exp(a_p[L - 1:L, :] - a_p)
            yo = jnp.exp(a_p) * _dot(C, s0.astype(BF16), NT)
            yd = []
            for jp in range(HPG // 2):
                x_pair = xdtb[:, jp * LANES:(jp + 1) * LANES]
                halves = []
                for j in (2 * jp, 2 * jp + 1):
                    dm = jnp.where(causal, jnp.exp(acs[:, j * LANES:(j + 1) * LANES] - at_ref[j:j + 1, rows]), 0.0)
                    halves.append(_dot((G * dm).astype(BF16), x_pair, NN))
                yd.append(_pair_select(halves, P))
            y_ref[rows, :] = jnp.concatenate(yd, axis=1) + yo
            st[...] = _head_rows(jnp.exp(acs[L - 1:L, :]), HPG, P) * s0 + _dot((xdt * ws).astype(BF16), B, TN)

    R = CPS * L
    y, states = pl.pallas_call(
        body, name=name, grid=(SG, nc // CPS),
        in_specs=[pl.BlockSpec((R, GW), lambda g, c: (c, g)),
                  pl.BlockSpec((R, SN), lambda g, c: (c, bcol + g)),
                  pl.BlockSpec((R, SN), lambda g, c: (c, ccol + g)),
                  pl.BlockSpec((R, LANES), lambda g, c: (c, 0)),
                  pl.BlockSpec((R, LANES), lambda g, c: (c, 0)),
                  pl.BlockSpec((HPG, R), lambda g, c: (g, c)),
                  pl.BlockSpec((1, LANES, GW), lambda g, c: (g, 0, 0)),
                  pl.BlockSpec((1, LANES, HPG * LANES), lambda g, c: (g, 0, 0))],
        out_specs=[pl.BlockSpec((R, GW), lambda g, c: (c, g)),
                   pl.BlockSpec((CPS, HPG, P, SN), lambda g, c: (c, g, 0, 0))],
        out_shape=[jax.ShapeDtypeStruct((S, SI), F32), jax.ShapeDtypeStruct((nc, cfg.SH, P, SN), F32)],
        scratch_shapes=[pltpu.VMEM((GW, SN), F32)],
        compiler_params=_cp(("parallel", "arbitrary")),
    )(xc, xc, xc, dt, acum, act, sel_p, sel_l)
    return y, states


def ssd_scan_bwd(xc, dt, acum, act, sel_p, sel_l, states, y, dy, dvec, cfg, name, side):
    S, L, P, SN, HPG, SG, SI = cfg.S, cfg.L, cfg.P, cfg.SN, cfg.HPG, cfg.SG, cfg.SI
    nc = S // L
    GW = HPG * P
    bcol, ccol = _blk(SI, SN), _blk(SI + cfg.GN, SN)

    def rc(c):
        return nc // CPS - 1 - c

    def body(xs_ref, b_ref, c_ref, dtn_ref, acn_ref, at_ref, sp_ref, sl_ref, st_ref, y_ref, dy_ref, dk_ref,
             dxs_ref, db_ref, dc_ref, dac_ref, dxsum_ref, dst):
        @pl.when(pl.program_id(1) == 0)
        def _():
            dst[...] = jnp.zeros_like(dst)

        sel = sp_ref[0]
        causal = lax.broadcasted_iota(jnp.int32, (L, L), 0) >= lax.broadcasted_iota(jnp.int32, (L, L), 1)
        low = lax.broadcasted_iota(jnp.int32, (L, LANES), 1) < P
        is_last = lax.broadcasted_iota(jnp.int32, (L, LANES), 0) == L - 1
        ones = jnp.ones((16, SN), BF16)
        for ci in reversed(range(CPS)):
            rows = slice(ci * L, (ci + 1) * L)
            acn = acn_ref[rows, :]
            dts = _spread_heads(dtn_ref[rows, :], sel)
            acs = _spread_heads(acn, sl_ref[0])
            a_p = _pair_lanes(acs, HPG, P)
            B = b_ref[rows, :].astype(BF16)
            C = c_ref[rows, :].astype(BF16)
            G = _dot(C, B, NT)
            xs = xs_ref[rows, :]
            dY = dy_ref[rows, :].astype(F32)
            xdt = xs * dts
            xdtb = xdt.astype(BF16)
            dYb = dY.astype(BF16)
            s0 = st_ref[ci].reshape(GW, SN)
            s0b = s0.astype(BF16)
            ds1 = dst[...]
            ds1b = ds1.astype(BF16)
            ws = jnp.exp(a_p[L - 1:L, :] - a_p)
            dR = (jnp.exp(a_p) * dY).astype(BF16)
            dX2 = ws * _dot(B, ds1b, NT)
            dgsum = jnp.zeros((L, L), F32)
            dX1, yd = [], []
            for jp in range(HPG // 2):
                lanes = slice(jp * LANES, (jp + 1) * LANES)
                x_pair, dy_pair = xdtb[:, lanes], dYb[:, lanes]
                h1, h2 = [], []
                for h, j in enumerate((2 * jp, 2 * jp + 1)):
                    dm = jnp.where(causal, jnp.exp(acs[:, j * LANES:(j + 1) * LANES] - at_ref[j:j + 1, rows]), 0.0)
                    mine = low if h == 0 else jnp.logical_not(low)
                    dgsum = dgsum + _dot(jnp.where(mine, dy_pair, jnp.zeros_like(dy_pair)), x_pair, NT) * dm
                    Mb = (G * dm).astype(BF16)
                    h1.append(_dot(Mb, dy_pair, TN))
                    h2.append(_dot(Mb, x_pair, NN))
                dX1.append(_pair_select(h1, P))
                yd.append(_pair_select(h2, P))
            dX1 = jnp.concatenate(dX1, axis=1)
            dX = dX1 + dX2
            pair = (dYb.astype(F32) - dY) * jnp.concatenate(yd, axis=1) - xdtb.astype(F32) * dX1
            through = _segment_sums(xdt * dX2, sel)
            u = ds1 * s0
            u_hi = u.astype(BF16)
            u_rows = _dot(ones, jnp.concatenate([u_hi, (u - u_hi.astype(F32)).astype(BF16)], axis=0), NT)
            u_rows = u_rows[:, :GW] + u_rows[:, GW:]
            at_end = jnp.exp(acn[L - 1:L, :]) * _segment_sums(u_rows, sel)[0:1, :] + \
                jnp.sum(through, axis=0, keepdims=True)
            dac_ref[0, rows, :] = _segment_sums(dY * y_ref[rows, :] + pair, sel) - through + \
                jnp.where(is_last, at_end, 0.0)
            dxsum_ref[0, rows, :] = _segment_sums(dX * xs, sel)
            dxs_ref[rows, :] = dX * dts + dk_ref[...] * dY
            dst[...] = _head_rows(jnp.exp(acs[L - 1:L, :]), HPG, P) * ds1 + _dot(dR, C, TN)
            dgb = dgsum.astype(BF16)
            dc_ref[rows, :] = _dot(dR, s0b, NN) + _dot(dgb, B, NN)
            db_ref[rows, :] = _dot((xdt * ws).astype(BF16), ds1b, NN) + _dot(dgb, C, TN)

    R = CPS * L
    wide = pl.BlockSpec((R, GW), lambda g, c: (rc(c), g))
    colspec = pl.BlockSpec((1, R, LANES), lambda g, c: (g, rc(c), 0))
    whole = pl.BlockSpec(memory_space=pl.ANY)
    arrs, gathers = side
    grid = (SG, nc // CPS)
    res = pl.pallas_call(
        with_exchange(body, 12, 5, gathers, grid), name=name, grid=grid,
        in_specs=[wide,
                  pl.BlockSpec((R, SN), lambda g, c: (rc(c), bcol + g)),
                  pl.BlockSpec((R, SN), lambda g, c: (rc(c), ccol + g)),
                  pl.BlockSpec((R, LANES), lambda g, c: (rc(c), 0)),
                  pl.BlockSpec((R, LANES), lambda g, c: (rc(c), 0)),
                  pl.BlockSpec((HPG, R), lambda g, c: (g, rc(c))),
                  pl.BlockSpec((1, LANES, GW), lambda g, c: (g, 0, 0)),
                  pl.BlockSpec((1, LANES, HPG * LANES), lambda g, c: (g, 0, 0)),
                  pl.BlockSpec((CPS, HPG, P, SN), lambda g, c: (rc(c), g, 0, 0)),
                  wide, wide,
                  pl.BlockSpec((1, GW), lambda g, c: (0, g))] + [whole] * len(arrs),
        out_specs=[wide,
                   pl.BlockSpec((R, SN), lambda g, c: (rc(c), g)),
                   pl.BlockSpec((R, SN), lambda g, c: (rc(c), g)),
                   colspec, colspec] + [whole] * len(arrs),
        out_shape=[jax.ShapeDtypeStruct((S, SI), F32), jax.ShapeDtypeStruct((S, cfg.GN), F32),
                   jax.ShapeDtypeStruct((S, cfg.GN), F32),
                   jax.ShapeDtypeStruct((SG, S, LANES), F32), jax.ShapeDtypeStruct((SG, S, LANES), F32)] +
        _exchange_shapes(arrs, gathers),
        scratch_shapes=[pltpu.VMEM((GW, SN), F32)] + _exchange_sems(len(arrs)),
        compiler_params=_cp(("arbitrary", "arbitrary")),
    )(xc, xc, xc, dt, acum, act, sel_p, sel_l, states, y, dy, dvec, *arrs)
    return res[:5], res[5:]


def dt_bwd(dac, dxsum, dt_raw, dt, dt_bias, a_log, cfg, name):
    S, L, SG = cfg.S, cfg.L, cfg.SG

    def body(da_ref, dx_ref, x_ref, dt_ref, b_ref, al_ref, o_ref, gb_ref, ga_ref):
        a = -jnp.exp(al_ref[...])
        dtv = dt_ref[...]
        dxs = jnp.sum(dx_ref[...], axis=0)
        upper = jnp.where(lax.broadcasted_iota(jnp.int32, (L, L), 1) >= lax.broadcasted_iota(jnp.int32, (L, L), 0),
                          1.0, 0.0).astype(F32)
        dda = lax.dot_general(upper, jnp.sum(da_ref[...], axis=0), (NN, ((), ())), precision=lax.Precision.HIGHEST,
                              preferred_element_type=F32)
        draw = (dxs + dda * a) * _sigmoid(x_ref[...] + b_ref[...])
        o_ref[...] = draw.astype(BF16)
        gb = jnp.sum(draw, axis=0, keepdims=True)
        ga = jnp.sum(dda * dtv, axis=0, keepdims=True) * a

        @pl.when(pl.program_id(0) == 0)
        def _():
            gb_ref[...] = gb
            ga_ref[...] = ga

        @pl.when(pl.program_id(0) > 0)
        def _():
            gb_ref[...] += gb
            ga_ref[...] += ga

    row = pl.BlockSpec((L, LANES), lambda i: (i, 0))
    vec = pl.BlockSpec((1, LANES), lambda i: (0, 0))
    return pl.pallas_call(
        body, name=name, grid=(S // L,),
        in_specs=[pl.BlockSpec((SG, L, LANES), lambda i: (0, i, 0))] * 2 + [row, row, vec, vec],
        out_specs=[row, vec, vec],
        out_shape=[jax.ShapeDtypeStruct((S, LANES), BF16), jax.ShapeDtypeStruct((1, LANES), F32),
                   jax.ShapeDtypeStruct((1, LANES), F32)],
        compiler_params=_cp(("arbitrary",)),
    )(dac, dxsum, dt_raw, dt, dt_bias, a_log)


def gated_norm_fwd(y, xc, proj, dvec, nw, cfg, name, tm=128):
    S, SI = cfg.S, cfg.SI

    def body(y_ref, xs_ref, z_ref, d_ref, w_ref, o_ref):
        z = z_ref[...]
        yg = (y_ref[...] + d_ref[...] * xs_ref[...]) * (z * _sigmoid(z))
        r = lax.rsqrt(jnp.mean(yg * yg, axis=-1, keepdims=True) + RMS_EPS)
        o_ref[...] = ((yg * r) * w_ref[...]).astype(BF16)

    row = pl.BlockSpec((tm, SI), lambda i: (i, 0))
    vec = pl.BlockSpec((1, SI), lambda i: (0, 0))
    return pl.pallas_call(
        body, name=name, grid=(S // tm,),
        in_specs=[row, row, pl.BlockSpec((tm, SI), lambda i: (i, _blk(cfg.zs0, SI))), vec, vec],
        out_specs=row, out_shape=jax.ShapeDtypeStruct((S, SI), BF16),
        compiler_params=_cp(("parallel",)),
    )(y, xc, proj, dvec, nw)


def gated_norm_bwd(dyn, y, xc, proj, dvec, nw, cfg, name, tm=128):
    S, SI = cfg.S, cfg.SI

    def body(dn_ref, y_ref, xs_ref, z_ref, d_ref, w_ref, dy_ref, dz_ref, gw_ref, gd_ref):
        z = z_ref[...]
        s = _sigmoid(z)
        sz = z * s
        xs = xs_ref[...]
        yf = y_ref[...] + d_ref[...] * xs
        yg = yf * sz
        r = lax.rsqrt(jnp.mean(yg * yg, axis=-1, keepdims=True) + RMS_EPS)
        dn = dn_ref[...].astype(F32)
        g = dn * w_ref[...]
        dyg = r * g - yg * (r * r * r) * jnp.mean(g * yg, axis=-1, keepdims=True)
        dy = dyg * sz
        dy_ref[...] = dy.astype(BF16)
        dz_ref[...] = (dyg * yf * (s * (1.0 + z * (1.0 - s)))).astype(BF16)
        gw = jnp.sum(dn * (yg * r), axis=0, keepdims=True)
        gd = jnp.sum(dy * xs, axis=0, keepdims=True)

        @pl.when(pl.program_id(0) == 0)
        def _():
            gw_ref[...] = gw
            gd_ref[...] = gd

        @pl.when(pl.program_id(0) > 0)
        def _():
            gw_ref[...] += gw
            gd_ref[...] += gd

    row = pl.BlockSpec((tm, SI), lambda i: (i, 0))
    vec = pl.BlockSpec((1, SI), lambda i: (0, 0))
    return pl.pallas_call(
        body, name=name, grid=(S // tm,),
        in_specs=[row, row, row, pl.BlockSpec((tm, SI), lambda i: (i, _blk(cfg.zs0, SI))), vec, vec],
        out_specs=[row, row, vec, vec],
        out_shape=[jax.ShapeDtypeStruct((S, SI), BF16), jax.ShapeDtypeStruct((S, SI), BF16),
                   jax.ShapeDtypeStruct((1, SI), F32), jax.ShapeDtypeStruct((1, SI), F32)],
        compiler_params=_cp(("arbitrary",)),
    )(dyn, y, xc, proj, dvec, nw)


def _shard_columns(cfg, main, dt):
    dt0 = 4 * cfg.AW + cfg.SI + cfg.CD
    ws = cfg.N_IN // N_DEV
    out = []
    for k in range(N_DEV):
        lo, hi, parts = k * ws, (k + 1) * ws, []
        if lo < dt0:
            parts.append((main, lo, min(hi, dt0)))
        if lo < dt0 + cfg.SH and hi > dt0:
            parts.append((dt, max(lo, dt0) - dt0, min(hi, dt0 + cfg.SH) - dt0))
        if hi > dt0 + cfg.SH:
            parts.append((main, max(lo, dt0 + cfg.SH) - cfg.SH, hi - cfg.SH))
        out.append(parts)
    return out


def local_step(cfg, x, tgt, norm_w, conv_w, conv_b, dt_bias, a_log, d_skip, ssm_norm_w, final_norm_w,
               w_main, w_dt, shards, dt0):
    S, D = cfg.S, cfg.D
    slopes = _slope_table(cfg)
    dt_bias_p = _pad_lanes(dt_bias)
    a_log_p = _pad_lanes(a_log)
    dvec = _spread(d_skip, cfg.P)

    hn = rmsnorm_fwd(x, norm_w, "rmsnorm_fwd")
    proj, gathered = matmul(hn, w_main, 'nn', 1024, 2048, 2048, F32, "in_proj", side=(shards, [True] * 3))
    w_attn, w_ssm, w_out = gathered[0].reshape(cfg.AW, D), gathered[1].reshape(cfg.SI, D), gathered[2].reshape(D, D)
    o_a, o_mix, ltot = attn_fused_fwd(proj, slopes, cfg, "attn_fwd")
    xc = conv_fwd(proj, conv_w, conv_b, cfg, "conv_fwd")
    dt_raw, dt, acum = ssd_prep(hn, w_dt, dt_bias_p, a_log_p, cfg, "ssd_prep")
    act = acum[:, :cfg.SH].T
    sel_p, sel_l = _head_selectors(cfg)
    y, states = ssd_scan_fwd(xc, dt, acum, act, sel_p, sel_l, cfg, "ssd_scan_fwd")
    y_n = gated_norm_fwd(y, xc, proj, dvec, ssm_norm_w, cfg, "gated_norm_fwd")
    a_out, s_out, merged = branch_merge(o_a, w_attn, y_n, w_ssm, proj, cfg, "branch_merge")
    dout, loss_p, g_final_w = out_proj_final(merged, w_out, x, final_norm_w.reshape(1, D), tgt, "out_proj_final")

    g_w_out = matmul(merged, dout, 'tn', 1024, 1024, 2048, BF16, "g_w_out")
    da_out, ds_out, dga, dgs = merge_bwd(dout, w_out, a_out, s_out, proj, cfg, "merge_bwd")
    g_w_attn = matmul(o_a, da_out, 'tn', 1024, 1024, 2048, BF16, "g_w_attn")
    g_w_ssm = matmul(y_n, ds_out, 'tn', 1024, 1024, 2048, BF16, "g_w_ssm")
    do_a = matmul(da_out, w_attn, 'nt', 512, 1024, 2048, BF16, "d_o_a")
    dyn = matmul(ds_out, w_ssm, 'nt', 512, 1024, 2048, BF16, "d_y_n")
    dy, dz_s, g_ssm_norm, g_dvec = gated_norm_bwd(dyn, y, xc, proj, dvec, ssm_norm_w, cfg, "gated_norm_bwd")
    sends = [g.reshape((N_DEV, g.shape[0] // N_DEV, D)) for g in (g_w_attn, g_w_ssm, g_w_out)]
    (dxs, dB, dC, dac_g, dxsum_g), (r_attn, r_ssm, r_out) = ssd_scan_bwd(
        xc, dt, acum, act, sel_p, sel_l, states, y, dy, dvec, cfg, "ssd_scan_bwd", side=(sends, [False] * 3))
    ddt_raw, g_dt_bias, g_a_log = dt_bwd(dac_g, dxsum_g, dt_raw, dt, dt_bias_p, a_log_p, cfg, "dt_bwd")
    dxbc, g_cw, g_cb = [], [], []
    for nm, piece, c_off in (("xs", dxs, 0), ("b", dB, cfg.SI), ("c", dC, cfg.SI + cfg.GN)):
        dx, gw, gb = conv_bwd(proj, piece, conv_w, conv_b, cfg, "conv_bwd_" + nm, c_off)
        dxbc.append(dx)
        g_cw.append(gw)
        g_cb.append(gb)
    g_conv_w, g_conv_b = jnp.concatenate(g_cw, axis=1), jnp.concatenate(g_cb, axis=1)
    dq, dk, dv, dz_a = attn_fused_bwd(proj, do_a, o_mix, ltot, slopes, cfg, "attn_bwd")
    dproj = jnp.concatenate([dq, dk, dv, dz_a, dz_s] + dxbc + [dga, dgs], axis=1)
    def slabs(g_main, g_dt):
        return jnp.stack([jnp.concatenate([g[:, lo:hi] for g, lo, hi in parts], axis=1)
                          for parts in _shard_columns(cfg, g_main, g_dt)])

    cut = D // 4
    g_w_dt = matmul(hn, ddt_raw, 'tn', 1024, 128, 2048, BF16, "g_w_dt")
    g_w_top = matmul(hn, dproj, 'tn', 512, 2048, 2048, BF16, "g_w_main_top", a_cols=(0, cut))
    g_w_bot, (r_top,) = matmul(hn, dproj, 'tn', 512, 2048, 2048, BF16, "g_w_main_rest", a_cols=(cut, D - cut),
                               side=([slabs(g_w_top, g_w_dt[:cut])], [False]))
    dhn_a, (r_bot,) = matmul(dproj, w_main, 'nt', 1024, 1024, 2048, F32, "d_hn",
                             side=([slabs(g_w_bot, g_w_dt[cut:])], [False]))
    grad_x, g_norm_w = rmsnorm_bwd(dhn_a, ddt_raw, w_dt, x, norm_w, dout, "rmsnorm_bwd")

    g_d_skip = jnp.sum(g_dvec.reshape(cfg.SH, cfg.P), axis=1).reshape(1, cfg.SH)
    small = dict(norm_w=g_norm_w, conv_b=g_conv_b, dt_bias=g_dt_bias[:, :cfg.SH], a_log=g_a_log[:, :cfg.SH],
                 d_skip=g_d_skip, ssm_norm_w=g_ssm_norm, final_norm_w=g_final_w, conv_w=g_conv_w[:cfg.KC])
    return loss_p, grad_x, small, dict(w_in=[r_top, r_bot], w_attn=[r_attn], w_ssm=[r_ssm], w_out=[r_out])


def _mesh_pos():
    return lax.axis_index("x"), lax.axis_index("y"), lax.axis_index("c")


def _flat(pos):
    return 4 * pos[0] + 2 * pos[1] + pos[2]


def _exchange_shapes(arrs, gathers):
    return [jax.ShapeDtypeStruct(((N_DEV,) + a.shape) if g else a.shape, a.dtype) for a, g in zip(arrs, gathers)]


def _exchange_sems(n):
    return [pltpu.SemaphoreType.DMA((n * (N_DEV - 1),)), pltpu.SemaphoreType.DMA((n * (N_DEV - 1),)),
            pltpu.SemaphoreType.DMA((n,))]


def _exchange_copies(ins, outs, gathers, send_sems, recv_sems, loc_sems):
    pos = _mesh_pos()
    me = _flat(pos)
    starts, waits = [], []
    for a in range(len(ins)):
        mine = ins[a] if gathers[a] else ins[a].at[me]
        loc = pltpu.make_async_copy(mine, outs[a].at[me], loc_sems.at[a])
        starts.append(loc)
        waits.append(loc)
        for k in range(1, N_DEV):
            flip = ((k >> 2) & 1, (k >> 1) & 1, k & 1)
            peer = tuple(1 - p if f else p for p, f in zip(pos, flip))
            pk = _flat(peer)
            src = ins[a] if gathers[a] else ins[a].at[pk]
            sems = dict(send_sem=send_sems.at[a * (N_DEV - 1) + k - 1], recv_sem=recv_sems.at[a * (N_DEV - 1) + k - 1],
                        device_id=peer, device_id_type=pl.DeviceIdType.MESH)
            starts.append(pltpu.make_async_remote_copy(src_ref=src, dst_ref=outs[a].at[me], **sems))
            waits.append(pltpu.make_async_remote_copy(src_ref=src, dst_ref=outs[a].at[pk], **sems))
    return starts, waits


def exchange(arrs, gathers, name):
    n = len(arrs)

    def body(*refs):
        starts, waits = _exchange_copies(refs[:n], refs[n:2 * n], gathers, *refs[2 * n:])
        for cp in starts:
            cp.start()
        for cp in waits:
            cp.wait()

    hbm = pl.BlockSpec(memory_space=pltpu.HBM)
    return pl.pallas_call(
        body, name=name, in_specs=[hbm] * n, out_specs=[hbm] * n, out_shape=_exchange_shapes(arrs, gathers),
        scratch_shapes=_exchange_sems(n),
    )(*arrs)


def gather_two_level(arrs, chunks, name):
    n = len(arrs)

    def body(*refs):
        ins, outs = refs[:n], refs[n:2 * n]
        send_sems, recv_sems, loc_sems = refs[2 * n:]
        x, y, c = _mesh_pos()
        me, sib = (x, y, c), (x, y, 1 - c)
        chips = [(1 - x, y), (x, 1 - y), (1 - x, 1 - y)]
        plan, base = [], 0
        for a in range(n):
            step = arrs[a].shape[0] // chunks[a]
            for q in range(chunks[a]):
                plan.append((a, pl.ds(q * step, step), base))
                base += N_DEV - 1

        def copy(a, rows, sem, block, to, own=False):
            dst = outs[a].at[_flat(block), rows]
            return pltpu.make_async_remote_copy(
                src_ref=ins[a].at[rows] if own else dst, dst_ref=dst, send_sem=send_sems.at[sem],
                recv_sem=recv_sems.at[sem], device_id=to, device_id_type=pl.DeviceIdType.MESH)

        local = [pltpu.make_async_copy(ins[a], outs[a].at[_flat(me)], loc_sems.at[a]) for a in range(n)]
        for cp in local:
            cp.start()
        sent = []
        for a, rows, s in plan:
            sent.append(copy(a, rows, s, me, sib, own=True))
            sent += [copy(a, rows, s + 1 + j, me, (*chip, c), own=True) for j, chip in enumerate(chips)]
        for cp in sent:
            cp.start()
        for a, rows, s in plan:
            for j, chip in enumerate(chips):
                copy(a, rows, s + 1 + j, (*chip, c), me).wait_recv()
                passed = copy(a, rows, s + 4 + j, (*chip, c), sib)
                passed.start()
                sent.append(passed)
        for a, rows, s in plan:
            copy(a, rows, s, sib, me).wait_recv()
            for j, chip in enumerate(chips):
                copy(a, rows, s + 4 + j, (*chip, 1 - c), me).wait_recv()
        for cp in sent:
            cp.wait_send()
        for cp in local:
            cp.wait()

    hbm = pl.BlockSpec(memory_space=pltpu.HBM)
    nsem = (N_DEV - 1) * sum(chunks)
    return pl.pallas_call(
        body, name=name, in_specs=[hbm] * n, out_specs=[hbm] * n, out_shape=_exchange_shapes(arrs, [True] * n),
        scratch_shapes=[pltpu.SemaphoreType.DMA((nsem,)), pltpu.SemaphoreType.DMA((nsem,)),
                        pltpu.SemaphoreType.DMA((n,))],
    )(*arrs)


def with_exchange(body, n_in, n_out, gathers, grid):
    n = len(gathers)

    def wrapped(*refs):
        ins, sends = refs[:n_in], refs[n_in:n_in + n]
        outs, recvs = refs[n_in + n:n_in + n + n_out], refs[n_in + 2 * n + n_out - n:n_in + 2 * n + n_out]
        scratch, sems = refs[n_in + 2 * n + n_out:-3], refs[-3:]
        ids = [pl.program_id(d) for d in range(len(grid))]
        first = functools.reduce(jnp.logical_and, [i == 0 for i in ids])
        last = functools.reduce(jnp.logical_and, [i == g - 1 for i, g in zip(ids, grid)])

        @pl.when(first)
        def _():
            for cp in _exchange_copies(sends, recvs, gathers, *sems)[0]:
                cp.start()

        body(*ins, *outs, *scratch)

        @pl.when(last)
        def _():
            for cp in _exchange_copies(sends, recvs, gathers, *sems)[1]:
                cp.wait()

    return wrapped


def adamw(g_src, w, m, v, summed, name, tr=64):
    R, C = w.shape
    tr = min(tr, R)
    assert R % tr == 0
    parts = g_src if summed else [g_src]
    starts = [sum(p.shape[1] for p in parts[:k]) // tr for k in range(len(parts))] if summed else [0]
    counts = [p.shape[1] // tr for p in parts] if summed else [R // tr]

    def body(*refs):
        g_refs, (w_ref, m_ref, v_ref, g_out, d_out, m_out, v_out) = refs[:len(parts)], refs[len(parts):]
        if summed:
            g = None
            for k, g_ref in enumerate(g_refs):
                gk = g_ref[0].astype(F32)
                for j in range(1, N_DEV):
                    gk = gk + g_ref[j].astype(F32)
                g = gk if g is None else jnp.where(pl.program_id(0) >= starts[k], gk, g)
        else:
            g = g_refs[0][...]
        mn = ADAM_B1 * m_ref[...] + (1.0 - ADAM_B1) * g
        vn = ADAM_B2 * v_ref[...] + (1.0 - ADAM_B2) * (g * g)
        m_hat = mn / (1.0 - ADAM_B1 ** ADAM_STEP)
        v_hat = vn / (1.0 - ADAM_B2 ** ADAM_STEP)
        g_out[...] = g
        d_out[...] = -ADAM_LR * (m_hat / (jnp.sqrt(v_hat) + ADAM_EPS) + ADAM_WD * w_ref[...])
        m_out[...] = mn
        v_out[...] = vn

    row = pl.BlockSpec((tr, C), lambda i: (i, 0))
    if summed:
        gspecs = [pl.BlockSpec((N_DEV, tr, C), lambda i, s=st, n=nb: (0, jnp.clip(i - s, 0, n - 1), 0))
                  for st, nb in zip(starts, counts)]
    else:
        gspecs = [row]
    sh = jax.ShapeDtypeStruct((R, C), F32)
    return pl.pallas_call(
        body, name=name, grid=(R // tr,), in_specs=gspecs + [row, row, row], out_specs=[row] * 4, out_shape=[sh] * 4,
        compiler_params=_cp(("parallel",)),
    )(*parts, w, m, v)


SMALL = ('norm_w', 'conv_b', 'dt_bias', 'a_log', 'd_skip', 'ssm_norm_w', 'final_norm_w')


def _rows(n):
    return -(-n // (8 * LANES)) * 8


def _pack(vals):
    parts = []
    for a in vals:
        f = a.reshape(-1)
        parts.append(jnp.pad(f, (0, _rows(f.size) * LANES - f.size)).reshape(-1, LANES))
    return jnp.concatenate(parts, axis=0)


def _unpack(packed, shapes):
    out, r = [], 0
    for s in shapes:
        n = math.prod(s)
        out.append(packed[r:r + _rows(n)].reshape(-1)[:n].reshape(s))
        r += _rows(n)
    return out


def kernel(x, norm_w, w_in, conv_w, conv_b, dt_bias, a_log, d_skip, ssm_norm_w, w_attn_branch, w_ssm_branch, w_out, final_norm_w, loss_target, m_norm_w, m_w_in, m_conv_w, m_conv_b, m_dt_bias, m_a_log, m_d_skip, m_ssm_norm_w, m_w_attn_branch, m_w_ssm_branch, m_w_out, m_final_norm_w, v_norm_w, v_w_in, v_conv_w, v_conv_b, v_dt_bias, v_a_log, v_d_skip, v_ssm_norm_w, v_w_attn_branch, v_w_ssm_branch, v_w_out, v_final_norm_w):
    cfg = CFG
    D, SH = cfg.D, cfg.SH
    me = _flat(_mesh_pos())
    dt0 = 4 * cfg.AW + cfg.SI + cfg.CD
    ws = w_in.shape[-1]

    g_in, g_cw = gather_two_level([w_in[0].astype(BF16), conv_w[0]], [4, 1], "gather_w_in")
    main_cols, dt_cols = [], []
    for k, parts in enumerate(_shard_columns(cfg, "main", "dt")):
        at = 0
        for which, lo, hi in parts:
            (main_cols if which == "main" else dt_cols).append(g_in[k][:, at:at + hi - lo])
            at += hi - lo
    w_main = jnp.concatenate(main_cols, axis=1)
    w_dt = _pad_lanes(jnp.concatenate(dt_cols, axis=1))
    conv_full = g_cw.transpose(1, 0, 2).reshape(cfg.KC, cfg.CD)
    shards = [w_attn_branch[0].astype(BF16), w_ssm_branch[0].astype(BF16), w_out[0].astype(BF16)]

    loss_p, grad_x, small, recv = local_step(
        cfg, x[0], loss_target[0], norm_w, conv_full, conv_b, dt_bias, a_log, d_skip,
        ssm_norm_w, final_norm_w, w_main, w_dt, shards, dt0)

    upd = {}
    upd['w_in'] = adamw(recv['w_in'], w_in[0], m_w_in[0], v_w_in[0], True, "adamw_w_in")
    upd['w_attn_branch'] = adamw(recv['w_attn'], w_attn_branch[0], m_w_attn_branch[0], v_w_attn_branch[0], True,
                                 "adamw_w_attn")
    upd['w_ssm_branch'] = adamw(recv['w_ssm'], w_ssm_branch[0], m_w_ssm_branch[0], v_w_ssm_branch[0], True,
                                "adamw_w_ssm")
    upd['w_out'] = adamw(recv['w_out'], w_out[0], m_w_out[0], v_w_out[0], True, "adamw_w_out")

    extra = [jnp.zeros((cfg.KC, cfg.CD), F32), jnp.zeros((1, 1), F32)]
    shapes = [small[n].shape for n in SMALL] + [e.shape for e in extra]
    part = _pack([small[n] for n in SMALL] + [small['conv_w'], loss_p[:, :1]])
    gathered, = exchange([part], [True], "gather_small")
    given = dict(norm_w=(norm_w, m_norm_w, v_norm_w), conv_b=(conv_b, m_conv_b, v_conv_b),
                 dt_bias=(dt_bias, m_dt_bias, v_dt_bias), a_log=(a_log, m_a_log, v_a_log),
                 d_skip=(d_skip, m_d_skip, v_d_skip), ssm_norm_w=(ssm_norm_w, m_ssm_norm_w, v_ssm_norm_w),
                 final_norm_w=(final_norm_w, m_final_norm_w, v_final_norm_w))
    packed = [_pack([given[n][t] for n in SMALL] + extra) for t in range(3)]
    outs = adamw([gathered], *packed, True, "adamw_small", tr=part.shape[0])
    unpacked = [_unpack(o, shapes) for o in outs]
    for i, n in enumerate(SMALL):
        upd[n] = [u[i].reshape(given[n][0].shape) for u in unpacked]
    loss = unpacked[0][-1].reshape(())
    cw = conv_w.shape[-1]
    g_cw_mine = lax.dynamic_slice_in_dim(unpacked[0][-2], me * cw, cw, axis=1)
    upd['conv_w'] = adamw(g_cw_mine.reshape(-1, LANES), conv_w.reshape(-1, LANES), m_conv_w.reshape(-1, LANES),
                          v_conv_w.reshape(-1, LANES), False, "adamw_conv_w")

    order = ['norm_w', 'w_in', 'conv_w', 'conv_b', 'dt_bias', 'a_log', 'd_skip', 'ssm_norm_w', 'w_attn_branch',
             'w_ssm_branch', 'w_out', 'final_norm_w']
    like = dict(norm_w=norm_w, w_in=w_in, conv_w=conv_w, conv_b=conv_b, dt_bias=dt_bias, a_log=a_log, d_skip=d_skip,
                ssm_norm_w=ssm_norm_w, w_attn_branch=w_attn_branch, w_ssm_branch=w_ssm_branch, w_out=w_out,
                final_norm_w=final_norm_w)
    result = [loss, grad_x[None]]
    for t in range(4):
        result += [upd[n][t].reshape(like[n].shape) for n in order]
    return tuple(result)
```

```python
import functools
import math
from typing import NamedTuple

import jax
import jax.numpy as jnp
from jax import lax
from jax.experimental import pallas as pl
from jax.experimental.pallas import tpu as pltpu

F32 = jnp.float32
BF16 = jnp.bfloat16
RMS_EPS = 1e-6
NEG = -1e30
N_DEV = 8
CPS = 8
LANES = 128
ATTN_BLOCK = 128
ADAM_LR, ADAM_B1, ADAM_B2, ADAM_EPS, ADAM_WD, ADAM_STEP = 0.001, 0.9, 0.999, 1e-08, 0.01, 10
VMEM_LIMIT = 56 * 1024 * 1024


class Cfg(NamedTuple):
    D: int = 2048
    S: int = 8192
    AH: int = 16
    E: int = 128
    patterns: tuple = ((128, 1), (512, 4), (2048, 16))
    SI: int = 4096
    P: int = 64
    SG: int = 8
    SN: int = 128
    KC: int = 4
    L: int = 128

    @property
    def AW(self): return self.AH * self.E
    @property
    def SH(self): return self.SI // self.P
    @property
    def HPG(self): return self.SH // self.SG
    @property
    def GN(self): return self.SG * self.SN
    @property
    def CD(self): return self.SI + 2 * self.GN
    @property
    def k0(self): return self.AW
    @property
    def v0(self): return 2 * self.AW
    @property
    def za0(self): return 3 * self.AW
    @property
    def zs0(self): return 4 * self.AW
    @property
    def xbc0(self): return 4 * self.AW + self.SI
    @property
    def ga0(self): return self.xbc0 + self.CD
    @property
    def gs0(self): return self.ga0 + self.D
    @property
    def NP(self): return self.gs0 + self.D
    @property
    def N_IN(self): return self.NP + self.SH


CFG = Cfg()


def _cp(sem=None, vmem=VMEM_LIMIT):
    return pltpu.CompilerParams(dimension_semantics=sem, vmem_limit_bytes=vmem)


def _sigmoid(z):
    return 1.0 / (1.0 + jnp.exp(-z))


def _dot(a, b, dims):
    return lax.dot_general(a, b, (dims, ((), ())), preferred_element_type=F32)


NN = ((1,), (0,))
NT = ((1,), (1,))
TN = ((0,), (0,))


def _blk(off, width):
    assert off % width == 0, (off, width)
    return off // width


def matmul(a, b, mode, tm, tn, tk, out_dtype, name, side=None, a_cols=None):
    m0 = 0
    if mode == 'nn':
        (M, K), (_, N) = a.shape, b.shape
    elif mode == 'nt':
        (M, K), (N, _) = a.shape, b.shape
    else:
        (K, M), (_, N) = a.shape, b.shape
        if a_cols is not None:
            m0, M = a_cols
    tm, tn, tk = math.gcd(min(tm, M), m0), min(tn, N), min(tk, K)
    assert M % tm == 0 and N % tn == 0 and K % tk == 0, (M, N, K, tm, tn, tk)
    nk = K // tk
    dims = {'nn': NN, 'nt': NT, 'tn': TN}[mode]

    def body(a_ref, b_ref, o_ref, *acc):
        part = _dot(a_ref[...].astype(BF16), b_ref[...].astype(BF16), dims)
        if nk == 1:
            o_ref[...] = part.astype(out_dtype)
        else:
            acc_ref, = acc
            k = pl.program_id(2)

            @pl.when(k == 0)
            def _():
                acc_ref[...] = part

            @pl.when(k > 0)
            def _():
                acc_ref[...] += part

            @pl.when(k == nk - 1)
            def _():
                o_ref[...] = acc_ref[...].astype(out_dtype)

    if mode == 'tn':
        a_spec = pl.BlockSpec((tk, tm), lambda n, m, k: (k, _blk(m0, tm) + m))
    else:
        a_spec = pl.BlockSpec((tm, tk), lambda n, m, k: (m, k))
    if mode == 'nt':
        b_spec = pl.BlockSpec((tn, tk), lambda n, m, k: (n, k))
    else:
        b_spec = pl.BlockSpec((tk, tn), lambda n, m, k: (k, n))
    grid = (N // tn, M // tm, nk)
    o_spec = pl.BlockSpec((tm, tn), lambda n, m, k: (m, n))
    o_shape = jax.ShapeDtypeStruct((M, N), out_dtype)
    acc = [] if nk == 1 else [pltpu.VMEM((tm, tn), F32)]
    if side is None:
        return pl.pallas_call(
            body, name=name, grid=grid, in_specs=[a_spec, b_spec], out_specs=o_spec, out_shape=o_shape,
            scratch_shapes=acc, compiler_params=_cp(("parallel", "parallel", "arbitrary")),
        )(a, b)
    arrs, gathers = side
    whole = pl.BlockSpec(memory_space=pl.ANY)
    res = pl.pallas_call(
        with_exchange(body, 2, 1, gathers, grid), name=name, grid=grid,
        in_specs=[a_spec, b_spec] + [whole] * len(arrs), out_specs=[o_spec] + [whole] * len(arrs),
        out_shape=[o_shape] + _exchange_shapes(arrs, gathers),
        scratch_shapes=acc + _exchange_sems(len(arrs)),
        compiler_params=_cp(("arbitrary", "arbitrary", "arbitrary")),
    )(a, b, *arrs)
    return res[0], res[1:]


def rmsnorm_fwd(x, w, name, tm=256):
    S, D = x.shape

    def body(x_ref, w_ref, o_ref):
        xv = x_ref[...]
        r = lax.rsqrt(jnp.mean(xv * xv, axis=-1, keepdims=True) + RMS_EPS)
        o_ref[...] = ((xv * r) * w_ref[...]).astype(BF16)

    return pl.pallas_call(
        body, name=name, grid=(S // tm,),
        in_specs=[pl.BlockSpec((tm, D), lambda i: (i, 0)), pl.BlockSpec((1, D), lambda i: (0, 0))],
        out_specs=pl.BlockSpec((tm, D), lambda i: (i, 0)),
        out_shape=jax.ShapeDtypeStruct((S, D), BF16),
        compiler_params=_cp(("parallel",)),
    )(x, w)


def rmsnorm_bwd(dh_a, ddt, w_dt, x, w, dout, name, tm=128):
    S, D = x.shape

    def body(da_ref, dd_ref, wd_ref, x_ref, w_ref, do_ref, gx_ref, gw_ref):
        xv = x_ref[...]
        dh = da_ref[...] + _dot(dd_ref[...], wd_ref[...], NT)
        r = lax.rsqrt(jnp.mean(xv * xv, axis=-1, keepdims=True) + RMS_EPS)
        g = dh * w_ref[...]
        dx = r * g - xv * (r * r * r) * jnp.mean(g * xv, axis=-1, keepdims=True)
        gx_ref[...] = do_ref[...] + dx
        gw = jnp.sum(dh * (xv * r), axis=0, keepdims=True)

        @pl.when(pl.program_id(0) == 0)
        def _():
            gw_ref[...] = gw

        @pl.when(pl.program_id(0) > 0)
        def _():
            gw_ref[...] += gw

    row = pl.BlockSpec((tm, D), lambda i: (i, 0))
    vec = pl.BlockSpec((1, D), lambda i: (0, 0))
    return pl.pallas_call(
        body, name=name, grid=(S // tm,),
        in_specs=[row, pl.BlockSpec((tm, LANES), lambda i: (i, 0)), pl.BlockSpec((D, LANES), lambda i: (0, 0)), row,
                  vec, row],
        out_specs=[row, vec],
        out_shape=[jax.ShapeDtypeStruct((S, D), F32), jax.ShapeDtypeStruct((1, D), F32)],
        compiler_params=_cp(("arbitrary",)),
    )(dh_a, ddt, w_dt, x, w, dout)


def out_proj_final(merged, w_out, x, fw, tgt, name, tm=256):
    S, D = x.shape

    def body(m_ref, wo_ref, x_ref, w_ref, t_ref, do_ref, loss_ref, gw_ref):
        out = x_ref[...] + _dot(m_ref[...], wo_ref[...], NN)
        w = w_ref[...]
        r = lax.rsqrt(jnp.mean(out * out, axis=-1, keepdims=True) + RMS_EPS)
        yn = out * r
        err = yn * w - t_ref[...]
        lrow = 0.5 * jnp.mean(err * err, axis=-1, keepdims=True)
        lsum = jnp.zeros((1, LANES), F32) + jnp.sum(lrow, axis=0, keepdims=True)
        dfin = err * (1.0 / D)
        g = dfin * w
        do_ref[...] = r * g - out * (r * r * r) * jnp.mean(g * out, axis=-1, keepdims=True)
        gw = jnp.sum(dfin * yn, axis=0, keepdims=True)

        @pl.when(pl.program_id(0) == 0)
        def _():
            gw_ref[...] = gw
            loss_ref[...] = lsum

        @pl.when(pl.program_id(0) > 0)
        def _():
            gw_ref[...] += gw
            loss_ref[...] += lsum

    row = pl.BlockSpec((tm, D), lambda i: (i, 0))
    vec = pl.BlockSpec((1, D), lambda i: (0, 0))
    return pl.pallas_call(
        body, name=name, grid=(S // tm,),
        in_specs=[row, pl.BlockSpec((D, D), lambda i: (0, 0)), row, vec, row],
        out_specs=[row, pl.BlockSpec((1, LANES), lambda i: (0, 0)), vec],
        out_shape=[jax.ShapeDtypeStruct((S, D), F32), jax.ShapeDtypeStruct((1, LANES), F32),
                   jax.ShapeDtypeStruct((1, D), F32)],
        compiler_params=_cp(("arbitrary",)),
    )(merged, w_out, x, fw, tgt)


def branch_merge(o_a, w_attn, y_n, w_ssm, proj, cfg, name, tm=512, tn=512):
    S, D = cfg.S, cfg.D
    tm, tn = min(tm, S), min(tn, D)

    def body(oa_ref, wa_ref, yn_ref, ws_ref, ga_ref, gs_ref, a_ref, s_ref, m_ref):
        a = _dot(oa_ref[...], wa_ref[...], NN)
        sv = _dot(yn_ref[...], ws_ref[...], NN)
        a_ref[...] = a.astype(BF16)
        s_ref[...] = sv.astype(BF16)
        m_ref[...] = (_sigmoid(ga_ref[...]) * a + _sigmoid(gs_ref[...]) * sv).astype(BF16)

    tile = pl.BlockSpec((tm, tn), lambda n, m: (m, n))
    return pl.pallas_call(
        body, name=name, grid=(D // tn, S // tm),
        in_specs=[pl.BlockSpec((tm, cfg.AW), lambda n, m: (m, 0)), pl.BlockSpec((cfg.AW, tn), lambda n, m: (0, n)),
                  pl.BlockSpec((tm, cfg.SI), lambda n, m: (m, 0)), pl.BlockSpec((cfg.SI, tn), lambda n, m: (0, n)),
                  pl.BlockSpec((tm, tn), lambda n, m: (m, _blk(cfg.ga0, tn) + n)),
                  pl.BlockSpec((tm, tn), lambda n, m: (m, _blk(cfg.gs0, tn) + n))],
        out_specs=[tile, tile, tile],
        out_shape=[jax.ShapeDtypeStruct((S, D), BF16)] * 3,
        compiler_params=_cp(("parallel", "parallel")),
    )(o_a, w_attn, y_n, w_ssm, proj, proj)


def merge_bwd(dout, w_out, a_out, s_out, proj, cfg, name, tm=512, tn=1024):
    S, D = cfg.S, cfg.D
    tm, tn = min(tm, S), min(tn, D)

    def body(do_ref, wo_ref, a_ref, s_ref, ga_ref, gs_ref, da_ref, ds_ref, dga_ref, dgs_ref):
        dmv = _dot(do_ref[...].astype(BF16), wo_ref[...], NT)
        sa = _sigmoid(ga_ref[...])
        ss = _sigmoid(gs_ref[...])
        da_ref[...] = (dmv * sa).astype(BF16)
        ds_ref[...] = (dmv * ss).astype(BF16)
        dga_ref[...] = (dmv * a_ref[...] * (sa * (1.0 - sa))).astype(BF16)
        dgs_ref[...] = (dmv * s_ref[...] * (ss * (1.0 - ss))).astype(BF16)

    tile = pl.BlockSpec((tm, tn), lambda n, m: (m, n))
    sh = jax.ShapeDtypeStruct((S, D), BF16)
    return pl.pallas_call(
        body, name=name, grid=(D // tn, S // tm),
        in_specs=[pl.BlockSpec((tm, D), lambda n, m: (m, 0)), pl.BlockSpec((tn, D), lambda n, m: (n, 0)), tile, tile,
                  pl.BlockSpec((tm, tn), lambda n, m: (m, _blk(cfg.ga0, tn) + n)),
                  pl.BlockSpec((tm, tn), lambda n, m: (m, _blk(cfg.gs0, tn) + n))],
        out_specs=[tile] * 4, out_shape=[sh] * 4,
        compiler_params=_cp(("parallel", "parallel")),
    )(dout, w_out, a_out, s_out, proj, proj)


def _attn_rows(base, d):
    return pl.ds(base, ATTN_BLOCK) if d == 1 else pl.ds(base, ATTN_BLOCK, stride=d)


def _attn_units(cfg):
    dmax = max(d for _, d in cfg.patterns)
    units = []
    for p, (window, d) in enumerate(cfg.patterns):
        assert window // d == ATTN_BLOCK and dmax % d == 0
        nsub = dmax // d
        for b in range(nsub):
            for r in range(d):
                base = b * ATTN_BLOCK * d + r
                if b > 0:
                    units.append((p, d, base, (b - 1) * ATTN_BLOCK * d + r, False))
                else:
                    units.append((p, d, base, (nsub - 1) * ATTN_BLOCK * d + r, True))
    return units, ATTN_BLOCK * dmax


def _set_bias_tiles(bias_s, slope, cfg):
    qi = lax.broadcasted_iota(jnp.int32, (ATTN_BLOCK, ATTN_BLOCK), 0)
    ki = lax.broadcasted_iota(jnp.int32, (ATTN_BLOCK, ATTN_BLOCK), 1)
    for p, (_, d) in enumerate(cfg.patterns):
        bias_s[2 * p] = jnp.where(ki >= qi, (-slope) * ((ATTN_BLOCK + qi - ki) * d).astype(F32), NEG)
        bias_s[2 * p + 1] = jnp.where(ki <= qi, (-slope) * ((qi - ki) * d).astype(F32), NEG)


def _unit_scores(q, kcat, bias_s, p, prev_ok, scale):
    s = _dot(q, kcat, NT) * scale + jnp.concatenate([bias_s[2 * p], bias_s[2 * p + 1]], axis=1)
    if prev_ok is not None:
        cur_half = lax.broadcasted_iota(jnp.int32, s.shape, 1) >= ATTN_BLOCK
        s = jnp.where(jnp.logical_or(cur_half, prev_ok), s, NEG)
    return s


def _slope_table(cfg):
    slopes = jnp.asarray([2.0 ** (-8.0 * (h + 1) / cfg.AH) for h in range(cfg.AH)], F32)
    return jnp.broadcast_to(slopes.reshape(cfg.AH, 1, 1), (cfg.AH, 8, LANES))


def attn_fused_fwd(proj, slopes, cfg, name):
    S, E, AH = cfg.S, cfg.E, cfg.AH
    units, SB = _attn_units(cfg)
    assert S % SB == 0
    npat = len(cfg.patterns)
    scale = E ** -0.5

    def spec(off, prev):
        c0 = _blk(off, E)
        if prev:
            return pl.BlockSpec((SB, E), lambda h, i: (jnp.maximum(i - 1, 0), c0 + h))
        return pl.BlockSpec((SB, E), lambda h, i: (i, c0 + h))

    def body(q_ref, kp_ref, kc_ref, vp_ref, vc_ref, z_ref, sl_ref, oa_ref, om_ref, lt_ref, *scr):
        o_s, l_s, bias_s = scr[:npat], scr[npat:2 * npat], scr[2 * npat]
        i = pl.program_id(1)

        @pl.when(i == 0)
        def _():
            _set_bias_tiles(bias_s, sl_ref[0, 0:1, :], cfg)

        for p, d, base, pbase, from_prev in units:
            rows, prows = _attn_rows(base, d), _attn_rows(pbase, d)
            q = q_ref[rows, :].astype(BF16)
            kp = (kp_ref if from_prev else kc_ref)[prows, :].astype(BF16)
            vp = (vp_ref if from_prev else vc_ref)[prows, :].astype(BF16)
            kcat = jnp.concatenate([kp, kc_ref[rows, :].astype(BF16)], axis=0)
            vcat = jnp.concatenate([vp, vc_ref[rows, :].astype(BF16)], axis=0)
            s = _unit_scores(q, kcat, bias_s, p, (i > 0) if from_prev else None, scale)
            m = jnp.max(s, axis=1, keepdims=True)
            pr = jnp.exp(s - m)
            l = jnp.sum(pr, axis=1, keepdims=True)
            o_s[p][rows, :] = _dot(pr.astype(BF16), vcat, NN) * (1.0 / l)
            l_s[p][rows, :] = m + jnp.log(l)
        ls = [l_s[p][...] for p in range(npat)]
        m = functools.reduce(jnp.maximum, ls)
        lt = m + jnp.log(sum(jnp.exp(l_ - m) for l_ in ls))
        lt_ref[...] = lt
        mix = sum(jnp.exp(ls[p] - lt) * o_s[p][...] for p in range(npat))
        om_ref[...] = mix
        z = z_ref[...]
        oa_ref[...] = (mix * (z * _sigmoid(z))).astype(BF16)

    out = pl.BlockSpec((SB, E), lambda h, i: (i, h))
    return pl.pallas_call(
        body, name=name, grid=(AH, S // SB),
        in_specs=[spec(0, False), spec(cfg.k0, True), spec(cfg.k0, False), spec(cfg.v0, True), spec(cfg.v0, False),
                  spec(cfg.za0, False), pl.BlockSpec((1, 8, LANES), lambda h, i: (h, 0, 0))],
        out_specs=[out, out, pl.BlockSpec((SB, 1), lambda h, i: (h * (S // SB) + i, 0))],
        out_shape=[jax.ShapeDtypeStruct((S, cfg.AW), BF16), jax.ShapeDtypeStruct((S, cfg.AW), F32),
                   jax.ShapeDtypeStruct((AH * S, 1), F32)],
        scratch_shapes=[pltpu.VMEM((SB, E), F32)] * npat + [pltpu.VMEM((SB, 1), F32)] * npat +
        [pltpu.VMEM((2 * npat, ATTN_BLOCK, ATTN_BLOCK), F32)],
        compiler_params=_cp(("parallel", "arbitrary")),
    )(proj, proj, proj, proj, proj, proj, slopes)


def attn_fused_bwd(proj, do_a, o_mix, ltot, slopes, cfg, name):
    S, E, AH = cfg.S, cfg.E, cfg.AH
    units, SB = _attn_units(cfg)
    nsb = S // SB
    last = nsb - 1
    scale = E ** -0.5

    def spec(off, prev):
        c0 = _blk(off, E)
        if prev:
            return pl.BlockSpec((SB, E), lambda h, i: (jnp.maximum(i - 1, 0), c0 + h))
        return pl.BlockSpec((SB, E), lambda h, i: (jnp.minimum(i, last), c0 + h))

    cur = pl.BlockSpec((SB, E), lambda h, i: (jnp.minimum(i, last), h))
    prev = pl.BlockSpec((SB, E), lambda h, i: (jnp.maximum(i - 1, 0), h))

    def body(q_ref, kp_ref, kc_ref, vp_ref, vc_ref, z_ref, doa_ref, om_ref, lt_ref, sl_ref,
             dq_ref, dk_ref, dv_ref, dz_ref, dmix_s, dl_s, dq_s, dkp_s, dvp_s, dkc_s, dvc_s, bias_s):
        i = pl.program_id(1)

        @pl.when(i == 0)
        def _():
            dkc_s[...] = jnp.zeros_like(dkc_s)
            dvc_s[...] = jnp.zeros_like(dvc_s)
            _set_bias_tiles(bias_s, sl_ref[0, 0:1, :], cfg)

        @pl.when(i < nsb)
        def _():
            z = z_ref[...]
            s = _sigmoid(z)
            doa = doa_ref[...].astype(F32)
            om = om_ref[...]
            dmix = doa * (z * s)
            dmix_s[...] = dmix
            dz_ref[...] = (doa * om * (s * (1.0 + z * (1.0 - s)))).astype(BF16)
            dl_s[...] = jnp.sum(dmix * om, axis=1, keepdims=True)
            dkp_s[...] = dkc_s[...]
            dvp_s[...] = dvc_s[...]
            dkc_s[...] = jnp.zeros_like(dkc_s)
            dvc_s[...] = jnp.zeros_like(dvc_s)
            dq_s[...] = jnp.zeros_like(dq_s)
            for p, d, base, pbase, from_prev in units:
                rows, prows = _attn_rows(base, d), _attn_rows(pbase, d)
                q = q_ref[rows, :].astype(BF16)
                kc = kc_ref[rows, :].astype(BF16)
                kp = (kp_ref if from_prev else kc_ref)[prows, :].astype(BF16)
                vp = (vp_ref if from_prev else vc_ref)[prows, :].astype(BF16)
                do = dmix_s[rows, :].astype(BF16)
                lt = lt_ref[rows, :]
                dlt = dl_s[rows, :]
                kcat = jnp.concatenate([kp, kc], axis=0)
                vcat = jnp.concatenate([vp, vc_ref[rows, :].astype(BF16)], axis=0)
                pr = jnp.exp(_unit_scores(q, kcat, bias_s, p, (i > 0) if from_prev else None, scale) - lt)
                ds = (pr * (_dot(do, vcat, NT) - dlt) * scale).astype(BF16)
                dq_s[rows, :] += _dot(ds, kcat, NN)
                dkcat = _dot(ds, q, TN)
                dvcat = _dot(pr.astype(BF16), do, TN)
                dk_t, dv_t = (dkp_s, dvp_s) if from_prev else (dkc_s, dvc_s)
                dk_t[prows, :] += dkcat[:ATTN_BLOCK, :]
                dv_t[prows, :] += dvcat[:ATTN_BLOCK, :]
                dkc_s[rows, :] += dkcat[ATTN_BLOCK:, :]
                dvc_s[rows, :] += dvcat[ATTN_BLOCK:, :]
            dq_ref[...] = dq_s[...].astype(BF16)
            dk_ref[...] = dkp_s[...].astype(BF16)
            dv_ref[...] = dvp_s[...].astype(BF16)

        @pl.when(i == nsb)
        def _():
            dk_ref[...] = dkc_s[...].astype(BF16)
            dv_ref[...] = dvc_s[...].astype(BF16)

    sh = jax.ShapeDtypeStruct((S, cfg.AW), BF16)
    acc = pltpu.VMEM((SB, E), F32)
    return pl.pallas_call(
        body, name=name, grid=(AH, nsb + 1),
        in_specs=[spec(0, False), spec(cfg.k0, True), spec(cfg.k0, False), spec(cfg.v0, True), spec(cfg.v0, False),
                  spec(cfg.za0, False), cur, cur,
                  pl.BlockSpec((SB, 1), lambda h, i: (h * nsb + jnp.minimum(i, last), 0)),
                  pl.BlockSpec((1, 8, LANES), lambda h, i: (h, 0, 0))],
        out_specs=[cur, prev, prev, cur], out_shape=[sh] * 4,
        scratch_shapes=[acc, pltpu.VMEM((SB, 1), F32), acc, acc, acc, acc, acc,
                        pltpu.VMEM((2 * len(cfg.patterns), ATTN_BLOCK, ATTN_BLOCK), F32)],
        compiler_params=_cp(("parallel", "arbitrary")),
    )(proj, proj, proj, proj, proj, proj, do_a, o_mix, ltot, slopes)


HALO = 8


def _conv_taps(x_ref, h_ref, kc):
    x = x_ref[...]
    full = jnp.concatenate([jnp.where(pl.program_id(1) == 0, 0.0, h_ref[...]), x], axis=0)
    return [pltpu.roll(full, s, axis=0)[HALO:, :] for s in range(kc - 1, 0, -1)] + [x]


def _conv_pre(taps, w_ref, b_ref):
    pre = b_ref[...] + w_ref[0:1, :] * taps[0]
    for k in range(1, len(taps)):
        pre = pre + w_ref[k:k + 1, :] * taps[k]
    return pre


def conv_fwd(proj, w, b, cfg, name, tm=1024, tc=512):
    S, CD, KC = cfg.S, cfg.CD, cfg.KC
    tc = min(tc, CD)
    c0 = _blk(cfg.xbc0, tc)
    hb = tm // HALO

    def body(x_ref, h_ref, w_ref, b_ref, o_ref):
        pre = _conv_pre(_conv_taps(x_ref, h_ref, KC), w_ref, b_ref)
        o_ref[...] = pre * _sigmoid(pre)

    return pl.pallas_call(
        body, name=name, grid=(CD // tc, S // tm),
        in_specs=[pl.BlockSpec((tm, tc), lambda c, i: (i, c0 + c)),
                  pl.BlockSpec((HALO, tc), lambda c, i: (jnp.maximum(i * hb - 1, 0), c0 + c)),
                  pl.BlockSpec((KC, tc), lambda c, i: (0, c)),
                  pl.BlockSpec((1, tc), lambda c, i: (0, c))],
        out_specs=pl.BlockSpec((tm, tc), lambda c, i: (i, c)),
        out_shape=jax.ShapeDtypeStruct((S, CD), F32),
        compiler_params=_cp(("parallel", "arbitrary")),
    )(proj, proj, w, b)


def conv_bwd(proj, dxc, w, b, cfg, name, c_off, tm=1024, tc=512):
    S, KC = cfg.S, cfg.KC
    CD = dxc.shape[1]
    tc = min(tc, CD)
    c0 = _blk(cfg.xbc0 + c_off, tc)
    w0 = _blk(c_off, tc)
    hb = tm // HALO
    nrb = S // tm
    last_h = S // HALO - 1

    def body(x_ref, hp_ref, hn_ref, d_ref, dn_ref, w_ref, b_ref, o_ref, gw_ref, gb_ref):
        i = pl.program_id(1)
        x = x_ref[...]
        full = jnp.concatenate([jnp.where(i == 0, 0.0, hp_ref[...]), x, hn_ref[...]], axis=0)
        rows = tm + HALO
        taps = [pltpu.roll(full, s_, axis=0)[HALO:, :] for s_ in range(KC - 1, 0, -1)] + [full[HALO:, :]]
        pre = _conv_pre(taps, w_ref, b_ref)
        sg = _sigmoid(pre)
        d_ext = jnp.concatenate([d_ref[...], jnp.where(i == nrb - 1, 0.0, dn_ref[...])], axis=0)
        dpre = d_ext * (sg * (1.0 + pre * (1.0 - sg)))
        own = dpre[:tm, :]
        acc = w_ref[KC - 1:KC, :] * own
        for j in range(1, KC):
            acc = acc + w_ref[KC - 1 - j:KC - j, :] * pltpu.roll(dpre, rows - j, axis=0)[:tm, :]
        o_ref[...] = acc.astype(BF16)
        gb = jnp.sum(own, axis=0, keepdims=True)
        gws = [jnp.sum(own * taps[k][:tm, :], axis=0, keepdims=True) for k in range(KC)]
        gw = jnp.concatenate(gws + [jnp.zeros((8 - KC, tc), F32)], axis=0)

        @pl.when(i == 0)
        def _():
            gw_ref[...] = gw
            gb_ref[...] = gb

        @pl.when(i > 0)
        def _():
            gw_ref[...] += gw
            gb_ref[...] += gb

    return pl.pallas_call(
        body, name=name, grid=(CD // tc, nrb),
        in_specs=[pl.BlockSpec((tm, tc), lambda c, i: (i, c0 + c)),
                  pl.BlockSpec((HALO, tc), lambda c, i: (jnp.maximum(i * hb - 1, 0), c0 + c)),
                  pl.BlockSpec((HALO, tc), lambda c, i: (jnp.minimum((i + 1) * hb, last_h), c0 + c)),
                  pl.BlockSpec((tm, tc), lambda c, i: (i, c)),
                  pl.BlockSpec((HALO, tc), lambda c, i: (jnp.minimum((i + 1) * hb, last_h), c)),
                  pl.BlockSpec((KC, tc), lambda c, i: (0, w0 + c)),
                  pl.BlockSpec((1, tc), lambda c, i: (0, w0 + c))],
        out_specs=[pl.BlockSpec((tm, tc), lambda c, i: (i, c)),
                   pl.BlockSpec((8, tc), lambda c, i: (0, c)),
                   pl.BlockSpec((1, tc), lambda c, i: (0, c))],
        out_shape=[jax.ShapeDtypeStruct((S, CD), BF16), jax.ShapeDtypeStruct((8, CD), F32),
                   jax.ShapeDtypeStruct((1, CD), F32)],
        compiler_params=_cp(("parallel", "arbitrary")),
    )(proj, proj, proj, dxc, dxc, w, b)


def _pad_lanes(v, width=LANES):
    return jnp.pad(v, ((0, 0), (0, width - v.shape[1])))


def ssd_prep(hn, w_dt, dt_bias, a_log, cfg, name):
    S, L, D = cfg.S, cfg.L, cfg.D
    R = CPS * L

    def body(h_ref, wd_ref, b_ref, al_ref, raw_ref, dt_ref, ac_ref):
        raw = _dot(h_ref[...], wd_ref[...], NN)
        raw_ref[...] = raw
        x = raw + b_ref[...]
        dt = jnp.maximum(x, 0.0) + jnp.log(1.0 + jnp.exp(-jnp.abs(x)))
        da = dt * (-jnp.exp(al_ref[...]))
        li = lax.broadcasted_iota(jnp.int32, (L, L), 0)
        si = lax.broadcasted_iota(jnp.int32, (L, L), 1)
        tri = jnp.where(li >= si, 1.0, 0.0).astype(F32)
        dt_ref[...] = dt
        for ci in range(CPS):
            rows = slice(ci * L, (ci + 1) * L)
            ac_ref[rows, :] = lax.dot_general(tri, da[rows, :], ((NN), ((), ())), precision=lax.Precision.HIGHEST,
                                              preferred_element_type=F32)

    row = pl.BlockSpec((R, LANES), lambda i: (i, 0))
    vec = pl.BlockSpec((1, LANES), lambda i: (0, 0))
    sh = jax.ShapeDtypeStruct((S, LANES), F32)
    return pl.pallas_call(
        body, name=name, grid=(S // R,),
        in_specs=[pl.BlockSpec((R, D), lambda i: (i, 0)), pl.BlockSpec((D, LANES), lambda i: (0, 0)), vec, vec],
        out_specs=[row, row, row], out_shape=[sh, sh, sh],
        compiler_params=_cp(("parallel",)),
    )(hn, w_dt, dt_bias, a_log)


def _spread(v, n):
    return jnp.broadcast_to(v[:, :, None], v.shape + (n,)).reshape(v.shape[0], v.shape[1] * n)


def _head_selectors(cfg):
    def sel(width):
        head = jnp.arange(LANES)[None, :, None]
        slot = jnp.arange(cfg.SG)[:, None, None] * cfg.HPG + (jnp.arange(cfg.HPG * width) // width)[None, None, :]
        return (head == slot).astype(BF16)
    return sel(cfg.P), sel(LANES)


def _spread_heads(v, sel):
    n = v.shape[0]
    hi = v.astype(BF16)
    r1 = v - hi.astype(F32)
    mid = r1.astype(BF16)
    lo = (r1 - mid.astype(F32)).astype(BF16)
    out = _dot(jnp.concatenate([hi, mid, lo], axis=0), sel, NN)
    return out[:n] + out[n:2 * n] + out[2 * n:]


def _pair_lanes(wide, hpg, p):
    low = lax.broadcasted_iota(jnp.int32, (wide.shape[0], LANES), 1) < p
    return jnp.concatenate([jnp.where(low, wide[:, 2 * jp * LANES:(2 * jp + 1) * LANES],
                                      wide[:, (2 * jp + 1) * LANES:(2 * jp + 2) * LANES])
                            for jp in range(hpg // 2)], axis=1)


def _pair_select(halves, p):
    low = lax.broadcasted_iota(jnp.int32, halves[0].shape, 1) < p
    return jnp.where(low, halves[0], halves[1])


def _head_rows(row, hpg, p):
    return jnp.concatenate([jnp.broadcast_to(row[:, j * LANES:(j + 1) * LANES], (p, LANES)) for j in range(hpg)],
                           axis=0)


def _segment_sums(t, sel):
    r = t.shape[0]
    hi = t.astype(BF16)
    out = _dot(jnp.concatenate([hi, (t - hi.astype(F32)).astype(BF16)], axis=0), sel, NT)
    return out[:r] + out[r:]


def ssd_scan_fwd(xc, dt, acum, act, sel_p, sel_l, cfg, name):
    S, L, P, SN, HPG, SG, SI = cfg.S, cfg.L, cfg.P, cfg.SN, cfg.HPG, cfg.SG, cfg.SI
    nc = S // L
    GW = HPG * P
    bcol, ccol = _blk(SI, SN), _blk(SI + cfg.GN, SN)
    assert nc % CPS == 0

    def body(xs_ref, b_ref, c_ref, dtn_ref, acn_ref, at_ref, sp_ref, sl_ref, y_ref, st_ref, st):
        @pl.when(pl.program_id(1) == 0)
        def _():
            st[...] = jnp.zeros_like(st)

        causal = lax.broadcasted_iota(jnp.int32, (L, L), 0) >= lax.broadcasted_iota(jnp.int32, (L, L), 1)
        for ci in range(CPS):
            rows = slice(ci * L, (ci + 1) * L)
            acn = acn_ref[rows, :]
            dts = _spread_heads(dtn_ref[rows, :], sp_ref[0])
            acs = _spread_heads(acn, sl_ref[0])
            a_p = _pair_lanes(acs, HPG, P)
            s0 = st[...]
            st_ref[ci] = s0.reshape(HPG, P, SN)
            B = b_ref[rows, :].astype(BF16)
            C = c_ref[rows, :].astype(BF16)
            G = _dot(C, B, NT)
            xdt = xs_ref[rows, :] * dts
            xdtb = xdt.astype(BF16)
            ws = jnp.exp(a_p[L - 1:L, :] - a_p)
            yo = jnp.exp(a_p) * _dot(C, s0.astype(BF16), NT)
            yd = []
            for jp in range(HPG // 2):
                x_pair = xdtb[:, jp * LANES:(jp + 1) * LANES]
                halves = []
                for j in (2 * jp, 2 * jp + 1):
                    dm = jnp.where(causal, jnp.exp(acs[:, j * LANES:(j + 1) * LANES] - at_ref[j:j + 1, rows]), 0.0)
                    halves.append(_dot((G * dm).astype(BF16), x_pair, NN))
                yd.append(_pair_select(halves, P))
            y_ref[rows, :] = jnp.concatenate(yd, axis=1) + yo
            st[...] = _head_rows(jnp.exp(acs[L - 1:L, :]), HPG, P) * s0 + _dot((xdt * ws).astype(BF16), B, TN)

    R = CPS * L
    y, states = pl.pallas_call(
        body, name=name, grid=(SG, nc // CPS),
        in_specs=[pl.BlockSpec((R, GW), lambda g, c: (c, g)),
                  pl.BlockSpec((R, SN), lambda g, c: (c, bcol + g)),
                  pl.BlockSpec((R, SN), lambda g, c: (c, ccol + g)),
                  pl.BlockSpec((R, LANES), lambda g, c: (c, 0)),
                  pl.BlockSpec((R, LANES), lambda g, c: (c, 0)),
                  pl.BlockSpec((HPG, R), lambda g, c: (g, c)),
                  pl.BlockSpec((1, LANES, GW), lambda g, c: (g, 0, 0)),
                  pl.BlockSpec((1, LANES, HPG * LANES), lambda g, c: (g, 0, 0))],
        out_specs=[pl.BlockSpec((R, GW), lambda g, c: (c, g)),
                   pl.BlockSpec((CPS, HPG, P, SN), lambda g, c: (c, g, 0, 0))],
        out_shape=[jax.ShapeDtypeStruct((S, SI), F32), jax.ShapeDtypeStruct((nc, cfg.SH, P, SN), F32)],
        scratch_shapes=[pltpu.VMEM((GW, SN), F32)],
        compiler_params=_cp(("parallel", "arbitrary")),
    )(xc, xc, xc, dt, acum, act, sel_p, sel_l)
    return y, states


def ssd_scan_bwd(xc, dt, acum, act, sel_p, sel_l, states, y, dy, dvec, cfg, name, side):
    S, L, P, SN, HPG, SG, SI = cfg.S, cfg.L, cfg.P, cfg.SN, cfg.HPG, cfg.SG, cfg.SI
    nc = S // L
    GW = HPG * P
    bcol, ccol = _blk(SI, SN), _blk(SI + cfg.GN, SN)

    def rc(c):
        return nc // CPS - 1 - c

    def body(xs_ref, b_ref, c_ref, dtn_ref, acn_ref, at_ref, sp_ref, sl_ref, st_ref, y_ref, dy_ref, dk_ref,
             dxs_ref, db_ref, dc_ref, dac_ref, dxsum_ref, dst):
        @pl.when(pl.program_id(1) == 0)
        def _():
            dst[...] = jnp.zeros_like(dst)

        sel = sp_ref[0]
        causal = lax.broadcasted_iota(jnp.int32, (L, L), 0) >= lax.broadcasted_iota(jnp.int32, (L, L), 1)
        low = lax.broadcasted_iota(jnp.int32, (L, LANES), 1) < P
        is_last = lax.broadcasted_iota(jnp.int32, (L, LANES), 0) == L - 1
        ones = jnp.ones((16, SN), BF16)
        for ci in reversed(range(CPS)):
            rows = slice(ci * L, (ci + 1) * L)
            acn = acn_ref[rows, :]
            dts = _spread_heads(dtn_ref[rows, :], sel)
            acs = _spread_heads(acn, sl_ref[0])
            a_p = _pair_lanes(acs, HPG, P)
            B = b_ref[rows, :].astype(BF16)
            C = c_ref[rows, :].astype(BF16)
            G = _dot(C, B, NT)
            xs = xs_ref[rows, :]
            dY = dy_ref[rows, :].astype(F32)
            xdt = xs * dts
            xdtb = xdt.astype(BF16)
            dYb = dY.astype(BF16)
            s0 = st_ref[ci].reshape(GW, SN)
            s0b = s0.astype(BF16)
            ds1 = dst[...]
            ds1b = ds1.astype(BF16)
            ws = jnp.exp(a_p[L - 1:L, :] - a_p)
            dR = (jnp.exp(a_p) * dY).astype(BF16)
            dX2 = ws * _dot(B, ds1b, NT)
            dgsum = jnp.zeros((L, L), F32)
            dX1, yd = [], []
            for jp in range(HPG // 2):
                lanes = slice(jp * LANES, (jp + 1) * LANES)
                x_pair, dy_pair = xdtb[:, lanes], dYb[:, lanes]
                h1, h2 = [], []
                for h, j in enumerate((2 * jp, 2 * jp + 1)):
                    dm = jnp.where(causal, jnp.exp(acs[:, j * LANES:(j + 1) * LANES] - at_ref[j:j + 1, rows]), 0.0)
                    mine = low if h == 0 else jnp.logical_not(low)
                    dgsum = dgsum + _dot(jnp.where(mine, dy_pair, jnp.zeros_like(dy_pair)), x_pair, NT) * dm
                    Mb = (G * dm).astype(BF16)
                    h1.append(_dot(Mb, dy_pair, TN))
                    h2.append(_dot(Mb, x_pair, NN))
                dX1.append(_pair_select(h1, P))
                yd.append(_pair_select(h2, P))
            dX1 = jnp.concatenate(dX1, axis=1)
            dX = dX1 + dX2
            pair = (dYb.astype(F32) - dY) * jnp.concatenate(yd, axis=1) - xdtb.astype(F32) * dX1
            through = _segment_sums(xdt * dX2, sel)
            u = ds1 * s0
            u_hi = u.astype(BF16)
            u_rows = _dot(ones, jnp.concatenate([u_hi, (u - u_hi.astype(F32)).astype(BF16)], axis=0), NT)
            u_rows = u_rows[:, :GW] + u_rows[:, GW:]
            at_end = jnp.exp(acn[L - 1:L, :]) * _segment_sums(u_rows, sel)[0:1, :] + \
                jnp.sum(through, axis=0, keepdims=True)
            dac_ref[0, rows, :] = _segment_sums(dY * y_ref[rows, :] + pair, sel) - through + \
                jnp.where(is_last, at_end, 0.0)
            dxsum_ref[0, rows, :] = _segment_sums(dX * xs, sel)
            dxs_ref[rows, :] = dX * dts + dk_ref[...] * dY
            dst[...] = _head_rows(jnp.exp(acs[L - 1:L, :]), HPG, P) * ds1 + _dot(dR, C, TN)
            dgb = dgsum.astype(BF16)
            dc_ref[rows, :] = _dot(dR, s0b, NN) + _dot(dgb, B, NN)
            db_ref[rows, :] = _dot((xdt * ws).astype(BF16), ds1b, NN) + _dot(dgb, C, TN)

    R = CPS * L
    wide = pl.BlockSpec((R, GW), lambda g, c: (rc(c), g))
    colspec = pl.BlockSpec((1, R, LANES), lambda g, c: (g, rc(c), 0))
    whole = pl.BlockSpec(memory_space=pl.ANY)
    arrs, gathers = side
    grid = (SG, nc // CPS)
    res = pl.pallas_call(
        with_exchange(body, 12, 5, gathers, grid), name=name, grid=grid,
        in_specs=[wide,
                  pl.BlockSpec((R, SN), lambda g, c: (rc(c), bcol + g)),
                  pl.BlockSpec((R, SN), lambda g, c: (rc(c), ccol + g)),
                  pl.BlockSpec((R, LANES), lambda g, c: (rc(c), 0)),
                  pl.BlockSpec((R, LANES), lambda g, c: (rc(c), 0)),
                  pl.BlockSpec((HPG, R), lambda g, c: (g, rc(c))),
                  pl.BlockSpec((1, LANES, GW), lambda g, c: (g, 0, 0)),
                  pl.BlockSpec((1, LANES, HPG * LANES), lambda g, c: (g, 0, 0)),
                  pl.BlockSpec((CPS, HPG, P, SN), lambda g, c: (rc(c), g, 0, 0)),
                  wide, wide,
                  pl.BlockSpec((1, GW), lambda g, c: (0, g))] + [whole] * len(arrs),
        out_specs=[wide,
                   pl.BlockSpec((R, SN), lambda g, c: (rc(c), g)),
                   pl.BlockSpec((R, SN), lambda g, c: (rc(c), g)),
                   colspec, colspec] + [whole] * len(arrs),
        out_shape=[jax.ShapeDtypeStruct((S, SI), F32), jax.ShapeDtypeStruct((S, cfg.GN), F32),
                   jax.ShapeDtypeStruct((S, cfg.GN), F32),
                   jax.ShapeDtypeStruct((SG, S, LANES), F32), jax.ShapeDtypeStruct((SG, S, LANES), F32)] +
        _exchange_shapes(arrs, gathers),
        scratch_shapes=[pltpu.VMEM((GW, SN), F32)] + _exchange_sems(len(arrs)),
        compiler_params=_cp(("arbitrary", "arbitrary")),
    )(xc, xc, xc, dt, acum, act, sel_p, sel_l, states, y, dy, dvec, *arrs)
    return res[:5], res[5:]


def dt_bwd(dac, dxsum, dt_raw, dt, dt_bias, a_log, cfg, name):
    S, L, SG = cfg.S, cfg.L, cfg.SG
    R = CPS * L

    def body(da_ref, dx_ref, x_ref, dt_ref, b_ref, al_ref, o_ref, gb_ref, ga_ref):
        a = -jnp.exp(al_ref[...])
        dtv = dt_ref[...]
        dxs = jnp.sum(dx_ref[...], axis=0)
        upper = jnp.where(lax.broadcasted_iota(jnp.int32, (L, L), 1) >= lax.broadcasted_iota(jnp.int32, (L, L), 0),
                          1.0, 0.0).astype(F32)
        dac = jnp.sum(da_ref[...], axis=0)
        dda = jnp.concatenate([lax.dot_general(upper, dac[ci * L:(ci + 1) * L, :], (NN, ((), ())),
                                               precision=lax.Precision.HIGHEST, preferred_element_type=F32)
                               for ci in range(CPS)], axis=0)
        draw = (dxs + dda * a) * _sigmoid(x_ref[...] + b_ref[...])
        o_ref[...] = draw.astype(BF16)
        gb = jnp.sum(draw, axis=0, keepdims=True)
        ga = jnp.sum(dda * dtv, axis=0, keepdims=True) * a

        @pl.when(pl.program_id(0) == 0)
        def _():
            gb_ref[...] = gb
            ga_ref[...] = ga

        @pl.when(pl.program_id(0) > 0)
        def _():
            gb_ref[...] += gb
            ga_ref[...] += ga

    row = pl.BlockSpec((R, LANES), lambda i: (i, 0))
    vec = pl.BlockSpec((1, LANES), lambda i: (0, 0))
    return pl.pallas_call(
        body, name=name, grid=(S // R,),
        in_specs=[pl.BlockSpec((SG, R, LANES), lambda i: (0, i, 0))] * 2 + [row, row, vec, vec],
        out_specs=[row, vec, vec],
        out_shape=[jax.ShapeDtypeStruct((S, LANES), BF16), jax.ShapeDtypeStruct((1, LANES), F32),
                   jax.ShapeDtypeStruct((1, LANES), F32)],
        compiler_params=_cp(("arbitrary",)),
    )(dac, dxsum, dt_raw, dt, dt_bias, a_log)


def gated_norm_fwd(y, xc, proj, dvec, nw, cfg, name, tm=128):
    S, SI = cfg.S, cfg.SI

    def body(y_ref, xs_ref, z_ref, d_ref, w_ref, o_ref):
        z = z_ref[...]
        yg = (y_ref[...] + d_ref[...] * xs_ref[...]) * (z * _sigmoid(z))
        r = lax.rsqrt(jnp.mean(yg * yg, axis=-1, keepdims=True) + RMS_EPS)
        o_ref[...] = ((yg * r) * w_ref[...]).astype(BF16)

    row = pl.BlockSpec((tm, SI), lambda i: (i, 0))
    vec = pl.BlockSpec((1, SI), lambda i: (0, 0))
    return pl.pallas_call(
        body, name=name, grid=(S // tm,),
        in_specs=[row, row, pl.BlockSpec((tm, SI), lambda i: (i, _blk(cfg.zs0, SI))), vec, vec],
        out_specs=row, out_shape=jax.ShapeDtypeStruct((S, SI), BF16),
        compiler_params=_cp(("parallel",)),
    )(y, xc, proj, dvec, nw)


def gated_norm_bwd(dyn, y, xc, proj, dvec, nw, cfg, name, tm=128):
    S, SI = cfg.S, cfg.SI

    def body(dn_ref, y_ref, xs_ref, z_ref, d_ref, w_ref, dy_ref, dz_ref, gw_ref, gd_ref):
        z = z_ref[...]
        s = _sigmoid(z)
        sz = z * s
        xs = xs_ref[...]
        yf = y_ref[...] + d_ref[...] * xs
        yg = yf * sz
        r = lax.rsqrt(jnp.mean(yg * yg, axis=-1, keepdims=True) + RMS_EPS)
        dn = dn_ref[...].astype(F32)
        g = dn * w_ref[...]
        dyg = r * g - yg * (r * r * r) * jnp.mean(g * yg, axis=-1, keepdims=True)
        dy = dyg * sz
        dy_ref[...] = dy.astype(BF16)
        dz_ref[...] = (dyg * yf * (s * (1.0 + z * (1.0 - s)))).astype(BF16)
        gw = jnp.sum(dn * (yg * r), axis=0, keepdims=True)
        gd = jnp.sum(dy * xs, axis=0, keepdims=True)

        @pl.when(pl.program_id(0) == 0)
        def _():
            gw_ref[...] = gw
            gd_ref[...] = gd

        @pl.when(pl.program_id(0) > 0)
        def _():
            gw_ref[...] += gw
            gd_ref[...] += gd

    row = pl.BlockSpec((tm, SI), lambda i: (i, 0))
    vec = pl.BlockSpec((1, SI), lambda i: (0, 0))
    return pl.pallas_call(
        body, name=name, grid=(S // tm,),
        in_specs=[row, row, row, pl.BlockSpec((tm, SI), lambda i: (i, _blk(cfg.zs0, SI))), vec, vec],
        out_specs=[row, row, vec, vec],
        out_shape=[jax.ShapeDtypeStruct((S, SI), BF16), jax.ShapeDtypeStruct((S, SI), BF16),
                   jax.ShapeDtypeStruct((1, SI), F32), jax.ShapeDtypeStruct((1, SI), F32)],
        compiler_params=_cp(("arbitrary",)),
    )(dyn, y, xc, proj, dvec, nw)


def _shard_columns(cfg, main, dt):
    dt0 = 4 * cfg.AW + cfg.SI + cfg.CD
    ws = cfg.N_IN // N_DEV
    out = []
    for k in range(N_DEV):
        lo, hi, parts = k * ws, (k + 1) * ws, []
        if lo < dt0:
            parts.append((main, lo, min(hi, dt0)))
        if lo < dt0 + cfg.SH and hi > dt0:
            parts.append((dt, max(lo, dt0) - dt0, min(hi, dt0 + cfg.SH) - dt0))
        if hi > dt0 + cfg.SH:
            parts.append((main, max(lo, dt0 + cfg.SH) - cfg.SH, hi - cfg.SH))
        out.append(parts)
    return out


def local_step(cfg, x, tgt, norm_w, conv_w, conv_b, dt_bias, a_log, d_skip, ssm_norm_w, final_norm_w,
               w_main, w_dt, shards, dt0):
    S, D = cfg.S, cfg.D
    slopes = _slope_table(cfg)
    dt_bias_p = _pad_lanes(dt_bias)
    a_log_p = _pad_lanes(a_log)
    dvec = _spread(d_skip, cfg.P)

    hn = rmsnorm_fwd(x, norm_w, "rmsnorm_fwd")
    proj, gathered = matmul(hn, w_main, 'nn', 1024, 2048, 2048, F32, "in_proj", side=(shards, [True] * 3))
    w_attn, w_ssm, w_out = gathered[0].reshape(cfg.AW, D), gathered[1].reshape(cfg.SI, D), gathered[2].reshape(D, D)
    o_a, o_mix, ltot = attn_fused_fwd(proj, slopes, cfg, "attn_fwd")
    xc = conv_fwd(proj, conv_w, conv_b, cfg, "conv_fwd")
    dt_raw, dt, acum = ssd_prep(hn, w_dt, dt_bias_p, a_log_p, cfg, "ssd_prep")
    act = acum[:, :cfg.SH].T
    sel_p, sel_l = _head_selectors(cfg)
    y, states = ssd_scan_fwd(xc, dt, acum, act, sel_p, sel_l, cfg, "ssd_scan_fwd")
    y_n = gated_norm_fwd(y, xc, proj, dvec, ssm_norm_w, cfg, "gated_norm_fwd")
    a_out, s_out, merged = branch_merge(o_a, w_attn, y_n, w_ssm, proj, cfg, "branch_merge")
    dout, loss_p, g_final_w = out_proj_final(merged, w_out, x, final_norm_w.reshape(1, D), tgt, "out_proj_final")

    g_w_out = matmul(merged, dout, 'tn', 1024, 1024, 2048, BF16, "g_w_out")
    da_out, ds_out, dga, dgs = merge_bwd(dout, w_out, a_out, s_out, proj, cfg, "merge_bwd")
    g_w_attn = matmul(o_a, da_out, 'tn', 1024, 1024, 2048, BF16, "g_w_attn")
    g_w_ssm = matmul(y_n, ds_out, 'tn', 1024, 1024, 2048, BF16, "g_w_ssm")
    do_a = matmul(da_out, w_attn, 'nt', 512, 1024, 2048, BF16, "d_o_a")
    dyn = matmul(ds_out, w_ssm, 'nt', 512, 1024, 2048, BF16, "d_y_n")
    dy, dz_s, g_ssm_norm, g_dvec = gated_norm_bwd(dyn, y, xc, proj, dvec, ssm_norm_w, cfg, "gated_norm_bwd")
    sends = [g.reshape((N_DEV, g.shape[0] // N_DEV, D)) for g in (g_w_attn, g_w_ssm, g_w_out)]
    (dxs, dB, dC, dac_g, dxsum_g), (r_attn, r_ssm, r_out) = ssd_scan_bwd(
        xc, dt, acum, act, sel_p, sel_l, states, y, dy, dvec, cfg, "ssd_scan_bwd", side=(sends, [False] * 3))
    ddt_raw, g_dt_bias, g_a_log = dt_bwd(dac_g, dxsum_g, dt_raw, dt, dt_bias_p, a_log_p, cfg, "dt_bwd")
    dxbc, g_cw, g_cb = [], [], []
    for nm, piece, c_off in (("xs", dxs, 0), ("b", dB, cfg.SI), ("c", dC, cfg.SI + cfg.GN)):
        dx, gw, gb = conv_bwd(proj, piece, conv_w, conv_b, cfg, "conv_bwd_" + nm, c_off)
        dxbc.append(dx)
        g_cw.append(gw)
        g_cb.append(gb)
    g_conv_w, g_conv_b = jnp.concatenate(g_cw, axis=1), jnp.concatenate(g_cb, axis=1)
    dq, dk, dv, dz_a = attn_fused_bwd(proj, do_a, o_mix, ltot, slopes, cfg, "attn_bwd")
    dproj = jnp.concatenate([dq, dk, dv, dz_a, dz_s] + dxbc + [dga, dgs], axis=1)
    def slabs(g_main, g_dt):
        return jnp.stack([jnp.concatenate([g[:, lo:hi] for g, lo, hi in parts], axis=1)
                          for parts in _shard_columns(cfg, g_main, g_dt)])

    cut = D // 4
    g_w_dt = matmul(hn, ddt_raw, 'tn', 1024, 128, 2048, BF16, "g_w_dt")
    g_w_top = matmul(hn, dproj, 'tn', 512, 2048, 2048, BF16, "g_w_main_top", a_cols=(0, cut))
    g_w_bot, (r_top,) = matmul(hn, dproj, 'tn', 512, 2048, 2048, BF16, "g_w_main_rest", a_cols=(cut, D - cut),
                               side=([slabs(g_w_top, g_w_dt[:cut])], [False]))
    dhn_a, (r_bot,) = matmul(dproj, w_main, 'nt', 1024, 1024, 2048, F32, "d_hn",
                             side=([slabs(g_w_bot, g_w_dt[cut:])], [False]))
    grad_x, g_norm_w = rmsnorm_bwd(dhn_a, ddt_raw, w_dt, x, norm_w, dout, "rmsnorm_bwd")

    g_d_skip = jnp.sum(g_dvec.reshape(cfg.SH, cfg.P), axis=1).reshape(1, cfg.SH)
    small = dict(norm_w=g_norm_w, conv_b=g_conv_b, dt_bias=g_dt_bias[:, :cfg.SH], a_log=g_a_log[:, :cfg.SH],
                 d_skip=g_d_skip, ssm_norm_w=g_ssm_norm, final_norm_w=g_final_w, conv_w=g_conv_w[:cfg.KC])
    return loss_p, grad_x, small, dict(w_in=[r_top, r_bot], w_attn=[r_attn], w_ssm=[r_ssm], w_out=[r_out])


def _mesh_pos():
    return lax.axis_index("x"), lax.axis_index("y"), lax.axis_index("c")


def _flat(pos):
    return 4 * pos[0] + 2 * pos[1] + pos[2]


def _exchange_shapes(arrs, gathers):
    return [jax.ShapeDtypeStruct(((N_DEV,) + a.shape) if g else a.shape, a.dtype) for a, g in zip(arrs, gathers)]


def _exchange_sems(n):
    return [pltpu.SemaphoreType.DMA((n * (N_DEV - 1),)), pltpu.SemaphoreType.DMA((n * (N_DEV - 1),)),
            pltpu.SemaphoreType.DMA((n,))]


def _exchange_copies(ins, outs, gathers, send_sems, recv_sems, loc_sems):
    pos = _mesh_pos()
    me = _flat(pos)
    starts, waits = [], []
    for a in range(len(ins)):
        mine = ins[a] if gathers[a] else ins[a].at[me]
        loc = pltpu.make_async_copy(mine, outs[a].at[me], loc_sems.at[a])
        starts.append(loc)
        waits.append(loc)
        for k in range(1, N_DEV):
            flip = ((k >> 2) & 1, (k >> 1) & 1, k & 1)
            peer = tuple(1 - p if f else p for p, f in zip(pos, flip))
            pk = _flat(peer)
            src = ins[a] if gathers[a] else ins[a].at[pk]
            sems = dict(send_sem=send_sems.at[a * (N_DEV - 1) + k - 1], recv_sem=recv_sems.at[a * (N_DEV - 1) + k - 1],
                        device_id=peer, device_id_type=pl.DeviceIdType.MESH)
            starts.append(pltpu.make_async_remote_copy(src_ref=src, dst_ref=outs[a].at[me], **sems))
            waits.append(pltpu.make_async_remote_copy(src_ref=src, dst_ref=outs[a].at[pk], **sems))
    return starts, waits


def exchange(arrs, gathers, name):
    n = len(arrs)

    def body(*refs):
        starts, waits = _exchange_copies(refs[:n], refs[n:2 * n], gathers, *refs[2 * n:])
        for cp in starts:
            cp.start()
        for cp in waits:
            cp.wait()

    hbm = pl.BlockSpec(memory_space=pltpu.HBM)
    return pl.pallas_call(
        body, name=name, in_specs=[hbm] * n, out_specs=[hbm] * n, out_shape=_exchange_shapes(arrs, gathers),
        scratch_shapes=_exchange_sems(n),
    )(*arrs)


def gather_two_level(arrs, chunks, name):
    n = len(arrs)

    def body(*refs):
        ins, outs = refs[:n], refs[n:2 * n]
        send_sems, recv_sems, loc_sems = refs[2 * n:]
        x, y, c = _mesh_pos()
        me, sib = (x, y, c), (x, y, 1 - c)
        chips = [(1 - x, y), (x, 1 - y), (1 - x, 1 - y)]
        plan, base = [], 0
        for a in range(n):
            step = arrs[a].shape[0] // chunks[a]
            for q in range(chunks[a]):
                plan.append((a, pl.ds(q * step, step), base))
                base += N_DEV - 1

        def copy(a, rows, sem, block, to, own=False):
            dst = outs[a].at[_flat(block), rows]
            return pltpu.make_async_remote_copy(
                src_ref=ins[a].at[rows] if own else dst, dst_ref=dst, send_sem=send_sems.at[sem],
                recv_sem=recv_sems.at[sem], device_id=to, device_id_type=pl.DeviceIdType.MESH)

        local = [pltpu.make_async_copy(ins[a], outs[a].at[_flat(me)], loc_sems.at[a]) for a in range(n)]
        for cp in local:
            cp.start()
        sent = []
        for a, rows, s in plan:
            sent.append(copy(a, rows, s, me, sib, own=True))
            sent += [copy(a, rows, s + 1 + j, me, (*chip, c), own=True) for j, chip in enumerate(chips)]
        for cp in sent:
            cp.start()
        for a, rows, s in plan:
            for j, chip in enumerate(chips):
                copy(a, rows, s + 1 + j, (*chip, c), me).wait_recv()
                passed = copy(a, rows, s + 4 + j, (*chip, c), sib)
                passed.start()
                sent.append(passed)
        for a, rows, s in plan:
            copy(a, rows, s, sib, me).wait_recv()
            for j, chip in enumerate(chips):
                copy(a, rows, s + 4 + j, (*chip, 1 - c), me).wait_recv()
        for cp in sent:
            cp.wait_send()
        for cp in local:
            cp.wait()

    hbm = pl.BlockSpec(memory_space=pltpu.HBM)
    nsem = (N_DEV - 1) * sum(chunks)
    return pl.pallas_call(
        body, name=name, in_specs=[hbm] * n, out_specs=[hbm] * n, out_shape=_exchange_shapes(arrs, [True] * n),
        scratch_shapes=[pltpu.SemaphoreType.DMA((nsem,)), pltpu.SemaphoreType.DMA((nsem,)),
                        pltpu.SemaphoreType.DMA((n,))],
    )(*arrs)


def with_exchange(body, n_in, n_out, gathers, grid):
    n = len(gathers)

    def wrapped(*refs):
        ins, sends = refs[:n_in], refs[n_in:n_in + n]
        outs, recvs = refs[n_in + n:n_in + n + n_out], refs[n_in + 2 * n + n_out - n:n_in + 2 * n + n_out]
        scratch, sems = refs[n_in + 2 * n + n_out:-3], refs[-3:]
        ids = [pl.program_id(d) for d in range(len(grid))]
        first = functools.reduce(jnp.logical_and, [i == 0 for i in ids])
        last = functools.reduce(jnp.logical_and, [i == g - 1 for i, g in zip(ids, grid)])

        @pl.when(first)
        def _():
            for cp in _exchange_copies(sends, recvs, gathers, *sems)[0]:
                cp.start()

        body(*ins, *outs, *scratch)

        @pl.when(last)
        def _():
            for cp in _exchange_copies(sends, recvs, gathers, *sems)[1]:
                cp.wait()

    return wrapped


def adamw(g_src, w, m, v, summed, name, tr=64):
    R, C = w.shape
    tr = min(tr, R)
    assert R % tr == 0
    parts = g_src if summed else [g_src]
    starts = [sum(p.shape[1] for p in parts[:k]) // tr for k in range(len(parts))] if summed else [0]
    counts = [p.shape[1] // tr for p in parts] if summed else [R // tr]

    def body(*refs):
        g_refs, (w_ref, m_ref, v_ref, g_out, d_out, m_out, v_out) = refs[:len(parts)], refs[len(parts):]
        if summed:
            g = None
            for k, g_ref in enumerate(g_refs):
                gk = g_ref[0].astype(F32)
                for j in range(1, N_DEV):
                    gk = gk + g_ref[j].astype(F32)
                g = gk if g is None else jnp.where(pl.program_id(0) >= starts[k], gk, g)
        else:
            g = g_refs[0][...]
        mn = ADAM_B1 * m_ref[...] + (1.0 - ADAM_B1) * g
        vn = ADAM_B2 * v_ref[...] + (1.0 - ADAM_B2) * (g * g)
        m_hat = mn / (1.0 - ADAM_B1 ** ADAM_STEP)
        v_hat = vn / (1.0 - ADAM_B2 ** ADAM_STEP)
        g_out[...] = g
        d_out[...] = -ADAM_LR * (m_hat / (jnp.sqrt(v_hat) + ADAM_EPS) + ADAM_WD * w_ref[...])
        m_out[...] = mn
        v_out[...] = vn

    row = pl.BlockSpec((tr, C), lambda i: (i, 0))
    if summed:
        gspecs = [pl.BlockSpec((N_DEV, tr, C), lambda i, s=st, n=nb: (0, jnp.clip(i - s, 0, n - 1), 0))
                  for st, nb in zip(starts, counts)]
    else:
        gspecs = [row]
    sh = jax.ShapeDtypeStruct((R, C), F32)
    return pl.pallas_call(
        body, name=name, grid=(R // tr,), in_specs=gspecs + [row, row, row], out_specs=[row] * 4, out_shape=[sh] * 4,
        compiler_params=_cp(("parallel",)),
    )(*parts, w, m, v)


SMALL = ('norm_w', 'conv_b', 'dt_bias', 'a_log', 'd_skip', 'ssm_norm_w', 'final_norm_w')


def _rows(n):
    return -(-n // (8 * LANES)) * 8


def _pack(vals):
    parts = []
    for a in vals:
        f = a.reshape(-1)
        parts.append(jnp.pad(f, (0, _rows(f.size) * LANES - f.size)).reshape(-1, LANES))
    return jnp.concatenate(parts, axis=0)


def _unpack(packed, shapes):
    out, r = [], 0
    for s in shapes:
        n = math.prod(s)
        out.append(packed[r:r + _rows(n)].reshape(-1)[:n].reshape(s))
        r += _rows(n)
    return out


def kernel(x, norm_w, w_in, conv_w, conv_b, dt_bias, a_log, d_skip, ssm_norm_w, w_attn_branch, w_ssm_branch, w_out, final_norm_w, loss_target, m_norm_w, m_w_in, m_conv_w, m_conv_b, m_dt_bias, m_a_log, m_d_skip, m_ssm_norm_w, m_w_attn_branch, m_w_ssm_branch, m_w_out, m_final_norm_w, v_norm_w, v_w_in, v_conv_w, v_conv_b, v_dt_bias, v_a_log, v_d_skip, v_ssm_norm_w, v_w_attn_branch, v_w_ssm_branch, v_w_out, v_final_norm_w):
    cfg = CFG
    D, SH = cfg.D, cfg.SH
    me = _flat(_mesh_pos())
    dt0 = 4 * cfg.AW + cfg.SI + cfg.CD
    ws = w_in.shape[-1]

    g_in, g_cw = gather_two_level([w_in[0].astype(BF16), conv_w[0]], [4, 1], "gather_w_in")
    main_cols, dt_cols = [], []
    for k, parts in enumerate(_shard_columns(cfg, "main", "dt")):
        at = 0
        for which, lo, hi in parts:
            (main_cols if which == "main" else dt_cols).append(g_in[k][:, at:at + hi - lo])
            at += hi - lo
    w_main = jnp.concatenate(main_cols, axis=1)
    w_dt = _pad_lanes(jnp.concatenate(dt_cols, axis=1))
    conv_full = g_cw.transpose(1, 0, 2).reshape(cfg.KC, cfg.CD)
    shards = [w_attn_branch[0].astype(BF16), w_ssm_branch[0].astype(BF16), w_out[0].astype(BF16)]

    loss_p, grad_x, small, recv = local_step(
        cfg, x[0], loss_target[0], norm_w, conv_full, conv_b, dt_bias, a_log, d_skip,
        ssm_norm_w, final_norm_w, w_main, w_dt, shards, dt0)

    upd = {}
    upd['w_in'] = adamw(recv['w_in'], w_in[0], m_w_in[0], v_w_in[0], True, "adamw_w_in")
    upd['w_attn_branch'] = adamw(recv['w_attn'], w_attn_branch[0], m_w_attn_branch[0], v_w_attn_branch[0], True,
                                 "adamw_w_attn")
    upd['w_ssm_branch'] = adamw(recv['w_ssm'], w_ssm_branch[0], m_w_ssm_branch[0], v_w_ssm_branch[0], True,
                                "adamw_w_ssm")
    upd['w_out'] = adamw(recv['w_out'], w_out[0], m_w_out[0], v_w_out[0], True, "adamw_w_out")

    extra = [jnp.zeros((cfg.KC, cfg.CD), F32), jnp.zeros((1, 1), F32)]
    shapes = [small[n].shape for n in SMALL] + [e.shape for e in extra]
    part = _pack([small[n] for n in SMALL] + [small['conv_w'], loss_p[:, :1]])
    gathered, = exchange([part], [True], "gather_small")
    given = dict(norm_w=(norm_w, m_norm_w, v_norm_w), conv_b=(conv_b, m_conv_b, v_conv_b),
                 dt_bias=(dt_bias, m_dt_bias, v_dt_bias), a_log=(a_log, m_a_log, v_a_log),
                 d_skip=(d_skip, m_d_skip, v_d_skip), ssm_norm_w=(ssm_norm_w, m_ssm_norm_w, v_ssm_norm_w),
                 final_norm_w=(final_norm_w, m_final_norm_w, v_final_norm_w))
    packed = [_pack([given[n][t] for n in SMALL] + extra) for t in range(3)]
    outs = adamw([gathered], *packed, True, "adamw_small", tr=part.shape[0])
    unpacked = [_unpack(o, shapes) for o in outs]
    for i, n in enumerate(SMALL):
        upd[n] = [u[i].reshape(given[n][0].shape) for u in unpacked]
    loss = unpacked[0][-1].reshape(())
    cw = conv_w.shape[-1]
    g_cw_mine = lax.dynamic_slice_in_dim(unpacked[0][-2], me * cw, cw, axis=1)
    upd['conv_w'] = adamw(g_cw_mine.reshape(-1, LANES), conv_w.reshape(-1, LANES), m_conv_w.reshape(-1, LANES),
                          v_conv_w.reshape(-1, LANES), False, "adamw_conv_w")

    order = ['norm_w', 'w_in', 'conv_w', 'conv_b', 'dt_bias', 'a_log', 'd_skip', 'ssm_norm_w', 'w_attn_branch',
             'w_ssm_branch', 'w_out', 'final_norm_w']
    like = dict(norm_w=norm_w, w_in=w_in, conv_w=conv_w, conv_b=conv_b, dt_bias=dt_bias, a_log=a_log, d_skip=d_skip,
                ssm_norm_w=ssm_norm_w, w_attn_branch=w_attn_branch, w_ssm_branch=w_ssm_branch, w_out=w_out,
                final_norm_w=final_norm_w)
    result = [loss, grad_x[None]]
    for t in range(4):
        result += [upd[n][t].reshape(like[n].shape) for n in order]
    return tuple(result)
```

```python
import functools
import math
from typing import NamedTuple

import jax
import jax.numpy as jnp
from jax import lax
from jax.experimental import pallas as pl
from jax.experimental.pallas import tpu as pltpu

F32 = jnp.float32
BF16 = jnp.bfloat16
RMS_EPS = 1e-6
NEG = -1e30
N_DEV = 8
CPS = 8
LANES = 128
ATTN_BLOCK = 128
ADAM_LR, ADAM_B1, ADAM_B2, ADAM_EPS, ADAM_WD, ADAM_STEP = 0.001, 0.9, 0.999, 1e-08, 0.01, 10
VMEM_LIMIT = 56 * 1024 * 1024


class Cfg(NamedTuple):
    D: int = 2048
    S: int = 8192
    AH: int = 16
    E: int = 128
    patterns: tuple = ((128, 1), (512, 4), (2048, 16))
    SI: int = 4096
    P: int = 64
    SG: int = 8
    SN: int = 128
    KC: int = 4
    L: int = 128

    @property
    def AW(self): return self.AH * self.E
    @property
    def SH(self): return self.SI // self.P
    @property
    def HPG(self): return self.SH // self.SG
    @property
    def GN(self): return self.SG * self.SN
    @property
    def CD(self): return self.SI + 2 * self.GN
    @property
    def k0(self): return self.AW
    @property
    def v0(self): return 2 * self.AW
    @property
    def za0(self): return 3 * self.AW
    @property
    def zs0(self): return 4 * self.AW
    @property
    def xbc0(self): return 4 * self.AW + self.SI
    @property
    def ga0(self): return self.xbc0 + self.CD
    @property
    def gs0(self): return self.ga0 + self.D
    @property
    def NP(self): return self.gs0 + self.D
    @property
    def N_IN(self): return self.NP + self.SH


CFG = Cfg()


def _cp(sem=None, vmem=VMEM_LIMIT):
    return pltpu.CompilerParams(dimension_semantics=sem, vmem_limit_bytes=vmem)


def _sigmoid(z):
    return 1.0 / (1.0 + jnp.exp(-z))


def _dot(a, b, dims):
    return lax.dot_general(a, b, (dims, ((), ())), preferred_element_type=F32)


NN = ((1,), (0,))
NT = ((1,), (1,))
TN = ((0,), (0,))


def _blk(off, width):
    assert off % width == 0, (off, width)
    return off // width


def matmul(a, b, mode, tm, tn, tk, out_dtype, name, side=None, a_cols=None):
    m0 = 0
    if mode == 'nn':
        (M, K), (_, N) = a.shape, b.shape
    elif mode == 'nt':
        (M, K), (N, _) = a.shape, b.shape
    else:
        (K, M), (_, N) = a.shape, b.shape
        if a_cols is not None:
            m0, M = a_cols
    tm, tn, tk = math.gcd(min(tm, M), m0), min(tn, N), min(tk, K)
    assert M % tm == 0 and N % tn == 0 and K % tk == 0, (M, N, K, tm, tn, tk)
    nk = K // tk
    dims = {'nn': NN, 'nt': NT, 'tn': TN}[mode]

    def body(a_ref, b_ref, o_ref, *acc):
        part = _dot(a_ref[...].astype(BF16), b_ref[...].astype(BF16), dims)
        if nk == 1:
            o_ref[...] = part.astype(out_dtype)
        else:
            acc_ref, = acc
            k = pl.program_id(2)

            @pl.when(k == 0)
            def _():
                acc_ref[...] = part

            @pl.when(k > 0)
            def _():
                acc_ref[...] += part

            @pl.when(k == nk - 1)
            def _():
                o_ref[...] = acc_ref[...].astype(out_dtype)

    if mode == 'tn':
        a_spec = pl.BlockSpec((tk, tm), lambda n, m, k: (k, _blk(m0, tm) + m))
    else:
        a_spec = pl.BlockSpec((tm, tk), lambda n, m, k: (m, k))
    if mode == 'nt':
        b_spec = pl.BlockSpec((tn, tk), lambda n, m, k: (n, k))
    else:
        b_spec = pl.BlockSpec((tk, tn), lambda n, m, k: (k, n))
    grid = (N // tn, M // tm, nk)
    o_spec = pl.BlockSpec((tm, tn), lambda n, m, k: (m, n))
    o_shape = jax.ShapeDtypeStruct((M, N), out_dtype)
    acc = [] if nk == 1 else [pltpu.VMEM((tm, tn), F32)]
    if side is None:
        return pl.pallas_call(
            body, name=name, grid=grid, in_specs=[a_spec, b_spec], out_specs=o_spec, out_shape=o_shape,
            scratch_shapes=acc, compiler_params=_cp(("parallel", "parallel", "arbitrary")),
        )(a, b)
    arrs, gathers = side
    whole = pl.BlockSpec(memory_space=pl.ANY)
    res = pl.pallas_call(
        with_exchange(body, 2, 1, gathers, grid), name=name, grid=grid,
        in_specs=[a_spec, b_spec] + [whole] * len(arrs), out_specs=[o_spec] + [whole] * len(arrs),
        out_shape=[o_shape] + _exchange_shapes(arrs, gathers),
        scratch_shapes=acc + _exchange_sems(len(arrs)),
        compiler_params=_cp(("arbitrary", "arbitrary", "arbitrary")),
    )(a, b, *arrs)
    return res[0], res[1:]


def rmsnorm_fwd(x, w, name, tm=256):
    S, D = x.shape

    def body(x_ref, w_ref, o_ref):
        xv = x_ref[...]
        r = lax.rsqrt(jnp.mean(xv * xv, axis=-1, keepdims=True) + RMS_EPS)
        o_ref[...] = ((xv * r) * w_ref[...]).astype(BF16)

    return pl.pallas_call(
        body, name=name, grid=(S // tm,),
        in_specs=[pl.BlockSpec((tm, D), lambda i: (i, 0)), pl.BlockSpec((1, D), lambda i: (0, 0))],
        out_specs=pl.BlockSpec((tm, D), lambda i: (i, 0)),
        out_shape=jax.ShapeDtypeStruct((S, D), BF16),
        compiler_params=_cp(("parallel",)),
    )(x, w)


def rmsnorm_bwd(dh_a, ddt, w_dt, x, w, dout, name, tm=256):
    S, D = x.shape

    def body(da_ref, dd_ref, wd_ref, x_ref, w_ref, do_ref, gx_ref, gw_ref):
        xv = x_ref[...]
        dh = da_ref[...] + _dot(dd_ref[...], wd_ref[...], NT)
        r = lax.rsqrt(jnp.mean(xv * xv, axis=-1, keepdims=True) + RMS_EPS)
        g = dh * w_ref[...]
        dx = r * g - xv * (r * r * r) * jnp.mean(g * xv, axis=-1, keepdims=True)
        gx_ref[...] = do_ref[...] + dx
        gw = jnp.sum(dh * (xv * r), axis=0, keepdims=True)

        @pl.when(pl.program_id(0) == 0)
        def _():
            gw_ref[...] = gw

        @pl.when(pl.program_id(0) > 0)
        def _():
            gw_ref[...] += gw

    row = pl.BlockSpec((tm, D), lambda i: (i, 0))
    vec = pl.BlockSpec((1, D), lambda i: (0, 0))
    return pl.pallas_call(
        body, name=name, grid=(S // tm,),
        in_specs=[row, pl.BlockSpec((tm, LANES), lambda i: (i, 0)), pl.BlockSpec((D, LANES), lambda i: (0, 0)), row,
                  vec, row],
        out_specs=[row, vec],
        out_shape=[jax.ShapeDtypeStruct((S, D), F32), jax.ShapeDtypeStruct((1, D), F32)],
        compiler_params=_cp(("arbitrary",)),
    )(dh_a, ddt, w_dt, x, w, dout)


def out_proj_final(merged, w_out, x, fw, tgt, name, tm=256):
    S, D = x.shape

    def body(m_ref, wo_ref, x_ref, w_ref, t_ref, do_ref, loss_ref, gw_ref):
        out = x_ref[...] + _dot(m_ref[...], wo_ref[...], NN)
        w = w_ref[...]
        r = lax.rsqrt(jnp.mean(out * out, axis=-1, keepdims=True) + RMS_EPS)
        yn = out * r
        err = yn * w - t_ref[...]
        lrow = 0.5 * jnp.mean(err * err, axis=-1, keepdims=True)
        lsum = jnp.zeros((1, LANES), F32) + jnp.sum(lrow, axis=0, keepdims=True)
        dfin = err * (1.0 / D)
        g = dfin * w
        do_ref[...] = r * g - out * (r * r * r) * jnp.mean(g * out, axis=-1, keepdims=True)
        gw = jnp.sum(dfin * yn, axis=0, keepdims=True)

        @pl.when(pl.program_id(0) == 0)
        def _():
            gw_ref[...] = gw
            loss_ref[...] = lsum

        @pl.when(pl.program_id(0) > 0)
        def _():
            gw_ref[...] += gw
            loss_ref[...] += lsum

    row = pl.BlockSpec((tm, D), lambda i: (i, 0))
    vec = pl.BlockSpec((1, D), lambda i: (0, 0))
    return pl.pallas_call(
        body, name=name, grid=(S // tm,),
        in_specs=[row, pl.BlockSpec((D, D), lambda i: (0, 0)), row, vec, row],
        out_specs=[row, pl.BlockSpec((1, LANES), lambda i: (0, 0)), vec],
        out_shape=[jax.ShapeDtypeStruct((S, D), F32), jax.ShapeDtypeStruct((1, LANES), F32),
                   jax.ShapeDtypeStruct((1, D), F32)],
        compiler_params=_cp(("arbitrary",)),
    )(merged, w_out, x, fw, tgt)


def branch_merge(o_a, w_attn, y_n, w_ssm, proj, cfg, name, tm=512, tn=512):
    S, D = cfg.S, cfg.D
    tm, tn = min(tm, S), min(tn, D)

    def body(oa_ref, wa_ref, yn_ref, ws_ref, ga_ref, gs_ref, a_ref, s_ref, m_ref):
        a = _dot(oa_ref[...], wa_ref[...], NN)
        sv = _dot(yn_ref[...], ws_ref[...], NN)
        a_ref[...] = a.astype(BF16)
        s_ref[...] = sv.astype(BF16)
        m_ref[...] = (_sigmoid(ga_ref[...]) * a + _sigmoid(gs_ref[...]) * sv).astype(BF16)

    tile = pl.BlockSpec((tm, tn), lambda n, m: (m, n))
    return pl.pallas_call(
        body, name=name, grid=(D // tn, S // tm),
        in_specs=[pl.BlockSpec((tm, cfg.AW), lambda n, m: (m, 0)), pl.BlockSpec((cfg.AW, tn), lambda n, m: (0, n)),
                  pl.BlockSpec((tm, cfg.SI), lambda n, m: (m, 0)), pl.BlockSpec((cfg.SI, tn), lambda n, m: (0, n)),
                  pl.BlockSpec((tm, tn), lambda n, m: (m, _blk(cfg.ga0, tn) + n)),
                  pl.BlockSpec((tm, tn), lambda n, m: (m, _blk(cfg.gs0, tn) + n))],
        out_specs=[tile, tile, tile],
        out_shape=[jax.ShapeDtypeStruct((S, D), BF16)] * 3,
        compiler_params=_cp(("parallel", "parallel")),
    )(o_a, w_attn, y_n, w_ssm, proj, proj)


def merge_bwd(dout, w_out, a_out, s_out, proj, cfg, name, tm=512, tn=1024):
    S, D = cfg.S, cfg.D
    tm, tn = min(tm, S), min(tn, D)

    def body(do_ref, wo_ref, a_ref, s_ref, ga_ref, gs_ref, da_ref, ds_ref, dga_ref, dgs_ref):
        dmv = _dot(do_ref[...].astype(BF16), wo_ref[...], NT)
        sa = _sigmoid(ga_ref[...])
        ss = _sigmoid(gs_ref[...])
        da_ref[...] = (dmv * sa).astype(BF16)
        ds_ref[...] = (dmv * ss).astype(BF16)
        dga_ref[...] = (dmv * a_ref[...] * (sa * (1.0 - sa))).astype(BF16)
        dgs_ref[...] = (dmv * s_ref[...] * (ss * (1.0 - ss))).astype(BF16)

    tile = pl.BlockSpec((tm, tn), lambda n, m: (m, n))
    sh = jax.ShapeDtypeStruct((S, D), BF16)
    return pl.pallas_call(
        body, name=name, grid=(D // tn, S // tm),
        in_specs=[pl.BlockSpec((tm, D), lambda n, m: (m, 0)), pl.BlockSpec((tn, D), lambda n, m: (n, 0)), tile, tile,
                  pl.BlockSpec((tm, tn), lambda n, m: (m, _blk(cfg.ga0, tn) + n)),
                  pl.BlockSpec((tm, tn), lambda n, m: (m, _blk(cfg.gs0, tn) + n))],
        out_specs=[tile] * 4, out_shape=[sh] * 4,
        compiler_params=_cp(("parallel", "parallel")),
    )(dout, w_out, a_out, s_out, proj, proj)


def _attn_rows(base, d):
    return pl.ds(base, ATTN_BLOCK) if d == 1 else pl.ds(base, ATTN_BLOCK, stride=d)


def _attn_units(cfg):
    dmax = max(d for _, d in cfg.patterns)
    units = []
    for p, (window, d) in enumerate(cfg.patterns):
        assert window // d == ATTN_BLOCK and dmax % d == 0
        nsub = dmax // d
        for b in range(nsub):
            for r in range(d):
                base = b * ATTN_BLOCK * d + r
                if b > 0:
                    units.append((p, d, base, (b - 1) * ATTN_BLOCK * d + r, False))
                else:
                    units.append((p, d, base, (nsub - 1) * ATTN_BLOCK * d + r, True))
    return units, ATTN_BLOCK * dmax


def _set_bias_tiles(bias_s, slope, cfg):
    qi = lax.broadcasted_iota(jnp.int32, (ATTN_BLOCK, ATTN_BLOCK), 0)
    ki = lax.broadcasted_iota(jnp.int32, (ATTN_BLOCK, ATTN_BLOCK), 1)
    for p, (_, d) in enumerate(cfg.patterns):
        bias_s[2 * p] = jnp.where(ki >= qi, (-slope) * ((ATTN_BLOCK + qi - ki) * d).astype(F32), NEG)
        bias_s[2 * p + 1] = jnp.where(ki <= qi, (-slope) * ((qi - ki) * d).astype(F32), NEG)


def _unit_scores(q, kcat, bias_s, p, prev_ok, scale):
    s = _dot(q, kcat, NT) * scale + jnp.concatenate([bias_s[2 * p], bias_s[2 * p + 1]], axis=1)
    if prev_ok is not None:
        cur_half = lax.broadcasted_iota(jnp.int32, s.shape, 1) >= ATTN_BLOCK
        s = jnp.where(jnp.logical_or(cur_half, prev_ok), s, NEG)
    return s


def _slope_table(cfg):
    slopes = jnp.asarray([2.0 ** (-8.0 * (h + 1) / cfg.AH) for h in range(cfg.AH)], F32)
    return jnp.broadcast_to(slopes.reshape(cfg.AH, 1, 1), (cfg.AH, 8, LANES))


def attn_fused_fwd(proj, slopes, cfg, name):
    S, E, AH = cfg.S, cfg.E, cfg.AH
    units, SB = _attn_units(cfg)
    assert S % SB == 0
    npat = len(cfg.patterns)
    scale = E ** -0.5

    def spec(off, prev):
        c0 = _blk(off, E)
        if prev:
            return pl.BlockSpec((SB, E), lambda h, i: (jnp.maximum(i - 1, 0), c0 + h))
        return pl.BlockSpec((SB, E), lambda h, i: (i, c0 + h))

    def body(q_ref, kp_ref, kc_ref, vp_ref, vc_ref, z_ref, sl_ref, oa_ref, om_ref, lt_ref, *scr):
        o_s, l_s, bias_s = scr[:npat], scr[npat:2 * npat], scr[2 * npat]
        i = pl.program_id(1)

        @pl.when(i == 0)
        def _():
            _set_bias_tiles(bias_s, sl_ref[0, 0:1, :], cfg)

        for p, d, base, pbase, from_prev in units:
            rows, prows = _attn_rows(base, d), _attn_rows(pbase, d)
            q = q_ref[rows, :].astype(BF16)
            kp = (kp_ref if from_prev else kc_ref)[prows, :].astype(BF16)
            vp = (vp_ref if from_prev else vc_ref)[prows, :].astype(BF16)
            kcat = jnp.concatenate([kp, kc_ref[rows, :].astype(BF16)], axis=0)
            vcat = jnp.concatenate([vp, vc_ref[rows, :].astype(BF16)], axis=0)
            s = _unit_scores(q, kcat, bias_s, p, (i > 0) if from_prev else None, scale)
            m = jnp.max(s, axis=1, keepdims=True)
            pr = jnp.exp(s - m)
            l = jnp.sum(pr, axis=1, keepdims=True)
            o_s[p][rows, :] = _dot(pr.astype(BF16), vcat, NN) * (1.0 / l)
            l_s[p][rows, :] = m + jnp.log(l)
        ls = [l_s[p][...] for p in range(npat)]
        m = functools.reduce(jnp.maximum, ls)
        lt = m + jnp.log(sum(jnp.exp(l_ - m) for l_ in ls))
        lt_ref[...] = lt
        mix = sum(jnp.exp(ls[p] - lt) * o_s[p][...] for p in range(npat))
        om_ref[...] = mix
        z = z_ref[...]
        oa_ref[...] = (mix * (z * _sigmoid(z))).astype(BF16)

    out = pl.BlockSpec((SB, E), lambda h, i: (i, h))
    return pl.pallas_call(
        body, name=name, grid=(AH, S // SB),
        in_specs=[spec(0, False), spec(cfg.k0, True), spec(cfg.k0, False), spec(cfg.v0, True), spec(cfg.v0, False),
                  spec(cfg.za0, False), pl.BlockSpec((1, 8, LANES), lambda h, i: (h, 0, 0))],
        out_specs=[out, out, pl.BlockSpec((SB, 1), lambda h, i: (h * (S // SB) + i, 0))],
        out_shape=[jax.ShapeDtypeStruct((S, cfg.AW), BF16), jax.ShapeDtypeStruct((S, cfg.AW), F32),
                   jax.ShapeDtypeStruct((AH * S, 1), F32)],
        scratch_shapes=[pltpu.VMEM((SB, E), F32)] * npat + [pltpu.VMEM((SB, 1), F32)] * npat +
        [pltpu.VMEM((2 * npat, ATTN_BLOCK, ATTN_BLOCK), F32)],
        compiler_params=_cp(("parallel", "arbitrary")),
    )(proj, proj, proj, proj, proj, proj, slopes)


def attn_fused_bwd(proj, do_a, o_mix, ltot, slopes, cfg, name):
    S, E, AH = cfg.S, cfg.E, cfg.AH
    units, SB = _attn_units(cfg)
    nsb = S // SB
    last = nsb - 1
    scale = E ** -0.5

    def spec(off, prev):
        c0 = _blk(off, E)
        if prev:
            return pl.BlockSpec((SB, E), lambda h, i: (jnp.maximum(i - 1, 0), c0 + h))
        return pl.BlockSpec((SB, E), lambda h, i: (jnp.minimum(i, last), c0 + h))

    cur = pl.BlockSpec((SB, E), lambda h, i: (jnp.minimum(i, last), h))
    prev = pl.BlockSpec((SB, E), lambda h, i: (jnp.maximum(i - 1, 0), h))

    def body(q_ref, kp_ref, kc_ref, vp_ref, vc_ref, z_ref, doa_ref, om_ref, lt_ref, sl_ref,
             dq_ref, dk_ref, dv_ref, dz_ref, dmix_s, dl_s, dq_s, dkp_s, dvp_s, dkc_s, dvc_s, bias_s):
        i = pl.program_id(1)

        @pl.when(i == 0)
        def _():
            dkc_s[...] = jnp.zeros_like(dkc_s)
            dvc_s[...] = jnp.zeros_like(dvc_s)
            _set_bias_tiles(bias_s, sl_ref[0, 0:1, :], cfg)

        @pl.when(i < nsb)
        def _():
            z = z_ref[...]
            s = _sigmoid(z)
            doa = doa_ref[...].astype(F32)
            om = om_ref[...]
            dmix = doa * (z * s)
            dmix_s[...] = dmix
            dz_ref[...] = (doa * om * (s * (1.0 + z * (1.0 - s)))).astype(BF16)
            dl_s[...] = jnp.sum(dmix * om, axis=1, keepdims=True)
            dkp_s[...] = dkc_s[...]
            dvp_s[...] = dvc_s[...]
            dkc_s[...] = jnp.zeros_like(dkc_s)
            dvc_s[...] = jnp.zeros_like(dvc_s)
            dq_s[...] = jnp.zeros_like(dq_s)
            for p, d, base, pbase, from_prev in units:
                rows, prows = _attn_rows(base, d), _attn_rows(pbase, d)
                q = q_ref[rows, :].astype(BF16)
                kc = kc_ref[rows, :].astype(BF16)
                kp = (kp_ref if from_prev else kc_ref)[prows, :].astype(BF16)
                vp = (vp_ref if from_prev else vc_ref)[prows, :].astype(BF16)
                do = dmix_s[rows, :].astype(BF16)
                lt = lt_ref[rows, :]
                dlt = dl_s[rows, :]
                kcat = jnp.concatenate([kp, kc], axis=0)
                vcat = jnp.concatenate([vp, vc_ref[rows, :].astype(BF16)], axis=0)
                pr = jnp.exp(_unit_scores(q, kcat, bias_s, p, (i > 0) if from_prev else None, scale) - lt)
                ds = (pr * (_dot(do, vcat, NT) - dlt) * scale).astype(BF16)
                dq_s[rows, :] += _dot(ds, kcat, NN)
                dkcat = _dot(ds, q, TN)
                dvcat = _dot(pr.astype(BF16), do, TN)
                dk_t, dv_t = (dkp_s, dvp_s) if from_prev else (dkc_s, dvc_s)
                dk_t[prows, :] += dkcat[:ATTN_BLOCK, :]
                dv_t[prows, :] += dvcat[:ATTN_BLOCK, :]
                dkc_s[rows, :] += dkcat[ATTN_BLOCK:, :]
                dvc_s[rows, :] += dvcat[ATTN_BLOCK:, :]
            dq_ref[...] = dq_s[...].astype(BF16)
            dk_ref[...] = dkp_s[...].astype(BF16)
            dv_ref[...] = dvp_s[...].astype(BF16)

        @pl.when(i == nsb)
        def _():
            dk_ref[...] = dkc_s[...].astype(BF16)
            dv_ref[...] = dvc_s[...].astype(BF16)

    sh = jax.ShapeDtypeStruct((S, cfg.AW), BF16)
    acc = pltpu.VMEM((SB, E), F32)
    return pl.pallas_call(
        body, name=name, grid=(AH, nsb + 1),
        in_specs=[spec(0, False), spec(cfg.k0, True), spec(cfg.k0, False), spec(cfg.v0, True), spec(cfg.v0, False),
                  spec(cfg.za0, False), cur, cur,
                  pl.BlockSpec((SB, 1), lambda h, i: (h * nsb + jnp.minimum(i, last), 0)),
                  pl.BlockSpec((1, 8, LANES), lambda h, i: (h, 0, 0))],
        out_specs=[cur, prev, prev, cur], out_shape=[sh] * 4,
        scratch_shapes=[acc, pltpu.VMEM((SB, 1), F32), acc, acc, acc, acc, acc,
                        pltpu.VMEM((2 * len(cfg.patterns), ATTN_BLOCK, ATTN_BLOCK), F32)],
        compiler_params=_cp(("parallel", "arbitrary")),
    )(proj, proj, proj, proj, proj, proj, do_a, o_mix, ltot, slopes)


HALO = 8


def _conv_taps(x_ref, h_ref, kc):
    x = x_ref[...]
    full = jnp.concatenate([jnp.where(pl.program_id(1) == 0, 0.0, h_ref[...]), x], axis=0)
    return [pltpu.roll(full, s, axis=0)[HALO:, :] for s in range(kc - 1, 0, -1)] + [x]


def _conv_pre(taps, w_ref, b_ref):
    pre = b_ref[...] + w_ref[0:1, :] * taps[0]
    for k in range(1, len(taps)):
        pre = pre + w_ref[k:k + 1, :] * taps[k]
    return pre


def conv_fwd(proj, w, b, cfg, name, tm=1024, tc=512):
    S, CD, KC = cfg.S, cfg.CD, cfg.KC
    tc = min(tc, CD)
    c0 = _blk(cfg.xbc0, tc)
    hb = tm // HALO

    def body(x_ref, h_ref, w_ref, b_ref, o_ref):
        pre = _conv_pre(_conv_taps(x_ref, h_ref, KC), w_ref, b_ref)
        o_ref[...] = pre * _sigmoid(pre)

    return pl.pallas_call(
        body, name=name, grid=(CD // tc, S // tm),
        in_specs=[pl.BlockSpec((tm, tc), lambda c, i: (i, c0 + c)),
                  pl.BlockSpec((HALO, tc), lambda c, i: (jnp.maximum(i * hb - 1, 0), c0 + c)),
                  pl.BlockSpec((KC, tc), lambda c, i: (0, c)),
                  pl.BlockSpec((1, tc), lambda c, i: (0, c))],
        out_specs=pl.BlockSpec((tm, tc), lambda c, i: (i, c)),
        out_shape=jax.ShapeDtypeStruct((S, CD), F32),
        compiler_params=_cp(("parallel", "arbitrary")),
    )(proj, proj, w, b)


def conv_bwd(proj, dxc, w, b, cfg, name, c_off, tm=1024, tc=512):
    S, KC = cfg.S, cfg.KC
    CD = dxc.shape[1]
    tc = min(tc, CD)
    c0 = _blk(cfg.xbc0 + c_off, tc)
    w0 = _blk(c_off, tc)
    hb = tm // HALO
    nrb = S // tm
    last_h = S // HALO - 1

    def body(x_ref, hp_ref, hn_ref, d_ref, dn_ref, w_ref, b_ref, o_ref, gw_ref, gb_ref):
        i = pl.program_id(1)
        x = x_ref[...]
        full = jnp.concatenate([jnp.where(i == 0, 0.0, hp_ref[...]), x, hn_ref[...]], axis=0)
        rows = tm + HALO
        taps = [pltpu.roll(full, s_, axis=0)[HALO:, :] for s_ in range(KC - 1, 0, -1)] + [full[HALO:, :]]
        pre = _conv_pre(taps, w_ref, b_ref)
        sg = _sigmoid(pre)
        d_ext = jnp.concatenate([d_ref[...], jnp.where(i == nrb - 1, 0.0, dn_ref[...])], axis=0)
        dpre = d_ext * (sg * (1.0 + pre * (1.0 - sg)))
        own = dpre[:tm, :]
        acc = w_ref[KC - 1:KC, :] * own
        for j in range(1, KC):
            acc = acc + w_ref[KC - 1 - j:KC - j, :] * pltpu.roll(dpre, rows - j, axis=0)[:tm, :]
        o_ref[...] = acc.astype(BF16)
        gb = jnp.sum(own, axis=0, keepdims=True)
        gws = [jnp.sum(own * taps[k][:tm, :], axis=0, keepdims=True) for k in range(KC)]
        gw = jnp.concatenate(gws + [jnp.zeros((8 - KC, tc), F32)], axis=0)

        @pl.when(i == 0)
        def _():
            gw_ref[...] = gw
            gb_ref[...] = gb

        @pl.when(i > 0)
        def _():
            gw_ref[...] += gw
            gb_ref[...] += gb

    return pl.pallas_call(
        body, name=name, grid=(CD // tc, nrb),
        in_specs=[pl.BlockSpec((tm, tc), lambda c, i: (i, c0 + c)),
                  pl.BlockSpec((HALO, tc), lambda c, i: (jnp.maximum(i * hb - 1, 0), c0 + c)),
                  pl.BlockSpec((HALO, tc), lambda c, i: (jnp.minimum((i + 1) * hb, last_h), c0 + c)),
                  pl.BlockSpec((tm, tc), lambda c, i: (i, c)),
                  pl.BlockSpec((HALO, tc), lambda c, i: (jnp.minimum((i + 1) * hb, last_h), c)),
                  pl.BlockSpec((KC, tc), lambda c, i: (0, w0 + c)),
                  pl.BlockSpec((1, tc), lambda c, i: (0, w0 + c))],
        out_specs=[pl.BlockSpec((tm, tc), lambda c, i: (i, c)),
                   pl.BlockSpec((8, tc), lambda c, i: (0, c)),
                   pl.BlockSpec((1, tc), lambda c, i: (0, c))],
        out_shape=[jax.ShapeDtypeStruct((S, CD), BF16), jax.ShapeDtypeStruct((8, CD), F32),
                   jax.ShapeDtypeStruct((1, CD), F32)],
        compiler_params=_cp(("parallel", "arbitrary")),
    )(proj, proj, proj, dxc, dxc, w, b)


def _pad_lanes(v, width=LANES):
    return jnp.pad(v, ((0, 0), (0, width - v.shape[1])))


def ssd_prep(hn, w_dt, dt_bias, a_log, cfg, name):
    S, L, D = cfg.S, cfg.L, cfg.D
    R = CPS * L

    def body(h_ref, wd_ref, b_ref, al_ref, raw_ref, dt_ref, ac_ref):
        raw = _dot(h_ref[...], wd_ref[...], NN)
        raw_ref[...] = raw
        x = raw + b_ref[...]
        dt = jnp.maximum(x, 0.0) + jnp.log(1.0 + jnp.exp(-jnp.abs(x)))
        da = dt * (-jnp.exp(al_ref[...]))
        li = lax.broadcasted_iota(jnp.int32, (L, L), 0)
        si = lax.broadcasted_iota(jnp.int32, (L, L), 1)
        tri = jnp.where(li >= si, 1.0, 0.0).astype(F32)
        dt_ref[...] = dt
        for ci in range(CPS):
            rows = slice(ci * L, (ci + 1) * L)
            ac_ref[rows, :] = lax.dot_general(tri, da[rows, :], ((NN), ((), ())), precision=lax.Precision.HIGHEST,
                                              preferred_element_type=F32)

    row = pl.BlockSpec((R, LANES), lambda i: (i, 0))
    vec = pl.BlockSpec((1, LANES), lambda i: (0, 0))
    sh = jax.ShapeDtypeStruct((S, LANES), F32)
    return pl.pallas_call(
        body, name=name, grid=(S // R,),
        in_specs=[pl.BlockSpec((R, D), lambda i: (i, 0)), pl.BlockSpec((D, LANES), lambda i: (0, 0)), vec, vec],
        out_specs=[row, row, row], out_shape=[sh, sh, sh],
        compiler_params=_cp(("parallel",)),
    )(hn, w_dt, dt_bias, a_log)


def _spread(v, n):
    return jnp.broadcast_to(v[:, :, None], v.shape + (n,)).reshape(v.shape[0], v.shape[1] * n)


def _head_selectors(cfg):
    def sel(width):
        head = jnp.arange(LANES)[None, :, None]
        slot = jnp.arange(cfg.SG)[:, None, None] * cfg.HPG + (jnp.arange(cfg.HPG * width) // width)[None, None, :]
        return (head == slot).astype(BF16)
    return sel(cfg.P), sel(LANES)


def _spread_heads(v, sel):
    n = v.shape[0]
    hi = v.astype(BF16)
    r1 = v - hi.astype(F32)
    mid = r1.astype(BF16)
    lo = (r1 - mid.astype(F32)).astype(BF16)
    out = _dot(jnp.concatenate([hi, mid, lo], axis=0), sel, NN)
    return out[:n] + out[n:2 * n] + out[2 * n:]


def _pair_lanes(wide, hpg, p):
    low = lax.broadcasted_iota(jnp.int32, (wide.shape[0], LANES), 1) < p
    return jnp.concatenate([jnp.where(low, wide[:, 2 * jp * LANES:(2 * jp + 1) * LANES],
                                      wide[:, (2 * jp + 1) * LANES:(2 * jp + 2) * LANES])
                            for jp in range(hpg // 2)], axis=1)


def _pair_select(halves, p):
    low = lax.broadcasted_iota(jnp.int32, halves[0].shape, 1) < p
    return jnp.where(low, halves[0], halves[1])


def _head_rows(row, hpg, p):
    return jnp.concatenate([jnp.broadcast_to(row[:, j * LANES:(j + 1) * LANES], (p, LANES)) for j in range(hpg)],
                           axis=0)


def _segment_sums(t, sel):
    r = t.shape[0]
    hi = t.astype(BF16)
    out = _dot(jnp.concatenate([hi, (t - hi.astype(F32)).astype(BF16)], axis=0), sel, NT)
    return out[:r] + out[r:]


def ssd_scan_fwd(xc, dt, acum, act, sel_p, sel_l, cfg, name):
    S, L, P, SN, HPG, SG, SI = cfg.S, cfg.L, cfg.P, cfg.SN, cfg.HPG, cfg.SG, cfg.SI
    nc = S // L
    GW = HPG * P
    bcol, ccol = _blk(SI, SN), _blk(SI + cfg.GN, SN)
    assert nc % CPS == 0

    def body(xs_ref, b_ref, c_ref, dtn_ref, acn_ref, at_ref, sp_ref, sl_ref, y_ref, st_ref, st):
        @pl.when(pl.program_id(1) == 0)
        def _():
            st[...] = jnp.zeros_like(st)

        causal = lax.broadcasted_iota(jnp.int32, (L, L), 0) >= lax.broadcasted_iota(jnp.int32, (L, L), 1)
        for ci in range(CPS):
            rows = slice(ci * L, (ci + 1) * L)
            acn = acn_ref[rows, :]
            dts = _spread_heads(dtn_ref[rows, :], sp_ref[0])
            acs = _spread_heads(acn, sl_ref[0])
            a_p = _pair_lanes(acs, HPG, P)
            s0 = st[...]
            st_ref[ci] = s0.reshape(HPG, P, SN)
            B = b_ref[rows, :].astype(BF16)
            C = c_ref[rows, :].astype(BF16)
            G = _dot(C, B, NT)
            xdt = xs_ref[rows, :] * dts
            xdtb = xdt.astype(BF16)
            ws = jnp.exp(a_p[L - 1:L, :] - a_p)
            yo = jnp.exp(a_p) * _dot(C, s0.astype(BF16), NT)
            yd = []
            for jp in range(HPG // 2):
                x_pair = xdtb[:, jp * LANES:(jp + 1) * LANES]
                halves = []
                for j in (2 * jp, 2 * jp + 1):
                    dm = jnp.where(causal, jnp.exp(acs[:, j * LANES:(j + 1) * LANES] - at_ref[j:j + 1, rows]), 0.0)
                    halves.append(_dot((G * dm).astype(BF16), x_pair, NN))
                yd.append(_pair_select(halves, P))
            y_ref[rows, :] = jnp.concatenate(yd, axis=1) + yo
            st[...] = _head_rows(jnp.exp(acs[L - 1:L, :]), HPG, P) * s0 + _dot((xdt * ws).astype(BF16), B, TN)

    R = CPS * L
    y, states = pl.pallas_call(
        body, name=name, grid=(SG, nc // CPS),
        in_specs=[pl.BlockSpec((R, GW), lambda g, c: (c, g)),
                  pl.BlockSpec((R, SN), lambda g, c: (c, bcol + g)),
                  pl.BlockSpec((R, SN), lambda g, c: (c, ccol + g)),
                  pl.BlockSpec((R, LANES), lambda g, c: (c, 0)),
                  pl.BlockSpec((R, LANES), lambda g, c: (c, 0)),
                  pl.BlockSpec((HPG, R), lambda g, c: (g, c)),
                  pl.BlockSpec((1, LANES, GW), lambda g, c: (g, 0, 0)),
                  pl.BlockSpec((1, LANES, HPG * LANES), lambda g, c: (g, 0, 0))],
        out_specs=[pl.BlockSpec((R, GW), lambda g, c: (c, g)),
                   pl.BlockSpec((CPS, HPG, P, SN), lambda g, c: (c, g, 0, 0))],
        out_shape=[jax.ShapeDtypeStruct((S, SI), F32), jax.ShapeDtypeStruct((nc, cfg.SH, P, SN), F32)],
        scratch_shapes=[pltpu.VMEM((GW, SN), F32)],
        compiler_params=_cp(("parallel", "arbitrary")),
    )(xc, xc, xc, dt, acum, act, sel_p, sel_l)
    return y, states


def ssd_scan_bwd(xc, dt, acum, act, sel_p, sel_l, states, y, dy, dvec, cfg, name, side):
    S, L, P, SN, HPG, SG, SI = cfg.S, cfg.L, cfg.P, cfg.SN, cfg.HPG, cfg.SG, cfg.SI
    nc = S // L
    GW = HPG * P
    bcol, ccol = _blk(SI, SN), _blk(SI + cfg.GN, SN)

    def rc(c):
        return nc // CPS - 1 - c

    def body(xs_ref, b_ref, c_ref, dtn_ref, acn_ref, at_ref, sp_ref, sl_ref, st_ref, y_ref, dy_ref, dk_ref,
             dxs_ref, db_ref, dc_ref, dac_ref, dxsum_ref, dst):
        @pl.when(pl.program_id(1) == 0)
        def _():
            dst[...] = jnp.zeros_like(dst)

        sel = sp_ref[0]
        causal = lax.broadcasted_iota(jnp.int32, (L, L), 0) >= lax.broadcasted_iota(jnp.int32, (L, L), 1)
        low = lax.broadcasted_iota(jnp.int32, (L, LANES), 1) < P
        is_last = lax.broadcasted_iota(jnp.int32, (L, LANES), 0) == L - 1
        ones = jnp.ones((16, SN), BF16)
        for ci in reversed(range(CPS)):
            rows = slice(ci * L, (ci + 1) * L)
            acn = acn_ref[rows, :]
            dts = _spread_heads(dtn_ref[rows, :], sel)
            acs = _spread_heads(acn, sl_ref[0])
            a_p = _pair_lanes(acs, HPG, P)
            B = b_ref[rows, :].astype(BF16)
            C = c_ref[rows, :].astype(BF16)
            G = _dot(C, B, NT)
            xs = xs_ref[rows, :]
            dY = dy_ref[rows, :].astype(F32)
            xdt = xs * dts
            xdtb = xdt.astype(BF16)
            dYb = dY.astype(BF16)
            s0 = st_ref[ci].reshape(GW, SN)
            s0b = s0.astype(BF16)
            ds1 = dst[...]
            ds1b = ds1.astype(BF16)
            ws = jnp.exp(a_p[L - 1:L, :] - a_p)
            dR = (jnp.exp(a_p) * dY).astype(BF16)
            dX2 = ws * _dot(B, ds1b, NT)
            dgsum = jnp.zeros((L, L), F32)
            dX1, yd = [], []
            for jp in range(HPG // 2):
                lanes = slice(jp * LANES, (jp + 1) * LANES)
                x_pair, dy_pair = xdtb[:, lanes], dYb[:, lanes]
                h1, h2 = [], []
                for h, j in enumerate((2 * jp, 2 * jp + 1)):
                    dm = jnp.where(causal, jnp.exp(acs[:, j * LANES:(j + 1) * LANES] - at_ref[j:j + 1, rows]), 0.0)
                    mine = low if h == 0 else jnp.logical_not(low)
                    dgsum = dgsum + _dot(jnp.where(mine, dy_pair, jnp.zeros_like(dy_pair)), x_pair, NT) * dm
                    Mb = (G * dm).astype(BF16)
                    h1.append(_dot(Mb, dy_pair, TN))
                    h2.append(_dot(Mb, x_pair, NN))
                dX1.append(_pair_select(h1, P))
                yd.append(_pair_select(h2, P))
            dX1 = jnp.concatenate(dX1, axis=1)
            dX = dX1 + dX2
            pair = (dYb.astype(F32) - dY) * jnp.concatenate(yd, axis=1) - xdtb.astype(F32) * dX1
            through = _segment_sums(xdt * dX2, sel)
            u = ds1 * s0
            u_hi = u.astype(BF16)
            u_rows = _dot(ones, jnp.concatenate([u_hi, (u - u_hi.astype(F32)).astype(BF16)], axis=0), NT)
            u_rows = u_rows[:, :GW] + u_rows[:, GW:]
            at_end = jnp.exp(acn[L - 1:L, :]) * _segment_sums(u_rows, sel)[0:1, :] + \
                jnp.sum(through, axis=0, keepdims=True)
            dac_ref[0, rows, :] = _segment_sums(dY * y_ref[rows, :] + pair, sel) - through + \
                jnp.where(is_last, at_end, 0.0)
            dxsum_ref[0, rows, :] = _segment_sums(dX * xs, sel)
            dxs_ref[rows, :] = dX * dts + dk_ref[...] * dY
            dst[...] = _head_rows(jnp.exp(acs[L - 1:L, :]), HPG, P) * ds1 + _dot(dR, C, TN)
            dgb = dgsum.astype(BF16)
            dc_ref[rows, :] = _dot(dR, s0b, NN) + _dot(dgb, B, NN)
            db_ref[rows, :] = _dot((xdt * ws).astype(BF16), ds1b, NN) + _dot(dgb, C, TN)

    R = CPS * L
    wide = pl.BlockSpec((R, GW), lambda g, c: (rc(c), g))
    colspec = pl.BlockSpec((1, R, LANES), lambda g, c: (g, rc(c), 0))
    whole = pl.BlockSpec(memory_space=pl.ANY)
    arrs, gathers = side
    grid = (SG, nc // CPS)
    res = pl.pallas_call(
        with_exchange(body, 12, 5, gathers, grid), name=name, grid=grid,
        in_specs=[wide,
                  pl.BlockSpec((R, SN), lambda g, c: (rc(c), bcol + g)),
                  pl.BlockSpec((R, SN), lambda g, c: (rc(c), ccol + g)),
                  pl.BlockSpec((R, LANES), lambda g, c: (rc(c), 0)),
                  pl.BlockSpec((R, LANES), lambda g, c: (rc(c), 0)),
                  pl.BlockSpec((HPG, R), lambda g, c: (g, rc(c))),
                  pl.BlockSpec((1, LANES, GW), lambda g, c: (g, 0, 0)),
                  pl.BlockSpec((1, LANES, HPG * LANES), lambda g, c: (g, 0, 0)),
                  pl.BlockSpec((CPS, HPG, P, SN), lambda g, c: (rc(c), g, 0, 0)),
                  wide, wide,
                  pl.BlockSpec((1, GW), lambda g, c: (0, g))] + [whole] * len(arrs),
        out_specs=[wide,
                   pl.BlockSpec((R, SN), lambda g, c: (rc(c), g)),
                   pl.BlockSpec((R, SN), lambda g, c: (rc(c), g)),
                   colspec, colspec] + [whole] * len(arrs),
        out_shape=[jax.ShapeDtypeStruct((S, SI), F32), jax.ShapeDtypeStruct((S, cfg.GN), F32),
                   jax.ShapeDtypeStruct((S, cfg.GN), F32),
                   jax.ShapeDtypeStruct((SG, S, LANES), F32), jax.ShapeDtypeStruct((SG, S, LANES), F32)] +
        _exchange_shapes(arrs, gathers),
        scratch_shapes=[pltpu.VMEM((GW, SN), F32)] + _exchange_sems(len(arrs)),
        compiler_params=_cp(("arbitrary", "arbitrary")),
    )(xc, xc, xc, dt, acum, act, sel_p, sel_l, states, y, dy, dvec, *arrs)
    return res[:5], res[5:]


def dt_bwd(dac, dxsum, dt_raw, dt, dt_bias, a_log, cfg, name):
    S, L, SG = cfg.S, cfg.L, cfg.SG
    R = CPS * L

    def body(da_ref, dx_ref, x_ref, dt_ref, b_ref, al_ref, o_ref, gb_ref, ga_ref):
        a = -jnp.exp(al_ref[...])
        dtv = dt_ref[...]
        dxs = jnp.sum(dx_ref[...], axis=0)
        upper = jnp.where(lax.broadcasted_iota(jnp.int32, (L, L), 1) >= lax.broadcasted_iota(jnp.int32, (L, L), 0),
                          1.0, 0.0).astype(F32)
        dac = jnp.sum(da_ref[...], axis=0)
        dda = jnp.concatenate([lax.dot_general(upper, dac[ci * L:(ci + 1) * L, :], (NN, ((), ())),
                                               precision=lax.Precision.HIGHEST, preferred_element_type=F32)
                               for ci in range(CPS)], axis=0)
        draw = (dxs + dda * a) * _sigmoid(x_ref[...] + b_ref[...])
        o_ref[...] = draw.astype(BF16)
        gb = jnp.sum(draw, axis=0, keepdims=True)
        ga = jnp.sum(dda * dtv, axis=0, keepdims=True) * a

        @pl.when(pl.program_id(0) == 0)
        def _():
            gb_ref[...] = gb
            ga_ref[...] = ga

        @pl.when(pl.program_id(0) > 0)
        def _():
            gb_ref[...] += gb
            ga_ref[...] += ga

    row = pl.BlockSpec((R, LANES), lambda i: (i, 0))
    vec = pl.BlockSpec((1, LANES), lambda i: (0, 0))
    return pl.pallas_call(
        body, name=name, grid=(S // R,),
        in_specs=[pl.BlockSpec((SG, R, LANES), lambda i: (0, i, 0))] * 2 + [row, row, vec, vec],
        out_specs=[row, vec, vec],
        out_shape=[jax.ShapeDtypeStruct((S, LANES), BF16), jax.ShapeDtypeStruct((1, LANES), F32),
                   jax.ShapeDtypeStruct((1, LANES), F32)],
        compiler_params=_cp(("arbitrary",)),
    )(dac, dxsum, dt_raw, dt, dt_bias, a_log)


def gated_norm_fwd(y, xc, proj, dvec, nw, cfg, name, tm=256):
    S, SI = cfg.S, cfg.SI

    def body(y_ref, xs_ref, z_ref, d_ref, w_ref, o_ref):
        z = z_ref[...]
        yg = (y_ref[...] + d_ref[...] * xs_ref[...]) * (z * _sigmoid(z))
        r = lax.rsqrt(jnp.mean(yg * yg, axis=-1, keepdims=True) + RMS_EPS)
        o_ref[...] = ((yg * r) * w_ref[...]).astype(BF16)

    row = pl.BlockSpec((tm, SI), lambda i: (i, 0))
    vec = pl.BlockSpec((1, SI), lambda i: (0, 0))
    return pl.pallas_call(
        body, name=name, grid=(S // tm,),
        in_specs=[row, row, pl.BlockSpec((tm, SI), lambda i: (i, _blk(cfg.zs0, SI))), vec, vec],
        out_specs=row, out_shape=jax.ShapeDtypeStruct((S, SI), BF16),
        compiler_params=_cp(("parallel",)),
    )(y, xc, proj, dvec, nw)


def gated_norm_bwd(dyn, y, xc, proj, dvec, nw, cfg, name, tm=128):
    S, SI = cfg.S, cfg.SI

    def body(dn_ref, y_ref, xs_ref, z_ref, d_ref, w_ref, dy_ref, dz_ref, gw_ref, gd_ref):
        z = z_ref[...]
        s = _sigmoid(z)
        sz = z * s
        xs = xs_ref[...]
        yf = y_ref[...] + d_ref[...] * xs
        yg = yf * sz
        r = lax.rsqrt(jnp.mean(yg * yg, axis=-1, keepdims=True) + RMS_EPS)
        dn = dn_ref[...].astype(F32)
        g = dn * w_ref[...]
        dyg = r * g - yg * (r * r * r) * jnp.mean(g * yg, axis=-1, keepdims=True)
        dy = dyg * sz
        dy_ref[...] = dy.astype(BF16)
        dz_ref[...] = (dyg * yf * (s * (1.0 + z * (1.0 - s)))).astype(BF16)
        gw = jnp.sum(dn * (yg * r), axis=0, keepdims=True)
        gd = jnp.sum(dy * xs, axis=0, keepdims=True)

        @pl.when(pl.program_id(0) == 0)
        def _():
            gw_ref[...] = gw
            gd_ref[...] = gd

        @pl.when(pl.program_id(0) > 0)
        def _():
            gw_ref[...] += gw
            gd_ref[...] += gd

    row = pl.BlockSpec((tm, SI), lambda i: (i, 0))
    vec = pl.BlockSpec((1, SI), lambda i: (0, 0))
    return pl.pallas_call(
        body, name=name, grid=(S // tm,),
        in_specs=[row, row, row, pl.BlockSpec((tm, SI), lambda i: (i, _blk(cfg.zs0, SI))), vec, vec],
        out_specs=[row, row, vec, vec],
        out_shape=[jax.ShapeDtypeStruct((S, SI), BF16), jax.ShapeDtypeStruct((S, SI), BF16),
                   jax.ShapeDtypeStruct((1, SI), F32), jax.ShapeDtypeStruct((1, SI), F32)],
        compiler_params=_cp(("arbitrary",)),
    )(dyn, y, xc, proj, dvec, nw)


def _shard_columns(cfg, main, dt):
    dt0 = 4 * cfg.AW + cfg.SI + cfg.CD
    ws = cfg.N_IN // N_DEV
    out = []
    for k in range(N_DEV):
        lo, hi, parts = k * ws, (k + 1) * ws, []
        if lo < dt0:
            parts.append((main, lo, min(hi, dt0)))
        if lo < dt0 + cfg.SH and hi > dt0:
            parts.append((dt, max(lo, dt0) - dt0, min(hi, dt0 + cfg.SH) - dt0))
        if hi > dt0 + cfg.SH:
            parts.append((main, max(lo, dt0 + cfg.SH) - cfg.SH, hi - cfg.SH))
        out.append(parts)
    return out


def local_step(cfg, x, tgt, norm_w, conv_w, conv_b, dt_bias, a_log, d_skip, ssm_norm_w, final_norm_w,
               w_main, w_dt, shards, dt0):
    S, D = cfg.S, cfg.D
    slopes = _slope_table(cfg)
    dt_bias_p = _pad_lanes(dt_bias)
    a_log_p = _pad_lanes(a_log)
    dvec = _spread(d_skip, cfg.P)

    hn = rmsnorm_fwd(x, norm_w, "rmsnorm_fwd")
    proj, gathered = matmul(hn, w_main, 'nn', 1024, 2048, 2048, F32, "in_proj", side=(shards, [True] * 3))
    w_attn, w_ssm, w_out = gathered[0].reshape(cfg.AW, D), gathered[1].reshape(cfg.SI, D), gathered[2].reshape(D, D)
    o_a, o_mix, ltot = attn_fused_fwd(proj, slopes, cfg, "attn_fwd")
    xc = conv_fwd(proj, conv_w, conv_b, cfg, "conv_fwd")
    dt_raw, dt, acum = ssd_prep(hn, w_dt, dt_bias_p, a_log_p, cfg, "ssd_prep")
    act = acum[:, :cfg.SH].T
    sel_p, sel_l = _head_selectors(cfg)
    y, states = ssd_scan_fwd(xc, dt, acum, act, sel_p, sel_l, cfg, "ssd_scan_fwd")
    y_n = gated_norm_fwd(y, xc, proj, dvec, ssm_norm_w, cfg, "gated_norm_fwd")
    a_out, s_out, merged = branch_merge(o_a, w_attn, y_n, w_ssm, proj, cfg, "branch_merge")
    dout, loss_p, g_final_w = out_proj_final(merged, w_out, x, final_norm_w.reshape(1, D), tgt, "out_proj_final")

    g_w_out = matmul(merged, dout, 'tn', 1024, 1024, 2048, BF16, "g_w_out")
    da_out, ds_out, dga, dgs = merge_bwd(dout, w_out, a_out, s_out, proj, cfg, "merge_bwd")
    g_w_attn = matmul(o_a, da_out, 'tn', 1024, 1024, 2048, BF16, "g_w_attn")
    g_w_ssm = matmul(y_n, ds_out, 'tn', 1024, 1024, 2048, BF16, "g_w_ssm")
    do_a = matmul(da_out, w_attn, 'nt', 512, 1024, 2048, BF16, "d_o_a")
    dyn = matmul(ds_out, w_ssm, 'nt', 512, 1024, 2048, BF16, "d_y_n")
    dy, dz_s, g_ssm_norm, g_dvec = gated_norm_bwd(dyn, y, xc, proj, dvec, ssm_norm_w, cfg, "gated_norm_bwd")
    sends = [g.reshape((N_DEV, g.shape[0] // N_DEV, D)) for g in (g_w_attn, g_w_ssm, g_w_out)]
    (dxs, dB, dC, dac_g, dxsum_g), (r_attn, r_ssm, r_out) = ssd_scan_bwd(
        xc, dt, acum, act, sel_p, sel_l, states, y, dy, dvec, cfg, "ssd_scan_bwd", side=(sends, [False] * 3))
    ddt_raw, g_dt_bias, g_a_log = dt_bwd(dac_g, dxsum_g, dt_raw, dt, dt_bias_p, a_log_p, cfg, "dt_bwd")
    dxbc, g_cw, g_cb = [], [], []
    for nm, piece, c_off in (("xs", dxs, 0), ("b", dB, cfg.SI), ("c", dC, cfg.SI + cfg.GN)):
        dx, gw, gb = conv_bwd(proj, piece, conv_w, conv_b, cfg, "conv_bwd_" + nm, c_off)
        dxbc.append(dx)
        g_cw.append(gw)
        g_cb.append(gb)
    g_conv_w, g_conv_b = jnp.concatenate(g_cw, axis=1), jnp.concatenate(g_cb, axis=1)
    dq, dk, dv, dz_a = attn_fused_bwd(proj, do_a, o_mix, ltot, slopes, cfg, "attn_bwd")
    dproj = jnp.concatenate([dq, dk, dv, dz_a, dz_s] + dxbc + [dga, dgs], axis=1)
    def slabs(g_main, g_dt):
        return jnp.stack([jnp.concatenate([g[:, lo:hi] for g, lo, hi in parts], axis=1)
                          for parts in _shard_columns(cfg, g_main, g_dt)])

    cut = D // 4
    g_w_dt = matmul(hn, ddt_raw, 'tn', 1024, 128, 2048, BF16, "g_w_dt")
    g_w_top = matmul(hn, dproj, 'tn', 512, 2048, 2048, BF16, "g_w_main_top", a_cols=(0, cut))
    g_w_bot, (r_top,) = matmul(hn, dproj, 'tn', 512, 2048, 2048, BF16, "g_w_main_rest", a_cols=(cut, D - cut),
                               side=([slabs(g_w_top, g_w_dt[:cut])], [False]))
    dhn_a, (r_bot,) = matmul(dproj, w_main, 'nt', 1024, 1024, 2048, F32, "d_hn",
                             side=([slabs(g_w_bot, g_w_dt[cut:])], [False]))
    grad_x, g_norm_w = rmsnorm_bwd(dhn_a, ddt_raw, w_dt, x, norm_w, dout, "rmsnorm_bwd")

    g_d_skip = jnp.sum(g_dvec.reshape(cfg.SH, cfg.P), axis=1).reshape(1, cfg.SH)
    small = dict(norm_w=g_norm_w, conv_b=g_conv_b, dt_bias=g_dt_bias[:, :cfg.SH], a_log=g_a_log[:, :cfg.SH],
                 d_skip=g_d_skip, ssm_norm_w=g_ssm_norm, final_norm_w=g_final_w, conv_w=g_conv_w[:cfg.KC])
    return loss_p, grad_x, small, dict(w_in=[r_top, r_bot], w_attn=[r_attn], w_ssm=[r_ssm], w_out=[r_out])


def _mesh_pos():
    return lax.axis_index("x"), lax.axis_index("y"), lax.axis_index("c")


def _flat(pos):
    return 4 * pos[0] + 2 * pos[1] + pos[2]


def _exchange_shapes(arrs, gathers):
    return [jax.ShapeDtypeStruct(((N_DEV,) + a.shape) if g else a.shape, a.dtype) for a, g in zip(arrs, gathers)]


def _exchange_sems(n):
    return [pltpu.SemaphoreType.DMA((n * (N_DEV - 1),)), pltpu.SemaphoreType.DMA((n * (N_DEV - 1),)),
            pltpu.SemaphoreType.DMA((n,))]


def _exchange_copies(ins, outs, gathers, send_sems, recv_sems, loc_sems):
    pos = _mesh_pos()
    me = _flat(pos)
    starts, waits = [], []
    for a in range(len(ins)):
        mine = ins[a] if gathers[a] else ins[a].at[me]
        loc = pltpu.make_async_copy(mine, outs[a].at[me], loc_sems.at[a])
        starts.append(loc)
        waits.append(loc)
        for k in range(1, N_DEV):
            flip = ((k >> 2) & 1, (k >> 1) & 1, k & 1)
            peer = tuple(1 - p if f else p for p, f in zip(pos, flip))
            pk = _flat(peer)
            src = ins[a] if gathers[a] else ins[a].at[pk]
            sems = dict(send_sem=send_sems.at[a * (N_DEV - 1) + k - 1], recv_sem=recv_sems.at[a * (N_DEV - 1) + k - 1],
                        device_id=peer, device_id_type=pl.DeviceIdType.MESH)
            starts.append(pltpu.make_async_remote_copy(src_ref=src, dst_ref=outs[a].at[me], **sems))
            waits.append(pltpu.make_async_remote_copy(src_ref=src, dst_ref=outs[a].at[pk], **sems))
    return starts, waits


def exchange(arrs, gathers, name):
    n = len(arrs)

    def body(*refs):
        starts, waits = _exchange_copies(refs[:n], refs[n:2 * n], gathers, *refs[2 * n:])
        for cp in starts:
            cp.start()
        for cp in waits:
            cp.wait()

    hbm = pl.BlockSpec(memory_space=pltpu.HBM)
    return pl.pallas_call(
        body, name=name, in_specs=[hbm] * n, out_specs=[hbm] * n, out_shape=_exchange_shapes(arrs, gathers),
        scratch_shapes=_exchange_sems(n),
    )(*arrs)


def gather_two_level(arrs, chunks, name):
    n = len(arrs)

    def body(*refs):
        ins, outs = refs[:n], refs[n:2 * n]
        send_sems, recv_sems, loc_sems = refs[2 * n:]
        x, y, c = _mesh_pos()
        me, sib = (x, y, c), (x, y, 1 - c)
        chips = [(1 - x, y), (x, 1 - y), (1 - x, 1 - y)]
        plan, base = [], 0
        for a in range(n):
            step = arrs[a].shape[0] // chunks[a]
            for q in range(chunks[a]):
                plan.append((a, pl.ds(q * step, step), base))
                base += N_DEV - 1

        def copy(a, rows, sem, block, to, own=False):
            dst = outs[a].at[_flat(block), rows]
            return pltpu.make_async_remote_copy(
                src_ref=ins[a].at[rows] if own else dst, dst_ref=dst, send_sem=send_sems.at[sem],
                recv_sem=recv_sems.at[sem], device_id=to, device_id_type=pl.DeviceIdType.MESH)

        local = [pltpu.make_async_copy(ins[a], outs[a].at[_flat(me)], loc_sems.at[a]) for a in range(n)]
        for cp in local:
            cp.start()
        sent = []
        for a, rows, s in plan:
            sent.append(copy(a, rows, s, me, sib, own=True))
            sent += [copy(a, rows, s + 1 + j, me, (*chip, c), own=True) for j, chip in enumerate(chips)]
        for cp in sent:
            cp.start()
        for a, rows, s in plan:
            for j, chip in enumerate(chips):
                copy(a, rows, s + 1 + j, (*chip, c), me).wait_recv()
                passed = copy(a, rows, s + 4 + j, (*chip, c), sib)
                passed.start()
                sent.append(passed)
        for a, rows, s in plan:
            copy(a, rows, s, sib, me).wait_recv()
            for j, chip in enumerate(chips):
                copy(a, rows, s + 4 + j, (*chip, 1 - c), me).wait_recv()
        for cp in sent:
            cp.wait_send()
        for cp in local:
            cp.wait()

    hbm = pl.BlockSpec(memory_space=pltpu.HBM)
    nsem = (N_DEV - 1) * sum(chunks)
    return pl.pallas_call(
        body, name=name, in_specs=[hbm] * n, out_specs=[hbm] * n, out_shape=_exchange_shapes(arrs, [True] * n),
        scratch_shapes=[pltpu.SemaphoreType.DMA((nsem,)), pltpu.SemaphoreType.DMA((nsem,)),
                        pltpu.SemaphoreType.DMA((n,))],
    )(*arrs)


def with_exchange(body, n_in, n_out, gathers, grid):
    n = len(gathers)

    def wrapped(*refs):
        ins, sends = refs[:n_in], refs[n_in:n_in + n]
        outs, recvs = refs[n_in + n:n_in + n + n_out], refs[n_in + 2 * n + n_out - n:n_in + 2 * n + n_out]
        scratch, sems = refs[n_in + 2 * n + n_out:-3], refs[-3:]
        ids = [pl.program_id(d) for d in range(len(grid))]
        first = functools.reduce(jnp.logical_and, [i == 0 for i in ids])
        last = functools.reduce(jnp.logical_and, [i == g - 1 for i, g in zip(ids, grid)])

        @pl.when(first)
        def _():
            for cp in _exchange_copies(sends, recvs, gathers, *sems)[0]:
                cp.start()

        body(*ins, *outs, *scratch)

        @pl.when(last)
        def _():
            for cp in _exchange_copies(sends, recvs, gathers, *sems)[1]:
                cp.wait()

    return wrapped


def adamw(g_src, w, m, v, summed, name, tr=64):
    R, C = w.shape
    tr = min(tr, R)
    assert R % tr == 0
    parts = g_src if summed else [g_src]
    starts = [sum(p.shape[1] for p in parts[:k]) // tr for k in range(len(parts))] if summed else [0]
    counts = [p.shape[1] // tr for p in parts] if summed else [R // tr]

    def body(*refs):
        g_refs, (w_ref, m_ref, v_ref, g_out, d_out, m_out, v_out) = refs[:len(parts)], refs[len(parts):]
        if summed:
            g = None
            for k, g_ref in enumerate(g_refs):
                gk = g_ref[0].astype(F32)
                for j in range(1, N_DEV):
                    gk = gk + g_ref[j].astype(F32)
                g = gk if g is None else jnp.where(pl.program_id(0) >= starts[k], gk, g)
        else:
            g = g_refs[0][...]
        mn = ADAM_B1 * m_ref[...] + (1.0 - ADAM_B1) * g
        vn = ADAM_B2 * v_ref[...] + (1.0 - ADAM_B2) * (g * g)
        m_hat = mn / (1.0 - ADAM_B1 ** ADAM_STEP)
        v_hat = vn / (1.0 - ADAM_B2 ** ADAM_STEP)
        g_out[...] = g
        d_out[...] = -ADAM_LR * (m_hat / (jnp.sqrt(v_hat) + ADAM_EPS) + ADAM_WD * w_ref[...])
        m_out[...] = mn
        v_out[...] = vn

    row = pl.BlockSpec((tr, C), lambda i: (i, 0))
    if summed:
        gspecs = [pl.BlockSpec((N_DEV, tr, C), lambda i, s=st, n=nb: (0, jnp.clip(i - s, 0, n - 1), 0))
                  for st, nb in zip(starts, counts)]
    else:
        gspecs = [row]
    sh = jax.ShapeDtypeStruct((R, C), F32)
    return pl.pallas_call(
        body, name=name, grid=(R // tr,), in_specs=gspecs + [row, row, row], out_specs=[row] * 4, out_shape=[sh] * 4,
        compiler_params=_cp(("parallel",)),
    )(*parts, w, m, v)


SMALL = ('norm_w', 'conv_b', 'dt_bias', 'a_log', 'd_skip', 'ssm_norm_w', 'final_norm_w')


def _rows(n):
    return -(-n // (8 * LANES)) * 8


def _pack(vals):
    parts = []
    for a in vals:
        f = a.reshape(-1)
        parts.append(jnp.pad(f, (0, _rows(f.size) * LANES - f.size)).reshape(-1, LANES))
    return jnp.concatenate(parts, axis=0)


def _unpack(packed, shapes):
    out, r = [], 0
    for s in shapes:
        n = math.prod(s)
        out.append(packed[r:r + _rows(n)].reshape(-1)[:n].reshape(s))
        r += _rows(n)
    return out


def kernel(x, norm_w, w_in, conv_w, conv_b, dt_bias, a_log, d_skip, ssm_norm_w, w_attn_branch, w_ssm_branch, w_out, final_norm_w, loss_target, m_norm_w, m_w_in, m_conv_w, m_conv_b, m_dt_bias, m_a_log, m_d_skip, m_ssm_norm_w, m_w_attn_branch, m_w_ssm_branch, m_w_out, m_final_norm_w, v_norm_w, v_w_in, v_conv_w, v_conv_b, v_dt_bias, v_a_log, v_d_skip, v_ssm_norm_w, v_w_attn_branch, v_w_ssm_branch, v_w_out, v_final_norm_w):
    cfg = CFG
    D, SH = cfg.D, cfg.SH
    me = _flat(_mesh_pos())
    dt0 = 4 * cfg.AW + cfg.SI + cfg.CD
    ws = w_in.shape[-1]

    g_in, g_cw = gather_two_level([w_in[0].astype(BF16), conv_w[0]], [4, 1], "gather_w_in")
    main_cols, dt_cols = [], []
    for k, parts in enumerate(_shard_columns(cfg, "main", "dt")):
        at = 0
        for which, lo, hi in parts:
            (main_cols if which == "main" else dt_cols).append(g_in[k][:, at:at + hi - lo])
            at += hi - lo
    w_main = jnp.concatenate(main_cols, axis=1)
    w_dt = _pad_lanes(jnp.concatenate(dt_cols, axis=1))
    conv_full = g_cw.transpose(1, 0, 2).reshape(cfg.KC, cfg.CD)
    shards = [w_attn_branch[0].astype(BF16), w_ssm_branch[0].astype(BF16), w_out[0].astype(BF16)]

    loss_p, grad_x, small, recv = local_step(
        cfg, x[0], loss_target[0], norm_w, conv_full, conv_b, dt_bias, a_log, d_skip,
        ssm_norm_w, final_norm_w, w_main, w_dt, shards, dt0)

    upd = {}
    upd['w_in'] = adamw(recv['w_in'], w_in[0], m_w_in[0], v_w_in[0], True, "adamw_w_in")
    upd['w_attn_branch'] = adamw(recv['w_attn'], w_attn_branch[0], m_w_attn_branch[0], v_w_attn_branch[0], True,
                                 "adamw_w_attn")
    upd['w_ssm_branch'] = adamw(recv['w_ssm'], w_ssm_branch[0], m_w_ssm_branch[0], v_w_ssm_branch[0], True,
                                "adamw_w_ssm")
    upd['w_out'] = adamw(recv['w_out'], w_out[0], m_w_out[0], v_w_out[0], True, "adamw_w_out")

    extra = [jnp.zeros((cfg.KC, cfg.CD), F32), jnp.zeros((1, 1), F32)]
    shapes = [small[n].shape for n in SMALL] + [e.shape for e in extra]
    part = _pack([small[n] for n in SMALL] + [small['conv_w'], loss_p[:, :1]])
    gathered, = exchange([part], [True], "gather_small")
    given = dict(norm_w=(norm_w, m_norm_w, v_norm_w), conv_b=(conv_b, m_conv_b, v_conv_b),
                 dt_bias=(dt_bias, m_dt_bias, v_dt_bias), a_log=(a_log, m_a_log, v_a_log),
                 d_skip=(d_skip, m_d_skip, v_d_skip), ssm_norm_w=(ssm_norm_w, m_ssm_norm_w, v_ssm_norm_w),
                 final_norm_w=(final_norm_w, m_final_norm_w, v_final_norm_w))
    packed = [_pack([given[n][t] for n in SMALL] + extra) for t in range(3)]
    outs = adamw([gathered], *packed, True, "adamw_small", tr=part.shape[0])
    unpacked = [_unpack(o, shapes) for o in outs]
    for i, n in enumerate(SMALL):
        upd[n] = [u[i].reshape(given[n][0].shape) for u in unpacked]
    loss = unpacked[0][-1].reshape(())
    cw = conv_w.shape[-1]
    g_cw_mine = lax.dynamic_slice_in_dim(unpacked[0][-2], me * cw, cw, axis=1)
    upd['conv_w'] = adamw(g_cw_mine.reshape(-1, LANES), conv_w.reshape(-1, LANES), m_conv_w.reshape(-1, LANES),
                          v_conv_w.reshape(-1, LANES), False, "adamw_conv_w")

    order = ['norm_w', 'w_in', 'conv_w', 'conv_b', 'dt_bias', 'a_log', 'd_skip', 'ssm_norm_w', 'w_attn_branch',
             'w_ssm_branch', 'w_out', 'final_norm_w']
    like = dict(norm_w=norm_w, w_in=w_in, conv_w=conv_w, conv_b=conv_b, dt_bias=dt_bias, a_log=a_log, d_skip=d_skip,
                ssm_norm_w=ssm_norm_w, w_attn_branch=w_attn_branch, w_ssm_branch=w_ssm_branch, w_out=w_out,
                final_norm_w=final_norm_w)
    result = [loss, grad_x[None]]
    for t in range(4):
        result += [upd[n][t].reshape(like[n].shape) for n in order]
    return tuple(result)
```
